```python
import jax, jax.numpy as jnp
from jax import lax
import numpy as np

D_MODEL = 2048
BATCH = 8
SEQ = 2048
DEPTH = 2

N_MIXERS = 2
N_MLA = (DEPTH + 1) // 2
N_CONV = DEPTH // 2
N_HEADS = 16
Q_LORA = 512
KV_LORA = 512
D_NOPE = 128
D_ROPE = 64
D_V = 128
QK_DIM = D_NOPE + D_ROPE
ROPE_THETA = 10000.0
Q_BLOCK = 128
CONV_CH = D_MODEL
CONV_WIDTH = 31
D_FF = 5632
FFN_RESIDUAL_WEIGHT = 0.5
D_PLE = 256
EPS = 1e-6

kernel_name = "mla_conformer_macaron_hybrid"


def rms_norm(x, g):
    xf = x.astype(jnp.float32)
    y = xf * lax.rsqrt(jnp.mean(xf * xf, axis=-1, keepdims=True) + EPS)
    return (y * g.astype(jnp.float32)).astype(x.dtype)


def layer_norm(x, g, b):
    xf = x.astype(jnp.float32)
    mu = jnp.mean(xf, axis=-1, keepdims=True)
    xc = xf - mu
    y = xc * lax.rsqrt(jnp.mean(xc * xc, axis=-1, keepdims=True) + EPS)
    return (y * g.astype(jnp.float32) + b.astype(jnp.float32)).astype(x.dtype)


def swiglu_ffn(h, w_in, w_out):
    g, u = jnp.split(h @ w_in, 2, axis=-1)
    return (jax.nn.silu(g) * u) @ w_out


def rope_tables(positions):
    inv_freq = ROPE_THETA ** (-jnp.arange(0, D_ROPE, 2, dtype=jnp.float32) / D_ROPE)
    ang = positions.astype(jnp.float32)[..., None] * inv_freq
    return jnp.cos(ang)[:, :, None, :], jnp.sin(ang)[:, :, None, :]


def apply_rope_tail(x, cos, sin):
    x_nope, x_rope = x[..., :D_NOPE], x[..., D_NOPE:]
    xr = x_rope.astype(jnp.float32)
    x1, x2 = xr[..., : D_ROPE // 2], xr[..., D_ROPE // 2:]
    rot = jnp.concatenate([x1 * cos - x2 * sin, x2 * cos + x1 * sin], axis=-1)
    return jnp.concatenate([x_nope, rot.astype(x.dtype)], axis=-1)


def causal_blocked_attention(q, k, v):
    S = q.shape[1]
    scale = QK_DIM ** -0.5
    outs = []
    for blk in range(S // Q_BLOCK):
        s0 = blk * Q_BLOCK
        n_keys = s0 + Q_BLOCK
        q_b = q[:, s0:s0 + Q_BLOCK]
        k_b = k[:, :n_keys]
        v_b = v[:, :n_keys]
        scores = jnp.einsum('bqhe,bkhe->bhqk', q_b, k_b).astype(jnp.float32) * scale
        q_idx = s0 + jnp.arange(Q_BLOCK)[:, None]
        k_idx = jnp.arange(n_keys)[None, :]
        scores = jnp.where(k_idx <= q_idx, scores, -jnp.inf)
        probs = jax.nn.softmax(scores, axis=-1).astype(v.dtype)
        outs.append(jnp.einsum('bhqk,bkhd->bqhd', probs, v_b))
    return jnp.concatenate(outs, axis=1)


def mla_mixer(h, positions, w_in, q_lat_norm, kv_lat_norm, w_uq, w_ukv, q_gain, k_gain, w_o):
    B, S, _ = h.shape
    lat = h @ w_in
    c_q, c_kv, k_rope = jnp.split(lat, [Q_LORA, Q_LORA + KV_LORA], axis=-1)
    c_q = rms_norm(c_q, q_lat_norm)
    c_kv = rms_norm(c_kv, kv_lat_norm)
    q = (c_q @ w_uq).reshape(B, S, N_HEADS, QK_DIM)
    kv = (c_kv @ w_ukv).reshape(B, S, N_HEADS, D_NOPE + D_V)
    k_nope, v = jnp.split(kv, [D_NOPE], axis=-1)
    k_rope = jnp.broadcast_to(k_rope[:, :, None, :], (B, S, N_HEADS, D_ROPE))
    k = jnp.concatenate([k_nope, k_rope], axis=-1)
    q = rms_norm(q, q_gain)
    k = rms_norm(k, k_gain)
    cos, sin = rope_tables(positions)
    q = apply_rope_tail(q, cos, sin)
    k = apply_rope_tail(k, cos, sin)
    o = causal_blocked_attention(q, k, v)
    return o.reshape(B, S, N_HEADS * D_V) @ w_o


def conv_mixer(h, w_pw1, b_pw1, w_dw, b_dw, ln_g, ln_b, w_pw2):
    a, g = jnp.split(h @ w_pw1 + b_pw1, 2, axis=-1)
    u = a * jax.nn.sigmoid(g)
    u = lax.conv_general_dilated(
        u, w_dw[:, None, :], window_strides=(1,), padding=[(CONV_WIDTH - 1, 0)],
        dimension_numbers=('NWC', 'WIO', 'NWC'), feature_group_count=CONV_CH) + b_dw
    u = jax.nn.silu(layer_norm(u, ln_g, ln_b))
    return u @ w_pw2


def _fwd_setup_inputs(seed: int = 0) -> dict:
    key = jax.random.key(seed)
    ks = iter(jax.random.split(key, 40))

    def w(shape, fan_in):
        return jax.random.normal(next(ks), shape, jnp.float32) * (fan_in ** -0.5)

    def gain(shape):
        return 1.0 + 0.02 * jax.random.normal(next(ks), shape, jnp.float32)

    def bias(shape):
        return 0.01 * jax.random.normal(next(ks), shape, jnp.float32)

    x = jax.random.normal(next(ks), (BATCH, SEQ, D_MODEL), jnp.float32)
    p = jax.random.normal(next(ks), (DEPTH, BATCH, SEQ, D_PLE), jnp.float32)
    positions = jnp.broadcast_to(jnp.arange(SEQ, dtype=jnp.int32)[None, :], (BATCH, SEQ))
    return {
        "x": x, "p": p, "positions": positions,
        "ffn_a_norm": gain((DEPTH, D_MODEL)),
        "ffn_a_w_in": w((DEPTH, D_MODEL, 2 * D_FF), D_MODEL),
        "ffn_a_w_out": w((DEPTH, D_FF, D_MODEL), D_FF),
        "ffn_b_norm": gain((DEPTH, D_MODEL)),
        "ffn_b_w_in": w((DEPTH, D_MODEL, 2 * D_FF), D_MODEL),
        "ffn_b_w_out": w((DEPTH, D_FF, D_MODEL), D_FF),
        "mix_norm": gain((DEPTH, D_MODEL)),
        "mla_w_in": w((N_MLA, D_MODEL, Q_LORA + KV_LORA + D_ROPE), D_MODEL),
        "mla_q_lat_norm": gain((N_MLA, Q_LORA)),
        "mla_kv_lat_norm": gain((N_MLA, KV_LORA)),
        "mla_w_uq": w((N_MLA, Q_LORA, N_HEADS * QK_DIM), Q_LORA),
        "mla_w_ukv": w((N_MLA, KV_LORA, N_HEADS * (D_NOPE + D_V)), KV_LORA),
        "mla_q_gain": gain((N_MLA, QK_DIM)),
        "mla_k_gain": gain((N_MLA, QK_DIM)),
        "mla_w_o": w((N_MLA, N_HEADS * D_V, D_MODEL), N_HEADS * D_V),
        "conv_w_pw1": w((N_CONV, D_MODEL, 2 * CONV_CH), D_MODEL),
        "conv_b_pw1": bias((N_CONV, 2 * CONV_CH)),
        "conv_w_dw": w((N_CONV, CONV_WIDTH, CONV_CH), CONV_WIDTH),
        "conv_b_dw": bias((N_CONV, CONV_CH)),
        "conv_ln_g": gain((N_CONV, CONV_CH)),
        "conv_ln_b": bias((N_CONV, CONV_CH)),
        "conv_w_pw2": w((N_CONV, CONV_CH, D_MODEL), CONV_CH),
        "ple_w_proj": w((DEPTH, D_PLE, D_MODEL), D_PLE),
        "ple_norm": gain((DEPTH, D_MODEL)),
        "ple_gate_norm": gain((DEPTH, D_MODEL)),
        "ple_w_gate": w((DEPTH, D_MODEL, D_MODEL), D_MODEL),
    }


def _fwd_reference(x, p, positions,
              ffn_a_norm, ffn_a_w_in, ffn_a_w_out,
              ffn_b_norm, ffn_b_w_in, ffn_b_w_out,
              mix_norm,
              mla_w_in, mla_q_lat_norm, mla_kv_lat_norm, mla_w_uq, mla_w_ukv,
              mla_q_gain, mla_k_gain, mla_w_o,
              conv_w_pw1, conv_b_pw1, conv_w_dw, conv_b_dw, conv_ln_g, conv_ln_b, conv_w_pw2,
              ple_w_proj, ple_norm, ple_gate_norm, ple_w_gate):
    h = x
    for i in range(DEPTH):
        h = h + FFN_RESIDUAL_WEIGHT * swiglu_ffn(rms_norm(h, ffn_a_norm[i]), ffn_a_w_in[i], ffn_a_w_out[i])
        hn = rms_norm(h, mix_norm[i])
        j = i // N_MIXERS
        if i % N_MIXERS == 0:
            h = h + mla_mixer(hn, positions, mla_w_in[j], mla_q_lat_norm[j], mla_kv_lat_norm[j],
                              mla_w_uq[j], mla_w_ukv[j], mla_q_gain[j], mla_k_gain[j], mla_w_o[j])
        else:
            h = h + conv_mixer(hn, conv_w_pw1[j], conv_b_pw1[j], conv_w_dw[j], conv_b_dw[j],
                               conv_ln_g[j], conv_ln_b[j], conv_w_pw2[j])
        h = h + FFN_RESIDUAL_WEIGHT * swiglu_ffn(rms_norm(h, ffn_b_norm[i]), ffn_b_w_in[i], ffn_b_w_out[i])
        e = rms_norm(p[i] @ ple_w_proj[i], ple_norm[i])
        gate = jax.nn.sigmoid(rms_norm(h, ple_gate_norm[i]) @ ple_w_gate[i])
        h = h + e * gate
    return h


import jax as _jax
import jax.numpy as _jnp

TWIN_FORMAT = 'train_step'
FWD_PARAMS = ['x', 'p', 'positions', 'ffn_a_norm', 'ffn_a_w_in', 'ffn_a_w_out', 'ffn_b_norm', 'ffn_b_w_in', 'ffn_b_w_out', 'mix_norm', 'mla_w_in', 'mla_q_lat_norm', 'mla_kv_lat_norm', 'mla_w_uq', 'mla_w_ukv', 'mla_q_gain', 'mla_k_gain', 'mla_w_o', 'conv_w_pw1', 'conv_b_pw1', 'conv_w_dw', 'conv_b_dw', 'conv_ln_g', 'conv_ln_b', 'conv_w_pw2', 'ple_w_proj', 'ple_norm', 'ple_gate_norm', 'ple_w_gate']
TWIN_WEIGHTS = ['ffn_a_norm', 'ffn_a_w_in', 'ffn_a_w_out', 'ffn_b_norm', 'ffn_b_w_in', 'ffn_b_w_out', 'mix_norm', 'mla_w_in', 'mla_q_lat_norm', 'mla_kv_lat_norm', 'mla_w_uq', 'mla_w_ukv', 'mla_q_gain', 'mla_k_gain', 'mla_w_o', 'conv_w_pw1', 'conv_b_pw1', 'conv_w_dw', 'conv_b_dw', 'conv_ln_g', 'conv_ln_b', 'conv_w_pw2', 'ple_w_proj', 'ple_norm', 'ple_gate_norm', 'ple_w_gate']
TWIN_DIFF_INPUT = 'x'
TWIN_INPUTS = ['x', 'p', 'positions', 'ffn_a_norm', 'ffn_a_w_in', 'ffn_a_w_out', 'ffn_b_norm', 'ffn_b_w_in', 'ffn_b_w_out', 'mix_norm', 'mla_w_in', 'mla_q_lat_norm', 'mla_kv_lat_norm', 'mla_w_uq', 'mla_w_ukv', 'mla_q_gain', 'mla_k_gain', 'mla_w_o', 'conv_w_pw1', 'conv_b_pw1', 'conv_w_dw', 'conv_b_dw', 'conv_ln_g', 'conv_ln_b', 'conv_w_pw2', 'ple_w_proj', 'ple_norm', 'ple_gate_norm', 'ple_w_gate', 'loss_target', 'm_ffn_a_norm', 'm_ffn_a_w_in', 'm_ffn_a_w_out', 'm_ffn_b_norm', 'm_ffn_b_w_in', 'm_ffn_b_w_out', 'm_mix_norm', 'm_mla_w_in', 'm_mla_q_lat_norm', 'm_mla_kv_lat_norm', 'm_mla_w_uq', 'm_mla_w_ukv', 'm_mla_q_gain', 'm_mla_k_gain', 'm_mla_w_o', 'm_conv_w_pw1', 'm_conv_b_pw1', 'm_conv_w_dw', 'm_conv_b_dw', 'm_conv_ln_g', 'm_conv_ln_b', 'm_conv_w_pw2', 'm_ple_w_proj', 'm_ple_norm', 'm_ple_gate_norm', 'm_ple_w_gate', 'v_ffn_a_norm', 'v_ffn_a_w_in', 'v_ffn_a_w_out', 'v_ffn_b_norm', 'v_ffn_b_w_in', 'v_ffn_b_w_out', 'v_mix_norm', 'v_mla_w_in', 'v_mla_q_lat_norm', 'v_mla_kv_lat_norm', 'v_mla_w_uq', 'v_mla_w_ukv', 'v_mla_q_gain', 'v_mla_k_gain', 'v_mla_w_o', 'v_conv_w_pw1', 'v_conv_b_pw1', 'v_conv_w_dw', 'v_conv_b_dw', 'v_conv_ln_g', 'v_conv_ln_b', 'v_conv_w_pw2', 'v_ple_w_proj', 'v_ple_norm', 'v_ple_gate_norm', 'v_ple_w_gate']
TWIN_OUTPUTS = ['loss', 'grad_x', 'grad_ffn_a_norm', 'grad_ffn_a_w_in', 'grad_ffn_a_w_out', 'grad_ffn_b_norm', 'grad_ffn_b_w_in', 'grad_ffn_b_w_out', 'grad_mix_norm', 'grad_mla_w_in', 'grad_mla_q_lat_norm', 'grad_mla_kv_lat_norm', 'grad_mla_w_uq', 'grad_mla_w_ukv', 'grad_mla_q_gain', 'grad_mla_k_gain', 'grad_mla_w_o', 'grad_conv_w_pw1', 'grad_conv_b_pw1', 'grad_conv_w_dw', 'grad_conv_b_dw', 'grad_conv_ln_g', 'grad_conv_ln_b', 'grad_conv_w_pw2', 'grad_ple_w_proj', 'grad_ple_norm', 'grad_ple_gate_norm', 'grad_ple_w_gate', 'delta_ffn_a_norm', 'delta_ffn_a_w_in', 'delta_ffn_a_w_out', 'delta_ffn_b_norm', 'delta_ffn_b_w_in', 'delta_ffn_b_w_out', 'delta_mix_norm', 'delta_mla_w_in', 'delta_mla_q_lat_norm', 'delta_mla_kv_lat_norm', 'delta_mla_w_uq', 'delta_mla_w_ukv', 'delta_mla_q_gain', 'delta_mla_k_gain', 'delta_mla_w_o', 'delta_conv_w_pw1', 'delta_conv_b_pw1', 'delta_conv_w_dw', 'delta_conv_b_dw', 'delta_conv_ln_g', 'delta_conv_ln_b', 'delta_conv_w_pw2', 'delta_ple_w_proj', 'delta_ple_norm', 'delta_ple_gate_norm', 'delta_ple_w_gate', 'new_m_ffn_a_norm', 'new_m_ffn_a_w_in', 'new_m_ffn_a_w_out', 'new_m_ffn_b_norm', 'new_m_ffn_b_w_in', 'new_m_ffn_b_w_out', 'new_m_mix_norm', 'new_m_mla_w_in', 'new_m_mla_q_lat_norm', 'new_m_mla_kv_lat_norm', 'new_m_mla_w_uq', 'new_m_mla_w_ukv', 'new_m_mla_q_gain', 'new_m_mla_k_gain', 'new_m_mla_w_o', 'new_m_conv_w_pw1', 'new_m_conv_b_pw1', 'new_m_conv_w_dw', 'new_m_conv_b_dw', 'new_m_conv_ln_g', 'new_m_conv_ln_b', 'new_m_conv_w_pw2', 'new_m_ple_w_proj', 'new_m_ple_norm', 'new_m_ple_gate_norm', 'new_m_ple_w_gate', 'new_v_ffn_a_norm', 'new_v_ffn_a_w_in', 'new_v_ffn_a_w_out', 'new_v_ffn_b_norm', 'new_v_ffn_b_w_in', 'new_v_ffn_b_w_out', 'new_v_mix_norm', 'new_v_mla_w_in', 'new_v_mla_q_lat_norm', 'new_v_mla_kv_lat_norm', 'new_v_mla_w_uq', 'new_v_mla_w_ukv', 'new_v_mla_q_gain', 'new_v_mla_k_gain', 'new_v_mla_w_o', 'new_v_conv_w_pw1', 'new_v_conv_b_pw1', 'new_v_conv_w_dw', 'new_v_conv_b_dw', 'new_v_conv_ln_g', 'new_v_conv_ln_b', 'new_v_conv_w_pw2', 'new_v_ple_w_proj', 'new_v_ple_norm', 'new_v_ple_gate_norm', 'new_v_ple_w_gate']
TWIN_LEAF_KINDS = {'loss': 'loss', 'grad_x': 'grad_x', 'grad_ffn_a_norm': 'grad_w', 'grad_ffn_a_w_in': 'grad_w', 'grad_ffn_a_w_out': 'grad_w', 'grad_ffn_b_norm': 'grad_w', 'grad_ffn_b_w_in': 'grad_w', 'grad_ffn_b_w_out': 'grad_w', 'grad_mix_norm': 'grad_w', 'grad_mla_w_in': 'grad_w', 'grad_mla_q_lat_norm': 'grad_w', 'grad_mla_kv_lat_norm': 'grad_w', 'grad_mla_w_uq': 'grad_w', 'grad_mla_w_ukv': 'grad_w', 'grad_mla_q_gain': 'grad_w', 'grad_mla_k_gain': 'grad_w', 'grad_mla_w_o': 'grad_w', 'grad_conv_w_pw1': 'grad_w', 'grad_conv_b_pw1': 'grad_w', 'grad_conv_w_dw': 'grad_w', 'grad_conv_b_dw': 'grad_w', 'grad_conv_ln_g': 'grad_w', 'grad_conv_ln_b': 'grad_w', 'grad_conv_w_pw2': 'grad_w', 'grad_ple_w_proj': 'grad_w', 'grad_ple_norm': 'grad_w', 'grad_ple_gate_norm': 'grad_w', 'grad_ple_w_gate': 'grad_w', 'delta_ffn_a_norm': 'delta_w', 'delta_ffn_a_w_in': 'delta_w', 'delta_ffn_a_w_out': 'delta_w', 'delta_ffn_b_norm': 'delta_w', 'delta_ffn_b_w_in': 'delta_w', 'delta_ffn_b_w_out': 'delta_w', 'delta_mix_norm': 'delta_w', 'delta_mla_w_in': 'delta_w', 'delta_mla_q_lat_norm': 'delta_w', 'delta_mla_kv_lat_norm': 'delta_w', 'delta_mla_w_uq': 'delta_w', 'delta_mla_w_ukv': 'delta_w', 'delta_mla_q_gain': 'delta_w', 'delta_mla_k_gain': 'delta_w', 'delta_mla_w_o': 'delta_w', 'delta_conv_w_pw1': 'delta_w', 'delta_conv_b_pw1': 'delta_w', 'delta_conv_w_dw': 'delta_w', 'delta_conv_b_dw': 'delta_w', 'delta_conv_ln_g': 'delta_w', 'delta_conv_ln_b': 'delta_w', 'delta_conv_w_pw2': 'delta_w', 'delta_ple_w_proj': 'delta_w', 'delta_ple_norm': 'delta_w', 'delta_ple_gate_norm': 'delta_w', 'delta_ple_w_gate': 'delta_w', 'new_m_ffn_a_norm': 'new_m', 'new_m_ffn_a_w_in': 'new_m', 'new_m_ffn_a_w_out': 'new_m', 'new_m_ffn_b_norm': 'new_m', 'new_m_ffn_b_w_in': 'new_m', 'new_m_ffn_b_w_out': 'new_m', 'new_m_mix_norm': 'new_m', 'new_m_mla_w_in': 'new_m', 'new_m_mla_q_lat_norm': 'new_m', 'new_m_mla_kv_lat_norm': 'new_m', 'new_m_mla_w_uq': 'new_m', 'new_m_mla_w_ukv': 'new_m', 'new_m_mla_q_gain': 'new_m', 'new_m_mla_k_gain': 'new_m', 'new_m_mla_w_o': 'new_m', 'new_m_conv_w_pw1': 'new_m', 'new_m_conv_b_pw1': 'new_m', 'new_m_conv_w_dw': 'new_m', 'new_m_conv_b_dw': 'new_m', 'new_m_conv_ln_g': 'new_m', 'new_m_conv_ln_b': 'new_m', 'new_m_conv_w_pw2': 'new_m', 'new_m_ple_w_proj': 'new_m', 'new_m_ple_norm': 'new_m', 'new_m_ple_gate_norm': 'new_m', 'new_m_ple_w_gate': 'new_m', 'new_v_ffn_a_norm': 'new_v', 'new_v_ffn_a_w_in': 'new_v', 'new_v_ffn_a_w_out': 'new_v', 'new_v_ffn_b_norm': 'new_v', 'new_v_ffn_b_w_in': 'new_v', 'new_v_ffn_b_w_out': 'new_v', 'new_v_mix_norm': 'new_v', 'new_v_mla_w_in': 'new_v', 'new_v_mla_q_lat_norm': 'new_v', 'new_v_mla_kv_lat_norm': 'new_v', 'new_v_mla_w_uq': 'new_v', 'new_v_mla_w_ukv': 'new_v', 'new_v_mla_q_gain': 'new_v', 'new_v_mla_k_gain': 'new_v', 'new_v_mla_w_o': 'new_v', 'new_v_conv_w_pw1': 'new_v', 'new_v_conv_b_pw1': 'new_v', 'new_v_conv_w_dw': 'new_v', 'new_v_conv_b_dw': 'new_v', 'new_v_conv_ln_g': 'new_v', 'new_v_conv_ln_b': 'new_v', 'new_v_conv_w_pw2': 'new_v', 'new_v_ple_w_proj': 'new_v', 'new_v_ple_norm': 'new_v', 'new_v_ple_gate_norm': 'new_v', 'new_v_ple_w_gate': 'new_v'}


def _forward(args):
    return _fwd_reference(*[args[k] for k in FWD_PARAMS])


def _output_shape():
    out = _jax.eval_shape(lambda: _forward(_fwd_setup_inputs(0)))
    return out.shape, out.dtype

N_MICROBATCH = 1
ADAM_LR = 0.001
ADAM_B1 = 0.9
ADAM_B2 = 0.999
ADAM_EPS = 1e-08
ADAM_WD = 0.01
ADAM_STEP = 10
PER_EXAMPLE_BATCH_AXIS = {'x': 0, 'p': 1, 'positions': 0, 'loss_target': 0}
SHARED_INPUTS = []
_WEIGHT_DTYPES = {'ffn_a_norm': _jnp.float32, 'ffn_a_w_in': _jnp.float32, 'ffn_a_w_out': _jnp.float32, 'ffn_b_norm': _jnp.float32, 'ffn_b_w_in': _jnp.float32, 'ffn_b_w_out': _jnp.float32, 'mix_norm': _jnp.float32, 'mla_w_in': _jnp.float32, 'mla_q_lat_norm': _jnp.float32, 'mla_kv_lat_norm': _jnp.float32, 'mla_w_uq': _jnp.float32, 'mla_w_ukv': _jnp.float32, 'mla_q_gain': _jnp.float32, 'mla_k_gain': _jnp.float32, 'mla_w_o': _jnp.float32, 'conv_w_pw1': _jnp.float32, 'conv_b_pw1': _jnp.float32, 'conv_w_dw': _jnp.float32, 'conv_b_dw': _jnp.float32, 'conv_ln_g': _jnp.float32, 'conv_ln_b': _jnp.float32, 'conv_w_pw2': _jnp.float32, 'ple_w_proj': _jnp.float32, 'ple_norm': _jnp.float32, 'ple_gate_norm': _jnp.float32, 'ple_w_gate': _jnp.float32}
MOMENT_SCALE = {'ffn_a_norm': 1.534345e+00, 'ffn_a_w_in': 2.979402e-02, 'ffn_a_w_out': 4.857252e-02, 'ffn_b_norm': 1.539108e+00, 'ffn_b_w_in': 3.018370e-02, 'ffn_b_w_out': 4.732438e-02, 'mix_norm': 7.943688e-02, 'mla_w_in': 9.725160e-02, 'mla_q_lat_norm': 6.616151e-02, 'mla_kv_lat_norm': 4.026121e-01, 'mla_w_uq': 2.724683e-02, 'mla_w_ukv': 4.033634e-02, 'mla_q_gain': 5.746448e-01, 'mla_k_gain': 5.754081e-01, 'mla_w_o': 4.804580e-02, 'conv_w_pw1': 6.139351e-02, 'conv_b_pw1': 6.388013e-01, 'conv_w_dw': 1.163988e-01, 'conv_b_dw': 1.674899e+00, 'conv_ln_g': 3.531458e+00, 'conv_ln_b': 2.402889e+00, 'conv_w_pw2': 3.476998e-01, 'ple_w_proj': 7.179636e-02, 'ple_norm': 2.351644e+00, 'ple_gate_norm': 2.534619e-01, 'ple_w_gate': 7.631450e-02}


def _to_microbatches(a, axis):
    t = _jnp.moveaxis(a, axis, 0)
    t = t.reshape((N_MICROBATCH, t.shape[0] // N_MICROBATCH) + t.shape[1:])
    return _jnp.moveaxis(t, 1, axis + 1)


def setup_inputs(seed: int = 0) -> dict:
    inp = _fwd_setup_inputs(seed)
    key = _jax.random.fold_in(_jax.random.key(seed), 7919)
    shape, _ = _output_shape()
    out = dict(inp)
    out["loss_target"] = _jax.random.normal(_jax.random.fold_in(key, 0), shape, _jnp.float32)
    for i, name in enumerate(TWIN_WEIGHTS):
        w = inp[name].astype(_jnp.float32)
        if MOMENT_SCALE is None:
            s = _jnp.sqrt(_jnp.mean(_jnp.square(w)) + 1e-30)
        else:
            s = MOMENT_SCALE[name]
        km, kv = _jax.random.split(_jax.random.fold_in(key, i + 1))
        out[name] = w
        out["m_" + name] = s * _jax.random.normal(km, w.shape, _jnp.float32)
        out["v_" + name] = (s * s) * _jax.random.uniform(kv, w.shape, _jnp.float32, 0.5, 1.5)
    if N_MICROBATCH > 1:
        for name, axis in PER_EXAMPLE_BATCH_AXIS.items():
            out[name] = _to_microbatches(out[name], axis)
    return {'x': out['x'], 'p': out['p'], 'positions': out['positions'], 'ffn_a_norm': out['ffn_a_norm'], 'ffn_a_w_in': out['ffn_a_w_in'], 'ffn_a_w_out': out['ffn_a_w_out'], 'ffn_b_norm': out['ffn_b_norm'], 'ffn_b_w_in': out['ffn_b_w_in'], 'ffn_b_w_out': out['ffn_b_w_out'], 'mix_norm': out['mix_norm'], 'mla_w_in': out['mla_w_in'], 'mla_q_lat_norm': out['mla_q_lat_norm'], 'mla_kv_lat_norm': out['mla_kv_lat_norm'], 'mla_w_uq': out['mla_w_uq'], 'mla_w_ukv': out['mla_w_ukv'], 'mla_q_gain': out['mla_q_gain'], 'mla_k_gain': out['mla_k_gain'], 'mla_w_o': out['mla_w_o'], 'conv_w_pw1': out['conv_w_pw1'], 'conv_b_pw1': out['conv_b_pw1'], 'conv_w_dw': out['conv_w_dw'], 'conv_b_dw': out['conv_b_dw'], 'conv_ln_g': out['conv_ln_g'], 'conv_ln_b': out['conv_ln_b'], 'conv_w_pw2': out['conv_w_pw2'], 'ple_w_proj': out['ple_w_proj'], 'ple_norm': out['ple_norm'], 'ple_gate_norm': out['ple_gate_norm'], 'ple_w_gate': out['ple_w_gate'], 'loss_target': out['loss_target'], 'm_ffn_a_norm': out['m_ffn_a_norm'], 'm_ffn_a_w_in': out['m_ffn_a_w_in'], 'm_ffn_a_w_out': out['m_ffn_a_w_out'], 'm_ffn_b_norm': out['m_ffn_b_norm'], 'm_ffn_b_w_in': out['m_ffn_b_w_in'], 'm_ffn_b_w_out': out['m_ffn_b_w_out'], 'm_mix_norm': out['m_mix_norm'], 'm_mla_w_in': out['m_mla_w_in'], 'm_mla_q_lat_norm': out['m_mla_q_lat_norm'], 'm_mla_kv_lat_norm': out['m_mla_kv_lat_norm'], 'm_mla_w_uq': out['m_mla_w_uq'], 'm_mla_w_ukv': out['m_mla_w_ukv'], 'm_mla_q_gain': out['m_mla_q_gain'], 'm_mla_k_gain': out['m_mla_k_gain'], 'm_mla_w_o': out['m_mla_w_o'], 'm_conv_w_pw1': out['m_conv_w_pw1'], 'm_conv_b_pw1': out['m_conv_b_pw1'], 'm_conv_w_dw': out['m_conv_w_dw'], 'm_conv_b_dw': out['m_conv_b_dw'], 'm_conv_ln_g': out['m_conv_ln_g'], 'm_conv_ln_b': out['m_conv_ln_b'], 'm_conv_w_pw2': out['m_conv_w_pw2'], 'm_ple_w_proj': out['m_ple_w_proj'], 'm_ple_norm': out['m_ple_norm'], 'm_ple_gate_norm': out['m_ple_gate_norm'], 'm_ple_w_gate': out['m_ple_w_gate'], 'v_ffn_a_norm': out['v_ffn_a_norm'], 'v_ffn_a_w_in': out['v_ffn_a_w_in'], 'v_ffn_a_w_out': out['v_ffn_a_w_out'], 'v_ffn_b_norm': out['v_ffn_b_norm'], 'v_ffn_b_w_in': out['v_ffn_b_w_in'], 'v_ffn_b_w_out': out['v_ffn_b_w_out'], 'v_mix_norm': out['v_mix_norm'], 'v_mla_w_in': out['v_mla_w_in'], 'v_mla_q_lat_norm': out['v_mla_q_lat_norm'], 'v_mla_kv_lat_norm': out['v_mla_kv_lat_norm'], 'v_mla_w_uq': out['v_mla_w_uq'], 'v_mla_w_ukv': out['v_mla_w_ukv'], 'v_mla_q_gain': out['v_mla_q_gain'], 'v_mla_k_gain': out['v_mla_k_gain'], 'v_mla_w_o': out['v_mla_w_o'], 'v_conv_w_pw1': out['v_conv_w_pw1'], 'v_conv_b_pw1': out['v_conv_b_pw1'], 'v_conv_w_dw': out['v_conv_w_dw'], 'v_conv_b_dw': out['v_conv_b_dw'], 'v_conv_ln_g': out['v_conv_ln_g'], 'v_conv_ln_b': out['v_conv_ln_b'], 'v_conv_w_pw2': out['v_conv_w_pw2'], 'v_ple_w_proj': out['v_ple_w_proj'], 'v_ple_norm': out['v_ple_norm'], 'v_ple_gate_norm': out['v_ple_gate_norm'], 'v_ple_w_gate': out['v_ple_w_gate']}


def _loss(weights, diff, rest, loss_target):
    with _jax.named_scope("forward"):
        args = {**rest, TWIN_DIFF_INPUT: diff, **{k: w.astype(_WEIGHT_DTYPES[k]) for k, w in weights.items()}}
        y = _forward(args)
    with _jax.named_scope("loss_head"):
        err = _jnp.square(y.astype(_jnp.float32) - loss_target)
        return 0.5 * _jnp.sum(_jnp.mean(err, axis=-1)) if err.ndim else 0.5 * err


def _adamw(w, g, m, v):
    m = ADAM_B1 * m + (1.0 - ADAM_B1) * g
    v = ADAM_B2 * v + (1.0 - ADAM_B2) * _jnp.square(g)
    m_hat = m / (1.0 - ADAM_B1 ** ADAM_STEP)
    v_hat = v / (1.0 - ADAM_B2 ** ADAM_STEP)
    delta = -ADAM_LR * (m_hat / (_jnp.sqrt(v_hat) + ADAM_EPS) + ADAM_WD * w)
    return delta, m, v


def reference(x, p, positions, ffn_a_norm, ffn_a_w_in, ffn_a_w_out, ffn_b_norm, ffn_b_w_in, ffn_b_w_out, mix_norm, mla_w_in, mla_q_lat_norm, mla_kv_lat_norm, mla_w_uq, mla_w_ukv, mla_q_gain, mla_k_gain, mla_w_o, conv_w_pw1, conv_b_pw1, conv_w_dw, conv_b_dw, conv_ln_g, conv_ln_b, conv_w_pw2, ple_w_proj, ple_norm, ple_gate_norm, ple_w_gate, loss_target, m_ffn_a_norm, m_ffn_a_w_in, m_ffn_a_w_out, m_ffn_b_norm, m_ffn_b_w_in, m_ffn_b_w_out, m_mix_norm, m_mla_w_in, m_mla_q_lat_norm, m_mla_kv_lat_norm, m_mla_w_uq, m_mla_w_ukv, m_mla_q_gain, m_mla_k_gain, m_mla_w_o, m_conv_w_pw1, m_conv_b_pw1, m_conv_w_dw, m_conv_b_dw, m_conv_ln_g, m_conv_ln_b, m_conv_w_pw2, m_ple_w_proj, m_ple_norm, m_ple_gate_norm, m_ple_w_gate, v_ffn_a_norm, v_ffn_a_w_in, v_ffn_a_w_out, v_ffn_b_norm, v_ffn_b_w_in, v_ffn_b_w_out, v_mix_norm, v_mla_w_in, v_mla_q_lat_norm, v_mla_kv_lat_norm, v_mla_w_uq, v_mla_w_ukv, v_mla_q_gain, v_mla_k_gain, v_mla_w_o, v_conv_w_pw1, v_conv_b_pw1, v_conv_w_dw, v_conv_b_dw, v_conv_ln_g, v_conv_ln_b, v_conv_w_pw2, v_ple_w_proj, v_ple_norm, v_ple_gate_norm, v_ple_w_gate):
    given = dict(x=x, p=p, positions=positions, ffn_a_norm=ffn_a_norm, ffn_a_w_in=ffn_a_w_in, ffn_a_w_out=ffn_a_w_out, ffn_b_norm=ffn_b_norm, ffn_b_w_in=ffn_b_w_in, ffn_b_w_out=ffn_b_w_out, mix_norm=mix_norm, mla_w_in=mla_w_in, mla_q_lat_norm=mla_q_lat_norm, mla_kv_lat_norm=mla_kv_lat_norm, mla_w_uq=mla_w_uq, mla_w_ukv=mla_w_ukv, mla_q_gain=mla_q_gain, mla_k_gain=mla_k_gain, mla_w_o=mla_w_o, conv_w_pw1=conv_w_pw1, conv_b_pw1=conv_b_pw1, conv_w_dw=conv_w_dw, conv_b_dw=conv_b_dw, conv_ln_g=conv_ln_g, conv_ln_b=conv_ln_b, conv_w_pw2=conv_w_pw2, ple_w_proj=ple_w_proj, ple_norm=ple_norm, ple_gate_norm=ple_gate_norm, ple_w_gate=ple_w_gate, loss_target=loss_target, m_ffn_a_norm=m_ffn_a_norm, m_ffn_a_w_in=m_ffn_a_w_in, m_ffn_a_w_out=m_ffn_a_w_out, m_ffn_b_norm=m_ffn_b_norm, m_ffn_b_w_in=m_ffn_b_w_in, m_ffn_b_w_out=m_ffn_b_w_out, m_mix_norm=m_mix_norm, m_mla_w_in=m_mla_w_in, m_mla_q_lat_norm=m_mla_q_lat_norm, m_mla_kv_lat_norm=m_mla_kv_lat_norm, m_mla_w_uq=m_mla_w_uq, m_mla_w_ukv=m_mla_w_ukv, m_mla_q_gain=m_mla_q_gain, m_mla_k_gain=m_mla_k_gain, m_mla_w_o=m_mla_w_o, m_conv_w_pw1=m_conv_w_pw1, m_conv_b_pw1=m_conv_b_pw1, m_conv_w_dw=m_conv_w_dw, m_conv_b_dw=m_conv_b_dw, m_conv_ln_g=m_conv_ln_g, m_conv_ln_b=m_conv_ln_b, m_conv_w_pw2=m_conv_w_pw2, m_ple_w_proj=m_ple_w_proj, m_ple_norm=m_ple_norm, m_ple_gate_norm=m_ple_gate_norm, m_ple_w_gate=m_ple_w_gate, v_ffn_a_norm=v_ffn_a_norm, v_ffn_a_w_in=v_ffn_a_w_in, v_ffn_a_w_out=v_ffn_a_w_out, v_ffn_b_norm=v_ffn_b_norm, v_ffn_b_w_in=v_ffn_b_w_in, v_ffn_b_w_out=v_ffn_b_w_out, v_mix_norm=v_mix_norm, v_mla_w_in=v_mla_w_in, v_mla_q_lat_norm=v_mla_q_lat_norm, v_mla_kv_lat_norm=v_mla_kv_lat_norm, v_mla_w_uq=v_mla_w_uq, v_mla_w_ukv=v_mla_w_ukv, v_mla_q_gain=v_mla_q_gain, v_mla_k_gain=v_mla_k_gain, v_mla_w_o=v_mla_w_o, v_conv_w_pw1=v_conv_w_pw1, v_conv_b_pw1=v_conv_b_pw1, v_conv_w_dw=v_conv_w_dw, v_conv_b_dw=v_conv_b_dw, v_conv_ln_g=v_conv_ln_g, v_conv_ln_b=v_conv_ln_b, v_conv_w_pw2=v_conv_w_pw2, v_ple_w_proj=v_ple_w_proj, v_ple_norm=v_ple_norm, v_ple_gate_norm=v_ple_gate_norm, v_ple_w_gate=v_ple_w_gate)
    weights = {n: given[n] for n in TWIN_WEIGHTS}
    shared = {n: given[n] for n in SHARED_INPUTS}
    per_example = {n: given[n] for n in ['x', 'p', 'positions']}
    grad_fn = _jax.value_and_grad(_loss, argnums=(0, 1))

    def one_microbatch(ex, loss_target):
        ex = dict(ex)
        diff = ex.pop(TWIN_DIFF_INPUT)
        return grad_fn(weights, diff, {**shared, **ex}, loss_target)

    if N_MICROBATCH == 1:
        loss, (grad_w, grad_x) = one_microbatch(per_example, given["loss_target"])
    else:
        def body(carry, xs):
            loss_sum, grad_sum = carry
            l_k, (gw_k, gx_k) = one_microbatch(xs[0], xs[1])
            with _jax.named_scope("update"):
                return (loss_sum + l_k, _jax.tree.map(_jnp.add, grad_sum, gw_k)), gx_k

        init = (_jnp.zeros((), _jnp.float32), _jax.tree.map(_jnp.zeros_like, weights))
        (loss, grad_w), grad_x = _jax.lax.scan(body, init, (per_example, given["loss_target"]))
    with _jax.named_scope("update"):
        delta_w, new_m, new_v = {}, {}, {}
        for n in TWIN_WEIGHTS:
            delta_w[n], new_m[n], new_v[n] = _adamw(weights[n], grad_w[n], given["m_" + n], given["v_" + n])
    return (loss, grad_x, *[grad_w[n] for n in TWIN_WEIGHTS], *[delta_w[n] for n in TWIN_WEIGHTS],
            *[new_m[n] for n in TWIN_WEIGHTS], *[new_v[n] for n in TWIN_WEIGHTS])
```

```python
import functools
import math

import jax
import jax.numpy as jnp
from jax import lax
from jax.experimental import pallas as pl
from jax.experimental.pallas import tpu as pltpu

F32, BF16 = jnp.float32, jnp.bfloat16
MESH = pl.DeviceIdType.MESH

N_CHIPS = 4
EPS = 1e-6
D_NOPE, D_ROPE, D_V = 128, 64, 128
QK_DIM = D_NOPE + D_ROPE
HEAD_PAD = 256
ROPE_THETA = 10000.0
CONV_WIDTH = 31
CONV_TAPS_PAD = 32
FFN_RESIDUAL_WEIGHT = 0.5
ADAM_LR, ADAM_B1, ADAM_B2, ADAM_EPS, ADAM_WD, ADAM_STEP = 0.001, 0.9, 0.999, 1e-08, 0.01, 10
VMEM_LIMIT_BYTES = 56 * 1024 * 1024
NEG_BIG = -1e30

NN_DIMS = (((1,), (0,)), ((), ()))
NT_DIMS = (((1,), (1,)), ((), ()))
TN_DIMS = (((0,), (0,)), ((), ()))


def _cparams(semantics=None):
    kw = dict(vmem_limit_bytes=VMEM_LIMIT_BYTES)
    if semantics is not None:
        kw["dimension_semantics"] = semantics
    return pltpu.CompilerParams(**kw)


def _tile(n, pref, mult=128):
    if n <= pref:
        return n
    t = (pref // mult) * mult
    while t >= mult:
        if n % t == 0:
            return t
        t -= mult
    return n


def _rowwise(name, fn, rows, vecs, outs, accs=(), tm=256, rc=16):
    T = rows[0].shape[0]
    tm = min(tm, T)
    rc = min(rc, tm)
    nr, nv, no, na = len(rows), len(vecs), len(outs), len(accs)
    steps = tm // rc

    def body(*refs):
        row_refs = refs[:nr]
        vec_refs = refs[nr:nr + nv]
        out_refs = refs[nr + nv:nr + nv + no]
        acc_refs = refs[nr + nv + no:]
        if na:
            @pl.when(pl.program_id(0) == 0)
            def _():
                for a in acc_refs:
                    a[...] = jnp.zeros_like(a)

        def step(r, carry):
            sl = pl.ds(pl.multiple_of(r * rc, rc), rc)
            res = fn(*[x[sl, :] for x in row_refs], *[v[...] for v in vec_refs])
            for o, val in zip(out_refs, res[:no]):
                o[sl, :] = val.astype(o.dtype)
            return tuple(c + val for c, val in zip(carry, res[no:]))

        init = tuple(jnp.zeros(s, F32) for s in accs)
        tot = lax.fori_loop(0, steps, step, init)
        for a, val in zip(acc_refs, tot):
            a[...] += val

    in_specs = [pl.BlockSpec((tm, x.shape[1]), lambda i: (i, 0)) for x in rows]
    in_specs += [pl.BlockSpec(v.shape, lambda i: (0, 0)) for v in vecs]
    out_specs = [pl.BlockSpec((tm, d), lambda i: (i, 0)) for d, _ in outs]
    out_specs += [pl.BlockSpec(s, lambda i: (0, 0)) for s in accs]
    out_shape = [jax.ShapeDtypeStruct((T, d), dt) for d, dt in outs]
    out_shape += [jax.ShapeDtypeStruct(s, F32) for s in accs]
    return pl.pallas_call(
        body, name=name, grid=(T // tm,), in_specs=in_specs, out_specs=out_specs, out_shape=out_shape,
        compiler_params=_cparams(("arbitrary",)),
    )(*rows, *vecs)


def _colsum(v):
    return jnp.sum(v, axis=0, keepdims=True)


def _rstd(x):
    return lax.rsqrt(jnp.mean(x * x, axis=-1, keepdims=True) + EPS)


def _rms_bwd(x, dy, g):
    r = _rstd(x)
    xh = x * r
    dyg = dy * g
    dx = r * (dyg - xh * jnp.mean(dyg * xh, axis=-1, keepdims=True))
    return dx, dy * xh


def _sigmoid(x):
    return 1.0 / (1.0 + jnp.exp(-x))


def rms_fwd(name, h, g):
    def fn(x, gv):
        return ((x * _rstd(x)) * gv,)
    return _rowwise(name, fn, [h], [g], [(h.shape[1], BF16)])[0]


def rms_bwd_res(name, h, d_y, g, d_res):
    D = h.shape[1]

    def fn(x, dy, dr, gv):
        dx, dgr = _rms_bwd(x, dy, gv)
        dh = dr + dx
        return dh, dh, _colsum(dgr)
    return _rowwise(name, fn, [h, d_y, d_res], [g], [(D, F32), (D, BF16)], [(1, D)], tm=128)


def swiglu_fwd(name, gu):
    Fh = gu.shape[1] // 2

    def fn(v):
        g = v[:, :Fh].astype(F32)
        u = v[:, Fh:].astype(F32)
        return (g * _sigmoid(g) * u,)
    return _rowwise(name, fn, [gu], [], [(Fh, BF16)])[0]


def swiglu_bwd(name, gu, d_act):
    Fh = gu.shape[1] // 2

    def fn(v, da):
        g = v[:, :Fh].astype(F32)
        u = v[:, Fh:].astype(F32)
        da = da.astype(F32)
        s = _sigmoid(g)
        d_u = da * g * s
        d_g = da * u * s * (1.0 + g * (1.0 - s))
        return (jnp.concatenate([d_g, d_u], axis=-1),)
    return _rowwise(name, fn, [gu, d_act], [], [(2 * Fh, BF16)])[0]


def loss_head(name, y, target):
    D = y.shape[1]

    def fn(yv, tv):
        e = yv - tv
        tot = jnp.sum(_colsum(e * e), axis=1, keepdims=True) * (0.5 / D)
        return e * (1.0 / D), jnp.broadcast_to(tot, (1, 128))
    return _rowwise(name, fn, [y, target], [], [(D, F32)], [(1, 128)])


def ple_fwd(name, h, pe, z, g_e):
    def fn(hv, pv, zv, gv):
        return (hv + (pv * _rstd(pv)) * gv * _sigmoid(zv),)
    return _rowwise(name, fn, [h, pe, z], [g_e], [(h.shape[1], F32)], tm=128)[0]


def ple_bwd(name, d_h, pe, z, g_e):
    D = d_h.shape[1]

    def fn(dh, pv, zv, gv):
        gate = _sigmoid(zv)
        e = (pv * _rstd(pv)) * gv
        d_z = dh * e * gate * (1.0 - gate)
        d_pe, dgr = _rms_bwd(pv, dh * gate, gv)
        return d_z, d_pe, _colsum(dgr)
    return _rowwise(name, fn, [d_h, pe, z], [g_e], [(D, BF16), (D, BF16)], [(1, D)], tm=128)


def lat_norm_fwd(name, lat, g_q, g_kv):
    QL, KL = g_q.shape[1], g_kv.shape[1]

    def fn(v, gq, gk):
        a = v[:, :QL]
        b = v[:, QL:QL + KL]
        return (a * _rstd(a)) * gq, (b * _rstd(b)) * gk
    return _rowwise(name, fn, [lat], [g_q, g_kv], [(QL, BF16), (KL, BF16)])


def lat_norm_bwd(name, lat, d_cq, d_ckv, d_krope, g_q, g_kv):
    QL, KL = g_q.shape[1], g_kv.shape[1]

    def fn(v, dq, dk, dr, gq, gk):
        da, ga = _rms_bwd(v[:, :QL], dq, gq)
        db, gb = _rms_bwd(v[:, QL:QL + KL], dk, gk)
        return jnp.concatenate([da, db, dr], axis=-1), _colsum(ga), _colsum(gb)
    return _rowwise(name, fn, [lat, d_cq, d_ckv, d_krope], [g_q, g_kv],
                    [(lat.shape[1], BF16)], [(1, QL), (1, KL)])


def glu_fwd(name, ag):
    C = ag.shape[1] // 2

    def fn(v):
        return (v[:, :C] * _sigmoid(v[:, C:]),)
    return _rowwise(name, fn, [ag], [], [(C, F32)], tm=128)[0]


def glu_bwd(name, ag, d_u):
    C = ag.shape[1] // 2

    def fn(v, du):
        a = v[:, :C]
        s = _sigmoid(v[:, C:])
        d = jnp.concatenate([du * s, du * a * s * (1.0 - s)], axis=-1)
        return d, _colsum(d)
    return _rowwise(name, fn, [ag, d_u], [], [(2 * C, BF16)], [(1, 2 * C)], tm=128)


def ln_silu_fwd(name, yc, g, b):
    def fn(v, gv, bv):
        xc = v - jnp.mean(v, axis=-1, keepdims=True)
        ln = xc * lax.rsqrt(jnp.mean(xc * xc, axis=-1, keepdims=True) + EPS) * gv + bv
        return (ln * _sigmoid(ln),)
    return _rowwise(name, fn, [yc], [g, b], [(yc.shape[1], BF16)], tm=128)[0]


def ln_silu_bwd(name, yc, d_out, g, b):
    C = yc.shape[1]

    def fn(v, do, gv, bv):
        xc = v - jnp.mean(v, axis=-1, keepdims=True)
        r = lax.rsqrt(jnp.mean(xc * xc, axis=-1, keepdims=True) + EPS)
        xh = xc * r
        ln = xh * gv + bv
        s = _sigmoid(ln)
        d_ln = do * s * (1.0 + ln * (1.0 - s))
        dxh = d_ln * gv
        dy = r * (dxh - jnp.mean(dxh, axis=-1, keepdims=True) - xh * jnp.mean(dxh * xh, axis=-1, keepdims=True))
        return dy, _colsum(d_ln * xh), _colsum(d_ln), _colsum(dy)
    return _rowwise(name, fn, [yc, d_out], [g, b], [(C, F32)], [(1, C), (1, C), (1, C)], tm=128)


def add_halves(name, core, own, got):
    S, Rh, C = got.shape
    tr = _tile(Rh, 256, 16)
    nrb = Rh // tr

    def body(c_ref, a_ref, b_ref, o_ref):
        o_ref[...] = (a_ref[...].astype(F32) + b_ref[...].astype(F32)).astype(o_ref.dtype)

    gs = pltpu.PrefetchScalarGridSpec(
        num_scalar_prefetch=1, grid=(S, nrb),
        in_specs=[pl.BlockSpec((None, tr, C), lambda s, r, c: (s, c[0] * nrb + r, 0)),
                  pl.BlockSpec((None, tr, C), lambda s, r, c: (s, r, 0))],
        out_specs=pl.BlockSpec((None, tr, C), lambda s, r, c: (s, r, 0)))
    return pl.pallas_call(
        body, name=name, grid_spec=gs, out_shape=jax.ShapeDtypeStruct((S, Rh, C), BF16),
        compiler_params=_cparams(("arbitrary", "arbitrary")))(core, own, got)


def sum_chips(name, parts, core, rows):
    S, Rh, C = parts.shape
    tr = _tile(Rh, 256, 16)
    nrb = Rh // tr

    def body(c_ref, p_ref, o_ref):
        acc = p_ref[0].astype(F32)
        for j in range(1, S):
            acc = acc + p_ref[j].astype(F32)
        o_ref[...] = acc

    gs = pltpu.PrefetchScalarGridSpec(
        num_scalar_prefetch=1, grid=(nrb,),
        in_specs=[pl.BlockSpec((S, tr, C), lambda r, c: (0, r, 0))],
        out_specs=pl.BlockSpec((tr, C), lambda r, c: (c[0] * nrb + r, 0)))
    return pl.pallas_call(
        body, name=name, grid_spec=gs, out_shape=jax.ShapeDtypeStruct((rows, C), F32),
        compiler_params=_cparams(("arbitrary",)))(core, parts)


def cast_into_slot(name, w, chip):
    R, C = w.shape
    tr = _tile(R, 256, 16)

    def body(s_ref, w_ref, o_ref):
        o_ref[...] = w_ref[...].astype(BF16)

    gs = pltpu.PrefetchScalarGridSpec(
        num_scalar_prefetch=1, grid=(R // tr,),
        in_specs=[pl.BlockSpec((tr, C), lambda r, s: (r, 0))],
        out_specs=pl.BlockSpec((None, tr, C), lambda r, s: (s[0], r, 0)))
    return pl.pallas_call(
        body, name=name, grid_spec=gs, out_shape=jax.ShapeDtypeStruct((N_CHIPS, R, C), BF16),
        compiler_params=_cparams(("arbitrary",)))(chip, w)


def adamw(name, w, g, m, v):
    R, C = w.shape
    tr = _tile(R, 256, 8)
    c1 = 1.0 / (1.0 - ADAM_B1 ** ADAM_STEP)
    c2 = 1.0 / (1.0 - ADAM_B2 ** ADAM_STEP)

    def body(w_ref, g_ref, m_ref, v_ref, d_ref, nm_ref, nv_ref):
        gv = g_ref[...]
        nm = ADAM_B1 * m_ref[...] + (1.0 - ADAM_B1) * gv
        nv = ADAM_B2 * v_ref[...] + (1.0 - ADAM_B2) * (gv * gv)
        d_ref[...] = -ADAM_LR * ((nm * c1) / (jnp.sqrt(nv * c2) + ADAM_EPS) + ADAM_WD * w_ref[...])
        nm_ref[...] = nm
        nv_ref[...] = nv

    spec = pl.BlockSpec((tr, C), lambda r: (r, 0))
    return pl.pallas_call(
        body, name=name, grid=(R // tr,), in_specs=[spec] * 4, out_specs=[spec] * 3,
        out_shape=[jax.ShapeDtypeStruct((R, C), F32)] * 3, compiler_params=_cparams(("arbitrary",)))(w, g, m, v)


def _matmul(name, a, b, *, grid, a_blk, a_map, b_blk, b_map, o_shape, o_dtype, o_blk, o_map, dims,
            scale=None, res=None, bias=None, bias_blk=None, bias_map=None, alias_into=None):
    nk = grid[2]
    has_res, has_bias, has_into = res is not None, bias is not None, alias_into is not None
    acc_shape = tuple(d for d in o_blk if d is not None)

    def body(*refs):
        a_ref, b_ref = refs[0], refs[1]
        pos = 2
        res_ref = bias_ref = None
        if has_res:
            res_ref = refs[pos]
            pos += 1
        if has_bias:
            bias_ref = refs[pos]
            pos += 1
        if has_into:
            pos += 1
        o_ref = refs[pos]
        av, bv = a_ref[...], b_ref[...]
        if av.dtype != BF16:
            av = av.astype(BF16)
        if bv.dtype != BF16:
            bv = bv.astype(BF16)
        part = lax.dot_general(av, bv, dims, preferred_element_type=F32)

        def finish(acc):
            if scale is not None:
                acc = acc * scale
            if has_bias:
                acc = acc + bias_ref[...]
            if has_res:
                acc = acc + res_ref[...]
            o_ref[...] = acc.astype(o_ref.dtype)

        if nk == 1:
            finish(part)
        else:
            acc_ref = refs[pos + 1]
            k = pl.program_id(2)

            @pl.when(k == 0)
            def _():
                acc_ref[...] = part

            @pl.when(k > 0)
            def _():
                acc_ref[...] += part

            @pl.when(k == nk - 1)
            def _():
                finish(acc_ref[...])

    operands = [a, b]
    in_specs = [pl.BlockSpec(a_blk, a_map), pl.BlockSpec(b_blk, b_map)]
    if has_res:
        operands.append(res)
        in_specs.append(pl.BlockSpec(o_blk, o_map))
    if has_bias:
        operands.append(bias)
        in_specs.append(pl.BlockSpec(bias_blk, bias_map))
    aliases = {}
    if has_into:
        aliases = {len(operands): 0}
        operands.append(alias_into)
        in_specs.append(pl.BlockSpec(memory_space=pl.ANY))
    return pl.pallas_call(
        body, name=name, grid=grid, in_specs=in_specs, out_specs=pl.BlockSpec(o_blk, o_map),
        out_shape=jax.ShapeDtypeStruct(o_shape, o_dtype),
        scratch_shapes=[pltpu.VMEM(acc_shape, F32)] if nk > 1 else [],
        input_output_aliases=aliases,
        compiler_params=_cparams(("parallel", "parallel", "arbitrary")),
    )(*operands)


def mm_nn(name, a, w3, kind, layer, *, out_dtype, scale=None, res=None, bias=None, tm=1024, tn=512, tk=2048):
    M, K = a.shape
    S, _, C = w3.shape
    tm = _tile(M, tm, 16)
    if kind == "col":
        N = S * C
        tk, tn = _tile(K, tk), _tile(C, tn)
        kb, nb = K // tk, C // tn
        b_map = lambda n, m, k: (n // nb, layer * kb + k, n % nb)
    else:
        N, K4 = C, K // S
        tk, tn = _tile(K4, tk), _tile(C, tn)
        kb4 = K4 // tk
        b_map = lambda n, m, k: (k // kb4, layer * kb4 + k % kb4, n)
    return _matmul(
        name, a, w3, grid=(N // tn, M // tm, K // tk),
        a_blk=(tm, tk), a_map=lambda n, m, k: (m, k), b_blk=(None, tk, tn), b_map=b_map,
        o_shape=(M, N), o_dtype=out_dtype, o_blk=(tm, tn), o_map=lambda n, m, k: (m, n), dims=NN_DIMS,
        scale=scale, res=res, bias=bias, bias_blk=(1, tn), bias_map=lambda n, m, k: (0, n))


def _mm_nt(name, g, w3, kind, layer, K, *, out_dtype, scale=None, tm=1024, to=1024, tc=1408):
    M, N = g.shape
    S, _, C = w3.shape
    tm = _tile(M, tm, 16)
    tc = _tile(C, tc)
    if kind == "col":
        nb = C // tc
        to = _tile(K, to)
        ob = K // to
        b_map = lambda o, m, c: (c // nb, layer * ob + o, c % nb)
    else:
        K4 = K // S
        to = _tile(K4, to)
        ob4 = K4 // to
        b_map = lambda o, m, c: (o // ob4, layer * ob4 + o % ob4, c)
    return _matmul(
        name, g, w3, grid=(K // to, M // tm, N // tc),
        a_blk=(tm, tc), a_map=lambda o, m, c: (m, c), b_blk=(None, to, tc), b_map=b_map,
        o_shape=(M, K), o_dtype=out_dtype, o_blk=(tm, to), o_map=lambda o, m, c: (m, o), dims=NT_DIMS,
        scale=scale)


def mm_tn(name, a, g, kind, *, shards=N_CHIPS, layer=0, layers=1, into=None, out_dtype=BF16, scale=None,
          tk=1024, tn=1408, tm=2048):
    M, K = a.shape
    N = g.shape[1]
    tm = _tile(M, tm, 16)
    if kind == "col":
        C = N // shards
        tk, tn = _tile(K, tk), _tile(C, tn)
        kb, nb = K // tk, C // tn
        o_shape = (shards, layers * K, C)
        o_map = lambda k, n, m: (n // nb, layer * kb + k, n % nb)
    else:
        K4 = K // shards
        tk, tn = _tile(K4, tk), _tile(N, tn)
        kb4 = K4 // tk
        o_shape = (shards, layers * K4, N)
        o_map = lambda k, n, m: (k // kb4, layer * kb4 + k % kb4, n)
    return _matmul(
        name, a, g, grid=(K // tk, N // tn, M // tm),
        a_blk=(tm, tk), a_map=lambda k, n, m: (m, k), b_blk=(tm, tn), b_map=lambda k, n, m: (m, n),
        o_shape=o_shape, o_dtype=out_dtype, o_blk=(None, tk, tn), o_map=o_map, dims=TN_DIMS,
        scale=scale, alias_into=into)


def rope_tables(name, pos):
    T = pos.shape[0]
    half = D_ROPE // 2

    def body(p_ref, c_ref, s1_ref, s2_ref):
        lane = lax.broadcasted_iota(jnp.int32, (T, 128), 1)
        idx = (lane & (half - 1)).astype(F32)
        ang = p_ref[...] * jnp.exp(idx * (-2.0 * math.log(ROPE_THETA) / D_ROPE))
        cs, sn = jnp.cos(ang), jnp.sin(ang)
        c_ref[...] = jnp.where(lane < D_ROPE, cs, 0.0)
        s1_ref[...] = jnp.where(lane < half, -sn, 0.0)
        s2_ref[...] = jnp.where((lane >= half) & (lane < D_ROPE), sn, 0.0)

    return pl.pallas_call(body, name=name, out_shape=[jax.ShapeDtypeStruct((T, 128), F32)] * 3,
                          compiler_params=_cparams())(pos)


def _rope(v, cs, s1, s2):
    return v * cs + pltpu.roll(v, 128 - D_ROPE // 2, 1) * s1 + pltpu.roll(v, D_ROPE // 2, 1) * s2


def _rope_bwd(d, cs, s1, s2):
    return d * cs + pltpu.roll(d * s1, D_ROPE // 2, 1) + pltpu.roll(d * s2, 128 - D_ROPE // 2, 1)


def _head_rstd(n, r):
    ms = (jnp.sum(n * n, axis=-1, keepdims=True) + jnp.sum(r * r, axis=-1, keepdims=True)) * (1.0 / QK_DIM)
    return lax.rsqrt(ms + EPS)


def mla_prep_fwd(name, q_raw, kv_raw, lat, tabs, q_gain, k_gain, rope_col, tm=128):
    T = q_raw.shape[0]
    H = q_raw.shape[1] // HEAD_PAD
    tm = min(tm, T)
    rope_blk = rope_col // 128

    def body(q_ref, kv_ref, kr_ref, c_ref, s1_ref, s2_ref, qg_ref, kg_ref, Q_ref, K_ref, V_ref):
        cs, s1, s2 = c_ref[...], s1_ref[...], s2_ref[...]
        qg, kg = qg_ref[...], kg_ref[...]
        kr = kr_ref[...]
        for h in range(H):
            lo = HEAD_PAD * h
            n, r = q_ref[:, lo:lo + 128], q_ref[:, lo + 128:lo + 256]
            rs = _head_rstd(n, r)
            Q_ref[h, :, 0:128] = (n * rs * qg[:, :128]).astype(BF16)
            Q_ref[h, :, 128:256] = _rope(r * rs * qg[:, 128:], cs, s1, s2).astype(BF16)
            n = kv_ref[:, lo:lo + 128]
            rs = _head_rstd(n, kr)
            K_ref[h, :, 0:128] = (n * rs * kg[:, :128]).astype(BF16)
            K_ref[h, :, 128:256] = _rope(kr * rs * kg[:, 128:], cs, s1, s2).astype(BF16)
            V_ref[h] = kv_ref[:, lo + 128:lo + 256].astype(BF16)

    row = lambda w: pl.BlockSpec((tm, w), lambda i: (i, 0))
    vec = pl.BlockSpec((1, HEAD_PAD), lambda i: (0, 0))
    return pl.pallas_call(
        body, name=name, grid=(T // tm,),
        in_specs=[row(H * HEAD_PAD), row(H * HEAD_PAD), pl.BlockSpec((tm, 128), lambda i: (i, rope_blk)),
                  row(128), row(128), row(128), vec, vec],
        out_specs=[pl.BlockSpec((H, tm, HEAD_PAD), lambda i: (0, i, 0))] * 2 + [pl.BlockSpec((H, tm, D_V), lambda i: (0, i, 0))],
        out_shape=[jax.ShapeDtypeStruct((H, T, HEAD_PAD), BF16)] * 2 + [jax.ShapeDtypeStruct((H, T, D_V), BF16)],
        compiler_params=_cparams(("arbitrary",)),
    )(q_raw, kv_raw, lat, *tabs, q_gain, k_gain)


def mla_prep_bwd(name, dQ, dK, dV, q_raw, kv_raw, lat, tabs, q_gain, k_gain, rope_col, tm=128):
    T = q_raw.shape[0]
    H = q_raw.shape[1] // HEAD_PAD
    tm = min(tm, T)
    rope_blk = rope_col // 128

    def body(dQ_ref, dK_ref, dV_ref, q_ref, kv_ref, kr_ref, c_ref, s1_ref, s2_ref, qg_ref, kg_ref,
             dq_ref, dkv_ref, dkr_ref, dqg_ref, dkg_ref):
        @pl.when(pl.program_id(0) == 0)
        def _():
            dqg_ref[...] = jnp.zeros_like(dqg_ref)
            dkg_ref[...] = jnp.zeros_like(dkg_ref)

        cs, s1, s2 = c_ref[...], s1_ref[...], s2_ref[...]
        qg, kg = qg_ref[...], kg_ref[...]
        kr = kr_ref[...]
        dkr = jnp.zeros((tm, 128), F32)
        gq_n = jnp.zeros((1, 128), F32)
        gq_r = jnp.zeros((1, 128), F32)
        gk_n = jnp.zeros((1, 128), F32)
        gk_r = jnp.zeros((1, 128), F32)

        def norm_bwd(n, r, dn, dr, gain):
            rs = _head_rstd(n, r)
            nh, rh = n * rs, r * rs
            dng, drg = dn * gain[:, :128], dr * gain[:, 128:]
            mean = (jnp.sum(dng * nh, axis=-1, keepdims=True) + jnp.sum(drg * rh, axis=-1, keepdims=True)) * (1.0 / QK_DIM)
            return rs * (dng - nh * mean), rs * (drg - rh * mean), _colsum(dn * nh), _colsum(dr * rh)

        for h in range(H):
            lo = HEAD_PAD * h
            n, r = q_ref[:, lo:lo + 128], q_ref[:, lo + 128:lo + 256]
            dn = dQ_ref[h, :, 0:128].astype(F32)
            dr = _rope_bwd(dQ_ref[h, :, 128:256].astype(F32), cs, s1, s2)
            a, b, g1, g2 = norm_bwd(n, r, dn, dr, qg)
            dq_ref[:, lo:lo + 128] = a.astype(BF16)
            dq_ref[:, lo + 128:lo + 256] = b.astype(BF16)
            gq_n, gq_r = gq_n + g1, gq_r + g2
            n = kv_ref[:, lo:lo + 128]
            dn = dK_ref[h, :, 0:128].astype(F32)
            dr = _rope_bwd(dK_ref[h, :, 128:256].astype(F32), cs, s1, s2)
            a, b, g1, g2 = norm_bwd(n, kr, dn, dr, kg)
            dkv_ref[:, lo:lo + 128] = a.astype(BF16)
            dkv_ref[:, lo + 128:lo + 256] = dV_ref[h].astype(BF16)
            dkr = dkr + b
            gk_n, gk_r = gk_n + g1, gk_r + g2
        dkr_ref[...] = dkr
        dqg_ref[:, 0:128] += gq_n
        dqg_ref[:, 128:256] += gq_r
        dkg_ref[:, 0:128] += gk_n
        dkg_ref[:, 128:256] += gk_r

    row = lambda w: pl.BlockSpec((tm, w), lambda i: (i, 0))
    vec = pl.BlockSpec((1, HEAD_PAD), lambda i: (0, 0))
    hd = lambda w: pl.BlockSpec((H, tm, w), lambda i: (0, i, 0))
    return pl.pallas_call(
        body, name=name, grid=(T // tm,),
        in_specs=[hd(HEAD_PAD), hd(HEAD_PAD), hd(D_V), row(H * HEAD_PAD), row(H * HEAD_PAD),
                  pl.BlockSpec((tm, 128), lambda i: (i, rope_blk)), row(128), row(128), row(128), vec, vec],
        out_specs=[row(H * HEAD_PAD), row(H * HEAD_PAD), row(128), vec, vec],
        out_shape=[jax.ShapeDtypeStruct((T, H * HEAD_PAD), BF16)] * 2 + [jax.ShapeDtypeStruct((T, 128), F32)]
        + [jax.ShapeDtypeStruct((1, HEAD_PAD), F32)] * 2,
        compiler_params=_cparams(("arbitrary",)),
    )(dQ, dK, dV, q_raw, kv_raw, lat, *tabs, q_gain, k_gain)


def _causal_probs(q, k, scale, row0):
    s = lax.dot_general(q, k, NT_DIMS, preferred_element_type=F32) * scale
    row = row0 + lax.broadcasted_iota(jnp.int32, s.shape, 0)
    col = lax.broadcasted_iota(jnp.int32, s.shape, 1)
    s = jnp.where(col <= row, s, NEG_BIG)
    p = jnp.exp(s - jnp.max(s, axis=-1, keepdims=True))
    return p, jnp.sum(p, axis=-1, keepdims=True)


def attn_fwd(name, Q, K, V, tq=512):
    H, T, E = Q.shape
    tq = min(tq, T)
    nq = T // tq
    scale = QK_DIM ** -0.5

    def body(q_ref, k_ref, v_ref, o_ref):
        i = pl.program_id(1)
        for ib in range(nq):
            @pl.when(i == ib)
            def _():
                n = (ib + 1) * tq
                p, l = _causal_probs(q_ref[...], k_ref[0:n, :], scale, ib * tq)
                o = jnp.dot(p.astype(BF16), v_ref[0:n, :], preferred_element_type=F32)
                o_ref[...] = (o / l).astype(o_ref.dtype)

    return pl.pallas_call(
        body, name=name, grid=(H, nq),
        in_specs=[pl.BlockSpec((None, tq, E), lambda h, i: (h, i, 0)),
                  pl.BlockSpec((None, T, E), lambda h, i: (h, 0, 0)),
                  pl.BlockSpec((None, T, D_V), lambda h, i: (h, 0, 0))],
        out_specs=pl.BlockSpec((tq, D_V), lambda h, i: (i, h)),
        out_shape=jax.ShapeDtypeStruct((T, H * D_V), BF16),
        compiler_params=_cparams(("parallel", "arbitrary")),
    )(Q, K, V)


def attn_bwd(name, Q, K, V, dO, tq=512):
    H, T, E = Q.shape
    tq = min(tq, T)
    nq = T // tq
    scale = QK_DIM ** -0.5

    def body(q_ref, k_ref, v_ref, do_ref, dq_ref, dk_ref, dv_ref):
        i = pl.program_id(1)

        @pl.when(i == 0)
        def _():
            dk_ref[...] = jnp.zeros_like(dk_ref)
            dv_ref[...] = jnp.zeros_like(dv_ref)

        for ib in range(nq):
            @pl.when(i == ib)
            def _():
                n = (ib + 1) * tq
                q, k, v, do = q_ref[...], k_ref[0:n, :], v_ref[0:n, :], do_ref[...]
                p, l = _causal_probs(q, k, scale, ib * tq)
                p = p / l
                dp = lax.dot_general(do, v, NT_DIMS, preferred_element_type=F32)
                ds = p * (dp - jnp.sum(p * dp, axis=-1, keepdims=True)) * scale
                dsb, pb = ds.astype(BF16), p.astype(BF16)
                dq_ref[...] = jnp.dot(dsb, k, preferred_element_type=F32)
                dk_ref[0:n, :] += lax.dot_general(dsb, q, TN_DIMS, preferred_element_type=F32)
                dv_ref[0:n, :] += lax.dot_general(pb, do, TN_DIMS, preferred_element_type=F32)

    return pl.pallas_call(
        body, name=name, grid=(H, nq),
        in_specs=[pl.BlockSpec((None, tq, E), lambda h, i: (h, i, 0)),
                  pl.BlockSpec((None, T, E), lambda h, i: (h, 0, 0)),
                  pl.BlockSpec((None, T, D_V), lambda h, i: (h, 0, 0)),
                  pl.BlockSpec((tq, D_V), lambda h, i: (i, h))],
        out_specs=[pl.BlockSpec((None, tq, E), lambda h, i: (h, i, 0)),
                   pl.BlockSpec((None, T, E), lambda h, i: (h, 0, 0)),
                   pl.BlockSpec((None, T, D_V), lambda h, i: (h, 0, 0))],
        out_shape=[jax.ShapeDtypeStruct((H, T, E), F32)] * 2 + [jax.ShapeDtypeStruct((H, T, D_V), F32)],
        compiler_params=_cparams(("parallel", "arbitrary")),
    )(Q, K, V, dO)


def _dw_specs(T, C, tm, tc, halo):
    cur = pl.BlockSpec((tm, tc), lambda j, i: (i, j))
    last = T // tm - 1
    if halo == "prev":
        nbr = pl.BlockSpec((tm, tc), lambda j, i: (jnp.maximum(i - 1, 0), j))
    else:
        nbr = pl.BlockSpec((tm, tc), lambda j, i: (jnp.minimum(i + 1, last), j))
    return cur, nbr


def dwconv_fwd(name, u, w, b, tm=256, tc=512, rs=32):
    T, C = u.shape
    tm, tc = min(tm, T), min(tc, C)
    cur, prev = _dw_specs(T, C, tm, tc, "prev")

    def body(up_ref, uc_ref, w_ref, b_ref, o_ref, scr):
        i = pl.program_id(1)

        @pl.when(i == 0)
        def _():
            scr[pl.ds(0, tm), :] = jnp.zeros((tm, tc), F32)

        @pl.when(i > 0)
        def _():
            scr[pl.ds(0, tm), :] = up_ref[...]

        scr[pl.ds(tm, tm), :] = uc_ref[...]
        for s in range(tm // rs):
            acc = jnp.broadcast_to(b_ref[...], (rs, tc))
            for k in range(CONV_WIDTH):
                acc = acc + w_ref[pl.ds(k, 1), :] * scr[pl.ds(tm - (CONV_WIDTH - 1) + k + rs * s, rs), :]
            o_ref[pl.ds(rs * s, rs), :] = acc

    return pl.pallas_call(
        body, name=name, grid=(C // tc, T // tm),
        in_specs=[prev, cur, pl.BlockSpec((CONV_TAPS_PAD, tc), lambda j, i: (0, j)), pl.BlockSpec((1, tc), lambda j, i: (0, j))],
        out_specs=cur, out_shape=jax.ShapeDtypeStruct((T, C), F32),
        scratch_shapes=[pltpu.VMEM((2 * tm, tc), F32)], compiler_params=_cparams(("parallel", "arbitrary")),
    )(u, u, w, b)


def dwconv_bwd_u(name, dy, w, tm=256, tc=512, rs=32):
    T, C = dy.shape
    tm, tc = min(tm, T), min(tc, C)
    cur, nxt = _dw_specs(T, C, tm, tc, "next")
    last = T // tm - 1

    def body(dc_ref, dn_ref, w_ref, o_ref, scr):
        i = pl.program_id(1)
        scr[pl.ds(0, tm), :] = dc_ref[...]

        @pl.when(i == last)
        def _():
            scr[pl.ds(tm, tm), :] = jnp.zeros((tm, tc), F32)

        @pl.when(i < last)
        def _():
            scr[pl.ds(tm, tm), :] = dn_ref[...]

        for s in range(tm // rs):
            acc = jnp.zeros((rs, tc), F32)
            for k in range(CONV_WIDTH):
                acc = acc + w_ref[pl.ds(k, 1), :] * scr[pl.ds((CONV_WIDTH - 1) - k + rs * s, rs), :]
            o_ref[pl.ds(rs * s, rs), :] = acc

    return pl.pallas_call(
        body, name=name, grid=(C // tc, T // tm),
        in_specs=[cur, nxt, pl.BlockSpec((CONV_TAPS_PAD, tc), lambda j, i: (0, j))],
        out_specs=cur, out_shape=jax.ShapeDtypeStruct((T, C), F32),
        scratch_shapes=[pltpu.VMEM((2 * tm, tc), F32)], compiler_params=_cparams(("parallel", "arbitrary")),
    )(dy, dy, w)


def dwconv_bwd_w(name, u, dy, tm=256, tc=512, rs=32):
    T, C = u.shape
    tm, tc = min(tm, T), min(tc, C)
    cur, prev = _dw_specs(T, C, tm, tc, "prev")

    def body(up_ref, uc_ref, dy_ref, o_ref, scr):
        i = pl.program_id(1)

        @pl.when(i == 0)
        def _():
            scr[pl.ds(0, tm), :] = jnp.zeros((tm, tc), F32)
            o_ref[...] = jnp.zeros_like(o_ref)

        @pl.when(i > 0)
        def _():
            scr[pl.ds(0, tm), :] = up_ref[...]

        scr[pl.ds(tm, tm), :] = uc_ref[...]
        for k in range(CONV_WIDTH):
            acc = jnp.zeros((rs, tc), F32)
            for s in range(tm // rs):
                acc = acc + dy_ref[pl.ds(rs * s, rs), :] * scr[pl.ds(tm - (CONV_WIDTH - 1) + k + rs * s, rs), :]
            o_ref[pl.ds(k, 1), :] += _colsum(acc)

    return pl.pallas_call(
        body, name=name, grid=(C // tc, T // tm),
        in_specs=[prev, cur, cur],
        out_specs=pl.BlockSpec((CONV_TAPS_PAD, tc), lambda j, i: (0, j)),
        out_shape=jax.ShapeDtypeStruct((CONV_TAPS_PAD, C), F32),
        scratch_shapes=[pltpu.VMEM((2 * tm, tc), F32)], compiler_params=_cparams(("parallel", "arbitrary")),
    )(u, u, dy)


def _place():
    x, y, c = lax.axis_index("x"), lax.axis_index("y"), lax.axis_index("c")
    return x, y, c


def _other_chips(x, y):
    return [(1 - x, y, 2 * (1 - x) + y), (x, 1 - y, 2 * x + (1 - y)), (1 - x, 1 - y, 2 * (1 - x) + (1 - y))]


def _hbm_specs(n):
    return [pl.BlockSpec(memory_space=pl.ANY)] * n


def gather_weights(name, bufs):
    n = len(bufs)

    def body(*refs):
        out = refs[n:2 * n]
        send_sems, recv_sems = refs[2 * n], refs[2 * n + 1]
        x, y, c = _place()
        me = 2 * x + y
        chips = _other_chips(x, y)
        sends = []

        def half(ref, slot, which):
            rh = ref.shape[1] // 2
            return ref.at[slot, pl.ds(pl.multiple_of(which * rh, 16), rh), :]

        def copy(a, k, slot, which, to):
            piece = half(out[a], slot, which)
            return pltpu.make_async_remote_copy(
                src_ref=piece, dst_ref=piece, send_sem=send_sems.at[6 * a + k], recv_sem=recv_sems.at[6 * a + k],
                device_id=to, device_id_type=MESH)

        for a in range(n):
            for j, (px, py, _) in enumerate(chips):
                cp = copy(a, j, me, c, (px, py, c))
                cp.start()
                sends.append(cp)
        for a in range(n):
            for j, (px, py, idx) in enumerate(chips):
                copy(a, j, idx, c, (px, py, c)).wait_recv()
                cp = copy(a, 3 + j, idx, c, (x, y, 1 - c))
                cp.start()
                sends.append(cp)
        for a in range(n):
            for j, (px, py, idx) in enumerate(chips):
                copy(a, 3 + j, idx, 1 - c, (x, y, 1 - c)).wait_recv()
        for cp in sends:
            cp.wait_send()

    return pl.pallas_call(
        body, name=name, in_specs=_hbm_specs(n), out_specs=_hbm_specs(n),
        out_shape=[jax.ShapeDtypeStruct(b.shape, b.dtype) for b in bufs],
        scratch_shapes=[pltpu.SemaphoreType.DMA((6 * n,)), pltpu.SemaphoreType.DMA((6 * n,))],
        input_output_aliases={a: a for a in range(n)}, compiler_params=_cparams(),
    )(*bufs)


def swap_grad_halves(name, parts):
    n = len(parts)

    def body(*refs):
        src, dst = refs[:n], refs[n:2 * n]
        send_sems, recv_sems = refs[2 * n], refs[2 * n + 1]
        x, y, c = _place()
        cps = []
        for a in range(n):
            rh = src[a].shape[1] // 2
            cp = pltpu.make_async_remote_copy(
                src_ref=src[a].at[:, pl.ds(pl.multiple_of((1 - c) * rh, 16), rh), :], dst_ref=dst[a],
                send_sem=send_sems.at[a], recv_sem=recv_sems.at[a], device_id=(x, y, 1 - c), device_id_type=MESH)
            cp.start()
            cps.append(cp)
        for cp in cps:
            cp.wait()

    return pl.pallas_call(
        body, name=name, in_specs=_hbm_specs(n), out_specs=_hbm_specs(n),
        out_shape=[jax.ShapeDtypeStruct((p.shape[0], p.shape[1] // 2, p.shape[2]), p.dtype) for p in parts],
        scratch_shapes=[pltpu.SemaphoreType.DMA((n,)), pltpu.SemaphoreType.DMA((n,))], compiler_params=_cparams(),
    )(*parts)


def exchange_chip_sums(name, sums):
    n = len(sums)

    def body(*refs):
        src, dst = refs[:n], refs[n:2 * n]
        send_sems, recv_sems, local_sems = refs[2 * n], refs[2 * n + 1], refs[2 * n + 2]
        x, y, c = _place()
        me = 2 * x + y
        chips = _other_chips(x, y)
        cps, locs = [], []
        for a in range(n):
            loc = pltpu.make_async_copy(src[a].at[me], dst[a].at[me], local_sems.at[a])
            loc.start()
            locs.append(loc)
            for j, (px, py, idx) in enumerate(chips):
                cp = pltpu.make_async_remote_copy(
                    src_ref=src[a].at[idx], dst_ref=dst[a].at[me], send_sem=send_sems.at[3 * a + j],
                    recv_sem=recv_sems.at[3 * a + j], device_id=(px, py, c), device_id_type=MESH)
                cp.start()
                cps.append((cp, a, j, idx))
        for cp, a, j, idx in cps:
            pltpu.make_async_remote_copy(
                src_ref=src[a].at[idx], dst_ref=dst[a].at[idx], send_sem=send_sems.at[3 * a + j],
                recv_sem=recv_sems.at[3 * a + j], device_id=(x, y, c), device_id_type=MESH).wait_recv()
        for cp, a, j, idx in cps:
            cp.wait_send()
        for loc in locs:
            loc.wait()

    return pl.pallas_call(
        body, name=name, in_specs=_hbm_specs(n), out_specs=_hbm_specs(n),
        out_shape=[jax.ShapeDtypeStruct(s.shape, s.dtype) for s in sums],
        scratch_shapes=[pltpu.SemaphoreType.DMA((3 * n,)), pltpu.SemaphoreType.DMA((3 * n,)), pltpu.SemaphoreType.DMA((n,))],
        compiler_params=_cparams(),
    )(*sums)


def share_grad_halves(name, grads):
    n = len(grads)

    def body(*refs):
        out = refs[n:2 * n]
        send_sems, recv_sems = refs[2 * n], refs[2 * n + 1]
        x, y, c = _place()
        cps = []
        for a in range(n):
            rh = out[a].shape[0] // 2
            mine = out[a].at[pl.ds(pl.multiple_of(c * rh, 8), rh), :]
            cp = pltpu.make_async_remote_copy(
                src_ref=mine, dst_ref=mine, send_sem=send_sems.at[a], recv_sem=recv_sems.at[a],
                device_id=(x, y, 1 - c), device_id_type=MESH)
            cp.start()
            cps.append(cp)
        for a, cp in enumerate(cps):
            rh = out[a].shape[0] // 2
            theirs = out[a].at[pl.ds(pl.multiple_of((1 - c) * rh, 8), rh), :]
            pltpu.make_async_remote_copy(
                src_ref=theirs, dst_ref=theirs, send_sem=send_sems.at[a], recv_sem=recv_sems.at[a],
                device_id=(x, y, 1 - c), device_id_type=MESH).wait_recv()
            cp.wait_send()

    return pl.pallas_call(
        body, name=name, in_specs=_hbm_specs(n), out_specs=_hbm_specs(n),
        out_shape=[jax.ShapeDtypeStruct(g.shape, g.dtype) for g in grads],
        scratch_shapes=[pltpu.SemaphoreType.DMA((n,)), pltpu.SemaphoreType.DMA((n,))],
        input_output_aliases={a: a for a in range(n)}, compiler_params=_cparams(),
    )(*grads)


def allreduce_pack(name, pack):
    R, W = pack.shape

    def body(p_ref, o_ref, sib, pair, got, send_sems, recv_sems):
        x, y, c = _place()

        def swap(k, src, dst, to):
            cp = pltpu.make_async_remote_copy(src_ref=src, dst_ref=dst, send_sem=send_sems.at[k],
                                              recv_sem=recv_sems.at[k], device_id=to, device_id_type=MESH)
            cp.start()
            return cp

        cp = swap(0, p_ref, sib, (x, y, 1 - c))
        cp.wait()
        pair[...] = p_ref[...] + sib[...]
        cps = [swap(1, pair, got.at[0], (1 - x, y, c)), swap(2, pair, got.at[1], (x, 1 - y, c)),
               swap(3, pair, got.at[2], (1 - x, 1 - y, c))]
        for cp in cps:
            cp.wait()
        o_ref[...] = (pair[...] + got[1]) + (got[0] + got[2])

    return pl.pallas_call(
        body, name=name, out_shape=jax.ShapeDtypeStruct((R, W), F32),
        in_specs=[pl.BlockSpec(memory_space=pltpu.VMEM)], out_specs=pl.BlockSpec(memory_space=pltpu.VMEM),
        scratch_shapes=[pltpu.VMEM((R, W), F32), pltpu.VMEM((R, W), F32), pltpu.VMEM((3, R, W), F32),
                        pltpu.SemaphoreType.DMA((4,)), pltpu.SemaphoreType.DMA((4,))],
        compiler_params=_cparams(),
    )(pack)


BIG_WEIGHTS = [
    ("ffn_a_w_in", "col"), ("ffn_a_w_out", "row"), ("ffn_b_w_in", "col"), ("ffn_b_w_out", "row"),
    ("mla_w_in", "row"), ("mla_w_uq", "col"), ("mla_w_ukv", "col"), ("mla_w_o", "row"),
    ("conv_w_pw1", "col"), ("conv_w_pw2", "row"), ("ple_w_proj", "col"), ("ple_w_gate", "row"),
]
WEIGHT_ORDER = ["ffn_a_norm", "ffn_a_w_in", "ffn_a_w_out", "ffn_b_norm", "ffn_b_w_in", "ffn_b_w_out", "mix_norm",
                "mla_w_in", "mla_q_lat_norm", "mla_kv_lat_norm", "mla_w_uq", "mla_w_ukv", "mla_q_gain", "mla_k_gain",
                "mla_w_o", "conv_w_pw1", "conv_b_pw1", "conv_w_dw", "conv_b_dw", "conv_ln_g", "conv_ln_b", "conv_w_pw2",
                "ple_w_proj", "ple_norm", "ple_gate_norm", "ple_w_gate"]
REPLICATED_SMALL = ["ffn_a_norm", "ffn_b_norm", "mix_norm", "ple_norm", "ple_gate_norm",
                    "mla_q_lat_norm", "mla_kv_lat_norm", "mla_q_gain", "mla_k_gain"]
SHARDED_SMALL = ["conv_b_pw1", "conv_w_dw", "conv_b_dw", "conv_ln_g", "conv_ln_b"]
PACK_ROWS = 8


def _pack_rows(arrs, width):
    out = []
    for a in arrs:
        r = -(-a.shape[0] // PACK_ROWS) * PACK_ROWS
        out.append(jnp.pad(a, ((0, r - a.shape[0]), (0, width - a.shape[1]))))
    return jnp.concatenate(out, axis=0)


def _unpack_rows(pack, shapes):
    out, r0 = [], 0
    for (r, w) in shapes:
        out.append(pack[r0:r0 + r, :w])
        r0 += -(-r // PACK_ROWS) * PACK_ROWS
    return out


def kernel(x, p, positions, ffn_a_norm, ffn_a_w_in, ffn_a_w_out, ffn_b_norm, ffn_b_w_in, ffn_b_w_out, mix_norm, mla_w_in, mla_q_lat_norm, mla_kv_lat_norm, mla_w_uq, mla_w_ukv, mla_q_gain, mla_k_gain, mla_w_o, conv_w_pw1, conv_b_pw1, conv_w_dw, conv_b_dw, conv_ln_g, conv_ln_b, conv_w_pw2, ple_w_proj, ple_norm, ple_gate_norm, ple_w_gate, loss_target, m_ffn_a_norm, m_ffn_a_w_in, m_ffn_a_w_out, m_ffn_b_norm, m_ffn_b_w_in, m_ffn_b_w_out, m_mix_norm, m_mla_w_in, m_mla_q_lat_norm, m_mla_kv_lat_norm, m_mla_w_uq, m_mla_w_ukv, m_mla_q_gain, m_mla_k_gain, m_mla_w_o, m_conv_w_pw1, m_conv_b_pw1, m_conv_w_dw, m_conv_b_dw, m_conv_ln_g, m_conv_ln_b, m_conv_w_pw2, m_ple_w_proj, m_ple_norm, m_ple_gate_norm, m_ple_w_gate, v_ffn_a_norm, v_ffn_a_w_in, v_ffn_a_w_out, v_ffn_b_norm, v_ffn_b_w_in, v_ffn_b_w_out, v_mix_norm, v_mla_w_in, v_mla_q_lat_norm, v_mla_kv_lat_norm, v_mla_w_uq, v_mla_w_ukv, v_mla_q_gain, v_mla_k_gain, v_mla_w_o, v_conv_w_pw1, v_conv_b_pw1, v_conv_w_dw, v_conv_b_dw, v_conv_ln_g, v_conv_ln_b, v_conv_w_pw2, v_ple_w_proj, v_ple_norm, v_ple_gate_norm, v_ple_w_gate):
    args = dict(locals())
    W = {n: args[n] for n in WEIGHT_ORDER}
    M1 = {n: args["m_" + n] for n in WEIGHT_ORDER}
    V2 = {n: args["v_" + n] for n in WEIGHT_ORDER}

    T, D = x.shape[1], x.shape[2]
    depth = ffn_a_norm.shape[0]
    H = mla_w_ukv.shape[2] * N_CHIPS // (D_NOPE + D_V)
    QL, KL = mla_q_lat_norm.shape[1], mla_kv_lat_norm.shape[1]
    C = conv_w_pw2.shape[1] * N_CHIPS
    lat_w = QL + KL + D_ROPE
    lat_pad = QL + KL + 128

    cx, cy, cc = lax.axis_index("x"), lax.axis_index("y"), lax.axis_index("c")
    chip = (2 * cx + cy).astype(jnp.int32)
    chip_arr = chip.reshape(1)
    core_arr = cc.astype(jnp.int32).reshape(1)

    slots = [cast_into_slot("cast_" + n, W[n].reshape(-1, W[n].shape[-1]), chip_arr) for n, _ in BIG_WEIGHTS]
    gathered = gather_weights("gather_weights", slots)
    G = {n: g for (n, _), g in zip(BIG_WEIGHTS, gathered)}
    KIND = dict(BIG_WEIGHTS)

    def placed(a, width):
        full = jnp.zeros(a.shape[:-1] + (width,), F32)
        full = lax.dynamic_update_slice_in_dim(full, a, chip * a.shape[-1], axis=a.ndim - 1)
        return full * (cc == 0).astype(F32)

    b_pw1_sh = conv_b_pw1.reshape(1, -1)
    small_in = [placed(b_pw1_sh, 2 * C).reshape(2, C), placed(conv_w_dw[0], C), placed(conv_b_dw, C),
                placed(conv_ln_g, C), placed(conv_ln_b, C)]
    small_full = _unpack_rows(allreduce_pack("gather_small", _pack_rows(small_in, C)),
                              [(2, C), (CONV_WIDTH, C), (1, C), (1, C), (1, C)])
    b_pw1_full = small_full[0].reshape(1, 2 * C)
    w_dw_full = jnp.pad(small_full[1], ((0, CONV_TAPS_PAD - CONV_WIDTH), (0, 0)))
    b_dw_full, ln_g_full, ln_b_full = small_full[2], small_full[3], small_full[4]

    w_in_pad = jnp.pad(G["mla_w_in"].reshape(D, lat_w), ((0, 0), (0, lat_pad - lat_w)))[None]
    uq = G["mla_w_uq"].transpose(1, 0, 2).reshape(QL, H, QK_DIM)
    w_uq_pad = jnp.pad(uq, ((0, 0), (0, 0), (0, HEAD_PAD - QK_DIM))).reshape(1, QL, H * HEAD_PAD)
    pad_gain = lambda g: jnp.pad(g, ((0, 0), (0, HEAD_PAD - QK_DIM)))
    q_gain_p, k_gain_p = pad_gain(mla_q_gain), pad_gain(mla_k_gain)
    tabs = rope_tables("rope_tables", positions.reshape(T, 1).astype(F32))

    def ffn_fwd(tag, h, norm, w_in, w_out, layer):
        hn = rms_fwd(f"{tag}_rms", h, norm)
        gu = mm_nn(f"{tag}_in", hn, G[w_in], "col", layer, out_dtype=BF16, tm=2048)
        act = swiglu_fwd(f"{tag}_act", gu)
        out = mm_nn(f"{tag}_out", act, G[w_out], "row", layer, out_dtype=F32, scale=FFN_RESIDUAL_WEIGHT, res=h,
                    tm=1024, tn=1024, tk=1408)
        return out, (h, hn, gu, act)

    saved = []
    h = x[0]
    for i in range(depth):
        L = {}
        h, L["ffn_a"] = ffn_fwd(f"l{i}_ffa", h, ffn_a_norm[i:i + 1], "ffn_a_w_in", "ffn_a_w_out", i)
        L["h1"] = h
        hn = rms_fwd(f"l{i}_mix_rms", h, mix_norm[i:i + 1])
        L["hn_m"] = hn
        if i % 2 == 0:
            lat = mm_nn(f"l{i}_lat", hn, w_in_pad, "row", 0, out_dtype=F32, tm=2048)
            cq, ckv = lat_norm_fwd(f"l{i}_latnorm", lat, mla_q_lat_norm, mla_kv_lat_norm)
            q_raw = mm_nn(f"l{i}_uq", cq, w_uq_pad, "row", 0, out_dtype=F32, tm=2048)
            kv_raw = mm_nn(f"l{i}_ukv", ckv, G["mla_w_ukv"], "col", 0, out_dtype=F32, tm=2048)
            Qh, Kh, Vh = mla_prep_fwd(f"l{i}_prep", q_raw, kv_raw, lat, tabs, q_gain_p, k_gain_p, QL + KL)
            O = attn_fwd(f"l{i}_attn", Qh, Kh, Vh)
            h = mm_nn(f"l{i}_wo", O, G["mla_w_o"], "row", 0, out_dtype=F32, res=h, tm=1024, tn=1024)
            L["mla"] = (lat, cq, ckv, q_raw, kv_raw, Qh, Kh, Vh, O)
        else:
            ag = mm_nn(f"l{i}_pw1", hn, G["conv_w_pw1"], "col", 0, out_dtype=F32, bias=b_pw1_full, tm=2048)
            u = glu_fwd(f"l{i}_glu", ag)
            yc = dwconv_fwd(f"l{i}_dw", u, w_dw_full, b_dw_full)
            cact = ln_silu_fwd(f"l{i}_ln", yc, ln_g_full, ln_b_full)
            h = mm_nn(f"l{i}_pw2", cact, G["conv_w_pw2"], "row", 0, out_dtype=F32, res=h, tm=1024, tn=1024)
            L["conv"] = (ag, u, yc, cact)
        L["h2"] = h
        h, L["ffn_b"] = ffn_fwd(f"l{i}_ffb", h, ffn_b_norm[i:i + 1], "ffn_b_w_in", "ffn_b_w_out", i)
        L["h3"] = h
        pe = mm_nn(f"l{i}_ple_proj", p[i, 0], G["ple_w_proj"], "col", i, out_dtype=F32, tm=2048)
        hg = rms_fwd(f"l{i}_gate_rms", h, ple_gate_norm[i:i + 1])
        z = mm_nn(f"l{i}_ple_gate", hg, G["ple_w_gate"], "row", i, out_dtype=F32, tm=2048)
        h = ple_fwd(f"l{i}_ple", h, pe, z, ple_norm[i:i + 1])
        L["ple"] = (pe, hg, z)
        saved.append(L)

    d_h, loss_part = loss_head("loss_head", h, loss_target[0])
    loss = lax.psum(loss_part[0, 0], ("x", "y", "c"))

    GW = {}
    SG = {}

    def put_small(name, i, val):
        SG.setdefault(name, {})[i] = val

    def ffn_bwd(tag, d_h, d_hb, norm, w_in, w_out, layer, fw):
        h_in, hn, gu, act = fw
        d_act = _mm_nt(f"{tag}_dact", d_hb, G[w_out], "row", layer, act.shape[1], out_dtype=BF16,
                       scale=FFN_RESIDUAL_WEIGHT, to=1408, tc=2048)
        GW[w_out] = mm_tn(f"{tag}_dwout", act, d_hb, "row", layer=layer, layers=depth, into=GW.get(w_out),
                          scale=FFN_RESIDUAL_WEIGHT, tk=1408, tn=1024)
        dgu = swiglu_bwd(f"{tag}_dgu", gu, d_act)
        d_hn = _mm_nt(f"{tag}_dhn", dgu, G[w_in], "col", layer, D, out_dtype=F32)
        GW[w_in] = mm_tn(f"{tag}_dwin", hn, dgu, "col", layer=layer, layers=depth, into=GW.get(w_in))
        return rms_bwd_res(f"{tag}_drms", h_in, d_hn, norm, d_h)

    d_hb = None
    for i in reversed(range(depth)):
        L = saved[i]
        pe, hg, z = L["ple"]
        d_z, d_pe, g = ple_bwd(f"l{i}_dple", d_h, pe, z, ple_norm[i:i + 1])
        put_small("ple_norm", i, g)
        d_hg = _mm_nt(f"l{i}_dhg", d_z, G["ple_w_gate"], "row", i, D, out_dtype=F32, to=512, tc=2048)
        GW["ple_w_gate"] = mm_tn(f"l{i}_dwgate", hg, d_z, "row", layer=i, layers=depth, into=GW.get("ple_w_gate"),
                                 tk=512, tn=1024)
        GW["ple_w_proj"] = mm_tn(f"l{i}_dwproj", p[i, 0], d_pe, "col", layer=i, layers=depth,
                                 into=GW.get("ple_w_proj"))
        d_h, d_hb, g = rms_bwd_res(f"l{i}_dgate_rms", L["h3"], d_hg, ple_gate_norm[i:i + 1], d_h)
        put_small("ple_gate_norm", i, g)

        d_h, d_hb, g = ffn_bwd(f"l{i}_ffb", d_h, d_hb, ffn_b_norm[i:i + 1], "ffn_b_w_in", "ffn_b_w_out", i, L["ffn_b"])
        put_small("ffn_b_norm", i, g)

        hn = L["hn_m"]
        if i % 2 == 0:
            lat, cq, ckv, q_raw, kv_raw, Qh, Kh, Vh, O = L["mla"]
            d_O = _mm_nt(f"l{i}_dO", d_hb, G["mla_w_o"], "row", 0, H * D_V, out_dtype=BF16, to=512, tc=2048)
            GW["mla_w_o"] = mm_tn(f"l{i}_dwo", O, d_hb, "row", tk=512, tn=1024)
            dQ, dK, dV = attn_bwd(f"l{i}_dattn", Qh, Kh, Vh, d_O)
            d_q_raw, d_kv_raw, d_kr, gq, gk = mla_prep_bwd(f"l{i}_dprep", dQ, dK, dV, q_raw, kv_raw, lat, tabs,
                                                           q_gain_p, k_gain_p, QL + KL)
            put_small("mla_q_gain", 0, gq[:, :QK_DIM])
            put_small("mla_k_gain", 0, gk[:, :QK_DIM])
            d_cq = _mm_nt(f"l{i}_dcq", d_q_raw, w_uq_pad, "row", 0, QL, out_dtype=F32, to=512, tc=2048)
            g_uq = mm_tn(f"l{i}_dwuq", cq, d_q_raw, "row", shards=1, out_dtype=F32, tk=512, tn=1024)
            g_uq = g_uq.reshape(QL, H, HEAD_PAD)[:, :, :QK_DIM].reshape(QL, N_CHIPS, -1).transpose(1, 0, 2)
            GW["mla_w_uq"] = g_uq.astype(BF16)
            d_ckv = _mm_nt(f"l{i}_dckv", d_kv_raw, G["mla_w_ukv"], "col", 0, KL, out_dtype=F32, to=512, tc=1024)
            GW["mla_w_ukv"] = mm_tn(f"l{i}_dwukv", ckv, d_kv_raw, "col", tk=512, tn=1024)
            d_lat, gq, gk = lat_norm_bwd(f"l{i}_dlatnorm", lat, d_cq, d_ckv, d_kr, mla_q_lat_norm, mla_kv_lat_norm)
            put_small("mla_q_lat_norm", 0, gq)
            put_small("mla_kv_lat_norm", 0, gk)
            d_hn = _mm_nt(f"l{i}_dhn_lat", d_lat, w_in_pad, "row", 0, D, out_dtype=F32, to=1024, tc=lat_pad)
            g_in = mm_tn(f"l{i}_dwin_lat", hn, d_lat, "row", shards=1, out_dtype=F32, tk=1024, tn=lat_pad)
            GW["mla_w_in"] = g_in[0, :, :lat_w].reshape(N_CHIPS, D // N_CHIPS, lat_w).astype(BF16)
        else:
            ag, u, yc, cact = L["conv"]
            d_cact = _mm_nt(f"l{i}_dcact", d_hb, G["conv_w_pw2"], "row", 0, C, out_dtype=F32, to=512, tc=2048)
            GW["conv_w_pw2"] = mm_tn(f"l{i}_dwpw2", cact, d_hb, "row", tk=512, tn=1024)
            d_yc, g1, g2, g3 = ln_silu_bwd(f"l{i}_dln", yc, d_cact, ln_g_full, ln_b_full)
            put_small("conv_ln_g", 0, g1)
            put_small("conv_ln_b", 0, g2)
            put_small("conv_b_dw", 0, g3)
            d_u = dwconv_bwd_u(f"l{i}_ddw_u", d_yc, w_dw_full)
            put_small("conv_w_dw", 0, dwconv_bwd_w(f"l{i}_ddw_w", u, d_yc))
            d_ag, g = glu_bwd(f"l{i}_dglu", ag, d_u)
            put_small("conv_b_pw1", 0, g)
            d_hn = _mm_nt(f"l{i}_dhn_pw1", d_ag, G["conv_w_pw1"], "col", 0, D, out_dtype=F32, tc=1024)
            GW["conv_w_pw1"] = mm_tn(f"l{i}_dwpw1", hn, d_ag, "col", tn=1024)
        d_h, d_hb, g = rms_bwd_res(f"l{i}_dmix_rms", L["h1"], d_hn, mix_norm[i:i + 1], d_h)
        put_small("mix_norm", i, g)

        d_h, d_hb, g = ffn_bwd(f"l{i}_ffa", d_h, d_hb, ffn_a_norm[i:i + 1], "ffn_a_w_in", "ffn_a_w_out", i, L["ffn_a"])
        put_small("ffn_a_norm", i, g)
    grad_x = d_h[None]

    names = [n for n, _ in BIG_WEIGHTS]
    parts = [GW[n] for n in names]
    from_sibling = swap_grad_halves("swap_grad_halves", parts)
    pair_sums = [add_halves("pair_sum_" + n, core_arr, q, r) for n, q, r in zip(names, parts, from_sibling)]
    chip_sums = exchange_chip_sums("exchange_chip_sums", pair_sums)
    halves = [sum_chips("chip_sum_" + n, s, core_arr, q.shape[1]) for n, s, q in zip(names, chip_sums, parts)]
    full = share_grad_halves("share_grad_halves", halves)
    grads = {n: g.reshape(W[n].shape) for n, g in zip(names, full)}

    rep = []
    for n in REPLICATED_SMALL:
        rep.append(jnp.concatenate([SG[n][i] for i in sorted(SG[n])], axis=0))
    shd = [SG["conv_b_pw1"][0].reshape(2, C), SG["conv_w_dw"][0][:CONV_WIDTH], SG["conv_b_dw"][0],
           SG["conv_ln_g"][0], SG["conv_ln_b"][0]]
    red = allreduce_pack("allreduce_small", _pack_rows(rep + shd, D))
    red = _unpack_rows(red, [a.shape for a in rep + shd])
    for n, g in zip(REPLICATED_SMALL, red):
        grads[n] = g
    own = lambda a, w: lax.dynamic_slice_in_dim(a, chip * w, w, axis=a.ndim - 1)
    sh = red[len(rep):]
    grads["conv_b_pw1"] = own(sh[0].reshape(1, 2 * C), 2 * C // N_CHIPS)
    grads["conv_w_dw"] = own(sh[1], C // N_CHIPS)[None]
    grads["conv_b_dw"] = own(sh[2], C // N_CHIPS)
    grads["conv_ln_g"] = own(sh[3], C // N_CHIPS)
    grads["conv_ln_b"] = own(sh[4], C // N_CHIPS)

    delta, new_m, new_v = {}, {}, {}
    for n in names:
        two = lambda a: a.reshape(-1, a.shape[-1])
        d, nm, nv = adamw("adamw_" + n, two(W[n]), two(grads[n]), two(M1[n]), two(V2[n]))
        delta[n], new_m[n], new_v[n] = d.reshape(W[n].shape), nm.reshape(W[n].shape), nv.reshape(W[n].shape)
    small = REPLICATED_SMALL + SHARDED_SMALL
    two = lambda a: a.reshape(-1, a.shape[-1])
    shapes = [two(W[n]).shape for n in small]
    packs = [_pack_rows([two(src[n]) for n in small], D) for src in (W, grads, M1, V2)]
    outs = adamw("adamw_small", *packs)
    for dst, pk in zip((delta, new_m, new_v), outs):
        for n, a in zip(small, _unpack_rows(pk, shapes)):
            dst[n] = a.reshape(W[n].shape)
    for n in small:
        grads[n] = grads[n].reshape(W[n].shape)

    return (loss, grad_x, *[grads[n] for n in WEIGHT_ORDER], *[delta[n] for n in WEIGHT_ORDER],
            *[new_m[n] for n in WEIGHT_ORDER], *[new_v[n] for n in WEIGHT_ORDER])
```

```python
import functools
import math

import jax
import jax.numpy as jnp
from jax import lax
from jax.experimental import pallas as pl
from jax.experimental.pallas import tpu as pltpu

F32, BF16 = jnp.float32, jnp.bfloat16
MESH = pl.DeviceIdType.MESH

N_CHIPS = 4
EPS = 1e-6
D_NOPE, D_ROPE, D_V = 128, 64, 128
QK_DIM = D_NOPE + D_ROPE
HEAD_PAD = 256
ROPE_THETA = 10000.0
CONV_WIDTH = 31
CONV_TAPS_PAD = 32
FFN_RESIDUAL_WEIGHT = 0.5
ADAM_LR, ADAM_B1, ADAM_B2, ADAM_EPS, ADAM_WD, ADAM_STEP = 0.001, 0.9, 0.999, 1e-08, 0.01, 10
VMEM_LIMIT_BYTES = 56 * 1024 * 1024
NEG_BIG = -1e30

NN_DIMS = (((1,), (0,)), ((), ()))
NT_DIMS = (((1,), (1,)), ((), ()))
TN_DIMS = (((0,), (0,)), ((), ()))


def _cparams(semantics=None):
    kw = dict(vmem_limit_bytes=VMEM_LIMIT_BYTES)
    if semantics is not None:
        kw["dimension_semantics"] = semantics
    return pltpu.CompilerParams(**kw)


def _tile(n, pref, mult=128):
    if n <= pref:
        return n
    t = (pref // mult) * mult
    while t >= mult:
        if n % t == 0:
            return t
        t -= mult
    return n


def _rowwise(name, fn, rows, vecs, outs, accs=(), tm=256, rc=16):
    T = rows[0].shape[0]
    tm = min(tm, T)
    rc = min(rc, tm)
    nr, nv, no, na = len(rows), len(vecs), len(outs), len(accs)
    steps = tm // rc

    def body(*refs):
        row_refs = refs[:nr]
        vec_refs = refs[nr:nr + nv]
        out_refs = refs[nr + nv:nr + nv + no]
        acc_refs = refs[nr + nv + no:]
        if na:
            @pl.when(pl.program_id(0) == 0)
            def _():
                for a in acc_refs:
                    a[...] = jnp.zeros_like(a)

        def step(r, carry):
            sl = pl.ds(pl.multiple_of(r * rc, rc), rc)
            res = fn(*[x[sl, :] for x in row_refs], *[v[...] for v in vec_refs])
            for o, val in zip(out_refs, res[:no]):
                o[sl, :] = val.astype(o.dtype)
            return tuple(c + val for c, val in zip(carry, res[no:]))

        init = tuple(jnp.zeros(s, F32) for s in accs)
        tot = lax.fori_loop(0, steps, step, init)
        for a, val in zip(acc_refs, tot):
            a[...] += val

    in_specs = [pl.BlockSpec((tm, x.shape[1]), lambda i: (i, 0)) for x in rows]
    in_specs += [pl.BlockSpec(v.shape, lambda i: (0, 0)) for v in vecs]
    out_specs = [pl.BlockSpec((tm, d), lambda i: (i, 0)) for d, _ in outs]
    out_specs += [pl.BlockSpec(s, lambda i: (0, 0)) for s in accs]
    out_shape = [jax.ShapeDtypeStruct((T, d), dt) for d, dt in outs]
    out_shape += [jax.ShapeDtypeStruct(s, F32) for s in accs]
    return pl.pallas_call(
        body, name=name, grid=(T // tm,), in_specs=in_specs, out_specs=out_specs, out_shape=out_shape,
        compiler_params=_cparams(("arbitrary",)),
    )(*rows, *vecs)


def _colsum(v):
    return jnp.sum(v, axis=0, keepdims=True)


def _rstd(x):
    return lax.rsqrt(jnp.mean(x * x, axis=-1, keepdims=True) + EPS)


def _rms_bwd(x, dy, g):
    r = _rstd(x)
    xh = x * r
    dyg = dy * g
    dx = r * (dyg - xh * jnp.mean(dyg * xh, axis=-1, keepdims=True))
    return dx, dy * xh


def _sigmoid(x):
    return 1.0 / (1.0 + jnp.exp(-x))


def rms_fwd(name, h, g):
    def fn(x, gv):
        return ((x * _rstd(x)) * gv,)
    return _rowwise(name, fn, [h], [g], [(h.shape[1], BF16)])[0]


def rms_bwd_res(name, h, d_y, g, d_res):
    D = h.shape[1]

    def fn(x, dy, dr, gv):
        dx, dgr = _rms_bwd(x, dy, gv)
        dh = dr + dx
        return dh, dh, _colsum(dgr)
    return _rowwise(name, fn, [h, d_y, d_res], [g], [(D, F32), (D, BF16)], [(1, D)], tm=128)


def swiglu_fwd(name, gu):
    Fh = gu.shape[1] // 2

    def fn(v):
        g = v[:, :Fh].astype(F32)
        u = v[:, Fh:].astype(F32)
        return (g * _sigmoid(g) * u,)
    return _rowwise(name, fn, [gu], [], [(Fh, BF16)])[0]


def swiglu_bwd(name, gu, d_act):
    Fh = gu.shape[1] // 2

    def fn(v, da):
        g = v[:, :Fh].astype(F32)
        u = v[:, Fh:].astype(F32)
        da = da.astype(F32)
        s = _sigmoid(g)
        d_u = da * g * s
        d_g = da * u * s * (1.0 + g * (1.0 - s))
        return (jnp.concatenate([d_g, d_u], axis=-1),)
    return _rowwise(name, fn, [gu, d_act], [], [(2 * Fh, BF16)])[0]


def loss_head(name, y, target):
    D = y.shape[1]

    def fn(yv, tv):
        e = yv - tv
        tot = jnp.sum(_colsum(e * e), axis=1, keepdims=True) * (0.5 / D)
        return e * (1.0 / D), jnp.broadcast_to(tot, (1, 128))
    return _rowwise(name, fn, [y, target], [], [(D, F32)], [(1, 128)])


def ple_fwd(name, h, pe, z, g_e):
    def fn(hv, pv, zv, gv):
        return (hv + (pv * _rstd(pv)) * gv * _sigmoid(zv),)
    return _rowwise(name, fn, [h, pe, z], [g_e], [(h.shape[1], F32)], tm=128)[0]


def ple_bwd(name, d_h, pe, z, g_e):
    D = d_h.shape[1]

    def fn(dh, pv, zv, gv):
        gate = _sigmoid(zv)
        e = (pv * _rstd(pv)) * gv
        d_z = dh * e * gate * (1.0 - gate)
        d_pe, dgr = _rms_bwd(pv, dh * gate, gv)
        return d_z, d_pe, _colsum(dgr)
    return _rowwise(name, fn, [d_h, pe, z], [g_e], [(D, BF16), (D, BF16)], [(1, D)], tm=128)


def lat_norm_fwd(name, lat, g_q, g_kv):
    QL, KL = g_q.shape[1], g_kv.shape[1]

    def fn(v, gq, gk):
        a = v[:, :QL]
        b = v[:, QL:QL + KL]
        return (a * _rstd(a)) * gq, (b * _rstd(b)) * gk
    return _rowwise(name, fn, [lat], [g_q, g_kv], [(QL, BF16), (KL, BF16)])


def lat_norm_bwd(name, lat, d_cq, d_ckv, d_krope, g_q, g_kv):
    QL, KL = g_q.shape[1], g_kv.shape[1]

    def fn(v, dq, dk, dr, gq, gk):
        da, ga = _rms_bwd(v[:, :QL], dq, gq)
        db, gb = _rms_bwd(v[:, QL:QL + KL], dk, gk)
        return jnp.concatenate([da, db, dr], axis=-1), _colsum(ga), _colsum(gb)
    return _rowwise(name, fn, [lat, d_cq, d_ckv, d_krope], [g_q, g_kv],
                    [(lat.shape[1], BF16)], [(1, QL), (1, KL)])


def glu_fwd(name, ag):
    C = ag.shape[1] // 2

    def fn(v):
        return (v[:, :C] * _sigmoid(v[:, C:]),)
    return _rowwise(name, fn, [ag], [], [(C, F32)], tm=128)[0]


def glu_bwd(name, ag, d_u):
    C = ag.shape[1] // 2

    def fn(v, du):
        a = v[:, :C]
        s = _sigmoid(v[:, C:])
        d = jnp.concatenate([du * s, du * a * s * (1.0 - s)], axis=-1)
        return d, _colsum(d)
    return _rowwise(name, fn, [ag, d_u], [], [(2 * C, BF16)], [(1, 2 * C)], tm=128)


def ln_silu_fwd(name, yc, g, b):
    def fn(v, gv, bv):
        xc = v - jnp.mean(v, axis=-1, keepdims=True)
        ln = xc * lax.rsqrt(jnp.mean(xc * xc, axis=-1, keepdims=True) + EPS) * gv + bv
        return (ln * _sigmoid(ln),)
    return _rowwise(name, fn, [yc], [g, b], [(yc.shape[1], BF16)], tm=128)[0]


def ln_silu_bwd(name, yc, d_out, g, b):
    C = yc.shape[1]

    def fn(v, do, gv, bv):
        xc = v - jnp.mean(v, axis=-1, keepdims=True)
        r = lax.rsqrt(jnp.mean(xc * xc, axis=-1, keepdims=True) + EPS)
        xh = xc * r
        ln = xh * gv + bv
        s = _sigmoid(ln)
        d_ln = do * s * (1.0 + ln * (1.0 - s))
        dxh = d_ln * gv
        dy = r * (dxh - jnp.mean(dxh, axis=-1, keepdims=True) - xh * jnp.mean(dxh * xh, axis=-1, keepdims=True))
        return dy, _colsum(d_ln * xh), _colsum(d_ln), _colsum(dy)
    return _rowwise(name, fn, [yc, d_out], [g, b], [(C, F32)], [(1, C), (1, C), (1, C)], tm=128)


def add_halves(name, core, own, got):
    S, Rh, C = got.shape
    tr = _tile(Rh, 256, 16)
    nrb = Rh // tr

    def body(c_ref, a_ref, b_ref, o_ref):
        o_ref[...] = (a_ref[...].astype(F32) + b_ref[...].astype(F32)).astype(o_ref.dtype)

    gs = pltpu.PrefetchScalarGridSpec(
        num_scalar_prefetch=1, grid=(S, nrb),
        in_specs=[pl.BlockSpec((None, tr, C), lambda s, r, c: (s, c[0] * nrb + r, 0)),
                  pl.BlockSpec((None, tr, C), lambda s, r, c: (s, r, 0))],
        out_specs=pl.BlockSpec((None, tr, C), lambda s, r, c: (s, r, 0)))
    return pl.pallas_call(
        body, name=name, grid_spec=gs, out_shape=jax.ShapeDtypeStruct((S, Rh, C), BF16),
        compiler_params=_cparams(("arbitrary", "arbitrary")))(core, own, got)


def sum_chips(name, parts, core, rows):
    S, Rh, C = parts.shape
    tr = _tile(Rh, 256, 16)
    nrb = Rh // tr

    def body(c_ref, p_ref, o_ref):
        acc = p_ref[0].astype(F32)
        for j in range(1, S):
            acc = acc + p_ref[j].astype(F32)
        o_ref[...] = acc

    gs = pltpu.PrefetchScalarGridSpec(
        num_scalar_prefetch=1, grid=(nrb,),
        in_specs=[pl.BlockSpec((S, tr, C), lambda r, c: (0, r, 0))],
        out_specs=pl.BlockSpec((tr, C), lambda r, c: (c[0] * nrb + r, 0)))
    return pl.pallas_call(
        body, name=name, grid_spec=gs, out_shape=jax.ShapeDtypeStruct((rows, C), F32),
        compiler_params=_cparams(("arbitrary",)))(core, parts)


def cast_into_slot(name, w, chip, layer, layers):
    R, C = w.shape[0] // layers, w.shape[1]
    tr = _tile(R, 256, 16)
    nrb = R // tr

    def body(s_ref, w_ref, o_ref):
        o_ref[...] = w_ref[...].astype(BF16)

    gs = pltpu.PrefetchScalarGridSpec(
        num_scalar_prefetch=1, grid=(nrb,),
        in_specs=[pl.BlockSpec((tr, C), lambda r, s: (layer * nrb + r, 0))],
        out_specs=pl.BlockSpec((None, tr, C), lambda r, s: (s[0], r, 0)))
    return pl.pallas_call(
        body, name=name, grid_spec=gs, out_shape=jax.ShapeDtypeStruct((N_CHIPS, R, C), BF16),
        compiler_params=_cparams(("arbitrary",)))(chip, w)


def adamw(name, w, g, m, v):
    R, C = w.shape
    tr = _tile(R, 256, 8)
    c1 = 1.0 / (1.0 - ADAM_B1 ** ADAM_STEP)
    c2 = 1.0 / (1.0 - ADAM_B2 ** ADAM_STEP)

    def body(w_ref, g_ref, m_ref, v_ref, d_ref, nm_ref, nv_ref):
        gv = g_ref[...]
        nm = ADAM_B1 * m_ref[...] + (1.0 - ADAM_B1) * gv
        nv = ADAM_B2 * v_ref[...] + (1.0 - ADAM_B2) * (gv * gv)
        d_ref[...] = -ADAM_LR * ((nm * c1) / (jnp.sqrt(nv * c2) + ADAM_EPS) + ADAM_WD * w_ref[...])
        nm_ref[...] = nm
        nv_ref[...] = nv

    spec = pl.BlockSpec((tr, C), lambda r: (r, 0))
    return pl.pallas_call(
        body, name=name, grid=(R // tr,), in_specs=[spec] * 4, out_specs=[spec] * 3,
        out_shape=[jax.ShapeDtypeStruct((R, C), F32)] * 3, compiler_params=_cparams(("arbitrary",)))(w, g, m, v)


def _matmul(name, a, b, *, grid, a_blk, a_map, b_blk, b_map, o_shape, o_dtype, o_blk, o_map, dims,
            scale=None, res=None, bias=None, bias_blk=None, bias_map=None, alias_into=None, after=None):
    nk = grid[2]
    has_res, has_bias, has_into = res is not None, bias is not None, alias_into is not None
    acc_shape = tuple(d for d in o_blk if d is not None)

    def body(*refs):
        a_ref, b_ref = refs[0], refs[1]
        pos = 2
        res_ref = bias_ref = None
        if has_res:
            res_ref = refs[pos]
            pos += 1
        if has_bias:
            bias_ref = refs[pos]
            pos += 1
        if has_into:
            pos += 1
        if after is not None:
            pos += 1
        o_ref = refs[pos]
        av, bv = a_ref[...], b_ref[...]
        if av.dtype != BF16:
            av = av.astype(BF16)
        if bv.dtype != BF16:
            bv = bv.astype(BF16)
        part = lax.dot_general(av, bv, dims, preferred_element_type=F32)

        def finish(acc):
            if scale is not None:
                acc = acc * scale
            if has_bias:
                acc = acc + bias_ref[...]
            if has_res:
                acc = acc + res_ref[...]
            o_ref[...] = acc.astype(o_ref.dtype)

        if nk == 1:
            finish(part)
        else:
            acc_ref = refs[pos + 1]
            k = pl.program_id(2)

            @pl.when(k == 0)
            def _():
                acc_ref[...] = part

            @pl.when(k > 0)
            def _():
                acc_ref[...] += part

            @pl.when(k == nk - 1)
            def _():
                finish(acc_ref[...])

    operands = [a, b]
    in_specs = [pl.BlockSpec(a_blk, a_map), pl.BlockSpec(b_blk, b_map)]
    if has_res:
        operands.append(res)
        in_specs.append(pl.BlockSpec(o_blk, o_map))
    if has_bias:
        operands.append(bias)
        in_specs.append(pl.BlockSpec(bias_blk, bias_map))
    aliases = {}
    if has_into:
        aliases = {len(operands): 0}
        operands.append(alias_into)
        in_specs.append(pl.BlockSpec(memory_space=pl.ANY))
    if after is not None:
        operands.append(after)
        in_specs.append(pl.BlockSpec(memory_space=pl.ANY))
    return pl.pallas_call(
        body, name=name, grid=grid, in_specs=in_specs, out_specs=pl.BlockSpec(o_blk, o_map),
        out_shape=jax.ShapeDtypeStruct(o_shape, o_dtype),
        scratch_shapes=[pltpu.VMEM(acc_shape, F32)] if nk > 1 else [],
        input_output_aliases=aliases,
        compiler_params=_cparams(("parallel", "parallel", "arbitrary")),
    )(*operands)


def mm_nn(name, a, w3, kind, layer, *, out_dtype, scale=None, res=None, bias=None, tm=1024, tn=512, tk=2048,
          after=None):
    M, K = a.shape
    S, _, C = w3.shape
    tm = _tile(M, tm, 16)
    if kind == "col":
        N = S * C
        tk, tn = _tile(K, tk), _tile(C, tn)
        kb, nb = K // tk, C // tn
        b_map = lambda n, m, k: (n // nb, layer * kb + k, n % nb)
    else:
        N, K4 = C, K // S
        tk, tn = _tile(K4, tk), _tile(C, tn)
        kb4 = K4 // tk
        b_map = lambda n, m, k: (k // kb4, layer * kb4 + k % kb4, n)
    return _matmul(
        name, a, w3, grid=(N // tn, M // tm, K // tk),
        a_blk=(tm, tk), a_map=lambda n, m, k: (m, k), b_blk=(None, tk, tn), b_map=b_map,
        o_shape=(M, N), o_dtype=out_dtype, o_blk=(tm, tn), o_map=lambda n, m, k: (m, n), dims=NN_DIMS,
        scale=scale, res=res, bias=bias, bias_blk=(1, tn), bias_map=lambda n, m, k: (0, n), after=after)


def _mm_nt(name, g, w3, kind, layer, K, *, out_dtype, scale=None, tm=1024, to=1024, tc=1408):
    M, N = g.shape
    S, _, C = w3.shape
    tm = _tile(M, tm, 16)
    tc = _tile(C, tc)
    if kind == "col":
        nb = C // tc
        to = _tile(K, to)
        ob = K // to
        b_map = lambda o, m, c: (c // nb, layer * ob + o, c % nb)
    else:
        K4 = K // S
        to = _tile(K4, to)
        ob4 = K4 // to
        b_map = lambda o, m, c: (o // ob4, layer * ob4 + o % ob4, c)
    return _matmul(
        name, g, w3, grid=(K // to, M // tm, N // tc),
        a_blk=(tm, tc), a_map=lambda o, m, c: (m, c), b_blk=(None, to, tc), b_map=b_map,
        o_shape=(M, K), o_dtype=out_dtype, o_blk=(tm, to), o_map=lambda o, m, c: (m, o), dims=NT_DIMS,
        scale=scale)


def mm_tn(name, a, g, kind, *, shards=N_CHIPS, layer=0, layers=1, into=None, out_dtype=BF16, scale=None,
          tk=1024, tn=1408, tm=2048):
    M, K = a.shape
    N = g.shape[1]
    tm = _tile(M, tm, 16)
    if kind == "col":
        C = N // shards
        tk, tn = _tile(K, tk), _tile(C, tn)
        kb, nb = K // tk, C // tn
        o_shape = (shards, layers * K, C)
        o_map = lambda k, n, m: (n // nb, layer * kb + k, n % nb)
    else:
        K4 = K // shards
        tk, tn = _tile(K4, tk), _tile(N, tn)
        kb4 = K4 // tk
        o_shape = (shards, layers * K4, N)
        o_map = lambda k, n, m: (k // kb4, layer * kb4 + k % kb4, n)
    return _matmul(
        name, a, g, grid=(K // tk, N // tn, M // tm),
        a_blk=(tm, tk), a_map=lambda k, n, m: (m, k), b_blk=(tm, tn), b_map=lambda k, n, m: (m, n),
        o_shape=o_shape, o_dtype=out_dtype, o_blk=(None, tk, tn), o_map=o_map, dims=TN_DIMS,
        scale=scale, alias_into=into)


def rope_tables(name, pos):
    T = pos.shape[0]
    half = D_ROPE // 2

    def body(p_ref, c_ref, s1_ref, s2_ref):
        lane = lax.broadcasted_iota(jnp.int32, (T, 128), 1)
        idx = (lane & (half - 1)).astype(F32)
        ang = p_ref[...] * jnp.exp(idx * (-2.0 * math.log(ROPE_THETA) / D_ROPE))
        cs, sn = jnp.cos(ang), jnp.sin(ang)
        c_ref[...] = jnp.where(lane < D_ROPE, cs, 0.0)
        s1_ref[...] = jnp.where(lane < half, -sn, 0.0)
        s2_ref[...] = jnp.where((lane >= half) & (lane < D_ROPE), sn, 0.0)

    return pl.pallas_call(body, name=name, out_shape=[jax.ShapeDtypeStruct((T, 128), F32)] * 3,
                          compiler_params=_cparams())(pos)


def _rope(v, cs, s1, s2):
    return v * cs + pltpu.roll(v, 128 - D_ROPE // 2, 1) * s1 + pltpu.roll(v, D_ROPE // 2, 1) * s2


def _rope_bwd(d, cs, s1, s2):
    return d * cs + pltpu.roll(d * s1, D_ROPE // 2, 1) + pltpu.roll(d * s2, 128 - D_ROPE // 2, 1)


def _head_rstd(n, r):
    ms = (jnp.sum(n * n, axis=-1, keepdims=True) + jnp.sum(r * r, axis=-1, keepdims=True)) * (1.0 / QK_DIM)
    return lax.rsqrt(ms + EPS)


def mla_prep_fwd(name, q_raw, kv_raw, lat, tabs, q_gain, k_gain, rope_col, tm=128):
    T = q_raw.shape[0]
    H = q_raw.shape[1] // HEAD_PAD
    tm = min(tm, T)
    rope_blk = rope_col // 128

    def body(q_ref, kv_ref, kr_ref, c_ref, s1_ref, s2_ref, qg_ref, kg_ref, Q_ref, K_ref, V_ref):
        cs, s1, s2 = c_ref[...], s1_ref[...], s2_ref[...]
        qg, kg = qg_ref[...], kg_ref[...]
        kr = kr_ref[...]
        for h in range(H):
            lo = HEAD_PAD * h
            n, r = q_ref[:, lo:lo + 128], q_ref[:, lo + 128:lo + 256]
            rs = _head_rstd(n, r)
            Q_ref[h, :, 0:128] = (n * rs * qg[:, :128]).astype(BF16)
            Q_ref[h, :, 128:256] = _rope(r * rs * qg[:, 128:], cs, s1, s2).astype(BF16)
            n = kv_ref[:, lo:lo + 128]
            rs = _head_rstd(n, kr)
            K_ref[h, :, 0:128] = (n * rs * kg[:, :128]).astype(BF16)
            K_ref[h, :, 128:256] = _rope(kr * rs * kg[:, 128:], cs, s1, s2).astype(BF16)
            V_ref[h] = kv_ref[:, lo + 128:lo + 256].astype(BF16)

    row = lambda w: pl.BlockSpec((tm, w), lambda i: (i, 0))
    vec = pl.BlockSpec((1, HEAD_PAD), lambda i: (0, 0))
    return pl.pallas_call(
        body, name=name, grid=(T // tm,),
        in_specs=[row(H * HEAD_PAD), row(H * HEAD_PAD), pl.BlockSpec((tm, 128), lambda i: (i, rope_blk)),
                  row(128), row(128), row(128), vec, vec],
        out_specs=[pl.BlockSpec((H, tm, HEAD_PAD), lambda i: (0, i, 0))] * 2 + [pl.BlockSpec((H, tm, D_V), lambda i: (0, i, 0))],
        out_shape=[jax.ShapeDtypeStruct((H, T, HEAD_PAD), BF16)] * 2 + [jax.ShapeDtypeStruct((H, T, D_V), BF16)],
        compiler_params=_cparams(("arbitrary",)),
    )(q_raw, kv_raw, lat, *tabs, q_gain, k_gain)


def mla_prep_bwd(name, dQ, dK, dV, q_raw, kv_raw, lat, tabs, q_gain, k_gain, rope_col, tm=128):
    T = q_raw.shape[0]
    H = q_raw.shape[1] // HEAD_PAD
    tm = min(tm, T)
    rope_blk = rope_col // 128

    def body(dQ_ref, dK_ref, dV_ref, q_ref, kv_ref, kr_ref, c_ref, s1_ref, s2_ref, qg_ref, kg_ref,
             dq_ref, dkv_ref, dkr_ref, dqg_ref, dkg_ref):
        @pl.when(pl.program_id(0) == 0)
        def _():
            dqg_ref[...] = jnp.zeros_like(dqg_ref)
            dkg_ref[...] = jnp.zeros_like(dkg_ref)

        cs, s1, s2 = c_ref[...], s1_ref[...], s2_ref[...]
        qg, kg = qg_ref[...], kg_ref[...]
        kr = kr_ref[...]
        dkr = jnp.zeros((tm, 128), F32)
        gq_n = jnp.zeros((1, 128), F32)
        gq_r = jnp.zeros((1, 128), F32)
        gk_n = jnp.zeros((1, 128), F32)
        gk_r = jnp.zeros((1, 128), F32)

        def norm_bwd(n, r, dn, dr, gain):
            rs = _head_rstd(n, r)
            nh, rh = n * rs, r * rs
            dng, drg = dn * gain[:, :128], dr * gain[:, 128:]
            mean = (jnp.sum(dng * nh, axis=-1, keepdims=True) + jnp.sum(drg * rh, axis=-1, keepdims=True)) * (1.0 / QK_DIM)
            return rs * (dng - nh * mean), rs * (drg - rh * mean), _colsum(dn * nh), _colsum(dr * rh)

        for h in range(H):
            lo = HEAD_PAD * h
            n, r = q_ref[:, lo:lo + 128], q_ref[:, lo + 128:lo + 256]
            dn = dQ_ref[h, :, 0:128].astype(F32)
            dr = _rope_bwd(dQ_ref[h, :, 128:256].astype(F32), cs, s1, s2)
            a, b, g1, g2 = norm_bwd(n, r, dn, dr, qg)
            dq_ref[:, lo:lo + 128] = a.astype(BF16)
            dq_ref[:, lo + 128:lo + 256] = b.astype(BF16)
            gq_n, gq_r = gq_n + g1, gq_r + g2
            n = kv_ref[:, lo:lo + 128]
            dn = dK_ref[h, :, 0:128].astype(F32)
            dr = _rope_bwd(dK_ref[h, :, 128:256].astype(F32), cs, s1, s2)
            a, b, g1, g2 = norm_bwd(n, kr, dn, dr, kg)
            dkv_ref[:, lo:lo + 128] = a.astype(BF16)
            dkv_ref[:, lo + 128:lo + 256] = dV_ref[h].astype(BF16)
            dkr = dkr + b
            gk_n, gk_r = gk_n + g1, gk_r + g2
        dkr_ref[...] = dkr
        dqg_ref[:, 0:128] += gq_n
        dqg_ref[:, 128:256] += gq_r
        dkg_ref[:, 0:128] += gk_n
        dkg_ref[:, 128:256] += gk_r

    row = lambda w: pl.BlockSpec((tm, w), lambda i: (i, 0))
    vec = pl.BlockSpec((1, HEAD_PAD), lambda i: (0, 0))
    hd = lambda w: pl.BlockSpec((H, tm, w), lambda i: (0, i, 0))
    return pl.pallas_call(
        body, name=name, grid=(T // tm,),
        in_specs=[hd(HEAD_PAD), hd(HEAD_PAD), hd(D_V), row(H * HEAD_PAD), row(H * HEAD_PAD),
                  pl.BlockSpec((tm, 128), lambda i: (i, rope_blk)), row(128), row(128), row(128), vec, vec],
        out_specs=[row(H * HEAD_PAD), row(H * HEAD_PAD), row(128), vec, vec],
        out_shape=[jax.ShapeDtypeStruct((T, H * HEAD_PAD), BF16)] * 2 + [jax.ShapeDtypeStruct((T, 128), F32)]
        + [jax.ShapeDtypeStruct((1, HEAD_PAD), F32)] * 2,
        compiler_params=_cparams(("arbitrary",)),
    )(dQ, dK, dV, q_raw, kv_raw, lat, *tabs, q_gain, k_gain)


def _causal_probs(q, k, scale, row0):
    s = lax.dot_general(q, k, NT_DIMS, preferred_element_type=F32) * scale
    row = row0 + lax.broadcasted_iota(jnp.int32, s.shape, 0)
    col = lax.broadcasted_iota(jnp.int32, s.shape, 1)
    s = jnp.where(col <= row, s, NEG_BIG)
    p = jnp.exp(s - jnp.max(s, axis=-1, keepdims=True))
    return p, jnp.sum(p, axis=-1, keepdims=True)


def attn_fwd(name, Q, K, V, tq=512):
    H, T, E = Q.shape
    tq = min(tq, T)
    nq = T // tq
    scale = QK_DIM ** -0.5

    def body(q_ref, k_ref, v_ref, o_ref):
        i = pl.program_id(1)
        for ib in range(nq):
            @pl.when(i == ib)
            def _():
                n = (ib + 1) * tq
                p, l = _causal_probs(q_ref[...], k_ref[0:n, :], scale, ib * tq)
                o = jnp.dot(p.astype(BF16), v_ref[0:n, :], preferred_element_type=F32)
                o_ref[...] = (o / l).astype(o_ref.dtype)

    return pl.pallas_call(
        body, name=name, grid=(H, nq),
        in_specs=[pl.BlockSpec((None, tq, E), lambda h, i: (h, i, 0)),
                  pl.BlockSpec((None, T, E), lambda h, i: (h, 0, 0)),
                  pl.BlockSpec((None, T, D_V), lambda h, i: (h, 0, 0))],
        out_specs=pl.BlockSpec((tq, D_V), lambda h, i: (i, h)),
        out_shape=jax.ShapeDtypeStruct((T, H * D_V), BF16),
        compiler_params=_cparams(("parallel", "arbitrary")),
    )(Q, K, V)


def attn_bwd(name, Q, K, V, dO, tq=512):
    H, T, E = Q.shape
    tq = min(tq, T)
    nq = T // tq
    scale = QK_DIM ** -0.5

    def body(q_ref, k_ref, v_ref, do_ref, dq_ref, dk_ref, dv_ref):
        i = pl.program_id(1)

        @pl.when(i == 0)
        def _():
            dk_ref[...] = jnp.zeros_like(dk_ref)
            dv_ref[...] = jnp.zeros_like(dv_ref)

        for ib in range(nq):
            @pl.when(i == ib)
            def _():
                n = (ib + 1) * tq
                q, k, v, do = q_ref[...], k_ref[0:n, :], v_ref[0:n, :], do_ref[...]
                p, l = _causal_probs(q, k, scale, ib * tq)
                p = p / l
                dp = lax.dot_general(do, v, NT_DIMS, preferred_element_type=F32)
                ds = p * (dp - jnp.sum(p * dp, axis=-1, keepdims=True)) * scale
                dsb, pb = ds.astype(BF16), p.astype(BF16)
                dq_ref[...] = jnp.dot(dsb, k, preferred_element_type=F32)
                dk_ref[0:n, :] += lax.dot_general(dsb, q, TN_DIMS, preferred_element_type=F32)
                dv_ref[0:n, :] += lax.dot_general(pb, do, TN_DIMS, preferred_element_type=F32)

    return pl.pallas_call(
        body, name=name, grid=(H, nq),
        in_specs=[pl.BlockSpec((None, tq, E), lambda h, i: (h, i, 0)),
                  pl.BlockSpec((None, T, E), lambda h, i: (h, 0, 0)),
                  pl.BlockSpec((None, T, D_V), lambda h, i: (h, 0, 0)),
                  pl.BlockSpec((tq, D_V), lambda h, i: (i, h))],
        out_specs=[pl.BlockSpec((None, tq, E), lambda h, i: (h, i, 0)),
                   pl.BlockSpec((None, T, E), lambda h, i: (h, 0, 0)),
                   pl.BlockSpec((None, T, D_V), lambda h, i: (h, 0, 0))],
        out_shape=[jax.ShapeDtypeStruct((H, T, E), F32)] * 2 + [jax.ShapeDtypeStruct((H, T, D_V), F32)],
        compiler_params=_cparams(("parallel", "arbitrary")),
    )(Q, K, V, dO)


def _dw_specs(T, C, tm, tc, halo):
    cur = pl.BlockSpec((tm, tc), lambda j, i: (i, j))
    last = T // tm - 1
    if halo == "prev":
        nbr = pl.BlockSpec((tm, tc), lambda j, i: (jnp.maximum(i - 1, 0), j))
    else:
        nbr = pl.BlockSpec((tm, tc), lambda j, i: (jnp.minimum(i + 1, last), j))
    return cur, nbr


def dwconv_fwd(name, u, w, b, tm=256, tc=512, rs=32):
    T, C = u.shape
    tm, tc = min(tm, T), min(tc, C)
    cur, prev = _dw_specs(T, C, tm, tc, "prev")

    def body(up_ref, uc_ref, w_ref, b_ref, o_ref, scr):
        i = pl.program_id(1)

        @pl.when(i == 0)
        def _():
            scr[pl.ds(0, tm), :] = jnp.zeros((tm, tc), F32)

        @pl.when(i > 0)
        def _():
            scr[pl.ds(0, tm), :] = up_ref[...]

        scr[pl.ds(tm, tm), :] = uc_ref[...]
        for s in range(tm // rs):
            acc = jnp.broadcast_to(b_ref[...], (rs, tc))
            for k in range(CONV_WIDTH):
                acc = acc + w_ref[pl.ds(k, 1), :] * scr[pl.ds(tm - (CONV_WIDTH - 1) + k + rs * s, rs), :]
            o_ref[pl.ds(rs * s, rs), :] = acc

    return pl.pallas_call(
        body, name=name, grid=(C // tc, T // tm),
        in_specs=[prev, cur, pl.BlockSpec((CONV_TAPS_PAD, tc), lambda j, i: (0, j)), pl.BlockSpec((1, tc), lambda j, i: (0, j))],
        out_specs=cur, out_shape=jax.ShapeDtypeStruct((T, C), F32),
        scratch_shapes=[pltpu.VMEM((2 * tm, tc), F32)], compiler_params=_cparams(("parallel", "arbitrary")),
    )(u, u, w, b)


def dwconv_bwd_u(name, dy, w, tm=256, tc=512, rs=32):
    T, C = dy.shape
    tm, tc = min(tm, T), min(tc, C)
    cur, nxt = _dw_specs(T, C, tm, tc, "next")
    last = T // tm - 1

    def body(dc_ref, dn_ref, w_ref, o_ref, scr):
        i = pl.program_id(1)
        scr[pl.ds(0, tm), :] = dc_ref[...]

        @pl.when(i == last)
        def _():
            scr[pl.ds(tm, tm), :] = jnp.zeros((tm, tc), F32)

        @pl.when(i < last)
        def _():
            scr[pl.ds(tm, tm), :] = dn_ref[...]

        for s in range(tm // rs):
            acc = jnp.zeros((rs, tc), F32)
            for k in range(CONV_WIDTH):
                acc = acc + w_ref[pl.ds(k, 1), :] * scr[pl.ds((CONV_WIDTH - 1) - k + rs * s, rs), :]
            o_ref[pl.ds(rs * s, rs), :] = acc

    return pl.pallas_call(
        body, name=name, grid=(C // tc, T // tm),
        in_specs=[cur, nxt, pl.BlockSpec((CONV_TAPS_PAD, tc), lambda j, i: (0, j))],
        out_specs=cur, out_shape=jax.ShapeDtypeStruct((T, C), F32),
        scratch_shapes=[pltpu.VMEM((2 * tm, tc), F32)], compiler_params=_cparams(("parallel", "arbitrary")),
    )(dy, dy, w)


def dwconv_bwd_w(name, u, dy, tm=256, tc=512, rs=32):
    T, C = u.shape
    tm, tc = min(tm, T), min(tc, C)
    cur, prev = _dw_specs(T, C, tm, tc, "prev")

    def body(up_ref, uc_ref, dy_ref, o_ref, scr):
        i = pl.program_id(1)

        @pl.when(i == 0)
        def _():
            scr[pl.ds(0, tm), :] = jnp.zeros((tm, tc), F32)
            o_ref[...] = jnp.zeros_like(o_ref)

        @pl.when(i > 0)
        def _():
            scr[pl.ds(0, tm), :] = up_ref[...]

        scr[pl.ds(tm, tm), :] = uc_ref[...]
        for k in range(CONV_WIDTH):
            acc = jnp.zeros((rs, tc), F32)
            for s in range(tm // rs):
                acc = acc + dy_ref[pl.ds(rs * s, rs), :] * scr[pl.ds(tm - (CONV_WIDTH - 1) + k + rs * s, rs), :]
            o_ref[pl.ds(k, 1), :] += _colsum(acc)

    return pl.pallas_call(
        body, name=name, grid=(C // tc, T // tm),
        in_specs=[prev, cur, cur],
        out_specs=pl.BlockSpec((CONV_TAPS_PAD, tc), lambda j, i: (0, j)),
        out_shape=jax.ShapeDtypeStruct((CONV_TAPS_PAD, C), F32),
        scratch_shapes=[pltpu.VMEM((2 * tm, tc), F32)], compiler_params=_cparams(("parallel", "arbitrary")),
    )(u, u, dy)


def _place():
    x, y, c = lax.axis_index("x"), lax.axis_index("y"), lax.axis_index("c")
    return x, y, c


def _other_chips(x, y):
    return [(1 - x, y, 2 * (1 - x) + y), (x, 1 - y, 2 * x + (1 - y)), (1 - x, 1 - y, 2 * (1 - x) + (1 - y))]


def _hbm_specs(n):
    return [pl.BlockSpec(memory_space=pl.ANY)] * n


HBM_SPEC = pl.BlockSpec(memory_space=pltpu.HBM)
SEM_SPEC = pl.BlockSpec(memory_space=pltpu.SEMAPHORE)
ANY_SPEC = pl.BlockSpec(memory_space=pl.ANY)
SIDE_EFFECT = pltpu.SideEffectType.DATAFLOW_SIDE_EFFECTING


def _half(ref, slot, which):
    rh = ref.shape[1] // 2
    return ref.at[slot, pl.ds(pl.multiple_of(which * rh, 16), rh), :]


def _hbm(a):
    return pltpu.with_memory_space_constraint(a, pltpu.HBM)


def gather_start(name, groups):
    flat = [b for g in groups for b in g]
    n, ng = len(flat), len(groups)

    def body(*refs):
        send, recv, out = refs[n:n + ng], refs[n + ng:n + 2 * ng], refs[n + 2 * ng:]
        x, y, c = _place()
        me = 2 * x + y
        a = 0
        for g, grp in enumerate(groups):
            for k in range(len(grp)):
                piece = _half(out[a], me, c)
                for j, (px, py, _) in enumerate(_other_chips(x, y)):
                    pltpu.make_async_remote_copy(
                        src_ref=piece, dst_ref=piece, send_sem=send[g].at[3 * k + j], recv_sem=recv[g].at[3 * k + j],
                        device_id=(px, py, c), device_id_type=MESH).start()
                a += 1

    sems = [pltpu.SemaphoreType.DMA((3 * len(g),)) for g in groups]
    res = pl.pallas_call(
        body, name=name, in_specs=[HBM_SPEC] * n, out_specs=[SEM_SPEC] * (2 * ng) + [HBM_SPEC] * n,
        out_shape=sems + sems + [pltpu.HBM(b.shape, b.dtype) for b in flat],
        input_output_aliases={a: 2 * ng + a for a in range(n)},
        compiler_params=pltpu.CompilerParams(has_side_effects=SIDE_EFFECT),
    )(*[_hbm(b) for b in flat])
    send, recv, bufs = res[:ng], res[ng:2 * ng], list(res[2 * ng:])
    out, a = [], 0
    for g, grp in enumerate(groups):
        out.append((send[g], recv[g], bufs[a:a + len(grp)]))
        a += len(grp)
    return out


def gather_relay(name, started, after):
    send1, recv1, bufs = started
    n = len(bufs)

    def body(*refs):
        s1, r1 = refs[n], refs[n + 1]
        s2, r2, out, token = refs[n + 3], refs[n + 4], refs[n + 5:2 * n + 5], refs[2 * n + 5]
        x, y, c = _place()
        me = 2 * x + y
        chips = _other_chips(x, y)
        for k in range(n):
            for j, (px, py, idx) in enumerate(chips):
                cp = pltpu.make_async_remote_copy(
                    src_ref=_half(out[k], me, c), dst_ref=_half(out[k], idx, c), send_sem=s1.at[3 * k + j],
                    recv_sem=r1.at[3 * k + j], device_id=(px, py, c), device_id_type=MESH)
                cp.wait_send()
                cp.wait_recv()
        for k in range(n):
            for j, (px, py, idx) in enumerate(chips):
                piece = _half(out[k], idx, c)
                pltpu.make_async_remote_copy(
                    src_ref=piece, dst_ref=piece, send_sem=s2.at[3 * k + j], recv_sem=r2.at[3 * k + j],
                    device_id=(x, y, 1 - c), device_id_type=MESH).start()
        token[...] = jnp.zeros_like(token)

    sem = pltpu.SemaphoreType.DMA((3 * n,))
    res = pl.pallas_call(
        body, name=name, in_specs=[HBM_SPEC] * n + [SEM_SPEC, SEM_SPEC, ANY_SPEC],
        out_specs=[SEM_SPEC, SEM_SPEC] + [HBM_SPEC] * n + [pl.BlockSpec(memory_space=pltpu.VMEM)],
        out_shape=[sem, sem] + [pltpu.HBM(b.shape, b.dtype) for b in bufs] + [jax.ShapeDtypeStruct((8, 128), F32)],
        input_output_aliases={a: 2 + a for a in range(n)},
        compiler_params=pltpu.CompilerParams(has_side_effects=SIDE_EFFECT),
    )(*bufs, send1, recv1, after)
    return res[0], res[1], list(res[2:2 + n]), res[2 + n]


def gather_wait(name, relayed, after):
    send2, recv2, bufs, _ = relayed
    n = len(bufs)

    def body(*refs):
        s2, r2, out = refs[n], refs[n + 1], refs[n + 3:]
        x, y, c = _place()
        for k in range(n):
            for j, (px, py, idx) in enumerate(_other_chips(x, y)):
                cp = pltpu.make_async_remote_copy(
                    src_ref=_half(out[k], idx, c), dst_ref=_half(out[k], idx, 1 - c), send_sem=s2.at[3 * k + j],
                    recv_sem=r2.at[3 * k + j], device_id=(x, y, 1 - c), device_id_type=MESH)
                cp.wait_send()
                cp.wait_recv()

    res = pl.pallas_call(
        body, name=name, in_specs=[HBM_SPEC] * n + [SEM_SPEC, SEM_SPEC, ANY_SPEC], out_specs=[HBM_SPEC] * n,
        out_shape=[pltpu.HBM(b.shape, b.dtype) for b in bufs], input_output_aliases={a: a for a in range(n)},
        compiler_params=pltpu.CompilerParams(has_side_effects=SIDE_EFFECT),
    )(*bufs, send2, recv2, after)
    return list(res)


def swap_grad_halves(name, parts):
    n = len(parts)

    def body(*refs):
        src, dst = refs[:n], refs[n:2 * n]
        send_sems, recv_sems = refs[2 * n], refs[2 * n + 1]
        x, y, c = _place()
        cps = []
        for a in range(n):
            rh = src[a].shape[1] // 2
            cp = pltpu.make_async_remote_copy(
                src_ref=src[a].at[:, pl.ds(pl.multiple_of((1 - c) * rh, 16), rh), :], dst_ref=dst[a],
                send_sem=send_sems.at[a], recv_sem=recv_sems.at[a], device_id=(x, y, 1 - c), device_id_type=MESH)
            cp.start()
            cps.append(cp)
        for cp in cps:
            cp.wait()

    return pl.pallas_call(
        body, name=name, in_specs=_hbm_specs(n), out_specs=_hbm_specs(n),
        out_shape=[jax.ShapeDtypeStruct((p.shape[0], p.shape[1] // 2, p.shape[2]), p.dtype) for p in parts],
        scratch_shapes=[pltpu.SemaphoreType.DMA((n,)), pltpu.SemaphoreType.DMA((n,))], compiler_params=_cparams(),
    )(*parts)


def exchange_chip_sums(name, sums):
    n = len(sums)

    def body(*refs):
        src, dst = refs[:n], refs[n:2 * n]
        send_sems, recv_sems, local_sems = refs[2 * n], refs[2 * n + 1], refs[2 * n + 2]
        x, y, c = _place()
        me = 2 * x + y
        chips = _other_chips(x, y)
        cps, locs = [], []
        for a in range(n):
            loc = pltpu.make_async_copy(src[a].at[me], dst[a].at[me], local_sems.at[a])
            loc.start()
            locs.append(loc)
            for j, (px, py, idx) in enumerate(chips):
                cp = pltpu.make_async_remote_copy(
                    src_ref=src[a].at[idx], dst_ref=dst[a].at[me], send_sem=send_sems.at[3 * a + j],
                    recv_sem=recv_sems.at[3 * a + j], device_id=(px, py, c), device_id_type=MESH)
                cp.start()
                cps.append((cp, a, j, idx))
        for cp, a, j, idx in cps:
            pltpu.make_async_remote_copy(
                src_ref=src[a].at[idx], dst_ref=dst[a].at[idx], send_sem=send_sems.at[3 * a + j],
                recv_sem=recv_sems.at[3 * a + j], device_id=(x, y, c), device_id_type=MESH).wait_recv()
        for cp, a, j, idx in cps:
            cp.wait_send()
        for loc in locs:
            loc.wait()

    return pl.pallas_call(
        body, name=name, in_specs=_hbm_specs(n), out_specs=_hbm_specs(n),
        out_shape=[jax.ShapeDtypeStruct(s.shape, s.dtype) for s in sums],
        scratch_shapes=[pltpu.SemaphoreType.DMA((3 * n,)), pltpu.SemaphoreType.DMA((3 * n,)), pltpu.SemaphoreType.DMA((n,))],
        compiler_params=_cparams(),
    )(*sums)


def share_grad_halves(name, grads):
    n = len(grads)

    def body(*refs):
        out = refs[n:2 * n]
        send_sems, recv_sems = refs[2 * n], refs[2 * n + 1]
        x, y, c = _place()
        cps = []
        for a in range(n):
            rh = out[a].shape[0] // 2
            mine = out[a].at[pl.ds(pl.multiple_of(c * rh, 8), rh), :]
            cp = pltpu.make_async_remote_copy(
                src_ref=mine, dst_ref=mine, send_sem=send_sems.at[a], recv_sem=recv_sems.at[a],
                device_id=(x, y, 1 - c), device_id_type=MESH)
            cp.start()
            cps.append(cp)
        for a, cp in enumerate(cps):
            rh = out[a].shape[0] // 2
            theirs = out[a].at[pl.ds(pl.multiple_of((1 - c) * rh, 8), rh), :]
            pltpu.make_async_remote_copy(
                src_ref=theirs, dst_ref=theirs, send_sem=send_sems.at[a], recv_sem=recv_sems.at[a],
                device_id=(x, y, 1 - c), device_id_type=MESH).wait_recv()
            cp.wait_send()

    return pl.pallas_call(
        body, name=name, in_specs=_hbm_specs(n), out_specs=_hbm_specs(n),
        out_shape=[jax.ShapeDtypeStruct(g.shape, g.dtype) for g in grads],
        scratch_shapes=[pltpu.SemaphoreType.DMA((n,)), pltpu.SemaphoreType.DMA((n,))],
        input_output_aliases={a: a for a in range(n)}, compiler_params=_cparams(),
    )(*grads)


def allreduce_pack(name, pack):
    R, W = pack.shape

    def body(p_ref, o_ref, sib, pair, got, send_sems, recv_sems):
        x, y, c = _place()

        def swap(k, src, dst, to):
            cp = pltpu.make_async_remote_copy(src_ref=src, dst_ref=dst, send_sem=send_sems.at[k],
                                              recv_sem=recv_sems.at[k], device_id=to, device_id_type=MESH)
            cp.start()
            return cp

        cp = swap(0, p_ref, sib, (x, y, 1 - c))
        cp.wait()
        pair[...] = p_ref[...] + sib[...]
        cps = [swap(1, pair, got.at[0], (1 - x, y, c)), swap(2, pair, got.at[1], (x, 1 - y, c)),
               swap(3, pair, got.at[2], (1 - x, 1 - y, c))]
        for cp in cps:
            cp.wait()
        o_ref[...] = (pair[...] + got[1]) + (got[0] + got[2])

    return pl.pallas_call(
        body, name=name, out_shape=jax.ShapeDtypeStruct((R, W), F32),
        in_specs=[pl.BlockSpec(memory_space=pltpu.VMEM)], out_specs=pl.BlockSpec(memory_space=pltpu.VMEM),
        scratch_shapes=[pltpu.VMEM((R, W), F32), pltpu.VMEM((R, W), F32), pltpu.VMEM((3, R, W), F32),
                        pltpu.SemaphoreType.DMA((4,)), pltpu.SemaphoreType.DMA((4,))],
        compiler_params=_cparams(),
    )(pack)


BIG_WEIGHTS = [
    ("ffn_a_w_in", "col"), ("ffn_a_w_out", "row"), ("ffn_b_w_in", "col"), ("ffn_b_w_out", "row"),
    ("mla_w_in", "row"), ("mla_w_uq", "col"), ("mla_w_ukv", "col"), ("mla_w_o", "row"),
    ("conv_w_pw1", "col"), ("conv_w_pw2", "row"), ("ple_w_proj", "col"), ("ple_w_gate", "row"),
]
WEIGHT_ORDER = ["ffn_a_norm", "ffn_a_w_in", "ffn_a_w_out", "ffn_b_norm", "ffn_b_w_in", "ffn_b_w_out", "mix_norm",
                "mla_w_in", "mla_q_lat_norm", "mla_kv_lat_norm", "mla_w_uq", "mla_w_ukv", "mla_q_gain", "mla_k_gain",
                "mla_w_o", "conv_w_pw1", "conv_b_pw1", "conv_w_dw", "conv_b_dw", "conv_ln_g", "conv_ln_b", "conv_w_pw2",
                "ple_w_proj", "ple_norm", "ple_gate_norm", "ple_w_gate"]
REPLICATED_SMALL = ["ffn_a_norm", "ffn_b_norm", "mix_norm", "ple_norm", "ple_gate_norm",
                    "mla_q_lat_norm", "mla_kv_lat_norm", "mla_q_gain", "mla_k_gain"]
SHARDED_SMALL = ["conv_b_pw1", "conv_w_dw", "conv_b_dw", "conv_ln_g", "conv_ln_b"]
PACK_ROWS = 8


def _pack_rows(arrs, width):
    out = []
    for a in arrs:
        r = -(-a.shape[0] // PACK_ROWS) * PACK_ROWS
        out.append(jnp.pad(a, ((0, r - a.shape[0]), (0, width - a.shape[1]))))
    return jnp.concatenate(out, axis=0)


def _unpack_rows(pack, shapes):
    out, r0 = [], 0
    for (r, w) in shapes:
        out.append(pack[r0:r0 + r, :w])
        r0 += -(-r // PACK_ROWS) * PACK_ROWS
    return out


def kernel(x, p, positions, ffn_a_norm, ffn_a_w_in, ffn_a_w_out, ffn_b_norm, ffn_b_w_in, ffn_b_w_out, mix_norm, mla_w_in, mla_q_lat_norm, mla_kv_lat_norm, mla_w_uq, mla_w_ukv, mla_q_gain, mla_k_gain, mla_w_o, conv_w_pw1, conv_b_pw1, conv_w_dw, conv_b_dw, conv_ln_g, conv_ln_b, conv_w_pw2, ple_w_proj, ple_norm, ple_gate_norm, ple_w_gate, loss_target, m_ffn_a_norm, m_ffn_a_w_in, m_ffn_a_w_out, m_ffn_b_norm, m_ffn_b_w_in, m_ffn_b_w_out, m_mix_norm, m_mla_w_in, m_mla_q_lat_norm, m_mla_kv_lat_norm, m_mla_w_uq, m_mla_w_ukv, m_mla_q_gain, m_mla_k_gain, m_mla_w_o, m_conv_w_pw1, m_conv_b_pw1, m_conv_w_dw, m_conv_b_dw, m_conv_ln_g, m_conv_ln_b, m_conv_w_pw2, m_ple_w_proj, m_ple_norm, m_ple_gate_norm, m_ple_w_gate, v_ffn_a_norm, v_ffn_a_w_in, v_ffn_a_w_out, v_ffn_b_norm, v_ffn_b_w_in, v_ffn_b_w_out, v_mix_norm, v_mla_w_in, v_mla_q_lat_norm, v_mla_kv_lat_norm, v_mla_w_uq, v_mla_w_ukv, v_mla_q_gain, v_mla_k_gain, v_mla_w_o, v_conv_w_pw1, v_conv_b_pw1, v_conv_w_dw, v_conv_b_dw, v_conv_ln_g, v_conv_ln_b, v_conv_w_pw2, v_ple_w_proj, v_ple_norm, v_ple_gate_norm, v_ple_w_gate):
    args = dict(locals())
    W = {n: args[n] for n in WEIGHT_ORDER}
    M1 = {n: args["m_" + n] for n in WEIGHT_ORDER}
    V2 = {n: args["v_" + n] for n in WEIGHT_ORDER}

    T, D = x.shape[1], x.shape[2]
    depth = ffn_a_norm.shape[0]
    H = mla_w_ukv.shape[2] * N_CHIPS // (D_NOPE + D_V)
    QL, KL = mla_q_lat_norm.shape[1], mla_kv_lat_norm.shape[1]
    C = conv_w_pw2.shape[1] * N_CHIPS
    lat_w = QL + KL + D_ROPE
    lat_pad = QL + KL + 128

    cx, cy, cc = lax.axis_index("x"), lax.axis_index("y"), lax.axis_index("c")
    chip = (2 * cx + cy).astype(jnp.int32)
    chip_arr = chip.reshape(1)
    core_arr = cc.astype(jnp.int32).reshape(1)

    def stage_groups(i):
        mix = ([("mla_w_in", i // 2), ("mla_w_uq", i // 2), ("mla_w_ukv", i // 2), ("mla_w_o", i // 2)] if i % 2 == 0
               else [("conv_w_pw1", i // 2), ("conv_w_pw2", i // 2)])
        return [[("ffn_a_w_in", i)], [("ffn_a_w_out", i)], mix, [("ffn_b_w_in", i)], [("ffn_b_w_out", i)],
                [("ple_w_proj", i), ("ple_w_gate", i)]]

    groups = [g for i in range(depth) for g in stage_groups(i)]
    A_IN, A_OUT, MIX, B_IN, B_OUT, PLE, PER_LAYER = 0, 1, 2, 3, 4, 5, 6

    def slot_of(key):
        n, l = key
        return cast_into_slot(f"cast_{n}_{l}", W[n].reshape(-1, W[n].shape[-1]), chip_arr, l, W[n].shape[0])

    started = gather_start("gather_start", [[slot_of(k) for k in g] for g in groups])
    relayed, G = {}, {}

    def relay(g, after):
        if g >= len(groups):
            return None
        relayed[g] = gather_relay(f"gather_relay_{g}", started[g], after)
        return relayed[g][3]

    def ready(g, after):
        if g not in relayed:
            relay(g, after)
        for key, buf in zip(groups[g], gather_wait(f"gather_wait_{g}", relayed[g], after)):
            G[key] = buf

    def placed(a, width):
        full = jnp.zeros(a.shape[:-1] + (width,), F32)
        full = lax.dynamic_update_slice_in_dim(full, a, chip * a.shape[-1], axis=a.ndim - 1)
        return full * (cc == 0).astype(F32)

    b_pw1_sh = conv_b_pw1.reshape(1, -1)
    small_in = [placed(b_pw1_sh, 2 * C).reshape(2, C), placed(conv_w_dw[0], C), placed(conv_b_dw, C),
                placed(conv_ln_g, C), placed(conv_ln_b, C)]
    small_full = _unpack_rows(allreduce_pack("gather_small", _pack_rows(small_in, C)),
                              [(2, C), (CONV_WIDTH, C), (1, C), (1, C), (1, C)])
    b_pw1_full = small_full[0].reshape(1, 2 * C)
    w_dw_full = jnp.pad(small_full[1], ((0, CONV_TAPS_PAD - CONV_WIDTH), (0, 0)))
    b_dw_full, ln_g_full, ln_b_full = small_full[2], small_full[3], small_full[4]

    pad_gain = lambda g: jnp.pad(g, ((0, 0), (0, HEAD_PAD - QK_DIM)))
    q_gain_p, k_gain_p = pad_gain(mla_q_gain), pad_gain(mla_k_gain)
    tabs = rope_tables("rope_tables", positions.reshape(T, 1).astype(F32))

    def ffn_fwd(tag, h, norm, w_in, w_out, layer, g_in):
        hn = rms_fwd(f"{tag}_rms", h, norm)
        ready(g_in, hn)
        tok = relay(g_in + 1, hn)
        gu = mm_nn(f"{tag}_in", hn, G[(w_in, layer)], "col", 0, out_dtype=BF16, tm=2048, after=tok)
        act = swiglu_fwd(f"{tag}_act", gu)
        ready(g_in + 1, act)
        tok = relay(g_in + 2, act)
        out = mm_nn(f"{tag}_out", act, G[(w_out, layer)], "row", 0, out_dtype=F32, scale=FFN_RESIDUAL_WEIGHT, res=h,
                    tm=1024, tn=1024, tk=1408, after=tok)
        return out, (h, hn, gu, act)

    saved = []
    h = x[0]
    for i in range(depth):
        L = {}
        g0 = PER_LAYER * i
        h, L["ffn_a"] = ffn_fwd(f"l{i}_ffa", h, ffn_a_norm[i:i + 1], "ffn_a_w_in", "ffn_a_w_out", i, g0 + A_IN)
        L["h1"] = h
        hn = rms_fwd(f"l{i}_mix_rms", h, mix_norm[i:i + 1])
        L["hn_m"] = hn
        ready(g0 + MIX, hn)
        j = i // 2
        if i % 2 == 0:
            w_in_pad = jnp.pad(G[("mla_w_in", j)].reshape(D, lat_w), ((0, 0), (0, lat_pad - lat_w)))[None]
            uq = G[("mla_w_uq", j)].transpose(1, 0, 2).reshape(QL, H, QK_DIM)
            w_uq_pad = jnp.pad(uq, ((0, 0), (0, 0), (0, HEAD_PAD - QK_DIM))).reshape(1, QL, H * HEAD_PAD)
            lat = mm_nn(f"l{i}_lat", hn, w_in_pad, "row", 0, out_dtype=F32, tm=2048)
            cq, ckv = lat_norm_fwd(f"l{i}_latnorm", lat, mla_q_lat_norm[j:j + 1], mla_kv_lat_norm[j:j + 1])
            q_raw = mm_nn(f"l{i}_uq", cq, w_uq_pad, "row", 0, out_dtype=F32, tm=2048)
            kv_raw = mm_nn(f"l{i}_ukv", ckv, G[("mla_w_ukv", j)], "col", 0, out_dtype=F32, tm=2048)
            Qh, Kh, Vh = mla_prep_fwd(f"l{i}_prep", q_raw, kv_raw, lat, tabs, q_gain_p, k_gain_p, QL + KL)
            O = attn_fwd(f"l{i}_attn", Qh, Kh, Vh)
            tok = relay(g0 + B_IN, O)
            h = mm_nn(f"l{i}_wo", O, G[("mla_w_o", j)], "row", 0, out_dtype=F32, res=h, tm=1024, tn=1024, after=tok)
            L["mla"] = (lat, cq, ckv, q_raw, kv_raw, Qh, Kh, Vh, O, w_in_pad, w_uq_pad)
        else:
            ag = mm_nn(f"l{i}_pw1", hn, G[("conv_w_pw1", j)], "col", 0, out_dtype=F32, bias=b_pw1_full, tm=2048)
            u = glu_fwd(f"l{i}_glu", ag)
            yc = dwconv_fwd(f"l{i}_dw", u, w_dw_full, b_dw_full)
            cact = ln_silu_fwd(f"l{i}_ln", yc, ln_g_full, ln_b_full)
            tok = relay(g0 + B_IN, cact)
            h = mm_nn(f"l{i}_pw2", cact, G[("conv_w_pw2", j)], "row", 0, out_dtype=F32, res=h, tm=1024, tn=1024,
                      after=tok)
            L["conv"] = (ag, u, yc, cact)
        L["h2"] = h
        h, L["ffn_b"] = ffn_fwd(f"l{i}_ffb", h, ffn_b_norm[i:i + 1], "ffn_b_w_in", "ffn_b_w_out", i, g0 + B_IN)
        L["h3"] = h
        ready(g0 + PLE, h)
        pe = mm_nn(f"l{i}_ple_proj", p[i, 0], G[("ple_w_proj", i)], "col", 0, out_dtype=F32, tm=2048)
        hg = rms_fwd(f"l{i}_gate_rms", h, ple_gate_norm[i:i + 1])
        tok = relay(g0 + PER_LAYER, hg)
        z = mm_nn(f"l{i}_ple_gate", hg, G[("ple_w_gate", i)], "row", 0, out_dtype=F32, tm=2048, after=tok)
        h = ple_fwd(f"l{i}_ple", h, pe, z, ple_norm[i:i + 1])
        L["ple"] = (pe, hg, z)
        saved.append(L)

    d_h, loss_part = loss_head("loss_head", h, loss_target[0])
    loss = lax.psum(loss_part[0, 0], ("x", "y", "c"))

    GW = {}
    SG = {}

    def put_small(name, i, val):
        SG.setdefault(name, {})[i] = val

    def ffn_bwd(tag, d_h, d_hb, norm, w_in, w_out, layer, fw):
        h_in, hn, gu, act = fw
        d_act = _mm_nt(f"{tag}_dact", d_hb, G[(w_out, layer)], "row", 0, act.shape[1], out_dtype=BF16,
                       scale=FFN_RESIDUAL_WEIGHT, to=1408, tc=2048)
        GW[w_out] = mm_tn(f"{tag}_dwout", act, d_hb, "row", layer=layer, layers=depth, into=GW.get(w_out),
                          scale=FFN_RESIDUAL_WEIGHT, tk=1408, tn=1024)
        dgu = swiglu_bwd(f"{tag}_dgu", gu, d_act)
        d_hn = _mm_nt(f"{tag}_dhn", dgu, G[(w_in, layer)], "col", 0, D, out_dtype=F32)
        GW[w_in] = mm_tn(f"{tag}_dwin", hn, dgu, "col", layer=layer, layers=depth, into=GW.get(w_in))
        return rms_bwd_res(f"{tag}_drms", h_in, d_hn, norm, d_h)

    d_hb = None
    for i in reversed(range(depth)):
        L = saved[i]
        pe, hg, z = L["ple"]
        d_z, d_pe, g = ple_bwd(f"l{i}_dple", d_h, pe, z, ple_norm[i:i + 1])
        put_small("ple_norm", i, g)
        d_hg = _mm_nt(f"l{i}_dhg", d_z, G[("ple_w_gate", i)], "row", 0, D, out_dtype=F32, to=512, tc=2048)
        GW["ple_w_gate"] = mm_tn(f"l{i}_dwgate", hg, d_z, "row", layer=i, layers=depth, into=GW.get("ple_w_gate"),
                                 tk=512, tn=1024)
        GW["ple_w_proj"] = mm_tn(f"l{i}_dwproj", p[i, 0], d_pe, "col", layer=i, layers=depth,
                                 into=GW.get("ple_w_proj"))
        d_h, d_hb, g = rms_bwd_res(f"l{i}_dgate_rms", L["h3"], d_hg, ple_gate_norm[i:i + 1], d_h)
        put_small("ple_gate_norm", i, g)

        d_h, d_hb, g = ffn_bwd(f"l{i}_ffb", d_h, d_hb, ffn_b_norm[i:i + 1], "ffn_b_w_in", "ffn_b_w_out", i, L["ffn_b"])
        put_small("ffn_b_norm", i, g)

        hn = L["hn_m"]
        if i % 2 == 0:
            lat, cq, ckv, q_raw, kv_raw, Qh, Kh, Vh, O, w_in_pad, w_uq_pad = L["mla"]
            d_O = _mm_nt(f"l{i}_dO", d_hb, G[("mla_w_o", i // 2)], "row", 0, H * D_V, out_dtype=BF16, to=512, tc=2048)
            GW["mla_w_o"] = mm_tn(f"l{i}_dwo", O, d_hb, "row", tk=512, tn=1024)
            dQ, dK, dV = attn_bwd(f"l{i}_dattn", Qh, Kh, Vh, d_O)
            d_q_raw, d_kv_raw, d_kr, gq, gk = mla_prep_bwd(f"l{i}_dprep", dQ, dK, dV, q_raw, kv_raw, lat, tabs,
                                                           q_gain_p, k_gain_p, QL + KL)
            put_small("mla_q_gain", 0, gq[:, :QK_DIM])
            put_small("mla_k_gain", 0, gk[:, :QK_DIM])
            d_cq = _mm_nt(f"l{i}_dcq", d_q_raw, w_uq_pad, "row", 0, QL, out_dtype=F32, to=512, tc=2048)
            g_uq = mm_tn(f"l{i}_dwuq", cq, d_q_raw, "row", shards=1, out_dtype=F32, tk=512, tn=1024)
            g_uq = g_uq.reshape(QL, H, HEAD_PAD)[:, :, :QK_DIM].reshape(QL, N_CHIPS, -1).transpose(1, 0, 2)
            GW["mla_w_uq"] = g_uq.astype(BF16)
            d_ckv = _mm_nt(f"l{i}_dckv", d_kv_raw, G[("mla_w_ukv", i // 2)], "col", 0, KL, out_dtype=F32, to=512, tc=1024)
            GW["mla_w_ukv"] = mm_tn(f"l{i}_dwukv", ckv, d_kv_raw, "col", tk=512, tn=1024)
            d_lat, gq, gk = lat_norm_bwd(f"l{i}_dlatnorm", lat, d_cq, d_ckv, d_kr, mla_q_lat_norm, mla_kv_lat_norm)
            put_small("mla_q_lat_norm", 0, gq)
            put_small("mla_kv_lat_norm", 0, gk)
            d_hn = _mm_nt(f"l{i}_dhn_lat", d_lat, w_in_pad, "row", 0, D, out_dtype=F32, to=1024, tc=lat_pad)
            g_in = mm_tn(f"l{i}_dwin_lat", hn, d_lat, "row", shards=1, out_dtype=F32, tk=1024, tn=lat_pad)
            GW["mla_w_in"] = g_in[0, :, :lat_w].reshape(N_CHIPS, D // N_CHIPS, lat_w).astype(BF16)
        else:
            ag, u, yc, cact = L["conv"]
            d_cact = _mm_nt(f"l{i}_dcact", d_hb, G[("conv_w_pw2", i // 2)], "row", 0, C, out_dtype=F32, to=512, tc=2048)
            GW["conv_w_pw2"] = mm_tn(f"l{i}_dwpw2", cact, d_hb, "row", tk=512, tn=1024)
            d_yc, g1, g2, g3 = ln_silu_bwd(f"l{i}_dln", yc, d_cact, ln_g_full, ln_b_full)
            put_small("conv_ln_g", 0, g1)
            put_small("conv_ln_b", 0, g2)
            put_small("conv_b_dw", 0, g3)
            d_u = dwconv_bwd_u(f"l{i}_ddw_u", d_yc, w_dw_full)
            put_small("conv_w_dw", 0, dwconv_bwd_w(f"l{i}_ddw_w", u, d_yc))
            d_ag, g = glu_bwd(f"l{i}_dglu", ag, d_u)
            put_small("conv_b_pw1", 0, g)
            d_hn = _mm_nt(f"l{i}_dhn_pw1", d_ag, G[("conv_w_pw1", i // 2)], "col", 0, D, out_dtype=F32, tc=1024)
            GW["conv_w_pw1"] = mm_tn(f"l{i}_dwpw1", hn, d_ag, "col", tn=1024)
        d_h, d_hb, g = rms_bwd_res(f"l{i}_dmix_rms", L["h1"], d_hn, mix_norm[i:i + 1], d_h)
        put_small("mix_norm", i, g)

        d_h, d_hb, g = ffn_bwd(f"l{i}_ffa", d_h, d_hb, ffn_a_norm[i:i + 1], "ffn_a_w_in", "ffn_a_w_out", i, L["ffn_a"])
        put_small("ffn_a_norm", i, g)
    grad_x = d_h[None]

    names = [n for n, _ in BIG_WEIGHTS]
    parts = [GW[n] for n in names]
    from_sibling = swap_grad_halves("swap_grad_halves", parts)
    pair_sums = [add_halves("pair_sum_" + n, core_arr, q, r) for n, q, r in zip(names, parts, from_sibling)]
    chip_sums = exchange_chip_sums("exchange_chip_sums", pair_sums)
    halves = [sum_chips("chip_sum_" + n, s, core_arr, q.shape[1]) for n, s, q in zip(names, chip_sums, parts)]
    full = share_grad_halves("share_grad_halves", halves)
    grads = {n: g.reshape(W[n].shape) for n, g in zip(names, full)}

    rep = []
    for n in REPLICATED_SMALL:
        rep.append(jnp.concatenate([SG[n][i] for i in sorted(SG[n])], axis=0))
    shd = [SG["conv_b_pw1"][0].reshape(2, C), SG["conv_w_dw"][0][:CONV_WIDTH], SG["conv_b_dw"][0],
           SG["conv_ln_g"][0], SG["conv_ln_b"][0]]
    red = allreduce_pack("allreduce_small", _pack_rows(rep + shd, D))
    red = _unpack_rows(red, [a.shape for a in rep + shd])
    for n, g in zip(REPLICATED_SMALL, red):
        grads[n] = g
    own = lambda a, w: lax.dynamic_slice_in_dim(a, chip * w, w, axis=a.ndim - 1)
    sh = red[len(rep):]
    grads["conv_b_pw1"] = own(sh[0].reshape(1, 2 * C), 2 * C // N_CHIPS)
    grads["conv_w_dw"] = own(sh[1], C // N_CHIPS)[None]
    grads["conv_b_dw"] = own(sh[2], C // N_CHIPS)
    grads["conv_ln_g"] = own(sh[3], C // N_CHIPS)
    grads["conv_ln_b"] = own(sh[4], C // N_CHIPS)

    delta, new_m, new_v = {}, {}, {}
    for n in names:
        two = lambda a: a.reshape(-1, a.shape[-1])
        d, nm, nv = adamw("adamw_" + n, two(W[n]), two(grads[n]), two(M1[n]), two(V2[n]))
        delta[n], new_m[n], new_v[n] = d.reshape(W[n].shape), nm.reshape(W[n].shape), nv.reshape(W[n].shape)
    small = REPLICATED_SMALL + SHARDED_SMALL
    two = lambda a: a.reshape(-1, a.shape[-1])
    shapes = [two(W[n]).shape for n in small]
    packs = [_pack_rows([two(src[n]) for n in small], D) for src in (W, grads, M1, V2)]
    outs = adamw("adamw_small", *packs)
    for dst, pk in zip((delta, new_m, new_v), outs):
        for n, a in zip(small, _unpack_rows(pk, shapes)):
            dst[n] = a.reshape(W[n].shape)
    for n in small:
        grads[n] = grads[n].reshape(W[n].shape)

    return (loss, grad_x, *[grads[n] for n in WEIGHT_ORDER], *[delta[n] for n in WEIGHT_ORDER],
            *[new_m[n] for n in WEIGHT_ORDER], *[new_v[n] for n in WEIGHT_ORDER])
```

```python
import functools
import math

import jax
import jax.numpy as jnp
from jax import lax
from jax.experimental import pallas as pl
from jax.experimental.pallas import tpu as pltpu

F32, BF16 = jnp.float32, jnp.bfloat16
MESH = pl.DeviceIdType.MESH

N_CHIPS = 4
EPS = 1e-6
D_NOPE, D_ROPE, D_V = 128, 64, 128
QK_DIM = D_NOPE + D_ROPE
HEAD_PAD = 256
ROPE_THETA = 10000.0
CONV_WIDTH = 31
CONV_TAPS_PAD = 32
FFN_RESIDUAL_WEIGHT = 0.5
ADAM_LR, ADAM_B1, ADAM_B2, ADAM_EPS, ADAM_WD, ADAM_STEP = 0.001, 0.9, 0.999, 1e-08, 0.01, 10
VMEM_LIMIT_BYTES = 56 * 1024 * 1024
NEG_BIG = -1e30

NN_DIMS = (((1,), (0,)), ((), ()))
NT_DIMS = (((1,), (1,)), ((), ()))
TN_DIMS = (((0,), (0,)), ((), ()))


def _cparams(semantics=None):
    kw = dict(vmem_limit_bytes=VMEM_LIMIT_BYTES)
    if semantics is not None:
        kw["dimension_semantics"] = semantics
    return pltpu.CompilerParams(**kw)


def _tile(n, pref, mult=128):
    if n <= pref:
        return n
    t = (pref // mult) * mult
    while t >= mult:
        if n % t == 0:
            return t
        t -= mult
    return n


def _rowwise(name, fn, rows, vecs, outs, accs=(), tm=256, rc=16):
    T = rows[0].shape[0]
    tm = min(tm, T)
    rc = min(rc, tm)
    nr, nv, no, na = len(rows), len(vecs), len(outs), len(accs)
    steps = tm // rc

    def body(*refs):
        row_refs = refs[:nr]
        vec_refs = refs[nr:nr + nv]
        out_refs = refs[nr + nv:nr + nv + no]
        acc_refs = refs[nr + nv + no:]
        if na:
            @pl.when(pl.program_id(0) == 0)
            def _():
                for a in acc_refs:
                    a[...] = jnp.zeros_like(a)

        def step(r, carry):
            sl = pl.ds(pl.multiple_of(r * rc, rc), rc)
            res = fn(*[x[sl, :] for x in row_refs], *[v[...] for v in vec_refs])
            for o, val in zip(out_refs, res[:no]):
                o[sl, :] = val.astype(o.dtype)
            return tuple(c + val for c, val in zip(carry, res[no:]))

        init = tuple(jnp.zeros(s, F32) for s in accs)
        tot = lax.fori_loop(0, steps, step, init)
        for a, val in zip(acc_refs, tot):
            a[...] += val

    in_specs = [pl.BlockSpec((tm, x.shape[1]), lambda i: (i, 0)) for x in rows]
    in_specs += [pl.BlockSpec(v.shape, lambda i: (0, 0)) for v in vecs]
    out_specs = [pl.BlockSpec((tm, d), lambda i: (i, 0)) for d, _ in outs]
    out_specs += [pl.BlockSpec(s, lambda i: (0, 0)) for s in accs]
    out_shape = [jax.ShapeDtypeStruct((T, d), dt) for d, dt in outs]
    out_shape += [jax.ShapeDtypeStruct(s, F32) for s in accs]
    return pl.pallas_call(
        body, name=name, grid=(T // tm,), in_specs=in_specs, out_specs=out_specs, out_shape=out_shape,
        compiler_params=_cparams(("arbitrary",)),
    )(*rows, *vecs)


def _colsum(v):
    return jnp.sum(v, axis=0, keepdims=True)


def _rstd(x):
    return lax.rsqrt(jnp.mean(x * x, axis=-1, keepdims=True) + EPS)


def _rms_bwd(x, dy, g):
    r = _rstd(x)
    xh = x * r
    dyg = dy * g
    dx = r * (dyg - xh * jnp.mean(dyg * xh, axis=-1, keepdims=True))
    return dx, dy * xh


def _sigmoid(x):
    return 1.0 / (1.0 + jnp.exp(-x))


def rms_fwd(name, h, g):
    def fn(x, gv):
        return ((x * _rstd(x)) * gv,)
    return _rowwise(name, fn, [h], [g], [(h.shape[1], BF16)])[0]


def rms_bwd_res(name, h, d_y, g, d_res):
    D = h.shape[1]

    def fn(x, dy, dr, gv):
        dx, dgr = _rms_bwd(x, dy, gv)
        dh = dr + dx
        return dh, dh, _colsum(dgr)
    return _rowwise(name, fn, [h, d_y, d_res], [g], [(D, F32), (D, BF16)], [(1, D)], tm=128)


def swiglu_fwd(name, gu):
    Fh = gu.shape[1] // 2

    def fn(v):
        g = v[:, :Fh].astype(F32)
        u = v[:, Fh:].astype(F32)
        return (g * _sigmoid(g) * u,)
    return _rowwise(name, fn, [gu], [], [(Fh, BF16)])[0]


def swiglu_bwd(name, gu, d_act):
    Fh = gu.shape[1] // 2

    def fn(v, da):
        g = v[:, :Fh].astype(F32)
        u = v[:, Fh:].astype(F32)
        da = da.astype(F32)
        s = _sigmoid(g)
        d_u = da * g * s
        d_g = da * u * s * (1.0 + g * (1.0 - s))
        return (jnp.concatenate([d_g, d_u], axis=-1),)
    return _rowwise(name, fn, [gu, d_act], [], [(2 * Fh, BF16)])[0]


def loss_head(name, y, target):
    D = y.shape[1]

    def fn(yv, tv):
        e = yv - tv
        tot = jnp.sum(_colsum(e * e), axis=1, keepdims=True) * (0.5 / D)
        return e * (1.0 / D), jnp.broadcast_to(tot, (1, 128))
    return _rowwise(name, fn, [y, target], [], [(D, F32)], [(1, 128)])


def ple_fwd(name, h, pe, z, g_e):
    def fn(hv, pv, zv, gv):
        return (hv + (pv * _rstd(pv)) * gv * _sigmoid(zv),)
    return _rowwise(name, fn, [h, pe, z], [g_e], [(h.shape[1], F32)], tm=128)[0]


def ple_bwd(name, d_h, pe, z, g_e):
    D = d_h.shape[1]

    def fn(dh, pv, zv, gv):
        gate = _sigmoid(zv)
        e = (pv * _rstd(pv)) * gv
        d_z = dh * e * gate * (1.0 - gate)
        d_pe, dgr = _rms_bwd(pv, dh * gate, gv)
        return d_z, d_pe, _colsum(dgr)
    return _rowwise(name, fn, [d_h, pe, z], [g_e], [(D, BF16), (D, BF16)], [(1, D)], tm=128)


def lat_norm_fwd(name, lat, g_q, g_kv):
    QL, KL = g_q.shape[1], g_kv.shape[1]

    def fn(v, gq, gk):
        a = v[:, :QL]
        b = v[:, QL:QL + KL]
        return (a * _rstd(a)) * gq, (b * _rstd(b)) * gk
    return _rowwise(name, fn, [lat], [g_q, g_kv], [(QL, BF16), (KL, BF16)])


def lat_norm_bwd(name, lat, d_cq, d_ckv, d_krope, g_q, g_kv):
    QL, KL = g_q.shape[1], g_kv.shape[1]

    def fn(v, dq, dk, dr, gq, gk):
        da, ga = _rms_bwd(v[:, :QL], dq, gq)
        db, gb = _rms_bwd(v[:, QL:QL + KL], dk, gk)
        return jnp.concatenate([da, db, dr], axis=-1), _colsum(ga), _colsum(gb)
    return _rowwise(name, fn, [lat, d_cq, d_ckv, d_krope], [g_q, g_kv],
                    [(lat.shape[1], BF16)], [(1, QL), (1, KL)])


def glu_fwd(name, ag):
    C = ag.shape[1] // 2

    def fn(v):
        return (v[:, :C] * _sigmoid(v[:, C:]),)
    return _rowwise(name, fn, [ag], [], [(C, F32)], tm=128)[0]


def glu_bwd(name, ag, d_u):
    C = ag.shape[1] // 2

    def fn(v, du):
        a = v[:, :C]
        s = _sigmoid(v[:, C:])
        d = jnp.concatenate([du * s, du * a * s * (1.0 - s)], axis=-1)
        return d, _colsum(d)
    return _rowwise(name, fn, [ag, d_u], [], [(2 * C, BF16)], [(1, 2 * C)], tm=128)


def ln_silu_fwd(name, yc, g, b):
    def fn(v, gv, bv):
        xc = v - jnp.mean(v, axis=-1, keepdims=True)
        ln = xc * lax.rsqrt(jnp.mean(xc * xc, axis=-1, keepdims=True) + EPS) * gv + bv
        return (ln * _sigmoid(ln),)
    return _rowwise(name, fn, [yc], [g, b], [(yc.shape[1], BF16)], tm=128)[0]


def ln_silu_bwd(name, yc, d_out, g, b):
    C = yc.shape[1]

    def fn(v, do, gv, bv):
        xc = v - jnp.mean(v, axis=-1, keepdims=True)
        r = lax.rsqrt(jnp.mean(xc * xc, axis=-1, keepdims=True) + EPS)
        xh = xc * r
        ln = xh * gv + bv
        s = _sigmoid(ln)
        d_ln = do * s * (1.0 + ln * (1.0 - s))
        dxh = d_ln * gv
        dy = r * (dxh - jnp.mean(dxh, axis=-1, keepdims=True) - xh * jnp.mean(dxh * xh, axis=-1, keepdims=True))
        return dy, _colsum(d_ln * xh), _colsum(d_ln), _colsum(dy)
    return _rowwise(name, fn, [yc, d_out], [g, b], [(C, F32)], [(1, C), (1, C), (1, C)], tm=128)


def add_halves(name, core, own, got):
    S, Rh, C = got.shape
    tr = _tile(Rh, 256, 16)
    nrb = Rh // tr

    def body(c_ref, a_ref, b_ref, o_ref):
        o_ref[...] = (a_ref[...].astype(F32) + b_ref[...].astype(F32)).astype(o_ref.dtype)

    gs = pltpu.PrefetchScalarGridSpec(
        num_scalar_prefetch=1, grid=(S, nrb),
        in_specs=[pl.BlockSpec((None, tr, C), lambda s, r, c: (s, c[0] * nrb + r, 0)),
                  pl.BlockSpec((None, tr, C), lambda s, r, c: (s, r, 0))],
        out_specs=pl.BlockSpec((None, tr, C), lambda s, r, c: (s, r, 0)))
    return pl.pallas_call(
        body, name=name, grid_spec=gs, out_shape=jax.ShapeDtypeStruct((S, Rh, C), BF16),
        compiler_params=_cparams(("arbitrary", "arbitrary")))(core, own, got)


def sum_chips(name, ids, sums, landed, rows):
    S, Rh, C = sums.shape
    tr = _tile(Rh, 256, 16)
    nrb = Rh // tr

    def body(ids_ref, own_ref, a_ref, b_ref, c_ref, o_ref):
        o_ref[...] = ((own_ref[...].astype(F32) + a_ref[...].astype(F32))
                      + (b_ref[...].astype(F32) + c_ref[...].astype(F32)))

    slot = lambda flip: pl.BlockSpec((None, tr, C), lambda r, ids: (ids[1] ^ flip, r, 0))
    gs = pltpu.PrefetchScalarGridSpec(
        num_scalar_prefetch=1, grid=(nrb,), in_specs=[slot(0), slot(1), slot(2), slot(3)],
        out_specs=pl.BlockSpec((tr, C), lambda r, ids: (ids[0] * nrb + r, 0)))
    return pl.pallas_call(
        body, name=name, grid_spec=gs, out_shape=jax.ShapeDtypeStruct((rows, C), F32),
        compiler_params=_cparams(("arbitrary",)))(ids, sums, landed, landed, landed)


def adamw_layer(name, w, g, m, v, layer, layers, into):
    R, C = g.shape
    tr = _tile(R, 256, 8)
    nrb = R // tr
    c1 = 1.0 / (1.0 - ADAM_B1 ** ADAM_STEP)
    c2 = 1.0 / (1.0 - ADAM_B2 ** ADAM_STEP)

    def body(w_ref, g_ref, m_ref, v_ref, *rest):
        go_ref, d_ref, nm_ref, nv_ref = rest[-4:]
        gv = g_ref[...]
        nm = ADAM_B1 * m_ref[...] + (1.0 - ADAM_B1) * gv
        nv = ADAM_B2 * v_ref[...] + (1.0 - ADAM_B2) * (gv * gv)
        go_ref[...] = gv
        d_ref[...] = -ADAM_LR * ((nm * c1) / (jnp.sqrt(nv * c2) + ADAM_EPS) + ADAM_WD * w_ref[...])
        nm_ref[...] = nm
        nv_ref[...] = nv

    at_layer = pl.BlockSpec((tr, C), lambda r: (layer * nrb + r, 0))
    in_specs = [at_layer, pl.BlockSpec((tr, C), lambda r: (r, 0)), at_layer, at_layer]
    operands = [w, g, m, v]
    aliases = {}
    if into is not None:
        in_specs += [pl.BlockSpec(memory_space=pl.ANY)] * 4
        operands += list(into)
        aliases = {4 + k: k for k in range(4)}
    return pl.pallas_call(
        body, name=name, grid=(nrb,), in_specs=in_specs, out_specs=[at_layer] * 4,
        out_shape=[jax.ShapeDtypeStruct((layers * R, C), F32)] * 4, input_output_aliases=aliases,
        compiler_params=_cparams(("arbitrary",)))(*operands)


def cast_into_slot(name, w, chip, layer, layers):
    R, C = w.shape[0] // layers, w.shape[1]
    tr = _tile(R, 256, 16)
    nrb = R // tr

    def body(s_ref, w_ref, o_ref):
        o_ref[...] = w_ref[...].astype(BF16)

    gs = pltpu.PrefetchScalarGridSpec(
        num_scalar_prefetch=1, grid=(nrb,),
        in_specs=[pl.BlockSpec((tr, C), lambda r, s: (layer * nrb + r, 0))],
        out_specs=pl.BlockSpec((None, tr, C), lambda r, s: (s[0], r, 0)))
    return pl.pallas_call(
        body, name=name, grid_spec=gs, out_shape=jax.ShapeDtypeStruct((N_CHIPS, R, C), BF16),
        compiler_params=_cparams(("arbitrary",)))(chip, w)


def adamw(name, w, g, m, v):
    R, C = w.shape
    tr = _tile(R, 256, 8)
    c1 = 1.0 / (1.0 - ADAM_B1 ** ADAM_STEP)
    c2 = 1.0 / (1.0 - ADAM_B2 ** ADAM_STEP)

    def body(w_ref, g_ref, m_ref, v_ref, d_ref, nm_ref, nv_ref):
        gv = g_ref[...]
        nm = ADAM_B1 * m_ref[...] + (1.0 - ADAM_B1) * gv
        nv = ADAM_B2 * v_ref[...] + (1.0 - ADAM_B2) * (gv * gv)
        d_ref[...] = -ADAM_LR * ((nm * c1) / (jnp.sqrt(nv * c2) + ADAM_EPS) + ADAM_WD * w_ref[...])
        nm_ref[...] = nm
        nv_ref[...] = nv

    spec = pl.BlockSpec((tr, C), lambda r: (r, 0))
    return pl.pallas_call(
        body, name=name, grid=(R // tr,), in_specs=[spec] * 4, out_specs=[spec] * 3,
        out_shape=[jax.ShapeDtypeStruct((R, C), F32)] * 3, compiler_params=_cparams(("arbitrary",)))(w, g, m, v)


def _matmul(name, a, b, *, grid, a_blk, a_map, b_blk, b_map, o_shape, o_dtype, o_blk, o_map, dims,
            scale=None, res=None, bias=None, bias_blk=None, bias_map=None, alias_into=None, after=None):
    nk = grid[2]
    has_res, has_bias, has_into = res is not None, bias is not None, alias_into is not None
    acc_shape = tuple(d for d in o_blk if d is not None)

    def body(*refs):
        a_ref, b_ref = refs[0], refs[1]
        pos = 2
        res_ref = bias_ref = None
        if has_res:
            res_ref = refs[pos]
            pos += 1
        if has_bias:
            bias_ref = refs[pos]
            pos += 1
        if has_into:
            pos += 1
        if after is not None:
            pos += 1
        o_ref = refs[pos]
        av, bv = a_ref[...], b_ref[...]
        if av.dtype != BF16:
            av = av.astype(BF16)
        if bv.dtype != BF16:
            bv = bv.astype(BF16)
        part = lax.dot_general(av, bv, dims, preferred_element_type=F32)

        def finish(acc):
            if scale is not None:
                acc = acc * scale
            if has_bias:
                acc = acc + bias_ref[...]
            if has_res:
                acc = acc + res_ref[...]
            o_ref[...] = acc.astype(o_ref.dtype)

        if nk == 1:
            finish(part)
        else:
            acc_ref = refs[pos + 1]
            k = pl.program_id(2)

            @pl.when(k == 0)
            def _():
                acc_ref[...] = part

            @pl.when(k > 0)
            def _():
                acc_ref[...] += part

            @pl.when(k == nk - 1)
            def _():
                finish(acc_ref[...])

    operands = [a, b]
    in_specs = [pl.BlockSpec(a_blk, a_map), pl.BlockSpec(b_blk, b_map)]
    if has_res:
        operands.append(res)
        in_specs.append(pl.BlockSpec(o_blk, o_map))
    if has_bias:
        operands.append(bias)
        in_specs.append(pl.BlockSpec(bias_blk, bias_map))
    aliases = {}
    if has_into:
        aliases = {len(operands): 0}
        operands.append(alias_into)
        in_specs.append(pl.BlockSpec(memory_space=pl.ANY))
    if after is not None:
        operands.append(after)
        in_specs.append(pl.BlockSpec(memory_space=pl.ANY))
    return pl.pallas_call(
        body, name=name, grid=grid, in_specs=in_specs, out_specs=pl.BlockSpec(o_blk, o_map),
        out_shape=jax.ShapeDtypeStruct(o_shape, o_dtype),
        scratch_shapes=[pltpu.VMEM(acc_shape, F32)] if nk > 1 else [],
        input_output_aliases=aliases,
        compiler_params=_cparams(("parallel", "parallel", "arbitrary")),
    )(*operands)


def mm_nn(name, a, w3, kind, layer, *, out_dtype, scale=None, res=None, bias=None, tm=1024, tn=512, tk=2048,
          after=None):
    M, K = a.shape
    S, _, C = w3.shape
    tm = _tile(M, tm, 16)
    if kind == "col":
        N = S * C
        tk, tn = _tile(K, tk), _tile(C, tn)
        kb, nb = K // tk, C // tn
        b_map = lambda n, m, k: (n // nb, layer * kb + k, n % nb)
    else:
        N, K4 = C, K // S
        tk, tn = _tile(K4, tk), _tile(C, tn)
        kb4 = K4 // tk
        b_map = lambda n, m, k: (k // kb4, layer * kb4 + k % kb4, n)
    return _matmul(
        name, a, w3, grid=(N // tn, M // tm, K // tk),
        a_blk=(tm, tk), a_map=lambda n, m, k: (m, k), b_blk=(None, tk, tn), b_map=b_map,
        o_shape=(M, N), o_dtype=out_dtype, o_blk=(tm, tn), o_map=lambda n, m, k: (m, n), dims=NN_DIMS,
        scale=scale, res=res, bias=bias, bias_blk=(1, tn), bias_map=lambda n, m, k: (0, n), after=after)


def _mm_nt(name, g, w3, kind, layer, K, *, out_dtype, scale=None, tm=1024, to=1024, tc=1408, after=None):
    M, N = g.shape
    S, _, C = w3.shape
    tm = _tile(M, tm, 16)
    tc = _tile(C, tc)
    if kind == "col":
        nb = C // tc
        to = _tile(K, to)
        ob = K // to
        b_map = lambda o, m, c: (c // nb, layer * ob + o, c % nb)
    else:
        K4 = K // S
        to = _tile(K4, to)
        ob4 = K4 // to
        b_map = lambda o, m, c: (o // ob4, layer * ob4 + o % ob4, c)
    return _matmul(
        name, g, w3, grid=(K // to, M // tm, N // tc),
        a_blk=(tm, tc), a_map=lambda o, m, c: (m, c), b_blk=(None, to, tc), b_map=b_map,
        o_shape=(M, K), o_dtype=out_dtype, o_blk=(tm, to), o_map=lambda o, m, c: (m, o), dims=NT_DIMS,
        scale=scale, after=after)


def mm_tn(name, a, g, kind, *, shards=N_CHIPS, layer=0, layers=1, into=None, out_dtype=BF16, scale=None,
          tk=1024, tn=1408, tm=2048, after=None):
    M, K = a.shape
    N = g.shape[1]
    tm = _tile(M, tm, 16)
    if kind == "col":
        C = N // shards
        tk, tn = _tile(K, tk), _tile(C, tn)
        kb, nb = K // tk, C // tn
        o_shape = (shards, layers * K, C)
        o_map = lambda k, n, m: (n // nb, layer * kb + k, n % nb)
    else:
        K4 = K // shards
        tk, tn = _tile(K4, tk), _tile(N, tn)
        kb4 = K4 // tk
        o_shape = (shards, layers * K4, N)
        o_map = lambda k, n, m: (k // kb4, layer * kb4 + k % kb4, n)
    return _matmul(
        name, a, g, grid=(K // tk, N // tn, M // tm),
        a_blk=(tm, tk), a_map=lambda k, n, m: (m, k), b_blk=(tm, tn), b_map=lambda k, n, m: (m, n),
        o_shape=o_shape, o_dtype=out_dtype, o_blk=(None, tk, tn), o_map=o_map, dims=TN_DIMS,
        scale=scale, alias_into=into, after=after)


def rope_tables(name, pos):
    T = pos.shape[0]
    half = D_ROPE // 2

    def body(p_ref, c_ref, s1_ref, s2_ref):
        lane = lax.broadcasted_iota(jnp.int32, (T, 128), 1)
        idx = (lane & (half - 1)).astype(F32)
        ang = p_ref[...] * jnp.exp(idx * (-2.0 * math.log(ROPE_THETA) / D_ROPE))
        cs, sn = jnp.cos(ang), jnp.sin(ang)
        c_ref[...] = jnp.where(lane < D_ROPE, cs, 0.0)
        s1_ref[...] = jnp.where(lane < half, -sn, 0.0)
        s2_ref[...] = jnp.where((lane >= half) & (lane < D_ROPE), sn, 0.0)

    return pl.pallas_call(body, name=name, out_shape=[jax.ShapeDtypeStruct((T, 128), F32)] * 3,
                          compiler_params=_cparams())(pos)


def _rope(v, cs, s1, s2):
    return v * cs + pltpu.roll(v, 128 - D_ROPE // 2, 1) * s1 + pltpu.roll(v, D_ROPE // 2, 1) * s2


def _rope_bwd(d, cs, s1, s2):
    return d * cs + pltpu.roll(d * s1, D_ROPE // 2, 1) + pltpu.roll(d * s2, 128 - D_ROPE // 2, 1)


def _head_rstd(n, r):
    ms = (jnp.sum(n * n, axis=-1, keepdims=True) + jnp.sum(r * r, axis=-1, keepdims=True)) * (1.0 / QK_DIM)
    return lax.rsqrt(ms + EPS)


def mla_prep_fwd(name, q_raw, kv_raw, lat, tabs, q_gain, k_gain, rope_col, tm=128):
    T = q_raw.shape[0]
    H = q_raw.shape[1] // HEAD_PAD
    tm = min(tm, T)
    rope_blk = rope_col // 128

    def body(q_ref, kv_ref, kr_ref, c_ref, s1_ref, s2_ref, qg_ref, kg_ref, Q_ref, K_ref, V_ref):
        cs, s1, s2 = c_ref[...], s1_ref[...], s2_ref[...]
        qg, kg = qg_ref[...], kg_ref[...]
        kr = kr_ref[...]
        for h in range(H):
            lo = HEAD_PAD * h
            n, r = q_ref[:, lo:lo + 128], q_ref[:, lo + 128:lo + 256]
            rs = _head_rstd(n, r)
            Q_ref[h, :, 0:128] = (n * rs * qg[:, :128]).astype(BF16)
            Q_ref[h, :, 128:256] = _rope(r * rs * qg[:, 128:], cs, s1, s2).astype(BF16)
            n = kv_ref[:, lo:lo + 128]
            rs = _head_rstd(n, kr)
            K_ref[h, :, 0:128] = (n * rs * kg[:, :128]).astype(BF16)
            K_ref[h, :, 128:256] = _rope(kr * rs * kg[:, 128:], cs, s1, s2).astype(BF16)
            V_ref[h] = kv_ref[:, lo + 128:lo + 256].astype(BF16)

    row = lambda w: pl.BlockSpec((tm, w), lambda i: (i, 0))
    vec = pl.BlockSpec((1, HEAD_PAD), lambda i: (0, 0))
    return pl.pallas_call(
        body, name=name, grid=(T // tm,),
        in_specs=[row(H * HEAD_PAD), row(H * HEAD_PAD), pl.BlockSpec((tm, 128), lambda i: (i, rope_blk)),
                  row(128), row(128), row(128), vec, vec],
        out_specs=[pl.BlockSpec((H, tm, HEAD_PAD), lambda i: (0, i, 0))] * 2 + [pl.BlockSpec((H, tm, D_V), lambda i: (0, i, 0))],
        out_shape=[jax.ShapeDtypeStruct((H, T, HEAD_PAD), BF16)] * 2 + [jax.ShapeDtypeStruct((H, T, D_V), BF16)],
        compiler_params=_cparams(("arbitrary",)),
    )(q_raw, kv_raw, lat, *tabs, q_gain, k_gain)


def mla_prep_bwd(name, dQ, dK, dV, q_raw, kv_raw, lat, tabs, q_gain, k_gain, rope_col, tm=128):
    T = q_raw.shape[0]
    H = q_raw.shape[1] // HEAD_PAD
    tm = min(tm, T)
    rope_blk = rope_col // 128

    def body(dQ_ref, dK_ref, dV_ref, q_ref, kv_ref, kr_ref, c_ref, s1_ref, s2_ref, qg_ref, kg_ref,
             dq_ref, dkv_ref, dkr_ref, dqg_ref, dkg_ref):
        @pl.when(pl.program_id(0) == 0)
        def _():
            dqg_ref[...] = jnp.zeros_like(dqg_ref)
            dkg_ref[...] = jnp.zeros_like(dkg_ref)

        cs, s1, s2 = c_ref[...], s1_ref[...], s2_ref[...]
        qg, kg = qg_ref[...], kg_ref[...]
        kr = kr_ref[...]
        dkr = jnp.zeros((tm, 128), F32)
        gq_n = jnp.zeros((1, 128), F32)
        gq_r = jnp.zeros((1, 128), F32)
        gk_n = jnp.zeros((1, 128), F32)
        gk_r = jnp.zeros((1, 128), F32)

        def norm_bwd(n, r, dn, dr, gain):
            rs = _head_rstd(n, r)
            nh, rh = n * rs, r * rs
            dng, drg = dn * gain[:, :128], dr * gain[:, 128:]
            mean = (jnp.sum(dng * nh, axis=-1, keepdims=True) + jnp.sum(drg * rh, axis=-1, keepdims=True)) * (1.0 / QK_DIM)
            return rs * (dng - nh * mean), rs * (drg - rh * mean), _colsum(dn * nh), _colsum(dr * rh)

        for h in range(H):
            lo = HEAD_PAD * h
            n, r = q_ref[:, lo:lo + 128], q_ref[:, lo + 128:lo + 256]
            dn = dQ_ref[h, :, 0:128].astype(F32)
            dr = _rope_bwd(dQ_ref[h, :, 128:256].astype(F32), cs, s1, s2)
            a, b, g1, g2 = norm_bwd(n, r, dn, dr, qg)
            dq_ref[:, lo:lo + 128] = a.astype(BF16)
            dq_ref[:, lo + 128:lo + 256] = b.astype(BF16)
            gq_n, gq_r = gq_n + g1, gq_r + g2
            n = kv_ref[:, lo:lo + 128]
            dn = dK_ref[h, :, 0:128].astype(F32)
            dr = _rope_bwd(dK_ref[h, :, 128:256].astype(F32), cs, s1, s2)
            a, b, g1, g2 = norm_bwd(n, kr, dn, dr, kg)
            dkv_ref[:, lo:lo + 128] = a.astype(BF16)
            dkv_ref[:, lo + 128:lo + 256] = dV_ref[h].astype(BF16)
            dkr = dkr + b
            gk_n, gk_r = gk_n + g1, gk_r + g2
        dkr_ref[...] = dkr
        dqg_ref[:, 0:128] += gq_n
        dqg_ref[:, 128:256] += gq_r
        dkg_ref[:, 0:128] += gk_n
        dkg_ref[:, 128:256] += gk_r

    row = lambda w: pl.BlockSpec((tm, w), lambda i: (i, 0))
    vec = pl.BlockSpec((1, HEAD_PAD), lambda i: (0, 0))
    hd = lambda w: pl.BlockSpec((H, tm, w), lambda i: (0, i, 0))
    return pl.pallas_call(
        body, name=name, grid=(T // tm,),
        in_specs=[hd(HEAD_PAD), hd(HEAD_PAD), hd(D_V), row(H * HEAD_PAD), row(H * HEAD_PAD),
                  pl.BlockSpec((tm, 128), lambda i: (i, rope_blk)), row(128), row(128), row(128), vec, vec],
        out_specs=[row(H * HEAD_PAD), row(H * HEAD_PAD), row(128), vec, vec],
        out_shape=[jax.ShapeDtypeStruct((T, H * HEAD_PAD), BF16)] * 2 + [jax.ShapeDtypeStruct((T, 128), F32)]
        + [jax.ShapeDtypeStruct((1, HEAD_PAD), F32)] * 2,
        compiler_params=_cparams(("arbitrary",)),
    )(dQ, dK, dV, q_raw, kv_raw, lat, *tabs, q_gain, k_gain)


def _causal_probs(q, k, scale, row0):
    s = lax.dot_general(q, k, NT_DIMS, preferred_element_type=F32) * scale
    row = row0 + lax.broadcasted_iota(jnp.int32, s.shape, 0)
    col = lax.broadcasted_iota(jnp.int32, s.shape, 1)
    s = jnp.where(col <= row, s, NEG_BIG)
    p = jnp.exp(s - jnp.max(s, axis=-1, keepdims=True))
    return p, jnp.sum(p, axis=-1, keepdims=True)


def attn_fwd(name, Q, K, V, tq=512):
    H, T, E = Q.shape
    tq = min(tq, T)
    nq = T // tq
    scale = QK_DIM ** -0.5

    def body(q_ref, k_ref, v_ref, o_ref):
        i = pl.program_id(1)
        for ib in range(nq):
            @pl.when(i == ib)
            def _():
                n = (ib + 1) * tq
                p, l = _causal_probs(q_ref[...], k_ref[0:n, :], scale, ib * tq)
                o = jnp.dot(p.astype(BF16), v_ref[0:n, :], preferred_element_type=F32)
                o_ref[...] = (o / l).astype(o_ref.dtype)

    return pl.pallas_call(
        body, name=name, grid=(H, nq),
        in_specs=[pl.BlockSpec((None, tq, E), lambda h, i: (h, i, 0)),
                  pl.BlockSpec((None, T, E), lambda h, i: (h, 0, 0)),
                  pl.BlockSpec((None, T, D_V), lambda h, i: (h, 0, 0))],
        out_specs=pl.BlockSpec((tq, D_V), lambda h, i: (i, h)),
        out_shape=jax.ShapeDtypeStruct((T, H * D_V), BF16),
        compiler_params=_cparams(("parallel", "arbitrary")),
    )(Q, K, V)


def attn_bwd(name, Q, K, V, dO, tq=512):
    H, T, E = Q.shape
    tq = min(tq, T)
    nq = T // tq
    scale = QK_DIM ** -0.5

    def body(q_ref, k_ref, v_ref, do_ref, dq_ref, dk_ref, dv_ref):
        i = pl.program_id(1)

        @pl.when(i == 0)
        def _():
            dk_ref[...] = jnp.zeros_like(dk_ref)
            dv_ref[...] = jnp.zeros_like(dv_ref)

        for ib in range(nq):
            @pl.when(i == ib)
            def _():
                n = (ib + 1) * tq
                q, k, v, do = q_ref[...], k_ref[0:n, :], v_ref[0:n, :], do_ref[...]
                p, l = _causal_probs(q, k, scale, ib * tq)
                p = p / l
                dp = lax.dot_general(do, v, NT_DIMS, preferred_element_type=F32)
                ds = p * (dp - jnp.sum(p * dp, axis=-1, keepdims=True)) * scale
                dsb, pb = ds.astype(BF16), p.astype(BF16)
                dq_ref[...] = jnp.dot(dsb, k, preferred_element_type=F32)
                dk_ref[0:n, :] += lax.dot_general(dsb, q, TN_DIMS, preferred_element_type=F32)
                dv_ref[0:n, :] += lax.dot_general(pb, do, TN_DIMS, preferred_element_type=F32)

    return pl.pallas_call(
        body, name=name, grid=(H, nq),
        in_specs=[pl.BlockSpec((None, tq, E), lambda h, i: (h, i, 0)),
                  pl.BlockSpec((None, T, E), lambda h, i: (h, 0, 0)),
                  pl.BlockSpec((None, T, D_V), lambda h, i: (h, 0, 0)),
                  pl.BlockSpec((tq, D_V), lambda h, i: (i, h))],
        out_specs=[pl.BlockSpec((None, tq, E), lambda h, i: (h, i, 0)),
                   pl.BlockSpec((None, T, E), lambda h, i: (h, 0, 0)),
                   pl.BlockSpec((None, T, D_V), lambda h, i: (h, 0, 0))],
        out_shape=[jax.ShapeDtypeStruct((H, T, E), F32)] * 2 + [jax.ShapeDtypeStruct((H, T, D_V), F32)],
        compiler_params=_cparams(("parallel", "arbitrary")),
    )(Q, K, V, dO)


def _dw_specs(T, C, tm, tc, halo):
    cur = pl.BlockSpec((tm, tc), lambda j, i: (i, j))
    last = T // tm - 1
    if halo == "prev":
        nbr = pl.BlockSpec((tm, tc), lambda j, i: (jnp.maximum(i - 1, 0), j))
    else:
        nbr = pl.BlockSpec((tm, tc), lambda j, i: (jnp.minimum(i + 1, last), j))
    return cur, nbr


def dwconv_fwd(name, u, w, b, tm=256, tc=512, rs=32):
    T, C = u.shape
    tm, tc = min(tm, T), min(tc, C)
    cur, prev = _dw_specs(T, C, tm, tc, "prev")

    def body(up_ref, uc_ref, w_ref, b_ref, o_ref, scr):
        i = pl.program_id(1)

        @pl.when(i == 0)
        def _():
            scr[pl.ds(0, tm), :] = jnp.zeros((tm, tc), F32)

        @pl.when(i > 0)
        def _():
            scr[pl.ds(0, tm), :] = up_ref[...]

        scr[pl.ds(tm, tm), :] = uc_ref[...]
        for s in range(tm // rs):
            acc = jnp.broadcast_to(b_ref[...], (rs, tc))
            for k in range(CONV_WIDTH):
                acc = acc + w_ref[pl.ds(k, 1), :] * scr[pl.ds(tm - (CONV_WIDTH - 1) + k + rs * s, rs), :]
            o_ref[pl.ds(rs * s, rs), :] = acc

    return pl.pallas_call(
        body, name=name, grid=(C // tc, T // tm),
        in_specs=[prev, cur, pl.BlockSpec((CONV_TAPS_PAD, tc), lambda j, i: (0, j)), pl.BlockSpec((1, tc), lambda j, i: (0, j))],
        out_specs=cur, out_shape=jax.ShapeDtypeStruct((T, C), F32),
        scratch_shapes=[pltpu.VMEM((2 * tm, tc), F32)], compiler_params=_cparams(("parallel", "arbitrary")),
    )(u, u, w, b)


def dwconv_bwd_u(name, dy, w, tm=256, tc=512, rs=32):
    T, C = dy.shape
    tm, tc = min(tm, T), min(tc, C)
    cur, nxt = _dw_specs(T, C, tm, tc, "next")
    last = T // tm - 1

    def body(dc_ref, dn_ref, w_ref, o_ref, scr):
        i = pl.program_id(1)
        scr[pl.ds(0, tm), :] = dc_ref[...]

        @pl.when(i == last)
        def _():
            scr[pl.ds(tm, tm), :] = jnp.zeros((tm, tc), F32)

        @pl.when(i < last)
        def _():
            scr[pl.ds(tm, tm), :] = dn_ref[...]

        for s in range(tm // rs):
            acc = jnp.zeros((rs, tc), F32)
            for k in range(CONV_WIDTH):
                acc = acc + w_ref[pl.ds(k, 1), :] * scr[pl.ds((CONV_WIDTH - 1) - k + rs * s, rs), :]
            o_ref[pl.ds(rs * s, rs), :] = acc

    return pl.pallas_call(
        body, name=name, grid=(C // tc, T // tm),
        in_specs=[cur, nxt, pl.BlockSpec((CONV_TAPS_PAD, tc), lambda j, i: (0, j))],
        out_specs=cur, out_shape=jax.ShapeDtypeStruct((T, C), F32),
        scratch_shapes=[pltpu.VMEM((2 * tm, tc), F32)], compiler_params=_cparams(("parallel", "arbitrary")),
    )(dy, dy, w)


def dwconv_bwd_w(name, u, dy, tm=256, tc=512, rs=32):
    T, C = u.shape
    tm, tc = min(tm, T), min(tc, C)
    cur, prev = _dw_specs(T, C, tm, tc, "prev")

    def body(up_ref, uc_ref, dy_ref, o_ref, scr):
        i = pl.program_id(1)

        @pl.when(i == 0)
        def _():
            scr[pl.ds(0, tm), :] = jnp.zeros((tm, tc), F32)
            o_ref[...] = jnp.zeros_like(o_ref)

        @pl.when(i > 0)
        def _():
            scr[pl.ds(0, tm), :] = up_ref[...]

        scr[pl.ds(tm, tm), :] = uc_ref[...]
        for k in range(CONV_WIDTH):
            acc = jnp.zeros((rs, tc), F32)
            for s in range(tm // rs):
                acc = acc + dy_ref[pl.ds(rs * s, rs), :] * scr[pl.ds(tm - (CONV_WIDTH - 1) + k + rs * s, rs), :]
            o_ref[pl.ds(k, 1), :] += _colsum(acc)

    return pl.pallas_call(
        body, name=name, grid=(C // tc, T // tm),
        in_specs=[prev, cur, cur],
        out_specs=pl.BlockSpec((CONV_TAPS_PAD, tc), lambda j, i: (0, j)),
        out_shape=jax.ShapeDtypeStruct((CONV_TAPS_PAD, C), F32),
        scratch_shapes=[pltpu.VMEM((2 * tm, tc), F32)], compiler_params=_cparams(("parallel", "arbitrary")),
    )(u, u, dy)


def _place():
    x, y, c = lax.axis_index("x"), lax.axis_index("y"), lax.axis_index("c")
    return x, y, c


def _other_chips(x, y):
    return [(1 - x, y, 2 * (1 - x) + y), (x, 1 - y, 2 * x + (1 - y)), (1 - x, 1 - y, 2 * (1 - x) + (1 - y))]


def _hbm_specs(n):
    return [pl.BlockSpec(memory_space=pl.ANY)] * n


HBM_SPEC = pl.BlockSpec(memory_space=pltpu.HBM)
SEM_SPEC = pl.BlockSpec(memory_space=pltpu.SEMAPHORE)
ANY_SPEC = pl.BlockSpec(memory_space=pl.ANY)
SIDE_EFFECT = pltpu.SideEffectType.DATAFLOW_SIDE_EFFECTING


def _half(ref, slot, which):
    rh = ref.shape[1] // 2
    return ref.at[slot, pl.ds(pl.multiple_of(which * rh, 16), rh), :]


def _hbm(a):
    return pltpu.with_memory_space_constraint(a, pltpu.HBM)


def gather_start(name, groups, after):
    flat = [b for g in groups for b in g]
    n, ng = len(flat), len(groups)

    def body(*refs):
        send, recv, out = refs[n + 1:n + 1 + ng], refs[n + 1 + ng:n + 1 + 2 * ng], refs[n + 1 + 2 * ng:]
        x, y, c = _place()
        me = 2 * x + y
        a = 0
        for g, grp in enumerate(groups):
            for k in range(len(grp)):
                piece = _half(out[a], me, c)
                for j, (px, py, _) in enumerate(_other_chips(x, y)):
                    pltpu.make_async_remote_copy(
                        src_ref=piece, dst_ref=piece, send_sem=send[g].at[3 * k + j], recv_sem=recv[g].at[3 * k + j],
                        device_id=(px, py, c), device_id_type=MESH).start()
                a += 1

    sems = [pltpu.SemaphoreType.DMA((3 * len(g),)) for g in groups]
    res = pl.pallas_call(
        body, name=name, in_specs=[HBM_SPEC] * n + [ANY_SPEC], out_specs=[SEM_SPEC] * (2 * ng) + [HBM_SPEC] * n,
        out_shape=sems + sems + [pltpu.HBM(b.shape, b.dtype) for b in flat],
        input_output_aliases={a: 2 * ng + a for a in range(n)},
        compiler_params=pltpu.CompilerParams(has_side_effects=SIDE_EFFECT),
    )(*[_hbm(b) for b in flat], after)
    send, recv, bufs = res[:ng], res[ng:2 * ng], list(res[2 * ng:])
    out, a = [], 0
    for g, grp in enumerate(groups):
        out.append((send[g], recv[g], bufs[a:a + len(grp)]))
        a += len(grp)
    return out


def gather_relay(name, started, after):
    send1, recv1, bufs = started
    n = len(bufs)

    def body(*refs):
        s1, r1 = refs[n], refs[n + 1]
        s2, r2, out, token = refs[n + 3], refs[n + 4], refs[n + 5:2 * n + 5], refs[2 * n + 5]
        x, y, c = _place()
        me = 2 * x + y
        chips = _other_chips(x, y)
        for k in range(n):
            for j, (px, py, idx) in enumerate(chips):
                cp = pltpu.make_async_remote_copy(
                    src_ref=_half(out[k], me, c), dst_ref=_half(out[k], idx, c), send_sem=s1.at[3 * k + j],
                    recv_sem=r1.at[3 * k + j], device_id=(px, py, c), device_id_type=MESH)
                cp.wait_send()
                cp.wait_recv()
        for k in range(n):
            for j, (px, py, idx) in enumerate(chips):
                piece = _half(out[k], idx, c)
                pltpu.make_async_remote_copy(
                    src_ref=piece, dst_ref=piece, send_sem=s2.at[3 * k + j], recv_sem=r2.at[3 * k + j],
                    device_id=(x, y, 1 - c), device_id_type=MESH).start()
        token[...] = jnp.zeros_like(token)

    sem = pltpu.SemaphoreType.DMA((3 * n,))
    res = pl.pallas_call(
        body, name=name, in_specs=[HBM_SPEC] * n + [SEM_SPEC, SEM_SPEC, ANY_SPEC],
        out_specs=[SEM_SPEC, SEM_SPEC] + [HBM_SPEC] * n + [pl.BlockSpec(memory_space=pltpu.VMEM)],
        out_shape=[sem, sem] + [pltpu.HBM(b.shape, b.dtype) for b in bufs] + [jax.ShapeDtypeStruct((8, 128), F32)],
        input_output_aliases={a: 2 + a for a in range(n)},
        compiler_params=pltpu.CompilerParams(has_side_effects=SIDE_EFFECT),
    )(*bufs, send1, recv1, after)
    return res[0], res[1], list(res[2:2 + n]), res[2 + n]


def gather_wait(name, relayed, after):
    send2, recv2, bufs, _ = relayed
    n = len(bufs)

    def body(*refs):
        s2, r2, out = refs[n], refs[n + 1], refs[n + 3:]
        x, y, c = _place()
        for k in range(n):
            for j, (px, py, idx) in enumerate(_other_chips(x, y)):
                cp = pltpu.make_async_remote_copy(
                    src_ref=_half(out[k], idx, c), dst_ref=_half(out[k], idx, 1 - c), send_sem=s2.at[3 * k + j],
                    recv_sem=r2.at[3 * k + j], device_id=(x, y, 1 - c), device_id_type=MESH)
                cp.wait_send()
                cp.wait_recv()

    res = pl.pallas_call(
        body, name=name, in_specs=[HBM_SPEC] * n + [SEM_SPEC, SEM_SPEC, ANY_SPEC], out_specs=[HBM_SPEC] * n,
        out_shape=[pltpu.HBM(b.shape, b.dtype) for b in bufs], input_output_aliases={a: a for a in range(n)},
        compiler_params=pltpu.CompilerParams(has_side_effects=SIDE_EFFECT),
    )(*bufs, send2, recv2, after)
    return list(res)


def split_start(name, bufs, land_shapes, n_copies, plan, after):
    nb, nl = len(bufs), len(land_shapes)

    def body(*refs):
        send, recv = refs[nb + 1], refs[nb + 2]
        out, lands, token = refs[nb + 3:2 * nb + 3], refs[2 * nb + 3:2 * nb + 3 + nl], refs[2 * nb + 3 + nl]
        x, y, c = _place()
        for k, (src, dst, to, _) in enumerate(plan(out, lands, x, y, c)):
            pltpu.make_async_remote_copy(src_ref=src, dst_ref=dst, send_sem=send.at[k], recv_sem=recv.at[k],
                                         device_id=to, device_id_type=MESH).start()
        token[...] = jnp.zeros_like(token)

    sem = pltpu.SemaphoreType.DMA((n_copies,))
    res = pl.pallas_call(
        body, name=name, in_specs=[HBM_SPEC] * nb + [ANY_SPEC],
        out_specs=[SEM_SPEC, SEM_SPEC] + [HBM_SPEC] * (nb + nl) + [pl.BlockSpec(memory_space=pltpu.VMEM)],
        out_shape=[sem, sem] + [pltpu.HBM(b.shape, b.dtype) for b in bufs]
        + [pltpu.HBM(s, d) for s, d in land_shapes] + [jax.ShapeDtypeStruct((8, 128), F32)],
        input_output_aliases={a: 2 + a for a in range(nb)},
        compiler_params=pltpu.CompilerParams(has_side_effects=SIDE_EFFECT),
    )(*[_hbm(b) for b in bufs], after)
    return res[0], res[1], list(res[2:2 + nb]), list(res[2 + nb:2 + nb + nl]), res[2 + nb + nl]


def split_wait(name, started, plan, after):
    send, recv, bufs, lands, _ = started
    nb, nl = len(bufs), len(lands)

    def body(*refs):
        s, r = refs[nb + nl], refs[nb + nl + 1]
        out, lo = refs[nb + nl + 3:2 * nb + nl + 3], refs[2 * nb + nl + 3:]
        x, y, c = _place()
        for k, (src, _, to, landed) in enumerate(plan(out, lo, x, y, c)):
            cp = pltpu.make_async_remote_copy(src_ref=src, dst_ref=landed, send_sem=s.at[k], recv_sem=r.at[k],
                                              device_id=to, device_id_type=MESH)
            cp.wait_send()
            cp.wait_recv()

    res = pl.pallas_call(
        body, name=name, in_specs=[HBM_SPEC] * (nb + nl) + [SEM_SPEC, SEM_SPEC, ANY_SPEC],
        out_specs=[HBM_SPEC] * (nb + nl), out_shape=[pltpu.HBM(b.shape, b.dtype) for b in bufs + lands],
        input_output_aliases={a: a for a in range(nb + nl)},
        compiler_params=pltpu.CompilerParams(has_side_effects=SIDE_EFFECT),
    )(*bufs, *lands, send, recv, after)
    return list(res[:nb]), list(res[nb:])


def swap_halves_plan(parts, lands, x, y, c):
    out = []
    for a in range(len(parts)):
        rh = parts[a].shape[1] // 2
        theirs = parts[a].at[:, pl.ds(pl.multiple_of((1 - c) * rh, 16), rh), :]
        out.append((theirs, lands[a], (x, y, 1 - c), lands[a]))
    return out


def chip_exchange_plan(sums, lands, x, y, c):
    me = 2 * x + y
    out = []
    for a in range(len(sums)):
        for px, py, idx in _other_chips(x, y):
            out.append((sums[a].at[idx], lands[a].at[me], (px, py, c), lands[a].at[idx]))
    return out


def share_halves_plan(grads, lands, x, y, c):
    out = []
    for a in range(len(grads)):
        rh = grads[a].shape[0] // 2
        mine = grads[a].at[pl.ds(pl.multiple_of(c * rh, 8), rh), :]
        theirs = grads[a].at[pl.ds(pl.multiple_of((1 - c) * rh, 8), rh), :]
        out.append((mine, mine, (x, y, 1 - c), theirs))
    return out


def allreduce_pack(name, pack):
    R, W = pack.shape

    def body(p_ref, o_ref, sib, pair, got, send_sems, recv_sems):
        x, y, c = _place()

        def swap(k, src, dst, to):
            cp = pltpu.make_async_remote_copy(src_ref=src, dst_ref=dst, send_sem=send_sems.at[k],
                                              recv_sem=recv_sems.at[k], device_id=to, device_id_type=MESH)
            cp.start()
            return cp

        cp = swap(0, p_ref, sib, (x, y, 1 - c))
        cp.wait()
        pair[...] = p_ref[...] + sib[...]
        cps = [swap(1, pair, got.at[0], (1 - x, y, c)), swap(2, pair, got.at[1], (x, 1 - y, c)),
               swap(3, pair, got.at[2], (1 - x, 1 - y, c))]
        for cp in cps:
            cp.wait()
        o_ref[...] = (pair[...] + got[1]) + (got[0] + got[2])

    return pl.pallas_call(
        body, name=name, out_shape=jax.ShapeDtypeStruct((R, W), F32),
        in_specs=[pl.BlockSpec(memory_space=pltpu.VMEM)], out_specs=pl.BlockSpec(memory_space=pltpu.VMEM),
        scratch_shapes=[pltpu.VMEM((R, W), F32), pltpu.VMEM((R, W), F32), pltpu.VMEM((3, R, W), F32),
                        pltpu.SemaphoreType.DMA((4,)), pltpu.SemaphoreType.DMA((4,))],
        compiler_params=_cparams(),
    )(pack)


BIG_WEIGHTS = [
    ("ffn_a_w_in", "col"), ("ffn_a_w_out", "row"), ("ffn_b_w_in", "col"), ("ffn_b_w_out", "row"),
    ("mla_w_in", "row"), ("mla_w_uq", "col"), ("mla_w_ukv", "col"), ("mla_w_o", "row"),
    ("conv_w_pw1", "col"), ("conv_w_pw2", "row"), ("ple_w_proj", "col"), ("ple_w_gate", "row"),
]
WEIGHT_ORDER = ["ffn_a_norm", "ffn_a_w_in", "ffn_a_w_out", "ffn_b_norm", "ffn_b_w_in", "ffn_b_w_out", "mix_norm",
                "mla_w_in", "mla_q_lat_norm", "mla_kv_lat_norm", "mla_w_uq", "mla_w_ukv", "mla_q_gain", "mla_k_gain",
                "mla_w_o", "conv_w_pw1", "conv_b_pw1", "conv_w_dw", "conv_b_dw", "conv_ln_g", "conv_ln_b", "conv_w_pw2",
                "ple_w_proj", "ple_norm", "ple_gate_norm", "ple_w_gate"]
REPLICATED_SMALL = ["ffn_a_norm", "ffn_b_norm", "mix_norm", "ple_norm", "ple_gate_norm",
                    "mla_q_lat_norm", "mla_kv_lat_norm", "mla_q_gain", "mla_k_gain"]
SHARDED_SMALL = ["conv_b_pw1", "conv_w_dw", "conv_b_dw", "conv_ln_g", "conv_ln_b"]
PACK_ROWS = 8


def _pack_rows(arrs, width):
    out = []
    for a in arrs:
        r = -(-a.shape[0] // PACK_ROWS) * PACK_ROWS
        out.append(jnp.pad(a, ((0, r - a.shape[0]), (0, width - a.shape[1]))))
    return jnp.concatenate(out, axis=0)


def _unpack_rows(pack, shapes):
    out, r0 = [], 0
    for (r, w) in shapes:
        out.append(pack[r0:r0 + r, :w])
        r0 += -(-r // PACK_ROWS) * PACK_ROWS
    return out


def kernel(x, p, positions, ffn_a_norm, ffn_a_w_in, ffn_a_w_out, ffn_b_norm, ffn_b_w_in, ffn_b_w_out, mix_norm, mla_w_in, mla_q_lat_norm, mla_kv_lat_norm, mla_w_uq, mla_w_ukv, mla_q_gain, mla_k_gain, mla_w_o, conv_w_pw1, conv_b_pw1, conv_w_dw, conv_b_dw, conv_ln_g, conv_ln_b, conv_w_pw2, ple_w_proj, ple_norm, ple_gate_norm, ple_w_gate, loss_target, m_ffn_a_norm, m_ffn_a_w_in, m_ffn_a_w_out, m_ffn_b_norm, m_ffn_b_w_in, m_ffn_b_w_out, m_mix_norm, m_mla_w_in, m_mla_q_lat_norm, m_mla_kv_lat_norm, m_mla_w_uq, m_mla_w_ukv, m_mla_q_gain, m_mla_k_gain, m_mla_w_o, m_conv_w_pw1, m_conv_b_pw1, m_conv_w_dw, m_conv_b_dw, m_conv_ln_g, m_conv_ln_b, m_conv_w_pw2, m_ple_w_proj, m_ple_norm, m_ple_gate_norm, m_ple_w_gate, v_ffn_a_norm, v_ffn_a_w_in, v_ffn_a_w_out, v_ffn_b_norm, v_ffn_b_w_in, v_ffn_b_w_out, v_mix_norm, v_mla_w_in, v_mla_q_lat_norm, v_mla_kv_lat_norm, v_mla_w_uq, v_mla_w_ukv, v_mla_q_gain, v_mla_k_gain, v_mla_w_o, v_conv_w_pw1, v_conv_b_pw1, v_conv_w_dw, v_conv_b_dw, v_conv_ln_g, v_conv_ln_b, v_conv_w_pw2, v_ple_w_proj, v_ple_norm, v_ple_gate_norm, v_ple_w_gate):
    args = dict(locals())
    W = {n: args[n] for n in WEIGHT_ORDER}
    M1 = {n: args["m_" + n] for n in WEIGHT_ORDER}
    V2 = {n: args["v_" + n] for n in WEIGHT_ORDER}

    T, D = x.shape[1], x.shape[2]
    depth = ffn_a_norm.shape[0]
    H = mla_w_ukv.shape[2] * N_CHIPS // (D_NOPE + D_V)
    QL, KL = mla_q_lat_norm.shape[1], mla_kv_lat_norm.shape[1]
    C = conv_w_pw2.shape[1] * N_CHIPS
    lat_w = QL + KL + D_ROPE
    lat_pad = QL + KL + 128

    cx, cy, cc = lax.axis_index("x"), lax.axis_index("y"), lax.axis_index("c")
    chip = (2 * cx + cy).astype(jnp.int32)
    chip_arr = chip.reshape(1)
    core_arr = cc.astype(jnp.int32).reshape(1)

    def stage_groups(i):
        mix = ([("mla_w_in", i // 2), ("mla_w_uq", i // 2), ("mla_w_ukv", i // 2), ("mla_w_o", i // 2)] if i % 2 == 0
               else [("conv_w_pw1", i // 2), ("conv_w_pw2", i // 2)])
        return [[("ffn_a_w_in", i)], [("ffn_a_w_out", i)], mix, [("ffn_b_w_in", i)], [("ffn_b_w_out", i)],
                [("ple_w_proj", i), ("ple_w_gate", i)]]

    groups = [g for i in range(depth) for g in stage_groups(i)]
    A_IN, A_OUT, MIX, B_IN, B_OUT, PLE, PER_LAYER = 0, 1, 2, 3, 4, 5, 6

    def slot_of(key):
        n, l = key
        return cast_into_slot(f"cast_{n}_{l}", W[n].reshape(-1, W[n].shape[-1]), chip_arr, l, W[n].shape[0])

    def placed(a, width):
        full = jnp.zeros(a.shape[:-1] + (width,), F32)
        full = lax.dynamic_update_slice_in_dim(full, a, chip * a.shape[-1], axis=a.ndim - 1)
        return full * (cc == 0).astype(F32)

    b_pw1_sh = conv_b_pw1.reshape(1, -1)
    small_in = [placed(b_pw1_sh, 2 * C).reshape(2, C), placed(conv_w_dw[0], C), placed(conv_b_dw, C),
                placed(conv_ln_g, C), placed(conv_ln_b, C)]
    small_pack = allreduce_pack("gather_small", _pack_rows(small_in, C))
    small_full = _unpack_rows(small_pack, [(2, C), (CONV_WIDTH, C), (1, C), (1, C), (1, C)])

    started = gather_start("gather_start", [[slot_of(k) for k in g] for g in groups], small_pack)
    relayed, G = {}, {}

    def relay(g, after):
        if g >= len(groups):
            return None
        relayed[g] = gather_relay(f"gather_relay_{g}", started[g], after)
        return relayed[g][3]

    def ready(g, after):
        if g not in relayed:
            relay(g, after)
        for key, buf in zip(groups[g], gather_wait(f"gather_wait_{g}", relayed[g], after)):
            G[key] = buf

    b_pw1_full = small_full[0].reshape(1, 2 * C)
    w_dw_full = jnp.pad(small_full[1], ((0, CONV_TAPS_PAD - CONV_WIDTH), (0, 0)))
    b_dw_full, ln_g_full, ln_b_full = small_full[2], small_full[3], small_full[4]

    pad_gain = lambda g: jnp.pad(g, ((0, 0), (0, HEAD_PAD - QK_DIM)))
    q_gain_p, k_gain_p = pad_gain(mla_q_gain), pad_gain(mla_k_gain)
    tabs = rope_tables("rope_tables", positions.reshape(T, 1).astype(F32))

    def ffn_fwd(tag, h, norm, w_in, w_out, layer, g_in):
        hn = rms_fwd(f"{tag}_rms", h, norm)
        ready(g_in, hn)
        tok = relay(g_in + 1, hn)
        gu = mm_nn(f"{tag}_in", hn, G[(w_in, layer)], "col", 0, out_dtype=BF16, tm=2048, after=tok)
        act = swiglu_fwd(f"{tag}_act", gu)
        ready(g_in + 1, act)
        tok = relay(g_in + 2, act)
        out = mm_nn(f"{tag}_out", act, G[(w_out, layer)], "row", 0, out_dtype=F32, scale=FFN_RESIDUAL_WEIGHT, res=h,
                    tm=1024, tn=1024, tk=1408, after=tok)
        return out, (h, hn, gu, act)

    saved = []
    h = x[0]
    for i in range(depth):
        L = {}
        g0 = PER_LAYER * i
        h, L["ffn_a"] = ffn_fwd(f"l{i}_ffa", h, ffn_a_norm[i:i + 1], "ffn_a_w_in", "ffn_a_w_out", i, g0 + A_IN)
        L["h1"] = h
        hn = rms_fwd(f"l{i}_mix_rms", h, mix_norm[i:i + 1])
        L["hn_m"] = hn
        ready(g0 + MIX, hn)
        j = i // 2
        if i % 2 == 0:
            w_in_pad = jnp.pad(G[("mla_w_in", j)].reshape(D, lat_w), ((0, 0), (0, lat_pad - lat_w)))[None]
            uq = G[("mla_w_uq", j)].transpose(1, 0, 2).reshape(QL, H, QK_DIM)
            w_uq_pad = jnp.pad(uq, ((0, 0), (0, 0), (0, HEAD_PAD - QK_DIM))).reshape(1, QL, H * HEAD_PAD)
            lat = mm_nn(f"l{i}_lat", hn, w_in_pad, "row", 0, out_dtype=F32, tm=2048)
            cq, ckv = lat_norm_fwd(f"l{i}_latnorm", lat, mla_q_lat_norm[j:j + 1], mla_kv_lat_norm[j:j + 1])
            q_raw = mm_nn(f"l{i}_uq", cq, w_uq_pad, "row", 0, out_dtype=F32, tm=2048)
            kv_raw = mm_nn(f"l{i}_ukv", ckv, G[("mla_w_ukv", j)], "col", 0, out_dtype=F32, tm=2048)
            Qh, Kh, Vh = mla_prep_fwd(f"l{i}_prep", q_raw, kv_raw, lat, tabs, q_gain_p, k_gain_p, QL + KL)
            O = attn_fwd(f"l{i}_attn", Qh, Kh, Vh)
            tok = relay(g0 + B_IN, O)
            h = mm_nn(f"l{i}_wo", O, G[("mla_w_o", j)], "row", 0, out_dtype=F32, res=h, tm=1024, tn=1024, after=tok)
            L["mla"] = (lat, cq, ckv, q_raw, kv_raw, Qh, Kh, Vh, O, w_in_pad, w_uq_pad)
        else:
            ag = mm_nn(f"l{i}_pw1", hn, G[("conv_w_pw1", j)], "col", 0, out_dtype=F32, bias=b_pw1_full, tm=2048)
            u = glu_fwd(f"l{i}_glu", ag)
            yc = dwconv_fwd(f"l{i}_dw", u, w_dw_full, b_dw_full)
            cact = ln_silu_fwd(f"l{i}_ln", yc, ln_g_full, ln_b_full)
            tok = relay(g0 + B_IN, cact)
            h = mm_nn(f"l{i}_pw2", cact, G[("conv_w_pw2", j)], "row", 0, out_dtype=F32, res=h, tm=1024, tn=1024,
                      after=tok)
            L["conv"] = (ag, u, yc, cact)
        L["h2"] = h
        h, L["ffn_b"] = ffn_fwd(f"l{i}_ffb", h, ffn_b_norm[i:i + 1], "ffn_b_w_in", "ffn_b_w_out", i, g0 + B_IN)
        L["h3"] = h
        ready(g0 + PLE, h)
        pe = mm_nn(f"l{i}_ple_proj", p[i, 0], G[("ple_w_proj", i)], "col", 0, out_dtype=F32, tm=2048)
        hg = rms_fwd(f"l{i}_gate_rms", h, ple_gate_norm[i:i + 1])
        tok = relay(g0 + PER_LAYER, hg)
        z = mm_nn(f"l{i}_ple_gate", hg, G[("ple_w_gate", i)], "row", 0, out_dtype=F32, tm=2048, after=tok)
        h = ple_fwd(f"l{i}_ple", h, pe, z, ple_norm[i:i + 1])
        L["ple"] = (pe, hg, z)
        saved.append(L)

    d_h, loss_part = loss_head("loss_head", h, loss_target[0])
    loss = lax.psum(loss_part[0, 0], ("x", "y", "c"))

    GW = {}
    SG = {}
    ids_arr = jnp.stack([cc.astype(jnp.int32), chip])
    two = lambda a: a.reshape(-1, a.shape[-1])
    merged = {}
    pipe = {}
    order = list(reversed(range(len(groups))))
    ticks = [0]

    def put_small(name, i, val):
        SG.setdefault(name, {})[i] = val

    def reduce_tick(after):
        k, tok = ticks[0], after
        ticks[0] += 1
        grp = lambda j: order[j] if 0 <= j < len(order) else None
        g = grp(k - 4)
        if g is not None:
            full, _ = split_wait(f"share_wait_{g}", pipe[g], share_halves_plan, tok)
            for (n, l), gr in zip(groups[g], full):
                merged[n] = adamw_layer(f"adamw_{n}_{l}", two(W[n]), gr, two(M1[n]), two(V2[n]), l, W[n].shape[0],
                                        merged.get(n))
        g = grp(k - 3)
        if g is not None:
            sums, landed = split_wait(f"exchange_wait_{g}", pipe[g], chip_exchange_plan, tok)
            halves = [sum_chips(f"chip_sum_{n}_{l}", ids_arr, s, ld, 2 * s.shape[1])
                      for (n, l), s, ld in zip(groups[g], sums, landed)]
            pipe[g] = split_start(f"share_start_{g}", halves, [], len(halves), share_halves_plan, tok)
            tok = pipe[g][4]
        g = grp(k - 1)
        if g is not None:
            parts, got = split_wait(f"swap_wait_{g}", pipe[g], swap_halves_plan, tok)
            sums = [add_halves(f"pair_sum_{n}_{l}", core_arr, q, r) for (n, l), q, r in zip(groups[g], parts, got)]
            pipe[g] = split_start(f"exchange_start_{g}", sums, [(s.shape, s.dtype) for s in sums], 3 * len(sums),
                                  chip_exchange_plan, tok)
            tok = pipe[g][4]
        g = grp(k)
        if g is not None:
            parts = [GW[key] for key in groups[g]]
            lands = [((q.shape[0], q.shape[1] // 2, q.shape[2]), q.dtype) for q in parts]
            pipe[g] = split_start(f"swap_start_{g}", parts, lands, len(parts), swap_halves_plan, tok)
            tok = pipe[g][4]
        return tok

    def ffn_bwd(tag, d_h, d_hb, norm, w_in, w_out, layer, fw, tok):
        h_in, hn, gu, act = fw
        GW[(w_out, layer)] = mm_tn(f"{tag}_dwout", act, d_hb, "row", scale=FFN_RESIDUAL_WEIGHT, tk=1408, tn=1024,
                                   after=tok)
        tok = reduce_tick(GW[(w_out, layer)])
        d_act = _mm_nt(f"{tag}_dact", d_hb, G[(w_out, layer)], "row", 0, act.shape[1], out_dtype=BF16,
                       scale=FFN_RESIDUAL_WEIGHT, to=1408, tc=2048, after=tok)
        dgu = swiglu_bwd(f"{tag}_dgu", gu, d_act)
        GW[(w_in, layer)] = mm_tn(f"{tag}_dwin", hn, dgu, "col")
        tok = reduce_tick(GW[(w_in, layer)])
        d_hn = _mm_nt(f"{tag}_dhn", dgu, G[(w_in, layer)], "col", 0, D, out_dtype=F32, after=tok)
        return (*rms_bwd_res(f"{tag}_drms", h_in, d_hn, norm, d_h), tok)

    d_hb, tok = None, None
    for i in reversed(range(depth)):
        L = saved[i]
        j = i // 2
        pe, hg, z = L["ple"]
        d_z, d_pe, g = ple_bwd(f"l{i}_dple", d_h, pe, z, ple_norm[i:i + 1])
        put_small("ple_norm", i, g)
        d_hg = _mm_nt(f"l{i}_dhg", d_z, G[("ple_w_gate", i)], "row", 0, D, out_dtype=F32, to=512, tc=2048, after=tok)
        GW[("ple_w_gate", i)] = mm_tn(f"l{i}_dwgate", hg, d_z, "row", tk=512, tn=1024)
        GW[("ple_w_proj", i)] = mm_tn(f"l{i}_dwproj", p[i, 0], d_pe, "col")
        tok = reduce_tick(GW[("ple_w_proj", i)])
        d_h, d_hb, g = rms_bwd_res(f"l{i}_dgate_rms", L["h3"], d_hg, ple_gate_norm[i:i + 1], d_h)
        put_small("ple_gate_norm", i, g)

        d_h, d_hb, g, tok = ffn_bwd(f"l{i}_ffb", d_h, d_hb, ffn_b_norm[i:i + 1], "ffn_b_w_in", "ffn_b_w_out", i,
                                    L["ffn_b"], tok)
        put_small("ffn_b_norm", i, g)

        hn = L["hn_m"]
        if i % 2 == 0:
            lat, cq, ckv, q_raw, kv_raw, Qh, Kh, Vh, O, w_in_pad, w_uq_pad = L["mla"]
            d_O = _mm_nt(f"l{i}_dO", d_hb, G[("mla_w_o", j)], "row", 0, H * D_V, out_dtype=BF16, to=512, tc=2048,
                         after=tok)
            GW[("mla_w_o", j)] = mm_tn(f"l{i}_dwo", O, d_hb, "row", tk=512, tn=1024)
            dQ, dK, dV = attn_bwd(f"l{i}_dattn", Qh, Kh, Vh, d_O)
            d_q_raw, d_kv_raw, d_kr, gq, gk = mla_prep_bwd(f"l{i}_dprep", dQ, dK, dV, q_raw, kv_raw, lat, tabs,
                                                           q_gain_p, k_gain_p, QL + KL)
            put_small("mla_q_gain", j, gq[:, :QK_DIM])
            put_small("mla_k_gain", j, gk[:, :QK_DIM])
            d_cq = _mm_nt(f"l{i}_dcq", d_q_raw, w_uq_pad, "row", 0, QL, out_dtype=F32, to=512, tc=2048)
            g_uq = mm_tn(f"l{i}_dwuq", cq, d_q_raw, "row", shards=1, out_dtype=F32, tk=512, tn=1024)
            g_uq = g_uq.reshape(QL, H, HEAD_PAD)[:, :, :QK_DIM].reshape(QL, N_CHIPS, -1).transpose(1, 0, 2)
            GW[("mla_w_uq", j)] = g_uq.astype(BF16)
            d_ckv = _mm_nt(f"l{i}_dckv", d_kv_raw, G[("mla_w_ukv", j)], "col", 0, KL, out_dtype=F32, to=512, tc=1024)
            GW[("mla_w_ukv", j)] = mm_tn(f"l{i}_dwukv", ckv, d_kv_raw, "col", tk=512, tn=1024)
            d_lat, gq, gk = lat_norm_bwd(f"l{i}_dlatnorm", lat, d_cq, d_ckv, d_kr, mla_q_lat_norm[j:j + 1],
                                         mla_kv_lat_norm[j:j + 1])
            put_small("mla_q_lat_norm", j, gq)
            put_small("mla_kv_lat_norm", j, gk)
            d_hn = _mm_nt(f"l{i}_dhn_lat", d_lat, w_in_pad, "row", 0, D, out_dtype=F32, to=1024, tc=lat_pad)
            g_in = mm_tn(f"l{i}_dwin_lat", hn, d_lat, "row", shards=1, out_dtype=F32, tk=1024, tn=lat_pad)
            GW[("mla_w_in", j)] = g_in[0, :, :lat_w].reshape(N_CHIPS, D // N_CHIPS, lat_w).astype(BF16)
            tok = reduce_tick(GW[("mla_w_in", j)])
        else:
            ag, u, yc, cact = L["conv"]
            d_cact = _mm_nt(f"l{i}_dcact", d_hb, G[("conv_w_pw2", j)], "row", 0, C, out_dtype=F32, to=512, tc=2048,
                            after=tok)
            GW[("conv_w_pw2", j)] = mm_tn(f"l{i}_dwpw2", cact, d_hb, "row", tk=512, tn=1024)
            d_yc, g1, g2, g3 = ln_silu_bwd(f"l{i}_dln", yc, d_cact, ln_g_full, ln_b_full)
            put_small("conv_ln_g", j, g1)
            put_small("conv_ln_b", j, g2)
            put_small("conv_b_dw", j, g3)
            d_u = dwconv_bwd_u(f"l{i}_ddw_u", d_yc, w_dw_full)
            put_small("conv_w_dw", j, dwconv_bwd_w(f"l{i}_ddw_w", u, d_yc))
            d_ag, g = glu_bwd(f"l{i}_dglu", ag, d_u)
            put_small("conv_b_pw1", j, g)
            d_hn = _mm_nt(f"l{i}_dhn_pw1", d_ag, G[("conv_w_pw1", j)], "col", 0, D, out_dtype=F32, tc=1024)
            GW[("conv_w_pw1", j)] = mm_tn(f"l{i}_dwpw1", hn, d_ag, "col", tn=1024)
            tok = reduce_tick(GW[("conv_w_pw1", j)])
        d_h, d_hb, g = rms_bwd_res(f"l{i}_dmix_rms", L["h1"], d_hn, mix_norm[i:i + 1], d_h)
        put_small("mix_norm", i, g)

        d_h, d_hb, g, tok = ffn_bwd(f"l{i}_ffa", d_h, d_hb, ffn_a_norm[i:i + 1], "ffn_a_w_in", "ffn_a_w_out", i,
                                    L["ffn_a"], tok)
        put_small("ffn_a_norm", i, g)
    grad_x = d_h[None]

    tok = d_h
    for _ in range(4):
        tok = reduce_tick(tok)
    names = [n for n, _ in BIG_WEIGHTS]
    grads, delta, new_m, new_v = {}, {}, {}, {}
    for n in names:
        grads[n], delta[n], new_m[n], new_v[n] = [a.reshape(W[n].shape) for a in merged[n]]

    rep = []
    for n in REPLICATED_SMALL:
        rep.append(jnp.concatenate([SG[n][i] for i in sorted(SG[n])], axis=0))
    shd = [SG["conv_b_pw1"][0].reshape(2, C), SG["conv_w_dw"][0][:CONV_WIDTH], SG["conv_b_dw"][0],
           SG["conv_ln_g"][0], SG["conv_ln_b"][0]]
    red = allreduce_pack("allreduce_small", _pack_rows(rep + shd, D))
    red = _unpack_rows(red, [a.shape for a in rep + shd])
    for n, g in zip(REPLICATED_SMALL, red):
        grads[n] = g
    own = lambda a, w: lax.dynamic_slice_in_dim(a, chip * w, w, axis=a.ndim - 1)
    sh = red[len(rep):]
    grads["conv_b_pw1"] = own(sh[0].reshape(1, 2 * C), 2 * C // N_CHIPS)
    grads["conv_w_dw"] = own(sh[1], C // N_CHIPS)[None]
    grads["conv_b_dw"] = own(sh[2], C // N_CHIPS)
    grads["conv_ln_g"] = own(sh[3], C // N_CHIPS)
    grads["conv_ln_b"] = own(sh[4], C // N_CHIPS)

    small = REPLICATED_SMALL + SHARDED_SMALL
    shapes = [two(W[n]).shape for n in small]
    packs = [_pack_rows([two(src[n]) for n in small], D) for src in (W, grads, M1, V2)]
    outs = adamw("adamw_small", *packs)
    for dst, pk in zip((delta, new_m, new_v), outs):
        for n, a in zip(small, _unpack_rows(pk, shapes)):
            dst[n] = a.reshape(W[n].shape)
    for n in small:
        grads[n] = grads[n].reshape(W[n].shape)

    return (loss, grad_x, *[grads[n] for n in WEIGHT_ORDER], *[delta[n] for n in WEIGHT_ORDER],
            *[new_m[n] for n in WEIGHT_ORDER], *[new_v[n] for n in WEIGHT_ORDER])
```

```python
import functools
import math

import jax
import jax.numpy as jnp
from jax import lax
from jax.experimental import pallas as pl
from jax.experimental.pallas import tpu as pltpu

F32, BF16 = jnp.float32, jnp.bfloat16
MESH = pl.DeviceIdType.MESH

N_CHIPS = 4
EPS = 1e-6
D_NOPE, D_ROPE, D_V = 128, 64, 128
QK_DIM = D_NOPE + D_ROPE
HEAD_PAD = 256
ROPE_THETA = 10000.0
CONV_WIDTH = 31
CONV_TAPS_PAD = 32
FFN_RESIDUAL_WEIGHT = 0.5
ADAM_LR, ADAM_B1, ADAM_B2, ADAM_EPS, ADAM_WD, ADAM_STEP = 0.001, 0.9, 0.999, 1e-08, 0.01, 10
VMEM_LIMIT_BYTES = 56 * 1024 * 1024
NEG_BIG = -1e30

NN_DIMS = (((1,), (0,)), ((), ()))
NT_DIMS = (((1,), (1,)), ((), ()))
TN_DIMS = (((0,), (0,)), ((), ()))


def _cparams(semantics=None):
    kw = dict(vmem_limit_bytes=VMEM_LIMIT_BYTES)
    if semantics is not None:
        kw["dimension_semantics"] = semantics
    return pltpu.CompilerParams(**kw)


def _tile(n, pref, mult=128):
    if n <= pref:
        return n
    t = (pref // mult) * mult
    while t >= mult:
        if n % t == 0:
            return t
        t -= mult
    return n


def _rowwise(name, fn, rows, vecs, outs, accs=(), tm=256, rc=16):
    T = rows[0].shape[0]
    tm = min(tm, T)
    rc = min(rc, tm)
    nr, nv, no, na = len(rows), len(vecs), len(outs), len(accs)
    steps = tm // rc

    def body(*refs):
        row_refs = refs[:nr]
        vec_refs = refs[nr:nr + nv]
        out_refs = refs[nr + nv:nr + nv + no]
        acc_refs = refs[nr + nv + no:]
        if na:
            @pl.when(pl.program_id(0) == 0)
            def _():
                for a in acc_refs:
                    a[...] = jnp.zeros_like(a)

        def step(r, carry):
            sl = pl.ds(pl.multiple_of(r * rc, rc), rc)
            res = fn(*[x[sl, :] for x in row_refs], *[v[...] for v in vec_refs])
            for o, val in zip(out_refs, res[:no]):
                o[sl, :] = val.astype(o.dtype)
            return tuple(c + val for c, val in zip(carry, res[no:]))

        init = tuple(jnp.zeros(s, F32) for s in accs)
        tot = lax.fori_loop(0, steps, step, init)
        for a, val in zip(acc_refs, tot):
            a[...] += val

    in_specs = [pl.BlockSpec((tm, x.shape[1]), lambda i: (i, 0)) for x in rows]
    in_specs += [pl.BlockSpec(v.shape, lambda i: (0, 0)) for v in vecs]
    out_specs = [pl.BlockSpec((tm, d), lambda i: (i, 0)) for d, _ in outs]
    out_specs += [pl.BlockSpec(s, lambda i: (0, 0)) for s in accs]
    out_shape = [jax.ShapeDtypeStruct((T, d), dt) for d, dt in outs]
    out_shape += [jax.ShapeDtypeStruct(s, F32) for s in accs]
    return pl.pallas_call(
        body, name=name, grid=(T // tm,), in_specs=in_specs, out_specs=out_specs, out_shape=out_shape,
        compiler_params=_cparams(("arbitrary",)),
    )(*rows, *vecs)


def _colsum(v):
    return jnp.sum(v, axis=0, keepdims=True)


def _rstd(x):
    return lax.rsqrt(jnp.mean(x * x, axis=-1, keepdims=True) + EPS)


def _rms_bwd(x, dy, g):
    r = _rstd(x)
    xh = x * r
    dyg = dy * g
    dx = r * (dyg - xh * jnp.mean(dyg * xh, axis=-1, keepdims=True))
    return dx, dy * xh


def _sigmoid(x):
    return 1.0 / (1.0 + jnp.exp(-x))


def rms_fwd(name, h, g):
    def fn(x, gv):
        return ((x * _rstd(x)) * gv,)
    return _rowwise(name, fn, [h], [g], [(h.shape[1], BF16)])[0]


def rms_bwd_res(name, h, d_y, g, d_res):
    D = h.shape[1]

    def fn(x, dy, dr, gv):
        dx, dgr = _rms_bwd(x, dy, gv)
        dh = dr + dx
        return dh, dh, _colsum(dgr)
    return _rowwise(name, fn, [h, d_y, d_res], [g], [(D, F32), (D, BF16)], [(1, D)], tm=128)


def swiglu_fwd(name, gu):
    Fh = gu.shape[1] // 2

    def fn(v):
        g = v[:, :Fh].astype(F32)
        u = v[:, Fh:].astype(F32)
        return (g * _sigmoid(g) * u,)
    return _rowwise(name, fn, [gu], [], [(Fh, BF16)])[0]


def swiglu_bwd(name, gu, d_act):
    Fh = gu.shape[1] // 2

    def fn(v, da):
        g = v[:, :Fh].astype(F32)
        u = v[:, Fh:].astype(F32)
        da = da.astype(F32)
        s = _sigmoid(g)
        d_u = da * g * s
        d_g = da * u * s * (1.0 + g * (1.0 - s))
        return (jnp.concatenate([d_g, d_u], axis=-1),)
    return _rowwise(name, fn, [gu, d_act], [], [(2 * Fh, BF16)])[0]


def loss_head(name, y, target):
    D = y.shape[1]

    def fn(yv, tv):
        e = yv - tv
        tot = jnp.sum(_colsum(e * e), axis=1, keepdims=True) * (0.5 / D)
        return e * (1.0 / D), jnp.broadcast_to(tot, (1, 128))
    return _rowwise(name, fn, [y, target], [], [(D, F32)], [(1, 128)])


def ple_fwd(name, h, pe, z, g_e):
    def fn(hv, pv, zv, gv):
        return (hv + (pv * _rstd(pv)) * gv * _sigmoid(zv),)
    return _rowwise(name, fn, [h, pe, z], [g_e], [(h.shape[1], F32)], tm=128)[0]


def ple_bwd(name, d_h, pe, z, g_e):
    D = d_h.shape[1]

    def fn(dh, pv, zv, gv):
        gate = _sigmoid(zv)
        e = (pv * _rstd(pv)) * gv
        d_z = dh * e * gate * (1.0 - gate)
        d_pe, dgr = _rms_bwd(pv, dh * gate, gv)
        return d_z, d_pe, _colsum(dgr)
    return _rowwise(name, fn, [d_h, pe, z], [g_e], [(D, BF16), (D, BF16)], [(1, D)], tm=128)


def lat_norm_fwd(name, lat, g_q, g_kv):
    QL, KL = g_q.shape[1], g_kv.shape[1]

    def fn(v, gq, gk):
        a = v[:, :QL]
        b = v[:, QL:QL + KL]
        return (a * _rstd(a)) * gq, (b * _rstd(b)) * gk
    return _rowwise(name, fn, [lat], [g_q, g_kv], [(QL, BF16), (KL, BF16)])


def lat_norm_bwd(name, lat, d_cq, d_ckv, d_krope, g_q, g_kv):
    QL, KL = g_q.shape[1], g_kv.shape[1]

    def fn(v, dq, dk, dr, gq, gk):
        da, ga = _rms_bwd(v[:, :QL], dq, gq)
        db, gb = _rms_bwd(v[:, QL:QL + KL], dk, gk)
        return jnp.concatenate([da, db, dr], axis=-1), _colsum(ga), _colsum(gb)
    return _rowwise(name, fn, [lat, d_cq, d_ckv, d_krope], [g_q, g_kv],
                    [(lat.shape[1], BF16)], [(1, QL), (1, KL)])


def glu_fwd(name, ag):
    C = ag.shape[1] // 2

    def fn(v):
        return (v[:, :C] * _sigmoid(v[:, C:]),)
    return _rowwise(name, fn, [ag], [], [(C, F32)], tm=128)[0]


def glu_bwd(name, ag, d_u):
    C = ag.shape[1] // 2

    def fn(v, du):
        a = v[:, :C]
        s = _sigmoid(v[:, C:])
        d = jnp.concatenate([du * s, du * a * s * (1.0 - s)], axis=-1)
        return d, _colsum(d)
    return _rowwise(name, fn, [ag, d_u], [], [(2 * C, BF16)], [(1, 2 * C)], tm=128)


def ln_silu_fwd(name, yc, g, b):
    def fn(v, gv, bv):
        xc = v - jnp.mean(v, axis=-1, keepdims=True)
        ln = xc * lax.rsqrt(jnp.mean(xc * xc, axis=-1, keepdims=True) + EPS) * gv + bv
        return (ln * _sigmoid(ln),)
    return _rowwise(name, fn, [yc], [g, b], [(yc.shape[1], BF16)], tm=128)[0]


def ln_silu_bwd(name, yc, d_out, g, b):
    C = yc.shape[1]

    def fn(v, do, gv, bv):
        xc = v - jnp.mean(v, axis=-1, keepdims=True)
        r = lax.rsqrt(jnp.mean(xc * xc, axis=-1, keepdims=True) + EPS)
        xh = xc * r
        ln = xh * gv + bv
        s = _sigmoid(ln)
        d_ln = do * s * (1.0 + ln * (1.0 - s))
        dxh = d_ln * gv
        dy = r * (dxh - jnp.mean(dxh, axis=-1, keepdims=True) - xh * jnp.mean(dxh * xh, axis=-1, keepdims=True))
        return dy, _colsum(d_ln * xh), _colsum(d_ln), _colsum(dy)
    return _rowwise(name, fn, [yc, d_out], [g, b], [(C, F32)], [(1, C), (1, C), (1, C)], tm=128)


def add_halves(name, core, own, got):
    S, Rh, C = got.shape
    tr = _tile(Rh, 256, 16)
    nrb = Rh // tr

    def body(c_ref, a_ref, b_ref, o_ref):
        o_ref[...] = (a_ref[...].astype(F32) + b_ref[...].astype(F32)).astype(o_ref.dtype)

    gs = pltpu.PrefetchScalarGridSpec(
        num_scalar_prefetch=1, grid=(S, nrb),
        in_specs=[pl.BlockSpec((None, tr, C), lambda s, r, c: (s, c[0] * nrb + r, 0)),
                  pl.BlockSpec((None, tr, C), lambda s, r, c: (s, r, 0))],
        out_specs=pl.BlockSpec((None, tr, C), lambda s, r, c: (s, r, 0)))
    return pl.pallas_call(
        body, name=name, grid_spec=gs, out_shape=jax.ShapeDtypeStruct((S, Rh, C), BF16),
        compiler_params=_cparams(("arbitrary", "arbitrary")))(core, own, got)


def reduce_adamw(name, ids, sums, landed, w, m, v, layer, layers, into):
    S, Rh, C = sums.shape
    tr = _tile(Rh, 128, 16)
    nrb = Rh // tr
    c1 = 1.0 / (1.0 - ADAM_B1 ** ADAM_STEP)
    c2 = 1.0 / (1.0 - ADAM_B2 ** ADAM_STEP)

    def body(ids_ref, own_ref, a_ref, b_ref, c_ref, w_ref, m_ref, v_ref, *rest):
        go_ref, d_ref, nm_ref, nv_ref = rest[-4:]
        gv = ((own_ref[...].astype(F32) + a_ref[...].astype(F32))
              + (b_ref[...].astype(F32) + c_ref[...].astype(F32)))
        nm = ADAM_B1 * m_ref[...] + (1.0 - ADAM_B1) * gv
        nv = ADAM_B2 * v_ref[...] + (1.0 - ADAM_B2) * (gv * gv)
        go_ref[...] = gv
        d_ref[...] = -ADAM_LR * ((nm * c1) / (jnp.sqrt(nv * c2) + ADAM_EPS) + ADAM_WD * w_ref[...])
        nm_ref[...] = nm
        nv_ref[...] = nv

    slot = lambda flip: pl.BlockSpec((None, tr, C), lambda r, ids: (ids[1] ^ flip, r, 0))
    mine = pl.BlockSpec((tr, C), lambda r, ids: ((2 * layer + ids[0]) * nrb + r, 0))
    in_specs = [slot(0), slot(1), slot(2), slot(3), mine, mine, mine]
    operands = [ids, sums, landed, landed, landed, w, m, v]
    aliases = {}
    if into is not None:
        in_specs += [pl.BlockSpec(memory_space=pl.ANY)] * 4
        aliases = {len(operands) + k: k for k in range(4)}
        operands += list(into)
    gs = pltpu.PrefetchScalarGridSpec(num_scalar_prefetch=1, grid=(nrb,), in_specs=in_specs, out_specs=[mine] * 4)
    return pl.pallas_call(
        body, name=name, grid_spec=gs, out_shape=[jax.ShapeDtypeStruct((layers * 2 * Rh, C), F32)] * 4,
        input_output_aliases=aliases, compiler_params=_cparams(("arbitrary",)))(*operands)


def cast_into_slot(name, w, chip, layer, layers):
    R, C = w.shape[0] // layers, w.shape[1]
    tr = _tile(R, 256, 16)
    nrb = R // tr

    def body(s_ref, w_ref, o_ref):
        o_ref[...] = w_ref[...].astype(BF16)

    gs = pltpu.PrefetchScalarGridSpec(
        num_scalar_prefetch=1, grid=(nrb,),
        in_specs=[pl.BlockSpec((tr, C), lambda r, s: (layer * nrb + r, 0))],
        out_specs=pl.BlockSpec((None, tr, C), lambda r, s: (s[0], r, 0)))
    return pl.pallas_call(
        body, name=name, grid_spec=gs, out_shape=jax.ShapeDtypeStruct((N_CHIPS, R, C), BF16),
        compiler_params=_cparams(("arbitrary",)))(chip, w)


def adamw(name, w, g, m, v):
    R, C = w.shape
    tr = _tile(R, 256, 8)
    c1 = 1.0 / (1.0 - ADAM_B1 ** ADAM_STEP)
    c2 = 1.0 / (1.0 - ADAM_B2 ** ADAM_STEP)

    def body(w_ref, g_ref, m_ref, v_ref, d_ref, nm_ref, nv_ref):
        gv = g_ref[...]
        nm = ADAM_B1 * m_ref[...] + (1.0 - ADAM_B1) * gv
        nv = ADAM_B2 * v_ref[...] + (1.0 - ADAM_B2) * (gv * gv)
        d_ref[...] = -ADAM_LR * ((nm * c1) / (jnp.sqrt(nv * c2) + ADAM_EPS) + ADAM_WD * w_ref[...])
        nm_ref[...] = nm
        nv_ref[...] = nv

    spec = pl.BlockSpec((tr, C), lambda r: (r, 0))
    return pl.pallas_call(
        body, name=name, grid=(R // tr,), in_specs=[spec] * 4, out_specs=[spec] * 3,
        out_shape=[jax.ShapeDtypeStruct((R, C), F32)] * 3, compiler_params=_cparams(("arbitrary",)))(w, g, m, v)


def _matmul(name, a, b, *, grid, a_blk, a_map, b_blk, b_map, o_shape, o_dtype, o_blk, o_map, dims,
            scale=None, res=None, bias=None, bias_blk=None, bias_map=None, alias_into=None, after=None):
    nk = grid[2]
    has_res, has_bias, has_into = res is not None, bias is not None, alias_into is not None
    acc_shape = tuple(d for d in o_blk if d is not None)

    def body(*refs):
        a_ref, b_ref = refs[0], refs[1]
        pos = 2
        res_ref = bias_ref = None
        if has_res:
            res_ref = refs[pos]
            pos += 1
        if has_bias:
            bias_ref = refs[pos]
            pos += 1
        if has_into:
            pos += 1
        if after is not None:
            pos += 1
        o_ref = refs[pos]
        av, bv = a_ref[...], b_ref[...]
        if bv.ndim == 3:
            bv = bv.reshape(-1, bv.shape[-1])
        if av.dtype != BF16:
            av = av.astype(BF16)
        if bv.dtype != BF16:
            bv = bv.astype(BF16)
        part = lax.dot_general(av, bv, dims, preferred_element_type=F32)

        def finish(acc):
            if scale is not None:
                acc = acc * scale
            if has_bias:
                acc = acc + bias_ref[...]
            if has_res:
                acc = acc + res_ref[...]
            o_ref[...] = acc.astype(o_ref.dtype)

        if nk == 1:
            finish(part)
        else:
            acc_ref = refs[pos + 1]
            k = pl.program_id(2)

            @pl.when(k == 0)
            def _():
                acc_ref[...] = part

            @pl.when(k > 0)
            def _():
                acc_ref[...] += part

            @pl.when(k == nk - 1)
            def _():
                finish(acc_ref[...])

    operands = [a, b]
    in_specs = [pl.BlockSpec(a_blk, a_map), pl.BlockSpec(b_blk, b_map)]
    if has_res:
        operands.append(res)
        in_specs.append(pl.BlockSpec(o_blk, o_map))
    if has_bias:
        operands.append(bias)
        in_specs.append(pl.BlockSpec(bias_blk, bias_map))
    aliases = {}
    if has_into:
        aliases = {len(operands): 0}
        operands.append(alias_into)
        in_specs.append(pl.BlockSpec(memory_space=pl.ANY))
    if after is not None:
        operands.append(after)
        in_specs.append(pl.BlockSpec(memory_space=pl.ANY))
    return pl.pallas_call(
        body, name=name, grid=grid, in_specs=in_specs, out_specs=pl.BlockSpec(o_blk, o_map),
        out_shape=jax.ShapeDtypeStruct(o_shape, o_dtype),
        scratch_shapes=[pltpu.VMEM(acc_shape, F32)] if nk > 1 else [],
        input_output_aliases=aliases,
        compiler_params=_cparams(("parallel", "parallel", "arbitrary")),
    )(*operands)


def mm_nn(name, a, w3, kind, layer, *, out_dtype, scale=None, res=None, bias=None, tm=1024, tn=512, tk=2048,
          after=None, whole_k=False):
    M, K = a.shape
    S, _, C = w3.shape
    tm = _tile(M, tm, 16)
    b_blk = None
    if kind == "col":
        N = S * C
        tk, tn = _tile(K, tk), _tile(C, tn)
        kb, nb = K // tk, C // tn
        b_map = lambda n, m, k: (n // nb, layer * kb + k, n % nb)
    elif whole_k:
        N, K4, tk, tn = C, K // S, K, _tile(C, tn)
        b_blk, b_map = (S, K4, tn), lambda n, m, k: (0, layer, n)
    else:
        N, K4 = C, K // S
        tk, tn = _tile(K4, tk), _tile(C, tn)
        kb4 = K4 // tk
        b_map = lambda n, m, k: (k // kb4, layer * kb4 + k % kb4, n)
    return _matmul(
        name, a, w3, grid=(N // tn, M // tm, K // tk),
        a_blk=(tm, tk), a_map=lambda n, m, k: (m, k), b_blk=b_blk or (None, tk, tn), b_map=b_map,
        o_shape=(M, N), o_dtype=out_dtype, o_blk=(tm, tn), o_map=lambda n, m, k: (m, n), dims=NN_DIMS,
        scale=scale, res=res, bias=bias, bias_blk=(1, tn), bias_map=lambda n, m, k: (0, n), after=after)


def _mm_nt(name, g, w3, kind, layer, K, *, out_dtype, scale=None, tm=1024, to=1024, tc=1408, after=None):
    M, N = g.shape
    S, _, C = w3.shape
    tm = _tile(M, tm, 16)
    tc = _tile(C, tc)
    if kind == "col":
        nb = C // tc
        to = _tile(K, to)
        ob = K // to
        b_map = lambda o, m, c: (c // nb, layer * ob + o, c % nb)
    else:
        K4 = K // S
        to = _tile(K4, to)
        ob4 = K4 // to
        b_map = lambda o, m, c: (o // ob4, layer * ob4 + o % ob4, c)
    return _matmul(
        name, g, w3, grid=(K // to, M // tm, N // tc),
        a_blk=(tm, tc), a_map=lambda o, m, c: (m, c), b_blk=(None, to, tc), b_map=b_map,
        o_shape=(M, K), o_dtype=out_dtype, o_blk=(tm, to), o_map=lambda o, m, c: (m, o), dims=NT_DIMS,
        scale=scale, after=after)


def mm_tn(name, a, g, kind, *, shards=N_CHIPS, layer=0, layers=1, into=None, out_dtype=BF16, scale=None,
          tk=1024, tn=1408, tm=2048, after=None):
    M, K = a.shape
    N = g.shape[1]
    tm = _tile(M, tm, 16)
    if kind == "col":
        C = N // shards
        tk, tn = _tile(K, tk), _tile(C, tn)
        kb, nb = K // tk, C // tn
        o_shape = (shards, layers * K, C)
        o_map = lambda k, n, m: (n // nb, layer * kb + k, n % nb)
    else:
        K4 = K // shards
        tk, tn = _tile(K4, tk), _tile(N, tn)
        kb4 = K4 // tk
        o_shape = (shards, layers * K4, N)
        o_map = lambda k, n, m: (k // kb4, layer * kb4 + k % kb4, n)
    return _matmul(
        name, a, g, grid=(K // tk, N // tn, M // tm),
        a_blk=(tm, tk), a_map=lambda k, n, m: (m, k), b_blk=(tm, tn), b_map=lambda k, n, m: (m, n),
        o_shape=o_shape, o_dtype=out_dtype, o_blk=(None, tk, tn), o_map=o_map, dims=TN_DIMS,
        scale=scale, alias_into=into, after=after)


def rope_tables(name, pos):
    T = pos.shape[0]
    half = D_ROPE // 2

    def body(p_ref, c_ref, s1_ref, s2_ref):
        lane = lax.broadcasted_iota(jnp.int32, (T, 128), 1)
        idx = (lane & (half - 1)).astype(F32)
        ang = p_ref[...] * jnp.exp(idx * (-2.0 * math.log(ROPE_THETA) / D_ROPE))
        cs, sn = jnp.cos(ang), jnp.sin(ang)
        c_ref[...] = jnp.where(lane < D_ROPE, cs, 0.0)
        s1_ref[...] = jnp.where(lane < half, -sn, 0.0)
        s2_ref[...] = jnp.where((lane >= half) & (lane < D_ROPE), sn, 0.0)

    return pl.pallas_call(body, name=name, out_shape=[jax.ShapeDtypeStruct((T, 128), F32)] * 3,
                          compiler_params=_cparams())(pos)


def _rope(v, cs, s1, s2):
    return v * cs + pltpu.roll(v, 128 - D_ROPE // 2, 1) * s1 + pltpu.roll(v, D_ROPE // 2, 1) * s2


def _rope_bwd(d, cs, s1, s2):
    return d * cs + pltpu.roll(d * s1, D_ROPE // 2, 1) + pltpu.roll(d * s2, 128 - D_ROPE // 2, 1)


def _head_rstd(n, r):
    ms = (jnp.sum(n * n, axis=-1, keepdims=True) + jnp.sum(r * r, axis=-1, keepdims=True)) * (1.0 / QK_DIM)
    return lax.rsqrt(ms + EPS)


def mla_prep_fwd(name, q_raw, kv_raw, lat, tabs, q_gain, k_gain, rope_col, tm=128):
    T = q_raw.shape[0]
    H = q_raw.shape[1] // HEAD_PAD
    tm = min(tm, T)
    rope_blk = rope_col // 128

    def body(q_ref, kv_ref, kr_ref, c_ref, s1_ref, s2_ref, qg_ref, kg_ref, Q_ref, K_ref, V_ref):
        cs, s1, s2 = c_ref[...], s1_ref[...], s2_ref[...]
        qg, kg = qg_ref[...], kg_ref[...]
        kr = kr_ref[...]
        for h in range(H):
            lo = HEAD_PAD * h
            n, r = q_ref[:, lo:lo + 128], q_ref[:, lo + 128:lo + 256]
            rs = _head_rstd(n, r)
            Q_ref[h, :, 0:128] = (n * rs * qg[:, :128]).astype(BF16)
            Q_ref[h, :, 128:256] = _rope(r * rs * qg[:, 128:], cs, s1, s2).astype(BF16)
            n = kv_ref[:, lo:lo + 128]
            rs = _head_rstd(n, kr)
            K_ref[h, :, 0:128] = (n * rs * kg[:, :128]).astype(BF16)
            K_ref[h, :, 128:256] = _rope(kr * rs * kg[:, 128:], cs, s1, s2).astype(BF16)
            V_ref[h] = kv_ref[:, lo + 128:lo + 256].astype(BF16)

    row = lambda w: pl.BlockSpec((tm, w), lambda i: (i, 0))
    vec = pl.BlockSpec((1, HEAD_PAD), lambda i: (0, 0))
    return pl.pallas_call(
        body, name=name, grid=(T // tm,),
        in_specs=[row(H * HEAD_PAD), row(H * HEAD_PAD), pl.BlockSpec((tm, 128), lambda i: (i, rope_blk)),
                  row(128), row(128), row(128), vec, vec],
        out_specs=[pl.BlockSpec((H, tm, HEAD_PAD), lambda i: (0, i, 0))] * 2 + [pl.BlockSpec((H, tm, D_V), lambda i: (0, i, 0))],
        out_shape=[jax.ShapeDtypeStruct((H, T, HEAD_PAD), BF16)] * 2 + [jax.ShapeDtypeStruct((H, T, D_V), BF16)],
        compiler_params=_cparams(("arbitrary",)),
    )(q_raw, kv_raw, lat, *tabs, q_gain, k_gain)


def mla_prep_bwd(name, dQ, dK, dV, q_raw, kv_raw, lat, tabs, q_gain, k_gain, rope_col, tm=128):
    T = q_raw.shape[0]
    H = q_raw.shape[1] // HEAD_PAD
    tm = min(tm, T)
    rope_blk = rope_col // 128

    def body(dQ_ref, dK_ref, dV_ref, q_ref, kv_ref, kr_ref, c_ref, s1_ref, s2_ref, qg_ref, kg_ref,
             dq_ref, dkv_ref, dkr_ref, dqg_ref, dkg_ref):
        @pl.when(pl.program_id(0) == 0)
        def _():
            dqg_ref[...] = jnp.zeros_like(dqg_ref)
            dkg_ref[...] = jnp.zeros_like(dkg_ref)

        cs, s1, s2 = c_ref[...], s1_ref[...], s2_ref[...]
        qg, kg = qg_ref[...], kg_ref[...]
        kr = kr_ref[...]
        dkr = jnp.zeros((tm, 128), F32)
        gq_n = jnp.zeros((1, 128), F32)
        gq_r = jnp.zeros((1, 128), F32)
        gk_n = jnp.zeros((1, 128), F32)
        gk_r = jnp.zeros((1, 128), F32)

        def norm_bwd(n, r, dn, dr, gain):
            rs = _head_rstd(n, r)
            nh, rh = n * rs, r * rs
            dng, drg = dn * gain[:, :128], dr * gain[:, 128:]
            mean = (jnp.sum(dng * nh, axis=-1, keepdims=True) + jnp.sum(drg * rh, axis=-1, keepdims=True)) * (1.0 / QK_DIM)
            return rs * (dng - nh * mean), rs * (drg - rh * mean), _colsum(dn * nh), _colsum(dr * rh)

        for h in range(H):
            lo = HEAD_PAD * h
            n, r = q_ref[:, lo:lo + 128], q_ref[:, lo + 128:lo + 256]
            dn = dQ_ref[h, :, 0:128].astype(F32)
            dr = _rope_bwd(dQ_ref[h, :, 128:256].astype(F32), cs, s1, s2)
            a, b, g1, g2 = norm_bwd(n, r, dn, dr, qg)
            dq_ref[:, lo:lo + 128] = a.astype(BF16)
            dq_ref[:, lo + 128:lo + 256] = b.astype(BF16)
            gq_n, gq_r = gq_n + g1, gq_r + g2
            n = kv_ref[:, lo:lo + 128]
            dn = dK_ref[h, :, 0:128].astype(F32)
            dr = _rope_bwd(dK_ref[h, :, 128:256].astype(F32), cs, s1, s2)
            a, b, g1, g2 = norm_bwd(n, kr, dn, dr, kg)
            dkv_ref[:, lo:lo + 128] = a.astype(BF16)
            dkv_ref[:, lo + 128:lo + 256] = dV_ref[h].astype(BF16)
            dkr = dkr + b
            gk_n, gk_r = gk_n + g1, gk_r + g2
        dkr_ref[...] = dkr
        dqg_ref[:, 0:128] += gq_n
        dqg_ref[:, 128:256] += gq_r
        dkg_ref[:, 0:128] += gk_n
        dkg_ref[:, 128:256] += gk_r

    row = lambda w: pl.BlockSpec((tm, w), lambda i: (i, 0))
    vec = pl.BlockSpec((1, HEAD_PAD), lambda i: (0, 0))
    hd = lambda w: pl.BlockSpec((H, tm, w), lambda i: (0, i, 0))
    return pl.pallas_call(
        body, name=name, grid=(T // tm,),
        in_specs=[hd(HEAD_PAD), hd(HEAD_PAD), hd(D_V), row(H * HEAD_PAD), row(H * HEAD_PAD),
                  pl.BlockSpec((tm, 128), lambda i: (i, rope_blk)), row(128), row(128), row(128), vec, vec],
        out_specs=[row(H * HEAD_PAD), row(H * HEAD_PAD), row(128), vec, vec],
        out_shape=[jax.ShapeDtypeStruct((T, H * HEAD_PAD), BF16)] * 2 + [jax.ShapeDtypeStruct((T, 128), F32)]
        + [jax.ShapeDtypeStruct((1, HEAD_PAD), F32)] * 2,
        compiler_params=_cparams(("arbitrary",)),
    )(dQ, dK, dV, q_raw, kv_raw, lat, *tabs, q_gain, k_gain)


def _causal_probs(q, k, scale, row0):
    s = lax.dot_general(q, k, NT_DIMS, preferred_element_type=F32) * scale
    row = row0 + lax.broadcasted_iota(jnp.int32, s.shape, 0)
    col = lax.broadcasted_iota(jnp.int32, s.shape, 1)
    s = jnp.where(col <= row, s, NEG_BIG)
    p = jnp.exp(s - jnp.max(s, axis=-1, keepdims=True))
    return p, jnp.sum(p, axis=-1, keepdims=True)


def attn_fwd(name, Q, K, V, tq=512):
    H, T, E = Q.shape
    tq = min(tq, T)
    nq = T // tq
    scale = QK_DIM ** -0.5

    def body(q_ref, k_ref, v_ref, o_ref):
        i = pl.program_id(1)
        for ib in range(nq):
            @pl.when(i == ib)
            def _():
                n = (ib + 1) * tq
                p, l = _causal_probs(q_ref[...], k_ref[0:n, :], scale, ib * tq)
                o = jnp.dot(p.astype(BF16), v_ref[0:n, :], preferred_element_type=F32)
                o_ref[...] = (o / l).astype(o_ref.dtype)

    return pl.pallas_call(
        body, name=name, grid=(H, nq),
        in_specs=[pl.BlockSpec((None, tq, E), lambda h, i: (h, i, 0)),
                  pl.BlockSpec((None, T, E), lambda h, i: (h, 0, 0)),
                  pl.BlockSpec((None, T, D_V), lambda h, i: (h, 0, 0))],
        out_specs=pl.BlockSpec((tq, D_V), lambda h, i: (i, h)),
        out_shape=jax.ShapeDtypeStruct((T, H * D_V), BF16),
        compiler_params=_cparams(("parallel", "arbitrary")),
    )(Q, K, V)


def attn_bwd(name, Q, K, V, dO, tq=512):
    H, T, E = Q.shape
    tq = min(tq, T)
    nq = T // tq
    scale = QK_DIM ** -0.5

    def body(q_ref, k_ref, v_ref, do_ref, dq_ref, dk_ref, dv_ref):
        i = pl.program_id(1)

        @pl.when(i == 0)
        def _():
            dk_ref[...] = jnp.zeros_like(dk_ref)
            dv_ref[...] = jnp.zeros_like(dv_ref)

        for ib in range(nq):
            @pl.when(i == ib)
            def _():
                n = (ib + 1) * tq
                q, k, v, do = q_ref[...], k_ref[0:n, :], v_ref[0:n, :], do_ref[...]
                p, l = _causal_probs(q, k, scale, ib * tq)
                p = p / l
                dp = lax.dot_general(do, v, NT_DIMS, preferred_element_type=F32)
                ds = p * (dp - jnp.sum(p * dp, axis=-1, keepdims=True)) * scale
                dsb, pb = ds.astype(BF16), p.astype(BF16)
                dq_ref[...] = jnp.dot(dsb, k, preferred_element_type=F32)
                dk_ref[0:n, :] += lax.dot_general(dsb, q, TN_DIMS, preferred_element_type=F32)
                dv_ref[0:n, :] += lax.dot_general(pb, do, TN_DIMS, preferred_element_type=F32)

    return pl.pallas_call(
        body, name=name, grid=(H, nq),
        in_specs=[pl.BlockSpec((None, tq, E), lambda h, i: (h, i, 0)),
                  pl.BlockSpec((None, T, E), lambda h, i: (h, 0, 0)),
                  pl.BlockSpec((None, T, D_V), lambda h, i: (h, 0, 0)),
                  pl.BlockSpec((tq, D_V), lambda h, i: (i, h))],
        out_specs=[pl.BlockSpec((None, tq, E), lambda h, i: (h, i, 0)),
                   pl.BlockSpec((None, T, E), lambda h, i: (h, 0, 0)),
                   pl.BlockSpec((None, T, D_V), lambda h, i: (h, 0, 0))],
        out_shape=[jax.ShapeDtypeStruct((H, T, E), F32)] * 2 + [jax.ShapeDtypeStruct((H, T, D_V), F32)],
        compiler_params=_cparams(("parallel", "arbitrary")),
    )(Q, K, V, dO)


def _dw_specs(T, C, tm, tc, halo):
    cur = pl.BlockSpec((tm, tc), lambda j, i: (i, j))
    last = T // tm - 1
    if halo == "prev":
        nbr = pl.BlockSpec((tm, tc), lambda j, i: (jnp.maximum(i - 1, 0), j))
    else:
        nbr = pl.BlockSpec((tm, tc), lambda j, i: (jnp.minimum(i + 1, last), j))
    return cur, nbr


def dwconv_fwd(name, u, w, b, tm=256, tc=512, rs=32):
    T, C = u.shape
    tm, tc = min(tm, T), min(tc, C)
    cur, prev = _dw_specs(T, C, tm, tc, "prev")

    def body(up_ref, uc_ref, w_ref, b_ref, o_ref, scr):
        i = pl.program_id(1)

        @pl.when(i == 0)
        def _():
            scr[pl.ds(0, tm), :] = jnp.zeros((tm, tc), F32)

        @pl.when(i > 0)
        def _():
            scr[pl.ds(0, tm), :] = up_ref[...]

        scr[pl.ds(tm, tm), :] = uc_ref[...]
        for s in range(tm // rs):
            acc = jnp.broadcast_to(b_ref[...], (rs, tc))
            for k in range(CONV_WIDTH):
                acc = acc + w_ref[pl.ds(k, 1), :] * scr[pl.ds(tm - (CONV_WIDTH - 1) + k + rs * s, rs), :]
            o_ref[pl.ds(rs * s, rs), :] = acc

    return pl.pallas_call(
        body, name=name, grid=(C // tc, T // tm),
        in_specs=[prev, cur, pl.BlockSpec((CONV_TAPS_PAD, tc), lambda j, i: (0, j)), pl.BlockSpec((1, tc), lambda j, i: (0, j))],
        out_specs=cur, out_shape=jax.ShapeDtypeStruct((T, C), F32),
        scratch_shapes=[pltpu.VMEM((2 * tm, tc), F32)], compiler_params=_cparams(("parallel", "arbitrary")),
    )(u, u, w, b)


def dwconv_bwd_u(name, dy, w, tm=256, tc=512, rs=32):
    T, C = dy.shape
    tm, tc = min(tm, T), min(tc, C)
    cur, nxt = _dw_specs(T, C, tm, tc, "next")
    last = T // tm - 1

    def body(dc_ref, dn_ref, w_ref, o_ref, scr):
        i = pl.program_id(1)
        scr[pl.ds(0, tm), :] = dc_ref[...]

        @pl.when(i == last)
        def _():
            scr[pl.ds(tm, tm), :] = jnp.zeros((tm, tc), F32)

        @pl.when(i < last)
        def _():
            scr[pl.ds(tm, tm), :] = dn_ref[...]

        for s in range(tm // rs):
            acc = jnp.zeros((rs, tc), F32)
            for k in range(CONV_WIDTH):
                acc = acc + w_ref[pl.ds(k, 1), :] * scr[pl.ds((CONV_WIDTH - 1) - k + rs * s, rs), :]
            o_ref[pl.ds(rs * s, rs), :] = acc

    return pl.pallas_call(
        body, name=name, grid=(C // tc, T // tm),
        in_specs=[cur, nxt, pl.BlockSpec((CONV_TAPS_PAD, tc), lambda j, i: (0, j))],
        out_specs=cur, out_shape=jax.ShapeDtypeStruct((T, C), F32),
        scratch_shapes=[pltpu.VMEM((2 * tm, tc), F32)], compiler_params=_cparams(("parallel", "arbitrary")),
    )(dy, dy, w)


def dwconv_bwd_w(name, u, dy, tm=256, tc=512, rs=32):
    T, C = u.shape
    tm, tc = min(tm, T), min(tc, C)
    cur, prev = _dw_specs(T, C, tm, tc, "prev")

    def body(up_ref, uc_ref, dy_ref, o_ref, scr):
        i = pl.program_id(1)

        @pl.when(i == 0)
        def _():
            scr[pl.ds(0, tm), :] = jnp.zeros((tm, tc), F32)
            o_ref[...] = jnp.zeros_like(o_ref)

        @pl.when(i > 0)
        def _():
            scr[pl.ds(0, tm), :] = up_ref[...]

        scr[pl.ds(tm, tm), :] = uc_ref[...]
        for k in range(CONV_WIDTH):
            acc = jnp.zeros((rs, tc), F32)
            for s in range(tm // rs):
                acc = acc + dy_ref[pl.ds(rs * s, rs), :] * scr[pl.ds(tm - (CONV_WIDTH - 1) + k + rs * s, rs), :]
            o_ref[pl.ds(k, 1), :] += _colsum(acc)

    return pl.pallas_call(
        body, name=name, grid=(C // tc, T // tm),
        in_specs=[prev, cur, cur],
        out_specs=pl.BlockSpec((CONV_TAPS_PAD, tc), lambda j, i: (0, j)),
        out_shape=jax.ShapeDtypeStruct((CONV_TAPS_PAD, C), F32),
        scratch_shapes=[pltpu.VMEM((2 * tm, tc), F32)], compiler_params=_cparams(("parallel", "arbitrary")),
    )(u, u, dy)


def _place():
    x, y, c = lax.axis_index("x"), lax.axis_index("y"), lax.axis_index("c")
    return x, y, c


def _other_chips(x, y):
    return [(1 - x, y, 2 * (1 - x) + y), (x, 1 - y, 2 * x + (1 - y)), (1 - x, 1 - y, 2 * (1 - x) + (1 - y))]


def _hbm_specs(n):
    return [pl.BlockSpec(memory_space=pl.ANY)] * n


HBM_SPEC = pl.BlockSpec(memory_space=pltpu.HBM)
SEM_SPEC = pl.BlockSpec(memory_space=pltpu.SEMAPHORE)
ANY_SPEC = pl.BlockSpec(memory_space=pl.ANY)
SIDE_EFFECT = pltpu.SideEffectType.DATAFLOW_SIDE_EFFECTING


def _half(ref, slot, which):
    rh = ref.shape[1] // 2
    return ref.at[slot, pl.ds(pl.multiple_of(which * rh, 16), rh), :]


def _hbm(a):
    return pltpu.with_memory_space_constraint(a, pltpu.HBM)


def gather_start(name, groups, after):
    flat = [b for g in groups for b in g]
    n, ng = len(flat), len(groups)

    def body(*refs):
        send, recv, out = refs[n + 1:n + 1 + ng], refs[n + 1 + ng:n + 1 + 2 * ng], refs[n + 1 + 2 * ng:]
        x, y, c = _place()
        me = 2 * x + y
        a = 0
        for g, grp in enumerate(groups):
            for k in range(len(grp)):
                piece = _half(out[a], me, c)
                for j, (px, py, _) in enumerate(_other_chips(x, y)):
                    pltpu.make_async_remote_copy(
                        src_ref=piece, dst_ref=piece, send_sem=send[g].at[3 * k + j], recv_sem=recv[g].at[3 * k + j],
                        device_id=(px, py, c), device_id_type=MESH).start()
                a += 1

    sems = [pltpu.SemaphoreType.DMA((3 * len(g),)) for g in groups]
    res = pl.pallas_call(
        body, name=name, in_specs=[HBM_SPEC] * n + [ANY_SPEC], out_specs=[SEM_SPEC] * (2 * ng) + [HBM_SPEC] * n,
        out_shape=sems + sems + [pltpu.HBM(b.shape, b.dtype) for b in flat],
        input_output_aliases={a: 2 * ng + a for a in range(n)},
        compiler_params=pltpu.CompilerParams(has_side_effects=SIDE_EFFECT),
    )(*[_hbm(b) for b in flat], after)
    send, recv, bufs = res[:ng], res[ng:2 * ng], list(res[2 * ng:])
    out, a = [], 0
    for g, grp in enumerate(groups):
        out.append((send[g], recv[g], bufs[a:a + len(grp)]))
        a += len(grp)
    return out


def gather_relay(name, started, after):
    send1, recv1, bufs = started
    n = len(bufs)

    def body(*refs):
        s1, r1 = refs[n], refs[n + 1]
        s2, r2, out, token = refs[n + 3], refs[n + 4], refs[n + 5:2 * n + 5], refs[2 * n + 5]
        x, y, c = _place()
        me = 2 * x + y
        chips = _other_chips(x, y)
        for k in range(n):
            for j, (px, py, idx) in enumerate(chips):
                cp = pltpu.make_async_remote_copy(
                    src_ref=_half(out[k], me, c), dst_ref=_half(out[k], idx, c), send_sem=s1.at[3 * k + j],
                    recv_sem=r1.at[3 * k + j], device_id=(px, py, c), device_id_type=MESH)
                cp.wait_send()
                cp.wait_recv()
        for k in range(n):
            for j, (px, py, idx) in enumerate(chips):
                piece = _half(out[k], idx, c)
                pltpu.make_async_remote_copy(
                    src_ref=piece, dst_ref=piece, send_sem=s2.at[3 * k + j], recv_sem=r2.at[3 * k + j],
                    device_id=(x, y, 1 - c), device_id_type=MESH).start()
        token[...] = jnp.zeros_like(token)

    sem = pltpu.SemaphoreType.DMA((3 * n,))
    res = pl.pallas_call(
        body, name=name, in_specs=[HBM_SPEC] * n + [SEM_SPEC, SEM_SPEC, ANY_SPEC],
        out_specs=[SEM_SPEC, SEM_SPEC] + [HBM_SPEC] * n + [pl.BlockSpec(memory_space=pltpu.VMEM)],
        out_shape=[sem, sem] + [pltpu.HBM(b.shape, b.dtype) for b in bufs] + [jax.ShapeDtypeStruct((8, 128), F32)],
        input_output_aliases={a: 2 + a for a in range(n)},
        compiler_params=pltpu.CompilerParams(has_side_effects=SIDE_EFFECT),
    )(*bufs, send1, recv1, after)
    return res[0], res[1], list(res[2:2 + n]), res[2 + n]


def gather_wait(name, relayed, after):
    send2, recv2, bufs, _ = relayed
    n = len(bufs)

    def body(*refs):
        s2, r2, out = refs[n], refs[n + 1], refs[n + 3:]
        x, y, c = _place()
        for k in range(n):
            for j, (px, py, idx) in enumerate(_other_chips(x, y)):
                cp = pltpu.make_async_remote_copy(
                    src_ref=_half(out[k], idx, c), dst_ref=_half(out[k], idx, 1 - c), send_sem=s2.at[3 * k + j],
                    recv_sem=r2.at[3 * k + j], device_id=(x, y, 1 - c), device_id_type=MESH)
                cp.wait_send()
                cp.wait_recv()

    res = pl.pallas_call(
        body, name=name, in_specs=[HBM_SPEC] * n + [SEM_SPEC, SEM_SPEC, ANY_SPEC], out_specs=[HBM_SPEC] * n,
        out_shape=[pltpu.HBM(b.shape, b.dtype) for b in bufs], input_output_aliases={a: a for a in range(n)},
        compiler_params=pltpu.CompilerParams(has_side_effects=SIDE_EFFECT),
    )(*bufs, send2, recv2, after)
    return list(res)


def split_start(name, bufs, land_shapes, n_copies, plan, after):
    nb, nl = len(bufs), len(land_shapes)

    def body(*refs):
        send, recv = refs[nb + 1], refs[nb + 2]
        out, lands, token = refs[nb + 3:2 * nb + 3], refs[2 * nb + 3:2 * nb + 3 + nl], refs[2 * nb + 3 + nl]
        x, y, c = _place()
        for k, (src, dst, to, _) in enumerate(plan(out, lands, x, y, c)):
            pltpu.make_async_remote_copy(src_ref=src, dst_ref=dst, send_sem=send.at[k], recv_sem=recv.at[k],
                                         device_id=to, device_id_type=MESH).start()
        token[...] = jnp.zeros_like(token)

    sem = pltpu.SemaphoreType.DMA((n_copies,))
    res = pl.pallas_call(
        body, name=name, in_specs=[HBM_SPEC] * nb + [ANY_SPEC],
        out_specs=[SEM_SPEC, SEM_SPEC] + [HBM_SPEC] * (nb + nl) + [pl.BlockSpec(memory_space=pltpu.VMEM)],
        out_shape=[sem, sem] + [pltpu.HBM(b.shape, b.dtype) for b in bufs]
        + [pltpu.HBM(s, d) for s, d in land_shapes] + [jax.ShapeDtypeStruct((8, 128), F32)],
        input_output_aliases={a: 2 + a for a in range(nb)},
        compiler_params=pltpu.CompilerParams(has_side_effects=SIDE_EFFECT),
    )(*[_hbm(b) for b in bufs], after)
    return res[0], res[1], list(res[2:2 + nb]), list(res[2 + nb:2 + nb + nl]), res[2 + nb + nl]


def split_wait(name, started, plan, after):
    send, recv, bufs, lands, _ = started
    nb, nl = len(bufs), len(lands)

    def body(*refs):
        s, r = refs[nb + nl], refs[nb + nl + 1]
        out, lo = refs[nb + nl + 3:2 * nb + nl + 3], refs[2 * nb + nl + 3:]
        x, y, c = _place()
        for k, (src, _, to, landed) in enumerate(plan(out, lo, x, y, c)):
            cp = pltpu.make_async_remote_copy(src_ref=src, dst_ref=landed, send_sem=s.at[k], recv_sem=r.at[k],
                                              device_id=to, device_id_type=MESH)
            cp.wait_send()
            cp.wait_recv()

    res = pl.pallas_call(
        body, name=name, in_specs=[HBM_SPEC] * (nb + nl) + [SEM_SPEC, SEM_SPEC, ANY_SPEC],
        out_specs=[HBM_SPEC] * (nb + nl), out_shape=[pltpu.HBM(b.shape, b.dtype) for b in bufs + lands],
        input_output_aliases={a: a for a in range(nb + nl)},
        compiler_params=pltpu.CompilerParams(has_side_effects=SIDE_EFFECT),
    )(*bufs, *lands, send, recv, after)
    return list(res[:nb]), list(res[nb:])


def swap_halves_plan(parts, lands, x, y, c):
    out = []
    for a in range(len(parts)):
        rh = parts[a].shape[1] // 2
        theirs = parts[a].at[:, pl.ds(pl.multiple_of((1 - c) * rh, 16), rh), :]
        out.append((theirs, lands[a], (x, y, 1 - c), lands[a]))
    return out


def chip_exchange_plan(sums, lands, x, y, c):
    me = 2 * x + y
    out = []
    for a in range(len(sums)):
        for px, py, idx in _other_chips(x, y):
            out.append((sums[a].at[idx], lands[a].at[me], (px, py, c), lands[a].at[idx]))
    return out


def share_halves_plan(rows):
    def plan(bufs, lands, x, y, c):
        out = []
        for buf, (row0, rh) in zip(bufs, rows):
            mine = buf.at[pl.ds(pl.multiple_of(row0 + c * rh, 8), rh), :]
            theirs = buf.at[pl.ds(pl.multiple_of(row0 + (1 - c) * rh, 8), rh), :]
            out.append((mine, mine, (x, y, 1 - c), theirs))
        return out
    return plan


def allreduce_pack(name, pack):
    R, W = pack.shape

    def body(p_ref, o_ref, sib, pair, got, send_sems, recv_sems):
        x, y, c = _place()

        def swap(k, src, dst, to):
            cp = pltpu.make_async_remote_copy(src_ref=src, dst_ref=dst, send_sem=send_sems.at[k],
                                              recv_sem=recv_sems.at[k], device_id=to, device_id_type=MESH)
            cp.start()
            return cp

        cp = swap(0, p_ref, sib, (x, y, 1 - c))
        cp.wait()
        pair[...] = p_ref[...] + sib[...]
        cps = [swap(1, pair, got.at[0], (1 - x, y, c)), swap(2, pair, got.at[1], (x, 1 - y, c)),
               swap(3, pair, got.at[2], (1 - x, 1 - y, c))]
        for cp in cps:
            cp.wait()
        o_ref[...] = (pair[...] + got[1]) + (got[0] + got[2])

    return pl.pallas_call(
        body, name=name, out_shape=jax.ShapeDtypeStruct((R, W), F32),
        in_specs=[pl.BlockSpec(memory_space=pltpu.VMEM)], out_specs=pl.BlockSpec(memory_space=pltpu.VMEM),
        scratch_shapes=[pltpu.VMEM((R, W), F32), pltpu.VMEM((R, W), F32), pltpu.VMEM((3, R, W), F32),
                        pltpu.SemaphoreType.DMA((4,)), pltpu.SemaphoreType.DMA((4,))],
        compiler_params=_cparams(),
    )(pack)


BIG_WEIGHTS = [
    ("ffn_a_w_in", "col"), ("ffn_a_w_out", "row"), ("ffn_b_w_in", "col"), ("ffn_b_w_out", "row"),
    ("mla_w_in", "row"), ("mla_w_uq", "col"), ("mla_w_ukv", "col"), ("mla_w_o", "row"),
    ("conv_w_pw1", "col"), ("conv_w_pw2", "row"), ("ple_w_proj", "col"), ("ple_w_gate", "row"),
]
WEIGHT_ORDER = ["ffn_a_norm", "ffn_a_w_in", "ffn_a_w_out", "ffn_b_norm", "ffn_b_w_in", "ffn_b_w_out", "mix_norm",
                "mla_w_in", "mla_q_lat_norm", "mla_kv_lat_norm", "mla_w_uq", "mla_w_ukv", "mla_q_gain", "mla_k_gain",
                "mla_w_o", "conv_w_pw1", "conv_b_pw1", "conv_w_dw", "conv_b_dw", "conv_ln_g", "conv_ln_b", "conv_w_pw2",
                "ple_w_proj", "ple_norm", "ple_gate_norm", "ple_w_gate"]
REPLICATED_SMALL = ["ffn_a_norm", "ffn_b_norm", "mix_norm", "ple_norm", "ple_gate_norm",
                    "mla_q_lat_norm", "mla_kv_lat_norm", "mla_q_gain", "mla_k_gain"]
SHARDED_SMALL = ["conv_b_pw1", "conv_w_dw", "conv_b_dw", "conv_ln_g", "conv_ln_b"]
PACK_ROWS = 8


def _pack_rows(arrs, width):
    out = []
    for a in arrs:
        r = -(-a.shape[0] // PACK_ROWS) * PACK_ROWS
        out.append(jnp.pad(a, ((0, r - a.shape[0]), (0, width - a.shape[1]))))
    return jnp.concatenate(out, axis=0)


def _unpack_rows(pack, shapes):
    out, r0 = [], 0
    for (r, w) in shapes:
        out.append(pack[r0:r0 + r, :w])
        r0 += -(-r // PACK_ROWS) * PACK_ROWS
    return out


def kernel(x, p, positions, ffn_a_norm, ffn_a_w_in, ffn_a_w_out, ffn_b_norm, ffn_b_w_in, ffn_b_w_out, mix_norm, mla_w_in, mla_q_lat_norm, mla_kv_lat_norm, mla_w_uq, mla_w_ukv, mla_q_gain, mla_k_gain, mla_w_o, conv_w_pw1, conv_b_pw1, conv_w_dw, conv_b_dw, conv_ln_g, conv_ln_b, conv_w_pw2, ple_w_proj, ple_norm, ple_gate_norm, ple_w_gate, loss_target, m_ffn_a_norm, m_ffn_a_w_in, m_ffn_a_w_out, m_ffn_b_norm, m_ffn_b_w_in, m_ffn_b_w_out, m_mix_norm, m_mla_w_in, m_mla_q_lat_norm, m_mla_kv_lat_norm, m_mla_w_uq, m_mla_w_ukv, m_mla_q_gain, m_mla_k_gain, m_mla_w_o, m_conv_w_pw1, m_conv_b_pw1, m_conv_w_dw, m_conv_b_dw, m_conv_ln_g, m_conv_ln_b, m_conv_w_pw2, m_ple_w_proj, m_ple_norm, m_ple_gate_norm, m_ple_w_gate, v_ffn_a_norm, v_ffn_a_w_in, v_ffn_a_w_out, v_ffn_b_norm, v_ffn_b_w_in, v_ffn_b_w_out, v_mix_norm, v_mla_w_in, v_mla_q_lat_norm, v_mla_kv_lat_norm, v_mla_w_uq, v_mla_w_ukv, v_mla_q_gain, v_mla_k_gain, v_mla_w_o, v_conv_w_pw1, v_conv_b_pw1, v_conv_w_dw, v_conv_b_dw, v_conv_ln_g, v_conv_ln_b, v_conv_w_pw2, v_ple_w_proj, v_ple_norm, v_ple_gate_norm, v_ple_w_gate):
    args = dict(locals())
    W = {n: args[n] for n in WEIGHT_ORDER}
    M1 = {n: args["m_" + n] for n in WEIGHT_ORDER}
    V2 = {n: args["v_" + n] for n in WEIGHT_ORDER}

    T, D = x.shape[1], x.shape[2]
    depth = ffn_a_norm.shape[0]
    H = mla_w_ukv.shape[2] * N_CHIPS // (D_NOPE + D_V)
    QL, KL = mla_q_lat_norm.shape[1], mla_kv_lat_norm.shape[1]
    C = conv_w_pw2.shape[1] * N_CHIPS
    lat_w = QL + KL + D_ROPE
    lat_pad = QL + KL + 128

    cx, cy, cc = lax.axis_index("x"), lax.axis_index("y"), lax.axis_index("c")
    chip = (2 * cx + cy).astype(jnp.int32)
    chip_arr = chip.reshape(1)
    core_arr = cc.astype(jnp.int32).reshape(1)

    def stage_groups(i):
        mix = ([("mla_w_in", i // 2), ("mla_w_uq", i // 2), ("mla_w_ukv", i // 2), ("mla_w_o", i // 2)] if i % 2 == 0
               else [("conv_w_pw1", i // 2), ("conv_w_pw2", i // 2)])
        return [[("ffn_a_w_in", i)], [("ffn_a_w_out", i)], mix, [("ffn_b_w_in", i)], [("ffn_b_w_out", i)],
                [("ple_w_proj", i), ("ple_w_gate", i)]]

    groups = [g for i in range(depth) for g in stage_groups(i)]
    A_IN, A_OUT, MIX, B_IN, B_OUT, PLE, PER_LAYER = 0, 1, 2, 3, 4, 5, 6

    def slot_of(key):
        n, l = key
        return cast_into_slot(f"cast_{n}_{l}", W[n].reshape(-1, W[n].shape[-1]), chip_arr, l, W[n].shape[0])

    def placed(a, width):
        full = jnp.zeros(a.shape[:-1] + (width,), F32)
        full = lax.dynamic_update_slice_in_dim(full, a, chip * a.shape[-1], axis=a.ndim - 1)
        return full * (cc == 0).astype(F32)

    b_pw1_sh = conv_b_pw1.reshape(1, -1)
    small_in = [placed(b_pw1_sh, 2 * C).reshape(2, C), placed(conv_w_dw[0], C), placed(conv_b_dw, C),
                placed(conv_ln_g, C), placed(conv_ln_b, C)]
    small_pack = allreduce_pack("gather_small", _pack_rows(small_in, C))
    small_full = _unpack_rows(small_pack, [(2, C), (CONV_WIDTH, C), (1, C), (1, C), (1, C)])

    started = gather_start("gather_start", [[slot_of(k) for k in g] for g in groups], small_pack)
    relayed, G = {}, {}

    def relay(g, after):
        if g >= len(groups):
            return None
        relayed[g] = gather_relay(f"gather_relay_{g}", started[g], after)
        return relayed[g][3]

    def ready(g, after):
        if g not in relayed:
            relay(g, after)
        for key, buf in zip(groups[g], gather_wait(f"gather_wait_{g}", relayed[g], after)):
            G[key] = buf

    b_pw1_full = small_full[0].reshape(1, 2 * C)
    w_dw_full = jnp.pad(small_full[1], ((0, CONV_TAPS_PAD - CONV_WIDTH), (0, 0)))
    b_dw_full, ln_g_full, ln_b_full = small_full[2], small_full[3], small_full[4]

    pad_gain = lambda g: jnp.pad(g, ((0, 0), (0, HEAD_PAD - QK_DIM)))
    q_gain_p, k_gain_p = pad_gain(mla_q_gain), pad_gain(mla_k_gain)
    tabs = rope_tables("rope_tables", positions.reshape(T, 1).astype(F32))

    def ffn_fwd(tag, h, norm, w_in, w_out, layer, g_in):
        hn = rms_fwd(f"{tag}_rms", h, norm)
        ready(g_in, hn)
        tok = relay(g_in + 1, hn)
        gu = mm_nn(f"{tag}_in", hn, G[(w_in, layer)], "col", 0, out_dtype=BF16, tm=2048, after=tok)
        act = swiglu_fwd(f"{tag}_act", gu)
        ready(g_in + 1, act)
        tok = relay(g_in + 2, act)
        out = mm_nn(f"{tag}_out", act, G[(w_out, layer)], "row", 0, out_dtype=F32, scale=FFN_RESIDUAL_WEIGHT, res=h,
                    tm=1024, tn=512, whole_k=True, after=tok)
        return out, (h, hn, gu, act)

    saved = []
    h = x[0]
    for i in range(depth):
        L = {}
        g0 = PER_LAYER * i
        h, L["ffn_a"] = ffn_fwd(f"l{i}_ffa", h, ffn_a_norm[i:i + 1], "ffn_a_w_in", "ffn_a_w_out", i, g0 + A_IN)
        L["h1"] = h
        hn = rms_fwd(f"l{i}_mix_rms", h, mix_norm[i:i + 1])
        L["hn_m"] = hn
        ready(g0 + MIX, hn)
        j = i // 2
        if i % 2 == 0:
            w_in_pad = jnp.pad(G[("mla_w_in", j)].reshape(D, lat_w), ((0, 0), (0, lat_pad - lat_w)))[None]
            uq = G[("mla_w_uq", j)].transpose(1, 0, 2).reshape(QL, H, QK_DIM)
            w_uq_pad = jnp.pad(uq, ((0, 0), (0, 0), (0, HEAD_PAD - QK_DIM))).reshape(1, QL, H * HEAD_PAD)
            lat = mm_nn(f"l{i}_lat", hn, w_in_pad, "row", 0, out_dtype=F32, tm=2048)
            cq, ckv = lat_norm_fwd(f"l{i}_latnorm", lat, mla_q_lat_norm[j:j + 1], mla_kv_lat_norm[j:j + 1])
            q_raw = mm_nn(f"l{i}_uq", cq, w_uq_pad, "row", 0, out_dtype=F32, tm=2048)
            kv_raw = mm_nn(f"l{i}_ukv", ckv, G[("mla_w_ukv", j)], "col", 0, out_dtype=F32, tm=2048)
            Qh, Kh, Vh = mla_prep_fwd(f"l{i}_prep", q_raw, kv_raw, lat, tabs, q_gain_p, k_gain_p, QL + KL)
            O = attn_fwd(f"l{i}_attn", Qh, Kh, Vh)
            tok = relay(g0 + B_IN, O)
            h = mm_nn(f"l{i}_wo", O, G[("mla_w_o", j)], "row", 0, out_dtype=F32, res=h, tm=1024, tn=1024, after=tok)
            L["mla"] = (lat, cq, ckv, q_raw, kv_raw, Qh, Kh, Vh, O, w_in_pad, w_uq_pad)
        else:
            ag = mm_nn(f"l{i}_pw1", hn, G[("conv_w_pw1", j)], "col", 0, out_dtype=F32, bias=b_pw1_full, tm=2048)
            u = glu_fwd(f"l{i}_glu", ag)
            yc = dwconv_fwd(f"l{i}_dw", u, w_dw_full, b_dw_full)
            cact = ln_silu_fwd(f"l{i}_ln", yc, ln_g_full, ln_b_full)
            tok = relay(g0 + B_IN, cact)
            h = mm_nn(f"l{i}_pw2", cact, G[("conv_w_pw2", j)], "row", 0, out_dtype=F32, res=h, tm=1024, tn=1024,
                      after=tok)
            L["conv"] = (ag, u, yc, cact)
        L["h2"] = h
        h, L["ffn_b"] = ffn_fwd(f"l{i}_ffb", h, ffn_b_norm[i:i + 1], "ffn_b_w_in", "ffn_b_w_out", i, g0 + B_IN)
        L["h3"] = h
        ready(g0 + PLE, h)
        pe = mm_nn(f"l{i}_ple_proj", p[i, 0], G[("ple_w_proj", i)], "col", 0, out_dtype=F32, tm=2048)
        hg = rms_fwd(f"l{i}_gate_rms", h, ple_gate_norm[i:i + 1])
        tok = relay(g0 + PER_LAYER, hg)
        z = mm_nn(f"l{i}_ple_gate", hg, G[("ple_w_gate", i)], "row", 0, out_dtype=F32, tm=2048, after=tok)
        h = ple_fwd(f"l{i}_ple", h, pe, z, ple_norm[i:i + 1])
        L["ple"] = (pe, hg, z)
        saved.append(L)

    d_h, loss_part = loss_head("loss_head", h, loss_target[0])
    loss = lax.psum(loss_part[0, 0], ("x", "y", "c"))

    GW = {}
    SG = {}
    ids_arr = jnp.stack([cc.astype(jnp.int32), chip])
    two = lambda a: a.reshape(-1, a.shape[-1])
    merged = {}
    pipe = {}
    order = list(reversed(range(len(groups))))
    ticks = [0]

    def put_small(name, i, val):
        SG.setdefault(name, {})[i] = val

    def reduce_tick(after):
        k, tok = ticks[0], after
        ticks[0] += 1
        grp = lambda j: order[j] if 0 <= j < len(order) else None
        g = grp(k - 4)
        if g is not None:
            full, _ = split_wait(f"share_wait_{g}", pipe[g][0], share_halves_plan(pipe[g][1]), tok)
            for a, (n, l) in enumerate(groups[g]):
                merged[n] = tuple(full[4 * a:4 * a + 4])
        g = grp(k - 3)
        if g is not None:
            sums, landed = split_wait(f"exchange_wait_{g}", pipe[g], chip_exchange_plan, tok)
            bufs, rows = [], []
            for (n, l), s, ld in zip(groups[g], sums, landed):
                merged[n] = reduce_adamw(f"adamw_{n}_{l}", ids_arr, s, ld, two(W[n]), two(M1[n]), two(V2[n]), l,
                                         W[n].shape[0], merged.get(n))
                bufs += list(merged[n])
                rows += [(l * 2 * s.shape[1], s.shape[1])] * 4
            pipe[g] = (split_start(f"share_start_{g}", bufs, [], len(bufs), share_halves_plan(rows), tok), rows)
            tok = pipe[g][0][4]
        g = grp(k - 1)
        if g is not None:
            parts, got = split_wait(f"swap_wait_{g}", pipe[g], swap_halves_plan, tok)
            sums = [add_halves(f"pair_sum_{n}_{l}", core_arr, q, r) for (n, l), q, r in zip(groups[g], parts, got)]
            pipe[g] = split_start(f"exchange_start_{g}", sums, [(s.shape, s.dtype) for s in sums], 3 * len(sums),
                                  chip_exchange_plan, tok)
            tok = pipe[g][4]
        g = grp(k)
        if g is not None:
            parts = [GW[key] for key in groups[g]]
            lands = [((q.shape[0], q.shape[1] // 2, q.shape[2]), q.dtype) for q in parts]
            pipe[g] = split_start(f"swap_start_{g}", parts, lands, len(parts), swap_halves_plan, tok)
            tok = pipe[g][4]
        return tok

    def ffn_bwd(tag, d_h, d_hb, norm, w_in, w_out, layer, fw, tok):
        h_in, hn, gu, act = fw
        GW[(w_out, layer)] = mm_tn(f"{tag}_dwout", act, d_hb, "row", scale=FFN_RESIDUAL_WEIGHT, tk=1408, tn=1024,
                                   after=tok)
        tok = reduce_tick(GW[(w_out, layer)])
        d_act = _mm_nt(f"{tag}_dact", d_hb, G[(w_out, layer)], "row", 0, act.shape[1], out_dtype=BF16,
                       scale=FFN_RESIDUAL_WEIGHT, to=1408, tc=2048, after=tok)
        dgu = swiglu_bwd(f"{tag}_dgu", gu, d_act)
        GW[(w_in, layer)] = mm_tn(f"{tag}_dwin", hn, dgu, "col")
        tok = reduce_tick(GW[(w_in, layer)])
        d_hn = _mm_nt(f"{tag}_dhn", dgu, G[(w_in, layer)], "col", 0, D, out_dtype=F32, tc=2816, after=tok)
        return (*rms_bwd_res(f"{tag}_drms", h_in, d_hn, norm, d_h), tok)

    d_hb, tok = None, None
    for i in reversed(range(depth)):
        L = saved[i]
        j = i // 2
        pe, hg, z = L["ple"]
        d_z, d_pe, g = ple_bwd(f"l{i}_dple", d_h, pe, z, ple_norm[i:i + 1])
        put_small("ple_norm", i, g)
        d_hg = _mm_nt(f"l{i}_dhg", d_z, G[("ple_w_gate", i)], "row", 0, D, out_dtype=F32, to=512, tc=2048, after=tok)
        GW[("ple_w_gate", i)] = mm_tn(f"l{i}_dwgate", hg, d_z, "row", tk=512, tn=1024)
        GW[("ple_w_proj", i)] = mm_tn(f"l{i}_dwproj", p[i, 0], d_pe, "col")
        tok = reduce_tick(GW[("ple_w_proj", i)])
        d_h, d_hb, g = rms_bwd_res(f"l{i}_dgate_rms", L["h3"], d_hg, ple_gate_norm[i:i + 1], d_h)
        put_small("ple_gate_norm", i, g)

        d_h, d_hb, g, tok = ffn_bwd(f"l{i}_ffb", d_h, d_hb, ffn_b_norm[i:i + 1], "ffn_b_w_in", "ffn_b_w_out", i,
                                    L["ffn_b"], tok)
        put_small("ffn_b_norm", i, g)

        hn = L["hn_m"]
        if i % 2 == 0:
            lat, cq, ckv, q_raw, kv_raw, Qh, Kh, Vh, O, w_in_pad, w_uq_pad = L["mla"]
            d_O = _mm_nt(f"l{i}_dO", d_hb, G[("mla_w_o", j)], "row", 0, H * D_V, out_dtype=BF16, to=512, tc=2048,
                         after=tok)
            GW[("mla_w_o", j)] = mm_tn(f"l{i}_dwo", O, d_hb, "row", tk=512, tn=1024)
            dQ, dK, dV = attn_bwd(f"l{i}_dattn", Qh, Kh, Vh, d_O)
            d_q_raw, d_kv_raw, d_kr, gq, gk = mla_prep_bwd(f"l{i}_dprep", dQ, dK, dV, q_raw, kv_raw, lat, tabs,
                                                           q_gain_p, k_gain_p, QL + KL)
            put_small("mla_q_gain", j, gq[:, :QK_DIM])
            put_small("mla_k_gain", j, gk[:, :QK_DIM])
            d_cq = _mm_nt(f"l{i}_dcq", d_q_raw, w_uq_pad, "row", 0, QL, out_dtype=F32, to=512, tc=2048)
            g_uq = mm_tn(f"l{i}_dwuq", cq, d_q_raw, "row", shards=1, out_dtype=F32, tk=512, tn=1024)
            g_uq = g_uq.reshape(QL, H, HEAD_PAD)[:, :, :QK_DIM].reshape(QL, N_CHIPS, -1).transpose(1, 0, 2)
            GW[("mla_w_uq", j)] = g_uq.astype(BF16)
            d_ckv = _mm_nt(f"l{i}_dckv", d_kv_raw, G[("mla_w_ukv", j)], "col", 0, KL, out_dtype=F32, to=512, tc=1024)
            GW[("mla_w_ukv", j)] = mm_tn(f"l{i}_dwukv", ckv, d_kv_raw, "col", tk=512, tn=1024)
            d_lat, gq, gk = lat_norm_bwd(f"l{i}_dlatnorm", lat, d_cq, d_ckv, d_kr, mla_q_lat_norm[j:j + 1],
                                         mla_kv_lat_norm[j:j + 1])
            put_small("mla_q_lat_norm", j, gq)
            put_small("mla_kv_lat_norm", j, gk)
            d_hn = _mm_nt(f"l{i}_dhn_lat", d_lat, w_in_pad, "row", 0, D, out_dtype=F32, to=1024, tc=lat_pad)
            g_in = mm_tn(f"l{i}_dwin_lat", hn, d_lat, "row", shards=1, out_dtype=F32, tk=1024, tn=lat_pad)
            GW[("mla_w_in", j)] = g_in[0, :, :lat_w].reshape(N_CHIPS, D // N_CHIPS, lat_w).astype(BF16)
            tok = reduce_tick(GW[("mla_w_in", j)])
        else:
            ag, u, yc, cact = L["conv"]
            d_cact = _mm_nt(f"l{i}_dcact", d_hb, G[("conv_w_pw2", j)], "row", 0, C, out_dtype=F32, to=512, tc=2048,
                            after=tok)
            GW[("conv_w_pw2", j)] = mm_tn(f"l{i}_dwpw2", cact, d_hb, "row", tk=512, tn=1024)
            d_yc, g1, g2, g3 = ln_silu_bwd(f"l{i}_dln", yc, d_cact, ln_g_full, ln_b_full)
            put_small("conv_ln_g", j, g1)
            put_small("conv_ln_b", j, g2)
            put_small("conv_b_dw", j, g3)
            d_u = dwconv_bwd_u(f"l{i}_ddw_u", d_yc, w_dw_full)
            put_small("conv_w_dw", j, dwconv_bwd_w(f"l{i}_ddw_w", u, d_yc))
            d_ag, g = glu_bwd(f"l{i}_dglu", ag, d_u)
            put_small("conv_b_pw1", j, g)
            d_hn = _mm_nt(f"l{i}_dhn_pw1", d_ag, G[("conv_w_pw1", j)], "col", 0, D, out_dtype=F32, tc=1024)
            GW[("conv_w_pw1", j)] = mm_tn(f"l{i}_dwpw1", hn, d_ag, "col", tn=1024)
            tok = reduce_tick(GW[("conv_w_pw1", j)])
        d_h, d_hb, g = rms_bwd_res(f"l{i}_dmix_rms", L["h1"], d_hn, mix_norm[i:i + 1], d_h)
        put_small("mix_norm", i, g)

        d_h, d_hb, g, tok = ffn_bwd(f"l{i}_ffa", d_h, d_hb, ffn_a_norm[i:i + 1], "ffn_a_w_in", "ffn_a_w_out", i,
                                    L["ffn_a"], tok)
        put_small("ffn_a_norm", i, g)
    grad_x = d_h[None]

    tok = d_h
    for _ in range(4):
        tok = reduce_tick(tok)
    names = [n for n, _ in BIG_WEIGHTS]
    grads, delta, new_m, new_v = {}, {}, {}, {}
    for n in names:
        grads[n], delta[n], new_m[n], new_v[n] = [a.reshape(W[n].shape) for a in merged[n]]

    rep = []
    for n in REPLICATED_SMALL:
        rep.append(jnp.concatenate([SG[n][i] for i in sorted(SG[n])], axis=0))
    shd = [SG["conv_b_pw1"][0].reshape(2, C), SG["conv_w_dw"][0][:CONV_WIDTH], SG["conv_b_dw"][0],
           SG["conv_ln_g"][0], SG["conv_ln_b"][0]]
    red = allreduce_pack("allreduce_small", _pack_rows(rep + shd, D))
    red = _unpack_rows(red, [a.shape for a in rep + shd])
    for n, g in zip(REPLICATED_SMALL, red):
        grads[n] = g
    own = lambda a, w: lax.dynamic_slice_in_dim(a, chip * w, w, axis=a.ndim - 1)
    sh = red[len(rep):]
    grads["conv_b_pw1"] = own(sh[0].reshape(1, 2 * C), 2 * C // N_CHIPS)
    grads["conv_w_dw"] = own(sh[1], C // N_CHIPS)[None]
    grads["conv_b_dw"] = own(sh[2], C // N_CHIPS)
    grads["conv_ln_g"] = own(sh[3], C // N_CHIPS)
    grads["conv_ln_b"] = own(sh[4], C // N_CHIPS)

    small = REPLICATED_SMALL + SHARDED_SMALL
    shapes = [two(W[n]).shape for n in small]
    packs = [_pack_rows([two(src[n]) for n in small], D) for src in (W, grads, M1, V2)]
    outs = adamw("adamw_small", *packs)
    for dst, pk in zip((delta, new_m, new_v), outs):
        for n, a in zip(small, _unpack_rows(pk, shapes)):
            dst[n] = a.reshape(W[n].shape)
    for n in small:
        grads[n] = grads[n].reshape(W[n].shape)

    return (loss, grad_x, *[grads[n] for n in WEIGHT_ORDER], *[delta[n] for n in WEIGHT_ORDER],
            *[new_m[n] for n in WEIGHT_ORDER], *[new_v[n] for n in WEIGHT_ORDER])
```

```python
import functools
import math

import jax
import jax.numpy as jnp
from jax import lax
from jax.experimental import pallas as pl
from jax.experimental.pallas import tpu as pltpu

F32, BF16 = jnp.float32, jnp.bfloat16
MESH = pl.DeviceIdType.MESH

N_CHIPS = 4
EPS = 1e-6
D_NOPE, D_ROPE, D_V = 128, 64, 128
QK_DIM = D_NOPE + D_ROPE
HEAD_PAD = 256
ROPE_THETA = 10000.0
CONV_WIDTH = 31
CONV_TAPS_PAD = 32
FFN_RESIDUAL_WEIGHT = 0.5
ADAM_LR, ADAM_B1, ADAM_B2, ADAM_EPS, ADAM_WD, ADAM_STEP = 0.001, 0.9, 0.999, 1e-08, 0.01, 10
VMEM_LIMIT_BYTES = 56 * 1024 * 1024
NEG_BIG = -1e30

NN_DIMS = (((1,), (0,)), ((), ()))
NT_DIMS = (((1,), (1,)), ((), ()))
TN_DIMS = (((0,), (0,)), ((), ()))


def _cparams(semantics=None):
    kw = dict(vmem_limit_bytes=VMEM_LIMIT_BYTES)
    if semantics is not None:
        kw["dimension_semantics"] = semantics
    return pltpu.CompilerParams(**kw)


def _tile(n, pref, mult=128):
    if n <= pref:
        return n
    t = (pref // mult) * mult
    while t >= mult:
        if n % t == 0:
            return t
        t -= mult
    return n


def _rowwise(name, fn, rows, vecs, outs, accs=(), tm=256, rc=32):
    T = rows[0].shape[0]
    tm = min(tm, T)
    rc = min(rc, tm)
    nr, nv, no, na = len(rows), len(vecs), len(outs), len(accs)
    steps = tm // rc

    def body(*refs):
        row_refs = refs[:nr]
        vec_refs = refs[nr:nr + nv]
        out_refs = refs[nr + nv:nr + nv + no]
        acc_refs = refs[nr + nv + no:]
        if na:
            @pl.when(pl.program_id(0) == 0)
            def _():
                for a in acc_refs:
                    a[...] = jnp.zeros_like(a)

        def step(r, carry):
            sl = pl.ds(pl.multiple_of(r * rc, rc), rc)
            res = fn(*[x[sl, :] for x in row_refs], *[v[...] for v in vec_refs])
            for o, val in zip(out_refs, res[:no]):
                o[sl, :] = val.astype(o.dtype)
            return tuple(c + val for c, val in zip(carry, res[no:]))

        init = tuple(jnp.zeros(s, F32) for s in accs)
        tot = lax.fori_loop(0, steps, step, init)
        for a, val in zip(acc_refs, tot):
            a[...] += val

    in_specs = [pl.BlockSpec((tm, x.shape[1]), lambda i: (i, 0)) for x in rows]
    in_specs += [pl.BlockSpec(v.shape, lambda i: (0, 0)) for v in vecs]
    out_specs = [pl.BlockSpec((tm, d), lambda i: (i, 0)) for d, _ in outs]
    out_specs += [pl.BlockSpec(s, lambda i: (0, 0)) for s in accs]
    out_shape = [jax.ShapeDtypeStruct((T, d), dt) for d, dt in outs]
    out_shape += [jax.ShapeDtypeStruct(s, F32) for s in accs]
    return pl.pallas_call(
        body, name=name, grid=(T // tm,), in_specs=in_specs, out_specs=out_specs, out_shape=out_shape,
        compiler_params=_cparams(("arbitrary",)),
    )(*rows, *vecs)


def _colsum(v):
    return jnp.sum(v, axis=0, keepdims=True)


def _rstd(x):
    return lax.rsqrt(jnp.mean(x * x, axis=-1, keepdims=True) + EPS)


def _rms_bwd(x, dy, g):
    r = _rstd(x)
    xh = x * r
    dyg = dy * g
    dx = r * (dyg - xh * jnp.mean(dyg * xh, axis=-1, keepdims=True))
    return dx, dy * xh


def _sigmoid(x):
    return 1.0 / (1.0 + jnp.exp(-x))


def rms_fwd(name, h, g):
    def fn(x, gv):
        return ((x * _rstd(x)) * gv,)
    return _rowwise(name, fn, [h], [g], [(h.shape[1], BF16)])[0]


def rms_bwd_res(name, h, d_y, g, d_res):
    D = h.shape[1]

    def fn(x, dy, dr, gv):
        dx, dgr = _rms_bwd(x, dy, gv)
        dh = dr + dx
        return dh, dh, _colsum(dgr)
    return _rowwise(name, fn, [h, d_y, d_res], [g], [(D, F32), (D, BF16)], [(1, D)], tm=128)


def swiglu_fwd(name, gu):
    Fh = gu.shape[1] // 2

    def fn(v):
        g = v[:, :Fh].astype(F32)
        u = v[:, Fh:].astype(F32)
        return (g * _sigmoid(g) * u,)
    return _rowwise(name, fn, [gu], [], [(Fh, BF16)])[0]


def swiglu_bwd(name, gu, d_act):
    Fh = gu.shape[1] // 2

    def fn(v, da):
        g = v[:, :Fh].astype(F32)
        u = v[:, Fh:].astype(F32)
        da = da.astype(F32)
        s = _sigmoid(g)
        d_u = da * g * s
        d_g = da * u * s * (1.0 + g * (1.0 - s))
        return (jnp.concatenate([d_g, d_u], axis=-1),)
    return _rowwise(name, fn, [gu, d_act], [], [(2 * Fh, BF16)])[0]


def loss_head(name, y, target):
    D = y.shape[1]

    def fn(yv, tv):
        e = yv - tv
        tot = jnp.sum(_colsum(e * e), axis=1, keepdims=True) * (0.5 / D)
        return e * (1.0 / D), jnp.broadcast_to(tot, (1, 128))
    return _rowwise(name, fn, [y, target], [], [(D, F32)], [(1, 128)])


def ple_fwd(name, h, pe, z, g_e):
    def fn(hv, pv, zv, gv):
        return (hv + (pv * _rstd(pv)) * gv * _sigmoid(zv),)
    return _rowwise(name, fn, [h, pe, z], [g_e], [(h.shape[1], F32)], tm=128)[0]


def ple_bwd(name, d_h, pe, z, g_e):
    D = d_h.shape[1]

    def fn(dh, pv, zv, gv):
        gate = _sigmoid(zv)
        e = (pv * _rstd(pv)) * gv
        d_z = dh * e * gate * (1.0 - gate)
        d_pe, dgr = _rms_bwd(pv, dh * gate, gv)
        return d_z, d_pe, _colsum(dgr)
    return _rowwise(name, fn, [d_h, pe, z], [g_e], [(D, BF16), (D, BF16)], [(1, D)], tm=128)


def lat_norm_fwd(name, lat, g_q, g_kv):
    QL, KL = g_q.shape[1], g_kv.shape[1]

    def fn(v, gq, gk):
        a = v[:, :QL]
        b = v[:, QL:QL + KL]
        return (a * _rstd(a)) * gq, (b * _rstd(b)) * gk
    return _rowwise(name, fn, [lat], [g_q, g_kv], [(QL, BF16), (KL, BF16)])


def lat_norm_bwd(name, lat, d_cq, d_ckv, d_krope, g_q, g_kv):
    QL, KL = g_q.shape[1], g_kv.shape[1]

    def fn(v, dq, dk, dr, gq, gk):
        da, ga = _rms_bwd(v[:, :QL], dq, gq)
        db, gb = _rms_bwd(v[:, QL:QL + KL], dk, gk)
        return jnp.concatenate([da, db, dr], axis=-1), _colsum(ga), _colsum(gb)
    return _rowwise(name, fn, [lat, d_cq, d_ckv, d_krope], [g_q, g_kv],
                    [(lat.shape[1], BF16)], [(1, QL), (1, KL)])


def glu_fwd(name, ag):
    C = ag.shape[1] // 2

    def fn(v):
        return (v[:, :C] * _sigmoid(v[:, C:]),)
    return _rowwise(name, fn, [ag], [], [(C, F32)], tm=128)[0]


def glu_bwd(name, ag, d_u):
    C = ag.shape[1] // 2

    def fn(v, du):
        a = v[:, :C]
        s = _sigmoid(v[:, C:])
        d = jnp.concatenate([du * s, du * a * s * (1.0 - s)], axis=-1)
        return d, _colsum(d)
    return _rowwise(name, fn, [ag, d_u], [], [(2 * C, BF16)], [(1, 2 * C)], tm=128)


def ln_silu_fwd(name, yc, g, b):
    def fn(v, gv, bv):
        xc = v - jnp.mean(v, axis=-1, keepdims=True)
        ln = xc * lax.rsqrt(jnp.mean(xc * xc, axis=-1, keepdims=True) + EPS) * gv + bv
        return (ln * _sigmoid(ln),)
    return _rowwise(name, fn, [yc], [g, b], [(yc.shape[1], BF16)], tm=128)[0]


def ln_silu_bwd(name, yc, d_out, g, b):
    C = yc.shape[1]

    def fn(v, do, gv, bv):
        xc = v - jnp.mean(v, axis=-1, keepdims=True)
        r = lax.rsqrt(jnp.mean(xc * xc, axis=-1, keepdims=True) + EPS)
        xh = xc * r
        ln = xh * gv + bv
        s = _sigmoid(ln)
        d_ln = do * s * (1.0 + ln * (1.0 - s))
        dxh = d_ln * gv
        dy = r * (dxh - jnp.mean(dxh, axis=-1, keepdims=True) - xh * jnp.mean(dxh * xh, axis=-1, keepdims=True))
        return dy, _colsum(d_ln * xh), _colsum(d_ln), _colsum(dy)
    return _rowwise(name, fn, [yc, d_out], [g, b], [(C, F32)], [(1, C), (1, C), (1, C)], tm=128)


def add_halves(name, core, own, got):
    S, Rh, C = got.shape
    tr = _tile(Rh, 256, 16)
    nrb = Rh // tr

    def body(c_ref, a_ref, b_ref, o_ref):
        o_ref[...] = (a_ref[...].astype(F32) + b_ref[...].astype(F32)).astype(o_ref.dtype)

    gs = pltpu.PrefetchScalarGridSpec(
        num_scalar_prefetch=1, grid=(S, nrb),
        in_specs=[pl.BlockSpec((None, tr, C), lambda s, r, c: (s, c[0] * nrb + r, 0)),
                  pl.BlockSpec((None, tr, C), lambda s, r, c: (s, r, 0))],
        out_specs=pl.BlockSpec((None, tr, C), lambda s, r, c: (s, r, 0)))
    return pl.pallas_call(
        body, name=name, grid_spec=gs, out_shape=jax.ShapeDtypeStruct((S, Rh, C), BF16),
        compiler_params=_cparams(("arbitrary", "arbitrary")))(core, own, got)


def adamw_half(name, ids, grad, w, m, v, layer, layers, into):
    own = isinstance(grad, tuple)
    Rh, C = grad[0].shape[1:] if own else grad.shape
    tr = _tile(Rh, 128, 16)
    nrb = Rh // tr
    n_grad = 4 if own else 1
    c1 = 1.0 / (1.0 - ADAM_B1 ** ADAM_STEP)
    c2 = 1.0 / (1.0 - ADAM_B2 ** ADAM_STEP)

    def body(ids_ref, *refs):
        g_refs, (w_ref, m_ref, v_ref) = refs[:n_grad], refs[n_grad:n_grad + 3]
        outs = refs[-5:] if own else refs[-4:]
        go_ref, d_ref, nm_ref, nv_ref = outs[:4]
        if own:
            gv = ((g_refs[0][...].astype(F32) + g_refs[1][...].astype(F32))
                  + (g_refs[2][...].astype(F32) + g_refs[3][...].astype(F32)))
            outs[4][...] = gv
        else:
            gv = g_refs[0][...]
        nm = ADAM_B1 * m_ref[...] + (1.0 - ADAM_B1) * gv
        nv = ADAM_B2 * v_ref[...] + (1.0 - ADAM_B2) * (gv * gv)
        go_ref[...] = gv
        d_ref[...] = -ADAM_LR * ((nm * c1) / (jnp.sqrt(nv * c2) + ADAM_EPS) + ADAM_WD * w_ref[...])
        nm_ref[...] = nm
        nv_ref[...] = nv

    half = (lambda ids: ids[0]) if own else (lambda ids: 1 - ids[0])
    rows = pl.BlockSpec((tr, C), lambda r, ids: ((2 * layer + half(ids)) * nrb + r, 0))
    plain = pl.BlockSpec((tr, C), lambda r, ids: (r, 0))
    if own:
        slot = lambda flip: pl.BlockSpec((None, tr, C), lambda r, ids: (ids[1] ^ flip, r, 0))
        in_specs = [slot(0), slot(1), slot(2), slot(3), rows, rows, rows]
        operands = [ids, grad[0], grad[1], grad[1], grad[1], w, m, v]
    else:
        in_specs = [plain, rows, rows, rows]
        operands = [ids, grad, w, m, v]
    aliases = {}
    if into is not None:
        in_specs += [pl.BlockSpec(memory_space=pl.ANY)] * 4
        aliases = {len(operands) + k: k for k in range(4)}
        operands += list(into)
    full = jax.ShapeDtypeStruct((layers * 2 * Rh, C), F32)
    gs = pltpu.PrefetchScalarGridSpec(num_scalar_prefetch=1, grid=(nrb,), in_specs=in_specs,
                                      out_specs=[rows] * 4 + [plain] * own)
    return pl.pallas_call(
        body, name=name, grid_spec=gs, out_shape=[full] * 4 + [jax.ShapeDtypeStruct((Rh, C), F32)] * own,
        input_output_aliases=aliases, compiler_params=_cparams(("arbitrary",)))(*operands)


def cast_into_slot(name, w, chip, layer, layers, after=None):
    R, C = w.shape[0] // layers, w.shape[1]
    tr = _tile(R, 256, 16)
    nrb = R // tr

    def body(s_ref, w_ref, *rest):
        rest[-1][...] = w_ref[...].astype(BF16)

    in_specs = [pl.BlockSpec((tr, C), lambda r, s: (layer * nrb + r, 0))]
    operands = [chip, w]
    if after is not None:
        in_specs.append(pl.BlockSpec(memory_space=pl.ANY))
        operands.append(after)
    gs = pltpu.PrefetchScalarGridSpec(
        num_scalar_prefetch=1, grid=(nrb,), in_specs=in_specs,
        out_specs=pl.BlockSpec((None, tr, C), lambda r, s: (s[0], r, 0)))
    return pl.pallas_call(
        body, name=name, grid_spec=gs, out_shape=jax.ShapeDtypeStruct((N_CHIPS, R, C), BF16),
        compiler_params=_cparams(("arbitrary",)))(*operands)


def adamw(name, w, g, m, v):
    R, C = w.shape
    tr = _tile(R, 256, 8)
    c1 = 1.0 / (1.0 - ADAM_B1 ** ADAM_STEP)
    c2 = 1.0 / (1.0 - ADAM_B2 ** ADAM_STEP)

    def body(w_ref, g_ref, m_ref, v_ref, d_ref, nm_ref, nv_ref):
        gv = g_ref[...]
        nm = ADAM_B1 * m_ref[...] + (1.0 - ADAM_B1) * gv
        nv = ADAM_B2 * v_ref[...] + (1.0 - ADAM_B2) * (gv * gv)
        d_ref[...] = -ADAM_LR * ((nm * c1) / (jnp.sqrt(nv * c2) + ADAM_EPS) + ADAM_WD * w_ref[...])
        nm_ref[...] = nm
        nv_ref[...] = nv

    spec = pl.BlockSpec((tr, C), lambda r: (r, 0))
    return pl.pallas_call(
        body, name=name, grid=(R // tr,), in_specs=[spec] * 4, out_specs=[spec] * 3,
        out_shape=[jax.ShapeDtypeStruct((R, C), F32)] * 3, compiler_params=_cparams(("arbitrary",)))(w, g, m, v)


def _matmul(name, a, b, *, grid, a_blk, a_map, b_blk, b_map, o_shape, o_dtype, o_blk, o_map, dims,
            scale=None, res=None, bias=None, bias_blk=None, bias_map=None, alias_into=None, after=None):
    nk = grid[2]
    has_res, has_bias, has_into = res is not None, bias is not None, alias_into is not None
    acc_shape = tuple(d for d in o_blk if d is not None)

    def body(*refs):
        a_ref, b_ref = refs[0], refs[1]
        pos = 2
        res_ref = bias_ref = None
        if has_res:
            res_ref = refs[pos]
            pos += 1
        if has_bias:
            bias_ref = refs[pos]
            pos += 1
        if has_into:
            pos += 1
        if after is not None:
            pos += 1
        o_ref = refs[pos]
        av, bv = a_ref[...], b_ref[...]
        if bv.ndim == 3:
            bv = bv.reshape(-1, bv.shape[-1])
        if av.dtype != BF16:
            av = av.astype(BF16)
        if bv.dtype != BF16:
            bv = bv.astype(BF16)
        part = lax.dot_general(av, bv, dims, preferred_element_type=F32)

        def finish(acc):
            if scale is not None:
                acc = acc * scale
            if has_bias:
                acc = acc + bias_ref[...]
            if has_res:
                acc = acc + res_ref[...]
            o_ref[...] = acc.astype(o_ref.dtype)

        if nk == 1:
            finish(part)
        else:
            acc_ref = refs[pos + 1]
            k = pl.program_id(2)

            @pl.when(k == 0)
            def _():
                acc_ref[...] = part

            @pl.when(k > 0)
            def _():
                acc_ref[...] += part

            @pl.when(k == nk - 1)
            def _():
                finish(acc_ref[...])

    operands = [a, b]
    in_specs = [pl.BlockSpec(a_blk, a_map), pl.BlockSpec(b_blk, b_map)]
    if has_res:
        operands.append(res)
        in_specs.append(pl.BlockSpec(o_blk, o_map))
    if has_bias:
        operands.append(bias)
        in_specs.append(pl.BlockSpec(bias_blk, bias_map))
    aliases = {}
    if has_into:
        aliases = {len(operands): 0}
        operands.append(alias_into)
        in_specs.append(pl.BlockSpec(memory_space=pl.ANY))
    if after is not None:
        operands.append(after)
        in_specs.append(pl.BlockSpec(memory_space=pl.ANY))
    return pl.pallas_call(
        body, name=name, grid=grid, in_specs=in_specs, out_specs=pl.BlockSpec(o_blk, o_map),
        out_shape=jax.ShapeDtypeStruct(o_shape, o_dtype),
        scratch_shapes=[pltpu.VMEM(acc_shape, F32)] if nk > 1 else [],
        input_output_aliases=aliases,
        compiler_params=_cparams(("parallel", "parallel", "arbitrary")),
    )(*operands)


def mm_nn(name, a, w3, kind, layer, *, out_dtype, scale=None, res=None, bias=None, tm=1024, tn=512, tk=2048,
          after=None, whole_k=False):
    M, K = a.shape
    S, _, C = w3.shape
    tm = _tile(M, tm, 16)
    b_blk = None
    if kind == "col":
        N = S * C
        tk, tn = _tile(K, tk), _tile(C, tn)
        kb, nb = K // tk, C // tn
        b_map = lambda n, m, k: (n // nb, layer * kb + k, n % nb)
    elif whole_k:
        N, K4, tk, tn = C, K // S, K, _tile(C, tn)
        b_blk, b_map = (S, K4, tn), lambda n, m, k: (0, layer, n)
    else:
        N, K4 = C, K // S
        tk, tn = _tile(K4, tk), _tile(C, tn)
        kb4 = K4 // tk
        b_map = lambda n, m, k: (k // kb4, layer * kb4 + k % kb4, n)
    return _matmul(
        name, a, w3, grid=(N // tn, M // tm, K // tk),
        a_blk=(tm, tk), a_map=lambda n, m, k: (m, k), b_blk=b_blk or (None, tk, tn), b_map=b_map,
        o_shape=(M, N), o_dtype=out_dtype, o_blk=(tm, tn), o_map=lambda n, m, k: (m, n), dims=NN_DIMS,
        scale=scale, res=res, bias=bias, bias_blk=(1, tn), bias_map=lambda n, m, k: (0, n), after=after)


def _mm_nt(name, g, w3, kind, layer, K, *, out_dtype, scale=None, tm=1024, to=1024, tc=1408, after=None):
    M, N = g.shape
    S, _, C = w3.shape
    tm = _tile(M, tm, 16)
    tc = _tile(C, tc)
    if kind == "col":
        nb = C // tc
        to = _tile(K, to)
        ob = K // to
        b_map = lambda o, m, c: (c // nb, layer * ob + o, c % nb)
    else:
        K4 = K // S
        to = _tile(K4, to)
        ob4 = K4 // to
        b_map = lambda o, m, c: (o // ob4, layer * ob4 + o % ob4, c)
    return _matmul(
        name, g, w3, grid=(K // to, M // tm, N // tc),
        a_blk=(tm, tc), a_map=lambda o, m, c: (m, c), b_blk=(None, to, tc), b_map=b_map,
        o_shape=(M, K), o_dtype=out_dtype, o_blk=(tm, to), o_map=lambda o, m, c: (m, o), dims=NT_DIMS,
        scale=scale, after=after)


def mm_tn(name, a, g, kind, *, shards=N_CHIPS, layer=0, layers=1, into=None, out_dtype=BF16, scale=None,
          tk=1024, tn=1408, tm=2048, after=None):
    M, K = a.shape
    N = g.shape[1]
    tm = _tile(M, tm, 16)
    if kind == "col":
        C = N // shards
        tk, tn = _tile(K, tk), _tile(C, tn)
        kb, nb = K // tk, C // tn
        o_shape = (shards, layers * K, C)
        o_map = lambda k, n, m: (n // nb, layer * kb + k, n % nb)
    else:
        K4 = K // shards
        tk, tn = _tile(K4, tk), _tile(N, tn)
        kb4 = K4 // tk
        o_shape = (shards, layers * K4, N)
        o_map = lambda k, n, m: (k // kb4, layer * kb4 + k % kb4, n)
    return _matmul(
        name, a, g, grid=(K // tk, N // tn, M // tm),
        a_blk=(tm, tk), a_map=lambda k, n, m: (m, k), b_blk=(tm, tn), b_map=lambda k, n, m: (m, n),
        o_shape=o_shape, o_dtype=out_dtype, o_blk=(None, tk, tn), o_map=o_map, dims=TN_DIMS,
        scale=scale, alias_into=into, after=after)


def rope_tables(name, pos):
    T = pos.shape[0]
    half = D_ROPE // 2

    def body(p_ref, c_ref, s1_ref, s2_ref):
        lane = lax.broadcasted_iota(jnp.int32, (T, 128), 1)
        idx = (lane & (half - 1)).astype(F32)
        ang = p_ref[...] * jnp.exp(idx * (-2.0 * math.log(ROPE_THETA) / D_ROPE))
        cs, sn = jnp.cos(ang), jnp.sin(ang)
        c_ref[...] = jnp.where(lane < D_ROPE, cs, 0.0)
        s1_ref[...] = jnp.where(lane < half, -sn, 0.0)
        s2_ref[...] = jnp.where((lane >= half) & (lane < D_ROPE), sn, 0.0)

    return pl.pallas_call(body, name=name, out_shape=[jax.ShapeDtypeStruct((T, 128), F32)] * 3,
                          compiler_params=_cparams())(pos)


def _rope(v, cs, s1, s2):
    return v * cs + pltpu.roll(v, 128 - D_ROPE // 2, 1) * s1 + pltpu.roll(v, D_ROPE // 2, 1) * s2


def _rope_bwd(d, cs, s1, s2):
    return d * cs + pltpu.roll(d * s1, D_ROPE // 2, 1) + pltpu.roll(d * s2, 128 - D_ROPE // 2, 1)


def _head_rstd(n, r):
    ms = (jnp.sum(n * n, axis=-1, keepdims=True) + jnp.sum(r * r, axis=-1, keepdims=True)) * (1.0 / QK_DIM)
    return lax.rsqrt(ms + EPS)


def mla_prep_fwd(name, q_raw, kv_raw, lat, tabs, q_gain, k_gain, rope_col, tm=128):
    T = q_raw.shape[0]
    H = q_raw.shape[1] // HEAD_PAD
    tm = min(tm, T)
    rope_blk = rope_col // 128

    def body(q_ref, kv_ref, kr_ref, c_ref, s1_ref, s2_ref, qg_ref, kg_ref, Q_ref, K_ref, V_ref):
        cs, s1, s2 = c_ref[...], s1_ref[...], s2_ref[...]
        qg, kg = qg_ref[...], kg_ref[...]
        kr = kr_ref[...]
        for h in range(H):
            lo = HEAD_PAD * h
            n, r = q_ref[:, lo:lo + 128], q_ref[:, lo + 128:lo + 256]
            rs = _head_rstd(n, r)
            Q_ref[h, :, 0:128] = (n * rs * qg[:, :128]).astype(BF16)
            Q_ref[h, :, 128:256] = _rope(r * rs * qg[:, 128:], cs, s1, s2).astype(BF16)
            n = kv_ref[:, lo:lo + 128]
            rs = _head_rstd(n, kr)
            K_ref[h, :, 0:128] = (n * rs * kg[:, :128]).astype(BF16)
            K_ref[h, :, 128:256] = _rope(kr * rs * kg[:, 128:], cs, s1, s2).astype(BF16)
            V_ref[h] = kv_ref[:, lo + 128:lo + 256].astype(BF16)

    row = lambda w: pl.BlockSpec((tm, w), lambda i: (i, 0))
    vec = pl.BlockSpec((1, HEAD_PAD), lambda i: (0, 0))
    return pl.pallas_call(
        body, name=name, grid=(T // tm,),
        in_specs=[row(H * HEAD_PAD), row(H * HEAD_PAD), pl.BlockSpec((tm, 128), lambda i: (i, rope_blk)),
                  row(128), row(128), row(128), vec, vec],
        out_specs=[pl.BlockSpec((H, tm, HEAD_PAD), lambda i: (0, i, 0))] * 2 + [pl.BlockSpec((H, tm, D_V), lambda i: (0, i, 0))],
        out_shape=[jax.ShapeDtypeStruct((H, T, HEAD_PAD), BF16)] * 2 + [jax.ShapeDtypeStruct((H, T, D_V), BF16)],
        compiler_params=_cparams(("arbitrary",)),
    )(q_raw, kv_raw, lat, *tabs, q_gain, k_gain)


def mla_prep_bwd(name, dQ, dK, dV, q_raw, kv_raw, lat, tabs, q_gain, k_gain, rope_col, tm=128):
    T = q_raw.shape[0]
    H = q_raw.shape[1] // HEAD_PAD
    tm = min(tm, T)
    rope_blk = rope_col // 128

    def body(dQ_ref, dK_ref, dV_ref, q_ref, kv_ref, kr_ref, c_ref, s1_ref, s2_ref, qg_ref, kg_ref,
             dq_ref, dkv_ref, dkr_ref, dqg_ref, dkg_ref):
        @pl.when(pl.program_id(0) == 0)
        def _():
            dqg_ref[...] = jnp.zeros_like(dqg_ref)
            dkg_ref[...] = jnp.zeros_like(dkg_ref)

        cs, s1, s2 = c_ref[...], s1_ref[...], s2_ref[...]
        qg, kg = qg_ref[...], kg_ref[...]
        kr = kr_ref[...]
        dkr = jnp.zeros((tm, 128), F32)
        gq_n = jnp.zeros((1, 128), F32)
        gq_r = jnp.zeros((1, 128), F32)
        gk_n = jnp.zeros((1, 128), F32)
        gk_r = jnp.zeros((1, 128), F32)

        def norm_bwd(n, r, dn, dr, gain):
            rs = _head_rstd(n, r)
            nh, rh = n * rs, r * rs
            dng, drg = dn * gain[:, :128], dr * gain[:, 128:]
            mean = (jnp.sum(dng * nh, axis=-1, keepdims=True) + jnp.sum(drg * rh, axis=-1, keepdims=True)) * (1.0 / QK_DIM)
            return rs * (dng - nh * mean), rs * (drg - rh * mean), _colsum(dn * nh), _colsum(dr * rh)

        for h in range(H):
            lo = HEAD_PAD * h
            n, r = q_ref[:, lo:lo + 128], q_ref[:, lo + 128:lo + 256]
            dn = dQ_ref[h, :, 0:128].astype(F32)
            dr = _rope_bwd(dQ_ref[h, :, 128:256].astype(F32), cs, s1, s2)
            a, b, g1, g2 = norm_bwd(n, r, dn, dr, qg)
            dq_ref[:, lo:lo + 128] = a.astype(BF16)
            dq_ref[:, lo + 128:lo + 256] = b.astype(BF16)
            gq_n, gq_r = gq_n + g1, gq_r + g2
            n = kv_ref[:, lo:lo + 128]
            dn = dK_ref[h, :, 0:128].astype(F32)
            dr = _rope_bwd(dK_ref[h, :, 128:256].astype(F32), cs, s1, s2)
            a, b, g1, g2 = norm_bwd(n, kr, dn, dr, kg)
            dkv_ref[:, lo:lo + 128] = a.astype(BF16)
            dkv_ref[:, lo + 128:lo + 256] = dV_ref[h].astype(BF16)
            dkr = dkr + b
            gk_n, gk_r = gk_n + g1, gk_r + g2
        dkr_ref[...] = dkr
        dqg_ref[:, 0:128] += gq_n
        dqg_ref[:, 128:256] += gq_r
        dkg_ref[:, 0:128] += gk_n
        dkg_ref[:, 128:256] += gk_r

    row = lambda w: pl.BlockSpec((tm, w), lambda i: (i, 0))
    vec = pl.BlockSpec((1, HEAD_PAD), lambda i: (0, 0))
    hd = lambda w: pl.BlockSpec((H, tm, w), lambda i: (0, i, 0))
    return pl.pallas_call(
        body, name=name, grid=(T // tm,),
        in_specs=[hd(HEAD_PAD), hd(HEAD_PAD), hd(D_V), row(H * HEAD_PAD), row(H * HEAD_PAD),
                  pl.BlockSpec((tm, 128), lambda i: (i, rope_blk)), row(128), row(128), row(128), vec, vec],
        out_specs=[row(H * HEAD_PAD), row(H * HEAD_PAD), row(128), vec, vec],
        out_shape=[jax.ShapeDtypeStruct((T, H * HEAD_PAD), BF16)] * 2 + [jax.ShapeDtypeStruct((T, 128), F32)]
        + [jax.ShapeDtypeStruct((1, HEAD_PAD), F32)] * 2,
        compiler_params=_cparams(("arbitrary",)),
    )(dQ, dK, dV, q_raw, kv_raw, lat, *tabs, q_gain, k_gain)


def _causal_probs(q, k, scale, row0):
    s = lax.dot_general(q, k, NT_DIMS, preferred_element_type=F32) * scale
    row = row0 + lax.broadcasted_iota(jnp.int32, s.shape, 0)
    col = lax.broadcasted_iota(jnp.int32, s.shape, 1)
    s = jnp.where(col <= row, s, NEG_BIG)
    p = jnp.exp(s - jnp.max(s, axis=-1, keepdims=True))
    return p, jnp.sum(p, axis=-1, keepdims=True)


def attn_fwd(name, Q, K, V, tq=512):
    H, T, E = Q.shape
    tq = min(tq, T)
    nq = T // tq
    scale = QK_DIM ** -0.5

    def body(q_ref, k_ref, v_ref, o_ref):
        i = pl.program_id(1)
        for ib in range(nq):
            @pl.when(i == ib)
            def _():
                n = (ib + 1) * tq
                p, l = _causal_probs(q_ref[...], k_ref[0:n, :], scale, ib * tq)
                o = jnp.dot(p.astype(BF16), v_ref[0:n, :], preferred_element_type=F32)
                o_ref[...] = (o / l).astype(o_ref.dtype)

    return pl.pallas_call(
        body, name=name, grid=(H, nq),
        in_specs=[pl.BlockSpec((None, tq, E), lambda h, i: (h, i, 0)),
                  pl.BlockSpec((None, T, E), lambda h, i: (h, 0, 0)),
                  pl.BlockSpec((None, T, D_V), lambda h, i: (h, 0, 0))],
        out_specs=pl.BlockSpec((tq, D_V), lambda h, i: (i, h)),
        out_shape=jax.ShapeDtypeStruct((T, H * D_V), BF16),
        compiler_params=_cparams(("parallel", "arbitrary")),
    )(Q, K, V)


def attn_bwd(name, Q, K, V, dO, tq=512):
    H, T, E = Q.shape
    tq = min(tq, T)
    nq = T // tq
    scale = QK_DIM ** -0.5

    def body(q_ref, k_ref, v_ref, do_ref, dq_ref, dk_ref, dv_ref):
        i = pl.program_id(1)

        @pl.when(i == 0)
        def _():
            dk_ref[...] = jnp.zeros_like(dk_ref)
            dv_ref[...] = jnp.zeros_like(dv_ref)

        for ib in range(nq):
            @pl.when(i == ib)
            def _():
                n = (ib + 1) * tq
                q, k, v, do = q_ref[...], k_ref[0:n, :], v_ref[0:n, :], do_ref[...]
                p, l = _causal_probs(q, k, scale, ib * tq)
                p = p / l
                dp = lax.dot_general(do, v, NT_DIMS, preferred_element_type=F32)
                ds = p * (dp - jnp.sum(p * dp, axis=-1, keepdims=True)) * scale
                dsb, pb = ds.astype(BF16), p.astype(BF16)
                dq_ref[...] = jnp.dot(dsb, k, preferred_element_type=F32)
                dk_ref[0:n, :] += lax.dot_general(dsb, q, TN_DIMS, preferred_element_type=F32)
                dv_ref[0:n, :] += lax.dot_general(pb, do, TN_DIMS, preferred_element_type=F32)

    return pl.pallas_call(
        body, name=name, grid=(H, nq),
        in_specs=[pl.BlockSpec((None, tq, E), lambda h, i: (h, i, 0)),
                  pl.BlockSpec((None, T, E), lambda h, i: (h, 0, 0)),
                  pl.BlockSpec((None, T, D_V), lambda h, i: (h, 0, 0)),
                  pl.BlockSpec((tq, D_V), lambda h, i: (i, h))],
        out_specs=[pl.BlockSpec((None, tq, E), lambda h, i: (h, i, 0)),
                   pl.BlockSpec((None, T, E), lambda h, i: (h, 0, 0)),
                   pl.BlockSpec((None, T, D_V), lambda h, i: (h, 0, 0))],
        out_shape=[jax.ShapeDtypeStruct((H, T, E), F32)] * 2 + [jax.ShapeDtypeStruct((H, T, D_V), F32)],
        compiler_params=_cparams(("parallel", "arbitrary")),
    )(Q, K, V, dO)


def _dw_specs(T, C, tm, tc, halo):
    cur = pl.BlockSpec((tm, tc), lambda j, i: (i, j))
    last = T // tm - 1
    if halo == "prev":
        nbr = pl.BlockSpec((tm, tc), lambda j, i: (jnp.maximum(i - 1, 0), j))
    else:
        nbr = pl.BlockSpec((tm, tc), lambda j, i: (jnp.minimum(i + 1, last), j))
    return cur, nbr


def dwconv_fwd(name, u, w, b, tm=256, tc=512, rs=32):
    T, C = u.shape
    tm, tc = min(tm, T), min(tc, C)
    cur, prev = _dw_specs(T, C, tm, tc, "prev")

    def body(up_ref, uc_ref, w_ref, b_ref, o_ref, scr):
        i = pl.program_id(1)

        @pl.when(i == 0)
        def _():
            scr[pl.ds(0, tm), :] = jnp.zeros((tm, tc), F32)

        @pl.when(i > 0)
        def _():
            scr[pl.ds(0, tm), :] = up_ref[...]

        scr[pl.ds(tm, tm), :] = uc_ref[...]
        for s in range(tm // rs):
            acc = jnp.broadcast_to(b_ref[...], (rs, tc))
            for k in range(CONV_WIDTH):
                acc = acc + w_ref[pl.ds(k, 1), :] * scr[pl.ds(tm - (CONV_WIDTH - 1) + k + rs * s, rs), :]
            o_ref[pl.ds(rs * s, rs), :] = acc

    return pl.pallas_call(
        body, name=name, grid=(C // tc, T // tm),
        in_specs=[prev, cur, pl.BlockSpec((CONV_TAPS_PAD, tc), lambda j, i: (0, j)), pl.BlockSpec((1, tc), lambda j, i: (0, j))],
        out_specs=cur, out_shape=jax.ShapeDtypeStruct((T, C), F32),
        scratch_shapes=[pltpu.VMEM((2 * tm, tc), F32)], compiler_params=_cparams(("parallel", "arbitrary")),
    )(u, u, w, b)


def dwconv_bwd_u(name, dy, w, tm=256, tc=512, rs=32):
    T, C = dy.shape
    tm, tc = min(tm, T), min(tc, C)
    cur, nxt = _dw_specs(T, C, tm, tc, "next")
    last = T // tm - 1

    def body(dc_ref, dn_ref, w_ref, o_ref, scr):
        i = pl.program_id(1)
        scr[pl.ds(0, tm), :] = dc_ref[...]

        @pl.when(i == last)
        def _():
            scr[pl.ds(tm, tm), :] = jnp.zeros((tm, tc), F32)

        @pl.when(i < last)
        def _():
            scr[pl.ds(tm, tm), :] = dn_ref[...]

        for s in range(tm // rs):
            acc = jnp.zeros((rs, tc), F32)
            for k in range(CONV_WIDTH):
                acc = acc + w_ref[pl.ds(k, 1), :] * scr[pl.ds((CONV_WIDTH - 1) - k + rs * s, rs), :]
            o_ref[pl.ds(rs * s, rs), :] = acc

    return pl.pallas_call(
        body, name=name, grid=(C // tc, T // tm),
        in_specs=[cur, nxt, pl.BlockSpec((CONV_TAPS_PAD, tc), lambda j, i: (0, j))],
        out_specs=cur, out_shape=jax.ShapeDtypeStruct((T, C), F32),
        scratch_shapes=[pltpu.VMEM((2 * tm, tc), F32)], compiler_params=_cparams(("parallel", "arbitrary")),
    )(dy, dy, w)


def dwconv_bwd_w(name, u, dy, tm=256, tc=512, rs=32):
    T, C = u.shape
    tm, tc = min(tm, T), min(tc, C)
    cur, prev = _dw_specs(T, C, tm, tc, "prev")

    def body(up_ref, uc_ref, dy_ref, o_ref, scr):
        i = pl.program_id(1)

        @pl.when(i == 0)
        def _():
            scr[pl.ds(0, tm), :] = jnp.zeros((tm, tc), F32)
            o_ref[...] = jnp.zeros_like(o_ref)

        @pl.when(i > 0)
        def _():
            scr[pl.ds(0, tm), :] = up_ref[...]

        scr[pl.ds(tm, tm), :] = uc_ref[...]
        for k in range(CONV_WIDTH):
            acc = jnp.zeros((rs, tc), F32)
            for s in range(tm // rs):
                acc = acc + dy_ref[pl.ds(rs * s, rs), :] * scr[pl.ds(tm - (CONV_WIDTH - 1) + k + rs * s, rs), :]
            o_ref[pl.ds(k, 1), :] += _colsum(acc)

    return pl.pallas_call(
        body, name=name, grid=(C // tc, T // tm),
        in_specs=[prev, cur, cur],
        out_specs=pl.BlockSpec((CONV_TAPS_PAD, tc), lambda j, i: (0, j)),
        out_shape=jax.ShapeDtypeStruct((CONV_TAPS_PAD, C), F32),
        scratch_shapes=[pltpu.VMEM((2 * tm, tc), F32)], compiler_params=_cparams(("parallel", "arbitrary")),
    )(u, u, dy)


def _place():
    x, y, c = lax.axis_index("x"), lax.axis_index("y"), lax.axis_index("c")
    return x, y, c


def _other_chips(x, y):
    return [(1 - x, y, 2 * (1 - x) + y), (x, 1 - y, 2 * x + (1 - y)), (1 - x, 1 - y, 2 * (1 - x) + (1 - y))]


def _hbm_specs(n):
    return [pl.BlockSpec(memory_space=pl.ANY)] * n


HBM_SPEC = pl.BlockSpec(memory_space=pltpu.HBM)
SEM_SPEC = pl.BlockSpec(memory_space=pltpu.SEMAPHORE)
ANY_SPEC = pl.BlockSpec(memory_space=pl.ANY)
SIDE_EFFECT = pltpu.SideEffectType.DATAFLOW_SIDE_EFFECTING


def _half(ref, slot, which):
    rh = ref.shape[1] // 2
    return ref.at[slot, pl.ds(pl.multiple_of(which * rh, 16), rh), :]


def _hbm(a):
    return pltpu.with_memory_space_constraint(a, pltpu.HBM)


def gather_start(name, groups, after):
    flat = [b for g in groups for b in g]
    n, ng = len(flat), len(groups)

    def body(*refs):
        send, recv = refs[n + 1:n + 1 + ng], refs[n + 1 + ng:n + 1 + 2 * ng]
        out, token = refs[n + 1 + 2 * ng:2 * n + 1 + 2 * ng], refs[2 * n + 1 + 2 * ng]
        token[...] = jnp.zeros_like(token)
        x, y, c = _place()
        me = 2 * x + y
        a = 0
        for g, grp in enumerate(groups):
            for k in range(len(grp)):
                piece = _half(out[a], me, c)
                for j, (px, py, _) in enumerate(_other_chips(x, y)):
                    pltpu.make_async_remote_copy(
                        src_ref=piece, dst_ref=piece, send_sem=send[g].at[3 * k + j], recv_sem=recv[g].at[3 * k + j],
                        device_id=(px, py, c), device_id_type=MESH).start()
                a += 1

    sems = [pltpu.SemaphoreType.DMA((3 * len(g),)) for g in groups]
    res = pl.pallas_call(
        body, name=name, in_specs=[HBM_SPEC] * n + [ANY_SPEC],
        out_specs=[SEM_SPEC] * (2 * ng) + [HBM_SPEC] * n + [pl.BlockSpec(memory_space=pltpu.VMEM)],
        out_shape=sems + sems + [pltpu.HBM(b.shape, b.dtype) for b in flat] + [jax.ShapeDtypeStruct((8, 128), F32)],
        input_output_aliases={a: 2 * ng + a for a in range(n)},
        compiler_params=pltpu.CompilerParams(has_side_effects=SIDE_EFFECT),
    )(*[_hbm(b) for b in flat], after)
    send, recv, bufs = res[:ng], res[ng:2 * ng], list(res[2 * ng:2 * ng + n])
    out, a = [], 0
    for g, grp in enumerate(groups):
        out.append((send[g], recv[g], bufs[a:a + len(grp)]))
        a += len(grp)
    return out, res[2 * ng + n]


def gather_relay(name, started, after):
    send1, recv1, bufs = started
    n = len(bufs)

    def body(*refs):
        s1, r1 = refs[n], refs[n + 1]
        s2, r2, out, token = refs[n + 3], refs[n + 4], refs[n + 5:2 * n + 5], refs[2 * n + 5]
        x, y, c = _place()
        me = 2 * x + y
        chips = _other_chips(x, y)
        for k in range(n):
            for j, (px, py, idx) in enumerate(chips):
                cp = pltpu.make_async_remote_copy(
                    src_ref=_half(out[k], me, c), dst_ref=_half(out[k], idx, c), send_sem=s1.at[3 * k + j],
                    recv_sem=r1.at[3 * k + j], device_id=(px, py, c), device_id_type=MESH)
                cp.wait_send()
                cp.wait_recv()
        for k in range(n):
            for j, (px, py, idx) in enumerate(chips):
                piece = _half(out[k], idx, c)
                pltpu.make_async_remote_copy(
                    src_ref=piece, dst_ref=piece, send_sem=s2.at[3 * k + j], recv_sem=r2.at[3 * k + j],
                    device_id=(x, y, 1 - c), device_id_type=MESH).start()
        token[...] = jnp.zeros_like(token)

    sem = pltpu.SemaphoreType.DMA((3 * n,))
    res = pl.pallas_call(
        body, name=name, in_specs=[HBM_SPEC] * n + [SEM_SPEC, SEM_SPEC, ANY_SPEC],
        out_specs=[SEM_SPEC, SEM_SPEC] + [HBM_SPEC] * n + [pl.BlockSpec(memory_space=pltpu.VMEM)],
        out_shape=[sem, sem] + [pltpu.HBM(b.shape, b.dtype) for b in bufs] + [jax.ShapeDtypeStruct((8, 128), F32)],
        input_output_aliases={a: 2 + a for a in range(n)},
        compiler_params=pltpu.CompilerParams(has_side_effects=SIDE_EFFECT),
    )(*bufs, send1, recv1, after)
    return res[0], res[1], list(res[2:2 + n]), res[2 + n]


def gather_wait(name, relayed, after):
    send2, recv2, bufs, _ = relayed
    n = len(bufs)

    def body(*refs):
        s2, r2, out = refs[n], refs[n + 1], refs[n + 3:]
        x, y, c = _place()
        for k in range(n):
            for j, (px, py, idx) in enumerate(_other_chips(x, y)):
                cp = pltpu.make_async_remote_copy(
                    src_ref=_half(out[k], idx, c), dst_ref=_half(out[k], idx, 1 - c), send_sem=s2.at[3 * k + j],
                    recv_sem=r2.at[3 * k + j], device_id=(x, y, 1 - c), device_id_type=MESH)
                cp.wait_send()
                cp.wait_recv()

    res = pl.pallas_call(
        body, name=name, in_specs=[HBM_SPEC] * n + [SEM_SPEC, SEM_SPEC, ANY_SPEC], out_specs=[HBM_SPEC] * n,
        out_shape=[pltpu.HBM(b.shape, b.dtype) for b in bufs], input_output_aliases={a: a for a in range(n)},
        compiler_params=pltpu.CompilerParams(has_side_effects=SIDE_EFFECT),
    )(*bufs, send2, recv2, after)
    return list(res)


def split_start(name, bufs, land_shapes, n_copies, plan, after):
    nb, nl = len(bufs), len(land_shapes)

    def body(*refs):
        send, recv = refs[nb + 1], refs[nb + 2]
        out, lands, token = refs[nb + 3:2 * nb + 3], refs[2 * nb + 3:2 * nb + 3 + nl], refs[2 * nb + 3 + nl]
        x, y, c = _place()
        for k, (src, dst, to, _) in enumerate(plan(out, lands, x, y, c)):
            pltpu.make_async_remote_copy(src_ref=src, dst_ref=dst, send_sem=send.at[k], recv_sem=recv.at[k],
                                         device_id=to, device_id_type=MESH).start()
        token[...] = jnp.zeros_like(token)

    sem = pltpu.SemaphoreType.DMA((n_copies,))
    res = pl.pallas_call(
        body, name=name, in_specs=[HBM_SPEC] * nb + [ANY_SPEC],
        out_specs=[SEM_SPEC, SEM_SPEC] + [HBM_SPEC] * (nb + nl) + [pl.BlockSpec(memory_space=pltpu.VMEM)],
        out_shape=[sem, sem] + [pltpu.HBM(b.shape, b.dtype) for b in bufs]
        + [pltpu.HBM(s, d) for s, d in land_shapes] + [jax.ShapeDtypeStruct((8, 128), F32)],
        input_output_aliases={a: 2 + a for a in range(nb)},
        compiler_params=pltpu.CompilerParams(has_side_effects=SIDE_EFFECT),
    )(*[_hbm(b) for b in bufs], after)
    return res[0], res[1], list(res[2:2 + nb]), list(res[2 + nb:2 + nb + nl]), res[2 + nb + nl]


def split_wait(name, started, plan, after):
    send, recv, bufs, lands, _ = started
    nb, nl = len(bufs), len(lands)

    def body(*refs):
        s, r = refs[nb + nl], refs[nb + nl + 1]
        out, lo = refs[nb + nl + 3:2 * nb + nl + 3], refs[2 * nb + nl + 3:]
        x, y, c = _place()
        for k, (src, _, to, landed) in enumerate(plan(out, lo, x, y, c)):
            cp = pltpu.make_async_remote_copy(src_ref=src, dst_ref=landed, send_sem=s.at[k], recv_sem=r.at[k],
                                              device_id=to, device_id_type=MESH)
            cp.wait_send()
            cp.wait_recv()

    res = pl.pallas_call(
        body, name=name, in_specs=[HBM_SPEC] * (nb + nl) + [SEM_SPEC, SEM_SPEC, ANY_SPEC],
        out_specs=[HBM_SPEC] * (nb + nl), out_shape=[pltpu.HBM(b.shape, b.dtype) for b in bufs + lands],
        input_output_aliases={a: a for a in range(nb + nl)},
        compiler_params=pltpu.CompilerParams(has_side_effects=SIDE_EFFECT),
    )(*bufs, *lands, send, recv, after)
    return list(res[:nb]), list(res[nb:])


def swap_halves_plan(parts, lands, x, y, c):
    out = []
    for a in range(len(parts)):
        rh = parts[a].shape[1] // 2
        theirs = parts[a].at[:, pl.ds(pl.multiple_of((1 - c) * rh, 16), rh), :]
        out.append((theirs, lands[a], (x, y, 1 - c), lands[a]))
    return out


def chip_exchange_plan(sums, lands, x, y, c):
    me = 2 * x + y
    out = []
    for a in range(len(sums)):
        for px, py, idx in _other_chips(x, y):
            out.append((sums[a].at[idx], lands[a].at[me], (px, py, c), lands[a].at[idx]))
    return out


def share_grad_plan(halves, lands, x, y, c):
    return [(halves[a], lands[a], (x, y, 1 - c), lands[a]) for a in range(len(halves))]


def allreduce_pack(name, pack):
    R, W = pack.shape

    def body(p_ref, o_ref, sib, pair, got, send_sems, recv_sems):
        x, y, c = _place()

        def swap(k, src, dst, to):
            cp = pltpu.make_async_remote_copy(src_ref=src, dst_ref=dst, send_sem=send_sems.at[k],
                                              recv_sem=recv_sems.at[k], device_id=to, device_id_type=MESH)
            cp.start()
            return cp

        cp = swap(0, p_ref, sib, (x, y, 1 - c))
        cp.wait()
        pair[...] = p_ref[...] + sib[...]
        cps = [swap(1, pair, got.at[0], (1 - x, y, c)), swap(2, pair, got.at[1], (x, 1 - y, c)),
               swap(3, pair, got.at[2], (1 - x, 1 - y, c))]
        for cp in cps:
            cp.wait()
        o_ref[...] = (pair[...] + got[1]) + (got[0] + got[2])

    return pl.pallas_call(
        body, name=name, out_shape=jax.ShapeDtypeStruct((R, W), F32),
        in_specs=[pl.BlockSpec(memory_space=pltpu.VMEM)], out_specs=pl.BlockSpec(memory_space=pltpu.VMEM),
        scratch_shapes=[pltpu.VMEM((R, W), F32), pltpu.VMEM((R, W), F32), pltpu.VMEM((3, R, W), F32),
                        pltpu.SemaphoreType.DMA((4,)), pltpu.SemaphoreType.DMA((4,))],
        compiler_params=_cparams(),
    )(pack)


BIG_WEIGHTS = [
    ("ffn_a_w_in", "col"), ("ffn_a_w_out", "row"), ("ffn_b_w_in", "col"), ("ffn_b_w_out", "row"),
    ("mla_w_in", "row"), ("mla_w_uq", "col"), ("mla_w_ukv", "col"), ("mla_w_o", "row"),
    ("conv_w_pw1", "col"), ("conv_w_pw2", "row"), ("ple_w_proj", "col"), ("ple_w_gate", "row"),
]
WEIGHT_ORDER = ["ffn_a_norm", "ffn_a_w_in", "ffn_a_w_out", "ffn_b_norm", "ffn_b_w_in", "ffn_b_w_out", "mix_norm",
                "mla_w_in", "mla_q_lat_norm", "mla_kv_lat_norm", "mla_w_uq", "mla_w_ukv", "mla_q_gain", "mla_k_gain",
                "mla_w_o", "conv_w_pw1", "conv_b_pw1", "conv_w_dw", "conv_b_dw", "conv_ln_g", "conv_ln_b", "conv_w_pw2",
                "ple_w_proj", "ple_norm", "ple_gate_norm", "ple_w_gate"]
REPLICATED_SMALL = ["ffn_a_norm", "ffn_b_norm", "mix_norm", "ple_norm", "ple_gate_norm",
                    "mla_q_lat_norm", "mla_kv_lat_norm", "mla_q_gain", "mla_k_gain"]
SHARDED_SMALL = ["conv_b_pw1", "conv_w_dw", "conv_b_dw", "conv_ln_g", "conv_ln_b"]
PACK_ROWS = 8


def _pack_rows(arrs, width):
    out = []
    for a in arrs:
        r = -(-a.shape[0] // PACK_ROWS) * PACK_ROWS
        out.append(jnp.pad(a, ((0, r - a.shape[0]), (0, width - a.shape[1]))))
    return jnp.concatenate(out, axis=0)


def _unpack_rows(pack, shapes):
    out, r0 = [], 0
    for (r, w) in shapes:
        out.append(pack[r0:r0 + r, :w])
        r0 += -(-r // PACK_ROWS) * PACK_ROWS
    return out


def kernel(x, p, positions, ffn_a_norm, ffn_a_w_in, ffn_a_w_out, ffn_b_norm, ffn_b_w_in, ffn_b_w_out, mix_norm, mla_w_in, mla_q_lat_norm, mla_kv_lat_norm, mla_w_uq, mla_w_ukv, mla_q_gain, mla_k_gain, mla_w_o, conv_w_pw1, conv_b_pw1, conv_w_dw, conv_b_dw, conv_ln_g, conv_ln_b, conv_w_pw2, ple_w_proj, ple_norm, ple_gate_norm, ple_w_gate, loss_target, m_ffn_a_norm, m_ffn_a_w_in, m_ffn_a_w_out, m_ffn_b_norm, m_ffn_b_w_in, m_ffn_b_w_out, m_mix_norm, m_mla_w_in, m_mla_q_lat_norm, m_mla_kv_lat_norm, m_mla_w_uq, m_mla_w_ukv, m_mla_q_gain, m_mla_k_gain, m_mla_w_o, m_conv_w_pw1, m_conv_b_pw1, m_conv_w_dw, m_conv_b_dw, m_conv_ln_g, m_conv_ln_b, m_conv_w_pw2, m_ple_w_proj, m_ple_norm, m_ple_gate_norm, m_ple_w_gate, v_ffn_a_norm, v_ffn_a_w_in, v_ffn_a_w_out, v_ffn_b_norm, v_ffn_b_w_in, v_ffn_b_w_out, v_mix_norm, v_mla_w_in, v_mla_q_lat_norm, v_mla_kv_lat_norm, v_mla_w_uq, v_mla_w_ukv, v_mla_q_gain, v_mla_k_gain, v_mla_w_o, v_conv_w_pw1, v_conv_b_pw1, v_conv_w_dw, v_conv_b_dw, v_conv_ln_g, v_conv_ln_b, v_conv_w_pw2, v_ple_w_proj, v_ple_norm, v_ple_gate_norm, v_ple_w_gate):
    args = dict(locals())
    W = {n: args[n] for n in WEIGHT_ORDER}
    M1 = {n: args["m_" + n] for n in WEIGHT_ORDER}
    V2 = {n: args["v_" + n] for n in WEIGHT_ORDER}

    T, D = x.shape[1], x.shape[2]
    depth = ffn_a_norm.shape[0]
    H = mla_w_ukv.shape[2] * N_CHIPS // (D_NOPE + D_V)
    QL, KL = mla_q_lat_norm.shape[1], mla_kv_lat_norm.shape[1]
    C = conv_w_pw2.shape[1] * N_CHIPS
    lat_w = QL + KL + D_ROPE
    lat_pad = QL + KL + 128

    cx, cy, cc = lax.axis_index("x"), lax.axis_index("y"), lax.axis_index("c")
    chip = (2 * cx + cy).astype(jnp.int32)
    chip_arr = chip.reshape(1)
    core_arr = cc.astype(jnp.int32).reshape(1)

    def stage_groups(i):
        mix = ([("mla_w_in", i // 2), ("mla_w_uq", i // 2), ("mla_w_ukv", i // 2), ("mla_w_o", i // 2)] if i % 2 == 0
               else [("conv_w_pw1", i // 2), ("conv_w_pw2", i // 2)])
        return [[("ffn_a_w_in", i)], [("ffn_a_w_out", i)], mix, [("ffn_b_w_in", i)], [("ffn_b_w_out", i)],
                [("ple_w_proj", i), ("ple_w_gate", i)]]

    groups = [g for i in range(depth) for g in stage_groups(i)]
    A_IN, A_OUT, MIX, B_IN, B_OUT, PLE, PER_LAYER = 0, 1, 2, 3, 4, 5, 6

    def slot_of(key, after=None):
        n, l = key
        return cast_into_slot(f"cast_{n}_{l}", W[n].reshape(-1, W[n].shape[-1]), chip_arr, l, W[n].shape[0], after)

    def placed(a, width):
        full = jnp.zeros(a.shape[:-1] + (width,), F32)
        full = lax.dynamic_update_slice_in_dim(full, a, chip * a.shape[-1], axis=a.ndim - 1)
        return full * (cc == 0).astype(F32)

    b_pw1_sh = conv_b_pw1.reshape(1, -1)
    small_in = [placed(b_pw1_sh, 2 * C).reshape(2, C), placed(conv_w_dw[0], C), placed(conv_b_dw, C),
                placed(conv_ln_g, C), placed(conv_ln_b, C)]
    small_pack = allreduce_pack("gather_small", _pack_rows(small_in, C))
    small_full = _unpack_rows(small_pack, [(2, C), (CONV_WIDTH, C), (1, C), (1, C), (1, C)])

    FIRST = 2
    started, tok0 = gather_start("gather_start_first", [[slot_of(k) for k in g] for g in groups[:FIRST]], small_pack)
    rest, tok_rest = gather_start("gather_start_rest", [[slot_of(k, tok0) for k in g] for g in groups[FIRST:]], tok0)
    started = started + rest
    relayed, G = {}, {}

    def relay(g, after):
        if g >= len(groups):
            return None
        relayed[g] = gather_relay(f"gather_relay_{g}", started[g], after)
        return relayed[g][3]

    def ready(g, after):
        if g not in relayed:
            relay(g, after)
        for key, buf in zip(groups[g], gather_wait(f"gather_wait_{g}", relayed[g], after)):
            G[key] = buf

    relay(0, tok_rest)

    b_pw1_full = small_full[0].reshape(1, 2 * C)
    w_dw_full = jnp.pad(small_full[1], ((0, CONV_TAPS_PAD - CONV_WIDTH), (0, 0)))
    b_dw_full, ln_g_full, ln_b_full = small_full[2], small_full[3], small_full[4]

    pad_gain = lambda g: jnp.pad(g, ((0, 0), (0, HEAD_PAD - QK_DIM)))
    q_gain_p, k_gain_p = pad_gain(mla_q_gain), pad_gain(mla_k_gain)
    tabs = rope_tables("rope_tables", positions.reshape(T, 1).astype(F32))

    def ffn_fwd(tag, h, norm, w_in, w_out, layer, g_in):
        hn = rms_fwd(f"{tag}_rms", h, norm)
        ready(g_in, hn)
        tok = relay(g_in + 1, hn)
        gu = mm_nn(f"{tag}_in", hn, G[(w_in, layer)], "col", 0, out_dtype=BF16, tm=2048, after=tok)
        act = swiglu_fwd(f"{tag}_act", gu)
        ready(g_in + 1, act)
        tok = relay(g_in + 2, act)
        out = mm_nn(f"{tag}_out", act, G[(w_out, layer)], "row", 0, out_dtype=F32, scale=FFN_RESIDUAL_WEIGHT, res=h,
                    tm=1024, tn=512, whole_k=True, after=tok)
        return out, (h, hn, gu, act)

    saved = []
    h = x[0]
    for i in range(depth):
        L = {}
        g0 = PER_LAYER * i
        h, L["ffn_a"] = ffn_fwd(f"l{i}_ffa", h, ffn_a_norm[i:i + 1], "ffn_a_w_in", "ffn_a_w_out", i, g0 + A_IN)
        L["h1"] = h
        hn = rms_fwd(f"l{i}_mix_rms", h, mix_norm[i:i + 1])
        L["hn_m"] = hn
        ready(g0 + MIX, hn)
        j = i // 2
        if i % 2 == 0:
            w_in_pad = jnp.pad(G[("mla_w_in", j)].reshape(D, lat_w), ((0, 0), (0, lat_pad - lat_w)))[None]
            uq = G[("mla_w_uq", j)].transpose(1, 0, 2).reshape(QL, H, QK_DIM)
            w_uq_pad = jnp.pad(uq, ((0, 0), (0, 0), (0, HEAD_PAD - QK_DIM))).reshape(1, QL, H * HEAD_PAD)
            lat = mm_nn(f"l{i}_lat", hn, w_in_pad, "row", 0, out_dtype=F32, tm=2048)
            cq, ckv = lat_norm_fwd(f"l{i}_latnorm", lat, mla_q_lat_norm[j:j + 1], mla_kv_lat_norm[j:j + 1])
            q_raw = mm_nn(f"l{i}_uq", cq, w_uq_pad, "row", 0, out_dtype=F32, tm=2048)
            kv_raw = mm_nn(f"l{i}_ukv", ckv, G[("mla_w_ukv", j)], "col", 0, out_dtype=F32, tm=2048)
            Qh, Kh, Vh = mla_prep_fwd(f"l{i}_prep", q_raw, kv_raw, lat, tabs, q_gain_p, k_gain_p, QL + KL)
            O = attn_fwd(f"l{i}_attn", Qh, Kh, Vh)
            tok = relay(g0 + B_IN, O)
            h = mm_nn(f"l{i}_wo", O, G[("mla_w_o", j)], "row", 0, out_dtype=F32, res=h, tm=1024, tn=1024, after=tok)
            L["mla"] = (lat, cq, ckv, q_raw, kv_raw, Qh, Kh, Vh, O, w_in_pad, w_uq_pad)
        else:
            ag = mm_nn(f"l{i}_pw1", hn, G[("conv_w_pw1", j)], "col", 0, out_dtype=F32, bias=b_pw1_full, tm=2048)
            u = glu_fwd(f"l{i}_glu", ag)
            yc = dwconv_fwd(f"l{i}_dw", u, w_dw_full, b_dw_full)
            cact = ln_silu_fwd(f"l{i}_ln", yc, ln_g_full, ln_b_full)
            tok = relay(g0 + B_IN, cact)
            h = mm_nn(f"l{i}_pw2", cact, G[("conv_w_pw2", j)], "row", 0, out_dtype=F32, res=h, tm=1024, tn=1024,
                      after=tok)
            L["conv"] = (ag, u, yc, cact)
        L["h2"] = h
        h, L["ffn_b"] = ffn_fwd(f"l{i}_ffb", h, ffn_b_norm[i:i + 1], "ffn_b_w_in", "ffn_b_w_out", i, g0 + B_IN)
        L["h3"] = h
        ready(g0 + PLE, h)
        pe = mm_nn(f"l{i}_ple_proj", p[i, 0], G[("ple_w_proj", i)], "col", 0, out_dtype=F32, tm=2048)
        hg = rms_fwd(f"l{i}_gate_rms", h, ple_gate_norm[i:i + 1])
        tok = relay(g0 + PER_LAYER, hg)
        z = mm_nn(f"l{i}_ple_gate", hg, G[("ple_w_gate", i)], "row", 0, out_dtype=F32, tm=2048, after=tok)
        h = ple_fwd(f"l{i}_ple", h, pe, z, ple_norm[i:i + 1])
        L["ple"] = (pe, hg, z)
        saved.append(L)

    d_h, loss_part = loss_head("loss_head", h, loss_target[0])
    loss = lax.psum(loss_part[0, 0], ("x", "y", "c"))

    GW = {}
    SG = {}
    ids_arr = jnp.stack([cc.astype(jnp.int32), chip])
    two = lambda a: a.reshape(-1, a.shape[-1])
    merged = {}
    pipe = {}
    order = list(reversed(range(len(groups))))
    ticks = [0]

    def put_small(name, i, val):
        SG.setdefault(name, {})[i] = val

    deferred = []

    def sibling_half(g, after):
        _, theirs = split_wait(f"share_wait_{g}", pipe[g], share_grad_plan, after)
        for (n, l), gr in zip(groups[g], theirs):
            merged[n] = adamw_half(f"adamw_sib_{n}_{l}", ids_arr, gr, two(W[n]), two(M1[n]), two(V2[n]), l,
                                   W[n].shape[0], merged[n])
        return merged[groups[g][-1][0]][0]

    def reduce_tick(after, defer=True):
        k, tok = ticks[0], after
        ticks[0] += 1
        grp = lambda j: order[j] if 0 <= j < len(order) else None
        g = grp(k - 4)
        if g is not None:
            if defer:
                deferred.append(g)
            else:
                sibling_half(g, tok)
        g = grp(k - 3)
        if g is not None:
            sums, landed = split_wait(f"exchange_wait_{g}", pipe[g], chip_exchange_plan, tok)
            halves = []
            for (n, l), s, ld in zip(groups[g], sums, landed):
                *merged[n], mine = adamw_half(f"adamw_own_{n}_{l}", ids_arr, (s, ld), two(W[n]), two(M1[n]),
                                              two(V2[n]), l, W[n].shape[0], merged.get(n))
                halves.append(mine)
            pipe[g] = split_start(f"share_start_{g}", halves, [(h.shape, h.dtype) for h in halves], len(halves),
                                  share_grad_plan, tok)
            tok = pipe[g][4]
        g = grp(k - 1)
        if g is not None:
            parts, got = split_wait(f"swap_wait_{g}", pipe[g], swap_halves_plan, tok)
            sums = [add_halves(f"pair_sum_{n}_{l}", core_arr, q, r) for (n, l), q, r in zip(groups[g], parts, got)]
            pipe[g] = split_start(f"exchange_start_{g}", sums, [(s.shape, s.dtype) for s in sums], 3 * len(sums),
                                  chip_exchange_plan, tok)
            tok = pipe[g][4]
        g = grp(k)
        if g is not None:
            parts = [GW[key] for key in groups[g]]
            lands = [((q.shape[0], q.shape[1] // 2, q.shape[2]), q.dtype) for q in parts]
            pipe[g] = split_start(f"swap_start_{g}", parts, lands, len(parts), swap_halves_plan, tok)
            tok = pipe[g][4]
        return tok

    def ffn_bwd(tag, d_h, d_hb, norm, w_in, w_out, layer, fw, tok):
        h_in, hn, gu, act = fw
        GW[(w_out, layer)] = mm_tn(f"{tag}_dwout", act, d_hb, "row", scale=FFN_RESIDUAL_WEIGHT, tk=1408, tn=1024,
                                   after=tok)
        tok = reduce_tick(GW[(w_out, layer)])
        d_act = _mm_nt(f"{tag}_dact", d_hb, G[(w_out, layer)], "row", 0, act.shape[1], out_dtype=BF16,
                       scale=FFN_RESIDUAL_WEIGHT, to=1408, tc=2048, after=tok)
        dgu = swiglu_bwd(f"{tag}_dgu", gu, d_act)
        GW[(w_in, layer)] = mm_tn(f"{tag}_dwin", hn, dgu, "col")
        tok = reduce_tick(GW[(w_in, layer)])
        d_hn = _mm_nt(f"{tag}_dhn", dgu, G[(w_in, layer)], "col", 0, D, out_dtype=F32, tc=2816, after=tok)
        return (*rms_bwd_res(f"{tag}_drms", h_in, d_hn, norm, d_h), tok)

    d_hb, tok = None, None
    for i in reversed(range(depth)):
        L = saved[i]
        j = i // 2
        pe, hg, z = L["ple"]
        d_z, d_pe, g = ple_bwd(f"l{i}_dple", d_h, pe, z, ple_norm[i:i + 1])
        put_small("ple_norm", i, g)
        d_hg = _mm_nt(f"l{i}_dhg", d_z, G[("ple_w_gate", i)], "row", 0, D, out_dtype=F32, to=512, tc=2048, after=tok)
        GW[("ple_w_gate", i)] = mm_tn(f"l{i}_dwgate", hg, d_z, "row", tk=512, tn=1024)
        GW[("ple_w_proj", i)] = mm_tn(f"l{i}_dwproj", p[i, 0], d_pe, "col")
        tok = reduce_tick(GW[("ple_w_proj", i)])
        d_h, d_hb, g = rms_bwd_res(f"l{i}_dgate_rms", L["h3"], d_hg, ple_gate_norm[i:i + 1], d_h)
        put_small("ple_gate_norm", i, g)

        d_h, d_hb, g, tok = ffn_bwd(f"l{i}_ffb", d_h, d_hb, ffn_b_norm[i:i + 1], "ffn_b_w_in", "ffn_b_w_out", i,
                                    L["ffn_b"], tok)
        put_small("ffn_b_norm", i, g)

        hn = L["hn_m"]
        if i % 2 == 0:
            lat, cq, ckv, q_raw, kv_raw, Qh, Kh, Vh, O, w_in_pad, w_uq_pad = L["mla"]
            d_O = _mm_nt(f"l{i}_dO", d_hb, G[("mla_w_o", j)], "row", 0, H * D_V, out_dtype=BF16, to=512, tc=2048,
                         after=tok)
            GW[("mla_w_o", j)] = mm_tn(f"l{i}_dwo", O, d_hb, "row", tk=512, tn=1024)
            dQ, dK, dV = attn_bwd(f"l{i}_dattn", Qh, Kh, Vh, d_O)
            d_q_raw, d_kv_raw, d_kr, gq, gk = mla_prep_bwd(f"l{i}_dprep", dQ, dK, dV, q_raw, kv_raw, lat, tabs,
                                                           q_gain_p, k_gain_p, QL + KL)
            put_small("mla_q_gain", j, gq[:, :QK_DIM])
            put_small("mla_k_gain", j, gk[:, :QK_DIM])
            d_cq = _mm_nt(f"l{i}_dcq", d_q_raw, w_uq_pad, "row", 0, QL, out_dtype=F32, to=512, tc=2048)
            g_uq = mm_tn(f"l{i}_dwuq", cq, d_q_raw, "row", shards=1, out_dtype=F32, tk=512, tn=1024)
            g_uq = g_uq.reshape(QL, H, HEAD_PAD)[:, :, :QK_DIM].reshape(QL, N_CHIPS, -1).transpose(1, 0, 2)
            GW[("mla_w_uq", j)] = g_uq.astype(BF16)
            d_ckv = _mm_nt(f"l{i}_dckv", d_kv_raw, G[("mla_w_ukv", j)], "col", 0, KL, out_dtype=F32, to=512, tc=1024)
            GW[("mla_w_ukv", j)] = mm_tn(f"l{i}_dwukv", ckv, d_kv_raw, "col", tk=512, tn=1024)
            d_lat, gq, gk = lat_norm_bwd(f"l{i}_dlatnorm", lat, d_cq, d_ckv, d_kr, mla_q_lat_norm[j:j + 1],
                                         mla_kv_lat_norm[j:j + 1])
            put_small("mla_q_lat_norm", j, gq)
            put_small("mla_kv_lat_norm", j, gk)
            d_hn = _mm_nt(f"l{i}_dhn_lat", d_lat, w_in_pad, "row", 0, D, out_dtype=F32, to=1024, tc=lat_pad)
            g_in = mm_tn(f"l{i}_dwin_lat", hn, d_lat, "row", shards=1, out_dtype=F32, tk=1024, tn=lat_pad)
            GW[("mla_w_in", j)] = g_in[0, :, :lat_w].reshape(N_CHIPS, D // N_CHIPS, lat_w).astype(BF16)
            tok = reduce_tick(GW[("mla_w_in", j)])
        else:
            ag, u, yc, cact = L["conv"]
            d_cact = _mm_nt(f"l{i}_dcact", d_hb, G[("conv_w_pw2", j)], "row", 0, C, out_dtype=F32, to=512, tc=2048,
                            after=tok)
            GW[("conv_w_pw2", j)] = mm_tn(f"l{i}_dwpw2", cact, d_hb, "row", tk=512, tn=1024)
            d_yc, g1, g2, g3 = ln_silu_bwd(f"l{i}_dln", yc, d_cact, ln_g_full, ln_b_full)
            put_small("conv_ln_g", j, g1)
            put_small("conv_ln_b", j, g2)
            put_small("conv_b_dw", j, g3)
            d_u = dwconv_bwd_u(f"l{i}_ddw_u", d_yc, w_dw_full)
            put_small("conv_w_dw", j, dwconv_bwd_w(f"l{i}_ddw_w", u, d_yc))
            d_ag, g = glu_bwd(f"l{i}_dglu", ag, d_u)
            put_small("conv_b_pw1", j, g)
            d_hn = _mm_nt(f"l{i}_dhn_pw1", d_ag, G[("conv_w_pw1", j)], "col", 0, D, out_dtype=F32, tc=1024)
            GW[("conv_w_pw1", j)] = mm_tn(f"l{i}_dwpw1", hn, d_ag, "col", tn=1024)
            tok = reduce_tick(GW[("conv_w_pw1", j)])
        d_h, d_hb, g = rms_bwd_res(f"l{i}_dmix_rms", L["h1"], d_hn, mix_norm[i:i + 1], d_h)
        put_small("mix_norm", i, g)

        d_h, d_hb, g, tok = ffn_bwd(f"l{i}_ffa", d_h, d_hb, ffn_a_norm[i:i + 1], "ffn_a_w_in", "ffn_a_w_out", i,
                                    L["ffn_a"], tok)
        put_small("ffn_a_norm", i, g)
    grad_x = d_h[None]

    tok = reduce_tick(reduce_tick(d_h))
    for g in deferred:
        tok = sibling_half(g, tok)
    tok = reduce_tick(reduce_tick(tok, defer=False), defer=False)
    names = [n for n, _ in BIG_WEIGHTS]
    grads, delta, new_m, new_v = {}, {}, {}, {}
    for n in names:
        grads[n], delta[n], new_m[n], new_v[n] = [a.reshape(W[n].shape) for a in merged[n]]

    rep = []
    for n in REPLICATED_SMALL:
        rep.append(jnp.concatenate([SG[n][i] for i in sorted(SG[n])], axis=0))
    shd = [SG["conv_b_pw1"][0].reshape(2, C), SG["conv_w_dw"][0][:CONV_WIDTH], SG["conv_b_dw"][0],
           SG["conv_ln_g"][0], SG["conv_ln_b"][0]]
    red = allreduce_pack("allreduce_small", _pack_rows(rep + shd, D))
    red = _unpack_rows(red, [a.shape for a in rep + shd])
    for n, g in zip(REPLICATED_SMALL, red):
        grads[n] = g
    own = lambda a, w: lax.dynamic_slice_in_dim(a, chip * w, w, axis=a.ndim - 1)
    sh = red[len(rep):]
    grads["conv_b_pw1"] = own(sh[0].reshape(1, 2 * C), 2 * C // N_CHIPS)
    grads["conv_w_dw"] = own(sh[1], C // N_CHIPS)[None]
    grads["conv_b_dw"] = own(sh[2], C // N_CHIPS)
    grads["conv_ln_g"] = own(sh[3], C // N_CHIPS)
    grads["conv_ln_b"] = own(sh[4], C // N_CHIPS)

    small = REPLICATED_SMALL + SHARDED_SMALL
    shapes = [two(W[n]).shape for n in small]
    packs = [_pack_rows([two(src[n]) for n in small], D) for src in (W, grads, M1, V2)]
    outs = adamw("adamw_small", *packs)
    for dst, pk in zip((delta, new_m, new_v), outs):
        for n, a in zip(small, _unpack_rows(pk, shapes)):
            dst[n] = a.reshape(W[n].shape)
    for n in small:
        grads[n] = grads[n].reshape(W[n].shape)

    return (loss, grad_x, *[grads[n] for n in WEIGHT_ORDER], *[delta[n] for n in WEIGHT_ORDER],
            *[new_m[n] for n in WEIGHT_ORDER], *[new_v[n] for n in WEIGHT_ORDER])
```

```python
import functools
import math

import jax
import jax.numpy as jnp
from jax import lax
from jax.experimental import pallas as pl
from jax.experimental.pallas import tpu as pltpu

F32, BF16 = jnp.float32, jnp.bfloat16
MESH = pl.DeviceIdType.MESH

N_CHIPS = 4
EPS = 1e-6
D_NOPE, D_ROPE, D_V = 128, 64, 128
QK_DIM = D_NOPE + D_ROPE
HEAD_PAD = 256
ROPE_THETA = 10000.0
CONV_WIDTH = 31
CONV_TAPS_PAD = 32
FFN_RESIDUAL_WEIGHT = 0.5
ADAM_LR, ADAM_B1, ADAM_B2, ADAM_EPS, ADAM_WD, ADAM_STEP = 0.001, 0.9, 0.999, 1e-08, 0.01, 10
VMEM_LIMIT_BYTES = 56 * 1024 * 1024
NEG_BIG = -1e30

NN_DIMS = (((1,), (0,)), ((), ()))
NT_DIMS = (((1,), (1,)), ((), ()))
TN_DIMS = (((0,), (0,)), ((), ()))


def _cparams(semantics=None):
    kw = dict(vmem_limit_bytes=VMEM_LIMIT_BYTES)
    if semantics is not None:
        kw["dimension_semantics"] = semantics
    return pltpu.CompilerParams(**kw)


def _tile(n, pref, mult=128):
    if n <= pref:
        return n
    t = (pref // mult) * mult
    while t >= mult:
        if n % t == 0:
            return t
        t -= mult
    return n


def _rowwise(name, fn, rows, vecs, outs, accs=(), tm=256, rc=32):
    T = rows[0].shape[0]
    tm = min(tm, T)
    rc = min(rc, tm)
    nr, nv, no, na = len(rows), len(vecs), len(outs), len(accs)
    steps = tm // rc

    def body(*refs):
        row_refs = refs[:nr]
        vec_refs = refs[nr:nr + nv]
        out_refs = refs[nr + nv:nr + nv + no]
        acc_refs = refs[nr + nv + no:]
        if na:
            @pl.when(pl.program_id(0) == 0)
            def _():
                for a in acc_refs:
                    a[...] = jnp.zeros_like(a)

        def step(r, carry):
            sl = pl.ds(pl.multiple_of(r * rc, rc), rc)
            res = fn(*[x[sl, :] for x in row_refs], *[v[...] for v in vec_refs])
            for o, val in zip(out_refs, res[:no]):
                o[sl, :] = val.astype(o.dtype)
            return tuple(c + val for c, val in zip(carry, res[no:]))

        init = tuple(jnp.zeros(s, F32) for s in accs)
        tot = lax.fori_loop(0, steps, step, init)
        for a, val in zip(acc_refs, tot):
            a[...] += val

    in_specs = [pl.BlockSpec((tm, x.shape[1]), lambda i: (i, 0)) for x in rows]
    in_specs += [pl.BlockSpec(v.shape, lambda i: (0, 0)) for v in vecs]
    out_specs = [pl.BlockSpec((tm, d), lambda i: (i, 0)) for d, _ in outs]
    out_specs += [pl.BlockSpec(s, lambda i: (0, 0)) for s in accs]
    out_shape = [jax.ShapeDtypeStruct((T, d), dt) for d, dt in outs]
    out_shape += [jax.ShapeDtypeStruct(s, F32) for s in accs]
    return pl.pallas_call(
        body, name=name, grid=(T // tm,), in_specs=in_specs, out_specs=out_specs, out_shape=out_shape,
        compiler_params=_cparams(("arbitrary",)),
    )(*rows, *vecs)


def _colsum(v):
    return jnp.sum(v, axis=0, keepdims=True)


def _rstd(x):
    return lax.rsqrt(jnp.mean(x * x, axis=-1, keepdims=True) + EPS)


def _rms_bwd(x, dy, g):
    r = _rstd(x)
    xh = x * r
    dyg = dy * g
    dx = r * (dyg - xh * jnp.mean(dyg * xh, axis=-1, keepdims=True))
    return dx, dy * xh


def _sigmoid(x):
    return 1.0 / (1.0 + jnp.exp(-x))


def rms_fwd(name, h, g):
    def fn(x, gv):
        return ((x * _rstd(x)) * gv,)
    return _rowwise(name, fn, [h], [g], [(h.shape[1], BF16)])[0]


def rms_bwd_res(name, h, d_y, g, d_res):
    D = h.shape[1]

    def fn(x, dy, dr, gv):
        dx, dgr = _rms_bwd(x, dy, gv)
        dh = dr + dx
        return dh, dh, _colsum(dgr)
    return _rowwise(name, fn, [h, d_y, d_res], [g], [(D, F32), (D, BF16)], [(1, D)], tm=128)


def loss_head(name, y, target):
    D = y.shape[1]

    def fn(yv, tv):
        e = yv - tv
        tot = jnp.sum(_colsum(e * e), axis=1, keepdims=True) * (0.5 / D)
        return e * (1.0 / D), jnp.broadcast_to(tot, (1, 128))
    return _rowwise(name, fn, [y, target], [], [(D, F32)], [(1, 128)])


def ple_fwd(name, h, pe, z, g_e):
    def fn(hv, pv, zv, gv):
        return (hv + (pv * _rstd(pv)) * gv * _sigmoid(zv),)
    return _rowwise(name, fn, [h, pe, z], [g_e], [(h.shape[1], F32)], tm=128)[0]


def ple_bwd(name, d_h, pe, z, g_e):
    D = d_h.shape[1]

    def fn(dh, pv, zv, gv):
        gate = _sigmoid(zv)
        e = (pv * _rstd(pv)) * gv
        d_z = dh * e * gate * (1.0 - gate)
        d_pe, dgr = _rms_bwd(pv, dh * gate, gv)
        return d_z, d_pe, _colsum(dgr)
    return _rowwise(name, fn, [d_h, pe, z], [g_e], [(D, BF16), (D, BF16)], [(1, D)], tm=128)


def lat_norm_fwd(name, lat, g_q, g_kv):
    QL, KL = g_q.shape[1], g_kv.shape[1]

    def fn(v, gq, gk):
        a = v[:, :QL]
        b = v[:, QL:QL + KL]
        return (a * _rstd(a)) * gq, (b * _rstd(b)) * gk
    return _rowwise(name, fn, [lat], [g_q, g_kv], [(QL, BF16), (KL, BF16)])


def lat_norm_bwd(name, lat, d_cq, d_ckv, d_krope, g_q, g_kv):
    QL, KL = g_q.shape[1], g_kv.shape[1]

    def fn(v, dq, dk, dr, gq, gk):
        da, ga = _rms_bwd(v[:, :QL], dq, gq)
        db, gb = _rms_bwd(v[:, QL:QL + KL], dk, gk)
        return jnp.concatenate([da, db, dr], axis=-1), _colsum(ga), _colsum(gb)
    return _rowwise(name, fn, [lat, d_cq, d_ckv, d_krope], [g_q, g_kv],
                    [(lat.shape[1], BF16)], [(1, QL), (1, KL)])


def glu_fwd(name, ag):
    C = ag.shape[1] // 2

    def fn(v):
        return (v[:, :C] * _sigmoid(v[:, C:]),)
    return _rowwise(name, fn, [ag], [], [(C, F32)], tm=128)[0]


def glu_bwd(name, ag, d_u):
    C = ag.shape[1] // 2

    def fn(v, du):
        a = v[:, :C]
        s = _sigmoid(v[:, C:])
        d = jnp.concatenate([du * s, du * a * s * (1.0 - s)], axis=-1)
        return d, _colsum(d)
    return _rowwise(name, fn, [ag, d_u], [], [(2 * C, BF16)], [(1, 2 * C)], tm=128)


def ln_silu_fwd(name, yc, g, b):
    def fn(v, gv, bv):
        xc = v - jnp.mean(v, axis=-1, keepdims=True)
        ln = xc * lax.rsqrt(jnp.mean(xc * xc, axis=-1, keepdims=True) + EPS) * gv + bv
        return (ln * _sigmoid(ln),)
    return _rowwise(name, fn, [yc], [g, b], [(yc.shape[1], BF16)], tm=128)[0]


def ln_silu_bwd(name, yc, d_out, g, b):
    C = yc.shape[1]

    def fn(v, do, gv, bv):
        xc = v - jnp.mean(v, axis=-1, keepdims=True)
        r = lax.rsqrt(jnp.mean(xc * xc, axis=-1, keepdims=True) + EPS)
        xh = xc * r
        ln = xh * gv + bv
        s = _sigmoid(ln)
        d_ln = do * s * (1.0 + ln * (1.0 - s))
        dxh = d_ln * gv
        dy = r * (dxh - jnp.mean(dxh, axis=-1, keepdims=True) - xh * jnp.mean(dxh * xh, axis=-1, keepdims=True))
        return dy, _colsum(d_ln * xh), _colsum(d_ln), _colsum(dy)
    return _rowwise(name, fn, [yc, d_out], [g, b], [(C, F32)], [(1, C), (1, C), (1, C)], tm=128)


def add_halves(name, core, own, got):
    S, Rh, C = got.shape
    tr = _tile(Rh, 512, 16)
    nrb = Rh // tr

    def body(c_ref, a_ref, b_ref, o_ref):
        o_ref[...] = (a_ref[...].astype(F32) + b_ref[...].astype(F32)).astype(o_ref.dtype)

    gs = pltpu.PrefetchScalarGridSpec(
        num_scalar_prefetch=1, grid=(S, nrb),
        in_specs=[pl.BlockSpec((None, tr, C), lambda s, r, c: (s, c[0] * nrb + r, 0)),
                  pl.BlockSpec((None, tr, C), lambda s, r, c: (s, r, 0))],
        out_specs=pl.BlockSpec((None, tr, C), lambda s, r, c: (s, r, 0)))
    return pl.pallas_call(
        body, name=name, grid_spec=gs, out_shape=jax.ShapeDtypeStruct((S, Rh, C), BF16),
        compiler_params=_cparams(("arbitrary", "arbitrary")))(core, own, got)


def adamw_half(name, ids, grad, w, m, v, layer, layers, into):
    own = isinstance(grad, tuple)
    Rh, C = grad[0].shape[1:] if own else grad.shape
    tr = _tile(Rh, 128, 16)
    nrb = Rh // tr
    n_grad = 4 if own else 1
    c1 = 1.0 / (1.0 - ADAM_B1 ** ADAM_STEP)
    c2 = 1.0 / (1.0 - ADAM_B2 ** ADAM_STEP)

    def body(ids_ref, *refs):
        g_refs, (w_ref, m_ref, v_ref) = refs[:n_grad], refs[n_grad:n_grad + 3]
        outs = refs[-5:] if own else refs[-4:]
        go_ref, d_ref, nm_ref, nv_ref = outs[:4]
        if own:
            gv = ((g_refs[0][...].astype(F32) + g_refs[1][...].astype(F32))
                  + (g_refs[2][...].astype(F32) + g_refs[3][...].astype(F32)))
            outs[4][...] = gv
        else:
            gv = g_refs[0][...]
        nm = ADAM_B1 * m_ref[...] + (1.0 - ADAM_B1) * gv
        nv = ADAM_B2 * v_ref[...] + (1.0 - ADAM_B2) * (gv * gv)
        go_ref[...] = gv
        d_ref[...] = -ADAM_LR * ((nm * c1) / (jnp.sqrt(nv * c2) + ADAM_EPS) + ADAM_WD * w_ref[...])
        nm_ref[...] = nm
        nv_ref[...] = nv

    half = (lambda ids: ids[0]) if own else (lambda ids: 1 - ids[0])
    rows = pl.BlockSpec((tr, C), lambda r, ids: ((2 * layer + half(ids)) * nrb + r, 0))
    plain = pl.BlockSpec((tr, C), lambda r, ids: (r, 0))
    if own:
        slot = lambda flip: pl.BlockSpec((None, tr, C), lambda r, ids: (ids[1] ^ flip, r, 0))
        in_specs = [slot(0), slot(1), slot(2), slot(3), rows, rows, rows]
        operands = [ids, grad[0], grad[1], grad[1], grad[1], w, m, v]
    else:
        in_specs = [plain, rows, rows, rows]
        operands = [ids, grad, w, m, v]
    aliases = {}
    if into is not None:
        in_specs += [pl.BlockSpec(memory_space=pl.ANY)] * 4
        aliases = {len(operands) + k: k for k in range(4)}
        operands += list(into)
    full = jax.ShapeDtypeStruct((layers * 2 * Rh, C), F32)
    gs = pltpu.PrefetchScalarGridSpec(num_scalar_prefetch=1, grid=(nrb,), in_specs=in_specs,
                                      out_specs=[rows] * 4 + [plain] * own)
    return pl.pallas_call(
        body, name=name, grid_spec=gs, out_shape=[full] * 4 + [jax.ShapeDtypeStruct((Rh, C), F32)] * own,
        input_output_aliases=aliases, compiler_params=_cparams(("arbitrary",)))(*operands)


def cast_into_slot(name, w, chip, layer, layers, after=None):
    R, C = w.shape[0] // layers, w.shape[1]
    tr = _tile(R, 512, 16)
    nrb = R // tr

    def body(s_ref, w_ref, *rest):
        rest[-1][...] = w_ref[...].astype(BF16)

    in_specs = [pl.BlockSpec((tr, C), lambda r, s: (layer * nrb + r, 0))]
    operands = [chip, w]
    if after is not None:
        in_specs.append(pl.BlockSpec(memory_space=pl.ANY))
        operands.append(after)
    gs = pltpu.PrefetchScalarGridSpec(
        num_scalar_prefetch=1, grid=(nrb,), in_specs=in_specs,
        out_specs=pl.BlockSpec((None, tr, C), lambda r, s: (s[0], r, 0)))
    return pl.pallas_call(
        body, name=name, grid_spec=gs, out_shape=jax.ShapeDtypeStruct((N_CHIPS, R, C), BF16),
        compiler_params=_cparams(("arbitrary",)))(*operands)


def adamw(name, w, g, m, v):
    R, C = w.shape
    tr = _tile(R, 256, 8)
    c1 = 1.0 / (1.0 - ADAM_B1 ** ADAM_STEP)
    c2 = 1.0 / (1.0 - ADAM_B2 ** ADAM_STEP)

    def body(w_ref, g_ref, m_ref, v_ref, d_ref, nm_ref, nv_ref):
        gv = g_ref[...]
        nm = ADAM_B1 * m_ref[...] + (1.0 - ADAM_B1) * gv
        nv = ADAM_B2 * v_ref[...] + (1.0 - ADAM_B2) * (gv * gv)
        d_ref[...] = -ADAM_LR * ((nm * c1) / (jnp.sqrt(nv * c2) + ADAM_EPS) + ADAM_WD * w_ref[...])
        nm_ref[...] = nm
        nv_ref[...] = nv

    spec = pl.BlockSpec((tr, C), lambda r: (r, 0))
    return pl.pallas_call(
        body, name=name, grid=(R // tr,), in_specs=[spec] * 4, out_specs=[spec] * 3,
        out_shape=[jax.ShapeDtypeStruct((R, C), F32)] * 3, compiler_params=_cparams(("arbitrary",)))(w, g, m, v)


def _matmul(name, a, b, *, grid, a_blk, a_map, b_blk, b_map, o_shape, o_dtype, o_blk, o_map, dims,
            scale=None, res=None, bias=None, bias_blk=None, bias_map=None, alias_into=None, after=None):
    nk = grid[2]
    has_res, has_bias, has_into = res is not None, bias is not None, alias_into is not None
    acc_shape = tuple(d for d in o_blk if d is not None)

    def body(*refs):
        a_ref, b_ref = refs[0], refs[1]
        pos = 2
        res_ref = bias_ref = None
        if has_res:
            res_ref = refs[pos]
            pos += 1
        if has_bias:
            bias_ref = refs[pos]
            pos += 1
        if has_into:
            pos += 1
        if after is not None:
            pos += 1
        o_ref = refs[pos]
        av, bv = a_ref[...], b_ref[...]
        if bv.ndim == 3:
            bv = bv.reshape(-1, bv.shape[-1])
        if av.dtype != BF16:
            av = av.astype(BF16)
        if bv.dtype != BF16:
            bv = bv.astype(BF16)
        part = lax.dot_general(av, bv, dims, preferred_element_type=F32)

        def finish(acc):
            if scale is not None:
                acc = acc * scale
            if has_bias:
                acc = acc + bias_ref[...]
            if has_res:
                acc = acc + res_ref[...]
            o_ref[...] = acc.astype(o_ref.dtype)

        if nk == 1:
            finish(part)
        else:
            acc_ref = refs[pos + 1]
            k = pl.program_id(2)

            @pl.when(k == 0)
            def _():
                acc_ref[...] = part

            @pl.when(k > 0)
            def _():
                acc_ref[...] += part

            @pl.when(k == nk - 1)
            def _():
                finish(acc_ref[...])

    operands = [a, b]
    in_specs = [pl.BlockSpec(a_blk, a_map), pl.BlockSpec(b_blk, b_map)]
    if has_res:
        operands.append(res)
        in_specs.append(pl.BlockSpec(o_blk, o_map))
    if has_bias:
        operands.append(bias)
        in_specs.append(pl.BlockSpec(bias_blk, bias_map))
    aliases = {}
    if has_into:
        aliases = {len(operands): 0}
        operands.append(alias_into)
        in_specs.append(pl.BlockSpec(memory_space=pl.ANY))
    if after is not None:
        operands.append(after)
        in_specs.append(pl.BlockSpec(memory_space=pl.ANY))
    return pl.pallas_call(
        body, name=name, grid=grid, in_specs=in_specs, out_specs=pl.BlockSpec(o_blk, o_map),
        out_shape=jax.ShapeDtypeStruct(o_shape, o_dtype),
        scratch_shapes=[pltpu.VMEM(acc_shape, F32)] if nk > 1 else [],
        input_output_aliases=aliases,
        compiler_params=_cparams(("parallel", "parallel", "arbitrary")),
    )(*operands)


def mm_nn(name, a, w3, kind, layer, *, out_dtype, scale=None, res=None, bias=None, tm=1024, tn=512, tk=2048,
          after=None, whole_k=False):
    M, K = a.shape
    S, _, C = w3.shape
    tm = _tile(M, tm, 16)
    b_blk = None
    if kind == "col":
        N = S * C
        tk, tn = _tile(K, tk), _tile(C, tn)
        kb, nb = K // tk, C // tn
        b_map = lambda n, m, k: (n // nb, layer * kb + k, n % nb)
    elif whole_k:
        N, K4, tk, tn = C, K // S, K, _tile(C, tn)
        b_blk, b_map = (S, K4, tn), lambda n, m, k: (0, layer, n)
    else:
        N, K4 = C, K // S
        tk, tn = _tile(K4, tk), _tile(C, tn)
        kb4 = K4 // tk
        b_map = lambda n, m, k: (k // kb4, layer * kb4 + k % kb4, n)
    return _matmul(
        name, a, w3, grid=(N // tn, M // tm, K // tk),
        a_blk=(tm, tk), a_map=lambda n, m, k: (m, k), b_blk=b_blk or (None, tk, tn), b_map=b_map,
        o_shape=(M, N), o_dtype=out_dtype, o_blk=(tm, tn), o_map=lambda n, m, k: (m, n), dims=NN_DIMS,
        scale=scale, res=res, bias=bias, bias_blk=(1, tn), bias_map=lambda n, m, k: (0, n), after=after)


def _mm_nt(name, g, w3, kind, layer, K, *, out_dtype, scale=None, tm=1024, to=1024, tc=1408, after=None):
    S, _, C = w3.shape
    tc = _tile(C, tc)
    if g.ndim == 3:
        M, N = g.shape[1], 2 * g.shape[2]
        tm = _tile(M, tm, 16)
        cb = g.shape[2] // tc
        a_blk, a_map = (None, tm, tc), lambda o, m, c: (c // cb, m, c % cb)
    else:
        M, N = g.shape
        tm = _tile(M, tm, 16)
        a_blk, a_map = (tm, tc), lambda o, m, c: (m, c)
    if kind == "col":
        nb = C // tc
        to = _tile(K, to)
        ob = K // to
        b_map = lambda o, m, c: (c // nb, layer * ob + o, c % nb)
    else:
        K4 = K // S
        to = _tile(K4, to)
        ob4 = K4 // to
        b_map = lambda o, m, c: (o // ob4, layer * ob4 + o % ob4, c)
    return _matmul(
        name, g, w3, grid=(K // to, M // tm, N // tc),
        a_blk=a_blk, a_map=a_map, b_blk=(None, to, tc), b_map=b_map,
        o_shape=(M, K), o_dtype=out_dtype, o_blk=(tm, to), o_map=lambda o, m, c: (m, o), dims=NT_DIMS,
        scale=scale, after=after)


def mm_tn(name, a, g, kind, *, shards=N_CHIPS, layer=0, layers=1, into=None, out_dtype=BF16, scale=None,
          tk=1024, tn=1408, tm=2048, after=None):
    M, K = a.shape
    N = 2 * g.shape[2] if g.ndim == 3 else g.shape[1]
    tm = _tile(M, tm, 16)
    if kind == "col":
        C = N // shards
        tk, tn = _tile(K, tk), _tile(C, tn)
        kb, nb = K // tk, C // tn
        o_shape = (shards, layers * K, C)
        o_map = lambda k, n, m: (n // nb, layer * kb + k, n % nb)
    else:
        K4 = K // shards
        tk, tn = _tile(K4, tk), _tile(N, tn)
        kb4 = K4 // tk
        o_shape = (shards, layers * K4, N)
        o_map = lambda k, n, m: (k // kb4, layer * kb4 + k % kb4, n)
    if g.ndim == 3:
        nbh = g.shape[2] // tn
        g_blk, g_map = (None, tm, tn), lambda k, n, m: (n // nbh, m, n % nbh)
    else:
        g_blk, g_map = (tm, tn), lambda k, n, m: (m, n)
    return _matmul(
        name, a, g, grid=(K // tk, N // tn, M // tm),
        a_blk=(tm, tk), a_map=lambda k, n, m: (m, k), b_blk=g_blk, b_map=g_map,
        o_shape=o_shape, o_dtype=out_dtype, o_blk=(None, tk, tn), o_map=o_map, dims=TN_DIMS,
        scale=scale, alias_into=into, after=after)


def ffn_in_act(name, hn, w3, *, tn=256, after=None):
    M, K = hn.shape
    S, _, C = w3.shape
    tn = _tile(C, tn)
    nb = C // tn

    def body(a_ref, bg_ref, bu_ref, *rest):
        gu_ref, act_ref = rest[-2:]
        a = a_ref[...]
        g = jnp.dot(a, bg_ref[...], preferred_element_type=F32)
        u = jnp.dot(a, bu_ref[...], preferred_element_type=F32)
        gu_ref[0] = g.astype(BF16)
        gu_ref[1] = u.astype(BF16)
        act_ref[...] = (g * _sigmoid(g) * u).astype(BF16)

    in_specs = [pl.BlockSpec((M, K), lambda j: (0, 0)),
                pl.BlockSpec((None, K, tn), lambda j: (j // nb, 0, j % nb)),
                pl.BlockSpec((None, K, tn), lambda j: (S // 2 + j // nb, 0, j % nb))]
    operands = [hn, w3, w3]
    if after is not None:
        in_specs.append(pl.BlockSpec(memory_space=pl.ANY))
        operands.append(after)
    width = S // 2 * C
    return pl.pallas_call(
        body, name=name, grid=(width // tn,), in_specs=in_specs,
        out_specs=[pl.BlockSpec((2, M, tn), lambda j: (0, 0, j)), pl.BlockSpec((M, tn), lambda j: (0, j))],
        out_shape=[jax.ShapeDtypeStruct((2, M, width), BF16), jax.ShapeDtypeStruct((M, width), BF16)],
        compiler_params=_cparams(("arbitrary",)))(*operands)


def ffn_dact_dgu(name, d_out, w3, gu, scale, *, tm=512, rc=32, after=None):
    M, N = d_out.shape
    S, K4, _ = w3.shape
    tm = _tile(M, tm, 16)
    to = K4
    rc = min(rc, tm)

    def body(a_ref, b_ref, gu_ref, *rest):
        o_ref, da_ref = rest[-2:]
        da_ref[...] = lax.dot_general(a_ref[...], b_ref[...], NT_DIMS, preferred_element_type=F32) * scale

        def step(r, carry):
            sl = pl.ds(pl.multiple_of(r * rc, rc), rc)
            da = da_ref[sl, :]
            g = gu_ref[0, sl, :].astype(F32)
            u = gu_ref[1, sl, :].astype(F32)
            s = _sigmoid(g)
            o_ref[0, sl, :] = (da * u * s * (1.0 + g * (1.0 - s))).astype(BF16)
            o_ref[1, sl, :] = (da * g * s).astype(BF16)
            return carry

        lax.fori_loop(0, tm // rc, step, 0)

    halves = pl.BlockSpec((2, tm, to), lambda o, m: (0, m, o))
    in_specs = [pl.BlockSpec((tm, N), lambda o, m: (m, 0)), pl.BlockSpec((None, to, N), lambda o, m: (o, 0, 0)), halves]
    operands = [d_out, w3, gu]
    if after is not None:
        in_specs.append(pl.BlockSpec(memory_space=pl.ANY))
        operands.append(after)
    return pl.pallas_call(
        body, name=name, grid=(S, M // tm), in_specs=in_specs, out_specs=halves,
        out_shape=jax.ShapeDtypeStruct((2, M, S * K4), BF16), scratch_shapes=[pltpu.VMEM((tm, to), F32)],
        compiler_params=_cparams(("parallel", "arbitrary")))(*operands)


def rope_tables(name, pos):
    T = pos.shape[0]
    half = D_ROPE // 2

    def body(p_ref, c_ref, s1_ref, s2_ref):
        lane = lax.broadcasted_iota(jnp.int32, (T, 128), 1)
        idx = (lane & (half - 1)).astype(F32)
        ang = p_ref[...] * jnp.exp(idx * (-2.0 * math.log(ROPE_THETA) / D_ROPE))
        cs, sn = jnp.cos(ang), jnp.sin(ang)
        c_ref[...] = jnp.where(lane < D_ROPE, cs, 0.0)
        s1_ref[...] = jnp.where(lane < half, -sn, 0.0)
        s2_ref[...] = jnp.where((lane >= half) & (lane < D_ROPE), sn, 0.0)

    return pl.pallas_call(body, name=name, out_shape=[jax.ShapeDtypeStruct((T, 128), F32)] * 3,
                          compiler_params=_cparams())(pos)


def _rope(v, cs, s1, s2):
    return v * cs + pltpu.roll(v, 128 - D_ROPE // 2, 1) * s1 + pltpu.roll(v, D_ROPE // 2, 1) * s2


def _rope_bwd(d, cs, s1, s2):
    return d * cs + pltpu.roll(d * s1, D_ROPE // 2, 1) + pltpu.roll(d * s2, 128 - D_ROPE // 2, 1)


def _head_rstd(n, r):
    ms = (jnp.sum(n * n, axis=-1, keepdims=True) + jnp.sum(r * r, axis=-1, keepdims=True)) * (1.0 / QK_DIM)
    return lax.rsqrt(ms + EPS)


def mla_prep_fwd(name, q_raw, kv_raw, lat, tabs, q_gain, k_gain, rope_col, tm=128):
    T = q_raw.shape[0]
    H = q_raw.shape[1] // HEAD_PAD
    tm = min(tm, T)
    rope_blk = rope_col // 128

    def body(q_ref, kv_ref, kr_ref, c_ref, s1_ref, s2_ref, qg_ref, kg_ref, Q_ref, K_ref, V_ref):
        cs, s1, s2 = c_ref[...], s1_ref[...], s2_ref[...]
        qg, kg = qg_ref[...], kg_ref[...]
        kr = kr_ref[...]
        for h in range(H):
            lo = HEAD_PAD * h
            n, r = q_ref[:, lo:lo + 128], q_ref[:, lo + 128:lo + 256]
            rs = _head_rstd(n, r)
            Q_ref[h, :, 0:128] = (n * rs * qg[:, :128]).astype(BF16)
            Q_ref[h, :, 128:256] = _rope(r * rs * qg[:, 128:], cs, s1, s2).astype(BF16)
            n = kv_ref[:, lo:lo + 128]
            rs = _head_rstd(n, kr)
            K_ref[h, :, 0:128] = (n * rs * kg[:, :128]).astype(BF16)
            K_ref[h, :, 128:256] = _rope(kr * rs * kg[:, 128:], cs, s1, s2).astype(BF16)
            V_ref[h] = kv_ref[:, lo + 128:lo + 256].astype(BF16)

    row = lambda w: pl.BlockSpec((tm, w), lambda i: (i, 0))
    vec = pl.BlockSpec((1, HEAD_PAD), lambda i: (0, 0))
    return pl.pallas_call(
        body, name=name, grid=(T // tm,),
        in_specs=[row(H * HEAD_PAD), row(H * HEAD_PAD), pl.BlockSpec((tm, 128), lambda i: (i, rope_blk)),
                  row(128), row(128), row(128), vec, vec],
        out_specs=[pl.BlockSpec((H, tm, HEAD_PAD), lambda i: (0, i, 0))] * 2 + [pl.BlockSpec((H, tm, D_V), lambda i: (0, i, 0))],
        out_shape=[jax.ShapeDtypeStruct((H, T, HEAD_PAD), BF16)] * 2 + [jax.ShapeDtypeStruct((H, T, D_V), BF16)],
        compiler_params=_cparams(("arbitrary",)),
    )(q_raw, kv_raw, lat, *tabs, q_gain, k_gain)


def mla_prep_bwd(name, dQ, dK, dV, q_raw, kv_raw, lat, tabs, q_gain, k_gain, rope_col, tm=128):
    T = q_raw.shape[0]
    H = q_raw.shape[1] // HEAD_PAD
    tm = min(tm, T)
    rope_blk = rope_col // 128

    def body(dQ_ref, dK_ref, dV_ref, q_ref, kv_ref, kr_ref, c_ref, s1_ref, s2_ref, qg_ref, kg_ref,
             dq_ref, dkv_ref, dkr_ref, dqg_ref, dkg_ref):
        @pl.when(pl.program_id(0) == 0)
        def _():
            dqg_ref[...] = jnp.zeros_like(dqg_ref)
            dkg_ref[...] = jnp.zeros_like(dkg_ref)

        cs, s1, s2 = c_ref[...], s1_ref[...], s2_ref[...]
        qg, kg = qg_ref[...], kg_ref[...]
        kr = kr_ref[...]
        dkr = jnp.zeros((tm, 128), F32)
        gq_n = jnp.zeros((1, 128), F32)
        gq_r = jnp.zeros((1, 128), F32)
        gk_n = jnp.zeros((1, 128), F32)
        gk_r = jnp.zeros((1, 128), F32)

        def norm_bwd(n, r, dn, dr, gain):
            rs = _head_rstd(n, r)
            nh, rh = n * rs, r * rs
            dng, drg = dn * gain[:, :128], dr * gain[:, 128:]
            mean = (jnp.sum(dng * nh, axis=-1, keepdims=True) + jnp.sum(drg * rh, axis=-1, keepdims=True)) * (1.0 / QK_DIM)
            return rs * (dng - nh * mean), rs * (drg - rh * mean), _colsum(dn * nh), _colsum(dr * rh)

        for h in range(H):
            lo = HEAD_PAD * h
            n, r = q_ref[:, lo:lo + 128], q_ref[:, lo + 128:lo + 256]
            dn = dQ_ref[h, :, 0:128].astype(F32)
            dr = _rope_bwd(dQ_ref[h, :, 128:256].astype(F32), cs, s1, s2)
            a, b, g1, g2 = norm_bwd(n, r, dn, dr, qg)
            dq_ref[:, lo:lo + 128] = a.astype(BF16)
            dq_ref[:, lo + 128:lo + 256] = b.astype(BF16)
            gq_n, gq_r = gq_n + g1, gq_r + g2
            n = kv_ref[:, lo:lo + 128]
            dn = dK_ref[h, :, 0:128].astype(F32)
            dr = _rope_bwd(dK_ref[h, :, 128:256].astype(F32), cs, s1, s2)
            a, b, g1, g2 = norm_bwd(n, kr, dn, dr, kg)
            dkv_ref[:, lo:lo + 128] = a.astype(BF16)
            dkv_ref[:, lo + 128:lo + 256] = dV_ref[h].astype(BF16)
            dkr = dkr + b
            gk_n, gk_r = gk_n + g1, gk_r + g2
        dkr_ref[...] = dkr
        dqg_ref[:, 0:128] += gq_n
        dqg_ref[:, 128:256] += gq_r
        dkg_ref[:, 0:128] += gk_n
        dkg_ref[:, 128:256] += gk_r

    row = lambda w: pl.BlockSpec((tm, w), lambda i: (i, 0))
    vec = pl.BlockSpec((1, HEAD_PAD), lambda i: (0, 0))
    hd = lambda w: pl.BlockSpec((H, tm, w), lambda i: (0, i, 0))
    return pl.pallas_call(
        body, name=name, grid=(T // tm,),
        in_specs=[hd(HEAD_PAD), hd(HEAD_PAD), hd(D_V), row(H * HEAD_PAD), row(H * HEAD_PAD),
                  pl.BlockSpec((tm, 128), lambda i: (i, rope_blk)), row(128), row(128), row(128), vec, vec],
        out_specs=[row(H * HEAD_PAD), row(H * HEAD_PAD), row(128), vec, vec],
        out_shape=[jax.ShapeDtypeStruct((T, H * HEAD_PAD), BF16)] * 2 + [jax.ShapeDtypeStruct((T, 128), F32)]
        + [jax.ShapeDtypeStruct((1, HEAD_PAD), F32)] * 2,
        compiler_params=_cparams(("arbitrary",)),
    )(dQ, dK, dV, q_raw, kv_raw, lat, *tabs, q_gain, k_gain)


def _causal_probs(q, k, scale, row0):
    s = lax.dot_general(q, k, NT_DIMS, preferred_element_type=F32) * scale
    row = row0 + lax.broadcasted_iota(jnp.int32, s.shape, 0)
    col = lax.broadcasted_iota(jnp.int32, s.shape, 1)
    s = jnp.where(col <= row, s, NEG_BIG)
    p = jnp.exp(s - jnp.max(s, axis=-1, keepdims=True))
    return p, jnp.sum(p, axis=-1, keepdims=True)


def attn_fwd(name, Q, K, V, tq=512):
    H, T, E = Q.shape
    tq = min(tq, T)
    nq = T // tq
    scale = QK_DIM ** -0.5

    def body(q_ref, k_ref, v_ref, o_ref):
        i = pl.program_id(1)
        for ib in range(nq):
            @pl.when(i == ib)
            def _():
                n = (ib + 1) * tq
                p, l = _causal_probs(q_ref[...], k_ref[0:n, :], scale, ib * tq)
                o = jnp.dot(p.astype(BF16), v_ref[0:n, :], preferred_element_type=F32)
                o_ref[...] = (o / l).astype(o_ref.dtype)

    return pl.pallas_call(
        body, name=name, grid=(H, nq),
        in_specs=[pl.BlockSpec((None, tq, E), lambda h, i: (h, i, 0)),
                  pl.BlockSpec((None, T, E), lambda h, i: (h, 0, 0)),
                  pl.BlockSpec((None, T, D_V), lambda h, i: (h, 0, 0))],
        out_specs=pl.BlockSpec((tq, D_V), lambda h, i: (i, h)),
        out_shape=jax.ShapeDtypeStruct((T, H * D_V), BF16),
        compiler_params=_cparams(("parallel", "arbitrary")),
    )(Q, K, V)


def attn_bwd(name, Q, K, V, dO, tq=512):
    H, T, E = Q.shape
    tq = min(tq, T)
    nq = T // tq
    scale = QK_DIM ** -0.5

    def body(q_ref, k_ref, v_ref, do_ref, dq_ref, dk_ref, dv_ref):
        i = pl.program_id(1)

        @pl.when(i == 0)
        def _():
            dk_ref[...] = jnp.zeros_like(dk_ref)
            dv_ref[...] = jnp.zeros_like(dv_ref)

        for ib in range(nq):
            @pl.when(i == ib)
            def _():
                n = (ib + 1) * tq
                q, k, v, do = q_ref[...], k_ref[0:n, :], v_ref[0:n, :], do_ref[...]
                p, l = _causal_probs(q, k, scale, ib * tq)
                p = p / l
                dp = lax.dot_general(do, v, NT_DIMS, preferred_element_type=F32)
                ds = p * (dp - jnp.sum(p * dp, axis=-1, keepdims=True)) * scale
                dsb, pb = ds.astype(BF16), p.astype(BF16)
                dq_ref[...] = jnp.dot(dsb, k, preferred_element_type=F32)
                dk_ref[0:n, :] += lax.dot_general(dsb, q, TN_DIMS, preferred_element_type=F32)
                dv_ref[0:n, :] += lax.dot_general(pb, do, TN_DIMS, preferred_element_type=F32)

    return pl.pallas_call(
        body, name=name, grid=(H, nq),
        in_specs=[pl.BlockSpec((None, tq, E), lambda h, i: (h, i, 0)),
                  pl.BlockSpec((None, T, E), lambda h, i: (h, 0, 0)),
                  pl.BlockSpec((None, T, D_V), lambda h, i: (h, 0, 0)),
                  pl.BlockSpec((tq, D_V), lambda h, i: (i, h))],
        out_specs=[pl.BlockSpec((None, tq, E), lambda h, i: (h, i, 0)),
                   pl.BlockSpec((None, T, E), lambda h, i: (h, 0, 0)),
                   pl.BlockSpec((None, T, D_V), lambda h, i: (h, 0, 0))],
        out_shape=[jax.ShapeDtypeStruct((H, T, E), F32)] * 2 + [jax.ShapeDtypeStruct((H, T, D_V), F32)],
        compiler_params=_cparams(("parallel", "arbitrary")),
    )(Q, K, V, dO)


def _dw_specs(T, C, tm, tc, halo):
    cur = pl.BlockSpec((tm, tc), lambda j, i: (i, j))
    last = T // tm - 1
    if halo == "prev":
        nbr = pl.BlockSpec((tm, tc), lambda j, i: (jnp.maximum(i - 1, 0), j))
    else:
        nbr = pl.BlockSpec((tm, tc), lambda j, i: (jnp.minimum(i + 1, last), j))
    return cur, nbr


def dwconv_fwd(name, u, w, b, tm=256, tc=512, rs=32):
    T, C = u.shape
    tm, tc = min(tm, T), min(tc, C)
    cur, prev = _dw_specs(T, C, tm, tc, "prev")

    def body(up_ref, uc_ref, w_ref, b_ref, o_ref, scr):
        i = pl.program_id(1)

        @pl.when(i == 0)
        def _():
            scr[pl.ds(0, tm), :] = jnp.zeros((tm, tc), F32)

        @pl.when(i > 0)
        def _():
            scr[pl.ds(0, tm), :] = up_ref[...]

        scr[pl.ds(tm, tm), :] = uc_ref[...]
        for s in range(tm // rs):
            acc = jnp.broadcast_to(b_ref[...], (rs, tc))
            for k in range(CONV_WIDTH):
                acc = acc + w_ref[pl.ds(k, 1), :] * scr[pl.ds(tm - (CONV_WIDTH - 1) + k + rs * s, rs), :]
            o_ref[pl.ds(rs * s, rs), :] = acc

    return pl.pallas_call(
        body, name=name, grid=(C // tc, T // tm),
        in_specs=[prev, cur, pl.BlockSpec((CONV_TAPS_PAD, tc), lambda j, i: (0, j)), pl.BlockSpec((1, tc), lambda j, i: (0, j))],
        out_specs=cur, out_shape=jax.ShapeDtypeStruct((T, C), F32),
        scratch_shapes=[pltpu.VMEM((2 * tm, tc), F32)], compiler_params=_cparams(("parallel", "arbitrary")),
    )(u, u, w, b)


def dwconv_bwd_u(name, dy, w, tm=256, tc=512, rs=32):
    T, C = dy.shape
    tm, tc = min(tm, T), min(tc, C)
    cur, nxt = _dw_specs(T, C, tm, tc, "next")
    last = T // tm - 1

    def body(dc_ref, dn_ref, w_ref, o_ref, scr):
        i = pl.program_id(1)
        scr[pl.ds(0, tm), :] = dc_ref[...]

        @pl.when(i == last)
        def _():
            scr[pl.ds(tm, tm), :] = jnp.zeros((tm, tc), F32)

        @pl.when(i < last)
        def _():
            scr[pl.ds(tm, tm), :] = dn_ref[...]

        for s in range(tm // rs):
            acc = jnp.zeros((rs, tc), F32)
            for k in range(CONV_WIDTH):
                acc = acc + w_ref[pl.ds(k, 1), :] * scr[pl.ds((CONV_WIDTH - 1) - k + rs * s, rs), :]
            o_ref[pl.ds(rs * s, rs), :] = acc

    return pl.pallas_call(
        body, name=name, grid=(C // tc, T // tm),
        in_specs=[cur, nxt, pl.BlockSpec((CONV_TAPS_PAD, tc), lambda j, i: (0, j))],
        out_specs=cur, out_shape=jax.ShapeDtypeStruct((T, C), F32),
        scratch_shapes=[pltpu.VMEM((2 * tm, tc), F32)], compiler_params=_cparams(("parallel", "arbitrary")),
    )(dy, dy, w)


def dwconv_bwd_w(name, u, dy, tm=256, tc=512, rs=32):
    T, C = u.shape
    tm, tc = min(tm, T), min(tc, C)
    cur, prev = _dw_specs(T, C, tm, tc, "prev")

    def body(up_ref, uc_ref, dy_ref, o_ref, scr):
        i = pl.program_id(1)

        @pl.when(i == 0)
        def _():
            scr[pl.ds(0, tm), :] = jnp.zeros((tm, tc), F32)
            o_ref[...] = jnp.zeros_like(o_ref)

        @pl.when(i > 0)
        def _():
            scr[pl.ds(0, tm), :] = up_ref[...]

        scr[pl.ds(tm, tm), :] = uc_ref[...]
        for k in range(CONV_WIDTH):
            acc = jnp.zeros((rs, tc), F32)
            for s in range(tm // rs):
                acc = acc + dy_ref[pl.ds(rs * s, rs), :] * scr[pl.ds(tm - (CONV_WIDTH - 1) + k + rs * s, rs), :]
            o_ref[pl.ds(k, 1), :] += _colsum(acc)

    return pl.pallas_call(
        body, name=name, grid=(C // tc, T // tm),
        in_specs=[prev, cur, cur],
        out_specs=pl.BlockSpec((CONV_TAPS_PAD, tc), lambda j, i: (0, j)),
        out_shape=jax.ShapeDtypeStruct((CONV_TAPS_PAD, C), F32),
        scratch_shapes=[pltpu.VMEM((2 * tm, tc), F32)], compiler_params=_cparams(("parallel", "arbitrary")),
    )(u, u, dy)


def _place():
    x, y, c = lax.axis_index("x"), lax.axis_index("y"), lax.axis_index("c")
    return x, y, c


def _other_chips(x, y):
    return [(1 - x, y, 2 * (1 - x) + y), (x, 1 - y, 2 * x + (1 - y)), (1 - x, 1 - y, 2 * (1 - x) + (1 - y))]


def _hbm_specs(n):
    return [pl.BlockSpec(memory_space=pl.ANY)] * n


HBM_SPEC = pl.BlockSpec(memory_space=pltpu.HBM)
SEM_SPEC = pl.BlockSpec(memory_space=pltpu.SEMAPHORE)
ANY_SPEC = pl.BlockSpec(memory_space=pl.ANY)
SIDE_EFFECT = pltpu.SideEffectType.DATAFLOW_SIDE_EFFECTING


def _half(ref, slot, which):
    rh = ref.shape[1] // 2
    return ref.at[slot, pl.ds(pl.multiple_of(which * rh, 16), rh), :]


def _hbm(a):
    return pltpu.with_memory_space_constraint(a, pltpu.HBM)


def gather_start(name, groups, after):
    flat = [b for g in groups for b in g]
    n, ng = len(flat), len(groups)

    def body(*refs):
        send, recv = refs[n + 1:n + 1 + ng], refs[n + 1 + ng:n + 1 + 2 * ng]
        out, token = refs[n + 1 + 2 * ng:2 * n + 1 + 2 * ng], refs[2 * n + 1 + 2 * ng]
        token[...] = jnp.zeros_like(token)
        x, y, c = _place()
        me = 2 * x + y
        a = 0
        for g, grp in enumerate(groups):
            for k in range(len(grp)):
                piece = _half(out[a], me, c)
                for j, (px, py, _) in enumerate(_other_chips(x, y)):
                    pltpu.make_async_remote_copy(
                        src_ref=piece, dst_ref=piece, send_sem=send[g].at[3 * k + j], recv_sem=recv[g].at[3 * k + j],
                        device_id=(px, py, c), device_id_type=MESH).start()
                a += 1

    sems = [pltpu.SemaphoreType.DMA((3 * len(g),)) for g in groups]
    res = pl.pallas_call(
        body, name=name, in_specs=[HBM_SPEC] * n + [ANY_SPEC],
        out_specs=[SEM_SPEC] * (2 * ng) + [HBM_SPEC] * n + [pl.BlockSpec(memory_space=pltpu.VMEM)],
        out_shape=sems + sems + [pltpu.HBM(b.shape, b.dtype) for b in flat] + [jax.ShapeDtypeStruct((8, 128), F32)],
        input_output_aliases={a: 2 * ng + a for a in range(n)},
        compiler_params=pltpu.CompilerParams(has_side_effects=SIDE_EFFECT),
    )(*[_hbm(b) for b in flat], after)
    send, recv, bufs = res[:ng], res[ng:2 * ng], list(res[2 * ng:2 * ng + n])
    out, a = [], 0
    for g, grp in enumerate(groups):
        out.append((send[g], recv[g], bufs[a:a + len(grp)]))
        a += len(grp)
    return out, res[2 * ng + n]


def gather_relay(name, started, after):
    send1, recv1, bufs = started
    n = len(bufs)

    def body(*refs):
        s1, r1 = refs[n], refs[n + 1]
        s2, r2, out, token = refs[n + 3], refs[n + 4], refs[n + 5:2 * n + 5], refs[2 * n + 5]
        x, y, c = _place()
        me = 2 * x + y
        chips = _other_chips(x, y)
        for k in range(n):
            for j, (px, py, idx) in enumerate(chips):
                cp = pltpu.make_async_remote_copy(
                    src_ref=_half(out[k], me, c), dst_ref=_half(out[k], idx, c), send_sem=s1.at[3 * k + j],
                    recv_sem=r1.at[3 * k + j], device_id=(px, py, c), device_id_type=MESH)
                cp.wait_send()
                cp.wait_recv()
        for k in range(n):
            for j, (px, py, idx) in enumerate(chips):
                piece = _half(out[k], idx, c)
                pltpu.make_async_remote_copy(
                    src_ref=piece, dst_ref=piece, send_sem=s2.at[3 * k + j], recv_sem=r2.at[3 * k + j],
                    device_id=(x, y, 1 - c), device_id_type=MESH).start()
        token[...] = jnp.zeros_like(token)

    sem = pltpu.SemaphoreType.DMA((3 * n,))
    res = pl.pallas_call(
        body, name=name, in_specs=[HBM_SPEC] * n + [SEM_SPEC, SEM_SPEC, ANY_SPEC],
        out_specs=[SEM_SPEC, SEM_SPEC] + [HBM_SPEC] * n + [pl.BlockSpec(memory_space=pltpu.VMEM)],
        out_shape=[sem, sem] + [pltpu.HBM(b.shape, b.dtype) for b in bufs] + [jax.ShapeDtypeStruct((8, 128), F32)],
        input_output_aliases={a: 2 + a for a in range(n)},
        compiler_params=pltpu.CompilerParams(has_side_effects=SIDE_EFFECT),
    )(*bufs, send1, recv1, after)
    return res[0], res[1], list(res[2:2 + n]), res[2 + n]


def gather_wait(name, relayed, after):
    send2, recv2, bufs, _ = relayed
    n = len(bufs)

    def body(*refs):
        s2, r2, out = refs[n], refs[n + 1], refs[n + 3:]
        x, y, c = _place()
        for k in range(n):
            for j, (px, py, idx) in enumerate(_other_chips(x, y)):
                cp = pltpu.make_async_remote_copy(
                    src_ref=_half(out[k], idx, c), dst_ref=_half(out[k], idx, 1 - c), send_sem=s2.at[3 * k + j],
                    recv_sem=r2.at[3 * k + j], device_id=(x, y, 1 - c), device_id_type=MESH)
                cp.wait_send()
                cp.wait_recv()

    res = pl.pallas_call(
        body, name=name, in_specs=[HBM_SPEC] * n + [SEM_SPEC, SEM_SPEC, ANY_SPEC], out_specs=[HBM_SPEC] * n,
        out_shape=[pltpu.HBM(b.shape, b.dtype) for b in bufs], input_output_aliases={a: a for a in range(n)},
        compiler_params=pltpu.CompilerParams(has_side_effects=SIDE_EFFECT),
    )(*bufs, send2, recv2, after)
    return list(res)


def split_start(name, bufs, land_shapes, n_copies, plan, after):
    nb, nl = len(bufs), len(land_shapes)

    def body(*refs):
        send, recv = refs[nb + 1], refs[nb + 2]
        out, lands, token = refs[nb + 3:2 * nb + 3], refs[2 * nb + 3:2 * nb + 3 + nl], refs[2 * nb + 3 + nl]
        x, y, c = _place()
        for k, (src, dst, to, _) in enumerate(plan(out, lands, x, y, c)):
            pltpu.make_async_remote_copy(src_ref=src, dst_ref=dst, send_sem=send.at[k], recv_sem=recv.at[k],
                                         device_id=to, device_id_type=MESH).start()
        token[...] = jnp.zeros_like(token)

    sem = pltpu.SemaphoreType.DMA((n_copies,))
    res = pl.pallas_call(
        body, name=name, in_specs=[HBM_SPEC] * nb + [ANY_SPEC],
        out_specs=[SEM_SPEC, SEM_SPEC] + [HBM_SPEC] * (nb + nl) + [pl.BlockSpec(memory_space=pltpu.VMEM)],
        out_shape=[sem, sem] + [pltpu.HBM(b.shape, b.dtype) for b in bufs]
        + [pltpu.HBM(s, d) for s, d in land_shapes] + [jax.ShapeDtypeStruct((8, 128), F32)],
        input_output_aliases={a: 2 + a for a in range(nb)},
        compiler_params=pltpu.CompilerParams(has_side_effects=SIDE_EFFECT),
    )(*[_hbm(b) for b in bufs], after)
    return res[0], res[1], list(res[2:2 + nb]), list(res[2 + nb:2 + nb + nl]), res[2 + nb + nl]


def split_wait(name, started, plan, after):
    send, recv, bufs, lands, _ = started
    nb, nl = len(bufs), len(lands)

    def body(*refs):
        s, r = refs[nb + nl], refs[nb + nl + 1]
        out, lo = refs[nb + nl + 3:2 * nb + nl + 3], refs[2 * nb + nl + 3:]
        x, y, c = _place()
        for k, (src, _, to, landed) in enumerate(plan(out, lo, x, y, c)):
            cp = pltpu.make_async_remote_copy(src_ref=src, dst_ref=landed, send_sem=s.at[k], recv_sem=r.at[k],
                                              device_id=to, device_id_type=MESH)
            cp.wait_send()
            cp.wait_recv()

    res = pl.pallas_call(
        body, name=name, in_specs=[HBM_SPEC] * (nb + nl) + [SEM_SPEC, SEM_SPEC, ANY_SPEC],
        out_specs=[HBM_SPEC] * (nb + nl), out_shape=[pltpu.HBM(b.shape, b.dtype) for b in bufs + lands],
        input_output_aliases={a: a for a in range(nb + nl)},
        compiler_params=pltpu.CompilerParams(has_side_effects=SIDE_EFFECT),
    )(*bufs, *lands, send, recv, after)
    return list(res[:nb]), list(res[nb:])


def swap_halves_plan(parts, lands, x, y, c):
    out = []
    for a in range(len(parts)):
        rh = parts[a].shape[1] // 2
        theirs = parts[a].at[:, pl.ds(pl.multiple_of((1 - c) * rh, 16), rh), :]
        out.append((theirs, lands[a], (x, y, 1 - c), lands[a]))
    return out


def chip_exchange_plan(sums, lands, x, y, c):
    me = 2 * x + y
    out = []
    for a in range(len(sums)):
        for px, py, idx in _other_chips(x, y):
            out.append((sums[a].at[idx], lands[a].at[me], (px, py, c), lands[a].at[idx]))
    return out


def share_grad_plan(halves, lands, x, y, c):
    return [(halves[a], lands[a], (x, y, 1 - c), lands[a]) for a in range(len(halves))]


def allreduce_pack(name, pack):
    R, W = pack.shape

    def body(p_ref, o_ref, sib, pair, got, send_sems, recv_sems):
        x, y, c = _place()

        def swap(k, src, dst, to):
            cp = pltpu.make_async_remote_copy(src_ref=src, dst_ref=dst, send_sem=send_sems.at[k],
                                              recv_sem=recv_sems.at[k], device_id=to, device_id_type=MESH)
            cp.start()
            return cp

        cp = swap(0, p_ref, sib, (x, y, 1 - c))
        cp.wait()
        pair[...] = p_ref[...] + sib[...]
        cps = [swap(1, pair, got.at[0], (1 - x, y, c)), swap(2, pair, got.at[1], (x, 1 - y, c)),
               swap(3, pair, got.at[2], (1 - x, 1 - y, c))]
        for cp in cps:
            cp.wait()
        o_ref[...] = (pair[...] + got[1]) + (got[0] + got[2])

    return pl.pallas_call(
        body, name=name, out_shape=jax.ShapeDtypeStruct((R, W), F32),
        in_specs=[pl.BlockSpec(memory_space=pltpu.VMEM)], out_specs=pl.BlockSpec(memory_space=pltpu.VMEM),
        scratch_shapes=[pltpu.VMEM((R, W), F32), pltpu.VMEM((R, W), F32), pltpu.VMEM((3, R, W), F32),
                        pltpu.SemaphoreType.DMA((4,)), pltpu.SemaphoreType.DMA((4,))],
        compiler_params=_cparams(),
    )(pack)


BIG_WEIGHTS = [
    ("ffn_a_w_in", "col"), ("ffn_a_w_out", "row"), ("ffn_b_w_in", "col"), ("ffn_b_w_out", "row"),
    ("mla_w_in", "row"), ("mla_w_uq", "col"), ("mla_w_ukv", "col"), ("mla_w_o", "row"),
    ("conv_w_pw1", "col"), ("conv_w_pw2", "row"), ("ple_w_proj", "col"), ("ple_w_gate", "row"),
]
WEIGHT_ORDER = ["ffn_a_norm", "ffn_a_w_in", "ffn_a_w_out", "ffn_b_norm", "ffn_b_w_in", "ffn_b_w_out", "mix_norm",
                "mla_w_in", "mla_q_lat_norm", "mla_kv_lat_norm", "mla_w_uq", "mla_w_ukv", "mla_q_gain", "mla_k_gain",
                "mla_w_o", "conv_w_pw1", "conv_b_pw1", "conv_w_dw", "conv_b_dw", "conv_ln_g", "conv_ln_b", "conv_w_pw2",
                "ple_w_proj", "ple_norm", "ple_gate_norm", "ple_w_gate"]
REPLICATED_SMALL = ["ffn_a_norm", "ffn_b_norm", "mix_norm", "ple_norm", "ple_gate_norm",
                    "mla_q_lat_norm", "mla_kv_lat_norm", "mla_q_gain", "mla_k_gain"]
SHARDED_SMALL = ["conv_b_pw1", "conv_w_dw", "conv_b_dw", "conv_ln_g", "conv_ln_b"]
PACK_ROWS = 8


def _pack_rows(arrs, width):
    out = []
    for a in arrs:
        r = -(-a.shape[0] // PACK_ROWS) * PACK_ROWS
        out.append(jnp.pad(a, ((0, r - a.shape[0]), (0, width - a.shape[1]))))
    return jnp.concatenate(out, axis=0)


def _unpack_rows(pack, shapes):
    out, r0 = [], 0
    for (r, w) in shapes:
        out.append(pack[r0:r0 + r, :w])
        r0 += -(-r // PACK_ROWS) * PACK_ROWS
    return out


def kernel(x, p, positions, ffn_a_norm, ffn_a_w_in, ffn_a_w_out, ffn_b_norm, ffn_b_w_in, ffn_b_w_out, mix_norm, mla_w_in, mla_q_lat_norm, mla_kv_lat_norm, mla_w_uq, mla_w_ukv, mla_q_gain, mla_k_gain, mla_w_o, conv_w_pw1, conv_b_pw1, conv_w_dw, conv_b_dw, conv_ln_g, conv_ln_b, conv_w_pw2, ple_w_proj, ple_norm, ple_gate_norm, ple_w_gate, loss_target, m_ffn_a_norm, m_ffn_a_w_in, m_ffn_a_w_out, m_ffn_b_norm, m_ffn_b_w_in, m_ffn_b_w_out, m_mix_norm, m_mla_w_in, m_mla_q_lat_norm, m_mla_kv_lat_norm, m_mla_w_uq, m_mla_w_ukv, m_mla_q_gain, m_mla_k_gain, m_mla_w_o, m_conv_w_pw1, m_conv_b_pw1, m_conv_w_dw, m_conv_b_dw, m_conv_ln_g, m_conv_ln_b, m_conv_w_pw2, m_ple_w_proj, m_ple_norm, m_ple_gate_norm, m_ple_w_gate, v_ffn_a_norm, v_ffn_a_w_in, v_ffn_a_w_out, v_ffn_b_norm, v_ffn_b_w_in, v_ffn_b_w_out, v_mix_norm, v_mla_w_in, v_mla_q_lat_norm, v_mla_kv_lat_norm, v_mla_w_uq, v_mla_w_ukv, v_mla_q_gain, v_mla_k_gain, v_mla_w_o, v_conv_w_pw1, v_conv_b_pw1, v_conv_w_dw, v_conv_b_dw, v_conv_ln_g, v_conv_ln_b, v_conv_w_pw2, v_ple_w_proj, v_ple_norm, v_ple_gate_norm, v_ple_w_gate):
    args = dict(locals())
    W = {n: args[n] for n in WEIGHT_ORDER}
    M1 = {n: args["m_" + n] for n in WEIGHT_ORDER}
    V2 = {n: args["v_" + n] for n in WEIGHT_ORDER}

    T, D = x.shape[1], x.shape[2]
    depth = ffn_a_norm.shape[0]
    H = mla_w_ukv.shape[2] * N_CHIPS // (D_NOPE + D_V)
    QL, KL = mla_q_lat_norm.shape[1], mla_kv_lat_norm.shape[1]
    C = conv_w_pw2.shape[1] * N_CHIPS
    lat_w = QL + KL + D_ROPE
    lat_pad = QL + KL + 128

    cx, cy, cc = lax.axis_index("x"), lax.axis_index("y"), lax.axis_index("c")
    chip = (2 * cx + cy).astype(jnp.int32)
    chip_arr = chip.reshape(1)
    core_arr = cc.astype(jnp.int32).reshape(1)

    def stage_groups(i):
        mix = ([("mla_w_in", i // 2), ("mla_w_uq", i // 2), ("mla_w_ukv", i // 2), ("mla_w_o", i // 2)] if i % 2 == 0
               else [("conv_w_pw1", i // 2), ("conv_w_pw2", i // 2)])
        return [[("ffn_a_w_in", i)], [("ffn_a_w_out", i)], mix, [("ffn_b_w_in", i)], [("ffn_b_w_out", i)],
                [("ple_w_proj", i), ("ple_w_gate", i)]]

    groups = [g for i in range(depth) for g in stage_groups(i)]
    A_IN, A_OUT, MIX, B_IN, B_OUT, PLE, PER_LAYER = 0, 1, 2, 3, 4, 5, 6

    def slot_of(key, after=None):
        n, l = key
        return cast_into_slot(f"cast_{n}_{l}", W[n].reshape(-1, W[n].shape[-1]), chip_arr, l, W[n].shape[0], after)

    def placed(a, width):
        full = jnp.zeros(a.shape[:-1] + (width,), F32)
        full = lax.dynamic_update_slice_in_dim(full, a, chip * a.shape[-1], axis=a.ndim - 1)
        return full * (cc == 0).astype(F32)

    b_pw1_sh = conv_b_pw1.reshape(1, -1)
    small_in = [placed(b_pw1_sh, 2 * C).reshape(2, C), placed(conv_w_dw[0], C), placed(conv_b_dw, C),
                placed(conv_ln_g, C), placed(conv_ln_b, C)]
    small_pack = allreduce_pack("gather_small", _pack_rows(small_in, C))
    small_full = _unpack_rows(small_pack, [(2, C), (CONV_WIDTH, C), (1, C), (1, C), (1, C)])

    FIRST = 2
    started, tok0 = gather_start("gather_start_first", [[slot_of(k) for k in g] for g in groups[:FIRST]], small_pack)
    rest, tok_rest = gather_start("gather_start_rest", [[slot_of(k, tok0) for k in g] for g in groups[FIRST:]], tok0)
    started = started + rest
    relayed, G = {}, {}

    def relay(g, after):
        if g >= len(groups):
            return None
        relayed[g] = gather_relay(f"gather_relay_{g}", started[g], after)
        return relayed[g][3]

    def ready(g, after):
        if g not in relayed:
            relay(g, after)
        for key, buf in zip(groups[g], gather_wait(f"gather_wait_{g}", relayed[g], after)):
            G[key] = buf

    relay(0, tok_rest)

    b_pw1_full = small_full[0].reshape(1, 2 * C)
    w_dw_full = jnp.pad(small_full[1], ((0, CONV_TAPS_PAD - CONV_WIDTH), (0, 0)))
    b_dw_full, ln_g_full, ln_b_full = small_full[2], small_full[3], small_full[4]

    pad_gain = lambda g: jnp.pad(g, ((0, 0), (0, HEAD_PAD - QK_DIM)))
    q_gain_p, k_gain_p = pad_gain(mla_q_gain), pad_gain(mla_k_gain)
    tabs = rope_tables("rope_tables", positions.reshape(T, 1).astype(F32))

    def ffn_fwd(tag, h, norm, w_in, w_out, layer, g_in):
        hn = rms_fwd(f"{tag}_rms", h, norm)
        ready(g_in, hn)
        tok = relay(g_in + 1, hn) if g_in > 0 else None
        gu, act = ffn_in_act(f"{tag}_in", hn, G[(w_in, layer)], after=tok)
        ready(g_in + 1, act)
        tok = relay(g_in + 2, act)
        out = mm_nn(f"{tag}_out", act, G[(w_out, layer)], "row", 0, out_dtype=F32, scale=FFN_RESIDUAL_WEIGHT, res=h,
                    tm=1024, tn=512, whole_k=True, after=tok)
        return out, (h, hn, gu, act)

    saved = []
    h = x[0]
    for i in range(depth):
        L = {}
        g0 = PER_LAYER * i
        h, L["ffn_a"] = ffn_fwd(f"l{i}_ffa", h, ffn_a_norm[i:i + 1], "ffn_a_w_in", "ffn_a_w_out", i, g0 + A_IN)
        L["h1"] = h
        hn = rms_fwd(f"l{i}_mix_rms", h, mix_norm[i:i + 1])
        L["hn_m"] = hn
        ready(g0 + MIX, hn)
        j = i // 2
        if i % 2 == 0:
            w_in_pad = jnp.pad(G[("mla_w_in", j)].reshape(D, lat_w), ((0, 0), (0, lat_pad - lat_w)))[None]
            uq = G[("mla_w_uq", j)].transpose(1, 0, 2).reshape(QL, H, QK_DIM)
            w_uq_pad = jnp.pad(uq, ((0, 0), (0, 0), (0, HEAD_PAD - QK_DIM))).reshape(1, QL, H * HEAD_PAD)
            lat = mm_nn(f"l{i}_lat", hn, w_in_pad, "row", 0, out_dtype=F32, tm=2048)
            cq, ckv = lat_norm_fwd(f"l{i}_latnorm", lat, mla_q_lat_norm[j:j + 1], mla_kv_lat_norm[j:j + 1])
            q_raw = mm_nn(f"l{i}_uq", cq, w_uq_pad, "row", 0, out_dtype=F32, tm=2048)
            kv_raw = mm_nn(f"l{i}_ukv", ckv, G[("mla_w_ukv", j)], "col", 0, out_dtype=F32, tm=2048)
            Qh, Kh, Vh = mla_prep_fwd(f"l{i}_prep", q_raw, kv_raw, lat, tabs, q_gain_p, k_gain_p, QL + KL)
            O = attn_fwd(f"l{i}_attn", Qh, Kh, Vh)
            tok = relay(g0 + B_IN, O)
            h = mm_nn(f"l{i}_wo", O, G[("mla_w_o", j)], "row", 0, out_dtype=F32, res=h, tm=1024, tn=1024, after=tok)
            L["mla"] = (lat, cq, ckv, q_raw, kv_raw, Qh, Kh, Vh, O, w_in_pad, w_uq_pad)
        else:
            ag = mm_nn(f"l{i}_pw1", hn, G[("conv_w_pw1", j)], "col", 0, out_dtype=F32, bias=b_pw1_full, tm=2048)
            u = glu_fwd(f"l{i}_glu", ag)
            yc = dwconv_fwd(f"l{i}_dw", u, w_dw_full, b_dw_full)
            cact = ln_silu_fwd(f"l{i}_ln", yc, ln_g_full, ln_b_full)
            tok = relay(g0 + B_IN, cact)
            h = mm_nn(f"l{i}_pw2", cact, G[("conv_w_pw2", j)], "row", 0, out_dtype=F32, res=h, tm=1024, tn=1024,
                      after=tok)
            L["conv"] = (ag, u, yc, cact)
        L["h2"] = h
        h, L["ffn_b"] = ffn_fwd(f"l{i}_ffb", h, ffn_b_norm[i:i + 1], "ffn_b_w_in", "ffn_b_w_out", i, g0 + B_IN)
        L["h3"] = h
        ready(g0 + PLE, h)
        pe = mm_nn(f"l{i}_ple_proj", p[i, 0], G[("ple_w_proj", i)], "col", 0, out_dtype=F32, tm=2048)
        hg = rms_fwd(f"l{i}_gate_rms", h, ple_gate_norm[i:i + 1])
        tok = relay(g0 + PER_LAYER, hg)
        z = mm_nn(f"l{i}_ple_gate", hg, G[("ple_w_gate", i)], "row", 0, out_dtype=F32, tm=2048, after=tok)
        h = ple_fwd(f"l{i}_ple", h, pe, z, ple_norm[i:i + 1])
        L["ple"] = (pe, hg, z)
        saved.append(L)

    d_h, loss_part = loss_head("loss_head", h, loss_target[0])
    loss = lax.psum(loss_part[0, 0], ("x", "y", "c"))

    GW = {}
    SG = {}
    ids_arr = jnp.stack([cc.astype(jnp.int32), chip])
    two = lambda a: a.reshape(-1, a.shape[-1])
    merged = {}
    pipe = {}
    order = list(reversed(range(len(groups))))
    ticks = [0]

    def put_small(name, i, val):
        SG.setdefault(name, {})[i] = val

    deferred = []

    def sibling_half(g, after):
        _, theirs = split_wait(f"share_wait_{g}", pipe[g], share_grad_plan, after)
        for (n, l), gr in zip(groups[g], theirs):
            merged[n] = adamw_half(f"adamw_sib_{n}_{l}", ids_arr, gr, two(W[n]), two(M1[n]), two(V2[n]), l,
                                   W[n].shape[0], merged[n])
        return merged[groups[g][-1][0]][0]

    def reduce_tick(after, defer=True):
        k, tok = ticks[0], after
        ticks[0] += 1
        grp = lambda j: order[j] if 0 <= j < len(order) else None
        g = grp(k - 4)
        if g is not None:
            if defer:
                deferred.append(g)
            else:
                sibling_half(g, tok)
        g = grp(k - 3)
        if g is not None:
            sums, landed = split_wait(f"exchange_wait_{g}", pipe[g], chip_exchange_plan, tok)
            halves = []
            for (n, l), s, ld in zip(groups[g], sums, landed):
                *merged[n], mine = adamw_half(f"adamw_own_{n}_{l}", ids_arr, (s, ld), two(W[n]), two(M1[n]),
                                              two(V2[n]), l, W[n].shape[0], merged.get(n))
                halves.append(mine)
            pipe[g] = split_start(f"share_start_{g}", halves, [(h.shape, h.dtype) for h in halves], len(halves),
                                  share_grad_plan, tok)
            tok = pipe[g][4]
        g = grp(k - 1)
        if g is not None:
            parts, got = split_wait(f"swap_wait_{g}", pipe[g], swap_halves_plan, tok)
            sums = [add_halves(f"pair_sum_{n}_{l}", core_arr, q, r) for (n, l), q, r in zip(groups[g], parts, got)]
            pipe[g] = split_start(f"exchange_start_{g}", sums, [(s.shape, s.dtype) for s in sums], 3 * len(sums),
                                  chip_exchange_plan, tok)
            tok = pipe[g][4]
        g = grp(k)
        if g is not None:
            parts = [GW[key] for key in groups[g]]
            lands = [((q.shape[0], q.shape[1] // 2, q.shape[2]), q.dtype) for q in parts]
            pipe[g] = split_start(f"swap_start_{g}", parts, lands, len(parts), swap_halves_plan, tok)
            tok = pipe[g][4]
        return tok

    def ffn_bwd(tag, d_h, d_hb, norm, w_in, w_out, layer, fw, tok):
        h_in, hn, gu, act = fw
        GW[(w_out, layer)] = mm_tn(f"{tag}_dwout", act, d_hb, "row", scale=FFN_RESIDUAL_WEIGHT, tk=1408, tn=1024,
                                   after=tok)
        tok = reduce_tick(GW[(w_out, layer)])
        dgu = ffn_dact_dgu(f"{tag}_dact", d_hb, G[(w_out, layer)], gu, FFN_RESIDUAL_WEIGHT, after=tok)
        GW[(w_in, layer)] = mm_tn(f"{tag}_dwin", hn, dgu, "col")
        tok = reduce_tick(GW[(w_in, layer)])
        d_hn = _mm_nt(f"{tag}_dhn", dgu, G[(w_in, layer)], "col", 0, D, out_dtype=F32, tc=2816, after=tok)
        return (*rms_bwd_res(f"{tag}_drms", h_in, d_hn, norm, d_h), tok)

    d_hb, tok = None, None
    for i in reversed(range(depth)):
        L = saved[i]
        j = i // 2
        pe, hg, z = L["ple"]
        d_z, d_pe, g = ple_bwd(f"l{i}_dple", d_h, pe, z, ple_norm[i:i + 1])
        put_small("ple_norm", i, g)
        d_hg = _mm_nt(f"l{i}_dhg", d_z, G[("ple_w_gate", i)], "row", 0, D, out_dtype=F32, to=512, tc=2048, after=tok)
        GW[("ple_w_gate", i)] = mm_tn(f"l{i}_dwgate", hg, d_z, "row", tk=512, tn=1024)
        GW[("ple_w_proj", i)] = mm_tn(f"l{i}_dwproj", p[i, 0], d_pe, "col")
        tok = reduce_tick(GW[("ple_w_proj", i)])
        d_h, d_hb, g = rms_bwd_res(f"l{i}_dgate_rms", L["h3"], d_hg, ple_gate_norm[i:i + 1], d_h)
        put_small("ple_gate_norm", i, g)

        d_h, d_hb, g, tok = ffn_bwd(f"l{i}_ffb", d_h, d_hb, ffn_b_norm[i:i + 1], "ffn_b_w_in", "ffn_b_w_out", i,
                                    L["ffn_b"], tok)
        put_small("ffn_b_norm", i, g)

        hn = L["hn_m"]
        if i % 2 == 0:
            lat, cq, ckv, q_raw, kv_raw, Qh, Kh, Vh, O, w_in_pad, w_uq_pad = L["mla"]
            d_O = _mm_nt(f"l{i}_dO", d_hb, G[("mla_w_o", j)], "row", 0, H * D_V, out_dtype=BF16, to=512, tc=2048,
                         after=tok)
            GW[("mla_w_o", j)] = mm_tn(f"l{i}_dwo", O, d_hb, "row", tk=512, tn=1024)
            dQ, dK, dV = attn_bwd(f"l{i}_dattn", Qh, Kh, Vh, d_O)
            d_q_raw, d_kv_raw, d_kr, gq, gk = mla_prep_bwd(f"l{i}_dprep", dQ, dK, dV, q_raw, kv_raw, lat, tabs,
                                                           q_gain_p, k_gain_p, QL + KL)
            put_small("mla_q_gain", j, gq[:, :QK_DIM])
            put_small("mla_k_gain", j, gk[:, :QK_DIM])
            d_cq = _mm_nt(f"l{i}_dcq", d_q_raw, w_uq_pad, "row", 0, QL, out_dtype=F32, to=512, tc=2048)
            g_uq = mm_tn(f"l{i}_dwuq", cq, d_q_raw, "row", shards=1, out_dtype=F32, tk=512, tn=1024)
            g_uq = g_uq.reshape(QL, H, HEAD_PAD)[:, :, :QK_DIM].reshape(QL, N_CHIPS, -1).transpose(1, 0, 2)
            GW[("mla_w_uq", j)] = g_uq.astype(BF16)
            d_ckv = _mm_nt(f"l{i}_dckv", d_kv_raw, G[("mla_w_ukv", j)], "col", 0, KL, out_dtype=F32, to=512, tc=1024)
            GW[("mla_w_ukv", j)] = mm_tn(f"l{i}_dwukv", ckv, d_kv_raw, "col", tk=512, tn=1024)
            d_lat, gq, gk = lat_norm_bwd(f"l{i}_dlatnorm", lat, d_cq, d_ckv, d_kr, mla_q_lat_norm[j:j + 1],
                                         mla_kv_lat_norm[j:j + 1])
            put_small("mla_q_lat_norm", j, gq)
            put_small("mla_kv_lat_norm", j, gk)
            d_hn = _mm_nt(f"l{i}_dhn_lat", d_lat, w_in_pad, "row", 0, D, out_dtype=F32, to=1024, tc=lat_pad)
            g_in = mm_tn(f"l{i}_dwin_lat", hn, d_lat, "row", shards=1, out_dtype=F32, tk=1024, tn=lat_pad)
            GW[("mla_w_in", j)] = g_in[0, :, :lat_w].reshape(N_CHIPS, D // N_CHIPS, lat_w).astype(BF16)
            tok = reduce_tick(GW[("mla_w_in", j)])
        else:
            ag, u, yc, cact = L["conv"]
            d_cact = _mm_nt(f"l{i}_dcact", d_hb, G[("conv_w_pw2", j)], "row", 0, C, out_dtype=F32, to=512, tc=2048,
                            after=tok)
            GW[("conv_w_pw2", j)] = mm_tn(f"l{i}_dwpw2", cact, d_hb, "row", tk=512, tn=1024)
            d_yc, g1, g2, g3 = ln_silu_bwd(f"l{i}_dln", yc, d_cact, ln_g_full, ln_b_full)
            put_small("conv_ln_g", j, g1)
            put_small("conv_ln_b", j, g2)
            put_small("conv_b_dw", j, g3)
            d_u = dwconv_bwd_u(f"l{i}_ddw_u", d_yc, w_dw_full)
            put_small("conv_w_dw", j, dwconv_bwd_w(f"l{i}_ddw_w", u, d_yc))
            d_ag, g = glu_bwd(f"l{i}_dglu", ag, d_u)
            put_small("conv_b_pw1", j, g)
            d_hn = _mm_nt(f"l{i}_dhn_pw1", d_ag, G[("conv_w_pw1", j)], "col", 0, D, out_dtype=F32, tc=1024)
            GW[("conv_w_pw1", j)] = mm_tn(f"l{i}_dwpw1", hn, d_ag, "col", tn=1024)
            tok = reduce_tick(GW[("conv_w_pw1", j)])
        d_h, d_hb, g = rms_bwd_res(f"l{i}_dmix_rms", L["h1"], d_hn, mix_norm[i:i + 1], d_h)
        put_small("mix_norm", i, g)

        d_h, d_hb, g, tok = ffn_bwd(f"l{i}_ffa", d_h, d_hb, ffn_a_norm[i:i + 1], "ffn_a_w_in", "ffn_a_w_out", i,
                                    L["ffn_a"], tok)
        put_small("ffn_a_norm", i, g)
    grad_x = d_h[None]

    tok = reduce_tick(reduce_tick(d_h))
    for g in deferred:
        tok = sibling_half(g, tok)
    tok = reduce_tick(reduce_tick(tok, defer=False), defer=False)
    names = [n for n, _ in BIG_WEIGHTS]
    grads, delta, new_m, new_v = {}, {}, {}, {}
    for n in names:
        grads[n], delta[n], new_m[n], new_v[n] = [a.reshape(W[n].shape) for a in merged[n]]

    rep = []
    for n in REPLICATED_SMALL:
        rep.append(jnp.concatenate([SG[n][i] for i in sorted(SG[n])], axis=0))
    shd = [SG["conv_b_pw1"][0].reshape(2, C), SG["conv_w_dw"][0][:CONV_WIDTH], SG["conv_b_dw"][0],
           SG["conv_ln_g"][0], SG["conv_ln_b"][0]]
    red = allreduce_pack("allreduce_small", _pack_rows(rep + shd, D))
    red = _unpack_rows(red, [a.shape for a in rep + shd])
    for n, g in zip(REPLICATED_SMALL, red):
        grads[n] = g
    own = lambda a, w: lax.dynamic_slice_in_dim(a, chip * w, w, axis=a.ndim - 1)
    sh = red[len(rep):]
    grads["conv_b_pw1"] = own(sh[0].reshape(1, 2 * C), 2 * C // N_CHIPS)
    grads["conv_w_dw"] = own(sh[1], C // N_CHIPS)[None]
    grads["conv_b_dw"] = own(sh[2], C // N_CHIPS)
    grads["conv_ln_g"] = own(sh[3], C // N_CHIPS)
    grads["conv_ln_b"] = own(sh[4], C // N_CHIPS)

    small = REPLICATED_SMALL + SHARDED_SMALL
    shapes = [two(W[n]).shape for n in small]
    packs = [_pack_rows([two(src[n]) for n in small], D) for src in (W, grads, M1, V2)]
    outs = adamw("adamw_small", *packs)
    for dst, pk in zip((delta, new_m, new_v), outs):
        for n, a in zip(small, _unpack_rows(pk, shapes)):
            dst[n] = a.reshape(W[n].shape)
    for n in small:
        grads[n] = grads[n].reshape(W[n].shape)

    return (loss, grad_x, *[grads[n] for n in WEIGHT_ORDER], *[delta[n] for n in WEIGHT_ORDER],
            *[new_m[n] for n in WEIGHT_ORDER], *[new_v[n] for n in WEIGHT_ORDER])
```

```python
import functools
import math

import jax
import jax.numpy as jnp
from jax import lax
from jax.experimental import pallas as pl
from jax.experimental.pallas import tpu as pltpu

F32, BF16 = jnp.float32, jnp.bfloat16
MESH = pl.DeviceIdType.MESH

N_CHIPS = 4
EPS = 1e-6
D_NOPE, D_ROPE, D_V = 128, 64, 128
QK_DIM = D_NOPE + D_ROPE
HEAD_PAD = 256
ROPE_THETA = 10000.0
CONV_WIDTH = 31
CONV_TAPS_PAD = 32
FFN_RESIDUAL_WEIGHT = 0.5
ADAM_LR, ADAM_B1, ADAM_B2, ADAM_EPS, ADAM_WD, ADAM_STEP = 0.001, 0.9, 0.999, 1e-08, 0.01, 10
VMEM_LIMIT_BYTES = 56 * 1024 * 1024
NEG_BIG = -1e30

NN_DIMS = (((1,), (0,)), ((), ()))
NT_DIMS = (((1,), (1,)), ((), ()))
TN_DIMS = (((0,), (0,)), ((), ()))


def _cparams(semantics=None):
    kw = dict(vmem_limit_bytes=VMEM_LIMIT_BYTES)
    if semantics is not None:
        kw["dimension_semantics"] = semantics
    return pltpu.CompilerParams(**kw)


def _tile(n, pref, mult=128):
    if n <= pref:
        return n
    t = (pref // mult) * mult
    while t >= mult:
        if n % t == 0:
            return t
        t -= mult
    return n


def _rowwise(name, fn, rows, vecs, outs, accs=(), tm=256, rc=32):
    T = rows[0].shape[0]
    tm = min(tm, T)
    rc = min(rc, tm)
    nr, nv, no, na = len(rows), len(vecs), len(outs), len(accs)
    steps = tm // rc

    def body(*refs):
        row_refs = refs[:nr]
        vec_refs = refs[nr:nr + nv]
        out_refs = refs[nr + nv:nr + nv + no]
        acc_refs = refs[nr + nv + no:]
        if na:
            @pl.when(pl.program_id(0) == 0)
            def _():
                for a in acc_refs:
                    a[...] = jnp.zeros_like(a)

        def step(r, carry):
            sl = pl.ds(pl.multiple_of(r * rc, rc), rc)
            res = fn(*[x[sl, :] for x in row_refs], *[v[...] for v in vec_refs])
            for o, val in zip(out_refs, res[:no]):
                o[sl, :] = val.astype(o.dtype)
            return tuple(c + val for c, val in zip(carry, res[no:]))

        init = tuple(jnp.zeros(s, F32) for s in accs)
        tot = lax.fori_loop(0, steps, step, init)
        for a, val in zip(acc_refs, tot):
            a[...] += val

    in_specs = [pl.BlockSpec((tm, x.shape[1]), lambda i: (i, 0)) for x in rows]
    in_specs += [pl.BlockSpec(v.shape, lambda i: (0, 0)) for v in vecs]
    out_specs = [pl.BlockSpec((tm, d), lambda i: (i, 0)) for d, _ in outs]
    out_specs += [pl.BlockSpec(s, lambda i: (0, 0)) for s in accs]
    out_shape = [jax.ShapeDtypeStruct((T, d), dt) for d, dt in outs]
    out_shape += [jax.ShapeDtypeStruct(s, F32) for s in accs]
    return pl.pallas_call(
        body, name=name, grid=(T // tm,), in_specs=in_specs, out_specs=out_specs, out_shape=out_shape,
        compiler_params=_cparams(("arbitrary",)),
    )(*rows, *vecs)


def _colsum(v):
    return jnp.sum(v, axis=0, keepdims=True)


def _rstd(x):
    return lax.rsqrt(jnp.mean(x * x, axis=-1, keepdims=True) + EPS)


def _rms_bwd(x, dy, g):
    r = _rstd(x)
    xh = x * r
    dyg = dy * g
    dx = r * (dyg - xh * jnp.mean(dyg * xh, axis=-1, keepdims=True))
    return dx, dy * xh


def _sigmoid(x):
    return 1.0 / (1.0 + jnp.exp(-x))


def rms_fwd(name, h, g):
    def fn(x, gv):
        return ((x * _rstd(x)) * gv,)
    return _rowwise(name, fn, [h], [g], [(h.shape[1], BF16)])[0]


def rms_bwd_res(name, h, d_y, g, d_res):
    D = h.shape[1]

    def fn(x, dy, dr, gv):
        dx, dgr = _rms_bwd(x, dy, gv)
        dh = dr + dx
        return dh, dh, _colsum(dgr)
    return _rowwise(name, fn, [h, d_y, d_res], [g], [(D, F32), (D, BF16)], [(1, D)], tm=128)


def loss_head(name, y, target):
    D = y.shape[1]

    def fn(yv, tv):
        e = yv - tv
        tot = jnp.sum(_colsum(e * e), axis=1, keepdims=True) * (0.5 / D)
        return e * (1.0 / D), jnp.broadcast_to(tot, (1, 128))
    return _rowwise(name, fn, [y, target], [], [(D, F32)], [(1, 128)])


def ple_fwd(name, h, pe, z, g_e):
    def fn(hv, pv, zv, gv):
        return (hv + (pv * _rstd(pv)) * gv * _sigmoid(zv),)
    return _rowwise(name, fn, [h, pe, z], [g_e], [(h.shape[1], F32)], tm=128)[0]


def ple_bwd(name, d_h, pe, z, g_e):
    D = d_h.shape[1]

    def fn(dh, pv, zv, gv):
        gate = _sigmoid(zv)
        e = (pv * _rstd(pv)) * gv
        d_z = dh * e * gate * (1.0 - gate)
        d_pe, dgr = _rms_bwd(pv, dh * gate, gv)
        return d_z, d_pe, _colsum(dgr)
    return _rowwise(name, fn, [d_h, pe, z], [g_e], [(D, BF16), (D, BF16)], [(1, D)], tm=128)


def lat_norm_fwd(name, lat, g_q, g_kv):
    QL, KL = g_q.shape[1], g_kv.shape[1]

    def fn(v, gq, gk):
        a = v[:, :QL]
        b = v[:, QL:QL + KL]
        return (a * _rstd(a)) * gq, (b * _rstd(b)) * gk
    return _rowwise(name, fn, [lat], [g_q, g_kv], [(QL, BF16), (KL, BF16)])


def lat_norm_bwd(name, lat, d_cq, d_ckv, d_krope, g_q, g_kv):
    QL, KL = g_q.shape[1], g_kv.shape[1]

    def fn(v, dq, dk, dr, gq, gk):
        da, ga = _rms_bwd(v[:, :QL], dq, gq)
        db, gb = _rms_bwd(v[:, QL:QL + KL], dk, gk)
        return jnp.concatenate([da, db, dr], axis=-1), _colsum(ga), _colsum(gb)
    return _rowwise(name, fn, [lat, d_cq, d_ckv, d_krope], [g_q, g_kv],
                    [(lat.shape[1], BF16)], [(1, QL), (1, KL)])


def glu_fwd(name, ag):
    C = ag.shape[1] // 2

    def fn(v):
        return (v[:, :C] * _sigmoid(v[:, C:]),)
    return _rowwise(name, fn, [ag], [], [(C, F32)], tm=128)[0]


def glu_bwd(name, ag, d_u):
    C = ag.shape[1] // 2

    def fn(v, du):
        a = v[:, :C]
        s = _sigmoid(v[:, C:])
        d = jnp.concatenate([du * s, du * a * s * (1.0 - s)], axis=-1)
        return d, _colsum(d)
    return _rowwise(name, fn, [ag, d_u], [], [(2 * C, BF16)], [(1, 2 * C)], tm=128)


def ln_silu_fwd(name, yc, g, b):
    def fn(v, gv, bv):
        xc = v - jnp.mean(v, axis=-1, keepdims=True)
        ln = xc * lax.rsqrt(jnp.mean(xc * xc, axis=-1, keepdims=True) + EPS) * gv + bv
        return (ln * _sigmoid(ln),)
    return _rowwise(name, fn, [yc], [g, b], [(yc.shape[1], BF16)], tm=128)[0]


def ln_silu_bwd(name, yc, d_out, g, b):
    C = yc.shape[1]

    def fn(v, do, gv, bv):
        xc = v - jnp.mean(v, axis=-1, keepdims=True)
        r = lax.rsqrt(jnp.mean(xc * xc, axis=-1, keepdims=True) + EPS)
        xh = xc * r
        ln = xh * gv + bv
        s = _sigmoid(ln)
        d_ln = do * s * (1.0 + ln * (1.0 - s))
        dxh = d_ln * gv
        dy = r * (dxh - jnp.mean(dxh, axis=-1, keepdims=True) - xh * jnp.mean(dxh * xh, axis=-1, keepdims=True))
        return dy, _colsum(d_ln * xh), _colsum(d_ln), _colsum(dy)
    return _rowwise(name, fn, [yc, d_out], [g, b], [(C, F32)], [(1, C), (1, C), (1, C)], tm=128)


def add_halves(name, core, own, got):
    S, Rh, C = got.shape
    tr = _tile(Rh, 512, 16)
    nrb = Rh // tr

    def body(c_ref, a_ref, b_ref, o_ref):
        o_ref[...] = (a_ref[...].astype(F32) + b_ref[...].astype(F32)).astype(o_ref.dtype)

    gs = pltpu.PrefetchScalarGridSpec(
        num_scalar_prefetch=1, grid=(S, nrb),
        in_specs=[pl.BlockSpec((None, tr, C), lambda s, r, c: (s, c[0] * nrb + r, 0)),
                  pl.BlockSpec((None, tr, C), lambda s, r, c: (s, r, 0))],
        out_specs=pl.BlockSpec((None, tr, C), lambda s, r, c: (s, r, 0)))
    return pl.pallas_call(
        body, name=name, grid_spec=gs, out_shape=jax.ShapeDtypeStruct((S, Rh, C), BF16),
        compiler_params=_cparams(("arbitrary", "arbitrary")))(core, own, got)


def adamw_half(name, ids, grad, w, m, v, layer, layers, into):
    own = isinstance(grad, tuple)
    Rh, C = grad[0].shape[1:] if own else grad.shape
    tr = _tile(Rh, 128, 16)
    nrb = Rh // tr
    n_grad = 4 if own else 1
    c1 = 1.0 / (1.0 - ADAM_B1 ** ADAM_STEP)
    c2 = 1.0 / (1.0 - ADAM_B2 ** ADAM_STEP)

    def body(ids_ref, *refs):
        g_refs, (w_ref, m_ref, v_ref) = refs[:n_grad], refs[n_grad:n_grad + 3]
        outs = refs[-5:] if own else refs[-4:]
        go_ref, d_ref, nm_ref, nv_ref = outs[:4]
        if own:
            gv = ((g_refs[0][...].astype(F32) + g_refs[1][...].astype(F32))
                  + (g_refs[2][...].astype(F32) + g_refs[3][...].astype(F32)))
            outs[4][...] = gv
        else:
            gv = g_refs[0][...]
        nm = ADAM_B1 * m_ref[...] + (1.0 - ADAM_B1) * gv
        nv = ADAM_B2 * v_ref[...] + (1.0 - ADAM_B2) * (gv * gv)
        go_ref[...] = gv
        d_ref[...] = -ADAM_LR * ((nm * c1) / (jnp.sqrt(nv * c2) + ADAM_EPS) + ADAM_WD * w_ref[...])
        nm_ref[...] = nm
        nv_ref[...] = nv

    half = (lambda ids: ids[0]) if own else (lambda ids: 1 - ids[0])
    rows = pl.BlockSpec((tr, C), lambda r, ids: ((2 * layer + half(ids)) * nrb + r, 0))
    plain = pl.BlockSpec((tr, C), lambda r, ids: (r, 0))
    if own:
        slot = lambda flip: pl.BlockSpec((None, tr, C), lambda r, ids: (ids[1] ^ flip, r, 0))
        in_specs = [slot(0), slot(1), slot(2), slot(3), rows, rows, rows]
        operands = [ids, grad[0], grad[1], grad[1], grad[1], w, m, v]
    else:
        in_specs = [plain, rows, rows, rows]
        operands = [ids, grad, w, m, v]
    aliases = {}
    if into is not None:
        in_specs += [pl.BlockSpec(memory_space=pl.ANY)] * 4
        aliases = {len(operands) + k: k for k in range(4)}
        operands += list(into)
    full = jax.ShapeDtypeStruct((layers * 2 * Rh, C), F32)
    gs = pltpu.PrefetchScalarGridSpec(num_scalar_prefetch=1, grid=(nrb,), in_specs=in_specs,
                                      out_specs=[rows] * 4 + [plain] * own)
    return pl.pallas_call(
        body, name=name, grid_spec=gs, out_shape=[full] * 4 + [jax.ShapeDtypeStruct((Rh, C), F32)] * own,
        input_output_aliases=aliases, compiler_params=_cparams(("arbitrary",)))(*operands)


def cast_into_slot(name, w, chip, layer, layers, after=None):
    R, C = w.shape[0] // layers, w.shape[1]
    tr = _tile(R, 512, 16)
    nrb = R // tr

    def body(s_ref, w_ref, *rest):
        rest[-1][...] = w_ref[...].astype(BF16)

    in_specs = [pl.BlockSpec((tr, C), lambda r, s: (layer * nrb + r, 0))]
    operands = [chip, w]
    if after is not None:
        in_specs.append(pl.BlockSpec(memory_space=pl.ANY))
        operands.append(after)
    gs = pltpu.PrefetchScalarGridSpec(
        num_scalar_prefetch=1, grid=(nrb,), in_specs=in_specs,
        out_specs=pl.BlockSpec((None, tr, C), lambda r, s: (s[0], r, 0)))
    return pl.pallas_call(
        body, name=name, grid_spec=gs, out_shape=jax.ShapeDtypeStruct((N_CHIPS, R, C), BF16),
        compiler_params=_cparams(("arbitrary",)))(*operands)


def adamw(name, w, g, m, v):
    R, C = w.shape
    tr = _tile(R, 256, 8)
    c1 = 1.0 / (1.0 - ADAM_B1 ** ADAM_STEP)
    c2 = 1.0 / (1.0 - ADAM_B2 ** ADAM_STEP)

    def body(w_ref, g_ref, m_ref, v_ref, d_ref, nm_ref, nv_ref):
        gv = g_ref[...]
        nm = ADAM_B1 * m_ref[...] + (1.0 - ADAM_B1) * gv
        nv = ADAM_B2 * v_ref[...] + (1.0 - ADAM_B2) * (gv * gv)
        d_ref[...] = -ADAM_LR * ((nm * c1) / (jnp.sqrt(nv * c2) + ADAM_EPS) + ADAM_WD * w_ref[...])
        nm_ref[...] = nm
        nv_ref[...] = nv

    spec = pl.BlockSpec((tr, C), lambda r: (r, 0))
    return pl.pallas_call(
        body, name=name, grid=(R // tr,), in_specs=[spec] * 4, out_specs=[spec] * 3,
        out_shape=[jax.ShapeDtypeStruct((R, C), F32)] * 3, compiler_params=_cparams(("arbitrary",)))(w, g, m, v)


def _matmul(name, a, b, *, grid, a_blk, a_map, b_blk, b_map, o_shape, o_dtype, o_blk, o_map, dims,
            scale=None, res=None, bias=None, bias_blk=None, bias_map=None, alias_into=None, after=None):
    nk = grid[2]
    has_res, has_bias, has_into = res is not None, bias is not None, alias_into is not None
    acc_shape = tuple(d for d in o_blk if d is not None)

    def body(*refs):
        a_ref, b_ref = refs[0], refs[1]
        pos = 2
        res_ref = bias_ref = None
        if has_res:
            res_ref = refs[pos]
            pos += 1
        if has_bias:
            bias_ref = refs[pos]
            pos += 1
        if has_into:
            pos += 1
        if after is not None:
            pos += 1
        o_ref = refs[pos]
        av, bv = a_ref[...], b_ref[...]
        if bv.ndim == 3:
            bv = bv.reshape(-1, bv.shape[-1])
        if av.dtype != BF16:
            av = av.astype(BF16)
        if bv.dtype != BF16:
            bv = bv.astype(BF16)
        part = lax.dot_general(av, bv, dims, preferred_element_type=F32)

        def finish(acc):
            if scale is not None:
                acc = acc * scale
            if has_bias:
                acc = acc + bias_ref[...]
            if has_res:
                acc = acc + res_ref[...]
            o_ref[...] = acc.astype(o_ref.dtype)

        if nk == 1:
            finish(part)
        else:
            acc_ref = refs[pos + 1]
            k = pl.program_id(2)

            @pl.when(k == 0)
            def _():
                acc_ref[...] = part

            @pl.when(k > 0)
            def _():
                acc_ref[...] += part

            @pl.when(k == nk - 1)
            def _():
                finish(acc_ref[...])

    operands = [a, b]
    in_specs = [pl.BlockSpec(a_blk, a_map), pl.BlockSpec(b_blk, b_map)]
    if has_res:
        operands.append(res)
        in_specs.append(pl.BlockSpec(o_blk, o_map))
    if has_bias:
        operands.append(bias)
        in_specs.append(pl.BlockSpec(bias_blk, bias_map))
    aliases = {}
    if has_into:
        aliases = {len(operands): 0}
        operands.append(alias_into)
        in_specs.append(pl.BlockSpec(memory_space=pl.ANY))
    if after is not None:
        operands.append(after)
        in_specs.append(pl.BlockSpec(memory_space=pl.ANY))
    return pl.pallas_call(
        body, name=name, grid=grid, in_specs=in_specs, out_specs=pl.BlockSpec(o_blk, o_map),
        out_shape=jax.ShapeDtypeStruct(o_shape, o_dtype),
        scratch_shapes=[pltpu.VMEM(acc_shape, F32)] if nk > 1 else [],
        input_output_aliases=aliases,
        compiler_params=_cparams(("parallel", "parallel", "arbitrary")),
    )(*operands)


def mm_nn(name, a, w3, kind, layer, *, out_dtype, scale=None, res=None, bias=None, tm=1024, tn=512, tk=2048,
          after=None, whole_k=False):
    M, K = a.shape
    S, _, C = w3.shape
    tm = _tile(M, tm, 16)
    b_blk = None
    if kind == "col":
        N = S * C
        tk, tn = _tile(K, tk), _tile(C, tn)
        kb, nb = K // tk, C // tn
        b_map = lambda n, m, k: (n // nb, layer * kb + k, n % nb)
    elif whole_k:
        N, K4, tk, tn = C, K // S, K, _tile(C, tn)
        b_blk, b_map = (S, K4, tn), lambda n, m, k: (0, layer, n)
    else:
        N, K4 = C, K // S
        tk, tn = _tile(K4, tk), _tile(C, tn)
        kb4 = K4 // tk
        b_map = lambda n, m, k: (k // kb4, layer * kb4 + k % kb4, n)
    return _matmul(
        name, a, w3, grid=(N // tn, M // tm, K // tk),
        a_blk=(tm, tk), a_map=lambda n, m, k: (m, k), b_blk=b_blk or (None, tk, tn), b_map=b_map,
        o_shape=(M, N), o_dtype=out_dtype, o_blk=(tm, tn), o_map=lambda n, m, k: (m, n), dims=NN_DIMS,
        scale=scale, res=res, bias=bias, bias_blk=(1, tn), bias_map=lambda n, m, k: (0, n), after=after)


def _mm_nt(name, g, w3, kind, layer, K, *, out_dtype, scale=None, tm=1024, to=1024, tc=1408, after=None):
    S, _, C = w3.shape
    tc = _tile(C, tc)
    if g.ndim == 3:
        M, N = g.shape[1], 2 * g.shape[2]
        tm = _tile(M, tm, 16)
        cb = g.shape[2] // tc
        a_blk, a_map = (None, tm, tc), lambda o, m, c: (c // cb, m, c % cb)
    else:
        M, N = g.shape
        tm = _tile(M, tm, 16)
        a_blk, a_map = (tm, tc), lambda o, m, c: (m, c)
    if kind == "col":
        nb = C // tc
        to = _tile(K, to)
        ob = K // to
        b_map = lambda o, m, c: (c // nb, layer * ob + o, c % nb)
    else:
        K4 = K // S
        to = _tile(K4, to)
        ob4 = K4 // to
        b_map = lambda o, m, c: (o // ob4, layer * ob4 + o % ob4, c)
    return _matmul(
        name, g, w3, grid=(K // to, M // tm, N // tc),
        a_blk=a_blk, a_map=a_map, b_blk=(None, to, tc), b_map=b_map,
        o_shape=(M, K), o_dtype=out_dtype, o_blk=(tm, to), o_map=lambda o, m, c: (m, o), dims=NT_DIMS,
        scale=scale, after=after)


def mm_tn(name, a, g, kind, *, shards=N_CHIPS, layer=0, layers=1, into=None, out_dtype=BF16, scale=None,
          tk=1024, tn=1408, tm=2048, after=None):
    M, K = a.shape
    N = 2 * g.shape[2] if g.ndim == 3 else g.shape[1]
    tm = _tile(M, tm, 16)
    if kind == "col":
        C = N // shards
        tk, tn = _tile(K, tk), _tile(C, tn)
        kb, nb = K // tk, C // tn
        o_shape = (shards, layers * K, C)
        o_map = lambda k, n, m: (n // nb, layer * kb + k, n % nb)
    else:
        K4 = K // shards
        tk, tn = _tile(K4, tk), _tile(N, tn)
        kb4 = K4 // tk
        o_shape = (shards, layers * K4, N)
        o_map = lambda k, n, m: (k // kb4, layer * kb4 + k % kb4, n)
    if g.ndim == 3:
        nbh = g.shape[2] // tn
        g_blk, g_map = (None, tm, tn), lambda k, n, m: (n // nbh, m, n % nbh)
    else:
        g_blk, g_map = (tm, tn), lambda k, n, m: (m, n)
    return _matmul(
        name, a, g, grid=(K // tk, N // tn, M // tm),
        a_blk=(tm, tk), a_map=lambda k, n, m: (m, k), b_blk=g_blk, b_map=g_map,
        o_shape=o_shape, o_dtype=out_dtype, o_blk=(None, tk, tn), o_map=o_map, dims=TN_DIMS,
        scale=scale, alias_into=into, after=after)


def ffn_in_act(name, hn, w3, *, tn=256, after=None):
    M, K = hn.shape
    S, _, C = w3.shape
    tn = _tile(C, tn)
    nb = C // tn

    def body(a_ref, bg_ref, bu_ref, *rest):
        gu_ref, act_ref = rest[-2:]
        a = a_ref[...]
        g = jnp.dot(a, bg_ref[...], preferred_element_type=F32)
        u = jnp.dot(a, bu_ref[...], preferred_element_type=F32)
        gu_ref[0] = g.astype(BF16)
        gu_ref[1] = u.astype(BF16)
        act_ref[...] = (g * _sigmoid(g) * u).astype(BF16)

    in_specs = [pl.BlockSpec((M, K), lambda j: (0, 0)),
                pl.BlockSpec((None, K, tn), lambda j: (j // nb, 0, j % nb)),
                pl.BlockSpec((None, K, tn), lambda j: (S // 2 + j // nb, 0, j % nb))]
    operands = [hn, w3, w3]
    if after is not None:
        in_specs.append(pl.BlockSpec(memory_space=pl.ANY))
        operands.append(after)
    width = S // 2 * C
    return pl.pallas_call(
        body, name=name, grid=(width // tn,), in_specs=in_specs,
        out_specs=[pl.BlockSpec((2, M, tn), lambda j: (0, 0, j)), pl.BlockSpec((M, tn), lambda j: (0, j))],
        out_shape=[jax.ShapeDtypeStruct((2, M, width), BF16), jax.ShapeDtypeStruct((M, width), BF16)],
        compiler_params=_cparams(("arbitrary",)))(*operands)


def ffn_dact_dgu(name, d_out, w3, gu, scale, *, tm=512, after=None):
    M, N = d_out.shape
    S, K4, _ = w3.shape
    tm = _tile(M, tm, 16)
    to = K4
    chunk = 256 if to > 256 else to

    def body(a_ref, b_ref, gu_ref, *rest):
        o_ref = rest[-1]
        a = a_ref[...]
        for c0 in range(0, to, chunk):
            cols = slice(c0, min(c0 + chunk, to))
            da = lax.dot_general(a, b_ref[cols, :], NT_DIMS, preferred_element_type=F32) * scale
            g = gu_ref[0, :, cols].astype(F32)
            u = gu_ref[1, :, cols].astype(F32)
            s = _sigmoid(g)
            o_ref[0, :, cols] = (da * u * s * (1.0 + g * (1.0 - s))).astype(BF16)
            o_ref[1, :, cols] = (da * g * s).astype(BF16)

    halves = pl.BlockSpec((2, tm, to), lambda o, m: (0, m, o))
    in_specs = [pl.BlockSpec((tm, N), lambda o, m: (m, 0)), pl.BlockSpec((None, to, N), lambda o, m: (o, 0, 0)), halves]
    operands = [d_out, w3, gu]
    if after is not None:
        in_specs.append(pl.BlockSpec(memory_space=pl.ANY))
        operands.append(after)
    return pl.pallas_call(
        body, name=name, grid=(S, M // tm), in_specs=in_specs, out_specs=halves,
        out_shape=jax.ShapeDtypeStruct((2, M, S * K4), BF16),
        compiler_params=_cparams(("parallel", "arbitrary")))(*operands)


def rope_tables(name, pos):
    T = pos.shape[0]
    half = D_ROPE // 2

    def body(p_ref, c_ref, s1_ref, s2_ref):
        lane = lax.broadcasted_iota(jnp.int32, (T, 128), 1)
        idx = (lane & (half - 1)).astype(F32)
        ang = p_ref[...] * jnp.exp(idx * (-2.0 * math.log(ROPE_THETA) / D_ROPE))
        cs, sn = jnp.cos(ang), jnp.sin(ang)
        c_ref[...] = jnp.where(lane < D_ROPE, cs, 0.0)
        s1_ref[...] = jnp.where(lane < half, -sn, 0.0)
        s2_ref[...] = jnp.where((lane >= half) & (lane < D_ROPE), sn, 0.0)

    return pl.pallas_call(body, name=name, out_shape=[jax.ShapeDtypeStruct((T, 128), F32)] * 3,
                          compiler_params=_cparams())(pos)


def _rope(v, cs, s1, s2):
    return v * cs + pltpu.roll(v, 128 - D_ROPE // 2, 1) * s1 + pltpu.roll(v, D_ROPE // 2, 1) * s2


def _rope_bwd(d, cs, s1, s2):
    return d * cs + pltpu.roll(d * s1, D_ROPE // 2, 1) + pltpu.roll(d * s2, 128 - D_ROPE // 2, 1)


def _head_rstd(n, r):
    ms = (jnp.sum(n * n, axis=-1, keepdims=True) + jnp.sum(r * r, axis=-1, keepdims=True)) * (1.0 / QK_DIM)
    return lax.rsqrt(ms + EPS)


def mla_prep_fwd(name, q_raw, kv_raw, lat, tabs, q_gain, k_gain, rope_col, tm=128):
    T = q_raw.shape[0]
    H = q_raw.shape[1] // HEAD_PAD
    tm = min(tm, T)
    rope_blk = rope_col // 128

    def body(q_ref, kv_ref, kr_ref, c_ref, s1_ref, s2_ref, qg_ref, kg_ref, Q_ref, K_ref, V_ref):
        cs, s1, s2 = c_ref[...], s1_ref[...], s2_ref[...]
        qg, kg = qg_ref[...], kg_ref[...]
        kr = kr_ref[...]
        for h in range(H):
            lo = HEAD_PAD * h
            n, r = q_ref[:, lo:lo + 128], q_ref[:, lo + 128:lo + 256]
            rs = _head_rstd(n, r)
            Q_ref[h, :, 0:128] = (n * rs * qg[:, :128]).astype(BF16)
            Q_ref[h, :, 128:256] = _rope(r * rs * qg[:, 128:], cs, s1, s2).astype(BF16)
            n = kv_ref[:, lo:lo + 128]
            rs = _head_rstd(n, kr)
            K_ref[h, :, 0:128] = (n * rs * kg[:, :128]).astype(BF16)
            K_ref[h, :, 128:256] = _rope(kr * rs * kg[:, 128:], cs, s1, s2).astype(BF16)
            V_ref[h] = kv_ref[:, lo + 128:lo + 256].astype(BF16)

    row = lambda w: pl.BlockSpec((tm, w), lambda i: (i, 0))
    vec = pl.BlockSpec((1, HEAD_PAD), lambda i: (0, 0))
    return pl.pallas_call(
        body, name=name, grid=(T // tm,),
        in_specs=[row(H * HEAD_PAD), row(H * HEAD_PAD), pl.BlockSpec((tm, 128), lambda i: (i, rope_blk)),
                  row(128), row(128), row(128), vec, vec],
        out_specs=[pl.BlockSpec((H, tm, HEAD_PAD), lambda i: (0, i, 0))] * 2 + [pl.BlockSpec((H, tm, D_V), lambda i: (0, i, 0))],
        out_shape=[jax.ShapeDtypeStruct((H, T, HEAD_PAD), BF16)] * 2 + [jax.ShapeDtypeStruct((H, T, D_V), BF16)],
        compiler_params=_cparams(("arbitrary",)),
    )(q_raw, kv_raw, lat, *tabs, q_gain, k_gain)


def mla_prep_bwd(name, dQ, dK, dV, q_raw, kv_raw, lat, tabs, q_gain, k_gain, rope_col, tm=128):
    T = q_raw.shape[0]
    H = q_raw.shape[1] // HEAD_PAD
    tm = min(tm, T)
    rope_blk = rope_col // 128

    def body(dQ_ref, dK_ref, dV_ref, q_ref, kv_ref, kr_ref, c_ref, s1_ref, s2_ref, qg_ref, kg_ref,
             dq_ref, dkv_ref, dkr_ref, dqg_ref, dkg_ref):
        @pl.when(pl.program_id(0) == 0)
        def _():
            dqg_ref[...] = jnp.zeros_like(dqg_ref)
            dkg_ref[...] = jnp.zeros_like(dkg_ref)

        cs, s1, s2 = c_ref[...], s1_ref[...], s2_ref[...]
        qg, kg = qg_ref[...], kg_ref[...]
        kr = kr_ref[...]
        dkr = jnp.zeros((tm, 128), F32)
        gq_n = jnp.zeros((1, 128), F32)
        gq_r = jnp.zeros((1, 128), F32)
        gk_n = jnp.zeros((1, 128), F32)
        gk_r = jnp.zeros((1, 128), F32)

        def norm_bwd(n, r, dn, dr, gain):
            rs = _head_rstd(n, r)
            nh, rh = n * rs, r * rs
            dng, drg = dn * gain[:, :128], dr * gain[:, 128:]
            mean = (jnp.sum(dng * nh, axis=-1, keepdims=True) + jnp.sum(drg * rh, axis=-1, keepdims=True)) * (1.0 / QK_DIM)
            return rs * (dng - nh * mean), rs * (drg - rh * mean), _colsum(dn * nh), _colsum(dr * rh)

        for h in range(H):
            lo = HEAD_PAD * h
            n, r = q_ref[:, lo:lo + 128], q_ref[:, lo + 128:lo + 256]
            dn = dQ_ref[h, :, 0:128].astype(F32)
            dr = _rope_bwd(dQ_ref[h, :, 128:256].astype(F32), cs, s1, s2)
            a, b, g1, g2 = norm_bwd(n, r, dn, dr, qg)
            dq_ref[:, lo:lo + 128] = a.astype(BF16)
            dq_ref[:, lo + 128:lo + 256] = b.astype(BF16)
            gq_n, gq_r = gq_n + g1, gq_r + g2
            n = kv_ref[:, lo:lo + 128]
            dn = dK_ref[h, :, 0:128].astype(F32)
            dr = _rope_bwd(dK_ref[h, :, 128:256].astype(F32), cs, s1, s2)
            a, b, g1, g2 = norm_bwd(n, kr, dn, dr, kg)
            dkv_ref[:, lo:lo + 128] = a.astype(BF16)
            dkv_ref[:, lo + 128:lo + 256] = dV_ref[h].astype(BF16)
            dkr = dkr + b
            gk_n, gk_r = gk_n + g1, gk_r + g2
        dkr_ref[...] = dkr
        dqg_ref[:, 0:128] += gq_n
        dqg_ref[:, 128:256] += gq_r
        dkg_ref[:, 0:128] += gk_n
        dkg_ref[:, 128:256] += gk_r

    row = lambda w: pl.BlockSpec((tm, w), lambda i: (i, 0))
    vec = pl.BlockSpec((1, HEAD_PAD), lambda i: (0, 0))
    hd = lambda w: pl.BlockSpec((H, tm, w), lambda i: (0, i, 0))
    return pl.pallas_call(
        body, name=name, grid=(T // tm,),
        in_specs=[hd(HEAD_PAD), hd(HEAD_PAD), hd(D_V), row(H * HEAD_PAD), row(H * HEAD_PAD),
                  pl.BlockSpec((tm, 128), lambda i: (i, rope_blk)), row(128), row(128), row(128), vec, vec],
        out_specs=[row(H * HEAD_PAD), row(H * HEAD_PAD), row(128), vec, vec],
        out_shape=[jax.ShapeDtypeStruct((T, H * HEAD_PAD), BF16)] * 2 + [jax.ShapeDtypeStruct((T, 128), F32)]
        + [jax.ShapeDtypeStruct((1, HEAD_PAD), F32)] * 2,
        compiler_params=_cparams(("arbitrary",)),
    )(dQ, dK, dV, q_raw, kv_raw, lat, *tabs, q_gain, k_gain)


def _causal_probs(q, k, scale, row0):
    s = lax.dot_general(q, k, NT_DIMS, preferred_element_type=F32) * scale
    row = row0 + lax.broadcasted_iota(jnp.int32, s.shape, 0)
    col = lax.broadcasted_iota(jnp.int32, s.shape, 1)
    s = jnp.where(col <= row, s, NEG_BIG)
    p = jnp.exp(s - jnp.max(s, axis=-1, keepdims=True))
    return p, jnp.sum(p, axis=-1, keepdims=True)


def attn_fwd(name, Q, K, V, tq=512):
    H, T, E = Q.shape
    tq = min(tq, T)
    nq = T // tq
    scale = QK_DIM ** -0.5

    def body(q_ref, k_ref, v_ref, o_ref):
        i = pl.program_id(1)
        for ib in range(nq):
            @pl.when(i == ib)
            def _():
                n = (ib + 1) * tq
                p, l = _causal_probs(q_ref[...], k_ref[0:n, :], scale, ib * tq)
                o = jnp.dot(p.astype(BF16), v_ref[0:n, :], preferred_element_type=F32)
                o_ref[...] = (o / l).astype(o_ref.dtype)

    return pl.pallas_call(
        body, name=name, grid=(H, nq),
        in_specs=[pl.BlockSpec((None, tq, E), lambda h, i: (h, i, 0)),
                  pl.BlockSpec((None, T, E), lambda h, i: (h, 0, 0)),
                  pl.BlockSpec((None, T, D_V), lambda h, i: (h, 0, 0))],
        out_specs=pl.BlockSpec((tq, D_V), lambda h, i: (i, h)),
        out_shape=jax.ShapeDtypeStruct((T, H * D_V), BF16),
        compiler_params=_cparams(("parallel", "arbitrary")),
    )(Q, K, V)


def attn_bwd(name, Q, K, V, dO, tq=512):
    H, T, E = Q.shape
    tq = min(tq, T)
    nq = T // tq
    scale = QK_DIM ** -0.5

    def body(q_ref, k_ref, v_ref, do_ref, dq_ref, dk_ref, dv_ref):
        i = pl.program_id(1)

        @pl.when(i == 0)
        def _():
            dk_ref[...] = jnp.zeros_like(dk_ref)
            dv_ref[...] = jnp.zeros_like(dv_ref)

        for ib in range(nq):
            @pl.when(i == ib)
            def _():
                n = (ib + 1) * tq
                q, k, v, do = q_ref[...], k_ref[0:n, :], v_ref[0:n, :], do_ref[...]
                p, l = _causal_probs(q, k, scale, ib * tq)
                p = p / l
                dp = lax.dot_general(do, v, NT_DIMS, preferred_element_type=F32)
                ds = p * (dp - jnp.sum(p * dp, axis=-1, keepdims=True)) * scale
                dsb, pb = ds.astype(BF16), p.astype(BF16)
                dq_ref[...] = jnp.dot(dsb, k, preferred_element_type=F32)
                dk_ref[0:n, :] += lax.dot_general(dsb, q, TN_DIMS, preferred_element_type=F32)
                dv_ref[0:n, :] += lax.dot_general(pb, do, TN_DIMS, preferred_element_type=F32)

    return pl.pallas_call(
        body, name=name, grid=(H, nq),
        in_specs=[pl.BlockSpec((None, tq, E), lambda h, i: (h, i, 0)),
                  pl.BlockSpec((None, T, E), lambda h, i: (h, 0, 0)),
                  pl.BlockSpec((None, T, D_V), lambda h, i: (h, 0, 0)),
                  pl.BlockSpec((tq, D_V), lambda h, i: (i, h))],
        out_specs=[pl.BlockSpec((None, tq, E), lambda h, i: (h, i, 0)),
                   pl.BlockSpec((None, T, E), lambda h, i: (h, 0, 0)),
                   pl.BlockSpec((None, T, D_V), lambda h, i: (h, 0, 0))],
        out_shape=[jax.ShapeDtypeStruct((H, T, E), F32)] * 2 + [jax.ShapeDtypeStruct((H, T, D_V), F32)],
        compiler_params=_cparams(("parallel", "arbitrary")),
    )(Q, K, V, dO)


def _dw_specs(T, C, tm, tc, halo):
    cur = pl.BlockSpec((tm, tc), lambda j, i: (i, j))
    last = T // tm - 1
    if halo == "prev":
        nbr = pl.BlockSpec((tm, tc), lambda j, i: (jnp.maximum(i - 1, 0), j))
    else:
        nbr = pl.BlockSpec((tm, tc), lambda j, i: (jnp.minimum(i + 1, last), j))
    return cur, nbr


def dwconv_fwd(name, u, w, b, tm=256, tc=512, rs=32):
    T, C = u.shape
    tm, tc = min(tm, T), min(tc, C)
    cur, prev = _dw_specs(T, C, tm, tc, "prev")

    def body(up_ref, uc_ref, w_ref, b_ref, o_ref, scr):
        i = pl.program_id(1)

        @pl.when(i == 0)
        def _():
            scr[pl.ds(0, tm), :] = jnp.zeros((tm, tc), F32)

        @pl.when(i > 0)
        def _():
            scr[pl.ds(0, tm), :] = up_ref[...]

        scr[pl.ds(tm, tm), :] = uc_ref[...]
        for s in range(tm // rs):
            acc = jnp.broadcast_to(b_ref[...], (rs, tc))
            for k in range(CONV_WIDTH):
                acc = acc + w_ref[pl.ds(k, 1), :] * scr[pl.ds(tm - (CONV_WIDTH - 1) + k + rs * s, rs), :]
            o_ref[pl.ds(rs * s, rs), :] = acc

    return pl.pallas_call(
        body, name=name, grid=(C // tc, T // tm),
        in_specs=[prev, cur, pl.BlockSpec((CONV_TAPS_PAD, tc), lambda j, i: (0, j)), pl.BlockSpec((1, tc), lambda j, i: (0, j))],
        out_specs=cur, out_shape=jax.ShapeDtypeStruct((T, C), F32),
        scratch_shapes=[pltpu.VMEM((2 * tm, tc), F32)], compiler_params=_cparams(("parallel", "arbitrary")),
    )(u, u, w, b)


def dwconv_bwd_u(name, dy, w, tm=256, tc=512, rs=32):
    T, C = dy.shape
    tm, tc = min(tm, T), min(tc, C)
    cur, nxt = _dw_specs(T, C, tm, tc, "next")
    last = T // tm - 1

    def body(dc_ref, dn_ref, w_ref, o_ref, scr):
        i = pl.program_id(1)
        scr[pl.ds(0, tm), :] = dc_ref[...]

        @pl.when(i == last)
        def _():
            scr[pl.ds(tm, tm), :] = jnp.zeros((tm, tc), F32)

        @pl.when(i < last)
        def _():
            scr[pl.ds(tm, tm), :] = dn_ref[...]

        for s in range(tm // rs):
            acc = jnp.zeros((rs, tc), F32)
            for k in range(CONV_WIDTH):
                acc = acc + w_ref[pl.ds(k, 1), :] * scr[pl.ds((CONV_WIDTH - 1) - k + rs * s, rs), :]
            o_ref[pl.ds(rs * s, rs), :] = acc

    return pl.pallas_call(
        body, name=name, grid=(C // tc, T // tm),
        in_specs=[cur, nxt, pl.BlockSpec((CONV_TAPS_PAD, tc), lambda j, i: (0, j))],
        out_specs=cur, out_shape=jax.ShapeDtypeStruct((T, C), F32),
        scratch_shapes=[pltpu.VMEM((2 * tm, tc), F32)], compiler_params=_cparams(("parallel", "arbitrary")),
    )(dy, dy, w)


def dwconv_bwd_w(name, u, dy, tm=256, tc=512, rs=32):
    T, C = u.shape
    tm, tc = min(tm, T), min(tc, C)
    cur, prev = _dw_specs(T, C, tm, tc, "prev")

    def body(up_ref, uc_ref, dy_ref, o_ref, scr):
        i = pl.program_id(1)

        @pl.when(i == 0)
        def _():
            scr[pl.ds(0, tm), :] = jnp.zeros((tm, tc), F32)
            o_ref[...] = jnp.zeros_like(o_ref)

        @pl.when(i > 0)
        def _():
            scr[pl.ds(0, tm), :] = up_ref[...]

        scr[pl.ds(tm, tm), :] = uc_ref[...]
        for k in range(CONV_WIDTH):
            acc = jnp.zeros((rs, tc), F32)
            for s in range(tm // rs):
                acc = acc + dy_ref[pl.ds(rs * s, rs), :] * scr[pl.ds(tm - (CONV_WIDTH - 1) + k + rs * s, rs), :]
            o_ref[pl.ds(k, 1), :] += _colsum(acc)

    return pl.pallas_call(
        body, name=name, grid=(C // tc, T // tm),
        in_specs=[prev, cur, cur],
        out_specs=pl.BlockSpec((CONV_TAPS_PAD, tc), lambda j, i: (0, j)),
        out_shape=jax.ShapeDtypeStruct((CONV_TAPS_PAD, C), F32),
        scratch_shapes=[pltpu.VMEM((2 * tm, tc), F32)], compiler_params=_cparams(("parallel", "arbitrary")),
    )(u, u, dy)


def _place():
    x, y, c = lax.axis_index("x"), lax.axis_index("y"), lax.axis_index("c")
    return x, y, c


def _other_chips(x, y):
    return [(1 - x, y, 2 * (1 - x) + y), (x, 1 - y, 2 * x + (1 - y)), (1 - x, 1 - y, 2 * (1 - x) + (1 - y))]


def _hbm_specs(n):
    return [pl.BlockSpec(memory_space=pl.ANY)] * n


HBM_SPEC = pl.BlockSpec(memory_space=pltpu.HBM)
SEM_SPEC = pl.BlockSpec(memory_space=pltpu.SEMAPHORE)
ANY_SPEC = pl.BlockSpec(memory_space=pl.ANY)
SIDE_EFFECT = pltpu.SideEffectType.DATAFLOW_SIDE_EFFECTING


def _half(ref, slot, which):
    rh = ref.shape[1] // 2
    return ref.at[slot, pl.ds(pl.multiple_of(which * rh, 16), rh), :]


def _hbm(a):
    return pltpu.with_memory_space_constraint(a, pltpu.HBM)


def gather_start(name, groups, after):
    flat = [b for g in groups for b in g]
    n, ng = len(flat), len(groups)

    def body(*refs):
        send, recv = refs[n + 1:n + 1 + ng], refs[n + 1 + ng:n + 1 + 2 * ng]
        out, token = refs[n + 1 + 2 * ng:2 * n + 1 + 2 * ng], refs[2 * n + 1 + 2 * ng]
        token[...] = jnp.zeros_like(token)
        x, y, c = _place()
        me = 2 * x + y
        a = 0
        for g, grp in enumerate(groups):
            for k in range(len(grp)):
                piece = _half(out[a], me, c)
                for j, (px, py, _) in enumerate(_other_chips(x, y)):
                    pltpu.make_async_remote_copy(
                        src_ref=piece, dst_ref=piece, send_sem=send[g].at[3 * k + j], recv_sem=recv[g].at[3 * k + j],
                        device_id=(px, py, c), device_id_type=MESH).start()
                a += 1

    sems = [pltpu.SemaphoreType.DMA((3 * len(g),)) for g in groups]
    res = pl.pallas_call(
        body, name=name, in_specs=[HBM_SPEC] * n + [ANY_SPEC],
        out_specs=[SEM_SPEC] * (2 * ng) + [HBM_SPEC] * n + [pl.BlockSpec(memory_space=pltpu.VMEM)],
        out_shape=sems + sems + [pltpu.HBM(b.shape, b.dtype) for b in flat] + [jax.ShapeDtypeStruct((8, 128), F32)],
        input_output_aliases={a: 2 * ng + a for a in range(n)},
        compiler_params=pltpu.CompilerParams(has_side_effects=SIDE_EFFECT),
    )(*[_hbm(b) for b in flat], after)
    send, recv, bufs = res[:ng], res[ng:2 * ng], list(res[2 * ng:2 * ng + n])
    out, a = [], 0
    for g, grp in enumerate(groups):
        out.append((send[g], recv[g], bufs[a:a + len(grp)]))
        a += len(grp)
    return out, res[2 * ng + n]


def gather_relay(name, started, after):
    send1, recv1, bufs = started
    n = len(bufs)

    def body(*refs):
        s1, r1 = refs[n], refs[n + 1]
        s2, r2, out, token = refs[n + 3], refs[n + 4], refs[n + 5:2 * n + 5], refs[2 * n + 5]
        x, y, c = _place()
        me = 2 * x + y
        chips = _other_chips(x, y)
        for k in range(n):
            for j, (px, py, idx) in enumerate(chips):
                cp = pltpu.make_async_remote_copy(
                    src_ref=_half(out[k], me, c), dst_ref=_half(out[k], idx, c), send_sem=s1.at[3 * k + j],
                    recv_sem=r1.at[3 * k + j], device_id=(px, py, c), device_id_type=MESH)
                cp.wait_send()
                cp.wait_recv()
        for k in range(n):
            for j, (px, py, idx) in enumerate(chips):
                piece = _half(out[k], idx, c)
                pltpu.make_async_remote_copy(
                    src_ref=piece, dst_ref=piece, send_sem=s2.at[3 * k + j], recv_sem=r2.at[3 * k + j],
                    device_id=(x, y, 1 - c), device_id_type=MESH).start()
        token[...] = jnp.zeros_like(token)

    sem = pltpu.SemaphoreType.DMA((3 * n,))
    res = pl.pallas_call(
        body, name=name, in_specs=[HBM_SPEC] * n + [SEM_SPEC, SEM_SPEC, ANY_SPEC],
        out_specs=[SEM_SPEC, SEM_SPEC] + [HBM_SPEC] * n + [pl.BlockSpec(memory_space=pltpu.VMEM)],
        out_shape=[sem, sem] + [pltpu.HBM(b.shape, b.dtype) for b in bufs] + [jax.ShapeDtypeStruct((8, 128), F32)],
        input_output_aliases={a: 2 + a for a in range(n)},
        compiler_params=pltpu.CompilerParams(has_side_effects=SIDE_EFFECT),
    )(*bufs, send1, recv1, after)
    return res[0], res[1], list(res[2:2 + n]), res[2 + n]


def gather_wait(name, relayed, after):
    send2, recv2, bufs, _ = relayed
    n = len(bufs)

    def body(*refs):
        s2, r2, out = refs[n], refs[n + 1], refs[n + 3:]
        x, y, c = _place()
        for k in range(n):
            for j, (px, py, idx) in enumerate(_other_chips(x, y)):
                cp = pltpu.make_async_remote_copy(
                    src_ref=_half(out[k], idx, c), dst_ref=_half(out[k], idx, 1 - c), send_sem=s2.at[3 * k + j],
                    recv_sem=r2.at[3 * k + j], device_id=(x, y, 1 - c), device_id_type=MESH)
                cp.wait_send()
                cp.wait_recv()

    res = pl.pallas_call(
        body, name=name, in_specs=[HBM_SPEC] * n + [SEM_SPEC, SEM_SPEC, ANY_SPEC], out_specs=[HBM_SPEC] * n,
        out_shape=[pltpu.HBM(b.shape, b.dtype) for b in bufs], input_output_aliases={a: a for a in range(n)},
        compiler_params=pltpu.CompilerParams(has_side_effects=SIDE_EFFECT),
    )(*bufs, send2, recv2, after)
    return list(res)


def split_start(name, parts, after):
    bufs = [b for p in parts for b in p[0]]
    land_shapes = [s for p in parts for s in p[1]]
    nb, nl = len(bufs), len(land_shapes)

    def body(*refs):
        send, recv = refs[nb + 1], refs[nb + 2]
        out, lands, token = refs[nb + 3:2 * nb + 3], refs[2 * nb + 3:2 * nb + 3 + nl], refs[2 * nb + 3 + nl]
        x, y, c = _place()
        b0 = l0 = k0 = 0
        for p_bufs, p_lands, n_copies, plan in parts:
            copies = plan(out[b0:b0 + len(p_bufs)], lands[l0:l0 + len(p_lands)], x, y, c)
            for k, (src, dst, to, _) in enumerate(copies):
                pltpu.make_async_remote_copy(src_ref=src, dst_ref=dst, send_sem=send.at[k0 + k],
                                             recv_sem=recv.at[k0 + k], device_id=to, device_id_type=MESH).start()
            b0, l0, k0 = b0 + len(p_bufs), l0 + len(p_lands), k0 + n_copies
        token[...] = jnp.zeros_like(token)

    sem = pltpu.SemaphoreType.DMA((sum(p[2] for p in parts),))
    res = pl.pallas_call(
        body, name=name, in_specs=[HBM_SPEC] * nb + [ANY_SPEC],
        out_specs=[SEM_SPEC, SEM_SPEC] + [HBM_SPEC] * (nb + nl) + [pl.BlockSpec(memory_space=pltpu.VMEM)],
        out_shape=[sem, sem] + [pltpu.HBM(b.shape, b.dtype) for b in bufs]
        + [pltpu.HBM(s, d) for s, d in land_shapes] + [jax.ShapeDtypeStruct((8, 128), F32)],
        input_output_aliases={a: 2 + a for a in range(nb)},
        compiler_params=pltpu.CompilerParams(has_side_effects=SIDE_EFFECT),
    )(*[_hbm(b) for b in bufs], after)
    out_bufs, out_lands, token = list(res[2:2 + nb]), list(res[2 + nb:2 + nb + nl]), res[2 + nb + nl]
    states, b0, l0, k0 = [], 0, 0, 0
    for p_bufs, p_lands, n_copies, _ in parts:
        states.append((res[0], res[1], out_bufs[b0:b0 + len(p_bufs)], out_lands[l0:l0 + len(p_lands)], token, k0))
        b0, l0, k0 = b0 + len(p_bufs), l0 + len(p_lands), k0 + n_copies
    return states


def split_wait(name, started, plan, after):
    send, recv, bufs, lands, _, k0 = started
    nb, nl = len(bufs), len(lands)

    def body(*refs):
        s, r = refs[nb + nl], refs[nb + nl + 1]
        out, lo = refs[nb + nl + 3:2 * nb + nl + 3], refs[2 * nb + nl + 3:]
        x, y, c = _place()
        for k, (src, _, to, landed) in enumerate(plan(out, lo, x, y, c)):
            cp = pltpu.make_async_remote_copy(src_ref=src, dst_ref=landed, send_sem=s.at[k0 + k],
                                              recv_sem=r.at[k0 + k], device_id=to, device_id_type=MESH)
            cp.wait_send()
            cp.wait_recv()

    res = pl.pallas_call(
        body, name=name, in_specs=[HBM_SPEC] * (nb + nl) + [SEM_SPEC, SEM_SPEC, ANY_SPEC],
        out_specs=[HBM_SPEC] * (nb + nl), out_shape=[pltpu.HBM(b.shape, b.dtype) for b in bufs + lands],
        input_output_aliases={a: a for a in range(nb + nl)},
        compiler_params=pltpu.CompilerParams(has_side_effects=SIDE_EFFECT),
    )(*bufs, *lands, send, recv, after)
    return list(res[:nb]), list(res[nb:])


def swap_halves_plan(parts, lands, x, y, c):
    out = []
    for a in range(len(parts)):
        rh = parts[a].shape[1] // 2
        theirs = parts[a].at[:, pl.ds(pl.multiple_of((1 - c) * rh, 16), rh), :]
        out.append((theirs, lands[a], (x, y, 1 - c), lands[a]))
    return out


def chip_exchange_plan(sums, lands, x, y, c):
    me = 2 * x + y
    out = []
    for a in range(len(sums)):
        for px, py, idx in _other_chips(x, y):
            out.append((sums[a].at[idx], lands[a].at[me], (px, py, c), lands[a].at[idx]))
    return out


def share_grad_plan(halves, lands, x, y, c):
    return [(halves[a], lands[a], (x, y, 1 - c), lands[a]) for a in range(len(halves))]


def allreduce_pack(name, pack):
    R, W = pack.shape

    def body(p_ref, o_ref, sib, pair, got, send_sems, recv_sems):
        x, y, c = _place()

        def swap(k, src, dst, to):
            cp = pltpu.make_async_remote_copy(src_ref=src, dst_ref=dst, send_sem=send_sems.at[k],
                                              recv_sem=recv_sems.at[k], device_id=to, device_id_type=MESH)
            cp.start()
            return cp

        cp = swap(0, p_ref, sib, (x, y, 1 - c))
        cp.wait()
        pair[...] = p_ref[...] + sib[...]
        cps = [swap(1, pair, got.at[0], (1 - x, y, c)), swap(2, pair, got.at[1], (x, 1 - y, c)),
               swap(3, pair, got.at[2], (1 - x, 1 - y, c))]
        for cp in cps:
            cp.wait()
        o_ref[...] = (pair[...] + got[1]) + (got[0] + got[2])

    return pl.pallas_call(
        body, name=name, out_shape=jax.ShapeDtypeStruct((R, W), F32),
        in_specs=[pl.BlockSpec(memory_space=pltpu.VMEM)], out_specs=pl.BlockSpec(memory_space=pltpu.VMEM),
        scratch_shapes=[pltpu.VMEM((R, W), F32), pltpu.VMEM((R, W), F32), pltpu.VMEM((3, R, W), F32),
                        pltpu.SemaphoreType.DMA((4,)), pltpu.SemaphoreType.DMA((4,))],
        compiler_params=_cparams(),
    )(pack)


BIG_WEIGHTS = [
    ("ffn_a_w_in", "col"), ("ffn_a_w_out", "row"), ("ffn_b_w_in", "col"), ("ffn_b_w_out", "row"),
    ("mla_w_in", "row"), ("mla_w_uq", "col"), ("mla_w_ukv", "col"), ("mla_w_o", "row"),
    ("conv_w_pw1", "col"), ("conv_w_pw2", "row"), ("ple_w_proj", "col"), ("ple_w_gate", "row"),
]
WEIGHT_ORDER = ["ffn_a_norm", "ffn_a_w_in", "ffn_a_w_out", "ffn_b_norm", "ffn_b_w_in", "ffn_b_w_out", "mix_norm",
                "mla_w_in", "mla_q_lat_norm", "mla_kv_lat_norm", "mla_w_uq", "mla_w_ukv", "mla_q_gain", "mla_k_gain",
                "mla_w_o", "conv_w_pw1", "conv_b_pw1", "conv_w_dw", "conv_b_dw", "conv_ln_g", "conv_ln_b", "conv_w_pw2",
                "ple_w_proj", "ple_norm", "ple_gate_norm", "ple_w_gate"]
REPLICATED_SMALL = ["ffn_a_norm", "ffn_b_norm", "mix_norm", "ple_norm", "ple_gate_norm",
                    "mla_q_lat_norm", "mla_kv_lat_norm", "mla_q_gain", "mla_k_gain"]
SHARDED_SMALL = ["conv_b_pw1", "conv_w_dw", "conv_b_dw", "conv_ln_g", "conv_ln_b"]
PACK_ROWS = 8


def _pack_rows(arrs, width):
    out = []
    for a in arrs:
        r = -(-a.shape[0] // PACK_ROWS) * PACK_ROWS
        out.append(jnp.pad(a, ((0, r - a.shape[0]), (0, width - a.shape[1]))))
    return jnp.concatenate(out, axis=0)


def _unpack_rows(pack, shapes):
    out, r0 = [], 0
    for (r, w) in shapes:
        out.append(pack[r0:r0 + r, :w])
        r0 += -(-r // PACK_ROWS) * PACK_ROWS
    return out


def kernel(x, p, positions, ffn_a_norm, ffn_a_w_in, ffn_a_w_out, ffn_b_norm, ffn_b_w_in, ffn_b_w_out, mix_norm, mla_w_in, mla_q_lat_norm, mla_kv_lat_norm, mla_w_uq, mla_w_ukv, mla_q_gain, mla_k_gain, mla_w_o, conv_w_pw1, conv_b_pw1, conv_w_dw, conv_b_dw, conv_ln_g, conv_ln_b, conv_w_pw2, ple_w_proj, ple_norm, ple_gate_norm, ple_w_gate, loss_target, m_ffn_a_norm, m_ffn_a_w_in, m_ffn_a_w_out, m_ffn_b_norm, m_ffn_b_w_in, m_ffn_b_w_out, m_mix_norm, m_mla_w_in, m_mla_q_lat_norm, m_mla_kv_lat_norm, m_mla_w_uq, m_mla_w_ukv, m_mla_q_gain, m_mla_k_gain, m_mla_w_o, m_conv_w_pw1, m_conv_b_pw1, m_conv_w_dw, m_conv_b_dw, m_conv_ln_g, m_conv_ln_b, m_conv_w_pw2, m_ple_w_proj, m_ple_norm, m_ple_gate_norm, m_ple_w_gate, v_ffn_a_norm, v_ffn_a_w_in, v_ffn_a_w_out, v_ffn_b_norm, v_ffn_b_w_in, v_ffn_b_w_out, v_mix_norm, v_mla_w_in, v_mla_q_lat_norm, v_mla_kv_lat_norm, v_mla_w_uq, v_mla_w_ukv, v_mla_q_gain, v_mla_k_gain, v_mla_w_o, v_conv_w_pw1, v_conv_b_pw1, v_conv_w_dw, v_conv_b_dw, v_conv_ln_g, v_conv_ln_b, v_conv_w_pw2, v_ple_w_proj, v_ple_norm, v_ple_gate_norm, v_ple_w_gate):
    args = dict(locals())
    W = {n: args[n] for n in WEIGHT_ORDER}
    M1 = {n: args["m_" + n] for n in WEIGHT_ORDER}
    V2 = {n: args["v_" + n] for n in WEIGHT_ORDER}

    T, D = x.shape[1], x.shape[2]
    depth = ffn_a_norm.shape[0]
    H = mla_w_ukv.shape[2] * N_CHIPS // (D_NOPE + D_V)
    QL, KL = mla_q_lat_norm.shape[1], mla_kv_lat_norm.shape[1]
    C = conv_w_pw2.shape[1] * N_CHIPS
    lat_w = QL + KL + D_ROPE
    lat_pad = QL + KL + 128

    cx, cy, cc = lax.axis_index("x"), lax.axis_index("y"), lax.axis_index("c")
    chip = (2 * cx + cy).astype(jnp.int32)
    chip_arr = chip.reshape(1)
    core_arr = cc.astype(jnp.int32).reshape(1)

    def stage_groups(i):
        mix = ([("mla_w_in", i // 2), ("mla_w_uq", i // 2), ("mla_w_ukv", i // 2), ("mla_w_o", i // 2)] if i % 2 == 0
               else [("conv_w_pw1", i // 2), ("conv_w_pw2", i // 2)])
        return [[("ffn_a_w_in", i)], [("ffn_a_w_out", i)], mix, [("ffn_b_w_in", i)], [("ffn_b_w_out", i)],
                [("ple_w_proj", i), ("ple_w_gate", i)]]

    groups = [g for i in range(depth) for g in stage_groups(i)]
    A_IN, A_OUT, MIX, B_IN, B_OUT, PLE, PER_LAYER = 0, 1, 2, 3, 4, 5, 6

    def slot_of(key, after=None):
        n, l = key
        return cast_into_slot(f"cast_{n}_{l}", W[n].reshape(-1, W[n].shape[-1]), chip_arr, l, W[n].shape[0], after)

    def placed(a, width):
        full = jnp.zeros(a.shape[:-1] + (width,), F32)
        full = lax.dynamic_update_slice_in_dim(full, a, chip * a.shape[-1], axis=a.ndim - 1)
        return full * (cc == 0).astype(F32)

    b_pw1_sh = conv_b_pw1.reshape(1, -1)
    small_in = [placed(b_pw1_sh, 2 * C).reshape(2, C), placed(conv_w_dw[0], C), placed(conv_b_dw, C),
                placed(conv_ln_g, C), placed(conv_ln_b, C)]
    small_pack = allreduce_pack("gather_small", _pack_rows(small_in, C))
    small_full = _unpack_rows(small_pack, [(2, C), (CONV_WIDTH, C), (1, C), (1, C), (1, C)])

    FIRST = 2
    started, tok0 = gather_start("gather_start_first", [[slot_of(k) for k in g] for g in groups[:FIRST]], small_pack)
    rest, tok_rest = gather_start("gather_start_rest", [[slot_of(k, tok0) for k in g] for g in groups[FIRST:]], tok0)
    started = started + rest
    relayed, G = {}, {}

    def relay(g, after):
        if g >= len(groups):
            return None
        relayed[g] = gather_relay(f"gather_relay_{g}", started[g], after)
        return relayed[g][3]

    def ready(g, after):
        if g not in relayed:
            relay(g, after)
        for key, buf in zip(groups[g], gather_wait(f"gather_wait_{g}", relayed[g], after)):
            G[key] = buf

    relay(0, tok_rest)

    b_pw1_full = small_full[0].reshape(1, 2 * C)
    w_dw_full = jnp.pad(small_full[1], ((0, CONV_TAPS_PAD - CONV_WIDTH), (0, 0)))
    b_dw_full, ln_g_full, ln_b_full = small_full[2], small_full[3], small_full[4]

    pad_gain = lambda g: jnp.pad(g, ((0, 0), (0, HEAD_PAD - QK_DIM)))
    q_gain_p, k_gain_p = pad_gain(mla_q_gain), pad_gain(mla_k_gain)
    tabs = rope_tables("rope_tables", positions.reshape(T, 1).astype(F32))

    def ffn_fwd(tag, h, norm, w_in, w_out, layer, g_in):
        hn = rms_fwd(f"{tag}_rms", h, norm)
        ready(g_in, hn)
        tok = relay(g_in + 1, hn) if g_in > 0 else None
        gu, act = ffn_in_act(f"{tag}_in", hn, G[(w_in, layer)], after=tok)
        ready(g_in + 1, act)
        tok = relay(g_in + 2, act)
        out = mm_nn(f"{tag}_out", act, G[(w_out, layer)], "row", 0, out_dtype=F32, scale=FFN_RESIDUAL_WEIGHT, res=h,
                    tm=1024, tn=512, whole_k=True, after=tok)
        return out, (h, hn, gu, act)

    saved = []
    h = x[0]
    for i in range(depth):
        L = {}
        g0 = PER_LAYER * i
        h, L["ffn_a"] = ffn_fwd(f"l{i}_ffa", h, ffn_a_norm[i:i + 1], "ffn_a_w_in", "ffn_a_w_out", i, g0 + A_IN)
        L["h1"] = h
        hn = rms_fwd(f"l{i}_mix_rms", h, mix_norm[i:i + 1])
        L["hn_m"] = hn
        ready(g0 + MIX, hn)
        j = i // 2
        if i % 2 == 0:
            w_in_pad = jnp.pad(G[("mla_w_in", j)].reshape(D, lat_w), ((0, 0), (0, lat_pad - lat_w)))[None]
            uq = G[("mla_w_uq", j)].transpose(1, 0, 2).reshape(QL, H, QK_DIM)
            w_uq_pad = jnp.pad(uq, ((0, 0), (0, 0), (0, HEAD_PAD - QK_DIM))).reshape(1, QL, H * HEAD_PAD)
            lat = mm_nn(f"l{i}_lat", hn, w_in_pad, "row", 0, out_dtype=F32, tm=2048)
            cq, ckv = lat_norm_fwd(f"l{i}_latnorm", lat, mla_q_lat_norm[j:j + 1], mla_kv_lat_norm[j:j + 1])
            q_raw = mm_nn(f"l{i}_uq", cq, w_uq_pad, "row", 0, out_dtype=F32, tm=2048)
            kv_raw = mm_nn(f"l{i}_ukv", ckv, G[("mla_w_ukv", j)], "col", 0, out_dtype=F32, tm=2048)
            Qh, Kh, Vh = mla_prep_fwd(f"l{i}_prep", q_raw, kv_raw, lat, tabs, q_gain_p, k_gain_p, QL + KL)
            O = attn_fwd(f"l{i}_attn", Qh, Kh, Vh)
            tok = relay(g0 + B_IN, O)
            h = mm_nn(f"l{i}_wo", O, G[("mla_w_o", j)], "row", 0, out_dtype=F32, res=h, tm=1024, tn=1024, after=tok)
            L["mla"] = (lat, cq, ckv, q_raw, kv_raw, Qh, Kh, Vh, O, w_in_pad, w_uq_pad)
        else:
            ag = mm_nn(f"l{i}_pw1", hn, G[("conv_w_pw1", j)], "col", 0, out_dtype=F32, bias=b_pw1_full, tm=2048)
            u = glu_fwd(f"l{i}_glu", ag)
            yc = dwconv_fwd(f"l{i}_dw", u, w_dw_full, b_dw_full)
            cact = ln_silu_fwd(f"l{i}_ln", yc, ln_g_full, ln_b_full)
            tok = relay(g0 + B_IN, cact)
            h = mm_nn(f"l{i}_pw2", cact, G[("conv_w_pw2", j)], "row", 0, out_dtype=F32, res=h, tm=1024, tn=1024,
                      after=tok)
            L["conv"] = (ag, u, yc, cact)
        L["h2"] = h
        h, L["ffn_b"] = ffn_fwd(f"l{i}_ffb", h, ffn_b_norm[i:i + 1], "ffn_b_w_in", "ffn_b_w_out", i, g0 + B_IN)
        L["h3"] = h
        ready(g0 + PLE, h)
        pe = mm_nn(f"l{i}_ple_proj", p[i, 0], G[("ple_w_proj", i)], "col", 0, out_dtype=F32, tm=2048)
        hg = rms_fwd(f"l{i}_gate_rms", h, ple_gate_norm[i:i + 1])
        tok = relay(g0 + PER_LAYER, hg)
        z = mm_nn(f"l{i}_ple_gate", hg, G[("ple_w_gate", i)], "row", 0, out_dtype=F32, tm=2048, after=tok)
        h = ple_fwd(f"l{i}_ple", h, pe, z, ple_norm[i:i + 1])
        L["ple"] = (pe, hg, z)
        saved.append(L)

    d_h, loss_part = loss_head("loss_head", h, loss_target[0])
    loss = lax.psum(loss_part[0, 0], ("x", "y", "c"))

    GW = {}
    SG = {}
    ids_arr = jnp.stack([cc.astype(jnp.int32), chip])
    two = lambda a: a.reshape(-1, a.shape[-1])
    merged = {}
    pipe = {}
    order = list(reversed(range(len(groups))))
    ticks = [0]

    def put_small(name, i, val):
        SG.setdefault(name, {})[i] = val

    deferred = []

    def sibling_half(g, after):
        _, theirs = split_wait(f"share_wait_{g}", pipe[g], share_grad_plan, after)
        for (n, l), gr in zip(groups[g], theirs):
            merged[n] = adamw_half(f"adamw_sib_{n}_{l}", ids_arr, gr, two(W[n]), two(M1[n]), two(V2[n]), l,
                                   W[n].shape[0], merged[n])
        return merged[groups[g][-1][0]][0]

    def reduce_tick(after, defer=True):
        k, tok = ticks[0], after
        ticks[0] += 1
        grp = lambda j: order[j] if 0 <= j < len(order) else None
        g = grp(k - 4)
        if g is not None:
            if defer:
                deferred.append(g)
            else:
                sibling_half(g, tok)
        starts = []
        g = grp(k - 3)
        if g is not None:
            sums, landed = split_wait(f"exchange_wait_{g}", pipe[g], chip_exchange_plan, tok)
            halves = []
            for (n, l), s, ld in zip(groups[g], sums, landed):
                *merged[n], mine = adamw_half(f"adamw_own_{n}_{l}", ids_arr, (s, ld), two(W[n]), two(M1[n]),
                                              two(V2[n]), l, W[n].shape[0], merged.get(n))
                halves.append(mine)
            starts.append((g, (halves, [(h.shape, h.dtype) for h in halves], len(halves), share_grad_plan)))
        g = grp(k - 1)
        if g is not None:
            parts, got = split_wait(f"swap_wait_{g}", pipe[g], swap_halves_plan, tok)
            sums = [add_halves(f"pair_sum_{n}_{l}", core_arr, q, r) for (n, l), q, r in zip(groups[g], parts, got)]
            starts.append((g, (sums, [(s.shape, s.dtype) for s in sums], 3 * len(sums), chip_exchange_plan)))
        g = grp(k)
        if g is not None:
            parts = [GW[key] for key in groups[g]]
            lands = [((q.shape[0], q.shape[1] // 2, q.shape[2]), q.dtype) for q in parts]
            starts.append((g, (parts, lands, len(parts), swap_halves_plan)))
        if starts:
            behind = ids_arr if grp(k) is not None else tok
            for (g, _), state in zip(starts, split_start(f"reduce_start_{k}", [p for _, p in starts], behind)):
                pipe[g] = state
            tok = state[4]
        return tok

    def ffn_bwd(tag, d_h, d_hb, norm, w_in, w_out, layer, fw, tok):
        h_in, hn, gu, act = fw
        GW[(w_out, layer)] = mm_tn(f"{tag}_dwout", act, d_hb, "row", scale=FFN_RESIDUAL_WEIGHT, tk=1408, tn=1024,
                                   after=tok)
        tok = reduce_tick(GW[(w_out, layer)])
        dgu = ffn_dact_dgu(f"{tag}_dact", d_hb, G[(w_out, layer)], gu, FFN_RESIDUAL_WEIGHT, after=tok)
        GW[(w_in, layer)] = mm_tn(f"{tag}_dwin", hn, dgu, "col")
        tok = reduce_tick(GW[(w_in, layer)])
        d_hn = _mm_nt(f"{tag}_dhn", dgu, G[(w_in, layer)], "col", 0, D, out_dtype=F32, tc=2816, after=tok)
        return (*rms_bwd_res(f"{tag}_drms", h_in, d_hn, norm, d_h), tok)

    d_hb, tok = None, None
    for i in reversed(range(depth)):
        L = saved[i]
        j = i // 2
        pe, hg, z = L["ple"]
        d_z, d_pe, g = ple_bwd(f"l{i}_dple", d_h, pe, z, ple_norm[i:i + 1])
        put_small("ple_norm", i, g)
        d_hg = _mm_nt(f"l{i}_dhg", d_z, G[("ple_w_gate", i)], "row", 0, D, out_dtype=F32, to=512, tc=2048, after=tok)
        GW[("ple_w_gate", i)] = mm_tn(f"l{i}_dwgate", hg, d_z, "row", tk=512, tn=1024)
        GW[("ple_w_proj", i)] = mm_tn(f"l{i}_dwproj", p[i, 0], d_pe, "col")
        tok = reduce_tick(GW[("ple_w_proj", i)])
        d_h, d_hb, g = rms_bwd_res(f"l{i}_dgate_rms", L["h3"], d_hg, ple_gate_norm[i:i + 1], d_h)
        put_small("ple_gate_norm", i, g)

        d_h, d_hb, g, tok = ffn_bwd(f"l{i}_ffb", d_h, d_hb, ffn_b_norm[i:i + 1], "ffn_b_w_in", "ffn_b_w_out", i,
                                    L["ffn_b"], tok)
        put_small("ffn_b_norm", i, g)

        hn = L["hn_m"]
        if i % 2 == 0:
            lat, cq, ckv, q_raw, kv_raw, Qh, Kh, Vh, O, w_in_pad, w_uq_pad = L["mla"]
            d_O = _mm_nt(f"l{i}_dO", d_hb, G[("mla_w_o", j)], "row", 0, H * D_V, out_dtype=BF16, to=512, tc=2048,
                         after=tok)
            GW[("mla_w_o", j)] = mm_tn(f"l{i}_dwo", O, d_hb, "row", tk=512, tn=1024)
            dQ, dK, dV = attn_bwd(f"l{i}_dattn", Qh, Kh, Vh, d_O)
            d_q_raw, d_kv_raw, d_kr, gq, gk = mla_prep_bwd(f"l{i}_dprep", dQ, dK, dV, q_raw, kv_raw, lat, tabs,
                                                           q_gain_p, k_gain_p, QL + KL)
            put_small("mla_q_gain", j, gq[:, :QK_DIM])
            put_small("mla_k_gain", j, gk[:, :QK_DIM])
            d_cq = _mm_nt(f"l{i}_dcq", d_q_raw, w_uq_pad, "row", 0, QL, out_dtype=F32, to=512, tc=2048)
            g_uq = mm_tn(f"l{i}_dwuq", cq, d_q_raw, "row", shards=1, out_dtype=F32, tk=512, tn=1024)
            g_uq = g_uq.reshape(QL, H, HEAD_PAD)[:, :, :QK_DIM].reshape(QL, N_CHIPS, -1).transpose(1, 0, 2)
            GW[("mla_w_uq", j)] = g_uq.astype(BF16)
            d_ckv = _mm_nt(f"l{i}_dckv", d_kv_raw, G[("mla_w_ukv", j)], "col", 0, KL, out_dtype=F32, to=512, tc=1024)
            GW[("mla_w_ukv", j)] = mm_tn(f"l{i}_dwukv", ckv, d_kv_raw, "col", tk=512, tn=1024)
            d_lat, gq, gk = lat_norm_bwd(f"l{i}_dlatnorm", lat, d_cq, d_ckv, d_kr, mla_q_lat_norm[j:j + 1],
                                         mla_kv_lat_norm[j:j + 1])
            put_small("mla_q_lat_norm", j, gq)
            put_small("mla_kv_lat_norm", j, gk)
            d_hn = _mm_nt(f"l{i}_dhn_lat", d_lat, w_in_pad, "row", 0, D, out_dtype=F32, to=1024, tc=lat_pad)
            g_in = mm_tn(f"l{i}_dwin_lat", hn, d_lat, "row", shards=1, out_dtype=F32, tk=1024, tn=lat_pad)
            GW[("mla_w_in", j)] = g_in[0, :, :lat_w].reshape(N_CHIPS, D // N_CHIPS, lat_w).astype(BF16)
            tok = reduce_tick(GW[("mla_w_in", j)])
        else:
            ag, u, yc, cact = L["conv"]
            d_cact = _mm_nt(f"l{i}_dcact", d_hb, G[("conv_w_pw2", j)], "row", 0, C, out_dtype=F32, to=512, tc=2048,
                            after=tok)
            GW[("conv_w_pw2", j)] = mm_tn(f"l{i}_dwpw2", cact, d_hb, "row", tk=512, tn=1024)
            d_yc, g1, g2, g3 = ln_silu_bwd(f"l{i}_dln", yc, d_cact, ln_g_full, ln_b_full)
            put_small("conv_ln_g", j, g1)
            put_small("conv_ln_b", j, g2)
            put_small("conv_b_dw", j, g3)
            d_u = dwconv_bwd_u(f"l{i}_ddw_u", d_yc, w_dw_full)
            put_small("conv_w_dw", j, dwconv_bwd_w(f"l{i}_ddw_w", u, d_yc))
            d_ag, g = glu_bwd(f"l{i}_dglu", ag, d_u)
            put_small("conv_b_pw1", j, g)
            d_hn = _mm_nt(f"l{i}_dhn_pw1", d_ag, G[("conv_w_pw1", j)], "col", 0, D, out_dtype=F32, tc=1024)
            GW[("conv_w_pw1", j)] = mm_tn(f"l{i}_dwpw1", hn, d_ag, "col", tn=1024)
            tok = reduce_tick(GW[("conv_w_pw1", j)])
        d_h, d_hb, g = rms_bwd_res(f"l{i}_dmix_rms", L["h1"], d_hn, mix_norm[i:i + 1], d_h)
        put_small("mix_norm", i, g)

        d_h, d_hb, g, tok = ffn_bwd(f"l{i}_ffa", d_h, d_hb, ffn_a_norm[i:i + 1], "ffn_a_w_in", "ffn_a_w_out", i,
                                    L["ffn_a"], tok)
        put_small("ffn_a_norm", i, g)
    grad_x = d_h[None]

    tok = reduce_tick(reduce_tick(d_h))
    for g in deferred:
        tok = sibling_half(g, tok)
    tok = reduce_tick(reduce_tick(tok, defer=False), defer=False)
    names = [n for n, _ in BIG_WEIGHTS]
    grads, delta, new_m, new_v = {}, {}, {}, {}
    for n in names:
        grads[n], delta[n], new_m[n], new_v[n] = [a.reshape(W[n].shape) for a in merged[n]]

    rep = []
    for n in REPLICATED_SMALL:
        rep.append(jnp.concatenate([SG[n][i] for i in sorted(SG[n])], axis=0))
    shd = [SG["conv_b_pw1"][0].reshape(2, C), SG["conv_w_dw"][0][:CONV_WIDTH], SG["conv_b_dw"][0],
           SG["conv_ln_g"][0], SG["conv_ln_b"][0]]
    red = allreduce_pack("allreduce_small", _pack_rows(rep + shd, D))
    red = _unpack_rows(red, [a.shape for a in rep + shd])
    for n, g in zip(REPLICATED_SMALL, red):
        grads[n] = g
    own = lambda a, w: lax.dynamic_slice_in_dim(a, chip * w, w, axis=a.ndim - 1)
    sh = red[len(rep):]
    grads["conv_b_pw1"] = own(sh[0].reshape(1, 2 * C), 2 * C // N_CHIPS)
    grads["conv_w_dw"] = own(sh[1], C // N_CHIPS)[None]
    grads["conv_b_dw"] = own(sh[2], C // N_CHIPS)
    grads["conv_ln_g"] = own(sh[3], C // N_CHIPS)
    grads["conv_ln_b"] = own(sh[4], C // N_CHIPS)

    small = REPLICATED_SMALL + SHARDED_SMALL
    shapes = [two(W[n]).shape for n in small]
    packs = [_pack_rows([two(src[n]) for n in small], D) for src in (W, grads, M1, V2)]
    outs = adamw("adamw_small", *packs)
    for dst, pk in zip((delta, new_m, new_v), outs):
        for n, a in zip(small, _unpack_rows(pk, shapes)):
            dst[n] = a.reshape(W[n].shape)
    for n in small:
        grads[n] = grads[n].reshape(W[n].shape)

    return (loss, grad_x, *[grads[n] for n in WEIGHT_ORDER], *[delta[n] for n in WEIGHT_ORDER],
            *[new_m[n] for n in WEIGHT_ORDER], *[new_v[n] for n in WEIGHT_ORDER])
```

```python
import functools
import math

import jax
import jax.numpy as jnp
from jax import lax
from jax.experimental import pallas as pl
from jax.experimental.pallas import tpu as pltpu

F32, BF16 = jnp.float32, jnp.bfloat16
MESH = pl.DeviceIdType.MESH

N_CHIPS = 4
EPS = 1e-6
D_NOPE, D_ROPE, D_V = 128, 64, 128
QK_DIM = D_NOPE + D_ROPE
HEAD_PAD = 256
ROPE_THETA = 10000.0
CONV_WIDTH = 31
CONV_TAPS_PAD = 32
FFN_RESIDUAL_WEIGHT = 0.5
ADAM_LR, ADAM_B1, ADAM_B2, ADAM_EPS, ADAM_WD, ADAM_STEP = 0.001, 0.9, 0.999, 1e-08, 0.01, 10
VMEM_LIMIT_BYTES = 56 * 1024 * 1024
NEG_BIG = -1e30

NN_DIMS = (((1,), (0,)), ((), ()))
NT_DIMS = (((1,), (1,)), ((), ()))
TN_DIMS = (((0,), (0,)), ((), ()))


def _cparams(semantics=None):
    kw = dict(vmem_limit_bytes=VMEM_LIMIT_BYTES)
    if semantics is not None:
        kw["dimension_semantics"] = semantics
    return pltpu.CompilerParams(**kw)


def _tile(n, pref, mult=128):
    if n <= pref:
        return n
    t = (pref // mult) * mult
    while t >= mult:
        if n % t == 0:
            return t
        t -= mult
    return n


def _rowwise(name, fn, rows, vecs, outs, accs=(), tm=256, rc=32):
    T = rows[0].shape[0]
    tm = min(tm, T)
    rc = min(rc, tm)
    nr, nv, no, na = len(rows), len(vecs), len(outs), len(accs)
    steps = tm // rc

    def body(*refs):
        row_refs = refs[:nr]
        vec_refs = refs[nr:nr + nv]
        out_refs = refs[nr + nv:nr + nv + no]
        acc_refs = refs[nr + nv + no:]
        if na:
            @pl.when(pl.program_id(0) == 0)
            def _():
                for a in acc_refs:
                    a[...] = jnp.zeros_like(a)

        def step(r, carry):
            sl = pl.ds(pl.multiple_of(r * rc, rc), rc)
            res = fn(*[x[sl, :] for x in row_refs], *[v[...] for v in vec_refs])
            for o, val in zip(out_refs, res[:no]):
                o[sl, :] = val.astype(o.dtype)
            return tuple(c + val for c, val in zip(carry, res[no:]))

        init = tuple(jnp.zeros(s, F32) for s in accs)
        tot = lax.fori_loop(0, steps, step, init)
        for a, val in zip(acc_refs, tot):
            a[...] += val

    in_specs = [pl.BlockSpec((tm, x.shape[1]), lambda i: (i, 0)) for x in rows]
    in_specs += [pl.BlockSpec(v.shape, lambda i: (0, 0)) for v in vecs]
    out_specs = [pl.BlockSpec((tm, d), lambda i: (i, 0)) for d, _ in outs]
    out_specs += [pl.BlockSpec(s, lambda i: (0, 0)) for s in accs]
    out_shape = [jax.ShapeDtypeStruct((T, d), dt) for d, dt in outs]
    out_shape += [jax.ShapeDtypeStruct(s, F32) for s in accs]
    return pl.pallas_call(
        body, name=name, grid=(T // tm,), in_specs=in_specs, out_specs=out_specs, out_shape=out_shape,
        compiler_params=_cparams(("arbitrary",)),
    )(*rows, *vecs)


def _colsum(v):
    return jnp.sum(v, axis=0, keepdims=True)


def _rstd(x):
    return lax.rsqrt(jnp.mean(x * x, axis=-1, keepdims=True) + EPS)


def _rms_bwd(x, dy, g):
    r = _rstd(x)
    xh = x * r
    dyg = dy * g
    dx = r * (dyg - xh * jnp.mean(dyg * xh, axis=-1, keepdims=True))
    return dx, dy * xh


def _sigmoid(x):
    return 1.0 / (1.0 + jnp.exp(-x))


def rms_fwd(name, h, g):
    def fn(x, gv):
        return ((x * _rstd(x)) * gv,)
    return _rowwise(name, fn, [h], [g], [(h.shape[1], BF16)])[0]


def rms_bwd_res(name, h, d_y, g, d_res):
    D = h.shape[1]

    def fn(x, dy, dr, gv):
        dx, dgr = _rms_bwd(x, dy, gv)
        dh = dr + dx
        return dh, dh, _colsum(dgr)
    return _rowwise(name, fn, [h, d_y, d_res], [g], [(D, F32), (D, BF16)], [(1, D)], tm=128)


def loss_head(name, y, target):
    D = y.shape[1]

    def fn(yv, tv):
        e = yv - tv
        tot = jnp.sum(_colsum(e * e), axis=1, keepdims=True) * (0.5 / D)
        return e * (1.0 / D), jnp.broadcast_to(tot, (1, 128))
    return _rowwise(name, fn, [y, target], [], [(D, F32)], [(1, 128)])


def ple_fwd(name, h, pe, z, g_e):
    def fn(hv, pv, zv, gv):
        return (hv + (pv * _rstd(pv)) * gv * _sigmoid(zv),)
    return _rowwise(name, fn, [h, pe, z], [g_e], [(h.shape[1], F32)], tm=128)[0]


def ple_bwd(name, d_h, pe, z, g_e):
    D = d_h.shape[1]

    def fn(dh, pv, zv, gv):
        gate = _sigmoid(zv)
        e = (pv * _rstd(pv)) * gv
        d_z = dh * e * gate * (1.0 - gate)
        d_pe, dgr = _rms_bwd(pv, dh * gate, gv)
        return d_z, d_pe, _colsum(dgr)
    return _rowwise(name, fn, [d_h, pe, z], [g_e], [(D, BF16), (D, BF16)], [(1, D)], tm=128)


def lat_norm_fwd(name, lat, g_q, g_kv):
    QL, KL = g_q.shape[1], g_kv.shape[1]

    def fn(v, gq, gk):
        a = v[:, :QL]
        b = v[:, QL:QL + KL]
        return (a * _rstd(a)) * gq, (b * _rstd(b)) * gk
    return _rowwise(name, fn, [lat], [g_q, g_kv], [(QL, BF16), (KL, BF16)])


def lat_norm_bwd(name, lat, d_cq, d_ckv, d_krope, g_q, g_kv):
    QL, KL = g_q.shape[1], g_kv.shape[1]

    def fn(v, dq, dk, dr, gq, gk):
        da, ga = _rms_bwd(v[:, :QL], dq, gq)
        db, gb = _rms_bwd(v[:, QL:QL + KL], dk, gk)
        return jnp.concatenate([da, db, dr], axis=-1), _colsum(ga), _colsum(gb)
    return _rowwise(name, fn, [lat, d_cq, d_ckv, d_krope], [g_q, g_kv],
                    [(lat.shape[1], BF16)], [(1, QL), (1, KL)])


def glu_fwd(name, ag):
    C = ag.shape[1] // 2

    def fn(v):
        return (v[:, :C] * _sigmoid(v[:, C:]),)
    return _rowwise(name, fn, [ag], [], [(C, F32)], tm=128)[0]


def glu_bwd(name, ag, d_u):
    C = ag.shape[1] // 2

    def fn(v, du):
        a = v[:, :C]
        s = _sigmoid(v[:, C:])
        d = jnp.concatenate([du * s, du * a * s * (1.0 - s)], axis=-1)
        return d, _colsum(d)
    return _rowwise(name, fn, [ag, d_u], [], [(2 * C, BF16)], [(1, 2 * C)], tm=128)


def ln_silu_fwd(name, yc, g, b):
    def fn(v, gv, bv):
        xc = v - jnp.mean(v, axis=-1, keepdims=True)
        ln = xc * lax.rsqrt(jnp.mean(xc * xc, axis=-1, keepdims=True) + EPS) * gv + bv
        return (ln * _sigmoid(ln),)
    return _rowwise(name, fn, [yc], [g, b], [(yc.shape[1], BF16)], tm=128)[0]


def ln_silu_bwd(name, yc, d_out, g, b):
    C = yc.shape[1]

    def fn(v, do, gv, bv):
        xc = v - jnp.mean(v, axis=-1, keepdims=True)
        r = lax.rsqrt(jnp.mean(xc * xc, axis=-1, keepdims=True) + EPS)
        xh = xc * r
        ln = xh * gv + bv
        s = _sigmoid(ln)
        d_ln = do * s * (1.0 + ln * (1.0 - s))
        dxh = d_ln * gv
        dy = r * (dxh - jnp.mean(dxh, axis=-1, keepdims=True) - xh * jnp.mean(dxh * xh, axis=-1, keepdims=True))
        return dy, _colsum(d_ln * xh), _colsum(d_ln), _colsum(dy)
    return _rowwise(name, fn, [yc, d_out], [g, b], [(C, F32)], [(1, C), (1, C), (1, C)], tm=128)


def add_halves(name, core, own, got):
    S, Rh, C = got.shape
    tr = _tile(Rh, 512, 16)
    nrb = Rh // tr

    def body(c_ref, a_ref, b_ref, o_ref):
        o_ref[...] = (a_ref[...].astype(F32) + b_ref[...].astype(F32)).astype(o_ref.dtype)

    gs = pltpu.PrefetchScalarGridSpec(
        num_scalar_prefetch=1, grid=(S, nrb),
        in_specs=[pl.BlockSpec((None, tr, C), lambda s, r, c: (s, c[0] * nrb + r, 0)),
                  pl.BlockSpec((None, tr, C), lambda s, r, c: (s, r, 0))],
        out_specs=pl.BlockSpec((None, tr, C), lambda s, r, c: (s, r, 0)))
    return pl.pallas_call(
        body, name=name, grid_spec=gs, out_shape=jax.ShapeDtypeStruct((S, Rh, C), BF16),
        compiler_params=_cparams(("arbitrary", "arbitrary")))(core, own, got)


def adamw_half(name, ids, grad, w, m, v, layer, layers, into):
    own = isinstance(grad, tuple)
    Rh, C = grad[0].shape[1:] if own else grad.shape
    tr = _tile(Rh, 128, 16)
    nrb = Rh // tr
    n_grad = 4 if own else 1
    c1 = 1.0 / (1.0 - ADAM_B1 ** ADAM_STEP)
    c2 = 1.0 / (1.0 - ADAM_B2 ** ADAM_STEP)

    def body(ids_ref, *refs):
        g_refs, (w_ref, m_ref, v_ref) = refs[:n_grad], refs[n_grad:n_grad + 3]
        outs = refs[-5:] if own else refs[-4:]
        go_ref, d_ref, nm_ref, nv_ref = outs[:4]
        if own:
            gv = ((g_refs[0][...].astype(F32) + g_refs[1][...].astype(F32))
                  + (g_refs[2][...].astype(F32) + g_refs[3][...].astype(F32)))
            outs[4][...] = gv
        else:
            gv = g_refs[0][...]
        nm = ADAM_B1 * m_ref[...] + (1.0 - ADAM_B1) * gv
        nv = ADAM_B2 * v_ref[...] + (1.0 - ADAM_B2) * (gv * gv)
        go_ref[...] = gv
        d_ref[...] = -ADAM_LR * ((nm * c1) / (jnp.sqrt(nv * c2) + ADAM_EPS) + ADAM_WD * w_ref[...])
        nm_ref[...] = nm
        nv_ref[...] = nv

    half = (lambda ids: ids[0]) if own else (lambda ids: 1 - ids[0])
    rows = pl.BlockSpec((tr, C), lambda r, ids: ((2 * layer + half(ids)) * nrb + r, 0))
    plain = pl.BlockSpec((tr, C), lambda r, ids: (r, 0))
    if own:
        slot = lambda flip: pl.BlockSpec((None, tr, C), lambda r, ids: (ids[1] ^ flip, r, 0))
        in_specs = [slot(0), slot(1), slot(2), slot(3), rows, rows, rows]
        operands = [ids, grad[0], grad[1], grad[1], grad[1], w, m, v]
    else:
        in_specs = [plain, rows, rows, rows]
        operands = [ids, grad, w, m, v]
    aliases = {}
    if into is not None:
        in_specs += [pl.BlockSpec(memory_space=pl.ANY)] * 4
        aliases = {len(operands) + k: k for k in range(4)}
        operands += list(into)
    full = jax.ShapeDtypeStruct((layers * 2 * Rh, C), F32)
    gs = pltpu.PrefetchScalarGridSpec(num_scalar_prefetch=1, grid=(nrb,), in_specs=in_specs,
                                      out_specs=[rows] * 4 + [plain] * own)
    return pl.pallas_call(
        body, name=name, grid_spec=gs, out_shape=[full] * 4 + [jax.ShapeDtypeStruct((Rh, C), F32)] * own,
        input_output_aliases=aliases, compiler_params=_cparams(("arbitrary",)))(*operands)


def cast_into_slot(name, w, chip, layer, layers, after=None):
    R, C = w.shape[0] // layers, w.shape[1]
    tr = _tile(R, 512, 16)
    nrb = R // tr

    def body(s_ref, w_ref, *rest):
        rest[-1][...] = w_ref[...].astype(BF16)

    in_specs = [pl.BlockSpec((tr, C), lambda r, s: (layer * nrb + r, 0))]
    operands = [chip, w]
    if after is not None:
        in_specs.append(pl.BlockSpec(memory_space=pl.ANY))
        operands.append(after)
    gs = pltpu.PrefetchScalarGridSpec(
        num_scalar_prefetch=1, grid=(nrb,), in_specs=in_specs,
        out_specs=pl.BlockSpec((None, tr, C), lambda r, s: (s[0], r, 0)))
    return pl.pallas_call(
        body, name=name, grid_spec=gs, out_shape=jax.ShapeDtypeStruct((N_CHIPS, R, C), BF16),
        compiler_params=_cparams(("arbitrary",)))(*operands)


def adamw(name, w, g, m, v):
    R, C = w.shape
    tr = _tile(R, 256, 8)
    c1 = 1.0 / (1.0 - ADAM_B1 ** ADAM_STEP)
    c2 = 1.0 / (1.0 - ADAM_B2 ** ADAM_STEP)

    def body(w_ref, g_ref, m_ref, v_ref, d_ref, nm_ref, nv_ref):
        gv = g_ref[...]
        nm = ADAM_B1 * m_ref[...] + (1.0 - ADAM_B1) * gv
        nv = ADAM_B2 * v_ref[...] + (1.0 - ADAM_B2) * (gv * gv)
        d_ref[...] = -ADAM_LR * ((nm * c1) / (jnp.sqrt(nv * c2) + ADAM_EPS) + ADAM_WD * w_ref[...])
        nm_ref[...] = nm
        nv_ref[...] = nv

    spec = pl.BlockSpec((tr, C), lambda r: (r, 0))
    return pl.pallas_call(
        body, name=name, grid=(R // tr,), in_specs=[spec] * 4, out_specs=[spec] * 3,
        out_shape=[jax.ShapeDtypeStruct((R, C), F32)] * 3, compiler_params=_cparams(("arbitrary",)))(w, g, m, v)


def _matmul(name, a, b, *, grid, a_blk, a_map, b_blk, b_map, o_shape, o_dtype, o_blk, o_map, dims,
            scale=None, res=None, bias=None, bias_blk=None, bias_map=None, alias_into=None, after=None):
    nk = grid[2]
    has_res, has_bias, has_into = res is not None, bias is not None, alias_into is not None
    acc_shape = tuple(d for d in o_blk if d is not None)

    def body(*refs):
        a_ref, b_ref = refs[0], refs[1]
        pos = 2
        res_ref = bias_ref = None
        if has_res:
            res_ref = refs[pos]
            pos += 1
        if has_bias:
            bias_ref = refs[pos]
            pos += 1
        if has_into:
            pos += 1
        if after is not None:
            pos += 1
        o_ref = refs[pos]
        av, bv = a_ref[...], b_ref[...]
        if bv.ndim == 3:
            bv = bv.reshape(-1, bv.shape[-1])
        if av.dtype != BF16:
            av = av.astype(BF16)
        if bv.dtype != BF16:
            bv = bv.astype(BF16)
        part = lax.dot_general(av, bv, dims, preferred_element_type=F32)

        def finish(acc):
            if scale is not None:
                acc = acc * scale
            if has_bias:
                acc = acc + bias_ref[...]
            if has_res:
                acc = acc + res_ref[...]
            o_ref[...] = acc.astype(o_ref.dtype)

        if nk == 1:
            finish(part)
        else:
            acc_ref = refs[pos + 1]
            k = pl.program_id(2)

            @pl.when(k == 0)
            def _():
                acc_ref[...] = part

            @pl.when(k > 0)
            def _():
                acc_ref[...] += part

            @pl.when(k == nk - 1)
            def _():
                finish(acc_ref[...])

    operands = [a, b]
    in_specs = [pl.BlockSpec(a_blk, a_map), pl.BlockSpec(b_blk, b_map)]
    if has_res:
        operands.append(res)
        in_specs.append(pl.BlockSpec(o_blk, o_map))
    if has_bias:
        operands.append(bias)
        in_specs.append(pl.BlockSpec(bias_blk, bias_map))
    aliases = {}
    if has_into:
        aliases = {len(operands): 0}
        operands.append(alias_into)
        in_specs.append(pl.BlockSpec(memory_space=pl.ANY))
    if after is not None:
        operands.append(after)
        in_specs.append(pl.BlockSpec(memory_space=pl.ANY))
    return pl.pallas_call(
        body, name=name, grid=grid, in_specs=in_specs, out_specs=pl.BlockSpec(o_blk, o_map),
        out_shape=jax.ShapeDtypeStruct(o_shape, o_dtype),
        scratch_shapes=[pltpu.VMEM(acc_shape, F32)] if nk > 1 else [],
        input_output_aliases=aliases,
        compiler_params=_cparams(("parallel", "parallel", "arbitrary")),
    )(*operands)


def mm_nn(name, a, w3, kind, layer, *, out_dtype, scale=None, res=None, bias=None, tm=1024, tn=512, tk=2048,
          after=None, whole_k=False):
    M, K = a.shape
    S, _, C = w3.shape
    tm = _tile(M, tm, 16)
    b_blk = None
    if kind == "col":
        N = S * C
        tk, tn = _tile(K, tk), _tile(C, tn)
        kb, nb = K // tk, C // tn
        b_map = lambda n, m, k: (n // nb, layer * kb + k, n % nb)
    elif whole_k:
        N, K4, tk, tn = C, K // S, K, _tile(C, tn)
        b_blk, b_map = (S, K4, tn), lambda n, m, k: (0, layer, n)
    else:
        N, K4 = C, K // S
        tk, tn = _tile(K4, tk), _tile(C, tn)
        kb4 = K4 // tk
        b_map = lambda n, m, k: (k // kb4, layer * kb4 + k % kb4, n)
    return _matmul(
        name, a, w3, grid=(N // tn, M // tm, K // tk),
        a_blk=(tm, tk), a_map=lambda n, m, k: (m, k), b_blk=b_blk or (None, tk, tn), b_map=b_map,
        o_shape=(M, N), o_dtype=out_dtype, o_blk=(tm, tn), o_map=lambda n, m, k: (m, n), dims=NN_DIMS,
        scale=scale, res=res, bias=bias, bias_blk=(1, tn), bias_map=lambda n, m, k: (0, n), after=after)


def _mm_nt(name, g, w3, kind, layer, K, *, out_dtype, scale=None, tm=1024, to=1024, tc=1408, after=None):
    S, _, C = w3.shape
    tc = _tile(C, tc)
    if g.ndim == 3:
        M, N = g.shape[1], 2 * g.shape[2]
        tm = _tile(M, tm, 16)
        cb = g.shape[2] // tc
        a_blk, a_map = (None, tm, tc), lambda o, m, c: (c // cb, m, c % cb)
    else:
        M, N = g.shape
        tm = _tile(M, tm, 16)
        a_blk, a_map = (tm, tc), lambda o, m, c: (m, c)
    if kind == "col":
        nb = C // tc
        to = _tile(K, to)
        ob = K // to
        b_map = lambda o, m, c: (c // nb, layer * ob + o, c % nb)
    else:
        K4 = K // S
        to = _tile(K4, to)
        ob4 = K4 // to
        b_map = lambda o, m, c: (o // ob4, layer * ob4 + o % ob4, c)
    return _matmul(
        name, g, w3, grid=(K // to, M // tm, N // tc),
        a_blk=a_blk, a_map=a_map, b_blk=(None, to, tc), b_map=b_map,
        o_shape=(M, K), o_dtype=out_dtype, o_blk=(tm, to), o_map=lambda o, m, c: (m, o), dims=NT_DIMS,
        scale=scale, after=after)


def mm_tn(name, a, g, kind, *, shards=N_CHIPS, layer=0, layers=1, into=None, out_dtype=BF16, scale=None,
          tk=1024, tn=1408, tm=2048, after=None):
    M, K = a.shape
    N = 2 * g.shape[2] if g.ndim == 3 else g.shape[1]
    tm = _tile(M, tm, 16)
    if kind == "col":
        C = N // shards
        tk, tn = _tile(K, tk), _tile(C, tn)
        kb, nb = K // tk, C // tn
        o_shape = (shards, layers * K, C)
        o_map = lambda k, n, m: (n // nb, layer * kb + k, n % nb)
    else:
        K4 = K // shards
        tk, tn = _tile(K4, tk), _tile(N, tn)
        kb4 = K4 // tk
        o_shape = (shards, layers * K4, N)
        o_map = lambda k, n, m: (k // kb4, layer * kb4 + k % kb4, n)
    if g.ndim == 3:
        nbh = g.shape[2] // tn
        g_blk, g_map = (None, tm, tn), lambda k, n, m: (n // nbh, m, n % nbh)
    else:
        g_blk, g_map = (tm, tn), lambda k, n, m: (m, n)
    return _matmul(
        name, a, g, grid=(K // tk, N // tn, M // tm),
        a_blk=(tm, tk), a_map=lambda k, n, m: (m, k), b_blk=g_blk, b_map=g_map,
        o_shape=o_shape, o_dtype=out_dtype, o_blk=(None, tk, tn), o_map=o_map, dims=TN_DIMS,
        scale=scale, alias_into=into, after=after)


def ffn_in_act(name, hn, w3, *, tn=256, after=None):
    M, K = hn.shape
    S, _, C = w3.shape
    tn = _tile(C, tn)
    nb = C // tn

    def body(a_ref, bg_ref, bu_ref, *rest):
        gu_ref, act_ref = rest[-2:]
        a = a_ref[...]
        g = jnp.dot(a, bg_ref[...], preferred_element_type=F32)
        u = jnp.dot(a, bu_ref[...], preferred_element_type=F32)
        gu_ref[0] = g.astype(BF16)
        gu_ref[1] = u.astype(BF16)
        act_ref[...] = (g * _sigmoid(g) * u).astype(BF16)

    in_specs = [pl.BlockSpec((M, K), lambda j: (0, 0)),
                pl.BlockSpec((None, K, tn), lambda j: (j // nb, 0, j % nb)),
                pl.BlockSpec((None, K, tn), lambda j: (S // 2 + j // nb, 0, j % nb))]
    operands = [hn, w3, w3]
    if after is not None:
        in_specs.append(pl.BlockSpec(memory_space=pl.ANY))
        operands.append(after)
    width = S // 2 * C
    return pl.pallas_call(
        body, name=name, grid=(width // tn,), in_specs=in_specs,
        out_specs=[pl.BlockSpec((2, M, tn), lambda j: (0, 0, j)), pl.BlockSpec((M, tn), lambda j: (0, j))],
        out_shape=[jax.ShapeDtypeStruct((2, M, width), BF16), jax.ShapeDtypeStruct((M, width), BF16)],
        compiler_params=_cparams(("arbitrary",)))(*operands)


def ffn_dact_dgu(name, d_out, w3, gu, scale, *, tm=512, after=None):
    M, N = d_out.shape
    S, K4, _ = w3.shape
    tm = _tile(M, tm, 16)
    to = K4
    chunk = 256 if to > 256 else to

    def body(a_ref, b_ref, gu_ref, *rest):
        o_ref = rest[-1]
        a = a_ref[...]
        for c0 in range(0, to, chunk):
            cols = slice(c0, min(c0 + chunk, to))
            da = lax.dot_general(a, b_ref[cols, :], NT_DIMS, preferred_element_type=F32) * scale
            g = gu_ref[0, :, cols].astype(F32)
            u = gu_ref[1, :, cols].astype(F32)
            s = _sigmoid(g)
            o_ref[0, :, cols] = (da * u * s * (1.0 + g * (1.0 - s))).astype(BF16)
            o_ref[1, :, cols] = (da * g * s).astype(BF16)

    halves = pl.BlockSpec((2, tm, to), lambda o, m: (0, m, o))
    in_specs = [pl.BlockSpec((tm, N), lambda o, m: (m, 0)), pl.BlockSpec((None, to, N), lambda o, m: (o, 0, 0)), halves]
    operands = [d_out, w3, gu]
    if after is not None:
        in_specs.append(pl.BlockSpec(memory_space=pl.ANY))
        operands.append(after)
    return pl.pallas_call(
        body, name=name, grid=(S, M // tm), in_specs=in_specs, out_specs=halves,
        out_shape=jax.ShapeDtypeStruct((2, M, S * K4), BF16),
        compiler_params=_cparams(("parallel", "arbitrary")))(*operands)


def rope_tables(name, pos):
    T = pos.shape[0]
    half = D_ROPE // 2

    def body(p_ref, c_ref, s1_ref, s2_ref):
        lane = lax.broadcasted_iota(jnp.int32, (T, 128), 1)
        idx = (lane & (half - 1)).astype(F32)
        ang = p_ref[...] * jnp.exp(idx * (-2.0 * math.log(ROPE_THETA) / D_ROPE))
        cs, sn = jnp.cos(ang), jnp.sin(ang)
        c_ref[...] = jnp.where(lane < D_ROPE, cs, 0.0)
        s1_ref[...] = jnp.where(lane < half, -sn, 0.0)
        s2_ref[...] = jnp.where((lane >= half) & (lane < D_ROPE), sn, 0.0)

    return pl.pallas_call(body, name=name, out_shape=[jax.ShapeDtypeStruct((T, 128), F32)] * 3,
                          compiler_params=_cparams())(pos)


def _rope(v, cs, s1, s2):
    return v * cs + pltpu.roll(v, 128 - D_ROPE // 2, 1) * s1 + pltpu.roll(v, D_ROPE // 2, 1) * s2


def _rope_bwd(d, cs, s1, s2):
    return d * cs + pltpu.roll(d * s1, D_ROPE // 2, 1) + pltpu.roll(d * s2, 128 - D_ROPE // 2, 1)


def _head_rstd(n, r):
    ms = (jnp.sum(n * n, axis=-1, keepdims=True) + jnp.sum(r * r, axis=-1, keepdims=True)) * (1.0 / QK_DIM)
    return lax.rsqrt(ms + EPS)


def mla_prep_fwd(name, q_raw, kv_raw, lat, tabs, q_gain, k_gain, rope_col, tm=128):
    T = q_raw.shape[0]
    H = q_raw.shape[1] // HEAD_PAD
    tm = min(tm, T)
    rope_blk = rope_col // 128

    def body(q_ref, kv_ref, kr_ref, c_ref, s1_ref, s2_ref, qg_ref, kg_ref, Q_ref, K_ref, V_ref):
        cs, s1, s2 = c_ref[...], s1_ref[...], s2_ref[...]
        qg, kg = qg_ref[...], kg_ref[...]
        kr = kr_ref[...]
        for h in range(H):
            lo = HEAD_PAD * h
            n, r = q_ref[:, lo:lo + 128], q_ref[:, lo + 128:lo + 256]
            rs = _head_rstd(n, r)
            Q_ref[h, :, 0:128] = (n * rs * qg[:, :128]).astype(BF16)
            Q_ref[h, :, 128:256] = _rope(r * rs * qg[:, 128:], cs, s1, s2).astype(BF16)
            n = kv_ref[:, lo:lo + 128]
            rs = _head_rstd(n, kr)
            K_ref[h, :, 0:128] = (n * rs * kg[:, :128]).astype(BF16)
            K_ref[h, :, 128:256] = _rope(kr * rs * kg[:, 128:], cs, s1, s2).astype(BF16)
            V_ref[h] = kv_ref[:, lo + 128:lo + 256].astype(BF16)

    row = lambda w: pl.BlockSpec((tm, w), lambda i: (i, 0))
    vec = pl.BlockSpec((1, HEAD_PAD), lambda i: (0, 0))
    return pl.pallas_call(
        body, name=name, grid=(T // tm,),
        in_specs=[row(H * HEAD_PAD), row(H * HEAD_PAD), pl.BlockSpec((tm, 128), lambda i: (i, rope_blk)),
                  row(128), row(128), row(128), vec, vec],
        out_specs=[pl.BlockSpec((H, tm, HEAD_PAD), lambda i: (0, i, 0))] * 2 + [pl.BlockSpec((H, tm, D_V), lambda i: (0, i, 0))],
        out_shape=[jax.ShapeDtypeStruct((H, T, HEAD_PAD), BF16)] * 2 + [jax.ShapeDtypeStruct((H, T, D_V), BF16)],
        compiler_params=_cparams(("arbitrary",)),
    )(q_raw, kv_raw, lat, *tabs, q_gain, k_gain)


def mla_prep_bwd(name, dQ, dK, dV, q_raw, kv_raw, lat, tabs, q_gain, k_gain, rope_col, tm=128):
    T = q_raw.shape[0]
    H = q_raw.shape[1] // HEAD_PAD
    tm = min(tm, T)
    rope_blk = rope_col // 128

    def body(dQ_ref, dK_ref, dV_ref, q_ref, kv_ref, kr_ref, c_ref, s1_ref, s2_ref, qg_ref, kg_ref,
             dq_ref, dkv_ref, dkr_ref, dqg_ref, dkg_ref):
        @pl.when(pl.program_id(0) == 0)
        def _():
            dqg_ref[...] = jnp.zeros_like(dqg_ref)
            dkg_ref[...] = jnp.zeros_like(dkg_ref)

        cs, s1, s2 = c_ref[...], s1_ref[...], s2_ref[...]
        qg, kg = qg_ref[...], kg_ref[...]
        kr = kr_ref[...]
        dkr = jnp.zeros((tm, 128), F32)
        gq_n = jnp.zeros((1, 128), F32)
        gq_r = jnp.zeros((1, 128), F32)
        gk_n = jnp.zeros((1, 128), F32)
        gk_r = jnp.zeros((1, 128), F32)

        def norm_bwd(n, r, dn, dr, gain):
            rs = _head_rstd(n, r)
            nh, rh = n * rs, r * rs
            dng, drg = dn * gain[:, :128], dr * gain[:, 128:]
            mean = (jnp.sum(dng * nh, axis=-1, keepdims=True) + jnp.sum(drg * rh, axis=-1, keepdims=True)) * (1.0 / QK_DIM)
            return rs * (dng - nh * mean), rs * (drg - rh * mean), _colsum(dn * nh), _colsum(dr * rh)

        for h in range(H):
            lo = HEAD_PAD * h
            n, r = q_ref[:, lo:lo + 128], q_ref[:, lo + 128:lo + 256]
            dn = dQ_ref[h, :, 0:128].astype(F32)
            dr = _rope_bwd(dQ_ref[h, :, 128:256].astype(F32), cs, s1, s2)
            a, b, g1, g2 = norm_bwd(n, r, dn, dr, qg)
            dq_ref[:, lo:lo + 128] = a.astype(BF16)
            dq_ref[:, lo + 128:lo + 256] = b.astype(BF16)
            gq_n, gq_r = gq_n + g1, gq_r + g2
            n = kv_ref[:, lo:lo + 128]
            dn = dK_ref[h, :, 0:128].astype(F32)
            dr = _rope_bwd(dK_ref[h, :, 128:256].astype(F32), cs, s1, s2)
            a, b, g1, g2 = norm_bwd(n, kr, dn, dr, kg)
            dkv_ref[:, lo:lo + 128] = a.astype(BF16)
            dkv_ref[:, lo + 128:lo + 256] = dV_ref[h].astype(BF16)
            dkr = dkr + b
            gk_n, gk_r = gk_n + g1, gk_r + g2
        dkr_ref[...] = dkr
        dqg_ref[:, 0:128] += gq_n
        dqg_ref[:, 128:256] += gq_r
        dkg_ref[:, 0:128] += gk_n
        dkg_ref[:, 128:256] += gk_r

    row = lambda w: pl.BlockSpec((tm, w), lambda i: (i, 0))
    vec = pl.BlockSpec((1, HEAD_PAD), lambda i: (0, 0))
    hd = lambda w: pl.BlockSpec((H, tm, w), lambda i: (0, i, 0))
    return pl.pallas_call(
        body, name=name, grid=(T // tm,),
        in_specs=[hd(HEAD_PAD), hd(HEAD_PAD), hd(D_V), row(H * HEAD_PAD), row(H * HEAD_PAD),
                  pl.BlockSpec((tm, 128), lambda i: (i, rope_blk)), row(128), row(128), row(128), vec, vec],
        out_specs=[row(H * HEAD_PAD), row(H * HEAD_PAD), row(128), vec, vec],
        out_shape=[jax.ShapeDtypeStruct((T, H * HEAD_PAD), BF16)] * 2 + [jax.ShapeDtypeStruct((T, 128), F32)]
        + [jax.ShapeDtypeStruct((1, HEAD_PAD), F32)] * 2,
        compiler_params=_cparams(("arbitrary",)),
    )(dQ, dK, dV, q_raw, kv_raw, lat, *tabs, q_gain, k_gain)


def _causal_probs(q, k, scale, row0):
    s = lax.dot_general(q, k, NT_DIMS, preferred_element_type=F32) * scale
    row = row0 + lax.broadcasted_iota(jnp.int32, s.shape, 0)
    col = lax.broadcasted_iota(jnp.int32, s.shape, 1)
    s = jnp.where(col <= row, s, NEG_BIG)
    p = jnp.exp(s - jnp.max(s, axis=-1, keepdims=True))
    return p, jnp.sum(p, axis=-1, keepdims=True)


def attn_fwd(name, Q, K, V, tq=512):
    H, T, E = Q.shape
    tq = min(tq, T)
    nq = T // tq
    scale = QK_DIM ** -0.5

    def body(q_ref, k_ref, v_ref, o_ref):
        i = pl.program_id(1)
        for ib in range(nq):
            @pl.when(i == ib)
            def _():
                n = (ib + 1) * tq
                p, l = _causal_probs(q_ref[...], k_ref[0:n, :], scale, ib * tq)
                o = jnp.dot(p.astype(BF16), v_ref[0:n, :], preferred_element_type=F32)
                o_ref[...] = (o / l).astype(o_ref.dtype)

    return pl.pallas_call(
        body, name=name, grid=(H, nq),
        in_specs=[pl.BlockSpec((None, tq, E), lambda h, i: (h, i, 0)),
                  pl.BlockSpec((None, T, E), lambda h, i: (h, 0, 0)),
                  pl.BlockSpec((None, T, D_V), lambda h, i: (h, 0, 0))],
        out_specs=pl.BlockSpec((tq, D_V), lambda h, i: (i, h)),
        out_shape=jax.ShapeDtypeStruct((T, H * D_V), BF16),
        compiler_params=_cparams(("parallel", "arbitrary")),
    )(Q, K, V)


def attn_bwd(name, Q, K, V, dO, tq=512):
    H, T, E = Q.shape
    tq = min(tq, T)
    nq = T // tq
    scale = QK_DIM ** -0.5

    def body(q_ref, k_ref, v_ref, do_ref, dq_ref, dk_ref, dv_ref):
        i = pl.program_id(1)

        @pl.when(i == 0)
        def _():
            dk_ref[...] = jnp.zeros_like(dk_ref)
            dv_ref[...] = jnp.zeros_like(dv_ref)

        for ib in range(nq):
            @pl.when(i == ib)
            def _():
                n = (ib + 1) * tq
                q, k, v, do = q_ref[...], k_ref[0:n, :], v_ref[0:n, :], do_ref[...]
                p, l = _causal_probs(q, k, scale, ib * tq)
                p = p / l
                dp = lax.dot_general(do, v, NT_DIMS, preferred_element_type=F32)
                ds = p * (dp - jnp.sum(p * dp, axis=-1, keepdims=True)) * scale
                dsb, pb = ds.astype(BF16), p.astype(BF16)
                dq_ref[...] = jnp.dot(dsb, k, preferred_element_type=F32)
                dk_ref[0:n, :] += lax.dot_general(dsb, q, TN_DIMS, preferred_element_type=F32)
                dv_ref[0:n, :] += lax.dot_general(pb, do, TN_DIMS, preferred_element_type=F32)

    return pl.pallas_call(
        body, name=name, grid=(H, nq),
        in_specs=[pl.BlockSpec((None, tq, E), lambda h, i: (h, i, 0)),
                  pl.BlockSpec((None, T, E), lambda h, i: (h, 0, 0)),
                  pl.BlockSpec((None, T, D_V), lambda h, i: (h, 0, 0)),
                  pl.BlockSpec((tq, D_V), lambda h, i: (i, h))],
        out_specs=[pl.BlockSpec((None, tq, E), lambda h, i: (h, i, 0)),
                   pl.BlockSpec((None, T, E), lambda h, i: (h, 0, 0)),
                   pl.BlockSpec((None, T, D_V), lambda h, i: (h, 0, 0))],
        out_shape=[jax.ShapeDtypeStruct((H, T, E), F32)] * 2 + [jax.ShapeDtypeStruct((H, T, D_V), F32)],
        compiler_params=_cparams(("parallel", "arbitrary")),
    )(Q, K, V, dO)


def _taps_by_residue(offsets):
    groups = {}
    for k, off in enumerate(offsets):
        groups.setdefault(off % 8, []).append((k, off - off % 8))
    return groups


def _dw_specs(T, C, tm, tc, halo):
    cur = pl.BlockSpec((tm, tc), lambda j, i: (i, j))
    last = T // tm - 1
    if halo == "prev":
        nbr = pl.BlockSpec((tm, tc), lambda j, i: (jnp.maximum(i - 1, 0), j))
    else:
        nbr = pl.BlockSpec((tm, tc), lambda j, i: (jnp.minimum(i + 1, last), j))
    return cur, nbr


def dwconv_fwd(name, u, w, b, tm=256, tc=512, rs=32):
    T, C = u.shape
    tm, tc = min(tm, T), min(tc, C)
    cur, prev = _dw_specs(T, C, tm, tc, "prev")

    def body(up_ref, uc_ref, w_ref, b_ref, o_ref, scr):
        i = pl.program_id(1)

        @pl.when(i == 0)
        def _():
            scr[pl.ds(0, tm), :] = jnp.zeros((tm, tc), F32)

        @pl.when(i > 0)
        def _():
            scr[pl.ds(0, tm), :] = up_ref[...]

        scr[pl.ds(tm, tm), :] = uc_ref[...]
        for s in range(tm // rs):
            acc = jnp.broadcast_to(b_ref[...], (rs, tc))
            for k in range(CONV_WIDTH):
                acc = acc + w_ref[pl.ds(k, 1), :] * scr[pl.ds(tm - (CONV_WIDTH - 1) + k + rs * s, rs), :]
            o_ref[pl.ds(rs * s, rs), :] = acc

    return pl.pallas_call(
        body, name=name, grid=(C // tc, T // tm),
        in_specs=[prev, cur, pl.BlockSpec((CONV_TAPS_PAD, tc), lambda j, i: (0, j)), pl.BlockSpec((1, tc), lambda j, i: (0, j))],
        out_specs=cur, out_shape=jax.ShapeDtypeStruct((T, C), F32),
        scratch_shapes=[pltpu.VMEM((2 * tm, tc), F32)], compiler_params=_cparams(("parallel", "arbitrary")),
    )(u, u, w, b)


def dwconv_bwd_u(name, dy, w, tm=256, tc=512, rs=32):
    T, C = dy.shape
    tm, tc = min(tm, T), min(tc, C)
    cur, nxt = _dw_specs(T, C, tm, tc, "next")
    last = T // tm - 1

    taps = _taps_by_residue([(CONV_WIDTH - 1) - k for k in range(CONV_WIDTH)])

    def body(dc_ref, dn_ref, w_ref, o_ref, scr, tmp):
        i = pl.program_id(1)
        scr[pl.ds(0, tm), :] = dc_ref[...]
        scr[pl.ds(2 * tm, 8), :] = jnp.zeros((8, tc), F32)

        @pl.when(i == last)
        def _():
            scr[pl.ds(tm, tm), :] = jnp.zeros((tm, tc), F32)

        @pl.when(i < last)
        def _():
            scr[pl.ds(tm, tm), :] = dn_ref[...]

        for s in range(tm // rs):
            acc = jnp.zeros((rs, tc), F32)
            for r, group in taps.items():
                a = jnp.zeros((rs + 8, tc), F32)
                for k, base in group:
                    a = a + w_ref[pl.ds(k, 1), :] * scr[pl.ds(base + rs * s, rs + 8), :]
                if r == 0:
                    acc = acc + a[:rs]
                else:
                    tmp[...] = a
                    acc = acc + tmp[pl.ds(r, rs), :]
            o_ref[pl.ds(rs * s, rs), :] = acc

    return pl.pallas_call(
        body, name=name, grid=(C // tc, T // tm),
        in_specs=[cur, nxt, pl.BlockSpec((CONV_TAPS_PAD, tc), lambda j, i: (0, j))],
        out_specs=cur, out_shape=jax.ShapeDtypeStruct((T, C), F32),
        scratch_shapes=[pltpu.VMEM((2 * tm + 8, tc), F32), pltpu.VMEM((rs + 8, tc), F32)],
        compiler_params=_cparams(("parallel", "arbitrary")),
    )(dy, dy, w)


def dwconv_bwd_w(name, u, dy, tm=256, tc=512, rs=32):
    T, C = u.shape
    tm, tc = min(tm, T), min(tc, C)
    cur, prev = _dw_specs(T, C, tm, tc, "prev")

    taps = _taps_by_residue([tm - (CONV_WIDTH - 1) + k for k in range(CONV_WIDTH)])
    span = tm + 8

    def body(up_ref, uc_ref, dy_ref, o_ref, scr, dyp, dys):
        i = pl.program_id(1)

        @pl.when(i == 0)
        def _():
            scr[pl.ds(0, tm), :] = jnp.zeros((tm, tc), F32)
            o_ref[...] = jnp.zeros_like(o_ref)

        @pl.when(i > 0)
        def _():
            scr[pl.ds(0, tm), :] = up_ref[...]

        scr[pl.ds(tm, tm), :] = uc_ref[...]
        scr[pl.ds(2 * tm, 8), :] = jnp.zeros((8, tc), F32)
        dyp[pl.ds(0, 8), :] = jnp.zeros((8, tc), F32)
        dyp[pl.ds(8, tm), :] = dy_ref[...]
        dyp[pl.ds(8 + tm, 8), :] = jnp.zeros((8, tc), F32)
        for r in taps:
            dys[r] = dyp[pl.ds(8 - r, span), :]
        for r, group in taps.items():
            for k, base in group:
                acc = jnp.zeros((8, tc), F32)
                for s in range(span // 8):
                    acc = acc + dys[r, pl.ds(8 * s, 8), :] * scr[pl.ds(base + 8 * s, 8), :]
                o_ref[pl.ds(k, 1), :] += _colsum(acc)

    return pl.pallas_call(
        body, name=name, grid=(C // tc, T // tm),
        in_specs=[prev, cur, cur],
        out_specs=pl.BlockSpec((CONV_TAPS_PAD, tc), lambda j, i: (0, j)),
        out_shape=jax.ShapeDtypeStruct((CONV_TAPS_PAD, C), F32),
        scratch_shapes=[pltpu.VMEM((2 * tm + 8, tc), F32), pltpu.VMEM((tm + 16, tc), F32),
                        pltpu.VMEM((8, span, tc), F32)],
        compiler_params=_cparams(("parallel", "arbitrary")),
    )(u, u, dy)


def _place():
    x, y, c = lax.axis_index("x"), lax.axis_index("y"), lax.axis_index("c")
    return x, y, c


def _other_chips(x, y):
    return [(1 - x, y, 2 * (1 - x) + y), (x, 1 - y, 2 * x + (1 - y)), (1 - x, 1 - y, 2 * (1 - x) + (1 - y))]


def _hbm_specs(n):
    return [pl.BlockSpec(memory_space=pl.ANY)] * n


HBM_SPEC = pl.BlockSpec(memory_space=pltpu.HBM)
SEM_SPEC = pl.BlockSpec(memory_space=pltpu.SEMAPHORE)
ANY_SPEC = pl.BlockSpec(memory_space=pl.ANY)
SIDE_EFFECT = pltpu.SideEffectType.DATAFLOW_SIDE_EFFECTING


def _half(ref, slot, which):
    rh = ref.shape[1] // 2
    return ref.at[slot, pl.ds(pl.multiple_of(which * rh, 16), rh), :]


def _hbm(a):
    return pltpu.with_memory_space_constraint(a, pltpu.HBM)


def gather_start(name, groups, after):
    flat = [b for g in groups for b in g]
    n, ng = len(flat), len(groups)

    def body(*refs):
        send, recv = refs[n + 1:n + 1 + ng], refs[n + 1 + ng:n + 1 + 2 * ng]
        out, token = refs[n + 1 + 2 * ng:2 * n + 1 + 2 * ng], refs[2 * n + 1 + 2 * ng]
        token[...] = jnp.zeros_like(token)
        x, y, c = _place()
        me = 2 * x + y
        a = 0
        for g, grp in enumerate(groups):
            for k in range(len(grp)):
                piece = _half(out[a], me, c)
                for j, (px, py, _) in enumerate(_other_chips(x, y)):
                    pltpu.make_async_remote_copy(
                        src_ref=piece, dst_ref=piece, send_sem=send[g].at[3 * k + j], recv_sem=recv[g].at[3 * k + j],
                        device_id=(px, py, c), device_id_type=MESH).start()
                a += 1

    sems = [pltpu.SemaphoreType.DMA((3 * len(g),)) for g in groups]
    res = pl.pallas_call(
        body, name=name, in_specs=[HBM_SPEC] * n + [ANY_SPEC],
        out_specs=[SEM_SPEC] * (2 * ng) + [HBM_SPEC] * n + [pl.BlockSpec(memory_space=pltpu.VMEM)],
        out_shape=sems + sems + [pltpu.HBM(b.shape, b.dtype) for b in flat] + [jax.ShapeDtypeStruct((8, 128), F32)],
        input_output_aliases={a: 2 * ng + a for a in range(n)},
        compiler_params=pltpu.CompilerParams(has_side_effects=SIDE_EFFECT),
    )(*[_hbm(b) for b in flat], after)
    send, recv, bufs = res[:ng], res[ng:2 * ng], list(res[2 * ng:2 * ng + n])
    out, a = [], 0
    for g, grp in enumerate(groups):
        out.append((send[g], recv[g], bufs[a:a + len(grp)]))
        a += len(grp)
    return out, res[2 * ng + n]


def gather_relay(name, started, after):
    send1, recv1, bufs = started
    n = len(bufs)

    def body(*refs):
        s1, r1 = refs[n], refs[n + 1]
        s2, r2, out, token = refs[n + 3], refs[n + 4], refs[n + 5:2 * n + 5], refs[2 * n + 5]
        x, y, c = _place()
        me = 2 * x + y
        chips = _other_chips(x, y)
        for k in range(n):
            for j, (px, py, idx) in enumerate(chips):
                cp = pltpu.make_async_remote_copy(
                    src_ref=_half(out[k], me, c), dst_ref=_half(out[k], idx, c), send_sem=s1.at[3 * k + j],
                    recv_sem=r1.at[3 * k + j], device_id=(px, py, c), device_id_type=MESH)
                cp.wait_send()
                cp.wait_recv()
        for k in range(n):
            for j, (px, py, idx) in enumerate(chips):
                piece = _half(out[k], idx, c)
                pltpu.make_async_remote_copy(
                    src_ref=piece, dst_ref=piece, send_sem=s2.at[3 * k + j], recv_sem=r2.at[3 * k + j],
                    device_id=(x, y, 1 - c), device_id_type=MESH).start()
        token[...] = jnp.zeros_like(token)

    sem = pltpu.SemaphoreType.DMA((3 * n,))
    res = pl.pallas_call(
        body, name=name, in_specs=[HBM_SPEC] * n + [SEM_SPEC, SEM_SPEC, ANY_SPEC],
        out_specs=[SEM_SPEC, SEM_SPEC] + [HBM_SPEC] * n + [pl.BlockSpec(memory_space=pltpu.VMEM)],
        out_shape=[sem, sem] + [pltpu.HBM(b.shape, b.dtype) for b in bufs] + [jax.ShapeDtypeStruct((8, 128), F32)],
        input_output_aliases={a: 2 + a for a in range(n)},
        compiler_params=pltpu.CompilerParams(has_side_effects=SIDE_EFFECT),
    )(*bufs, send1, recv1, after)
    return res[0], res[1], list(res[2:2 + n]), res[2 + n]


def gather_wait(name, relayed, after):
    send2, recv2, bufs, _ = relayed
    n = len(bufs)

    def body(*refs):
        s2, r2, out = refs[n], refs[n + 1], refs[n + 3:]
        x, y, c = _place()
        for k in range(n):
            for j, (px, py, idx) in enumerate(_other_chips(x, y)):
                cp = pltpu.make_async_remote_copy(
                    src_ref=_half(out[k], idx, c), dst_ref=_half(out[k], idx, 1 - c), send_sem=s2.at[3 * k + j],
                    recv_sem=r2.at[3 * k + j], device_id=(x, y, 1 - c), device_id_type=MESH)
                cp.wait_send()
                cp.wait_recv()

    res = pl.pallas_call(
        body, name=name, in_specs=[HBM_SPEC] * n + [SEM_SPEC, SEM_SPEC, ANY_SPEC], out_specs=[HBM_SPEC] * n,
        out_shape=[pltpu.HBM(b.shape, b.dtype) for b in bufs], input_output_aliases={a: a for a in range(n)},
        compiler_params=pltpu.CompilerParams(has_side_effects=SIDE_EFFECT),
    )(*bufs, send2, recv2, after)
    return list(res)


def split_start(name, parts, after):
    bufs = [b for p in parts for b in p[0]]
    land_shapes = [s for p in parts for s in p[1]]
    nb, nl = len(bufs), len(land_shapes)

    def body(*refs):
        send, recv = refs[nb + 1], refs[nb + 2]
        out, lands, token = refs[nb + 3:2 * nb + 3], refs[2 * nb + 3:2 * nb + 3 + nl], refs[2 * nb + 3 + nl]
        x, y, c = _place()
        b0 = l0 = k0 = 0
        for p_bufs, p_lands, n_copies, plan in parts:
            copies = plan(out[b0:b0 + len(p_bufs)], lands[l0:l0 + len(p_lands)], x, y, c)
            for k, (src, dst, to, _) in enumerate(copies):
                pltpu.make_async_remote_copy(src_ref=src, dst_ref=dst, send_sem=send.at[k0 + k],
                                             recv_sem=recv.at[k0 + k], device_id=to, device_id_type=MESH).start()
            b0, l0, k0 = b0 + len(p_bufs), l0 + len(p_lands), k0 + n_copies
        token[...] = jnp.zeros_like(token)

    sem = pltpu.SemaphoreType.DMA((sum(p[2] for p in parts),))
    res = pl.pallas_call(
        body, name=name, in_specs=[HBM_SPEC] * nb + [ANY_SPEC],
        out_specs=[SEM_SPEC, SEM_SPEC] + [HBM_SPEC] * (nb + nl) + [pl.BlockSpec(memory_space=pltpu.VMEM)],
        out_shape=[sem, sem] + [pltpu.HBM(b.shape, b.dtype) for b in bufs]
        + [pltpu.HBM(s, d) for s, d in land_shapes] + [jax.ShapeDtypeStruct((8, 128), F32)],
        input_output_aliases={a: 2 + a for a in range(nb)},
        compiler_params=pltpu.CompilerParams(has_side_effects=SIDE_EFFECT),
    )(*[_hbm(b) for b in bufs], after)
    out_bufs, out_lands, token = list(res[2:2 + nb]), list(res[2 + nb:2 + nb + nl]), res[2 + nb + nl]
    states, b0, l0, k0 = [], 0, 0, 0
    for p_bufs, p_lands, n_copies, _ in parts:
        states.append((res[0], res[1], out_bufs[b0:b0 + len(p_bufs)], out_lands[l0:l0 + len(p_lands)], token, k0))
        b0, l0, k0 = b0 + len(p_bufs), l0 + len(p_lands), k0 + n_copies
    return states


def split_wait(name, started, plan, after):
    send, recv, bufs, lands, _, k0 = started
    nb, nl = len(bufs), len(lands)

    def body(*refs):
        s, r = refs[nb + nl], refs[nb + nl + 1]
        out, lo = refs[nb + nl + 3:2 * nb + nl + 3], refs[2 * nb + nl + 3:]
        x, y, c = _place()
        for k, (src, _, to, landed) in enumerate(plan(out, lo, x, y, c)):
            cp = pltpu.make_async_remote_copy(src_ref=src, dst_ref=landed, send_sem=s.at[k0 + k],
                                              recv_sem=r.at[k0 + k], device_id=to, device_id_type=MESH)
            cp.wait_send()
            cp.wait_recv()

    res = pl.pallas_call(
        body, name=name, in_specs=[HBM_SPEC] * (nb + nl) + [SEM_SPEC, SEM_SPEC, ANY_SPEC],
        out_specs=[HBM_SPEC] * (nb + nl), out_shape=[pltpu.HBM(b.shape, b.dtype) for b in bufs + lands],
        input_output_aliases={a: a for a in range(nb + nl)},
        compiler_params=pltpu.CompilerParams(has_side_effects=SIDE_EFFECT),
    )(*bufs, *lands, send, recv, after)
    return list(res[:nb]), list(res[nb:])


def swap_halves_plan(parts, lands, x, y, c):
    out = []
    for a in range(len(parts)):
        rh = parts[a].shape[1] // 2
        theirs = parts[a].at[:, pl.ds(pl.multiple_of((1 - c) * rh, 16), rh), :]
        out.append((theirs, lands[a], (x, y, 1 - c), lands[a]))
    return out


def chip_exchange_plan(sums, lands, x, y, c):
    me = 2 * x + y
    out = []
    for a in range(len(sums)):
        for px, py, idx in _other_chips(x, y):
            out.append((sums[a].at[idx], lands[a].at[me], (px, py, c), lands[a].at[idx]))
    return out


def share_grad_plan(halves, lands, x, y, c):
    return [(halves[a], lands[a], (x, y, 1 - c), lands[a]) for a in range(len(halves))]


def allreduce_pack(name, pack):
    R, W = pack.shape

    def body(p_ref, o_ref, sib, pair, got, send_sems, recv_sems):
        x, y, c = _place()

        def swap(k, src, dst, to):
            cp = pltpu.make_async_remote_copy(src_ref=src, dst_ref=dst, send_sem=send_sems.at[k],
                                              recv_sem=recv_sems.at[k], device_id=to, device_id_type=MESH)
            cp.start()
            return cp

        cp = swap(0, p_ref, sib, (x, y, 1 - c))
        cp.wait()
        pair[...] = p_ref[...] + sib[...]
        cps = [swap(1, pair, got.at[0], (1 - x, y, c)), swap(2, pair, got.at[1], (x, 1 - y, c)),
               swap(3, pair, got.at[2], (1 - x, 1 - y, c))]
        for cp in cps:
            cp.wait()
        o_ref[...] = (pair[...] + got[1]) + (got[0] + got[2])

    return pl.pallas_call(
        body, name=name, out_shape=jax.ShapeDtypeStruct((R, W), F32),
        in_specs=[pl.BlockSpec(memory_space=pltpu.VMEM)], out_specs=pl.BlockSpec(memory_space=pltpu.VMEM),
        scratch_shapes=[pltpu.VMEM((R, W), F32), pltpu.VMEM((R, W), F32), pltpu.VMEM((3, R, W), F32),
                        pltpu.SemaphoreType.DMA((4,)), pltpu.SemaphoreType.DMA((4,))],
        compiler_params=_cparams(),
    )(pack)


BIG_WEIGHTS = [
    ("ffn_a_w_in", "col"), ("ffn_a_w_out", "row"), ("ffn_b_w_in", "col"), ("ffn_b_w_out", "row"),
    ("mla_w_in", "row"), ("mla_w_uq", "col"), ("mla_w_ukv", "col"), ("mla_w_o", "row"),
    ("conv_w_pw1", "col"), ("conv_w_pw2", "row"), ("ple_w_proj", "col"), ("ple_w_gate", "row"),
]
WEIGHT_ORDER = ["ffn_a_norm", "ffn_a_w_in", "ffn_a_w_out", "ffn_b_norm", "ffn_b_w_in", "ffn_b_w_out", "mix_norm",
                "mla_w_in", "mla_q_lat_norm", "mla_kv_lat_norm", "mla_w_uq", "mla_w_ukv", "mla_q_gain", "mla_k_gain",
                "mla_w_o", "conv_w_pw1", "conv_b_pw1", "conv_w_dw", "conv_b_dw", "conv_ln_g", "conv_ln_b", "conv_w_pw2",
                "ple_w_proj", "ple_norm", "ple_gate_norm", "ple_w_gate"]
REPLICATED_SMALL = ["ffn_a_norm", "ffn_b_norm", "mix_norm", "ple_norm", "ple_gate_norm",
                    "mla_q_lat_norm", "mla_kv_lat_norm", "mla_q_gain", "mla_k_gain"]
SHARDED_SMALL = ["conv_b_pw1", "conv_w_dw", "conv_b_dw", "conv_ln_g", "conv_ln_b"]
PACK_ROWS = 8


def _pack_rows(arrs, width):
    out = []
    for a in arrs:
        r = -(-a.shape[0] // PACK_ROWS) * PACK_ROWS
        out.append(jnp.pad(a, ((0, r - a.shape[0]), (0, width - a.shape[1]))))
    return jnp.concatenate(out, axis=0)


def _unpack_rows(pack, shapes):
    out, r0 = [], 0
    for (r, w) in shapes:
        out.append(pack[r0:r0 + r, :w])
        r0 += -(-r // PACK_ROWS) * PACK_ROWS
    return out


def kernel(x, p, positions, ffn_a_norm, ffn_a_w_in, ffn_a_w_out, ffn_b_norm, ffn_b_w_in, ffn_b_w_out, mix_norm, mla_w_in, mla_q_lat_norm, mla_kv_lat_norm, mla_w_uq, mla_w_ukv, mla_q_gain, mla_k_gain, mla_w_o, conv_w_pw1, conv_b_pw1, conv_w_dw, conv_b_dw, conv_ln_g, conv_ln_b, conv_w_pw2, ple_w_proj, ple_norm, ple_gate_norm, ple_w_gate, loss_target, m_ffn_a_norm, m_ffn_a_w_in, m_ffn_a_w_out, m_ffn_b_norm, m_ffn_b_w_in, m_ffn_b_w_out, m_mix_norm, m_mla_w_in, m_mla_q_lat_norm, m_mla_kv_lat_norm, m_mla_w_uq, m_mla_w_ukv, m_mla_q_gain, m_mla_k_gain, m_mla_w_o, m_conv_w_pw1, m_conv_b_pw1, m_conv_w_dw, m_conv_b_dw, m_conv_ln_g, m_conv_ln_b, m_conv_w_pw2, m_ple_w_proj, m_ple_norm, m_ple_gate_norm, m_ple_w_gate, v_ffn_a_norm, v_ffn_a_w_in, v_ffn_a_w_out, v_ffn_b_norm, v_ffn_b_w_in, v_ffn_b_w_out, v_mix_norm, v_mla_w_in, v_mla_q_lat_norm, v_mla_kv_lat_norm, v_mla_w_uq, v_mla_w_ukv, v_mla_q_gain, v_mla_k_gain, v_mla_w_o, v_conv_w_pw1, v_conv_b_pw1, v_conv_w_dw, v_conv_b_dw, v_conv_ln_g, v_conv_ln_b, v_conv_w_pw2, v_ple_w_proj, v_ple_norm, v_ple_gate_norm, v_ple_w_gate):
    args = dict(locals())
    W = {n: args[n] for n in WEIGHT_ORDER}
    M1 = {n: args["m_" + n] for n in WEIGHT_ORDER}
    V2 = {n: args["v_" + n] for n in WEIGHT_ORDER}

    T, D = x.shape[1], x.shape[2]
    depth = ffn_a_norm.shape[0]
    H = mla_w_ukv.shape[2] * N_CHIPS // (D_NOPE + D_V)
    QL, KL = mla_q_lat_norm.shape[1], mla_kv_lat_norm.shape[1]
    C = conv_w_pw2.shape[1] * N_CHIPS
    lat_w = QL + KL + D_ROPE
    lat_pad = QL + KL + 128

    cx, cy, cc = lax.axis_index("x"), lax.axis_index("y"), lax.axis_index("c")
    chip = (2 * cx + cy).astype(jnp.int32)
    chip_arr = chip.reshape(1)
    core_arr = cc.astype(jnp.int32).reshape(1)

    def stage_groups(i):
        mix = ([("mla_w_in", i // 2), ("mla_w_uq", i // 2), ("mla_w_ukv", i // 2), ("mla_w_o", i // 2)] if i % 2 == 0
               else [("conv_w_pw1", i // 2), ("conv_w_pw2", i // 2)])
        return [[("ffn_a_w_in", i)], [("ffn_a_w_out", i)], mix, [("ffn_b_w_in", i)], [("ffn_b_w_out", i)],
                [("ple_w_proj", i), ("ple_w_gate", i)]]

    groups = [g for i in range(depth) for g in stage_groups(i)]
    A_IN, A_OUT, MIX, B_IN, B_OUT, PLE, PER_LAYER = 0, 1, 2, 3, 4, 5, 6

    def slot_of(key, after=None):
        n, l = key
        return cast_into_slot(f"cast_{n}_{l}", W[n].reshape(-1, W[n].shape[-1]), chip_arr, l, W[n].shape[0], after)

    def placed(a, width):
        full = jnp.zeros(a.shape[:-1] + (width,), F32)
        full = lax.dynamic_update_slice_in_dim(full, a, chip * a.shape[-1], axis=a.ndim - 1)
        return full * (cc == 0).astype(F32)

    b_pw1_sh = conv_b_pw1.reshape(1, -1)
    small_in = [placed(b_pw1_sh, 2 * C).reshape(2, C), placed(conv_w_dw[0], C), placed(conv_b_dw, C),
                placed(conv_ln_g, C), placed(conv_ln_b, C)]
    small_pack = allreduce_pack("gather_small", _pack_rows(small_in, C))
    small_full = _unpack_rows(small_pack, [(2, C), (CONV_WIDTH, C), (1, C), (1, C), (1, C)])

    FIRST = 2
    started, tok0 = gather_start("gather_start_first", [[slot_of(k) for k in g] for g in groups[:FIRST]], small_pack)
    rest, tok_rest = gather_start("gather_start_rest", [[slot_of(k, tok0) for k in g] for g in groups[FIRST:]], tok0)
    started = started + rest
    relayed, G = {}, {}

    def relay(g, after):
        if g >= len(groups):
            return None
        relayed[g] = gather_relay(f"gather_relay_{g}", started[g], after)
        return relayed[g][3]

    def ready(g, after):
        if g not in relayed:
            relay(g, after)
        for key, buf in zip(groups[g], gather_wait(f"gather_wait_{g}", relayed[g], after)):
            G[key] = buf

    relay(0, tok_rest)

    b_pw1_full = small_full[0].reshape(1, 2 * C)
    w_dw_full = jnp.pad(small_full[1], ((0, CONV_TAPS_PAD - CONV_WIDTH), (0, 0)))
    b_dw_full, ln_g_full, ln_b_full = small_full[2], small_full[3], small_full[4]

    pad_gain = lambda g: jnp.pad(g, ((0, 0), (0, HEAD_PAD - QK_DIM)))
    q_gain_p, k_gain_p = pad_gain(mla_q_gain), pad_gain(mla_k_gain)
    tabs = rope_tables("rope_tables", positions.reshape(T, 1).astype(F32))

    def ffn_fwd(tag, h, norm, w_in, w_out, layer, g_in):
        hn = rms_fwd(f"{tag}_rms", h, norm)
        ready(g_in, hn)
        tok = relay(g_in + 1, hn) if g_in > 0 else None
        gu, act = ffn_in_act(f"{tag}_in", hn, G[(w_in, layer)], after=tok)
        ready(g_in + 1, act)
        tok = relay(g_in + 2, act)
        out = mm_nn(f"{tag}_out", act, G[(w_out, layer)], "row", 0, out_dtype=F32, scale=FFN_RESIDUAL_WEIGHT, res=h,
                    tm=1024, tn=512, whole_k=True, after=tok)
        return out, (h, hn, gu, act)

    saved = []
    h = x[0]
    for i in range(depth):
        L = {}
        g0 = PER_LAYER * i
        h, L["ffn_a"] = ffn_fwd(f"l{i}_ffa", h, ffn_a_norm[i:i + 1], "ffn_a_w_in", "ffn_a_w_out", i, g0 + A_IN)
        L["h1"] = h
        hn = rms_fwd(f"l{i}_mix_rms", h, mix_norm[i:i + 1])
        L["hn_m"] = hn
        ready(g0 + MIX, hn)
        j = i // 2
        if i % 2 == 0:
            w_in_pad = jnp.pad(G[("mla_w_in", j)].reshape(D, lat_w), ((0, 0), (0, lat_pad - lat_w)))[None]
            uq = G[("mla_w_uq", j)].transpose(1, 0, 2).reshape(QL, H, QK_DIM)
            w_uq_pad = jnp.pad(uq, ((0, 0), (0, 0), (0, HEAD_PAD - QK_DIM))).reshape(1, QL, H * HEAD_PAD)
            lat = mm_nn(f"l{i}_lat", hn, w_in_pad, "row", 0, out_dtype=F32, tm=2048)
            cq, ckv = lat_norm_fwd(f"l{i}_latnorm", lat, mla_q_lat_norm[j:j + 1], mla_kv_lat_norm[j:j + 1])
            q_raw = mm_nn(f"l{i}_uq", cq, w_uq_pad, "row", 0, out_dtype=F32, tm=2048)
            kv_raw = mm_nn(f"l{i}_ukv", ckv, G[("mla_w_ukv", j)], "col", 0, out_dtype=F32, tm=2048)
            Qh, Kh, Vh = mla_prep_fwd(f"l{i}_prep", q_raw, kv_raw, lat, tabs, q_gain_p, k_gain_p, QL + KL)
            O = attn_fwd(f"l{i}_attn", Qh, Kh, Vh)
            tok = relay(g0 + B_IN, O)
            h = mm_nn(f"l{i}_wo", O, G[("mla_w_o", j)], "row", 0, out_dtype=F32, res=h, tm=1024, tn=1024, after=tok)
            L["mla"] = (lat, cq, ckv, q_raw, kv_raw, Qh, Kh, Vh, O, w_in_pad, w_uq_pad)
        else:
            ag = mm_nn(f"l{i}_pw1", hn, G[("conv_w_pw1", j)], "col", 0, out_dtype=F32, bias=b_pw1_full, tm=2048)
            u = glu_fwd(f"l{i}_glu", ag)
            yc = dwconv_fwd(f"l{i}_dw", u, w_dw_full, b_dw_full)
            cact = ln_silu_fwd(f"l{i}_ln", yc, ln_g_full, ln_b_full)
            tok = relay(g0 + B_IN, cact)
            h = mm_nn(f"l{i}_pw2", cact, G[("conv_w_pw2", j)], "row", 0, out_dtype=F32, res=h, tm=1024, tn=1024,
                      after=tok)
            L["conv"] = (ag, u, yc, cact)
        L["h2"] = h
        h, L["ffn_b"] = ffn_fwd(f"l{i}_ffb", h, ffn_b_norm[i:i + 1], "ffn_b_w_in", "ffn_b_w_out", i, g0 + B_IN)
        L["h3"] = h
        ready(g0 + PLE, h)
        pe = mm_nn(f"l{i}_ple_proj", p[i, 0], G[("ple_w_proj", i)], "col", 0, out_dtype=F32, tm=2048)
        hg = rms_fwd(f"l{i}_gate_rms", h, ple_gate_norm[i:i + 1])
        tok = relay(g0 + PER_LAYER, hg)
        z = mm_nn(f"l{i}_ple_gate", hg, G[("ple_w_gate", i)], "row", 0, out_dtype=F32, tm=2048, after=tok)
        h = ple_fwd(f"l{i}_ple", h, pe, z, ple_norm[i:i + 1])
        L["ple"] = (pe, hg, z)
        saved.append(L)

    d_h, loss_part = loss_head("loss_head", h, loss_target[0])
    loss = lax.psum(loss_part[0, 0], ("x", "y", "c"))

    GW = {}
    SG = {}
    ids_arr = jnp.stack([cc.astype(jnp.int32), chip])
    two = lambda a: a.reshape(-1, a.shape[-1])
    merged = {}
    pipe = {}
    order = list(reversed(range(len(groups))))
    ticks = [0]

    def put_small(name, i, val):
        SG.setdefault(name, {})[i] = val

    deferred = []

    def sibling_half(g, after):
        _, theirs = split_wait(f"share_wait_{g}", pipe[g], share_grad_plan, after)
        for (n, l), gr in zip(groups[g], theirs):
            merged[n] = adamw_half(f"adamw_sib_{n}_{l}", ids_arr, gr, two(W[n]), two(M1[n]), two(V2[n]), l,
                                   W[n].shape[0], merged[n])
        return merged[groups[g][-1][0]][0]

    def reduce_tick(after, defer=True):
        k, tok = ticks[0], after
        ticks[0] += 1
        grp = lambda j: order[j] if 0 <= j < len(order) else None
        g = grp(k - 4)
        if g is not None:
            if defer:
                deferred.append(g)
            else:
                sibling_half(g, tok)
        starts = []
        g = grp(k - 3)
        if g is not None:
            sums, landed = split_wait(f"exchange_wait_{g}", pipe[g], chip_exchange_plan, tok)
            halves = []
            for (n, l), s, ld in zip(groups[g], sums, landed):
                *merged[n], mine = adamw_half(f"adamw_own_{n}_{l}", ids_arr, (s, ld), two(W[n]), two(M1[n]),
                                              two(V2[n]), l, W[n].shape[0], merged.get(n))
                halves.append(mine)
            starts.append((g, (halves, [(h.shape, h.dtype) for h in halves], len(halves), share_grad_plan)))
        g = grp(k - 1)
        if g is not None:
            parts, got = split_wait(f"swap_wait_{g}", pipe[g], swap_halves_plan, tok)
            sums = [add_halves(f"pair_sum_{n}_{l}", core_arr, q, r) for (n, l), q, r in zip(groups[g], parts, got)]
            starts.append((g, (sums, [(s.shape, s.dtype) for s in sums], 3 * len(sums), chip_exchange_plan)))
        g = grp(k)
        if g is not None:
            parts = [GW[key] for key in groups[g]]
            lands = [((q.shape[0], q.shape[1] // 2, q.shape[2]), q.dtype) for q in parts]
            starts.append((g, (parts, lands, len(parts), swap_halves_plan)))
        if starts:
            behind = ids_arr if grp(k) is not None else tok
            for (g, _), state in zip(starts, split_start(f"reduce_start_{k}", [p for _, p in starts], behind)):
                pipe[g] = state
            tok = state[4]
        return tok

    def ffn_bwd(tag, d_h, d_hb, norm, w_in, w_out, layer, fw, tok):
        h_in, hn, gu, act = fw
        GW[(w_out, layer)] = mm_tn(f"{tag}_dwout", act, d_hb, "row", scale=FFN_RESIDUAL_WEIGHT, tk=1408, tn=1024,
                                   after=tok)
        tok = reduce_tick(GW[(w_out, layer)])
        dgu = ffn_dact_dgu(f"{tag}_dact", d_hb, G[(w_out, layer)], gu, FFN_RESIDUAL_WEIGHT, after=tok)
        GW[(w_in, layer)] = mm_tn(f"{tag}_dwin", hn, dgu, "col")
        tok = reduce_tick(GW[(w_in, layer)])
        d_hn = _mm_nt(f"{tag}_dhn", dgu, G[(w_in, layer)], "col", 0, D, out_dtype=F32, tc=2816, after=tok)
        return (*rms_bwd_res(f"{tag}_drms", h_in, d_hn, norm, d_h), tok)

    d_hb, tok = None, None
    for i in reversed(range(depth)):
        L = saved[i]
        j = i // 2
        pe, hg, z = L["ple"]
        d_z, d_pe, g = ple_bwd(f"l{i}_dple", d_h, pe, z, ple_norm[i:i + 1])
        put_small("ple_norm", i, g)
        d_hg = _mm_nt(f"l{i}_dhg", d_z, G[("ple_w_gate", i)], "row", 0, D, out_dtype=F32, to=512, tc=2048, after=tok)
        GW[("ple_w_gate", i)] = mm_tn(f"l{i}_dwgate", hg, d_z, "row", tk=512, tn=1024)
        GW[("ple_w_proj", i)] = mm_tn(f"l{i}_dwproj", p[i, 0], d_pe, "col")
        tok = reduce_tick(GW[("ple_w_proj", i)])
        d_h, d_hb, g = rms_bwd_res(f"l{i}_dgate_rms", L["h3"], d_hg, ple_gate_norm[i:i + 1], d_h)
        put_small("ple_gate_norm", i, g)

        d_h, d_hb, g, tok = ffn_bwd(f"l{i}_ffb", d_h, d_hb, ffn_b_norm[i:i + 1], "ffn_b_w_in", "ffn_b_w_out", i,
                                    L["ffn_b"], tok)
        put_small("ffn_b_norm", i, g)

        hn = L["hn_m"]
        if i % 2 == 0:
            lat, cq, ckv, q_raw, kv_raw, Qh, Kh, Vh, O, w_in_pad, w_uq_pad = L["mla"]
            d_O = _mm_nt(f"l{i}_dO", d_hb, G[("mla_w_o", j)], "row", 0, H * D_V, out_dtype=BF16, to=512, tc=2048,
                         after=tok)
            GW[("mla_w_o", j)] = mm_tn(f"l{i}_dwo", O, d_hb, "row", tk=512, tn=1024)
            dQ, dK, dV = attn_bwd(f"l{i}_dattn", Qh, Kh, Vh, d_O)
            d_q_raw, d_kv_raw, d_kr, gq, gk = mla_prep_bwd(f"l{i}_dprep", dQ, dK, dV, q_raw, kv_raw, lat, tabs,
                                                           q_gain_p, k_gain_p, QL + KL)
            put_small("mla_q_gain", j, gq[:, :QK_DIM])
            put_small("mla_k_gain", j, gk[:, :QK_DIM])
            d_cq = _mm_nt(f"l{i}_dcq", d_q_raw, w_uq_pad, "row", 0, QL, out_dtype=F32, to=512, tc=2048)
            g_uq = mm_tn(f"l{i}_dwuq", cq, d_q_raw, "row", shards=1, out_dtype=F32, tk=512, tn=1024)
            g_uq = g_uq.reshape(QL, H, HEAD_PAD)[:, :, :QK_DIM].reshape(QL, N_CHIPS, -1).transpose(1, 0, 2)
            GW[("mla_w_uq", j)] = g_uq.astype(BF16)
            d_ckv = _mm_nt(f"l{i}_dckv", d_kv_raw, G[("mla_w_ukv", j)], "col", 0, KL, out_dtype=F32, to=512, tc=1024)
            GW[("mla_w_ukv", j)] = mm_tn(f"l{i}_dwukv", ckv, d_kv_raw, "col", tk=512, tn=1024)
            d_lat, gq, gk = lat_norm_bwd(f"l{i}_dlatnorm", lat, d_cq, d_ckv, d_kr, mla_q_lat_norm[j:j + 1],
                                         mla_kv_lat_norm[j:j + 1])
            put_small("mla_q_lat_norm", j, gq)
            put_small("mla_kv_lat_norm", j, gk)
            d_hn = _mm_nt(f"l{i}_dhn_lat", d_lat, w_in_pad, "row", 0, D, out_dtype=F32, to=1024, tc=lat_pad)
            g_in = mm_tn(f"l{i}_dwin_lat", hn, d_lat, "row", shards=1, out_dtype=F32, tk=1024, tn=lat_pad)
            GW[("mla_w_in", j)] = g_in[0, :, :lat_w].reshape(N_CHIPS, D // N_CHIPS, lat_w).astype(BF16)
            tok = reduce_tick(GW[("mla_w_in", j)])
        else:
            ag, u, yc, cact = L["conv"]
            d_cact = _mm_nt(f"l{i}_dcact", d_hb, G[("conv_w_pw2", j)], "row", 0, C, out_dtype=F32, to=512, tc=2048,
                            after=tok)
            GW[("conv_w_pw2", j)] = mm_tn(f"l{i}_dwpw2", cact, d_hb, "row", tk=512, tn=1024)
            d_yc, g1, g2, g3 = ln_silu_bwd(f"l{i}_dln", yc, d_cact, ln_g_full, ln_b_full)
            put_small("conv_ln_g", j, g1)
            put_small("conv_ln_b", j, g2)
            put_small("conv_b_dw", j, g3)
            d_u = dwconv_bwd_u(f"l{i}_ddw_u", d_yc, w_dw_full)
            put_small("conv_w_dw", j, dwconv_bwd_w(f"l{i}_ddw_w", u, d_yc))
            d_ag, g = glu_bwd(f"l{i}_dglu", ag, d_u)
            put_small("conv_b_pw1", j, g)
            d_hn = _mm_nt(f"l{i}_dhn_pw1", d_ag, G[("conv_w_pw1", j)], "col", 0, D, out_dtype=F32, tc=1024)
            GW[("conv_w_pw1", j)] = mm_tn(f"l{i}_dwpw1", hn, d_ag, "col", tn=1024)
            tok = reduce_tick(GW[("conv_w_pw1", j)])
        d_h, d_hb, g = rms_bwd_res(f"l{i}_dmix_rms", L["h1"], d_hn, mix_norm[i:i + 1], d_h)
        put_small("mix_norm", i, g)

        d_h, d_hb, g, tok = ffn_bwd(f"l{i}_ffa", d_h, d_hb, ffn_a_norm[i:i + 1], "ffn_a_w_in", "ffn_a_w_out", i,
                                    L["ffn_a"], tok)
        put_small("ffn_a_norm", i, g)
    grad_x = d_h[None]

    tok = reduce_tick(reduce_tick(d_h))
    for g in deferred:
        tok = sibling_half(g, tok)
    tok = reduce_tick(reduce_tick(tok, defer=False), defer=False)
    names = [n for n, _ in BIG_WEIGHTS]
    grads, delta, new_m, new_v = {}, {}, {}, {}
    for n in names:
        grads[n], delta[n], new_m[n], new_v[n] = [a.reshape(W[n].shape) for a in merged[n]]

    rep = []
    for n in REPLICATED_SMALL:
        rep.append(jnp.concatenate([SG[n][i] for i in sorted(SG[n])], axis=0))
    shd = [SG["conv_b_pw1"][0].reshape(2, C), SG["conv_w_dw"][0][:CONV_WIDTH], SG["conv_b_dw"][0],
           SG["conv_ln_g"][0], SG["conv_ln_b"][0]]
    red = allreduce_pack("allreduce_small", _pack_rows(rep + shd, D))
    red = _unpack_rows(red, [a.shape for a in rep + shd])
    for n, g in zip(REPLICATED_SMALL, red):
        grads[n] = g
    own = lambda a, w: lax.dynamic_slice_in_dim(a, chip * w, w, axis=a.ndim - 1)
    sh = red[len(rep):]
    grads["conv_b_pw1"] = own(sh[0].reshape(1, 2 * C), 2 * C // N_CHIPS)
    grads["conv_w_dw"] = own(sh[1], C // N_CHIPS)[None]
    grads["conv_b_dw"] = own(sh[2], C // N_CHIPS)
    grads["conv_ln_g"] = own(sh[3], C // N_CHIPS)
    grads["conv_ln_b"] = own(sh[4], C // N_CHIPS)

    small = REPLICATED_SMALL + SHARDED_SMALL
    shapes = [two(W[n]).shape for n in small]
    packs = [_pack_rows([two(src[n]) for n in small], D) for src in (W, grads, M1, V2)]
    outs = adamw("adamw_small", *packs)
    for dst, pk in zip((delta, new_m, new_v), outs):
        for n, a in zip(small, _unpack_rows(pk, shapes)):
            dst[n] = a.reshape(W[n].shape)
    for n in small:
        grads[n] = grads[n].reshape(W[n].shape)

    return (loss, grad_x, *[grads[n] for n in WEIGHT_ORDER], *[delta[n] for n in WEIGHT_ORDER],
            *[new_m[n] for n in WEIGHT_ORDER], *[new_v[n] for n in WEIGHT_ORDER])
```

```python
import functools
import math

import jax
import jax.numpy as jnp
from jax import lax
from jax.experimental import pallas as pl
from jax.experimental.pallas import tpu as pltpu

F32, BF16 = jnp.float32, jnp.bfloat16
MESH = pl.DeviceIdType.MESH

N_CHIPS = 4
EPS = 1e-6
D_NOPE, D_ROPE, D_V = 128, 64, 128
QK_DIM = D_NOPE + D_ROPE
HEAD_PAD = 256
ROPE_THETA = 10000.0
CONV_WIDTH = 31
CONV_TAPS_PAD = 32
FFN_RESIDUAL_WEIGHT = 0.5
ADAM_LR, ADAM_B1, ADAM_B2, ADAM_EPS, ADAM_WD, ADAM_STEP = 0.001, 0.9, 0.999, 1e-08, 0.01, 10
VMEM_LIMIT_BYTES = 56 * 1024 * 1024
NEG_BIG = -1e30

NN_DIMS = (((1,), (0,)), ((), ()))
NT_DIMS = (((1,), (1,)), ((), ()))
TN_DIMS = (((0,), (0,)), ((), ()))


def _cparams(semantics=None):
    kw = dict(vmem_limit_bytes=VMEM_LIMIT_BYTES)
    if semantics is not None:
        kw["dimension_semantics"] = semantics
    return pltpu.CompilerParams(**kw)


def _tile(n, pref, mult=128):
    if n <= pref:
        return n
    t = (pref // mult) * mult
    while t >= mult:
        if n % t == 0:
            return t
        t -= mult
    return n


def _rowwise(name, fn, rows, vecs, outs, accs=(), tm=256, rc=32):
    T = rows[0].shape[0]
    tm = min(tm, T)
    rc = min(rc, tm)
    nr, nv, no, na = len(rows), len(vecs), len(outs), len(accs)
    steps = tm // rc

    def body(*refs):
        row_refs = refs[:nr]
        vec_refs = refs[nr:nr + nv]
        out_refs = refs[nr + nv:nr + nv + no]
        acc_refs = refs[nr + nv + no:]
        if na:
            @pl.when(pl.program_id(0) == 0)
            def _():
                for a in acc_refs:
                    a[...] = jnp.zeros_like(a)

        def step(r, carry):
            sl = pl.ds(pl.multiple_of(r * rc, rc), rc)
            res = fn(*[x[sl, :] for x in row_refs], *[v[...] for v in vec_refs])
            for o, val in zip(out_refs, res[:no]):
                o[sl, :] = val.astype(o.dtype)
            return tuple(c + val for c, val in zip(carry, res[no:]))

        init = tuple(jnp.zeros(s, F32) for s in accs)
        tot = lax.fori_loop(0, steps, step, init)
        for a, val in zip(acc_refs, tot):
            a[...] += val

    in_specs = [pl.BlockSpec((tm, x.shape[1]), lambda i: (i, 0)) for x in rows]
    in_specs += [pl.BlockSpec(v.shape, lambda i: (0, 0)) for v in vecs]
    out_specs = [pl.BlockSpec((tm, d), lambda i: (i, 0)) for d, _ in outs]
    out_specs += [pl.BlockSpec(s, lambda i: (0, 0)) for s in accs]
    out_shape = [jax.ShapeDtypeStruct((T, d), dt) for d, dt in outs]
    out_shape += [jax.ShapeDtypeStruct(s, F32) for s in accs]
    return pl.pallas_call(
        body, name=name, grid=(T // tm,), in_specs=in_specs, out_specs=out_specs, out_shape=out_shape,
        compiler_params=_cparams(("arbitrary",)),
    )(*rows, *vecs)


def _colsum(v):
    return jnp.sum(v, axis=0, keepdims=True)


def _rstd(x):
    return lax.rsqrt(jnp.mean(x * x, axis=-1, keepdims=True) + EPS)


def _rms_bwd(x, dy, g):
    r = _rstd(x)
    xh = x * r
    dyg = dy * g
    dx = r * (dyg - xh * jnp.mean(dyg * xh, axis=-1, keepdims=True))
    return dx, dy * xh


def _sigmoid(x):
    return 1.0 / (1.0 + jnp.exp(-x))


def rms_fwd(name, h, g):
    def fn(x, gv):
        return ((x * _rstd(x)) * gv,)
    return _rowwise(name, fn, [h], [g], [(h.shape[1], BF16)])[0]


def rms_bwd_res(name, h, d_y, g, d_res):
    D = h.shape[1]

    def fn(x, dy, dr, gv):
        dx, dgr = _rms_bwd(x, dy, gv)
        dh = dr + dx
        return dh, dh, _colsum(dgr)
    return _rowwise(name, fn, [h, d_y, d_res], [g], [(D, F32), (D, BF16)], [(1, D)], tm=128)


def loss_head(name, y, target):
    D = y.shape[1]

    def fn(yv, tv):
        e = yv - tv
        tot = jnp.sum(_colsum(e * e), axis=1, keepdims=True) * (0.5 / D)
        return e * (1.0 / D), jnp.broadcast_to(tot, (1, 128))
    return _rowwise(name, fn, [y, target], [], [(D, F32)], [(1, 128)])


def ple_fwd(name, h, pe, z, g_e):
    def fn(hv, pv, zv, gv):
        return (hv + (pv * _rstd(pv)) * gv * _sigmoid(zv),)
    return _rowwise(name, fn, [h, pe, z], [g_e], [(h.shape[1], F32)], tm=128)[0]


def ple_bwd(name, d_h, pe, z, g_e):
    D = d_h.shape[1]

    def fn(dh, pv, zv, gv):
        gate = _sigmoid(zv)
        e = (pv * _rstd(pv)) * gv
        d_z = dh * e * gate * (1.0 - gate)
        d_pe, dgr = _rms_bwd(pv, dh * gate, gv)
        return d_z, d_pe, _colsum(dgr)
    return _rowwise(name, fn, [d_h, pe, z], [g_e], [(D, BF16), (D, BF16)], [(1, D)], tm=128)


def lat_norm_fwd(name, lat, g_q, g_kv):
    QL, KL = g_q.shape[1], g_kv.shape[1]

    def fn(v, gq, gk):
        a = v[:, :QL]
        b = v[:, QL:QL + KL]
        return (a * _rstd(a)) * gq, (b * _rstd(b)) * gk
    return _rowwise(name, fn, [lat], [g_q, g_kv], [(QL, BF16), (KL, BF16)])


def lat_norm_bwd(name, lat, d_cq, d_ckv, d_krope, g_q, g_kv):
    QL, KL = g_q.shape[1], g_kv.shape[1]

    def fn(v, dq, dk, dr, gq, gk):
        da, ga = _rms_bwd(v[:, :QL], dq, gq)
        db, gb = _rms_bwd(v[:, QL:QL + KL], dk, gk)
        return jnp.concatenate([da, db, dr], axis=-1), _colsum(ga), _colsum(gb)
    return _rowwise(name, fn, [lat, d_cq, d_ckv, d_krope], [g_q, g_kv],
                    [(lat.shape[1], BF16)], [(1, QL), (1, KL)])


def glu_fwd(name, ag):
    C = ag.shape[1] // 2

    def fn(v):
        return (v[:, :C] * _sigmoid(v[:, C:]),)
    return _rowwise(name, fn, [ag], [], [(C, F32)], tm=128)[0]


def glu_bwd(name, ag, d_u):
    C = ag.shape[1] // 2

    def fn(v, du):
        a = v[:, :C]
        s = _sigmoid(v[:, C:])
        d = jnp.concatenate([du * s, du * a * s * (1.0 - s)], axis=-1)
        return d, _colsum(d)
    return _rowwise(name, fn, [ag, d_u], [], [(2 * C, BF16)], [(1, 2 * C)], tm=128)


def ln_silu_fwd(name, yc, g, b):
    def fn(v, gv, bv):
        xc = v - jnp.mean(v, axis=-1, keepdims=True)
        ln = xc * lax.rsqrt(jnp.mean(xc * xc, axis=-1, keepdims=True) + EPS) * gv + bv
        return (ln * _sigmoid(ln),)
    return _rowwise(name, fn, [yc], [g, b], [(yc.shape[1], BF16)], tm=128)[0]


def ln_silu_bwd(name, yc, d_out, g, b):
    C = yc.shape[1]

    def fn(v, do, gv, bv):
        xc = v - jnp.mean(v, axis=-1, keepdims=True)
        r = lax.rsqrt(jnp.mean(xc * xc, axis=-1, keepdims=True) + EPS)
        xh = xc * r
        ln = xh * gv + bv
        s = _sigmoid(ln)
        d_ln = do * s * (1.0 + ln * (1.0 - s))
        dxh = d_ln * gv
        dy = r * (dxh - jnp.mean(dxh, axis=-1, keepdims=True) - xh * jnp.mean(dxh * xh, axis=-1, keepdims=True))
        return dy, _colsum(d_ln * xh), _colsum(d_ln), _colsum(dy)
    return _rowwise(name, fn, [yc, d_out], [g, b], [(C, F32)], [(1, C), (1, C), (1, C)], tm=128)


def add_halves(name, ids, own, got):
    S, Rh, C = got.shape
    tr = _tile(Rh, 512, 16)
    nrb = Rh // tr

    def body(ids_ref, a_ref, b_ref, o_ref):
        o_ref[...] = (a_ref[...].astype(F32) + b_ref[...].astype(F32)).astype(o_ref.dtype)

    gs = pltpu.PrefetchScalarGridSpec(
        num_scalar_prefetch=1, grid=(S - 1, nrb),
        in_specs=[pl.BlockSpec((None, tr, C), lambda s, r, ids: (ids[1] ^ (s + 1), ids[0] * nrb + r, 0)),
                  pl.BlockSpec((None, tr, C), lambda s, r, ids: (ids[1] ^ (s + 1), r, 0))],
        out_specs=pl.BlockSpec((None, tr, C), lambda s, r, ids: (s, r, 0)))
    return pl.pallas_call(
        body, name=name, grid_spec=gs, out_shape=jax.ShapeDtypeStruct((S - 1, Rh, C), BF16),
        compiler_params=_cparams(("arbitrary", "arbitrary")))(ids, own, got)


def adamw_half(name, ids, grad, w, m, v, layer, layers, into):
    own = isinstance(grad, tuple)
    Rh, C = grad[1].shape[1:] if own else grad.shape
    tr = _tile(Rh, 128, 16)
    nrb = Rh // tr
    n_grad = 5 if own else 1
    c1 = 1.0 / (1.0 - ADAM_B1 ** ADAM_STEP)
    c2 = 1.0 / (1.0 - ADAM_B2 ** ADAM_STEP)

    def body(ids_ref, *refs):
        g_refs, (w_ref, m_ref, v_ref) = refs[:n_grad], refs[n_grad:n_grad + 3]
        outs = refs[-5:] if own else refs[-4:]
        go_ref, d_ref, nm_ref, nv_ref = outs[:4]
        if own:
            f = [g[...].astype(F32) for g in g_refs]
            gv = ((f[0] + f[1]) + f[2]) + (f[3] + f[4])
            outs[4][...] = gv
        else:
            gv = g_refs[0][...]
        nm = ADAM_B1 * m_ref[...] + (1.0 - ADAM_B1) * gv
        nv = ADAM_B2 * v_ref[...] + (1.0 - ADAM_B2) * (gv * gv)
        go_ref[...] = gv
        d_ref[...] = -ADAM_LR * ((nm * c1) / (jnp.sqrt(nv * c2) + ADAM_EPS) + ADAM_WD * w_ref[...])
        nm_ref[...] = nm
        nv_ref[...] = nv

    half = (lambda ids: ids[0]) if own else (lambda ids: 1 - ids[0])
    rows = pl.BlockSpec((tr, C), lambda r, ids: ((2 * layer + half(ids)) * nrb + r, 0))
    plain = pl.BlockSpec((tr, C), lambda r, ids: (r, 0))
    if own:
        slot = lambda flip: pl.BlockSpec((None, tr, C), lambda r, ids: (ids[1] ^ flip, r, 0))
        mine = pl.BlockSpec((None, tr, C), lambda r, ids: (ids[1], ids[0] * nrb + r, 0))
        in_specs = [mine, slot(0), slot(1), slot(2), slot(3), rows, rows, rows]
        operands = [ids, grad[0], grad[1], grad[2], grad[2], grad[2], w, m, v]
    else:
        in_specs = [plain, rows, rows, rows]
        operands = [ids, grad, w, m, v]
    aliases = {}
    if into is not None:
        in_specs += [pl.BlockSpec(memory_space=pl.ANY)] * 4
        aliases = {len(operands) + k: k for k in range(4)}
        operands += list(into)
    full = jax.ShapeDtypeStruct((layers * 2 * Rh, C), F32)
    gs = pltpu.PrefetchScalarGridSpec(num_scalar_prefetch=1, grid=(nrb,), in_specs=in_specs,
                                      out_specs=[rows] * 4 + [plain] * own)
    return pl.pallas_call(
        body, name=name, grid_spec=gs, out_shape=[full] * 4 + [jax.ShapeDtypeStruct((Rh, C), F32)] * own,
        input_output_aliases=aliases, compiler_params=_cparams(("arbitrary",)))(*operands)


def cast_into_slot(name, w, chip, layer, layers, after=None):
    R, C = w.shape[0] // layers, w.shape[1]
    tr = _tile(R, 512, 16)
    nrb = R // tr

    def body(s_ref, w_ref, *rest):
        rest[-1][...] = w_ref[...].astype(BF16)

    in_specs = [pl.BlockSpec((tr, C), lambda r, s: (layer * nrb + r, 0))]
    operands = [chip, w]
    if after is not None:
        in_specs.append(pl.BlockSpec(memory_space=pl.ANY))
        operands.append(after)
    gs = pltpu.PrefetchScalarGridSpec(
        num_scalar_prefetch=1, grid=(nrb,), in_specs=in_specs,
        out_specs=pl.BlockSpec((None, tr, C), lambda r, s: (s[0], r, 0)))
    return pl.pallas_call(
        body, name=name, grid_spec=gs, out_shape=jax.ShapeDtypeStruct((N_CHIPS, R, C), BF16),
        compiler_params=_cparams(("arbitrary",)))(*operands)


def adamw(name, w, g, m, v):
    R, C = w.shape
    tr = _tile(R, 256, 8)
    c1 = 1.0 / (1.0 - ADAM_B1 ** ADAM_STEP)
    c2 = 1.0 / (1.0 - ADAM_B2 ** ADAM_STEP)

    def body(w_ref, g_ref, m_ref, v_ref, d_ref, nm_ref, nv_ref):
        gv = g_ref[...]
        nm = ADAM_B1 * m_ref[...] + (1.0 - ADAM_B1) * gv
        nv = ADAM_B2 * v_ref[...] + (1.0 - ADAM_B2) * (gv * gv)
        d_ref[...] = -ADAM_LR * ((nm * c1) / (jnp.sqrt(nv * c2) + ADAM_EPS) + ADAM_WD * w_ref[...])
        nm_ref[...] = nm
        nv_ref[...] = nv

    spec = pl.BlockSpec((tr, C), lambda r: (r, 0))
    return pl.pallas_call(
        body, name=name, grid=(R // tr,), in_specs=[spec] * 4, out_specs=[spec] * 3,
        out_shape=[jax.ShapeDtypeStruct((R, C), F32)] * 3, compiler_params=_cparams(("arbitrary",)))(w, g, m, v)


def _matmul(name, a, b, *, grid, a_blk, a_map, b_blk, b_map, o_shape, o_dtype, o_blk, o_map, dims,
            scale=None, res=None, bias=None, bias_blk=None, bias_map=None, alias_into=None, after=None):
    nk = grid[2]
    has_res, has_bias, has_into = res is not None, bias is not None, alias_into is not None
    acc_shape = tuple(d for d in o_blk if d is not None)

    def body(*refs):
        a_ref, b_ref = refs[0], refs[1]
        pos = 2
        res_ref = bias_ref = None
        if has_res:
            res_ref = refs[pos]
            pos += 1
        if has_bias:
            bias_ref = refs[pos]
            pos += 1
        if has_into:
            pos += 1
        if after is not None:
            pos += 1
        o_ref = refs[pos]
        av, bv = a_ref[...], b_ref[...]
        if bv.ndim == 3:
            bv = bv.reshape(-1, bv.shape[-1])
        if av.dtype != BF16:
            av = av.astype(BF16)
        if bv.dtype != BF16:
            bv = bv.astype(BF16)
        part = lax.dot_general(av, bv, dims, preferred_element_type=F32)

        def finish(acc):
            if scale is not None:
                acc = acc * scale
            if has_bias:
                acc = acc + bias_ref[...]
            if has_res:
                acc = acc + res_ref[...]
            o_ref[...] = acc.astype(o_ref.dtype)

        if nk == 1:
            finish(part)
        else:
            acc_ref = refs[pos + 1]
            k = pl.program_id(2)

            @pl.when(k == 0)
            def _():
                acc_ref[...] = part

            @pl.when(k > 0)
            def _():
                acc_ref[...] += part

            @pl.when(k == nk - 1)
            def _():
                finish(acc_ref[...])

    operands = [a, b]
    in_specs = [pl.BlockSpec(a_blk, a_map), pl.BlockSpec(b_blk, b_map)]
    if has_res:
        operands.append(res)
        in_specs.append(pl.BlockSpec(o_blk, o_map))
    if has_bias:
        operands.append(bias)
        in_specs.append(pl.BlockSpec(bias_blk, bias_map))
    aliases = {}
    if has_into:
        aliases = {len(operands): 0}
        operands.append(alias_into)
        in_specs.append(pl.BlockSpec(memory_space=pl.ANY))
    if after is not None:
        operands.append(after)
        in_specs.append(pl.BlockSpec(memory_space=pl.ANY))
    return pl.pallas_call(
        body, name=name, grid=grid, in_specs=in_specs, out_specs=pl.BlockSpec(o_blk, o_map),
        out_shape=jax.ShapeDtypeStruct(o_shape, o_dtype),
        scratch_shapes=[pltpu.VMEM(acc_shape, F32)] if nk > 1 else [],
        input_output_aliases=aliases,
        compiler_params=_cparams(("parallel", "parallel", "arbitrary")),
    )(*operands)


def mm_nn(name, a, w3, kind, layer, *, out_dtype, scale=None, res=None, bias=None, tm=1024, tn=512, tk=2048,
          after=None, whole_k=False):
    M, K = a.shape
    S, _, C = w3.shape
    tm = _tile(M, tm, 16)
    b_blk = None
    if kind == "col":
        N = S * C
        tk, tn = _tile(K, tk), _tile(C, tn)
        kb, nb = K // tk, C // tn
        b_map = lambda n, m, k: (n // nb, layer * kb + k, n % nb)
    elif whole_k:
        N, K4, tk, tn = C, K // S, K, _tile(C, tn)
        b_blk, b_map = (S, K4, tn), lambda n, m, k: (0, layer, n)
    else:
        N, K4 = C, K // S
        tk, tn = _tile(K4, tk), _tile(C, tn)
        kb4 = K4 // tk
        b_map = lambda n, m, k: (k // kb4, layer * kb4 + k % kb4, n)
    return _matmul(
        name, a, w3, grid=(N // tn, M // tm, K // tk),
        a_blk=(tm, tk), a_map=lambda n, m, k: (m, k), b_blk=b_blk or (None, tk, tn), b_map=b_map,
        o_shape=(M, N), o_dtype=out_dtype, o_blk=(tm, tn), o_map=lambda n, m, k: (m, n), dims=NN_DIMS,
        scale=scale, res=res, bias=bias, bias_blk=(1, tn), bias_map=lambda n, m, k: (0, n), after=after)


def _mm_nt(name, g, w3, kind, layer, K, *, out_dtype, scale=None, tm=1024, to=1024, tc=1408, after=None):
    S, _, C = w3.shape
    tc = _tile(C, tc)
    if g.ndim == 3:
        M, N = g.shape[1], 2 * g.shape[2]
        tm = _tile(M, tm, 16)
        cb = g.shape[2] // tc
        a_blk, a_map = (None, tm, tc), lambda o, m, c: (c // cb, m, c % cb)
    else:
        M, N = g.shape
        tm = _tile(M, tm, 16)
        a_blk, a_map = (tm, tc), lambda o, m, c: (m, c)
    if kind == "col":
        nb = C // tc
        to = _tile(K, to)
        ob = K // to
        b_map = lambda o, m, c: (c // nb, layer * ob + o, c % nb)
    else:
        K4 = K // S
        to = _tile(K4, to)
        ob4 = K4 // to
        b_map = lambda o, m, c: (o // ob4, layer * ob4 + o % ob4, c)
    return _matmul(
        name, g, w3, grid=(K // to, M // tm, N // tc),
        a_blk=a_blk, a_map=a_map, b_blk=(None, to, tc), b_map=b_map,
        o_shape=(M, K), o_dtype=out_dtype, o_blk=(tm, to), o_map=lambda o, m, c: (m, o), dims=NT_DIMS,
        scale=scale, after=after)


def mm_tn(name, a, g, kind, *, shards=N_CHIPS, layer=0, layers=1, into=None, out_dtype=BF16, scale=None,
          tk=1024, tn=1408, tm=2048, after=None):
    M, K = a.shape
    N = 2 * g.shape[2] if g.ndim == 3 else g.shape[1]
    tm = _tile(M, tm, 16)
    if kind == "col":
        C = N // shards
        tk, tn = _tile(K, tk), _tile(C, tn)
        kb, nb = K // tk, C // tn
        o_shape = (shards, layers * K, C)
        o_map = lambda k, n, m: (n // nb, layer * kb + k, n % nb)
    else:
        K4 = K // shards
        tk, tn = _tile(K4, tk), _tile(N, tn)
        kb4 = K4 // tk
        o_shape = (shards, layers * K4, N)
        o_map = lambda k, n, m: (k // kb4, layer * kb4 + k % kb4, n)
    if g.ndim == 3:
        nbh = g.shape[2] // tn
        g_blk, g_map = (None, tm, tn), lambda k, n, m: (n // nbh, m, n % nbh)
    else:
        g_blk, g_map = (tm, tn), lambda k, n, m: (m, n)
    return _matmul(
        name, a, g, grid=(K // tk, N // tn, M // tm),
        a_blk=(tm, tk), a_map=lambda k, n, m: (m, k), b_blk=g_blk, b_map=g_map,
        o_shape=o_shape, o_dtype=out_dtype, o_blk=(None, tk, tn), o_map=o_map, dims=TN_DIMS,
        scale=scale, alias_into=into, after=after)


def ffn_in_act(name, hn, w3, *, tn=256, after=None):
    M, K = hn.shape
    S, _, C = w3.shape
    tn = _tile(C, tn)
    nb = C // tn

    def body(a_ref, bg_ref, bu_ref, *rest):
        gu_ref, act_ref = rest[-2:]
        a = a_ref[...]
        g = jnp.dot(a, bg_ref[...], preferred_element_type=F32)
        u = jnp.dot(a, bu_ref[...], preferred_element_type=F32)
        gu_ref[0] = g.astype(BF16)
        gu_ref[1] = u.astype(BF16)
        act_ref[...] = (g * _sigmoid(g) * u).astype(BF16)

    in_specs = [pl.BlockSpec((M, K), lambda j: (0, 0)),
                pl.BlockSpec((None, K, tn), lambda j: (j // nb, 0, j % nb)),
                pl.BlockSpec((None, K, tn), lambda j: (S // 2 + j // nb, 0, j % nb))]
    operands = [hn, w3, w3]
    if after is not None:
        in_specs.append(pl.BlockSpec(memory_space=pl.ANY))
        operands.append(after)
    width = S // 2 * C
    return pl.pallas_call(
        body, name=name, grid=(width // tn,), in_specs=in_specs,
        out_specs=[pl.BlockSpec((2, M, tn), lambda j: (0, 0, j)), pl.BlockSpec((M, tn), lambda j: (0, j))],
        out_shape=[jax.ShapeDtypeStruct((2, M, width), BF16), jax.ShapeDtypeStruct((M, width), BF16)],
        compiler_params=_cparams(("arbitrary",)))(*operands)


def ffn_dact_dgu(name, d_out, w3, gu, scale, *, tm=512, after=None):
    M, N = d_out.shape
    S, K4, _ = w3.shape
    tm = _tile(M, tm, 16)
    to = K4
    chunk = 256 if to > 256 else to

    def body(a_ref, b_ref, gu_ref, *rest):
        o_ref = rest[-1]
        a = a_ref[...]
        for c0 in range(0, to, chunk):
            cols = slice(c0, min(c0 + chunk, to))
            da = lax.dot_general(a, b_ref[cols, :], NT_DIMS, preferred_element_type=F32) * scale
            g = gu_ref[0, :, cols].astype(F32)
            u = gu_ref[1, :, cols].astype(F32)
            s = _sigmoid(g)
            o_ref[0, :, cols] = (da * u * s * (1.0 + g * (1.0 - s))).astype(BF16)
            o_ref[1, :, cols] = (da * g * s).astype(BF16)

    halves = pl.BlockSpec((2, tm, to), lambda o, m: (0, m, o))
    in_specs = [pl.BlockSpec((tm, N), lambda o, m: (m, 0)), pl.BlockSpec((None, to, N), lambda o, m: (o, 0, 0)), halves]
    operands = [d_out, w3, gu]
    if after is not None:
        in_specs.append(pl.BlockSpec(memory_space=pl.ANY))
        operands.append(after)
    return pl.pallas_call(
        body, name=name, grid=(S, M // tm), in_specs=in_specs, out_specs=halves,
        out_shape=jax.ShapeDtypeStruct((2, M, S * K4), BF16),
        compiler_params=_cparams(("parallel", "arbitrary")))(*operands)


def rope_tables(name, pos):
    T = pos.shape[0]
    half = D_ROPE // 2

    def body(p_ref, c_ref, s1_ref, s2_ref):
        lane = lax.broadcasted_iota(jnp.int32, (T, 128), 1)
        idx = (lane & (half - 1)).astype(F32)
        ang = p_ref[...] * jnp.exp(idx * (-2.0 * math.log(ROPE_THETA) / D_ROPE))
        cs, sn = jnp.cos(ang), jnp.sin(ang)
        c_ref[...] = jnp.where(lane < D_ROPE, cs, 0.0)
        s1_ref[...] = jnp.where(lane < half, -sn, 0.0)
        s2_ref[...] = jnp.where((lane >= half) & (lane < D_ROPE), sn, 0.0)

    return pl.pallas_call(body, name=name, out_shape=[jax.ShapeDtypeStruct((T, 128), F32)] * 3,
                          compiler_params=_cparams())(pos)


def _rope(v, cs, s1, s2):
    return v * cs + pltpu.roll(v, 128 - D_ROPE // 2, 1) * s1 + pltpu.roll(v, D_ROPE // 2, 1) * s2


def _rope_bwd(d, cs, s1, s2):
    return d * cs + pltpu.roll(d * s1, D_ROPE // 2, 1) + pltpu.roll(d * s2, 128 - D_ROPE // 2, 1)


def _head_rstd(n, r):
    ms = (jnp.sum(n * n, axis=-1, keepdims=True) + jnp.sum(r * r, axis=-1, keepdims=True)) * (1.0 / QK_DIM)
    return lax.rsqrt(ms + EPS)


def mla_prep_fwd(name, q_raw, kv_raw, lat, tabs, q_gain, k_gain, rope_col, tm=128):
    T = q_raw.shape[0]
    H = q_raw.shape[1] // HEAD_PAD
    tm = min(tm, T)
    rope_blk = rope_col // 128

    def body(q_ref, kv_ref, kr_ref, c_ref, s1_ref, s2_ref, qg_ref, kg_ref, Q_ref, K_ref, V_ref):
        cs, s1, s2 = c_ref[...], s1_ref[...], s2_ref[...]
        qg, kg = qg_ref[...], kg_ref[...]
        kr = kr_ref[...]
        for h in range(H):
            lo = HEAD_PAD * h
            n, r = q_ref[:, lo:lo + 128], q_ref[:, lo + 128:lo + 256]
            rs = _head_rstd(n, r)
            Q_ref[h, :, 0:128] = (n * rs * qg[:, :128]).astype(BF16)
            Q_ref[h, :, 128:256] = _rope(r * rs * qg[:, 128:], cs, s1, s2).astype(BF16)
            n = kv_ref[:, lo:lo + 128]
            rs = _head_rstd(n, kr)
            K_ref[h, :, 0:128] = (n * rs * kg[:, :128]).astype(BF16)
            K_ref[h, :, 128:256] = _rope(kr * rs * kg[:, 128:], cs, s1, s2).astype(BF16)
            V_ref[h] = kv_ref[:, lo + 128:lo + 256].astype(BF16)

    row = lambda w: pl.BlockSpec((tm, w), lambda i: (i, 0))
    vec = pl.BlockSpec((1, HEAD_PAD), lambda i: (0, 0))
    return pl.pallas_call(
        body, name=name, grid=(T // tm,),
        in_specs=[row(H * HEAD_PAD), row(H * HEAD_PAD), pl.BlockSpec((tm, 128), lambda i: (i, rope_blk)),
                  row(128), row(128), row(128), vec, vec],
        out_specs=[pl.BlockSpec((H, tm, HEAD_PAD), lambda i: (0, i, 0))] * 2 + [pl.BlockSpec((H, tm, D_V), lambda i: (0, i, 0))],
        out_shape=[jax.ShapeDtypeStruct((H, T, HEAD_PAD), BF16)] * 2 + [jax.ShapeDtypeStruct((H, T, D_V), BF16)],
        compiler_params=_cparams(("arbitrary",)),
    )(q_raw, kv_raw, lat, *tabs, q_gain, k_gain)


def mla_prep_bwd(name, dQ, dK, dV, q_raw, kv_raw, lat, tabs, q_gain, k_gain, rope_col, tm=128):
    T = q_raw.shape[0]
    H = q_raw.shape[1] // HEAD_PAD
    tm = min(tm, T)
    rope_blk = rope_col // 128

    def body(dQ_ref, dK_ref, dV_ref, q_ref, kv_ref, kr_ref, c_ref, s1_ref, s2_ref, qg_ref, kg_ref,
             dq_ref, dkv_ref, dkr_ref, dqg_ref, dkg_ref):
        @pl.when(pl.program_id(0) == 0)
        def _():
            dqg_ref[...] = jnp.zeros_like(dqg_ref)
            dkg_ref[...] = jnp.zeros_like(dkg_ref)

        cs, s1, s2 = c_ref[...], s1_ref[...], s2_ref[...]
        qg, kg = qg_ref[...], kg_ref[...]
        kr = kr_ref[...]
        dkr = jnp.zeros((tm, 128), F32)
        gq_n = jnp.zeros((1, 128), F32)
        gq_r = jnp.zeros((1, 128), F32)
        gk_n = jnp.zeros((1, 128), F32)
        gk_r = jnp.zeros((1, 128), F32)

        def norm_bwd(n, r, dn, dr, gain):
            rs = _head_rstd(n, r)
            nh, rh = n * rs, r * rs
            dng, drg = dn * gain[:, :128], dr * gain[:, 128:]
            mean = (jnp.sum(dng * nh, axis=-1, keepdims=True) + jnp.sum(drg * rh, axis=-1, keepdims=True)) * (1.0 / QK_DIM)
            return rs * (dng - nh * mean), rs * (drg - rh * mean), _colsum(dn * nh), _colsum(dr * rh)

        for h in range(H):
            lo = HEAD_PAD * h
            n, r = q_ref[:, lo:lo + 128], q_ref[:, lo + 128:lo + 256]
            dn = dQ_ref[h, :, 0:128].astype(F32)
            dr = _rope_bwd(dQ_ref[h, :, 128:256].astype(F32), cs, s1, s2)
            a, b, g1, g2 = norm_bwd(n, r, dn, dr, qg)
            dq_ref[:, lo:lo + 128] = a.astype(BF16)
            dq_ref[:, lo + 128:lo + 256] = b.astype(BF16)
            gq_n, gq_r = gq_n + g1, gq_r + g2
            n = kv_ref[:, lo:lo + 128]
            dn = dK_ref[h, :, 0:128].astype(F32)
            dr = _rope_bwd(dK_ref[h, :, 128:256].astype(F32), cs, s1, s2)
            a, b, g1, g2 = norm_bwd(n, kr, dn, dr, kg)
            dkv_ref[:, lo:lo + 128] = a.astype(BF16)
            dkv_ref[:, lo + 128:lo + 256] = dV_ref[h].astype(BF16)
            dkr = dkr + b
            gk_n, gk_r = gk_n + g1, gk_r + g2
        dkr_ref[...] = dkr
        dqg_ref[:, 0:128] += gq_n
        dqg_ref[:, 128:256] += gq_r
        dkg_ref[:, 0:128] += gk_n
        dkg_ref[:, 128:256] += gk_r

    row = lambda w: pl.BlockSpec((tm, w), lambda i: (i, 0))
    vec = pl.BlockSpec((1, HEAD_PAD), lambda i: (0, 0))
    hd = lambda w: pl.BlockSpec((H, tm, w), lambda i: (0, i, 0))
    return pl.pallas_call(
        body, name=name, grid=(T // tm,),
        in_specs=[hd(HEAD_PAD), hd(HEAD_PAD), hd(D_V), row(H * HEAD_PAD), row(H * HEAD_PAD),
                  pl.BlockSpec((tm, 128), lambda i: (i, rope_blk)), row(128), row(128), row(128), vec, vec],
        out_specs=[row(H * HEAD_PAD), row(H * HEAD_PAD), row(128), vec, vec],
        out_shape=[jax.ShapeDtypeStruct((T, H * HEAD_PAD), BF16)] * 2 + [jax.ShapeDtypeStruct((T, 128), F32)]
        + [jax.ShapeDtypeStruct((1, HEAD_PAD), F32)] * 2,
        compiler_params=_cparams(("arbitrary",)),
    )(dQ, dK, dV, q_raw, kv_raw, lat, *tabs, q_gain, k_gain)


def _causal_probs(q, k, scale, row0):
    s = lax.dot_general(q, k, NT_DIMS, preferred_element_type=F32) * scale
    row = row0 + lax.broadcasted_iota(jnp.int32, s.shape, 0)
    col = lax.broadcasted_iota(jnp.int32, s.shape, 1)
    s = jnp.where(col <= row, s, NEG_BIG)
    p = jnp.exp(s - jnp.max(s, axis=-1, keepdims=True))
    return p, jnp.sum(p, axis=-1, keepdims=True)


def attn_fwd(name, Q, K, V, tq=512):
    H, T, E = Q.shape
    tq = min(tq, T)
    nq = T // tq
    scale = QK_DIM ** -0.5

    def body(q_ref, k_ref, v_ref, o_ref):
        i = pl.program_id(1)
        for ib in range(nq):
            @pl.when(i == ib)
            def _():
                n = (ib + 1) * tq
                p, l = _causal_probs(q_ref[...], k_ref[0:n, :], scale, ib * tq)
                o = jnp.dot(p.astype(BF16), v_ref[0:n, :], preferred_element_type=F32)
                o_ref[...] = (o / l).astype(o_ref.dtype)

    return pl.pallas_call(
        body, name=name, grid=(H, nq),
        in_specs=[pl.BlockSpec((None, tq, E), lambda h, i: (h, i, 0)),
                  pl.BlockSpec((None, T, E), lambda h, i: (h, 0, 0)),
                  pl.BlockSpec((None, T, D_V), lambda h, i: (h, 0, 0))],
        out_specs=pl.BlockSpec((tq, D_V), lambda h, i: (i, h)),
        out_shape=jax.ShapeDtypeStruct((T, H * D_V), BF16),
        compiler_params=_cparams(("parallel", "arbitrary")),
    )(Q, K, V)


def attn_bwd(name, Q, K, V, dO, tq=512):
    H, T, E = Q.shape
    tq = min(tq, T)
    nq = T // tq
    scale = QK_DIM ** -0.5

    def body(q_ref, k_ref, v_ref, do_ref, dq_ref, dk_ref, dv_ref):
        i = pl.program_id(1)

        @pl.when(i == 0)
        def _():
            dk_ref[...] = jnp.zeros_like(dk_ref)
            dv_ref[...] = jnp.zeros_like(dv_ref)

        for ib in range(nq):
            @pl.when(i == ib)
            def _():
                n = (ib + 1) * tq
                q, k, v, do = q_ref[...], k_ref[0:n, :], v_ref[0:n, :], do_ref[...]
                p, l = _causal_probs(q, k, scale, ib * tq)
                p = p / l
                dp = lax.dot_general(do, v, NT_DIMS, preferred_element_type=F32)
                ds = p * (dp - jnp.sum(p * dp, axis=-1, keepdims=True)) * scale
                dsb, pb = ds.astype(BF16), p.astype(BF16)
                dq_ref[...] = jnp.dot(dsb, k, preferred_element_type=F32)
                dk_ref[0:n, :] += lax.dot_general(dsb, q, TN_DIMS, preferred_element_type=F32)
                dv_ref[0:n, :] += lax.dot_general(pb, do, TN_DIMS, preferred_element_type=F32)

    return pl.pallas_call(
        body, name=name, grid=(H, nq),
        in_specs=[pl.BlockSpec((None, tq, E), lambda h, i: (h, i, 0)),
                  pl.BlockSpec((None, T, E), lambda h, i: (h, 0, 0)),
                  pl.BlockSpec((None, T, D_V), lambda h, i: (h, 0, 0)),
                  pl.BlockSpec((tq, D_V), lambda h, i: (i, h))],
        out_specs=[pl.BlockSpec((None, tq, E), lambda h, i: (h, i, 0)),
                   pl.BlockSpec((None, T, E), lambda h, i: (h, 0, 0)),
                   pl.BlockSpec((None, T, D_V), lambda h, i: (h, 0, 0))],
        out_shape=[jax.ShapeDtypeStruct((H, T, E), F32)] * 2 + [jax.ShapeDtypeStruct((H, T, D_V), F32)],
        compiler_params=_cparams(("parallel", "arbitrary")),
    )(Q, K, V, dO)


def _taps_by_residue(offsets):
    groups = {}
    for k, off in enumerate(offsets):
        groups.setdefault(off % 8, []).append((k, off - off % 8))
    return groups


def _dw_specs(T, C, tm, tc, halo):
    cur = pl.BlockSpec((tm, tc), lambda j, i: (i, j))
    last = T // tm - 1
    if halo == "prev":
        nbr = pl.BlockSpec((tm, tc), lambda j, i: (jnp.maximum(i - 1, 0), j))
    else:
        nbr = pl.BlockSpec((tm, tc), lambda j, i: (jnp.minimum(i + 1, last), j))
    return cur, nbr


def dwconv_fwd(name, u, w, b, tm=256, tc=512, rs=32):
    T, C = u.shape
    tm, tc = min(tm, T), min(tc, C)
    cur, prev = _dw_specs(T, C, tm, tc, "prev")

    def body(up_ref, uc_ref, w_ref, b_ref, o_ref, scr):
        i = pl.program_id(1)

        @pl.when(i == 0)
        def _():
            scr[pl.ds(0, tm), :] = jnp.zeros((tm, tc), F32)

        @pl.when(i > 0)
        def _():
            scr[pl.ds(0, tm), :] = up_ref[...]

        scr[pl.ds(tm, tm), :] = uc_ref[...]
        for s in range(tm // rs):
            acc = jnp.broadcast_to(b_ref[...], (rs, tc))
            for k in range(CONV_WIDTH):
                acc = acc + w_ref[pl.ds(k, 1), :] * scr[pl.ds(tm - (CONV_WIDTH - 1) + k + rs * s, rs), :]
            o_ref[pl.ds(rs * s, rs), :] = acc

    return pl.pallas_call(
        body, name=name, grid=(C // tc, T // tm),
        in_specs=[prev, cur, pl.BlockSpec((CONV_TAPS_PAD, tc), lambda j, i: (0, j)), pl.BlockSpec((1, tc), lambda j, i: (0, j))],
        out_specs=cur, out_shape=jax.ShapeDtypeStruct((T, C), F32),
        scratch_shapes=[pltpu.VMEM((2 * tm, tc), F32)], compiler_params=_cparams(("parallel", "arbitrary")),
    )(u, u, w, b)


def dwconv_bwd_u(name, dy, w, tm=256, tc=512, rs=32):
    T, C = dy.shape
    tm, tc = min(tm, T), min(tc, C)
    cur, nxt = _dw_specs(T, C, tm, tc, "next")
    last = T // tm - 1

    taps = _taps_by_residue([(CONV_WIDTH - 1) - k for k in range(CONV_WIDTH)])

    def body(dc_ref, dn_ref, w_ref, o_ref, scr, tmp):
        i = pl.program_id(1)
        scr[pl.ds(0, tm), :] = dc_ref[...]
        scr[pl.ds(2 * tm, 8), :] = jnp.zeros((8, tc), F32)

        @pl.when(i == last)
        def _():
            scr[pl.ds(tm, tm), :] = jnp.zeros((tm, tc), F32)

        @pl.when(i < last)
        def _():
            scr[pl.ds(tm, tm), :] = dn_ref[...]

        for s in range(tm // rs):
            acc = jnp.zeros((rs, tc), F32)
            for r, group in taps.items():
                a = jnp.zeros((rs + 8, tc), F32)
                for k, base in group:
                    a = a + w_ref[pl.ds(k, 1), :] * scr[pl.ds(base + rs * s, rs + 8), :]
                if r == 0:
                    acc = acc + a[:rs]
                else:
                    tmp[...] = a
                    acc = acc + tmp[pl.ds(r, rs), :]
            o_ref[pl.ds(rs * s, rs), :] = acc

    return pl.pallas_call(
        body, name=name, grid=(C // tc, T // tm),
        in_specs=[cur, nxt, pl.BlockSpec((CONV_TAPS_PAD, tc), lambda j, i: (0, j))],
        out_specs=cur, out_shape=jax.ShapeDtypeStruct((T, C), F32),
        scratch_shapes=[pltpu.VMEM((2 * tm + 8, tc), F32), pltpu.VMEM((rs + 8, tc), F32)],
        compiler_params=_cparams(("parallel", "arbitrary")),
    )(dy, dy, w)


def dwconv_bwd_w(name, u, dy, tm=256, tc=512, rs=32):
    T, C = u.shape
    tm, tc = min(tm, T), min(tc, C)
    cur, prev = _dw_specs(T, C, tm, tc, "prev")

    taps = _taps_by_residue([tm - (CONV_WIDTH - 1) + k for k in range(CONV_WIDTH)])
    span = tm + 8

    def body(up_ref, uc_ref, dy_ref, o_ref, scr, dyp, dys):
        i = pl.program_id(1)

        @pl.when(i == 0)
        def _():
            scr[pl.ds(0, tm), :] = jnp.zeros((tm, tc), F32)
            o_ref[...] = jnp.zeros_like(o_ref)

        @pl.when(i > 0)
        def _():
            scr[pl.ds(0, tm), :] = up_ref[...]

        scr[pl.ds(tm, tm), :] = uc_ref[...]
        scr[pl.ds(2 * tm, 8), :] = jnp.zeros((8, tc), F32)
        dyp[pl.ds(0, 8), :] = jnp.zeros((8, tc), F32)
        dyp[pl.ds(8, tm), :] = dy_ref[...]
        dyp[pl.ds(8 + tm, 8), :] = jnp.zeros((8, tc), F32)
        for r in taps:
            dys[r] = dyp[pl.ds(8 - r, span), :]
        for r, group in taps.items():
            for k, base in group:
                acc = jnp.zeros((8, tc), F32)
                for s in range(span // 8):
                    acc = acc + dys[r, pl.ds(8 * s, 8), :] * scr[pl.ds(base + 8 * s, 8), :]
                o_ref[pl.ds(k, 1), :] += _colsum(acc)

    return pl.pallas_call(
        body, name=name, grid=(C // tc, T // tm),
        in_specs=[prev, cur, cur],
        out_specs=pl.BlockSpec((CONV_TAPS_PAD, tc), lambda j, i: (0, j)),
        out_shape=jax.ShapeDtypeStruct((CONV_TAPS_PAD, C), F32),
        scratch_shapes=[pltpu.VMEM((2 * tm + 8, tc), F32), pltpu.VMEM((tm + 16, tc), F32),
                        pltpu.VMEM((8, span, tc), F32)],
        compiler_params=_cparams(("parallel", "arbitrary")),
    )(u, u, dy)


def _place():
    x, y, c = lax.axis_index("x"), lax.axis_index("y"), lax.axis_index("c")
    return x, y, c


def _other_chips(x, y):
    return [(1 - x, y, 2 * (1 - x) + y), (x, 1 - y, 2 * x + (1 - y)), (1 - x, 1 - y, 2 * (1 - x) + (1 - y))]


def _hbm_specs(n):
    return [pl.BlockSpec(memory_space=pl.ANY)] * n


HBM_SPEC = pl.BlockSpec(memory_space=pltpu.HBM)
SEM_SPEC = pl.BlockSpec(memory_space=pltpu.SEMAPHORE)
ANY_SPEC = pl.BlockSpec(memory_space=pl.ANY)
SIDE_EFFECT = pltpu.SideEffectType.DATAFLOW_SIDE_EFFECTING


def _half(ref, slot, which):
    rh = ref.shape[1] // 2
    return ref.at[slot, pl.ds(pl.multiple_of(which * rh, 16), rh), :]


def _hbm(a):
    return pltpu.with_memory_space_constraint(a, pltpu.HBM)


def gather_start(name, groups, after):
    flat = [b for g in groups for b in g]
    n, ng = len(flat), len(groups)

    def body(*refs):
        send, recv = refs[n + 1:n + 1 + ng], refs[n + 1 + ng:n + 1 + 2 * ng]
        out, token = refs[n + 1 + 2 * ng:2 * n + 1 + 2 * ng], refs[2 * n + 1 + 2 * ng]
        token[...] = jnp.zeros_like(token)
        x, y, c = _place()
        me = 2 * x + y
        a = 0
        for g, grp in enumerate(groups):
            for k in range(len(grp)):
                piece = _half(out[a], me, c)
                for j, (px, py, _) in enumerate(_other_chips(x, y)):
                    pltpu.make_async_remote_copy(
                        src_ref=piece, dst_ref=piece, send_sem=send[g].at[3 * k + j], recv_sem=recv[g].at[3 * k + j],
                        device_id=(px, py, c), device_id_type=MESH).start()
                a += 1

    sems = [pltpu.SemaphoreType.DMA((3 * len(g),)) for g in groups]
    res = pl.pallas_call(
        body, name=name, in_specs=[HBM_SPEC] * n + [ANY_SPEC],
        out_specs=[SEM_SPEC] * (2 * ng) + [HBM_SPEC] * n + [pl.BlockSpec(memory_space=pltpu.VMEM)],
        out_shape=sems + sems + [pltpu.HBM(b.shape, b.dtype) for b in flat] + [jax.ShapeDtypeStruct((8, 128), F32)],
        input_output_aliases={a: 2 * ng + a for a in range(n)},
        compiler_params=pltpu.CompilerParams(has_side_effects=SIDE_EFFECT),
    )(*[_hbm(b) for b in flat], after)
    send, recv, bufs = res[:ng], res[ng:2 * ng], list(res[2 * ng:2 * ng + n])
    out, a = [], 0
    for g, grp in enumerate(groups):
        out.append((send[g], recv[g], bufs[a:a + len(grp)]))
        a += len(grp)
    return out, res[2 * ng + n]


def gather_relay(name, started, after):
    send1, recv1, bufs = started
    n = len(bufs)

    def body(*refs):
        s1, r1 = refs[n], refs[n + 1]
        s2, r2, out, token = refs[n + 3], refs[n + 4], refs[n + 5:2 * n + 5], refs[2 * n + 5]
        x, y, c = _place()
        me = 2 * x + y
        chips = _other_chips(x, y)
        for k in range(n):
            for j, (px, py, idx) in enumerate(chips):
                cp = pltpu.make_async_remote_copy(
                    src_ref=_half(out[k], me, c), dst_ref=_half(out[k], idx, c), send_sem=s1.at[3 * k + j],
                    recv_sem=r1.at[3 * k + j], device_id=(px, py, c), device_id_type=MESH)
                cp.wait_send()
                cp.wait_recv()
        for k in range(n):
            for j, (px, py, idx) in enumerate(chips):
                piece = _half(out[k], idx, c)
                pltpu.make_async_remote_copy(
                    src_ref=piece, dst_ref=piece, send_sem=s2.at[3 * k + j], recv_sem=r2.at[3 * k + j],
                    device_id=(x, y, 1 - c), device_id_type=MESH).start()
        token[...] = jnp.zeros_like(token)

    sem = pltpu.SemaphoreType.DMA((3 * n,))
    res = pl.pallas_call(
        body, name=name, in_specs=[HBM_SPEC] * n + [SEM_SPEC, SEM_SPEC, ANY_SPEC],
        out_specs=[SEM_SPEC, SEM_SPEC] + [HBM_SPEC] * n + [pl.BlockSpec(memory_space=pltpu.VMEM)],
        out_shape=[sem, sem] + [pltpu.HBM(b.shape, b.dtype) for b in bufs] + [jax.ShapeDtypeStruct((8, 128), F32)],
        input_output_aliases={a: 2 + a for a in range(n)},
        compiler_params=pltpu.CompilerParams(has_side_effects=SIDE_EFFECT),
    )(*bufs, send1, recv1, after)
    return res[0], res[1], list(res[2:2 + n]), res[2 + n]


def gather_wait(name, relayed, after):
    send2, recv2, bufs, _ = relayed
    n = len(bufs)

    def body(*refs):
        s2, r2, out = refs[n], refs[n + 1], refs[n + 3:]
        x, y, c = _place()
        for k in range(n):
            for j, (px, py, idx) in enumerate(_other_chips(x, y)):
                cp = pltpu.make_async_remote_copy(
                    src_ref=_half(out[k], idx, c), dst_ref=_half(out[k], idx, 1 - c), send_sem=s2.at[3 * k + j],
                    recv_sem=r2.at[3 * k + j], device_id=(x, y, 1 - c), device_id_type=MESH)
                cp.wait_send()
                cp.wait_recv()

    res = pl.pallas_call(
        body, name=name, in_specs=[HBM_SPEC] * n + [SEM_SPEC, SEM_SPEC, ANY_SPEC], out_specs=[HBM_SPEC] * n,
        out_shape=[pltpu.HBM(b.shape, b.dtype) for b in bufs], input_output_aliases={a: a for a in range(n)},
        compiler_params=pltpu.CompilerParams(has_side_effects=SIDE_EFFECT),
    )(*bufs, send2, recv2, after)
    return list(res)


def split_start(name, parts, after):
    bufs = [b for p in parts for b in p[0]]
    land_shapes = [s for p in parts for s in p[1]]
    nb, nl = len(bufs), len(land_shapes)

    def body(*refs):
        send, recv = refs[nb + 1], refs[nb + 2]
        out, lands, token = refs[nb + 3:2 * nb + 3], refs[2 * nb + 3:2 * nb + 3 + nl], refs[2 * nb + 3 + nl]
        x, y, c = _place()
        b0 = l0 = k0 = 0
        for p_bufs, p_lands, n_copies, plan in parts:
            copies = plan(out[b0:b0 + len(p_bufs)], lands[l0:l0 + len(p_lands)], x, y, c)
            for k, (src, dst, to, _) in enumerate(copies):
                pltpu.make_async_remote_copy(src_ref=src, dst_ref=dst, send_sem=send.at[k0 + k],
                                             recv_sem=recv.at[k0 + k], device_id=to, device_id_type=MESH).start()
            b0, l0, k0 = b0 + len(p_bufs), l0 + len(p_lands), k0 + n_copies
        token[...] = jnp.zeros_like(token)

    sem = pltpu.SemaphoreType.DMA((sum(p[2] for p in parts),))
    res = pl.pallas_call(
        body, name=name, in_specs=[HBM_SPEC] * nb + [ANY_SPEC],
        out_specs=[SEM_SPEC, SEM_SPEC] + [HBM_SPEC] * (nb + nl) + [pl.BlockSpec(memory_space=pltpu.VMEM)],
        out_shape=[sem, sem] + [pltpu.HBM(b.shape, b.dtype) for b in bufs]
        + [pltpu.HBM(s, d) for s, d in land_shapes] + [jax.ShapeDtypeStruct((8, 128), F32)],
        input_output_aliases={a: 2 + a for a in range(nb)},
        compiler_params=pltpu.CompilerParams(has_side_effects=SIDE_EFFECT),
    )(*[_hbm(b) for b in bufs], after)
    out_bufs, out_lands, token = list(res[2:2 + nb]), list(res[2 + nb:2 + nb + nl]), res[2 + nb + nl]
    states, b0, l0, k0 = [], 0, 0, 0
    for p_bufs, p_lands, n_copies, _ in parts:
        states.append((res[0], res[1], out_bufs[b0:b0 + len(p_bufs)], out_lands[l0:l0 + len(p_lands)], token, k0))
        b0, l0, k0 = b0 + len(p_bufs), l0 + len(p_lands), k0 + n_copies
    return states


def split_wait(name, started, plan, after):
    send, recv, bufs, lands, _, k0 = started
    nb, nl = len(bufs), len(lands)

    def body(*refs):
        s, r = refs[nb + nl], refs[nb + nl + 1]
        out, lo = refs[nb + nl + 3:2 * nb + nl + 3], refs[2 * nb + nl + 3:]
        x, y, c = _place()
        for k, (src, _, to, landed) in enumerate(plan(out, lo, x, y, c)):
            cp = pltpu.make_async_remote_copy(src_ref=src, dst_ref=landed, send_sem=s.at[k0 + k],
                                              recv_sem=r.at[k0 + k], device_id=to, device_id_type=MESH)
            cp.wait_send()
            cp.wait_recv()

    res = pl.pallas_call(
        body, name=name, in_specs=[HBM_SPEC] * (nb + nl) + [SEM_SPEC, SEM_SPEC, ANY_SPEC],
        out_specs=[HBM_SPEC] * (nb + nl), out_shape=[pltpu.HBM(b.shape, b.dtype) for b in bufs + lands],
        input_output_aliases={a: a for a in range(nb + nl)},
        compiler_params=pltpu.CompilerParams(has_side_effects=SIDE_EFFECT),
    )(*bufs, *lands, send, recv, after)
    return list(res[:nb]), list(res[nb:])


def swap_halves_plan(parts, lands, x, y, c):
    out = []
    for a in range(len(parts)):
        rh = parts[a].shape[1] // 2
        theirs = parts[a].at[:, pl.ds(pl.multiple_of((1 - c) * rh, 16), rh), :]
        out.append((theirs, lands[a], (x, y, 1 - c), lands[a]))
    return out


def chip_exchange_plan(sums, lands, x, y, c):
    me = 2 * x + y
    out = []
    for a in range(len(sums)):
        for (px, py, idx), flip in zip(_other_chips(x, y), (2, 1, 3)):
            out.append((sums[a].at[flip - 1], lands[a].at[me], (px, py, c), lands[a].at[idx]))
    return out


def share_grad_plan(halves, lands, x, y, c):
    return [(halves[a], lands[a], (x, y, 1 - c), lands[a]) for a in range(len(halves))]


def allreduce_pack(name, pack):
    R, W = pack.shape

    def body(p_ref, o_ref, sib, pair, got, send_sems, recv_sems):
        x, y, c = _place()

        def swap(k, src, dst, to):
            cp = pltpu.make_async_remote_copy(src_ref=src, dst_ref=dst, send_sem=send_sems.at[k],
                                              recv_sem=recv_sems.at[k], device_id=to, device_id_type=MESH)
            cp.start()
            return cp

        cp = swap(0, p_ref, sib, (x, y, 1 - c))
        cp.wait()
        pair[...] = p_ref[...] + sib[...]
        cps = [swap(1, pair, got.at[0], (1 - x, y, c)), swap(2, pair, got.at[1], (x, 1 - y, c)),
               swap(3, pair, got.at[2], (1 - x, 1 - y, c))]
        for cp in cps:
            cp.wait()
        o_ref[...] = (pair[...] + got[1]) + (got[0] + got[2])

    return pl.pallas_call(
        body, name=name, out_shape=jax.ShapeDtypeStruct((R, W), F32),
        in_specs=[pl.BlockSpec(memory_space=pltpu.VMEM)], out_specs=pl.BlockSpec(memory_space=pltpu.VMEM),
        scratch_shapes=[pltpu.VMEM((R, W), F32), pltpu.VMEM((R, W), F32), pltpu.VMEM((3, R, W), F32),
                        pltpu.SemaphoreType.DMA((4,)), pltpu.SemaphoreType.DMA((4,))],
        compiler_params=_cparams(),
    )(pack)


BIG_WEIGHTS = [
    ("ffn_a_w_in", "col"), ("ffn_a_w_out", "row"), ("ffn_b_w_in", "col"), ("ffn_b_w_out", "row"),
    ("mla_w_in", "row"), ("mla_w_uq", "col"), ("mla_w_ukv", "col"), ("mla_w_o", "row"),
    ("conv_w_pw1", "col"), ("conv_w_pw2", "row"), ("ple_w_proj", "col"), ("ple_w_gate", "row"),
]
WEIGHT_ORDER = ["ffn_a_norm", "ffn_a_w_in", "ffn_a_w_out", "ffn_b_norm", "ffn_b_w_in", "ffn_b_w_out", "mix_norm",
                "mla_w_in", "mla_q_lat_norm", "mla_kv_lat_norm", "mla_w_uq", "mla_w_ukv", "mla_q_gain", "mla_k_gain",
                "mla_w_o", "conv_w_pw1", "conv_b_pw1", "conv_w_dw", "conv_b_dw", "conv_ln_g", "conv_ln_b", "conv_w_pw2",
                "ple_w_proj", "ple_norm", "ple_gate_norm", "ple_w_gate"]
REPLICATED_SMALL = ["ffn_a_norm", "ffn_b_norm", "mix_norm", "ple_norm", "ple_gate_norm",
                    "mla_q_lat_norm", "mla_kv_lat_norm", "mla_q_gain", "mla_k_gain"]
SHARDED_SMALL = ["conv_b_pw1", "conv_w_dw", "conv_b_dw", "conv_ln_g", "conv_ln_b"]
PACK_ROWS = 8


def _pack_rows(arrs, width):
    out = []
    for a in arrs:
        r = -(-a.shape[0] // PACK_ROWS) * PACK_ROWS
        out.append(jnp.pad(a, ((0, r - a.shape[0]), (0, width - a.shape[1]))))
    return jnp.concatenate(out, axis=0)


def _unpack_rows(pack, shapes):
    out, r0 = [], 0
    for (r, w) in shapes:
        out.append(pack[r0:r0 + r, :w])
        r0 += -(-r // PACK_ROWS) * PACK_ROWS
    return out


def kernel(x, p, positions, ffn_a_norm, ffn_a_w_in, ffn_a_w_out, ffn_b_norm, ffn_b_w_in, ffn_b_w_out, mix_norm, mla_w_in, mla_q_lat_norm, mla_kv_lat_norm, mla_w_uq, mla_w_ukv, mla_q_gain, mla_k_gain, mla_w_o, conv_w_pw1, conv_b_pw1, conv_w_dw, conv_b_dw, conv_ln_g, conv_ln_b, conv_w_pw2, ple_w_proj, ple_norm, ple_gate_norm, ple_w_gate, loss_target, m_ffn_a_norm, m_ffn_a_w_in, m_ffn_a_w_out, m_ffn_b_norm, m_ffn_b_w_in, m_ffn_b_w_out, m_mix_norm, m_mla_w_in, m_mla_q_lat_norm, m_mla_kv_lat_norm, m_mla_w_uq, m_mla_w_ukv, m_mla_q_gain, m_mla_k_gain, m_mla_w_o, m_conv_w_pw1, m_conv_b_pw1, m_conv_w_dw, m_conv_b_dw, m_conv_ln_g, m_conv_ln_b, m_conv_w_pw2, m_ple_w_proj, m_ple_norm, m_ple_gate_norm, m_ple_w_gate, v_ffn_a_norm, v_ffn_a_w_in, v_ffn_a_w_out, v_ffn_b_norm, v_ffn_b_w_in, v_ffn_b_w_out, v_mix_norm, v_mla_w_in, v_mla_q_lat_norm, v_mla_kv_lat_norm, v_mla_w_uq, v_mla_w_ukv, v_mla_q_gain, v_mla_k_gain, v_mla_w_o, v_conv_w_pw1, v_conv_b_pw1, v_conv_w_dw, v_conv_b_dw, v_conv_ln_g, v_conv_ln_b, v_conv_w_pw2, v_ple_w_proj, v_ple_norm, v_ple_gate_norm, v_ple_w_gate):
    args = dict(locals())
    W = {n: args[n] for n in WEIGHT_ORDER}
    M1 = {n: args["m_" + n] for n in WEIGHT_ORDER}
    V2 = {n: args["v_" + n] for n in WEIGHT_ORDER}

    T, D = x.shape[1], x.shape[2]
    depth = ffn_a_norm.shape[0]
    H = mla_w_ukv.shape[2] * N_CHIPS // (D_NOPE + D_V)
    QL, KL = mla_q_lat_norm.shape[1], mla_kv_lat_norm.shape[1]
    C = conv_w_pw2.shape[1] * N_CHIPS
    lat_w = QL + KL + D_ROPE
    lat_pad = QL + KL + 128

    cx, cy, cc = lax.axis_index("x"), lax.axis_index("y"), lax.axis_index("c")
    chip = (2 * cx + cy).astype(jnp.int32)
    chip_arr = chip.reshape(1)
    core_arr = cc.astype(jnp.int32).reshape(1)

    def stage_groups(i):
        mix = ([("mla_w_in", i // 2), ("mla_w_uq", i // 2), ("mla_w_ukv", i // 2), ("mla_w_o", i // 2)] if i % 2 == 0
               else [("conv_w_pw1", i // 2), ("conv_w_pw2", i // 2)])
        return [[("ffn_a_w_in", i)], [("ffn_a_w_out", i)], mix, [("ffn_b_w_in", i)], [("ffn_b_w_out", i)],
                [("ple_w_proj", i), ("ple_w_gate", i)]]

    groups = [g for i in range(depth) for g in stage_groups(i)]
    A_IN, A_OUT, MIX, B_IN, B_OUT, PLE, PER_LAYER = 0, 1, 2, 3, 4, 5, 6

    def slot_of(key, after=None):
        n, l = key
        return cast_into_slot(f"cast_{n}_{l}", W[n].reshape(-1, W[n].shape[-1]), chip_arr, l, W[n].shape[0], after)

    def placed(a, width):
        full = jnp.zeros(a.shape[:-1] + (width,), F32)
        full = lax.dynamic_update_slice_in_dim(full, a, chip * a.shape[-1], axis=a.ndim - 1)
        return full * (cc == 0).astype(F32)

    b_pw1_sh = conv_b_pw1.reshape(1, -1)
    small_in = [placed(b_pw1_sh, 2 * C).reshape(2, C), placed(conv_w_dw[0], C), placed(conv_b_dw, C),
                placed(conv_ln_g, C), placed(conv_ln_b, C)]
    small_pack = allreduce_pack("gather_small", _pack_rows(small_in, C))
    small_full = _unpack_rows(small_pack, [(2, C), (CONV_WIDTH, C), (1, C), (1, C), (1, C)])

    FIRST = 2
    started, tok0 = gather_start("gather_start_first", [[slot_of(k) for k in g] for g in groups[:FIRST]], small_pack)
    rest, tok_rest = gather_start("gather_start_rest", [[slot_of(k, tok0) for k in g] for g in groups[FIRST:]], tok0)
    started = started + rest
    relayed, G = {}, {}

    def relay(g, after):
        if g >= len(groups):
            return None
        relayed[g] = gather_relay(f"gather_relay_{g}", started[g], after)
        return relayed[g][3]

    def ready(g, after):
        if g not in relayed:
            relay(g, after)
        for key, buf in zip(groups[g], gather_wait(f"gather_wait_{g}", relayed[g], after)):
            G[key] = buf

    relay(0, tok_rest)

    b_pw1_full = small_full[0].reshape(1, 2 * C)
    w_dw_full = jnp.pad(small_full[1], ((0, CONV_TAPS_PAD - CONV_WIDTH), (0, 0)))
    b_dw_full, ln_g_full, ln_b_full = small_full[2], small_full[3], small_full[4]

    pad_gain = lambda g: jnp.pad(g, ((0, 0), (0, HEAD_PAD - QK_DIM)))
    q_gain_p, k_gain_p = pad_gain(mla_q_gain), pad_gain(mla_k_gain)
    tabs = rope_tables("rope_tables", positions.reshape(T, 1).astype(F32))

    def ffn_fwd(tag, h, norm, w_in, w_out, layer, g_in):
        hn = rms_fwd(f"{tag}_rms", h, norm)
        ready(g_in, hn)
        tok = relay(g_in + 1, hn) if g_in > 0 else None
        gu, act = ffn_in_act(f"{tag}_in", hn, G[(w_in, layer)], after=tok)
        ready(g_in + 1, act)
        tok = relay(g_in + 2, act)
        out = mm_nn(f"{tag}_out", act, G[(w_out, layer)], "row", 0, out_dtype=F32, scale=FFN_RESIDUAL_WEIGHT, res=h,
                    tm=1024, tn=512, whole_k=True, after=tok)
        return out, (h, hn, gu, act)

    saved = []
    h = x[0]
    for i in range(depth):
        L = {}
        g0 = PER_LAYER * i
        h, L["ffn_a"] = ffn_fwd(f"l{i}_ffa", h, ffn_a_norm[i:i + 1], "ffn_a_w_in", "ffn_a_w_out", i, g0 + A_IN)
        L["h1"] = h
        hn = rms_fwd(f"l{i}_mix_rms", h, mix_norm[i:i + 1])
        L["hn_m"] = hn
        ready(g0 + MIX, hn)
        j = i // 2
        if i % 2 == 0:
            w_in_pad = jnp.pad(G[("mla_w_in", j)].reshape(D, lat_w), ((0, 0), (0, lat_pad - lat_w)))[None]
            uq = G[("mla_w_uq", j)].transpose(1, 0, 2).reshape(QL, H, QK_DIM)
            w_uq_pad = jnp.pad(uq, ((0, 0), (0, 0), (0, HEAD_PAD - QK_DIM))).reshape(1, QL, H * HEAD_PAD)
            lat = mm_nn(f"l{i}_lat", hn, w_in_pad, "row", 0, out_dtype=F32, tm=2048)
            cq, ckv = lat_norm_fwd(f"l{i}_latnorm", lat, mla_q_lat_norm[j:j + 1], mla_kv_lat_norm[j:j + 1])
            q_raw = mm_nn(f"l{i}_uq", cq, w_uq_pad, "row", 0, out_dtype=F32, tm=2048)
            kv_raw = mm_nn(f"l{i}_ukv", ckv, G[("mla_w_ukv", j)], "col", 0, out_dtype=F32, tm=2048)
            Qh, Kh, Vh = mla_prep_fwd(f"l{i}_prep", q_raw, kv_raw, lat, tabs, q_gain_p, k_gain_p, QL + KL)
            O = attn_fwd(f"l{i}_attn", Qh, Kh, Vh)
            tok = relay(g0 + B_IN, O)
            h = mm_nn(f"l{i}_wo", O, G[("mla_w_o", j)], "row", 0, out_dtype=F32, res=h, tm=1024, tn=1024, after=tok)
            L["mla"] = (lat, cq, ckv, q_raw, kv_raw, Qh, Kh, Vh, O, w_in_pad, w_uq_pad)
        else:
            ag = mm_nn(f"l{i}_pw1", hn, G[("conv_w_pw1", j)], "col", 0, out_dtype=F32, bias=b_pw1_full, tm=2048)
            u = glu_fwd(f"l{i}_glu", ag)
            yc = dwconv_fwd(f"l{i}_dw", u, w_dw_full, b_dw_full)
            cact = ln_silu_fwd(f"l{i}_ln", yc, ln_g_full, ln_b_full)
            tok = relay(g0 + B_IN, cact)
            h = mm_nn(f"l{i}_pw2", cact, G[("conv_w_pw2", j)], "row", 0, out_dtype=F32, res=h, tm=1024, tn=1024,
                      after=tok)
            L["conv"] = (ag, u, yc, cact)
        L["h2"] = h
        h, L["ffn_b"] = ffn_fwd(f"l{i}_ffb", h, ffn_b_norm[i:i + 1], "ffn_b_w_in", "ffn_b_w_out", i, g0 + B_IN)
        L["h3"] = h
        ready(g0 + PLE, h)
        pe = mm_nn(f"l{i}_ple_proj", p[i, 0], G[("ple_w_proj", i)], "col", 0, out_dtype=F32, tm=2048)
        hg = rms_fwd(f"l{i}_gate_rms", h, ple_gate_norm[i:i + 1])
        tok = relay(g0 + PER_LAYER, hg)
        z = mm_nn(f"l{i}_ple_gate", hg, G[("ple_w_gate", i)], "row", 0, out_dtype=F32, tm=2048, after=tok)
        h = ple_fwd(f"l{i}_ple", h, pe, z, ple_norm[i:i + 1])
        L["ple"] = (pe, hg, z)
        saved.append(L)

    d_h, loss_part = loss_head("loss_head", h, loss_target[0])
    loss = lax.psum(loss_part[0, 0], ("x", "y", "c"))

    GW = {}
    SG = {}
    ids_arr = jnp.stack([cc.astype(jnp.int32), chip])
    two = lambda a: a.reshape(-1, a.shape[-1])
    merged = {}
    pipe = {}
    swapped = {}
    order = list(reversed(range(len(groups))))
    ticks = [0]

    def put_small(name, i, val):
        SG.setdefault(name, {})[i] = val

    deferred = []

    def sibling_half(g, after):
        _, theirs = split_wait(f"share_wait_{g}", pipe[g], share_grad_plan, after)
        for (n, l), gr in zip(groups[g], theirs):
            merged[n] = adamw_half(f"adamw_sib_{n}_{l}", ids_arr, gr, two(W[n]), two(M1[n]), two(V2[n]), l,
                                   W[n].shape[0], merged[n])
        return merged[groups[g][-1][0]][0]

    def reduce_tick(after, defer=True):
        k, tok = ticks[0], after
        ticks[0] += 1
        grp = lambda j: order[j] if 0 <= j < len(order) else None
        g = grp(k - 4)
        if g is not None:
            if defer:
                deferred.append(g)
            else:
                sibling_half(g, tok)
        starts = []
        g = grp(k - 3)
        if g is not None:
            _, landed = split_wait(f"exchange_wait_{g}", pipe[g], chip_exchange_plan, tok)
            halves = []
            for (n, l), (q, r), ld in zip(groups[g], swapped[g], landed):
                *merged[n], mine = adamw_half(f"adamw_own_{n}_{l}", ids_arr, (q, r, ld), two(W[n]), two(M1[n]),
                                              two(V2[n]), l, W[n].shape[0], merged.get(n))
                halves.append(mine)
            starts.append((g, (halves, [(h.shape, h.dtype) for h in halves], len(halves), share_grad_plan)))
        g = grp(k - 1)
        if g is not None:
            parts, got = split_wait(f"swap_wait_{g}", pipe[g], swap_halves_plan, tok)
            swapped[g] = list(zip(parts, got))
            sums = [add_halves(f"pair_sum_{n}_{l}", ids_arr, q, r) for (n, l), q, r in zip(groups[g], parts, got)]
            lands = [((N_CHIPS,) + s.shape[1:], s.dtype) for s in sums]
            starts.append((g, (sums, lands, 3 * len(sums), chip_exchange_plan)))
        g = grp(k)
        if g is not None:
            parts = [GW[key] for key in groups[g]]
            lands = [((q.shape[0], q.shape[1] // 2, q.shape[2]), q.dtype) for q in parts]
            starts.append((g, (parts, lands, len(parts), swap_halves_plan)))
        if starts:
            behind = ids_arr if grp(k) is not None else tok
            for (g, _), state in zip(starts, split_start(f"reduce_start_{k}", [p for _, p in starts], behind)):
                pipe[g] = state
            tok = state[4]
        return tok

    def ffn_bwd(tag, d_h, d_hb, norm, w_in, w_out, layer, fw, tok):
        h_in, hn, gu, act = fw
        GW[(w_out, layer)] = mm_tn(f"{tag}_dwout", act, d_hb, "row", scale=FFN_RESIDUAL_WEIGHT, tk=1408, tn=1024,
                                   after=tok)
        tok = reduce_tick(GW[(w_out, layer)])
        dgu = ffn_dact_dgu(f"{tag}_dact", d_hb, G[(w_out, layer)], gu, FFN_RESIDUAL_WEIGHT, after=tok)
        GW[(w_in, layer)] = mm_tn(f"{tag}_dwin", hn, dgu, "col")
        tok = reduce_tick(GW[(w_in, layer)])
        d_hn = _mm_nt(f"{tag}_dhn", dgu, G[(w_in, layer)], "col", 0, D, out_dtype=F32, tc=2816, after=tok)
        return (*rms_bwd_res(f"{tag}_drms", h_in, d_hn, norm, d_h), tok)

    d_hb, tok = None, None
    for i in reversed(range(depth)):
        L = saved[i]
        j = i // 2
        pe, hg, z = L["ple"]
        d_z, d_pe, g = ple_bwd(f"l{i}_dple", d_h, pe, z, ple_norm[i:i + 1])
        put_small("ple_norm", i, g)
        d_hg = _mm_nt(f"l{i}_dhg", d_z, G[("ple_w_gate", i)], "row", 0, D, out_dtype=F32, to=512, tc=2048, after=tok)
        GW[("ple_w_gate", i)] = mm_tn(f"l{i}_dwgate", hg, d_z, "row", tk=512, tn=1024)
        GW[("ple_w_proj", i)] = mm_tn(f"l{i}_dwproj", p[i, 0], d_pe, "col")
        tok = reduce_tick(GW[("ple_w_proj", i)])
        d_h, d_hb, g = rms_bwd_res(f"l{i}_dgate_rms", L["h3"], d_hg, ple_gate_norm[i:i + 1], d_h)
        put_small("ple_gate_norm", i, g)

        d_h, d_hb, g, tok = ffn_bwd(f"l{i}_ffb", d_h, d_hb, ffn_b_norm[i:i + 1], "ffn_b_w_in", "ffn_b_w_out", i,
                                    L["ffn_b"], tok)
        put_small("ffn_b_norm", i, g)

        hn = L["hn_m"]
        if i % 2 == 0:
            lat, cq, ckv, q_raw, kv_raw, Qh, Kh, Vh, O, w_in_pad, w_uq_pad = L["mla"]
            d_O = _mm_nt(f"l{i}_dO", d_hb, G[("mla_w_o", j)], "row", 0, H * D_V, out_dtype=BF16, to=512, tc=2048,
                         after=tok)
            GW[("mla_w_o", j)] = mm_tn(f"l{i}_dwo", O, d_hb, "row", tk=512, tn=1024)
            dQ, dK, dV = attn_bwd(f"l{i}_dattn", Qh, Kh, Vh, d_O)
            d_q_raw, d_kv_raw, d_kr, gq, gk = mla_prep_bwd(f"l{i}_dprep", dQ, dK, dV, q_raw, kv_raw, lat, tabs,
                                                           q_gain_p, k_gain_p, QL + KL)
            put_small("mla_q_gain", j, gq[:, :QK_DIM])
            put_small("mla_k_gain", j, gk[:, :QK_DIM])
            d_cq = _mm_nt(f"l{i}_dcq", d_q_raw, w_uq_pad, "row", 0, QL, out_dtype=F32, to=512, tc=2048)
            g_uq = mm_tn(f"l{i}_dwuq", cq, d_q_raw, "row", shards=1, out_dtype=F32, tk=512, tn=1024)
            g_uq = g_uq.reshape(QL, H, HEAD_PAD)[:, :, :QK_DIM].reshape(QL, N_CHIPS, -1).transpose(1, 0, 2)
            GW[("mla_w_uq", j)] = g_uq.astype(BF16)
            d_ckv = _mm_nt(f"l{i}_dckv", d_kv_raw, G[("mla_w_ukv", j)], "col", 0, KL, out_dtype=F32, to=512, tc=1024)
            GW[("mla_w_ukv", j)] = mm_tn(f"l{i}_dwukv", ckv, d_kv_raw, "col", tk=512, tn=1024)
            d_lat, gq, gk = lat_norm_bwd(f"l{i}_dlatnorm", lat, d_cq, d_ckv, d_kr, mla_q_lat_norm[j:j + 1],
                                         mla_kv_lat_norm[j:j + 1])
            put_small("mla_q_lat_norm", j, gq)
            put_small("mla_kv_lat_norm", j, gk)
            d_hn = _mm_nt(f"l{i}_dhn_lat", d_lat, w_in_pad, "row", 0, D, out_dtype=F32, to=1024, tc=lat_pad)
            g_in = mm_tn(f"l{i}_dwin_lat", hn, d_lat, "row", shards=1, out_dtype=F32, tk=1024, tn=lat_pad)
            GW[("mla_w_in", j)] = g_in[0, :, :lat_w].reshape(N_CHIPS, D // N_CHIPS, lat_w).astype(BF16)
            tok = reduce_tick(GW[("mla_w_in", j)])
        else:
            ag, u, yc, cact = L["conv"]
            d_cact = _mm_nt(f"l{i}_dcact", d_hb, G[("conv_w_pw2", j)], "row", 0, C, out_dtype=F32, to=512, tc=2048,
                            after=tok)
            GW[("conv_w_pw2", j)] = mm_tn(f"l{i}_dwpw2", cact, d_hb, "row", tk=512, tn=1024)
            d_yc, g1, g2, g3 = ln_silu_bwd(f"l{i}_dln", yc, d_cact, ln_g_full, ln_b_full)
            put_small("conv_ln_g", j, g1)
            put_small("conv_ln_b", j, g2)
            put_small("conv_b_dw", j, g3)
            d_u = dwconv_bwd_u(f"l{i}_ddw_u", d_yc, w_dw_full)
            put_small("conv_w_dw", j, dwconv_bwd_w(f"l{i}_ddw_w", u, d_yc))
            d_ag, g = glu_bwd(f"l{i}_dglu", ag, d_u)
            put_small("conv_b_pw1", j, g)
            d_hn = _mm_nt(f"l{i}_dhn_pw1", d_ag, G[("conv_w_pw1", j)], "col", 0, D, out_dtype=F32, tc=1024)
            GW[("conv_w_pw1", j)] = mm_tn(f"l{i}_dwpw1", hn, d_ag, "col", tn=1024)
            tok = reduce_tick(GW[("conv_w_pw1", j)])
        d_h, d_hb, g = rms_bwd_res(f"l{i}_dmix_rms", L["h1"], d_hn, mix_norm[i:i + 1], d_h)
        put_small("mix_norm", i, g)

        d_h, d_hb, g, tok = ffn_bwd(f"l{i}_ffa", d_h, d_hb, ffn_a_norm[i:i + 1], "ffn_a_w_in", "ffn_a_w_out", i,
                                    L["ffn_a"], tok)
        put_small("ffn_a_norm", i, g)
    grad_x = d_h[None]

    tok = reduce_tick(reduce_tick(d_h))
    for g in deferred:
        tok = sibling_half(g, tok)
    tok = reduce_tick(reduce_tick(tok, defer=False), defer=False)
    names = [n for n, _ in BIG_WEIGHTS]
    grads, delta, new_m, new_v = {}, {}, {}, {}
    for n in names:
        grads[n], delta[n], new_m[n], new_v[n] = [a.reshape(W[n].shape) for a in merged[n]]

    rep = []
    for n in REPLICATED_SMALL:
        rep.append(jnp.concatenate([SG[n][i] for i in sorted(SG[n])], axis=0))
    shd = [SG["conv_b_pw1"][0].reshape(2, C), SG["conv_w_dw"][0][:CONV_WIDTH], SG["conv_b_dw"][0],
           SG["conv_ln_g"][0], SG["conv_ln_b"][0]]
    red = allreduce_pack("allreduce_small", _pack_rows(rep + shd, D))
    red = _unpack_rows(red, [a.shape for a in rep + shd])
    for n, g in zip(REPLICATED_SMALL, red):
        grads[n] = g
    own = lambda a, w: lax.dynamic_slice_in_dim(a, chip * w, w, axis=a.ndim - 1)
    sh = red[len(rep):]
    grads["conv_b_pw1"] = own(sh[0].reshape(1, 2 * C), 2 * C // N_CHIPS)
    grads["conv_w_dw"] = own(sh[1], C // N_CHIPS)[None]
    grads["conv_b_dw"] = own(sh[2], C // N_CHIPS)
    grads["conv_ln_g"] = own(sh[3], C // N_CHIPS)
    grads["conv_ln_b"] = own(sh[4], C // N_CHIPS)

    small = REPLICATED_SMALL + SHARDED_SMALL
    shapes = [two(W[n]).shape for n in small]
    packs = [_pack_rows([two(src[n]) for n in small], D) for src in (W, grads, M1, V2)]
    outs = adamw("adamw_small", *packs)
    for dst, pk in zip((delta, new_m, new_v), outs):
        for n, a in zip(small, _unpack_rows(pk, shapes)):
            dst[n] = a.reshape(W[n].shape)
    for n in small:
        grads[n] = grads[n].reshape(W[n].shape)

    return (loss, grad_x, *[grads[n] for n in WEIGHT_ORDER], *[delta[n] for n in WEIGHT_ORDER],
            *[new_m[n] for n in WEIGHT_ORDER], *[new_v[n] for n in WEIGHT_ORDER])
```

```python
import functools
import math

import jax
import jax.numpy as jnp
from jax import lax
from jax.experimental import pallas as pl
from jax.experimental.pallas import tpu as pltpu

F32, BF16 = jnp.float32, jnp.bfloat16
MESH = pl.DeviceIdType.MESH

N_CHIPS = 4
EPS = 1e-6
D_NOPE, D_ROPE, D_V = 128, 64, 128
QK_DIM = D_NOPE + D_ROPE
HEAD_PAD = 256
ROPE_THETA = 10000.0
CONV_WIDTH = 31
CONV_TAPS_PAD = 32
FFN_RESIDUAL_WEIGHT = 0.5
ADAM_LR, ADAM_B1, ADAM_B2, ADAM_EPS, ADAM_WD, ADAM_STEP = 0.001, 0.9, 0.999, 1e-08, 0.01, 10
VMEM_LIMIT_BYTES = 56 * 1024 * 1024
NEG_BIG = -1e30

NN_DIMS = (((1,), (0,)), ((), ()))
NT_DIMS = (((1,), (1,)), ((), ()))
TN_DIMS = (((0,), (0,)), ((), ()))


def _cparams(semantics=None):
    kw = dict(vmem_limit_bytes=VMEM_LIMIT_BYTES)
    if semantics is not None:
        kw["dimension_semantics"] = semantics
    return pltpu.CompilerParams(**kw)


def _tile(n, pref, mult=128):
    if n <= pref:
        return n
    t = (pref // mult) * mult
    while t >= mult:
        if n % t == 0:
            return t
        t -= mult
    return n


def _rowwise(name, fn, rows, vecs, outs, accs=(), tm=256, rc=32):
    T = rows[0].shape[0]
    tm = min(tm, T)
    rc = min(rc, tm)
    nr, nv, no, na = len(rows), len(vecs), len(outs), len(accs)
    steps = tm // rc

    def body(*refs):
        row_refs = refs[:nr]
        vec_refs = refs[nr:nr + nv]
        out_refs = refs[nr + nv:nr + nv + no]
        acc_refs = refs[nr + nv + no:]
        if na:
            @pl.when(pl.program_id(0) == 0)
            def _():
                for a in acc_refs:
                    a[...] = jnp.zeros_like(a)

        def step(r, carry):
            sl = pl.ds(pl.multiple_of(r * rc, rc), rc)
            res = fn(*[x[sl, :] for x in row_refs], *[v[...] for v in vec_refs])
            for o, val in zip(out_refs, res[:no]):
                o[sl, :] = val.astype(o.dtype)
            return tuple(c + val for c, val in zip(carry, res[no:]))

        init = tuple(jnp.zeros(s, F32) for s in accs)
        tot = lax.fori_loop(0, steps, step, init)
        for a, val in zip(acc_refs, tot):
            a[...] += val

    in_specs = [pl.BlockSpec((tm, x.shape[1]), lambda i: (i, 0)) for x in rows]
    in_specs += [pl.BlockSpec(v.shape, lambda i: (0, 0)) for v in vecs]
    out_specs = [pl.BlockSpec((tm, d), lambda i: (i, 0)) for d, _ in outs]
    out_specs += [pl.BlockSpec(s, lambda i: (0, 0)) for s in accs]
    out_shape = [jax.ShapeDtypeStruct((T, d), dt) for d, dt in outs]
    out_shape += [jax.ShapeDtypeStruct(s, F32) for s in accs]
    return pl.pallas_call(
        body, name=name, grid=(T // tm,), in_specs=in_specs, out_specs=out_specs, out_shape=out_shape,
        compiler_params=_cparams(("arbitrary",)),
    )(*rows, *vecs)


def _colsum(v):
    return jnp.sum(v, axis=0, keepdims=True)


def _rstd(x):
    return lax.rsqrt(jnp.mean(x * x, axis=-1, keepdims=True) + EPS)


def _rms_bwd(x, dy, g):
    r = _rstd(x)
    xh = x * r
    dyg = dy * g
    dx = r * (dyg - xh * jnp.mean(dyg * xh, axis=-1, keepdims=True))
    return dx, dy * xh


def _sigmoid(x):
    return 1.0 / (1.0 + jnp.exp(-x))


def rms_fwd(name, h, g):
    def fn(x, gv):
        return ((x * _rstd(x)) * gv,)
    return _rowwise(name, fn, [h], [g], [(h.shape[1], BF16)])[0]


def rms_bwd_res(name, h, d_y, g, d_res):
    D = h.shape[1]

    def fn(x, dy, dr, gv):
        dx, dgr = _rms_bwd(x, dy, gv)
        dh = dr + dx
        return dh, dh, _colsum(dgr)
    return _rowwise(name, fn, [h, d_y, d_res], [g], [(D, F32), (D, BF16)], [(1, D)], tm=128)


def loss_head(name, y, target):
    D = y.shape[1]

    def fn(yv, tv):
        e = yv - tv
        tot = jnp.sum(_colsum(e * e), axis=1, keepdims=True) * (0.5 / D)
        return e * (1.0 / D), jnp.broadcast_to(tot, (1, 128))
    return _rowwise(name, fn, [y, target], [], [(D, F32)], [(1, 128)])


def ple_fwd(name, h, pe, z, g_e):
    def fn(hv, pv, zv, gv):
        return (hv + (pv * _rstd(pv)) * gv * _sigmoid(zv),)
    return _rowwise(name, fn, [h, pe, z], [g_e], [(h.shape[1], F32)], tm=128)[0]


def ple_bwd(name, d_h, pe, z, g_e):
    D = d_h.shape[1]

    def fn(dh, pv, zv, gv):
        gate = _sigmoid(zv)
        e = (pv * _rstd(pv)) * gv
        d_z = dh * e * gate * (1.0 - gate)
        d_pe, dgr = _rms_bwd(pv, dh * gate, gv)
        return d_z, d_pe, _colsum(dgr)
    return _rowwise(name, fn, [d_h, pe, z], [g_e], [(D, BF16), (D, BF16)], [(1, D)], tm=128)


def lat_norm_fwd(name, lat, g_q, g_kv):
    QL, KL = g_q.shape[1], g_kv.shape[1]

    def fn(v, gq, gk):
        a = v[:, :QL]
        b = v[:, QL:QL + KL]
        return (a * _rstd(a)) * gq, (b * _rstd(b)) * gk
    return _rowwise(name, fn, [lat], [g_q, g_kv], [(QL, BF16), (KL, BF16)])


def lat_norm_bwd(name, lat, d_cq, d_ckv, d_krope, g_q, g_kv):
    QL, KL = g_q.shape[1], g_kv.shape[1]

    def fn(v, dq, dk, dr, gq, gk):
        da, ga = _rms_bwd(v[:, :QL], dq, gq)
        db, gb = _rms_bwd(v[:, QL:QL + KL], dk, gk)
        return jnp.concatenate([da, db, dr], axis=-1), _colsum(ga), _colsum(gb)
    return _rowwise(name, fn, [lat, d_cq, d_ckv, d_krope], [g_q, g_kv],
                    [(lat.shape[1], BF16)], [(1, QL), (1, KL)])


def glu_fwd(name, ag):
    C = ag.shape[1] // 2

    def fn(v):
        return (v[:, :C] * _sigmoid(v[:, C:]),)
    return _rowwise(name, fn, [ag], [], [(C, F32)], tm=128)[0]


def glu_bwd(name, ag, d_u):
    C = ag.shape[1] // 2

    def fn(v, du):
        a = v[:, :C]
        s = _sigmoid(v[:, C:])
        d = jnp.concatenate([du * s, du * a * s * (1.0 - s)], axis=-1)
        return d, _colsum(d)
    return _rowwise(name, fn, [ag, d_u], [], [(2 * C, BF16)], [(1, 2 * C)], tm=128)


def ln_silu_fwd(name, yc, g, b):
    def fn(v, gv, bv):
        xc = v - jnp.mean(v, axis=-1, keepdims=True)
        ln = xc * lax.rsqrt(jnp.mean(xc * xc, axis=-1, keepdims=True) + EPS) * gv + bv
        return (ln * _sigmoid(ln),)
    return _rowwise(name, fn, [yc], [g, b], [(yc.shape[1], BF16)], tm=128)[0]


def ln_silu_bwd(name, yc, d_out, g, b):
    C = yc.shape[1]

    def fn(v, do, gv, bv):
        xc = v - jnp.mean(v, axis=-1, keepdims=True)
        r = lax.rsqrt(jnp.mean(xc * xc, axis=-1, keepdims=True) + EPS)
        xh = xc * r
        ln = xh * gv + bv
        s = _sigmoid(ln)
        d_ln = do * s * (1.0 + ln * (1.0 - s))
        dxh = d_ln * gv
        dy = r * (dxh - jnp.mean(dxh, axis=-1, keepdims=True) - xh * jnp.mean(dxh * xh, axis=-1, keepdims=True))
        return dy, _colsum(d_ln * xh), _colsum(d_ln), _colsum(dy)
    return _rowwise(name, fn, [yc, d_out], [g, b], [(C, F32)], [(1, C), (1, C), (1, C)], tm=128)


def add_halves(name, ids, own, got):
    S, Rh, C = got.shape
    tr = _tile(Rh, 512, 16)
    nrb = Rh // tr

    def body(ids_ref, a_ref, b_ref, o_ref):
        o_ref[...] = (a_ref[...].astype(F32) + b_ref[...].astype(F32)).astype(o_ref.dtype)

    gs = pltpu.PrefetchScalarGridSpec(
        num_scalar_prefetch=1, grid=(S - 1, nrb),
        in_specs=[pl.BlockSpec((None, tr, C), lambda s, r, ids: (ids[1] ^ (s + 1), ids[0] * nrb + r, 0)),
                  pl.BlockSpec((None, tr, C), lambda s, r, ids: (ids[1] ^ (s + 1), r, 0))],
        out_specs=pl.BlockSpec((None, tr, C), lambda s, r, ids: (s, r, 0)))
    return pl.pallas_call(
        body, name=name, grid_spec=gs, out_shape=jax.ShapeDtypeStruct((S - 1, Rh, C), BF16),
        compiler_params=_cparams(("arbitrary", "arbitrary")))(ids, own, got)


def adamw_half(name, ids, grad, w, m, v, layer, layers, into):
    own = isinstance(grad, tuple)
    Rh, C = grad[1].shape[1:] if own else grad.shape
    tr = _tile(Rh, 128, 16)
    nrb = Rh // tr
    n_grad = 5 if own else 1
    c1 = 1.0 / (1.0 - ADAM_B1 ** ADAM_STEP)
    c2 = 1.0 / (1.0 - ADAM_B2 ** ADAM_STEP)

    def body(ids_ref, *refs):
        g_refs, (w_ref, m_ref, v_ref) = refs[:n_grad], refs[n_grad:n_grad + 3]
        outs = refs[-5:] if own else refs[-4:]
        go_ref, d_ref, nm_ref, nv_ref = outs[:4]
        if own:
            f = [g[...].astype(F32) for g in g_refs]
            gv = ((f[0] + f[1]) + f[2]) + (f[3] + f[4])
            outs[4][...] = gv
        else:
            gv = g_refs[0][...]
        nm = ADAM_B1 * m_ref[...] + (1.0 - ADAM_B1) * gv
        nv = ADAM_B2 * v_ref[...] + (1.0 - ADAM_B2) * (gv * gv)
        go_ref[...] = gv
        d_ref[...] = -ADAM_LR * ((nm * c1) / (jnp.sqrt(nv * c2) + ADAM_EPS) + ADAM_WD * w_ref[...])
        nm_ref[...] = nm
        nv_ref[...] = nv

    half = (lambda ids: ids[0]) if own else (lambda ids: 1 - ids[0])
    rows = pl.BlockSpec((tr, C), lambda r, ids: ((2 * layer + half(ids)) * nrb + r, 0))
    plain = pl.BlockSpec((tr, C), lambda r, ids: (r, 0))
    if own:
        slot = lambda flip: pl.BlockSpec((None, tr, C), lambda r, ids: (ids[1] ^ flip, r, 0))
        mine = pl.BlockSpec((None, tr, C), lambda r, ids: (ids[1], ids[0] * nrb + r, 0))
        in_specs = [mine, slot(0), slot(1), slot(2), slot(3), rows, rows, rows]
        operands = [ids, grad[0], grad[1], grad[2], grad[2], grad[2], w, m, v]
    else:
        in_specs = [plain, rows, rows, rows]
        operands = [ids, grad, w, m, v]
    aliases = {}
    if into is not None:
        in_specs += [pl.BlockSpec(memory_space=pl.ANY)] * 4
        aliases = {len(operands) + k: k for k in range(4)}
        operands += list(into)
    full = jax.ShapeDtypeStruct((layers * 2 * Rh, C), F32)
    gs = pltpu.PrefetchScalarGridSpec(num_scalar_prefetch=1, grid=(nrb,), in_specs=in_specs,
                                      out_specs=[rows] * 4 + [plain] * own)
    return pl.pallas_call(
        body, name=name, grid_spec=gs, out_shape=[full] * 4 + [jax.ShapeDtypeStruct((Rh, C), F32)] * own,
        input_output_aliases=aliases, compiler_params=_cparams(("arbitrary",)))(*operands)


def cast_into_slot(name, w, chip, layer, layers, after=None):
    R, C = w.shape[0] // layers, w.shape[1]
    tr = _tile(R, 512, 16)
    nrb = R // tr

    def body(s_ref, w_ref, *rest):
        rest[-1][...] = w_ref[...].astype(BF16)

    in_specs = [pl.BlockSpec((tr, C), lambda r, s: (layer * nrb + r, 0))]
    operands = [chip, w]
    if after is not None:
        in_specs.append(pl.BlockSpec(memory_space=pl.ANY))
        operands.append(after)
    gs = pltpu.PrefetchScalarGridSpec(
        num_scalar_prefetch=1, grid=(nrb,), in_specs=in_specs,
        out_specs=pl.BlockSpec((None, tr, C), lambda r, s: (s[0], r, 0)))
    return pl.pallas_call(
        body, name=name, grid_spec=gs, out_shape=jax.ShapeDtypeStruct((N_CHIPS, R, C), BF16),
        compiler_params=_cparams(("arbitrary",)))(*operands)


def adamw(name, w, g, m, v):
    R, C = w.shape
    tr = _tile(R, 256, 8)
    c1 = 1.0 / (1.0 - ADAM_B1 ** ADAM_STEP)
    c2 = 1.0 / (1.0 - ADAM_B2 ** ADAM_STEP)

    def body(w_ref, g_ref, m_ref, v_ref, d_ref, nm_ref, nv_ref):
        gv = g_ref[...]
        nm = ADAM_B1 * m_ref[...] + (1.0 - ADAM_B1) * gv
        nv = ADAM_B2 * v_ref[...] + (1.0 - ADAM_B2) * (gv * gv)
        d_ref[...] = -ADAM_LR * ((nm * c1) / (jnp.sqrt(nv * c2) + ADAM_EPS) + ADAM_WD * w_ref[...])
        nm_ref[...] = nm
        nv_ref[...] = nv

    spec = pl.BlockSpec((tr, C), lambda r: (r, 0))
    return pl.pallas_call(
        body, name=name, grid=(R // tr,), in_specs=[spec] * 4, out_specs=[spec] * 3,
        out_shape=[jax.ShapeDtypeStruct((R, C), F32)] * 3, compiler_params=_cparams(("arbitrary",)))(w, g, m, v)


def _matmul(name, a, b, *, grid, a_blk, a_map, b_blk, b_map, o_shape, o_dtype, o_blk, o_map, dims,
            scale=None, res=None, bias=None, bias_blk=None, bias_map=None, alias_into=None, after=None):
    nk = grid[2]
    has_res, has_bias, has_into = res is not None, bias is not None, alias_into is not None
    acc_shape = tuple(d for d in o_blk if d is not None)

    def body(*refs):
        a_ref, b_ref = refs[0], refs[1]
        pos = 2
        res_ref = bias_ref = None
        if has_res:
            res_ref = refs[pos]
            pos += 1
        if has_bias:
            bias_ref = refs[pos]
            pos += 1
        if has_into:
            pos += 1
        if after is not None:
            pos += 1
        o_ref = refs[pos]
        av, bv = a_ref[...], b_ref[...]
        if bv.ndim == 3:
            bv = bv.reshape(-1, bv.shape[-1])
        if av.dtype != BF16:
            av = av.astype(BF16)
        if bv.dtype != BF16:
            bv = bv.astype(BF16)
        part = lax.dot_general(av, bv, dims, preferred_element_type=F32)

        def finish(acc):
            if scale is not None:
                acc = acc * scale
            if has_bias:
                acc = acc + bias_ref[...]
            if has_res:
                acc = acc + res_ref[...]
            o_ref[...] = acc.astype(o_ref.dtype)

        if nk == 1:
            finish(part)
        else:
            acc_ref = refs[pos + 1]
            k = pl.program_id(2)

            @pl.when(k == 0)
            def _():
                acc_ref[...] = part

            @pl.when(k > 0)
            def _():
                acc_ref[...] += part

            @pl.when(k == nk - 1)
            def _():
                finish(acc_ref[...])

    operands = [a, b]
    in_specs = [pl.BlockSpec(a_blk, a_map), pl.BlockSpec(b_blk, b_map)]
    if has_res:
        operands.append(res)
        in_specs.append(pl.BlockSpec(o_blk, o_map))
    if has_bias:
        operands.append(bias)
        in_specs.append(pl.BlockSpec(bias_blk, bias_map))
    aliases = {}
    if has_into:
        aliases = {len(operands): 0}
        operands.append(alias_into)
        in_specs.append(pl.BlockSpec(memory_space=pl.ANY))
    if after is not None:
        operands.append(after)
        in_specs.append(pl.BlockSpec(memory_space=pl.ANY))
    return pl.pallas_call(
        body, name=name, grid=grid, in_specs=in_specs, out_specs=pl.BlockSpec(o_blk, o_map),
        out_shape=jax.ShapeDtypeStruct(o_shape, o_dtype),
        scratch_shapes=[pltpu.VMEM(acc_shape, F32)] if nk > 1 else [],
        input_output_aliases=aliases,
        compiler_params=_cparams(("parallel", "parallel", "arbitrary")),
    )(*operands)


def mm_nn(name, a, w3, kind, layer, *, out_dtype, scale=None, res=None, bias=None, tm=1024, tn=512, tk=2048,
          after=None, whole_k=False):
    M, K = a.shape
    S, _, C = w3.shape
    tm = _tile(M, tm, 16)
    b_blk = None
    if kind == "col":
        N = S * C
        tk, tn = _tile(K, tk), _tile(C, tn)
        kb, nb = K // tk, C // tn
        b_map = lambda n, m, k: (n // nb, layer * kb + k, n % nb)
    elif whole_k:
        N, K4, tk, tn = C, K // S, K, _tile(C, tn)
        b_blk, b_map = (S, K4, tn), lambda n, m, k: (0, layer, n)
    else:
        N, K4 = C, K // S
        tk, tn = _tile(K4, tk), _tile(C, tn)
        kb4 = K4 // tk
        b_map = lambda n, m, k: (k // kb4, layer * kb4 + k % kb4, n)
    return _matmul(
        name, a, w3, grid=(N // tn, M // tm, K // tk),
        a_blk=(tm, tk), a_map=lambda n, m, k: (m, k), b_blk=b_blk or (None, tk, tn), b_map=b_map,
        o_shape=(M, N), o_dtype=out_dtype, o_blk=(tm, tn), o_map=lambda n, m, k: (m, n), dims=NN_DIMS,
        scale=scale, res=res, bias=bias, bias_blk=(1, tn), bias_map=lambda n, m, k: (0, n), after=after)


def _mm_nt(name, g, w3, kind, layer, K, *, out_dtype, scale=None, tm=1024, to=1024, tc=1408, after=None):
    S, _, C = w3.shape
    tc = _tile(C, tc)
    if g.ndim == 3:
        M, N = g.shape[1], 2 * g.shape[2]
        tm = _tile(M, tm, 16)
        cb = g.shape[2] // tc
        a_blk, a_map = (None, tm, tc), lambda o, m, c: (c // cb, m, c % cb)
    else:
        M, N = g.shape
        tm = _tile(M, tm, 16)
        a_blk, a_map = (tm, tc), lambda o, m, c: (m, c)
    if kind == "col":
        nb = C // tc
        to = _tile(K, to)
        ob = K // to
        b_map = lambda o, m, c: (c // nb, layer * ob + o, c % nb)
    else:
        K4 = K // S
        to = _tile(K4, to)
        ob4 = K4 // to
        b_map = lambda o, m, c: (o // ob4, layer * ob4 + o % ob4, c)
    return _matmul(
        name, g, w3, grid=(K // to, M // tm, N // tc),
        a_blk=a_blk, a_map=a_map, b_blk=(None, to, tc), b_map=b_map,
        o_shape=(M, K), o_dtype=out_dtype, o_blk=(tm, to), o_map=lambda o, m, c: (m, o), dims=NT_DIMS,
        scale=scale, after=after)


def mm_tn(name, a, g, kind, *, shards=N_CHIPS, layer=0, layers=1, into=None, out_dtype=BF16, scale=None,
          tk=1024, tn=1408, tm=2048, after=None):
    M, K = a.shape
    N = 2 * g.shape[2] if g.ndim == 3 else g.shape[1]
    tm = _tile(M, tm, 16)
    if kind == "col":
        C = N // shards
        tk, tn = _tile(K, tk), _tile(C, tn)
        kb, nb = K // tk, C // tn
        o_shape = (shards, layers * K, C)
        o_map = lambda k, n, m: (n // nb, layer * kb + k, n % nb)
    else:
        K4 = K // shards
        tk, tn = _tile(K4, tk), _tile(N, tn)
        kb4 = K4 // tk
        o_shape = (shards, layers * K4, N)
        o_map = lambda k, n, m: (k // kb4, layer * kb4 + k % kb4, n)
    if g.ndim == 3:
        nbh = g.shape[2] // tn
        g_blk, g_map = (None, tm, tn), lambda k, n, m: (n // nbh, m, n % nbh)
    else:
        g_blk, g_map = (tm, tn), lambda k, n, m: (m, n)
    return _matmul(
        name, a, g, grid=(K // tk, N // tn, M // tm),
        a_blk=(tm, tk), a_map=lambda k, n, m: (m, k), b_blk=g_blk, b_map=g_map,
        o_shape=o_shape, o_dtype=out_dtype, o_blk=(None, tk, tn), o_map=o_map, dims=TN_DIMS,
        scale=scale, alias_into=into, after=after)


def ffn_in_act(name, hn, w3, *, tn=256, after=None):
    M, K = hn.shape
    S, _, C = w3.shape
    tn = _tile(C, tn)
    nb = C // tn

    def body(a_ref, bg_ref, bu_ref, *rest):
        gu_ref, act_ref = rest[-2:]
        a = a_ref[...]
        g = jnp.dot(a, bg_ref[...], preferred_element_type=F32)
        u = jnp.dot(a, bu_ref[...], preferred_element_type=F32)
        gu_ref[0] = g.astype(BF16)
        gu_ref[1] = u.astype(BF16)
        act_ref[...] = (g * _sigmoid(g) * u).astype(BF16)

    in_specs = [pl.BlockSpec((M, K), lambda j: (0, 0)),
                pl.BlockSpec((None, K, tn), lambda j: (j // nb, 0, j % nb)),
                pl.BlockSpec((None, K, tn), lambda j: (S // 2 + j // nb, 0, j % nb))]
    operands = [hn, w3, w3]
    if after is not None:
        in_specs.append(pl.BlockSpec(memory_space=pl.ANY))
        operands.append(after)
    width = S // 2 * C
    return pl.pallas_call(
        body, name=name, grid=(width // tn,), in_specs=in_specs,
        out_specs=[pl.BlockSpec((2, M, tn), lambda j: (0, 0, j)), pl.BlockSpec((M, tn), lambda j: (0, j))],
        out_shape=[jax.ShapeDtypeStruct((2, M, width), BF16), jax.ShapeDtypeStruct((M, width), BF16)],
        compiler_params=_cparams(("arbitrary",)))(*operands)


def ffn_dact_dgu(name, d_out, w3, gu, scale, *, tm=512, after=None):
    M, N = d_out.shape
    S, K4, _ = w3.shape
    tm = _tile(M, tm, 16)
    to = K4
    chunk = 256 if to > 256 else to

    def body(a_ref, b_ref, gu_ref, *rest):
        o_ref = rest[-1]
        a = a_ref[...]
        for c0 in range(0, to, chunk):
            cols = slice(c0, min(c0 + chunk, to))
            da = lax.dot_general(a, b_ref[cols, :], NT_DIMS, preferred_element_type=F32) * scale
            g = gu_ref[0, :, cols].astype(F32)
            u = gu_ref[1, :, cols].astype(F32)
            s = _sigmoid(g)
            o_ref[0, :, cols] = (da * u * s * (1.0 + g * (1.0 - s))).astype(BF16)
            o_ref[1, :, cols] = (da * g * s).astype(BF16)

    halves = pl.BlockSpec((2, tm, to), lambda o, m: (0, m, o))
    in_specs = [pl.BlockSpec((tm, N), lambda o, m: (m, 0)), pl.BlockSpec((None, to, N), lambda o, m: (o, 0, 0)), halves]
    operands = [d_out, w3, gu]
    if after is not None:
        in_specs.append(pl.BlockSpec(memory_space=pl.ANY))
        operands.append(after)
    return pl.pallas_call(
        body, name=name, grid=(S, M // tm), in_specs=in_specs, out_specs=halves,
        out_shape=jax.ShapeDtypeStruct((2, M, S * K4), BF16),
        compiler_params=_cparams(("parallel", "arbitrary")))(*operands)


def rope_tables(name, pos):
    T = pos.shape[0]
    half = D_ROPE // 2

    def body(p_ref, c_ref, s1_ref, s2_ref):
        lane = lax.broadcasted_iota(jnp.int32, (T, 128), 1)
        idx = (lane & (half - 1)).astype(F32)
        ang = p_ref[...] * jnp.exp(idx * (-2.0 * math.log(ROPE_THETA) / D_ROPE))
        cs, sn = jnp.cos(ang), jnp.sin(ang)
        c_ref[...] = jnp.where(lane < D_ROPE, cs, 0.0)
        s1_ref[...] = jnp.where(lane < half, -sn, 0.0)
        s2_ref[...] = jnp.where((lane >= half) & (lane < D_ROPE), sn, 0.0)

    return pl.pallas_call(body, name=name, out_shape=[jax.ShapeDtypeStruct((T, 128), F32)] * 3,
                          compiler_params=_cparams())(pos)


def _rope(v, cs, s1, s2):
    return v * cs + pltpu.roll(v, 128 - D_ROPE // 2, 1) * s1 + pltpu.roll(v, D_ROPE // 2, 1) * s2


def _rope_bwd(d, cs, s1, s2):
    return d * cs + pltpu.roll(d * s1, D_ROPE // 2, 1) + pltpu.roll(d * s2, 128 - D_ROPE // 2, 1)


def _head_rstd(n, r):
    ms = jnp.sum(n * n + r * r, axis=-1, keepdims=True) * (1.0 / QK_DIM)
    return lax.rsqrt(ms + EPS)


def mla_prep_fwd(name, q_raw, kv_raw, lat, tabs, q_gain, k_gain, rope_col, tm=128):
    T = q_raw.shape[0]
    H = q_raw.shape[1] // HEAD_PAD
    tm = min(tm, T)
    rope_blk = rope_col // 128

    def body(q_ref, kv_ref, kr_ref, c_ref, s1_ref, s2_ref, qg_ref, kg_ref, Q_ref, K_ref, V_ref):
        cs, s1, s2 = c_ref[...], s1_ref[...], s2_ref[...]
        qg, kg = qg_ref[...], kg_ref[...]
        kr = kr_ref[...]
        for h in range(H):
            lo = HEAD_PAD * h
            n, r = q_ref[:, lo:lo + 128], q_ref[:, lo + 128:lo + 256]
            rs = _head_rstd(n, r)
            Q_ref[h, :, 0:128] = (n * rs * qg[:, :128]).astype(BF16)
            Q_ref[h, :, 128:256] = _rope(r * rs * qg[:, 128:], cs, s1, s2).astype(BF16)
            n = kv_ref[:, lo:lo + 128]
            rs = _head_rstd(n, kr)
            K_ref[h, :, 0:128] = (n * rs * kg[:, :128]).astype(BF16)
            K_ref[h, :, 128:256] = _rope(kr * rs * kg[:, 128:], cs, s1, s2).astype(BF16)
            V_ref[h] = kv_ref[:, lo + 128:lo + 256].astype(BF16)

    row = lambda w: pl.BlockSpec((tm, w), lambda i: (i, 0))
    vec = pl.BlockSpec((1, HEAD_PAD), lambda i: (0, 0))
    return pl.pallas_call(
        body, name=name, grid=(T // tm,),
        in_specs=[row(H * HEAD_PAD), row(H * HEAD_PAD), pl.BlockSpec((tm, 128), lambda i: (i, rope_blk)),
                  row(128), row(128), row(128), vec, vec],
        out_specs=[pl.BlockSpec((H, tm, HEAD_PAD), lambda i: (0, i, 0))] * 2 + [pl.BlockSpec((H, tm, D_V), lambda i: (0, i, 0))],
        out_shape=[jax.ShapeDtypeStruct((H, T, HEAD_PAD), BF16)] * 2 + [jax.ShapeDtypeStruct((H, T, D_V), BF16)],
        compiler_params=_cparams(("arbitrary",)),
    )(q_raw, kv_raw, lat, *tabs, q_gain, k_gain)


def mla_prep_bwd(name, dQ, dK, dV, q_raw, kv_raw, lat, tabs, q_gain, k_gain, rope_col, tm=128):
    T = q_raw.shape[0]
    H = q_raw.shape[1] // HEAD_PAD
    tm = min(tm, T)
    rope_blk = rope_col // 128

    def body(dQ_ref, dK_ref, dV_ref, q_ref, kv_ref, kr_ref, c_ref, s1_ref, s2_ref, qg_ref, kg_ref,
             dq_ref, dkv_ref, dkr_ref, dqg_ref, dkg_ref):
        @pl.when(pl.program_id(0) == 0)
        def _():
            dqg_ref[...] = jnp.zeros_like(dqg_ref)
            dkg_ref[...] = jnp.zeros_like(dkg_ref)

        cs, s1, s2 = c_ref[...], s1_ref[...], s2_ref[...]
        qg, kg = qg_ref[...], kg_ref[...]
        kr = kr_ref[...]
        dkr = jnp.zeros((tm, 128), F32)
        gq_n = jnp.zeros((1, 128), F32)
        gq_r = jnp.zeros((1, 128), F32)
        gk_n = jnp.zeros((1, 128), F32)
        gk_r = jnp.zeros((1, 128), F32)

        def norm_bwd(n, r, dn, dr, gain):
            rs = _head_rstd(n, r)
            nh, rh = n * rs, r * rs
            dng, drg = dn * gain[:, :128], dr * gain[:, 128:]
            mean = jnp.sum(dng * nh + drg * rh, axis=-1, keepdims=True) * (1.0 / QK_DIM)
            return rs * (dng - nh * mean), rs * (drg - rh * mean), _colsum(dn * nh), _colsum(dr * rh)

        for h in range(H):
            lo = HEAD_PAD * h
            n, r = q_ref[:, lo:lo + 128], q_ref[:, lo + 128:lo + 256]
            dn = dQ_ref[h, :, 0:128].astype(F32)
            dr = _rope_bwd(dQ_ref[h, :, 128:256].astype(F32), cs, s1, s2)
            a, b, g1, g2 = norm_bwd(n, r, dn, dr, qg)
            dq_ref[:, lo:lo + 128] = a.astype(BF16)
            dq_ref[:, lo + 128:lo + 256] = b.astype(BF16)
            gq_n, gq_r = gq_n + g1, gq_r + g2
            n = kv_ref[:, lo:lo + 128]
            dn = dK_ref[h, :, 0:128].astype(F32)
            dr = _rope_bwd(dK_ref[h, :, 128:256].astype(F32), cs, s1, s2)
            a, b, g1, g2 = norm_bwd(n, kr, dn, dr, kg)
            dkv_ref[:, lo:lo + 128] = a.astype(BF16)
            dkv_ref[:, lo + 128:lo + 256] = dV_ref[h].astype(BF16)
            dkr = dkr + b
            gk_n, gk_r = gk_n + g1, gk_r + g2
        dkr_ref[...] = dkr
        dqg_ref[:, 0:128] += gq_n
        dqg_ref[:, 128:256] += gq_r
        dkg_ref[:, 0:128] += gk_n
        dkg_ref[:, 128:256] += gk_r

    row = lambda w: pl.BlockSpec((tm, w), lambda i: (i, 0))
    vec = pl.BlockSpec((1, HEAD_PAD), lambda i: (0, 0))
    hd = lambda w: pl.BlockSpec((H, tm, w), lambda i: (0, i, 0))
    return pl.pallas_call(
        body, name=name, grid=(T // tm,),
        in_specs=[hd(HEAD_PAD), hd(HEAD_PAD), hd(D_V), row(H * HEAD_PAD), row(H * HEAD_PAD),
                  pl.BlockSpec((tm, 128), lambda i: (i, rope_blk)), row(128), row(128), row(128), vec, vec],
        out_specs=[row(H * HEAD_PAD), row(H * HEAD_PAD), row(128), vec, vec],
        out_shape=[jax.ShapeDtypeStruct((T, H * HEAD_PAD), BF16)] * 2 + [jax.ShapeDtypeStruct((T, 128), F32)]
        + [jax.ShapeDtypeStruct((1, HEAD_PAD), F32)] * 2,
        compiler_params=_cparams(("arbitrary",)),
    )(dQ, dK, dV, q_raw, kv_raw, lat, *tabs, q_gain, k_gain)


def _causal_probs(q, k, scale, row0):
    s = lax.dot_general(q, k, NT_DIMS, preferred_element_type=F32) * scale
    row = row0 + lax.broadcasted_iota(jnp.int32, s.shape, 0)
    col = lax.broadcasted_iota(jnp.int32, s.shape, 1)
    s = jnp.where(col <= row, s, NEG_BIG)
    p = jnp.exp(s - jnp.max(s, axis=-1, keepdims=True))
    return p, jnp.sum(p, axis=-1, keepdims=True)


def attn_fwd(name, Q, K, V, tq=512):
    H, T, E = Q.shape
    tq = min(tq, T)
    nq = T // tq
    scale = QK_DIM ** -0.5

    def body(q_ref, k_ref, v_ref, o_ref):
        i = pl.program_id(1)
        for ib in range(nq):
            @pl.when(i == ib)
            def _():
                n = (ib + 1) * tq
                p, l = _causal_probs(q_ref[...], k_ref[0:n, :], scale, ib * tq)
                o = jnp.dot(p.astype(BF16), v_ref[0:n, :], preferred_element_type=F32)
                o_ref[...] = (o / l).astype(o_ref.dtype)

    return pl.pallas_call(
        body, name=name, grid=(H, nq),
        in_specs=[pl.BlockSpec((None, tq, E), lambda h, i: (h, i, 0)),
                  pl.BlockSpec((None, T, E), lambda h, i: (h, 0, 0)),
                  pl.BlockSpec((None, T, D_V), lambda h, i: (h, 0, 0))],
        out_specs=pl.BlockSpec((tq, D_V), lambda h, i: (i, h)),
        out_shape=jax.ShapeDtypeStruct((T, H * D_V), BF16),
        compiler_params=_cparams(("parallel", "arbitrary")),
    )(Q, K, V)


def attn_bwd(name, Q, K, V, dO, tq=512):
    H, T, E = Q.shape
    tq = min(tq, T)
    nq = T // tq
    scale = QK_DIM ** -0.5

    def body(q_ref, k_ref, v_ref, do_ref, dq_ref, dk_ref, dv_ref):
        i = pl.program_id(1)

        @pl.when(i == 0)
        def _():
            dk_ref[...] = jnp.zeros_like(dk_ref)
            dv_ref[...] = jnp.zeros_like(dv_ref)

        for ib in range(nq):
            @pl.when(i == ib)
            def _():
                n = (ib + 1) * tq
                q, k, v, do = q_ref[...], k_ref[0:n, :], v_ref[0:n, :], do_ref[...]
                p, l = _causal_probs(q, k, scale, ib * tq)
                p = p / l
                dp = lax.dot_general(do, v, NT_DIMS, preferred_element_type=F32)
                ds = p * (dp - jnp.sum(p * dp, axis=-1, keepdims=True)) * scale
                dsb, pb = ds.astype(BF16), p.astype(BF16)
                dq_ref[...] = jnp.dot(dsb, k, preferred_element_type=F32)
                dk_ref[0:n, :] += lax.dot_general(dsb, q, TN_DIMS, preferred_element_type=F32)
                dv_ref[0:n, :] += lax.dot_general(pb, do, TN_DIMS, preferred_element_type=F32)

    return pl.pallas_call(
        body, name=name, grid=(H, nq),
        in_specs=[pl.BlockSpec((None, tq, E), lambda h, i: (h, i, 0)),
                  pl.BlockSpec((None, T, E), lambda h, i: (h, 0, 0)),
                  pl.BlockSpec((None, T, D_V), lambda h, i: (h, 0, 0)),
                  pl.BlockSpec((tq, D_V), lambda h, i: (i, h))],
        out_specs=[pl.BlockSpec((None, tq, E), lambda h, i: (h, i, 0)),
                   pl.BlockSpec((None, T, E), lambda h, i: (h, 0, 0)),
                   pl.BlockSpec((None, T, D_V), lambda h, i: (h, 0, 0))],
        out_shape=[jax.ShapeDtypeStruct((H, T, E), F32)] * 2 + [jax.ShapeDtypeStruct((H, T, D_V), F32)],
        compiler_params=_cparams(("parallel", "arbitrary")),
    )(Q, K, V, dO)


def _taps_by_residue(offsets):
    groups = {}
    for k, off in enumerate(offsets):
        groups.setdefault(off % 8, []).append((k, off - off % 8))
    return groups


def _dw_specs(T, C, tm, tc, halo):
    cur = pl.BlockSpec((tm, tc), lambda j, i: (i, j))
    last = T // tm - 1
    if halo == "prev":
        nbr = pl.BlockSpec((tm, tc), lambda j, i: (jnp.maximum(i - 1, 0), j))
    else:
        nbr = pl.BlockSpec((tm, tc), lambda j, i: (jnp.minimum(i + 1, last), j))
    return cur, nbr


def dwconv_fwd(name, u, w, b, tm=256, tc=512, rs=32):
    T, C = u.shape
    tm, tc = min(tm, T), min(tc, C)
    cur, prev = _dw_specs(T, C, tm, tc, "prev")

    def body(up_ref, uc_ref, w_ref, b_ref, o_ref, scr):
        i = pl.program_id(1)

        @pl.when(i == 0)
        def _():
            scr[pl.ds(0, tm), :] = jnp.zeros((tm, tc), F32)

        @pl.when(i > 0)
        def _():
            scr[pl.ds(0, tm), :] = up_ref[...]

        scr[pl.ds(tm, tm), :] = uc_ref[...]
        for s in range(tm // rs):
            acc = jnp.broadcast_to(b_ref[...], (rs, tc))
            for k in range(CONV_WIDTH):
                acc = acc + w_ref[pl.ds(k, 1), :] * scr[pl.ds(tm - (CONV_WIDTH - 1) + k + rs * s, rs), :]
            o_ref[pl.ds(rs * s, rs), :] = acc

    return pl.pallas_call(
        body, name=name, grid=(C // tc, T // tm),
        in_specs=[prev, cur, pl.BlockSpec((CONV_TAPS_PAD, tc), lambda j, i: (0, j)), pl.BlockSpec((1, tc), lambda j, i: (0, j))],
        out_specs=cur, out_shape=jax.ShapeDtypeStruct((T, C), F32),
        scratch_shapes=[pltpu.VMEM((2 * tm, tc), F32)], compiler_params=_cparams(("parallel", "arbitrary")),
    )(u, u, w, b)


def dwconv_bwd_u(name, dy, w, tm=256, tc=512, rs=32):
    T, C = dy.shape
    tm, tc = min(tm, T), min(tc, C)
    cur, nxt = _dw_specs(T, C, tm, tc, "next")
    last = T // tm - 1

    taps = _taps_by_residue([(CONV_WIDTH - 1) - k for k in range(CONV_WIDTH)])

    def body(dc_ref, dn_ref, w_ref, o_ref, scr, tmp):
        i = pl.program_id(1)
        scr[pl.ds(0, tm), :] = dc_ref[...]
        scr[pl.ds(2 * tm, 8), :] = jnp.zeros((8, tc), F32)

        @pl.when(i == last)
        def _():
            scr[pl.ds(tm, tm), :] = jnp.zeros((tm, tc), F32)

        @pl.when(i < last)
        def _():
            scr[pl.ds(tm, tm), :] = dn_ref[...]

        for s in range(tm // rs):
            acc = jnp.zeros((rs, tc), F32)
            for r, group in taps.items():
                a = jnp.zeros((rs + 8, tc), F32)
                for k, base in group:
                    a = a + w_ref[pl.ds(k, 1), :] * scr[pl.ds(base + rs * s, rs + 8), :]
                if r == 0:
                    acc = acc + a[:rs]
                else:
                    tmp[...] = a
                    acc = acc + tmp[pl.ds(r, rs), :]
            o_ref[pl.ds(rs * s, rs), :] = acc

    return pl.pallas_call(
        body, name=name, grid=(C // tc, T // tm),
        in_specs=[cur, nxt, pl.BlockSpec((CONV_TAPS_PAD, tc), lambda j, i: (0, j))],
        out_specs=cur, out_shape=jax.ShapeDtypeStruct((T, C), F32),
        scratch_shapes=[pltpu.VMEM((2 * tm + 8, tc), F32), pltpu.VMEM((rs + 8, tc), F32)],
        compiler_params=_cparams(("parallel", "arbitrary")),
    )(dy, dy, w)


def dwconv_bwd_w(name, u, dy, tm=256, tc=512, rs=32):
    T, C = u.shape
    tm, tc = min(tm, T), min(tc, C)
    cur, prev = _dw_specs(T, C, tm, tc, "prev")

    taps = _taps_by_residue([tm - (CONV_WIDTH - 1) + k for k in range(CONV_WIDTH)])
    span = tm + 8

    def body(up_ref, uc_ref, dy_ref, o_ref, scr, dyp, dys):
        i = pl.program_id(1)

        @pl.when(i == 0)
        def _():
            scr[pl.ds(0, tm), :] = jnp.zeros((tm, tc), F32)
            o_ref[...] = jnp.zeros_like(o_ref)

        @pl.when(i > 0)
        def _():
            scr[pl.ds(0, tm), :] = up_ref[...]

        scr[pl.ds(tm, tm), :] = uc_ref[...]
        scr[pl.ds(2 * tm, 8), :] = jnp.zeros((8, tc), F32)
        dyp[pl.ds(0, 8), :] = jnp.zeros((8, tc), F32)
        dyp[pl.ds(8, tm), :] = dy_ref[...]
        dyp[pl.ds(8 + tm, 8), :] = jnp.zeros((8, tc), F32)
        for r in taps:
            dys[r] = dyp[pl.ds(8 - r, span), :]
        for r, group in taps.items():
            for k, base in group:
                acc = jnp.zeros((8, tc), F32)
                for s in range(span // 8):
                    acc = acc + dys[r, pl.ds(8 * s, 8), :] * scr[pl.ds(base + 8 * s, 8), :]
                o_ref[pl.ds(k, 1), :] += _colsum(acc)

    return pl.pallas_call(
        body, name=name, grid=(C // tc, T // tm),
        in_specs=[prev, cur, cur],
        out_specs=pl.BlockSpec((CONV_TAPS_PAD, tc), lambda j, i: (0, j)),
        out_shape=jax.ShapeDtypeStruct((CONV_TAPS_PAD, C), F32),
        scratch_shapes=[pltpu.VMEM((2 * tm + 8, tc), F32), pltpu.VMEM((tm + 16, tc), F32),
                        pltpu.VMEM((8, span, tc), F32)],
        compiler_params=_cparams(("parallel", "arbitrary")),
    )(u, u, dy)


def _place():
    x, y, c = lax.axis_index("x"), lax.axis_index("y"), lax.axis_index("c")
    return x, y, c


def _other_chips(x, y):
    return [(1 - x, y, 2 * (1 - x) + y), (x, 1 - y, 2 * x + (1 - y)), (1 - x, 1 - y, 2 * (1 - x) + (1 - y))]


def _hbm_specs(n):
    return [pl.BlockSpec(memory_space=pl.ANY)] * n


HBM_SPEC = pl.BlockSpec(memory_space=pltpu.HBM)
SEM_SPEC = pl.BlockSpec(memory_space=pltpu.SEMAPHORE)
ANY_SPEC = pl.BlockSpec(memory_space=pl.ANY)
SIDE_EFFECT = pltpu.SideEffectType.DATAFLOW_SIDE_EFFECTING


def _half(ref, slot, which):
    rh = ref.shape[1] // 2
    return ref.at[slot, pl.ds(pl.multiple_of(which * rh, 16), rh), :]


def _hbm(a):
    return pltpu.with_memory_space_constraint(a, pltpu.HBM)


def gather_start(name, groups, after):
    flat = [b for g in groups for b in g]
    n, ng = len(flat), len(groups)

    def body(*refs):
        send, recv = refs[n + 1:n + 1 + ng], refs[n + 1 + ng:n + 1 + 2 * ng]
        out, token = refs[n + 1 + 2 * ng:2 * n + 1 + 2 * ng], refs[2 * n + 1 + 2 * ng]
        token[...] = jnp.zeros_like(token)
        x, y, c = _place()
        me = 2 * x + y
        a = 0
        for g, grp in enumerate(groups):
            for k in range(len(grp)):
                piece = _half(out[a], me, c)
                for j, (px, py, _) in enumerate(_other_chips(x, y)):
                    pltpu.make_async_remote_copy(
                        src_ref=piece, dst_ref=piece, send_sem=send[g].at[3 * k + j], recv_sem=recv[g].at[3 * k + j],
                        device_id=(px, py, c), device_id_type=MESH).start()
                a += 1

    sems = [pltpu.SemaphoreType.DMA((3 * len(g),)) for g in groups]
    res = pl.pallas_call(
        body, name=name, in_specs=[HBM_SPEC] * n + [ANY_SPEC],
        out_specs=[SEM_SPEC] * (2 * ng) + [HBM_SPEC] * n + [pl.BlockSpec(memory_space=pltpu.VMEM)],
        out_shape=sems + sems + [pltpu.HBM(b.shape, b.dtype) for b in flat] + [jax.ShapeDtypeStruct((8, 128), F32)],
        input_output_aliases={a: 2 * ng + a for a in range(n)},
        compiler_params=pltpu.CompilerParams(has_side_effects=SIDE_EFFECT),
    )(*[_hbm(b) for b in flat], after)
    send, recv, bufs = res[:ng], res[ng:2 * ng], list(res[2 * ng:2 * ng + n])
    out, a = [], 0
    for g, grp in enumerate(groups):
        out.append((send[g], recv[g], bufs[a:a + len(grp)]))
        a += len(grp)
    return out, res[2 * ng + n]


def gather_relay(name, started, after):
    send1, recv1, bufs = started
    n = len(bufs)

    def body(*refs):
        s1, r1 = refs[n], refs[n + 1]
        s2, r2, out, token = refs[n + 3], refs[n + 4], refs[n + 5:2 * n + 5], refs[2 * n + 5]
        x, y, c = _place()
        me = 2 * x + y
        chips = _other_chips(x, y)
        for k in range(n):
            for j, (px, py, idx) in enumerate(chips):
                cp = pltpu.make_async_remote_copy(
                    src_ref=_half(out[k], me, c), dst_ref=_half(out[k], idx, c), send_sem=s1.at[3 * k + j],
                    recv_sem=r1.at[3 * k + j], device_id=(px, py, c), device_id_type=MESH)
                cp.wait_send()
                cp.wait_recv()
        for k in range(n):
            for j, (px, py, idx) in enumerate(chips):
                piece = _half(out[k], idx, c)
                pltpu.make_async_remote_copy(
                    src_ref=piece, dst_ref=piece, send_sem=s2.at[3 * k + j], recv_sem=r2.at[3 * k + j],
                    device_id=(x, y, 1 - c), device_id_type=MESH).start()
        token[...] = jnp.zeros_like(token)

    sem = pltpu.SemaphoreType.DMA((3 * n,))
    res = pl.pallas_call(
        body, name=name, in_specs=[HBM_SPEC] * n + [SEM_SPEC, SEM_SPEC, ANY_SPEC],
        out_specs=[SEM_SPEC, SEM_SPEC] + [HBM_SPEC] * n + [pl.BlockSpec(memory_space=pltpu.VMEM)],
        out_shape=[sem, sem] + [pltpu.HBM(b.shape, b.dtype) for b in bufs] + [jax.ShapeDtypeStruct((8, 128), F32)],
        input_output_aliases={a: 2 + a for a in range(n)},
        compiler_params=pltpu.CompilerParams(has_side_effects=SIDE_EFFECT),
    )(*bufs, send1, recv1, after)
    return res[0], res[1], list(res[2:2 + n]), res[2 + n]


def gather_wait(name, relayed, after):
    send2, recv2, bufs, _ = relayed
    n = len(bufs)

    def body(*refs):
        s2, r2, out = refs[n], refs[n + 1], refs[n + 3:]
        x, y, c = _place()
        for k in range(n):
            for j, (px, py, idx) in enumerate(_other_chips(x, y)):
                cp = pltpu.make_async_remote_copy(
                    src_ref=_half(out[k], idx, c), dst_ref=_half(out[k], idx, 1 - c), send_sem=s2.at[3 * k + j],
                    recv_sem=r2.at[3 * k + j], device_id=(x, y, 1 - c), device_id_type=MESH)
                cp.wait_send()
                cp.wait_recv()

    res = pl.pallas_call(
        body, name=name, in_specs=[HBM_SPEC] * n + [SEM_SPEC, SEM_SPEC, ANY_SPEC], out_specs=[HBM_SPEC] * n,
        out_shape=[pltpu.HBM(b.shape, b.dtype) for b in bufs], input_output_aliases={a: a for a in range(n)},
        compiler_params=pltpu.CompilerParams(has_side_effects=SIDE_EFFECT),
    )(*bufs, send2, recv2, after)
    return list(res)


def split_start(name, parts, after):
    bufs = [b for p in parts for b in p[0]]
    land_shapes = [s for p in parts for s in p[1]]
    nb, nl = len(bufs), len(land_shapes)

    def body(*refs):
        send, recv = refs[nb + 1], refs[nb + 2]
        out, lands, token = refs[nb + 3:2 * nb + 3], refs[2 * nb + 3:2 * nb + 3 + nl], refs[2 * nb + 3 + nl]
        x, y, c = _place()
        b0 = l0 = k0 = 0
        for p_bufs, p_lands, n_copies, plan in parts:
            copies = plan(out[b0:b0 + len(p_bufs)], lands[l0:l0 + len(p_lands)], x, y, c)
            for k, (src, dst, to, _) in enumerate(copies):
                pltpu.make_async_remote_copy(src_ref=src, dst_ref=dst, send_sem=send.at[k0 + k],
                                             recv_sem=recv.at[k0 + k], device_id=to, device_id_type=MESH).start()
            b0, l0, k0 = b0 + len(p_bufs), l0 + len(p_lands), k0 + n_copies
        token[...] = jnp.zeros_like(token)

    sem = pltpu.SemaphoreType.DMA((sum(p[2] for p in parts),))
    res = pl.pallas_call(
        body, name=name, in_specs=[HBM_SPEC] * nb + [ANY_SPEC],
        out_specs=[SEM_SPEC, SEM_SPEC] + [HBM_SPEC] * (nb + nl) + [pl.BlockSpec(memory_space=pltpu.VMEM)],
        out_shape=[sem, sem] + [pltpu.HBM(b.shape, b.dtype) for b in bufs]
        + [pltpu.HBM(s, d) for s, d in land_shapes] + [jax.ShapeDtypeStruct((8, 128), F32)],
        input_output_aliases={a: 2 + a for a in range(nb)},
        compiler_params=pltpu.CompilerParams(has_side_effects=SIDE_EFFECT),
    )(*[_hbm(b) for b in bufs], after)
    out_bufs, out_lands, token = list(res[2:2 + nb]), list(res[2 + nb:2 + nb + nl]), res[2 + nb + nl]
    states, b0, l0, k0 = [], 0, 0, 0
    for p_bufs, p_lands, n_copies, _ in parts:
        states.append((res[0], res[1], out_bufs[b0:b0 + len(p_bufs)], out_lands[l0:l0 + len(p_lands)], token, k0))
        b0, l0, k0 = b0 + len(p_bufs), l0 + len(p_lands), k0 + n_copies
    return states


def split_wait(name, started, plan, after):
    send, recv, bufs, lands, _, k0 = started
    nb, nl = len(bufs), len(lands)

    def body(*refs):
        s, r = refs[nb + nl], refs[nb + nl + 1]
        out, lo = refs[nb + nl + 3:2 * nb + nl + 3], refs[2 * nb + nl + 3:]
        x, y, c = _place()
        for k, (src, _, to, landed) in enumerate(plan(out, lo, x, y, c)):
            cp = pltpu.make_async_remote_copy(src_ref=src, dst_ref=landed, send_sem=s.at[k0 + k],
                                              recv_sem=r.at[k0 + k], device_id=to, device_id_type=MESH)
            cp.wait_send()
            cp.wait_recv()

    res = pl.pallas_call(
        body, name=name, in_specs=[HBM_SPEC] * (nb + nl) + [SEM_SPEC, SEM_SPEC, ANY_SPEC],
        out_specs=[HBM_SPEC] * (nb + nl), out_shape=[pltpu.HBM(b.shape, b.dtype) for b in bufs + lands],
        input_output_aliases={a: a for a in range(nb + nl)},
        compiler_params=pltpu.CompilerParams(has_side_effects=SIDE_EFFECT),
    )(*bufs, *lands, send, recv, after)
    return list(res[:nb]), list(res[nb:])


def swap_halves_plan(parts, lands, x, y, c):
    out = []
    for a in range(len(parts)):
        rh = parts[a].shape[1] // 2
        theirs = parts[a].at[:, pl.ds(pl.multiple_of((1 - c) * rh, 16), rh), :]
        out.append((theirs, lands[a], (x, y, 1 - c), lands[a]))
    return out


def chip_exchange_plan(sums, lands, x, y, c):
    me = 2 * x + y
    out = []
    for a in range(len(sums)):
        for (px, py, idx), flip in zip(_other_chips(x, y), (2, 1, 3)):
            out.append((sums[a].at[flip - 1], lands[a].at[me], (px, py, c), lands[a].at[idx]))
    return out


def share_grad_plan(halves, lands, x, y, c):
    return [(halves[a], lands[a], (x, y, 1 - c), lands[a]) for a in range(len(halves))]


def allreduce_pack(name, pack):
    R, W = pack.shape

    def body(p_ref, o_ref, sib, pair, got, send_sems, recv_sems):
        x, y, c = _place()

        def swap(k, src, dst, to):
            cp = pltpu.make_async_remote_copy(src_ref=src, dst_ref=dst, send_sem=send_sems.at[k],
                                              recv_sem=recv_sems.at[k], device_id=to, device_id_type=MESH)
            cp.start()
            return cp

        cp = swap(0, p_ref, sib, (x, y, 1 - c))
        cp.wait()
        pair[...] = p_ref[...] + sib[...]
        cps = [swap(1, pair, got.at[0], (1 - x, y, c)), swap(2, pair, got.at[1], (x, 1 - y, c)),
               swap(3, pair, got.at[2], (1 - x, 1 - y, c))]
        for cp in cps:
            cp.wait()
        o_ref[...] = (pair[...] + got[1]) + (got[0] + got[2])

    return pl.pallas_call(
        body, name=name, out_shape=jax.ShapeDtypeStruct((R, W), F32),
        in_specs=[pl.BlockSpec(memory_space=pltpu.VMEM)], out_specs=pl.BlockSpec(memory_space=pltpu.VMEM),
        scratch_shapes=[pltpu.VMEM((R, W), F32), pltpu.VMEM((R, W), F32), pltpu.VMEM((3, R, W), F32),
                        pltpu.SemaphoreType.DMA((4,)), pltpu.SemaphoreType.DMA((4,))],
        compiler_params=_cparams(),
    )(pack)


BIG_WEIGHTS = [
    ("ffn_a_w_in", "col"), ("ffn_a_w_out", "row"), ("ffn_b_w_in", "col"), ("ffn_b_w_out", "row"),
    ("mla_w_in", "row"), ("mla_w_uq", "col"), ("mla_w_ukv", "col"), ("mla_w_o", "row"),
    ("conv_w_pw1", "col"), ("conv_w_pw2", "row"), ("ple_w_proj", "col"), ("ple_w_gate", "row"),
]
WEIGHT_ORDER = ["ffn_a_norm", "ffn_a_w_in", "ffn_a_w_out", "ffn_b_norm", "ffn_b_w_in", "ffn_b_w_out", "mix_norm",
                "mla_w_in", "mla_q_lat_norm", "mla_kv_lat_norm", "mla_w_uq", "mla_w_ukv", "mla_q_gain", "mla_k_gain",
                "mla_w_o", "conv_w_pw1", "conv_b_pw1", "conv_w_dw", "conv_b_dw", "conv_ln_g", "conv_ln_b", "conv_w_pw2",
                "ple_w_proj", "ple_norm", "ple_gate_norm", "ple_w_gate"]
REPLICATED_SMALL = ["ffn_a_norm", "ffn_b_norm", "mix_norm", "ple_norm", "ple_gate_norm",
                    "mla_q_lat_norm", "mla_kv_lat_norm", "mla_q_gain", "mla_k_gain"]
SHARDED_SMALL = ["conv_b_pw1", "conv_w_dw", "conv_b_dw", "conv_ln_g", "conv_ln_b"]
PACK_ROWS = 8


def _pack_rows(arrs, width):
    out = []
    for a in arrs:
        r = -(-a.shape[0] // PACK_ROWS) * PACK_ROWS
        out.append(jnp.pad(a, ((0, r - a.shape[0]), (0, width - a.shape[1]))))
    return jnp.concatenate(out, axis=0)


def _unpack_rows(pack, shapes):
    out, r0 = [], 0
    for (r, w) in shapes:
        out.append(pack[r0:r0 + r, :w])
        r0 += -(-r // PACK_ROWS) * PACK_ROWS
    return out


def kernel(x, p, positions, ffn_a_norm, ffn_a_w_in, ffn_a_w_out, ffn_b_norm, ffn_b_w_in, ffn_b_w_out, mix_norm, mla_w_in, mla_q_lat_norm, mla_kv_lat_norm, mla_w_uq, mla_w_ukv, mla_q_gain, mla_k_gain, mla_w_o, conv_w_pw1, conv_b_pw1, conv_w_dw, conv_b_dw, conv_ln_g, conv_ln_b, conv_w_pw2, ple_w_proj, ple_norm, ple_gate_norm, ple_w_gate, loss_target, m_ffn_a_norm, m_ffn_a_w_in, m_ffn_a_w_out, m_ffn_b_norm, m_ffn_b_w_in, m_ffn_b_w_out, m_mix_norm, m_mla_w_in, m_mla_q_lat_norm, m_mla_kv_lat_norm, m_mla_w_uq, m_mla_w_ukv, m_mla_q_gain, m_mla_k_gain, m_mla_w_o, m_conv_w_pw1, m_conv_b_pw1, m_conv_w_dw, m_conv_b_dw, m_conv_ln_g, m_conv_ln_b, m_conv_w_pw2, m_ple_w_proj, m_ple_norm, m_ple_gate_norm, m_ple_w_gate, v_ffn_a_norm, v_ffn_a_w_in, v_ffn_a_w_out, v_ffn_b_norm, v_ffn_b_w_in, v_ffn_b_w_out, v_mix_norm, v_mla_w_in, v_mla_q_lat_norm, v_mla_kv_lat_norm, v_mla_w_uq, v_mla_w_ukv, v_mla_q_gain, v_mla_k_gain, v_mla_w_o, v_conv_w_pw1, v_conv_b_pw1, v_conv_w_dw, v_conv_b_dw, v_conv_ln_g, v_conv_ln_b, v_conv_w_pw2, v_ple_w_proj, v_ple_norm, v_ple_gate_norm, v_ple_w_gate):
    args = dict(locals())
    W = {n: args[n] for n in WEIGHT_ORDER}
    M1 = {n: args["m_" + n] for n in WEIGHT_ORDER}
    V2 = {n: args["v_" + n] for n in WEIGHT_ORDER}

    T, D = x.shape[1], x.shape[2]
    depth = ffn_a_norm.shape[0]
    H = mla_w_ukv.shape[2] * N_CHIPS // (D_NOPE + D_V)
    QL, KL = mla_q_lat_norm.shape[1], mla_kv_lat_norm.shape[1]
    C = conv_w_pw2.shape[1] * N_CHIPS
    lat_w = QL + KL + D_ROPE
    lat_pad = QL + KL + 128

    cx, cy, cc = lax.axis_index("x"), lax.axis_index("y"), lax.axis_index("c")
    chip = (2 * cx + cy).astype(jnp.int32)
    chip_arr = chip.reshape(1)
    core_arr = cc.astype(jnp.int32).reshape(1)

    def stage_groups(i):
        mix = ([("mla_w_in", i // 2), ("mla_w_uq", i // 2), ("mla_w_ukv", i // 2), ("mla_w_o", i // 2)] if i % 2 == 0
               else [("conv_w_pw1", i // 2), ("conv_w_pw2", i // 2)])
        return [[("ffn_a_w_in", i)], [("ffn_a_w_out", i)], mix, [("ffn_b_w_in", i)], [("ffn_b_w_out", i)],
                [("ple_w_proj", i), ("ple_w_gate", i)]]

    groups = [g for i in range(depth) for g in stage_groups(i)]
    A_IN, A_OUT, MIX, B_IN, B_OUT, PLE, PER_LAYER = 0, 1, 2, 3, 4, 5, 6

    def slot_of(key, after=None):
        n, l = key
        return cast_into_slot(f"cast_{n}_{l}", W[n].reshape(-1, W[n].shape[-1]), chip_arr, l, W[n].shape[0], after)

    def placed(a, width):
        full = jnp.zeros(a.shape[:-1] + (width,), F32)
        full = lax.dynamic_update_slice_in_dim(full, a, chip * a.shape[-1], axis=a.ndim - 1)
        return full * (cc == 0).astype(F32)

    b_pw1_sh = conv_b_pw1.reshape(1, -1)
    small_in = [placed(b_pw1_sh, 2 * C).reshape(2, C), placed(conv_w_dw[0], C), placed(conv_b_dw, C),
                placed(conv_ln_g, C), placed(conv_ln_b, C)]
    small_pack = allreduce_pack("gather_small", _pack_rows(small_in, C))
    small_full = _unpack_rows(small_pack, [(2, C), (CONV_WIDTH, C), (1, C), (1, C), (1, C)])

    FIRST = 2
    started, tok0 = gather_start("gather_start_first", [[slot_of(k) for k in g] for g in groups[:FIRST]], small_pack)
    small = REPLICATED_SMALL + SHARDED_SMALL
    small_packs = [_pack_rows([src[n].reshape(-1, src[n].shape[-1]) for n in small], D) for src in (W, M1, V2)]
    tok0 = tok0 + 0.0 * (small_packs[0][:8, :128] + small_packs[1][:8, :128] + small_packs[2][:8, :128])
    rest, tok_rest = gather_start("gather_start_rest", [[slot_of(k, tok0) for k in g] for g in groups[FIRST:]], tok0)
    started = started + rest
    relayed, G = {}, {}

    def relay(g, after):
        if g >= len(groups):
            return None
        relayed[g] = gather_relay(f"gather_relay_{g}", started[g], after)
        return relayed[g][3]

    def ready(g, after):
        if g not in relayed:
            relay(g, after)
        for key, buf in zip(groups[g], gather_wait(f"gather_wait_{g}", relayed[g], after)):
            G[key] = buf

    relay(0, tok_rest)

    b_pw1_full = small_full[0].reshape(1, 2 * C)
    w_dw_full = jnp.pad(small_full[1], ((0, CONV_TAPS_PAD - CONV_WIDTH), (0, 0)))
    b_dw_full, ln_g_full, ln_b_full = small_full[2], small_full[3], small_full[4]

    pad_gain = lambda g: jnp.pad(g, ((0, 0), (0, HEAD_PAD - QK_DIM)))
    q_gain_p, k_gain_p = pad_gain(mla_q_gain), pad_gain(mla_k_gain)
    tabs = rope_tables("rope_tables", positions.reshape(T, 1).astype(F32))

    def ffn_fwd(tag, h, norm, w_in, w_out, layer, g_in):
        hn = rms_fwd(f"{tag}_rms", h, norm)
        ready(g_in, hn)
        tok = relay(g_in + 1, hn) if g_in > 0 else None
        gu, act = ffn_in_act(f"{tag}_in", hn, G[(w_in, layer)], after=tok)
        ready(g_in + 1, act)
        tok = relay(g_in + 2, act)
        out = mm_nn(f"{tag}_out", act, G[(w_out, layer)], "row", 0, out_dtype=F32, scale=FFN_RESIDUAL_WEIGHT, res=h,
                    tm=1024, tn=512, whole_k=True, after=tok)
        return out, (h, hn, gu, act)

    saved = []
    h = x[0]
    for i in range(depth):
        L = {}
        g0 = PER_LAYER * i
        h, L["ffn_a"] = ffn_fwd(f"l{i}_ffa", h, ffn_a_norm[i:i + 1], "ffn_a_w_in", "ffn_a_w_out", i, g0 + A_IN)
        L["h1"] = h
        hn = rms_fwd(f"l{i}_mix_rms", h, mix_norm[i:i + 1])
        L["hn_m"] = hn
        ready(g0 + MIX, hn)
        j = i // 2
        if i % 2 == 0:
            w_in_pad = jnp.pad(G[("mla_w_in", j)].reshape(D, lat_w), ((0, 0), (0, lat_pad - lat_w)))[None]
            uq = G[("mla_w_uq", j)].transpose(1, 0, 2).reshape(QL, H, QK_DIM)
            w_uq_pad = jnp.pad(uq, ((0, 0), (0, 0), (0, HEAD_PAD - QK_DIM))).reshape(1, QL, H * HEAD_PAD)
            lat = mm_nn(f"l{i}_lat", hn, w_in_pad, "row", 0, out_dtype=F32, tm=2048)
            cq, ckv = lat_norm_fwd(f"l{i}_latnorm", lat, mla_q_lat_norm[j:j + 1], mla_kv_lat_norm[j:j + 1])
            q_raw = mm_nn(f"l{i}_uq", cq, w_uq_pad, "row", 0, out_dtype=F32, tm=2048)
            kv_raw = mm_nn(f"l{i}_ukv", ckv, G[("mla_w_ukv", j)], "col", 0, out_dtype=F32, tm=2048)
            Qh, Kh, Vh = mla_prep_fwd(f"l{i}_prep", q_raw, kv_raw, lat, tabs, q_gain_p, k_gain_p, QL + KL)
            O = attn_fwd(f"l{i}_attn", Qh, Kh, Vh)
            tok = relay(g0 + B_IN, O)
            h = mm_nn(f"l{i}_wo", O, G[("mla_w_o", j)], "row", 0, out_dtype=F32, res=h, tm=1024, tn=1024, after=tok)
            L["mla"] = (lat, cq, ckv, q_raw, kv_raw, Qh, Kh, Vh, O, w_in_pad, w_uq_pad)
        else:
            ag = mm_nn(f"l{i}_pw1", hn, G[("conv_w_pw1", j)], "col", 0, out_dtype=F32, bias=b_pw1_full, tm=2048)
            u = glu_fwd(f"l{i}_glu", ag)
            yc = dwconv_fwd(f"l{i}_dw", u, w_dw_full, b_dw_full)
            cact = ln_silu_fwd(f"l{i}_ln", yc, ln_g_full, ln_b_full)
            tok = relay(g0 + B_IN, cact)
            h = mm_nn(f"l{i}_pw2", cact, G[("conv_w_pw2", j)], "row", 0, out_dtype=F32, res=h, tm=1024, tn=1024,
                      after=tok)
            L["conv"] = (ag, u, yc, cact)
        L["h2"] = h
        h, L["ffn_b"] = ffn_fwd(f"l{i}_ffb", h, ffn_b_norm[i:i + 1], "ffn_b_w_in", "ffn_b_w_out", i, g0 + B_IN)
        L["h3"] = h
        ready(g0 + PLE, h)
        pe = mm_nn(f"l{i}_ple_proj", p[i, 0], G[("ple_w_proj", i)], "col", 0, out_dtype=F32, tm=2048)
        hg = rms_fwd(f"l{i}_gate_rms", h, ple_gate_norm[i:i + 1])
        tok = relay(g0 + PER_LAYER, hg)
        z = mm_nn(f"l{i}_ple_gate", hg, G[("ple_w_gate", i)], "row", 0, out_dtype=F32, tm=2048, after=tok)
        h = ple_fwd(f"l{i}_ple", h, pe, z, ple_norm[i:i + 1])
        L["ple"] = (pe, hg, z)
        saved.append(L)

    d_h, loss_part = loss_head("loss_head", h, loss_target[0])
    loss = lax.psum(loss_part[0, 0], ("x", "y", "c"))

    GW = {}
    SG = {}
    ids_arr = jnp.stack([cc.astype(jnp.int32), chip])
    two = lambda a: a.reshape(-1, a.shape[-1])
    merged = {}
    pipe = {}
    swapped = {}
    order = list(reversed(range(len(groups))))
    ticks = [0]

    def put_small(name, i, val):
        SG.setdefault(name, {})[i] = val

    deferred = []

    def sibling_half(g, after):
        _, theirs = split_wait(f"share_wait_{g}", pipe[g], share_grad_plan, after)
        for (n, l), gr in zip(groups[g], theirs):
            merged[n] = adamw_half(f"adamw_sib_{n}_{l}", ids_arr, gr, two(W[n]), two(M1[n]), two(V2[n]), l,
                                   W[n].shape[0], merged[n])
        return merged[groups[g][-1][0]][0]

    def reduce_tick(after, defer=True):
        k, tok = ticks[0], after
        ticks[0] += 1
        grp = lambda j: order[j] if 0 <= j < len(order) else None
        g = grp(k - 4)
        if g is not None:
            if defer:
                deferred.append(g)
            else:
                sibling_half(g, tok)
        starts = []
        g = grp(k - 3)
        if g is not None:
            _, landed = split_wait(f"exchange_wait_{g}", pipe[g], chip_exchange_plan, tok)
            halves = []
            for (n, l), (q, r), ld in zip(groups[g], swapped[g], landed):
                *merged[n], mine = adamw_half(f"adamw_own_{n}_{l}", ids_arr, (q, r, ld), two(W[n]), two(M1[n]),
                                              two(V2[n]), l, W[n].shape[0], merged.get(n))
                halves.append(mine)
            starts.append((g, (halves, [(h.shape, h.dtype) for h in halves], len(halves), share_grad_plan)))
        g = grp(k - 1)
        if g is not None:
            parts, got = split_wait(f"swap_wait_{g}", pipe[g], swap_halves_plan, tok)
            swapped[g] = list(zip(parts, got))
            sums = [add_halves(f"pair_sum_{n}_{l}", ids_arr, q, r) for (n, l), q, r in zip(groups[g], parts, got)]
            lands = [((N_CHIPS,) + s.shape[1:], s.dtype) for s in sums]
            starts.append((g, (sums, lands, 3 * len(sums), chip_exchange_plan)))
        g = grp(k)
        if g is not None:
            parts = [GW[key] for key in groups[g]]
            lands = [((q.shape[0], q.shape[1] // 2, q.shape[2]), q.dtype) for q in parts]
            starts.append((g, (parts, lands, len(parts), swap_halves_plan)))
        if starts:
            behind = ids_arr if grp(k) is not None else tok
            for (g, _), state in zip(starts, split_start(f"reduce_start_{k}", [p for _, p in starts], behind)):
                pipe[g] = state
            tok = state[4]
        return tok

    def ffn_bwd(tag, d_h, d_hb, norm, w_in, w_out, layer, fw, tok):
        h_in, hn, gu, act = fw
        GW[(w_out, layer)] = mm_tn(f"{tag}_dwout", act, d_hb, "row", scale=FFN_RESIDUAL_WEIGHT, tk=1408, tn=1024,
                                   after=tok)
        tok = reduce_tick(GW[(w_out, layer)])
        dgu = ffn_dact_dgu(f"{tag}_dact", d_hb, G[(w_out, layer)], gu, FFN_RESIDUAL_WEIGHT, after=tok)
        GW[(w_in, layer)] = mm_tn(f"{tag}_dwin", hn, dgu, "col")
        tok = reduce_tick(GW[(w_in, layer)])
        d_hn = _mm_nt(f"{tag}_dhn", dgu, G[(w_in, layer)], "col", 0, D, out_dtype=F32, tc=2816, after=tok)
        return (*rms_bwd_res(f"{tag}_drms", h_in, d_hn, norm, d_h), tok)

    d_hb, tok = None, None
    for i in reversed(range(depth)):
        L = saved[i]
        j = i // 2
        pe, hg, z = L["ple"]
        d_z, d_pe, g = ple_bwd(f"l{i}_dple", d_h, pe, z, ple_norm[i:i + 1])
        put_small("ple_norm", i, g)
        d_hg = _mm_nt(f"l{i}_dhg", d_z, G[("ple_w_gate", i)], "row", 0, D, out_dtype=F32, to=512, tc=2048, after=tok)
        GW[("ple_w_gate", i)] = mm_tn(f"l{i}_dwgate", hg, d_z, "row", tk=512, tn=1024)
        GW[("ple_w_proj", i)] = mm_tn(f"l{i}_dwproj", p[i, 0], d_pe, "col")
        tok = reduce_tick(GW[("ple_w_proj", i)])
        d_h, d_hb, g = rms_bwd_res(f"l{i}_dgate_rms", L["h3"], d_hg, ple_gate_norm[i:i + 1], d_h)
        put_small("ple_gate_norm", i, g)

        d_h, d_hb, g, tok = ffn_bwd(f"l{i}_ffb", d_h, d_hb, ffn_b_norm[i:i + 1], "ffn_b_w_in", "ffn_b_w_out", i,
                                    L["ffn_b"], tok)
        put_small("ffn_b_norm", i, g)

        hn = L["hn_m"]
        if i % 2 == 0:
            lat, cq, ckv, q_raw, kv_raw, Qh, Kh, Vh, O, w_in_pad, w_uq_pad = L["mla"]
            d_O = _mm_nt(f"l{i}_dO", d_hb, G[("mla_w_o", j)], "row", 0, H * D_V, out_dtype=BF16, to=512, tc=2048,
                         after=tok)
            GW[("mla_w_o", j)] = mm_tn(f"l{i}_dwo", O, d_hb, "row", tk=512, tn=1024)
            dQ, dK, dV = attn_bwd(f"l{i}_dattn", Qh, Kh, Vh, d_O)
            d_q_raw, d_kv_raw, d_kr, gq, gk = mla_prep_bwd(f"l{i}_dprep", dQ, dK, dV, q_raw, kv_raw, lat, tabs,
                                                           q_gain_p, k_gain_p, QL + KL)
            put_small("mla_q_gain", j, gq[:, :QK_DIM])
            put_small("mla_k_gain", j, gk[:, :QK_DIM])
            d_cq = _mm_nt(f"l{i}_dcq", d_q_raw, w_uq_pad, "row", 0, QL, out_dtype=F32, to=512, tc=2048)
            g_uq = mm_tn(f"l{i}_dwuq", cq, d_q_raw, "row", shards=1, out_dtype=F32, tk=512, tn=1024)
            g_uq = g_uq.reshape(QL, H, HEAD_PAD)[:, :, :QK_DIM].reshape(QL, N_CHIPS, -1).transpose(1, 0, 2)
            GW[("mla_w_uq", j)] = g_uq.astype(BF16)
            d_ckv = _mm_nt(f"l{i}_dckv", d_kv_raw, G[("mla_w_ukv", j)], "col", 0, KL, out_dtype=F32, to=512, tc=1024)
            GW[("mla_w_ukv", j)] = mm_tn(f"l{i}_dwukv", ckv, d_kv_raw, "col", tk=512, tn=1024)
            d_lat, gq, gk = lat_norm_bwd(f"l{i}_dlatnorm", lat, d_cq, d_ckv, d_kr, mla_q_lat_norm[j:j + 1],
                                         mla_kv_lat_norm[j:j + 1])
            put_small("mla_q_lat_norm", j, gq)
            put_small("mla_kv_lat_norm", j, gk)
            d_hn = _mm_nt(f"l{i}_dhn_lat", d_lat, w_in_pad, "row", 0, D, out_dtype=F32, to=1024, tc=lat_pad)
            g_in = mm_tn(f"l{i}_dwin_lat", hn, d_lat, "row", shards=1, out_dtype=F32, tk=1024, tn=lat_pad)
            GW[("mla_w_in", j)] = g_in[0, :, :lat_w].reshape(N_CHIPS, D // N_CHIPS, lat_w).astype(BF16)
            tok = reduce_tick(GW[("mla_w_in", j)])
        else:
            ag, u, yc, cact = L["conv"]
            d_cact = _mm_nt(f"l{i}_dcact", d_hb, G[("conv_w_pw2", j)], "row", 0, C, out_dtype=F32, to=512, tc=2048,
                            after=tok)
            GW[("conv_w_pw2", j)] = mm_tn(f"l{i}_dwpw2", cact, d_hb, "row", tk=512, tn=1024)
            d_yc, g1, g2, g3 = ln_silu_bwd(f"l{i}_dln", yc, d_cact, ln_g_full, ln_b_full)
            put_small("conv_ln_g", j, g1)
            put_small("conv_ln_b", j, g2)
            put_small("conv_b_dw", j, g3)
            d_u = dwconv_bwd_u(f"l{i}_ddw_u", d_yc, w_dw_full)
            put_small("conv_w_dw", j, dwconv_bwd_w(f"l{i}_ddw_w", u, d_yc))
            d_ag, g = glu_bwd(f"l{i}_dglu", ag, d_u)
            put_small("conv_b_pw1", j, g)
            d_hn = _mm_nt(f"l{i}_dhn_pw1", d_ag, G[("conv_w_pw1", j)], "col", 0, D, out_dtype=F32, tc=1024)
            GW[("conv_w_pw1", j)] = mm_tn(f"l{i}_dwpw1", hn, d_ag, "col", tn=1024)
            tok = reduce_tick(GW[("conv_w_pw1", j)])
        d_h, d_hb, g = rms_bwd_res(f"l{i}_dmix_rms", L["h1"], d_hn, mix_norm[i:i + 1], d_h)
        put_small("mix_norm", i, g)

        d_h, d_hb, g, tok = ffn_bwd(f"l{i}_ffa", d_h, d_hb, ffn_a_norm[i:i + 1], "ffn_a_w_in", "ffn_a_w_out", i,
                                    L["ffn_a"], tok)
        put_small("ffn_a_norm", i, g)
    grad_x = d_h[None]

    tok = reduce_tick(reduce_tick(d_h))
    for g in deferred:
        tok = sibling_half(g, tok)
    tok = reduce_tick(reduce_tick(tok, defer=False), defer=False)
    names = [n for n, _ in BIG_WEIGHTS]
    grads, delta, new_m, new_v = {}, {}, {}, {}
    for n in names:
        grads[n], delta[n], new_m[n], new_v[n] = [a.reshape(W[n].shape) for a in merged[n]]

    rep = []
    for n in REPLICATED_SMALL:
        rep.append(jnp.concatenate([SG[n][i] for i in sorted(SG[n])], axis=0))
    shd = [SG["conv_b_pw1"][0].reshape(2, C), SG["conv_w_dw"][0][:CONV_WIDTH], SG["conv_b_dw"][0],
           SG["conv_ln_g"][0], SG["conv_ln_b"][0]]
    red = allreduce_pack("allreduce_small", _pack_rows(rep + shd, D))
    red = _unpack_rows(red, [a.shape for a in rep + shd])
    for n, g in zip(REPLICATED_SMALL, red):
        grads[n] = g
    own = lambda a, w: lax.dynamic_slice_in_dim(a, chip * w, w, axis=a.ndim - 1)
    sh = red[len(rep):]
    grads["conv_b_pw1"] = own(sh[0].reshape(1, 2 * C), 2 * C // N_CHIPS)
    grads["conv_w_dw"] = own(sh[1], C // N_CHIPS)[None]
    grads["conv_b_dw"] = own(sh[2], C // N_CHIPS)
    grads["conv_ln_g"] = own(sh[3], C // N_CHIPS)
    grads["conv_ln_b"] = own(sh[4], C // N_CHIPS)

    shapes = [two(W[n]).shape for n in small]
    grad_pack = _pack_rows([two(grads[n]) for n in small], D)
    outs = adamw("adamw_small", small_packs[0], grad_pack, small_packs[1], small_packs[2])
    for dst, pk in zip((delta, new_m, new_v), outs):
        for n, a in zip(small, _unpack_rows(pk, shapes)):
            dst[n] = a.reshape(W[n].shape)
    for n in small:
        grads[n] = grads[n].reshape(W[n].shape)

    return (loss, grad_x, *[grads[n] for n in WEIGHT_ORDER], *[delta[n] for n in WEIGHT_ORDER],
            *[new_m[n] for n in WEIGHT_ORDER], *[new_v[n] for n in WEIGHT_ORDER])
```

```python
import functools
import math

import jax
import jax.numpy as jnp
from jax import lax
from jax.experimental import pallas as pl
from jax.experimental.pallas import tpu as pltpu

F32, BF16 = jnp.float32, jnp.bfloat16
MESH = pl.DeviceIdType.MESH

N_CHIPS = 4
EPS = 1e-6
D_NOPE, D_ROPE, D_V = 128, 64, 128
QK_DIM = D_NOPE + D_ROPE
HEAD_PAD = 256
ROPE_THETA = 10000.0
CONV_WIDTH = 31
CONV_TAPS_PAD = 32
FFN_RESIDUAL_WEIGHT = 0.5
ADAM_LR, ADAM_B1, ADAM_B2, ADAM_EPS, ADAM_WD, ADAM_STEP = 0.001, 0.9, 0.999, 1e-08, 0.01, 10
VMEM_LIMIT_BYTES = 56 * 1024 * 1024
NEG_BIG = -1e30

NN_DIMS = (((1,), (0,)), ((), ()))
NT_DIMS = (((1,), (1,)), ((), ()))
TN_DIMS = (((0,), (0,)), ((), ()))


def _cparams(semantics=None):
    kw = dict(vmem_limit_bytes=VMEM_LIMIT_BYTES)
    if semantics is not None:
        kw["dimension_semantics"] = semantics
    return pltpu.CompilerParams(**kw)


def _tile(n, pref, mult=128):
    if n <= pref:
        return n
    t = (pref // mult) * mult
    while t >= mult:
        if n % t == 0:
            return t
        t -= mult
    return n


def _rowwise(name, fn, rows, vecs, outs, accs=(), tm=256, rc=32):
    T = rows[0].shape[0]
    tm = min(tm, T)
    rc = min(rc, tm)
    nr, nv, no, na = len(rows), len(vecs), len(outs), len(accs)
    steps = tm // rc

    def body(*refs):
        row_refs = refs[:nr]
        vec_refs = refs[nr:nr + nv]
        out_refs = refs[nr + nv:nr + nv + no]
        acc_refs = refs[nr + nv + no:]
        if na:
            @pl.when(pl.program_id(0) == 0)
            def _():
                for a in acc_refs:
                    a[...] = jnp.zeros_like(a)

        def step(r, carry):
            sl = pl.ds(pl.multiple_of(r * rc, rc), rc)
            res = fn(*[x[sl, :] for x in row_refs], *[v[...] for v in vec_refs])
            for o, val in zip(out_refs, res[:no]):
                o[sl, :] = val.astype(o.dtype)
            return tuple(c + val for c, val in zip(carry, res[no:]))

        init = tuple(jnp.zeros(s, F32) for s in accs)
        tot = lax.fori_loop(0, steps, step, init)
        for a, val in zip(acc_refs, tot):
            a[...] += val

    in_specs = [pl.BlockSpec((tm, x.shape[1]), lambda i: (i, 0)) for x in rows]
    in_specs += [pl.BlockSpec(v.shape, lambda i: (0, 0)) for v in vecs]
    out_specs = [pl.BlockSpec((tm, d), lambda i: (i, 0)) for d, _ in outs]
    out_specs += [pl.BlockSpec(s, lambda i: (0, 0)) for s in accs]
    out_shape = [jax.ShapeDtypeStruct((T, d), dt) for d, dt in outs]
    out_shape += [jax.ShapeDtypeStruct(s, F32) for s in accs]
    return pl.pallas_call(
        body, name=name, grid=(T // tm,), in_specs=in_specs, out_specs=out_specs, out_shape=out_shape,
        compiler_params=_cparams(("arbitrary",)),
    )(*rows, *vecs)


def _colsum(v):
    return jnp.sum(v, axis=0, keepdims=True)


def _rstd(x):
    return lax.rsqrt(jnp.mean(x * x, axis=-1, keepdims=True) + EPS)


def _rms_bwd(x, dy, g):
    r = _rstd(x)
    xh = x * r
    dyg = dy * g
    dx = r * (dyg - xh * jnp.mean(dyg * xh, axis=-1, keepdims=True))
    return dx, dy * xh


def _sigmoid(x):
    return 1.0 / (1.0 + jnp.exp(-x))


def rms_fwd(name, h, g):
    def fn(x, gv):
        return ((x * _rstd(x)) * gv,)
    return _rowwise(name, fn, [h], [g], [(h.shape[1], BF16)])[0]


def rms_bwd_res(name, h, d_y, g, d_res):
    D = h.shape[1]

    def fn(x, dy, dr, gv):
        dx, dgr = _rms_bwd(x, dy, gv)
        dh = dr + dx
        return dh, dh, _colsum(dgr)
    return _rowwise(name, fn, [h, d_y, d_res], [g], [(D, F32), (D, BF16)], [(1, D)], tm=128)


def loss_head(name, y, target):
    D = y.shape[1]

    def fn(yv, tv):
        e = yv - tv
        tot = jnp.sum(_colsum(e * e), axis=1, keepdims=True) * (0.5 / D)
        return e * (1.0 / D), jnp.broadcast_to(tot, (1, 128))
    return _rowwise(name, fn, [y, target], [], [(D, F32)], [(1, 128)])


def ple_fwd(name, h, pe, z, g_e):
    def fn(hv, pv, zv, gv):
        return (hv + (pv * _rstd(pv)) * gv * _sigmoid(zv),)
    return _rowwise(name, fn, [h, pe, z], [g_e], [(h.shape[1], F32)], tm=128)[0]


def ple_bwd(name, d_h, pe, z, g_e):
    D = d_h.shape[1]

    def fn(dh, pv, zv, gv):
        gate = _sigmoid(zv)
        e = (pv * _rstd(pv)) * gv
        d_z = dh * e * gate * (1.0 - gate)
        d_pe, dgr = _rms_bwd(pv, dh * gate, gv)
        return d_z, d_pe, _colsum(dgr)
    return _rowwise(name, fn, [d_h, pe, z], [g_e], [(D, BF16), (D, BF16)], [(1, D)], tm=128)


def lat_norm_fwd(name, lat, g_q, g_kv):
    QL, KL = g_q.shape[1], g_kv.shape[1]

    def fn(v, gq, gk):
        a = v[:, :QL]
        b = v[:, QL:QL + KL]
        return (a * _rstd(a)) * gq, (b * _rstd(b)) * gk
    return _rowwise(name, fn, [lat], [g_q, g_kv], [(QL, BF16), (KL, BF16)])


def lat_norm_bwd(name, lat, d_cq, d_ckv, d_krope, g_q, g_kv):
    QL, KL = g_q.shape[1], g_kv.shape[1]

    def fn(v, dq, dk, dr, gq, gk):
        da, ga = _rms_bwd(v[:, :QL], dq, gq)
        db, gb = _rms_bwd(v[:, QL:QL + KL], dk, gk)
        return jnp.concatenate([da, db, dr], axis=-1), _colsum(ga), _colsum(gb)
    return _rowwise(name, fn, [lat, d_cq, d_ckv, d_krope], [g_q, g_kv],
                    [(lat.shape[1], BF16)], [(1, QL), (1, KL)])


def glu_fwd(name, ag):
    C = ag.shape[1] // 2

    def fn(v):
        return (v[:, :C] * _sigmoid(v[:, C:]),)
    return _rowwise(name, fn, [ag], [], [(C, F32)], tm=128)[0]


def glu_bwd(name, ag, d_u):
    C = ag.shape[1] // 2

    def fn(v, du):
        a = v[:, :C]
        s = _sigmoid(v[:, C:])
        d = jnp.concatenate([du * s, du * a * s * (1.0 - s)], axis=-1)
        return d, _colsum(d)
    return _rowwise(name, fn, [ag, d_u], [], [(2 * C, BF16)], [(1, 2 * C)], tm=128)


def ln_silu_fwd(name, yc, g, b):
    def fn(v, gv, bv):
        xc = v - jnp.mean(v, axis=-1, keepdims=True)
        ln = xc * lax.rsqrt(jnp.mean(xc * xc, axis=-1, keepdims=True) + EPS) * gv + bv
        return (ln * _sigmoid(ln),)
    return _rowwise(name, fn, [yc], [g, b], [(yc.shape[1], BF16)], tm=128)[0]


def ln_silu_bwd(name, yc, d_out, g, b):
    C = yc.shape[1]

    def fn(v, do, gv, bv):
        xc = v - jnp.mean(v, axis=-1, keepdims=True)
        r = lax.rsqrt(jnp.mean(xc * xc, axis=-1, keepdims=True) + EPS)
        xh = xc * r
        ln = xh * gv + bv
        s = _sigmoid(ln)
        d_ln = do * s * (1.0 + ln * (1.0 - s))
        dxh = d_ln * gv
        dy = r * (dxh - jnp.mean(dxh, axis=-1, keepdims=True) - xh * jnp.mean(dxh * xh, axis=-1, keepdims=True))
        return dy, _colsum(d_ln * xh), _colsum(d_ln), _colsum(dy)
    return _rowwise(name, fn, [yc, d_out], [g, b], [(C, F32)], [(1, C), (1, C), (1, C)], tm=128)


def add_halves(name, ids, own, got):
    S, Rh, C = got.shape
    tr = _tile(Rh, 512, 16)
    nrb = Rh // tr

    def body(ids_ref, a_ref, b_ref, o_ref):
        o_ref[...] = (a_ref[...].astype(F32) + b_ref[...].astype(F32)).astype(o_ref.dtype)

    gs = pltpu.PrefetchScalarGridSpec(
        num_scalar_prefetch=1, grid=(S - 1, nrb),
        in_specs=[pl.BlockSpec((None, tr, C), lambda s, r, ids: (ids[1] ^ (s + 1), ids[0] * nrb + r, 0)),
                  pl.BlockSpec((None, tr, C), lambda s, r, ids: (ids[1] ^ (s + 1), r, 0))],
        out_specs=pl.BlockSpec((None, tr, C), lambda s, r, ids: (s, r, 0)))
    return pl.pallas_call(
        body, name=name, grid_spec=gs, out_shape=jax.ShapeDtypeStruct((S - 1, Rh, C), BF16),
        compiler_params=_cparams(("arbitrary", "arbitrary")))(ids, own, got)


def adamw_half(name, ids, grad, w, m, v, layer, layers, into):
    own = isinstance(grad, tuple)
    Rh, C = grad[1].shape[1:] if own else grad.shape
    tr = _tile(Rh, 128, 16)
    nrb = Rh // tr
    n_grad = 5 if own else 1
    c1 = 1.0 / (1.0 - ADAM_B1 ** ADAM_STEP)
    c2 = 1.0 / (1.0 - ADAM_B2 ** ADAM_STEP)

    def body(ids_ref, *refs):
        g_refs, (w_ref, m_ref, v_ref) = refs[:n_grad], refs[n_grad:n_grad + 3]
        outs = refs[-5:] if own else refs[-4:]
        go_ref, d_ref, nm_ref, nv_ref = outs[:4]
        if own:
            f = [g[...].astype(F32) for g in g_refs]
            gv = ((f[0] + f[1]) + f[2]) + (f[3] + f[4])
            outs[4][...] = gv
        else:
            gv = g_refs[0][...]
        nm = ADAM_B1 * m_ref[...] + (1.0 - ADAM_B1) * gv
        nv = ADAM_B2 * v_ref[...] + (1.0 - ADAM_B2) * (gv * gv)
        go_ref[...] = gv
        d_ref[...] = -ADAM_LR * ((nm * c1) / (jnp.sqrt(nv * c2) + ADAM_EPS) + ADAM_WD * w_ref[...])
        nm_ref[...] = nm
        nv_ref[...] = nv

    half = (lambda ids: ids[0]) if own else (lambda ids: 1 - ids[0])
    rows = pl.BlockSpec((tr, C), lambda r, ids: ((2 * layer + half(ids)) * nrb + r, 0))
    plain = pl.BlockSpec((tr, C), lambda r, ids: (r, 0))
    if own:
        slot = lambda flip: pl.BlockSpec((None, tr, C), lambda r, ids: (ids[1] ^ flip, r, 0))
        mine = pl.BlockSpec((None, tr, C), lambda r, ids: (ids[1], ids[0] * nrb + r, 0))
        in_specs = [mine, slot(0), slot(1), slot(2), slot(3), rows, rows, rows]
        operands = [ids, grad[0], grad[1], grad[2], grad[2], grad[2], w, m, v]
    else:
        in_specs = [plain, rows, rows, rows]
        operands = [ids, grad, w, m, v]
    aliases = {}
    if into is not None:
        in_specs += [pl.BlockSpec(memory_space=pl.ANY)] * 4
        aliases = {len(operands) + k: k for k in range(4)}
        operands += list(into)
    full = jax.ShapeDtypeStruct((layers * 2 * Rh, C), F32)
    gs = pltpu.PrefetchScalarGridSpec(num_scalar_prefetch=1, grid=(nrb,), in_specs=in_specs,
                                      out_specs=[rows] * 4 + [plain] * own)
    return pl.pallas_call(
        body, name=name, grid_spec=gs, out_shape=[full] * 4 + [jax.ShapeDtypeStruct((Rh, C), F32)] * own,
        input_output_aliases=aliases, compiler_params=_cparams(("arbitrary",)))(*operands)


def cast_into_slot(name, w, chip, layer, layers, after=None):
    R, C = w.shape[0] // layers, w.shape[1]
    tr = _tile(R, 512, 16)
    nrb = R // tr

    def body(s_ref, w_ref, *rest):
        rest[-1][...] = w_ref[...].astype(BF16)

    in_specs = [pl.BlockSpec((tr, C), lambda r, s: (layer * nrb + r, 0))]
    operands = [chip, w]
    if after is not None:
        in_specs.append(pl.BlockSpec(memory_space=pl.ANY))
        operands.append(after)
    gs = pltpu.PrefetchScalarGridSpec(
        num_scalar_prefetch=1, grid=(nrb,), in_specs=in_specs,
        out_specs=pl.BlockSpec((None, tr, C), lambda r, s: (s[0], r, 0)))
    return pl.pallas_call(
        body, name=name, grid_spec=gs, out_shape=jax.ShapeDtypeStruct((N_CHIPS, R, C), BF16),
        compiler_params=_cparams(("arbitrary",)))(*operands)


def adamw(name, w, g, m, v):
    R, C = w.shape
    tr = _tile(R, 256, 8)
    c1 = 1.0 / (1.0 - ADAM_B1 ** ADAM_STEP)
    c2 = 1.0 / (1.0 - ADAM_B2 ** ADAM_STEP)

    def body(w_ref, g_ref, m_ref, v_ref, d_ref, nm_ref, nv_ref):
        gv = g_ref[...]
        nm = ADAM_B1 * m_ref[...] + (1.0 - ADAM_B1) * gv
        nv = ADAM_B2 * v_ref[...] + (1.0 - ADAM_B2) * (gv * gv)
        d_ref[...] = -ADAM_LR * ((nm * c1) / (jnp.sqrt(nv * c2) + ADAM_EPS) + ADAM_WD * w_ref[...])
        nm_ref[...] = nm
        nv_ref[...] = nv

    spec = pl.BlockSpec((tr, C), lambda r: (r, 0))
    return pl.pallas_call(
        body, name=name, grid=(R // tr,), in_specs=[spec] * 4, out_specs=[spec] * 3,
        out_shape=[jax.ShapeDtypeStruct((R, C), F32)] * 3, compiler_params=_cparams(("arbitrary",)))(w, g, m, v)


def _matmul(name, a, b, *, grid, a_blk, a_map, b_blk, b_map, o_shape, o_dtype, o_blk, o_map, dims,
            scale=None, res=None, bias=None, bias_blk=None, bias_map=None, alias_into=None, after=None):
    nk = grid[2]
    has_res, has_bias, has_into = res is not None, bias is not None, alias_into is not None
    acc_shape = tuple(d for d in o_blk if d is not None)

    def body(*refs):
        a_ref, b_ref = refs[0], refs[1]
        pos = 2
        res_ref = bias_ref = None
        if has_res:
            res_ref = refs[pos]
            pos += 1
        if has_bias:
            bias_ref = refs[pos]
            pos += 1
        if has_into:
            pos += 1
        if after is not None:
            pos += 1
        o_ref = refs[pos]
        av, bv = a_ref[...], b_ref[...]
        if bv.ndim == 3:
            bv = bv.reshape(-1, bv.shape[-1])
        if av.dtype != BF16:
            av = av.astype(BF16)
        if bv.dtype != BF16:
            bv = bv.astype(BF16)
        part = lax.dot_general(av, bv, dims, preferred_element_type=F32)

        def finish(acc):
            if scale is not None:
                acc = acc * scale
            if has_bias:
                acc = acc + bias_ref[...]
            if has_res:
                acc = acc + res_ref[...]
            o_ref[...] = acc.astype(o_ref.dtype)

        if nk == 1:
            finish(part)
        else:
            acc_ref = refs[pos + 1]
            k = pl.program_id(2)

            @pl.when(k == 0)
            def _():
                acc_ref[...] = part

            @pl.when(k > 0)
            def _():
                acc_ref[...] += part

            @pl.when(k == nk - 1)
            def _():
                finish(acc_ref[...])

    operands = [a, b]
    in_specs = [pl.BlockSpec(a_blk, a_map), pl.BlockSpec(b_blk, b_map)]
    if has_res:
        operands.append(res)
        in_specs.append(pl.BlockSpec(o_blk, o_map))
    if has_bias:
        operands.append(bias)
        in_specs.append(pl.BlockSpec(bias_blk, bias_map))
    aliases = {}
    if has_into:
        aliases = {len(operands): 0}
        operands.append(alias_into)
        in_specs.append(pl.BlockSpec(memory_space=pl.ANY))
    if after is not None:
        operands.append(after)
        in_specs.append(pl.BlockSpec(memory_space=pl.ANY))
    return pl.pallas_call(
        body, name=name, grid=grid, in_specs=in_specs, out_specs=pl.BlockSpec(o_blk, o_map),
        out_shape=jax.ShapeDtypeStruct(o_shape, o_dtype),
        scratch_shapes=[pltpu.VMEM(acc_shape, F32)] if nk > 1 else [],
        input_output_aliases=aliases,
        compiler_params=_cparams(("parallel", "parallel", "arbitrary")),
    )(*operands)


def mm_nn(name, a, w3, kind, layer, *, out_dtype, scale=None, res=None, bias=None, tm=1024, tn=512, tk=2048,
          after=None, whole_k=False):
    M, K = a.shape
    S, _, C = w3.shape
    tm = _tile(M, tm, 16)
    b_blk = None
    if kind == "col":
        N = S * C
        tk, tn = _tile(K, tk), _tile(C, tn)
        kb, nb = K // tk, C // tn
        b_map = lambda n, m, k: (n // nb, layer * kb + k, n % nb)
    elif whole_k:
        N, K4, tk, tn = C, K // S, K, _tile(C, tn)
        b_blk, b_map = (S, K4, tn), lambda n, m, k: (0, layer, n)
    else:
        N, K4 = C, K // S
        tk, tn = _tile(K4, tk), _tile(C, tn)
        kb4 = K4 // tk
        b_map = lambda n, m, k: (k // kb4, layer * kb4 + k % kb4, n)
    return _matmul(
        name, a, w3, grid=(N // tn, M // tm, K // tk),
        a_blk=(tm, tk), a_map=lambda n, m, k: (m, k), b_blk=b_blk or (None, tk, tn), b_map=b_map,
        o_shape=(M, N), o_dtype=out_dtype, o_blk=(tm, tn), o_map=lambda n, m, k: (m, n), dims=NN_DIMS,
        scale=scale, res=res, bias=bias, bias_blk=(1, tn), bias_map=lambda n, m, k: (0, n), after=after)


def _mm_nt(name, g, w3, kind, layer, K, *, out_dtype, scale=None, tm=1024, to=1024, tc=1408, after=None):
    S, _, C = w3.shape
    tc = _tile(C, tc)
    if g.ndim == 3:
        M, N = g.shape[1], 2 * g.shape[2]
        tm = _tile(M, tm, 16)
        cb = g.shape[2] // tc
        a_blk, a_map = (None, tm, tc), lambda o, m, c: (c // cb, m, c % cb)
    else:
        M, N = g.shape
        tm = _tile(M, tm, 16)
        a_blk, a_map = (tm, tc), lambda o, m, c: (m, c)
    if kind == "col":
        nb = C // tc
        to = _tile(K, to)
        ob = K // to
        b_map = lambda o, m, c: (c // nb, layer * ob + o, c % nb)
    else:
        K4 = K // S
        to = _tile(K4, to)
        ob4 = K4 // to
        b_map = lambda o, m, c: (o // ob4, layer * ob4 + o % ob4, c)
    return _matmul(
        name, g, w3, grid=(K // to, M // tm, N // tc),
        a_blk=a_blk, a_map=a_map, b_blk=(None, to, tc), b_map=b_map,
        o_shape=(M, K), o_dtype=out_dtype, o_blk=(tm, to), o_map=lambda o, m, c: (m, o), dims=NT_DIMS,
        scale=scale, after=after)


def mm_tn(name, a, g, kind, *, shards=N_CHIPS, layer=0, layers=1, into=None, out_dtype=BF16, scale=None,
          tk=1024, tn=1408, tm=2048, after=None):
    M, K = a.shape
    N = 2 * g.shape[2] if g.ndim == 3 else g.shape[1]
    tm = _tile(M, tm, 16)
    if kind == "col":
        C = N // shards
        tk, tn = _tile(K, tk), _tile(C, tn)
        kb, nb = K // tk, C // tn
        o_shape = (shards, layers * K, C)
        o_map = lambda k, n, m: (n // nb, layer * kb + k, n % nb)
    else:
        K4 = K // shards
        tk, tn = _tile(K4, tk), _tile(N, tn)
        kb4 = K4 // tk
        o_shape = (shards, layers * K4, N)
        o_map = lambda k, n, m: (k // kb4, layer * kb4 + k % kb4, n)
    if g.ndim == 3:
        nbh = g.shape[2] // tn
        g_blk, g_map = (None, tm, tn), lambda k, n, m: (n // nbh, m, n % nbh)
    else:
        g_blk, g_map = (tm, tn), lambda k, n, m: (m, n)
    return _matmul(
        name, a, g, grid=(K // tk, N // tn, M // tm),
        a_blk=(tm, tk), a_map=lambda k, n, m: (m, k), b_blk=g_blk, b_map=g_map,
        o_shape=o_shape, o_dtype=out_dtype, o_blk=(None, tk, tn), o_map=o_map, dims=TN_DIMS,
        scale=scale, alias_into=into, after=after)


def ffn_in_act(name, hn, w3, *, tn=256, after=None):
    M, K = hn.shape
    S, _, C = w3.shape
    tn = _tile(C, tn)
    nb = C // tn

    def body(a_ref, bg_ref, bu_ref, *rest):
        gu_ref, act_ref = rest[-2:]
        a = a_ref[...]
        g = jnp.dot(a, bg_ref[...], preferred_element_type=F32)
        u = jnp.dot(a, bu_ref[...], preferred_element_type=F32)
        gu_ref[0] = g.astype(BF16)
        gu_ref[1] = u.astype(BF16)
        act_ref[...] = (g * _sigmoid(g) * u).astype(BF16)

    in_specs = [pl.BlockSpec((M, K), lambda j: (0, 0)),
                pl.BlockSpec((None, K, tn), lambda j: (j // nb, 0, j % nb)),
                pl.BlockSpec((None, K, tn), lambda j: (S // 2 + j // nb, 0, j % nb))]
    operands = [hn, w3, w3]
    if after is not None:
        in_specs.append(pl.BlockSpec(memory_space=pl.ANY))
        operands.append(after)
    width = S // 2 * C
    return pl.pallas_call(
        body, name=name, grid=(width // tn,), in_specs=in_specs,
        out_specs=[pl.BlockSpec((2, M, tn), lambda j: (0, 0, j)), pl.BlockSpec((M, tn), lambda j: (0, j))],
        out_shape=[jax.ShapeDtypeStruct((2, M, width), BF16), jax.ShapeDtypeStruct((M, width), BF16)],
        compiler_params=_cparams(("arbitrary",)))(*operands)


def ffn_dact_dgu(name, d_out, w3, gu, scale, *, tm=512, after=None):
    M, N = d_out.shape
    S, K4, _ = w3.shape
    tm = _tile(M, tm, 16)
    to = K4
    chunk = 256 if to > 256 else to

    def body(a_ref, b_ref, gu_ref, *rest):
        o_ref = rest[-1]
        a = a_ref[...]
        for c0 in range(0, to, chunk):
            cols = slice(c0, min(c0 + chunk, to))
            da = lax.dot_general(a, b_ref[cols, :], NT_DIMS, preferred_element_type=F32) * scale
            g = gu_ref[0, :, cols].astype(F32)
            u = gu_ref[1, :, cols].astype(F32)
            s = _sigmoid(g)
            o_ref[0, :, cols] = (da * u * s * (1.0 + g * (1.0 - s))).astype(BF16)
            o_ref[1, :, cols] = (da * g * s).astype(BF16)

    halves = pl.BlockSpec((2, tm, to), lambda o, m: (0, m, o))
    in_specs = [pl.BlockSpec((tm, N), lambda o, m: (m, 0)), pl.BlockSpec((None, to, N), lambda o, m: (o, 0, 0)), halves]
    operands = [d_out, w3, gu]
    if after is not None:
        in_specs.append(pl.BlockSpec(memory_space=pl.ANY))
        operands.append(after)
    return pl.pallas_call(
        body, name=name, grid=(S, M // tm), in_specs=in_specs, out_specs=halves,
        out_shape=jax.ShapeDtypeStruct((2, M, S * K4), BF16),
        compiler_params=_cparams(("parallel", "arbitrary")))(*operands)


def rope_tables(name, pos):
    T = pos.shape[0]
    half = D_ROPE // 2

    def body(p_ref, c_ref, s1_ref, s2_ref):
        lane = lax.broadcasted_iota(jnp.int32, (T, 128), 1)
        idx = (lane & (half - 1)).astype(F32)
        ang = p_ref[...] * jnp.exp(idx * (-2.0 * math.log(ROPE_THETA) / D_ROPE))
        cs, sn = jnp.cos(ang), jnp.sin(ang)
        c_ref[...] = jnp.where(lane < D_ROPE, cs, 0.0)
        s1_ref[...] = jnp.where(lane < half, -sn, 0.0)
        s2_ref[...] = jnp.where((lane >= half) & (lane < D_ROPE), sn, 0.0)

    return pl.pallas_call(body, name=name, out_shape=[jax.ShapeDtypeStruct((T, 128), F32)] * 3,
                          compiler_params=_cparams())(pos)


def _rope(v, cs, s1, s2):
    return v * cs + pltpu.roll(v, 128 - D_ROPE // 2, 1) * s1 + pltpu.roll(v, D_ROPE // 2, 1) * s2


def _rope_bwd(d, cs, s1, s2):
    return d * cs + pltpu.roll(d * s1, D_ROPE // 2, 1) + pltpu.roll(d * s2, 128 - D_ROPE // 2, 1)


def _head_rstd(n, r):
    ms = jnp.sum(n * n + r * r, axis=-1, keepdims=True) * (1.0 / QK_DIM)
    return lax.rsqrt(ms + EPS)


def mla_prep_fwd(name, q_raw, kv_raw, lat, tabs, q_gain, k_gain, rope_col, tm=128):
    T = q_raw.shape[0]
    H = q_raw.shape[1] // HEAD_PAD
    tm = min(tm, T)
    rope_blk = rope_col // 128

    def body(q_ref, kv_ref, kr_ref, c_ref, s1_ref, s2_ref, qg_ref, kg_ref, Q_ref, K_ref, V_ref):
        cs, s1, s2 = c_ref[...], s1_ref[...], s2_ref[...]
        qg, kg = qg_ref[...], kg_ref[...]
        kr = kr_ref[...]
        for h in range(H):
            lo = HEAD_PAD * h
            n, r = q_ref[:, lo:lo + 128], q_ref[:, lo + 128:lo + 256]
            rs = _head_rstd(n, r)
            Q_ref[h, :, 0:128] = (n * rs * qg[:, :128]).astype(BF16)
            Q_ref[h, :, 128:256] = _rope(r * rs * qg[:, 128:], cs, s1, s2).astype(BF16)
            n = kv_ref[:, lo:lo + 128]
            rs = _head_rstd(n, kr)
            K_ref[h, :, 0:128] = (n * rs * kg[:, :128]).astype(BF16)
            K_ref[h, :, 128:256] = _rope(kr * rs * kg[:, 128:], cs, s1, s2).astype(BF16)
            V_ref[h] = kv_ref[:, lo + 128:lo + 256].astype(BF16)

    row = lambda w: pl.BlockSpec((tm, w), lambda i: (i, 0))
    vec = pl.BlockSpec((1, HEAD_PAD), lambda i: (0, 0))
    return pl.pallas_call(
        body, name=name, grid=(T // tm,),
        in_specs=[row(H * HEAD_PAD), row(H * HEAD_PAD), pl.BlockSpec((tm, 128), lambda i: (i, rope_blk)),
                  row(128), row(128), row(128), vec, vec],
        out_specs=[pl.BlockSpec((H, tm, HEAD_PAD), lambda i: (0, i, 0))] * 2 + [pl.BlockSpec((H, tm, D_V), lambda i: (0, i, 0))],
        out_shape=[jax.ShapeDtypeStruct((H, T, HEAD_PAD), BF16)] * 2 + [jax.ShapeDtypeStruct((H, T, D_V), BF16)],
        compiler_params=_cparams(("arbitrary",)),
    )(q_raw, kv_raw, lat, *tabs, q_gain, k_gain)


def mla_prep_bwd(name, dQ, dK, dV, q_raw, kv_raw, lat, tabs, q_gain, k_gain, rope_col, tm=128):
    T = q_raw.shape[0]
    H = q_raw.shape[1] // HEAD_PAD
    tm = min(tm, T)
    rope_blk = rope_col // 128

    def body(dQ_ref, dK_ref, dV_ref, q_ref, kv_ref, kr_ref, c_ref, s1_ref, s2_ref, qg_ref, kg_ref,
             dq_ref, dkv_ref, dkr_ref, dqg_ref, dkg_ref):
        @pl.when(pl.program_id(0) == 0)
        def _():
            dqg_ref[...] = jnp.zeros_like(dqg_ref)
            dkg_ref[...] = jnp.zeros_like(dkg_ref)

        cs, s1, s2 = c_ref[...], s1_ref[...], s2_ref[...]
        qg, kg = qg_ref[...], kg_ref[...]
        kr = kr_ref[...]
        dkr = jnp.zeros((tm, 128), F32)
        gq_n = jnp.zeros((1, 128), F32)
        gq_r = jnp.zeros((1, 128), F32)
        gk_n = jnp.zeros((1, 128), F32)
        gk_r = jnp.zeros((1, 128), F32)

        def norm_bwd(n, r, dn, dr, gain):
            rs = _head_rstd(n, r)
            nh, rh = n * rs, r * rs
            dng, drg = dn * gain[:, :128], dr * gain[:, 128:]
            mean = jnp.sum(dng * nh + drg * rh, axis=-1, keepdims=True) * (1.0 / QK_DIM)
            return rs * (dng - nh * mean), rs * (drg - rh * mean), _colsum(dn * nh), _colsum(dr * rh)

        for h in range(H):
            lo = HEAD_PAD * h
            n, r = q_ref[:, lo:lo + 128], q_ref[:, lo + 128:lo + 256]
            dn = dQ_ref[h, :, 0:128].astype(F32)
            dr = _rope_bwd(dQ_ref[h, :, 128:256].astype(F32), cs, s1, s2)
            a, b, g1, g2 = norm_bwd(n, r, dn, dr, qg)
            dq_ref[:, lo:lo + 128] = a.astype(BF16)
            dq_ref[:, lo + 128:lo + 256] = b.astype(BF16)
            gq_n, gq_r = gq_n + g1, gq_r + g2
            n = kv_ref[:, lo:lo + 128]
            dn = dK_ref[h, :, 0:128].astype(F32)
            dr = _rope_bwd(dK_ref[h, :, 128:256].astype(F32), cs, s1, s2)
            a, b, g1, g2 = norm_bwd(n, kr, dn, dr, kg)
            dkv_ref[:, lo:lo + 128] = a.astype(BF16)
            dkv_ref[:, lo + 128:lo + 256] = dV_ref[h].astype(BF16)
            dkr = dkr + b
            gk_n, gk_r = gk_n + g1, gk_r + g2
        dkr_ref[...] = dkr
        dqg_ref[:, 0:128] += gq_n
        dqg_ref[:, 128:256] += gq_r
        dkg_ref[:, 0:128] += gk_n
        dkg_ref[:, 128:256] += gk_r

    row = lambda w: pl.BlockSpec((tm, w), lambda i: (i, 0))
    vec = pl.BlockSpec((1, HEAD_PAD), lambda i: (0, 0))
    hd = lambda w: pl.BlockSpec((H, tm, w), lambda i: (0, i, 0))
    return pl.pallas_call(
        body, name=name, grid=(T // tm,),
        in_specs=[hd(HEAD_PAD), hd(HEAD_PAD), hd(D_V), row(H * HEAD_PAD), row(H * HEAD_PAD),
                  pl.BlockSpec((tm, 128), lambda i: (i, rope_blk)), row(128), row(128), row(128), vec, vec],
        out_specs=[row(H * HEAD_PAD), row(H * HEAD_PAD), row(128), vec, vec],
        out_shape=[jax.ShapeDtypeStruct((T, H * HEAD_PAD), BF16)] * 2 + [jax.ShapeDtypeStruct((T, 128), F32)]
        + [jax.ShapeDtypeStruct((1, HEAD_PAD), F32)] * 2,
        compiler_params=_cparams(("arbitrary",)),
    )(dQ, dK, dV, q_raw, kv_raw, lat, *tabs, q_gain, k_gain)


def _causal_probs(q, k, scale, row0):
    s = lax.dot_general(q, k, NT_DIMS, preferred_element_type=F32) * scale
    row = row0 + lax.broadcasted_iota(jnp.int32, s.shape, 0)
    col = lax.broadcasted_iota(jnp.int32, s.shape, 1)
    s = jnp.where(col <= row, s, NEG_BIG)
    p = jnp.exp(s - jnp.max(s, axis=-1, keepdims=True))
    return p, jnp.sum(p, axis=-1, keepdims=True)


def attn_fwd(name, Q, K, V, tq=512):
    H, T, E = Q.shape
    tq = min(tq, T)
    nq = T // tq
    scale = QK_DIM ** -0.5

    def body(q_ref, k_ref, v_ref, o_ref):
        i = pl.program_id(1)
        for ib in range(nq):
            @pl.when(i == ib)
            def _():
                n = (ib + 1) * tq
                p, l = _causal_probs(q_ref[...], k_ref[0:n, :], scale, ib * tq)
                o = jnp.dot(p.astype(BF16), v_ref[0:n, :], preferred_element_type=F32)
                o_ref[...] = (o / l).astype(o_ref.dtype)

    return pl.pallas_call(
        body, name=name, grid=(H, nq),
        in_specs=[pl.BlockSpec((None, tq, E), lambda h, i: (h, i, 0)),
                  pl.BlockSpec((None, T, E), lambda h, i: (h, 0, 0)),
                  pl.BlockSpec((None, T, D_V), lambda h, i: (h, 0, 0))],
        out_specs=pl.BlockSpec((tq, D_V), lambda h, i: (i, h)),
        out_shape=jax.ShapeDtypeStruct((T, H * D_V), BF16),
        compiler_params=_cparams(("parallel", "arbitrary")),
    )(Q, K, V)


def attn_bwd(name, Q, K, V, dO, tq=512):
    H, T, E = Q.shape
    tq = min(tq, T)
    nq = T // tq
    scale = QK_DIM ** -0.5

    def body(q_ref, k_ref, v_ref, do_ref, dq_ref, dk_ref, dv_ref):
        i = pl.program_id(1)

        @pl.when(i == 0)
        def _():
            dk_ref[...] = jnp.zeros_like(dk_ref)
            dv_ref[...] = jnp.zeros_like(dv_ref)

        for ib in range(nq):
            @pl.when(i == ib)
            def _():
                n = (ib + 1) * tq
                q, k, v, do = q_ref[...], k_ref[0:n, :], v_ref[0:n, :], do_ref[...]
                p, l = _causal_probs(q, k, scale, ib * tq)
                p = p / l
                dp = lax.dot_general(do, v, NT_DIMS, preferred_element_type=F32)
                ds = p * (dp - jnp.sum(p * dp, axis=-1, keepdims=True)) * scale
                dsb, pb = ds.astype(BF16), p.astype(BF16)
                dq_ref[...] = jnp.dot(dsb, k, preferred_element_type=F32)
                dk_ref[0:n, :] += lax.dot_general(dsb, q, TN_DIMS, preferred_element_type=F32)
                dv_ref[0:n, :] += lax.dot_general(pb, do, TN_DIMS, preferred_element_type=F32)

    return pl.pallas_call(
        body, name=name, grid=(H, nq),
        in_specs=[pl.BlockSpec((None, tq, E), lambda h, i: (h, i, 0)),
                  pl.BlockSpec((None, T, E), lambda h, i: (h, 0, 0)),
                  pl.BlockSpec((None, T, D_V), lambda h, i: (h, 0, 0)),
                  pl.BlockSpec((tq, D_V), lambda h, i: (i, h))],
        out_specs=[pl.BlockSpec((None, tq, E), lambda h, i: (h, i, 0)),
                   pl.BlockSpec((None, T, E), lambda h, i: (h, 0, 0)),
                   pl.BlockSpec((None, T, D_V), lambda h, i: (h, 0, 0))],
        out_shape=[jax.ShapeDtypeStruct((H, T, E), F32)] * 2 + [jax.ShapeDtypeStruct((H, T, D_V), F32)],
        compiler_params=_cparams(("parallel", "arbitrary")),
    )(Q, K, V, dO)


def _taps_by_residue(offsets):
    groups = {}
    for k, off in enumerate(offsets):
        groups.setdefault(off % 8, []).append((k, off - off % 8))
    return groups


def _dw_specs(T, C, tm, tc, halo):
    cur = pl.BlockSpec((tm, tc), lambda j, i: (i, j))
    last = T // tm - 1
    if halo == "prev":
        nbr = pl.BlockSpec((tm, tc), lambda j, i: (jnp.maximum(i - 1, 0), j))
    else:
        nbr = pl.BlockSpec((tm, tc), lambda j, i: (jnp.minimum(i + 1, last), j))
    return cur, nbr


def dwconv_fwd(name, u, w, b, tm=256, tc=512, rs=32):
    T, C = u.shape
    tm, tc = min(tm, T), min(tc, C)
    cur, prev = _dw_specs(T, C, tm, tc, "prev")

    def body(up_ref, uc_ref, w_ref, b_ref, o_ref, scr):
        i = pl.program_id(1)

        @pl.when(i == 0)
        def _():
            scr[pl.ds(0, tm), :] = jnp.zeros((tm, tc), F32)

        @pl.when(i > 0)
        def _():
            scr[pl.ds(0, tm), :] = up_ref[...]

        scr[pl.ds(tm, tm), :] = uc_ref[...]
        for s in range(tm // rs):
            acc = jnp.broadcast_to(b_ref[...], (rs, tc))
            for k in range(CONV_WIDTH):
                acc = acc + w_ref[pl.ds(k, 1), :] * scr[pl.ds(tm - (CONV_WIDTH - 1) + k + rs * s, rs), :]
            o_ref[pl.ds(rs * s, rs), :] = acc

    return pl.pallas_call(
        body, name=name, grid=(C // tc, T // tm),
        in_specs=[prev, cur, pl.BlockSpec((CONV_TAPS_PAD, tc), lambda j, i: (0, j)), pl.BlockSpec((1, tc), lambda j, i: (0, j))],
        out_specs=cur, out_shape=jax.ShapeDtypeStruct((T, C), F32),
        scratch_shapes=[pltpu.VMEM((2 * tm, tc), F32)], compiler_params=_cparams(("parallel", "arbitrary")),
    )(u, u, w, b)


def dwconv_bwd_u(name, dy, w, tm=256, tc=512, rs=32):
    T, C = dy.shape
    tm, tc = min(tm, T), min(tc, C)
    cur, nxt = _dw_specs(T, C, tm, tc, "next")
    last = T // tm - 1

    taps = _taps_by_residue([(CONV_WIDTH - 1) - k for k in range(CONV_WIDTH)])

    def body(dc_ref, dn_ref, w_ref, o_ref, scr, tmp):
        i = pl.program_id(1)
        scr[pl.ds(0, tm), :] = dc_ref[...]
        scr[pl.ds(2 * tm, 8), :] = jnp.zeros((8, tc), F32)

        @pl.when(i == last)
        def _():
            scr[pl.ds(tm, tm), :] = jnp.zeros((tm, tc), F32)

        @pl.when(i < last)
        def _():
            scr[pl.ds(tm, tm), :] = dn_ref[...]

        for s in range(tm // rs):
            acc = jnp.zeros((rs, tc), F32)
            for r, group in taps.items():
                a = jnp.zeros((rs + 8, tc), F32)
                for k, base in group:
                    a = a + w_ref[pl.ds(k, 1), :] * scr[pl.ds(base + rs * s, rs + 8), :]
                if r == 0:
                    acc = acc + a[:rs]
                else:
                    tmp[...] = a
                    acc = acc + tmp[pl.ds(r, rs), :]
            o_ref[pl.ds(rs * s, rs), :] = acc

    return pl.pallas_call(
        body, name=name, grid=(C // tc, T // tm),
        in_specs=[cur, nxt, pl.BlockSpec((CONV_TAPS_PAD, tc), lambda j, i: (0, j))],
        out_specs=cur, out_shape=jax.ShapeDtypeStruct((T, C), F32),
        scratch_shapes=[pltpu.VMEM((2 * tm + 8, tc), F32), pltpu.VMEM((rs + 8, tc), F32)],
        compiler_params=_cparams(("parallel", "arbitrary")),
    )(dy, dy, w)


def dwconv_bwd_w(name, u, dy, tm=256, tc=512, rs=32):
    T, C = u.shape
    tm, tc = min(tm, T), min(tc, C)
    cur, prev = _dw_specs(T, C, tm, tc, "prev")

    taps = _taps_by_residue([tm - (CONV_WIDTH - 1) + k for k in range(CONV_WIDTH)])
    span = tm + 8

    def body(up_ref, uc_ref, dy_ref, o_ref, scr, dyp, dys):
        i = pl.program_id(1)

        @pl.when(i == 0)
        def _():
            scr[pl.ds(0, tm), :] = jnp.zeros((tm, tc), F32)
            o_ref[...] = jnp.zeros_like(o_ref)

        @pl.when(i > 0)
        def _():
            scr[pl.ds(0, tm), :] = up_ref[...]

        scr[pl.ds(tm, tm), :] = uc_ref[...]
        scr[pl.ds(2 * tm, 8), :] = jnp.zeros((8, tc), F32)
        dyp[pl.ds(0, 8), :] = jnp.zeros((8, tc), F32)
        dyp[pl.ds(8, tm), :] = dy_ref[...]
        dyp[pl.ds(8 + tm, 8), :] = jnp.zeros((8, tc), F32)
        for r in taps:
            dys[r] = dyp[pl.ds(8 - r, span), :]
        for r, group in taps.items():
            for k, base in group:
                acc = jnp.zeros((8, tc), F32)
                for s in range(span // 8):
                    acc = acc + dys[r, pl.ds(8 * s, 8), :] * scr[pl.ds(base + 8 * s, 8), :]
                o_ref[pl.ds(k, 1), :] += _colsum(acc)

    return pl.pallas_call(
        body, name=name, grid=(C // tc, T // tm),
        in_specs=[prev, cur, cur],
        out_specs=pl.BlockSpec((CONV_TAPS_PAD, tc), lambda j, i: (0, j)),
        out_shape=jax.ShapeDtypeStruct((CONV_TAPS_PAD, C), F32),
        scratch_shapes=[pltpu.VMEM((2 * tm + 8, tc), F32), pltpu.VMEM((tm + 16, tc), F32),
                        pltpu.VMEM((8, span, tc), F32)],
        compiler_params=_cparams(("parallel", "arbitrary")),
    )(u, u, dy)


def _place():
    x, y, c = lax.axis_index("x"), lax.axis_index("y"), lax.axis_index("c")
    return x, y, c


def _other_chips(x, y):
    return [(1 - x, y, 2 * (1 - x) + y), (x, 1 - y, 2 * x + (1 - y)), (1 - x, 1 - y, 2 * (1 - x) + (1 - y))]


def _hbm_specs(n):
    return [pl.BlockSpec(memory_space=pl.ANY)] * n


HBM_SPEC = pl.BlockSpec(memory_space=pltpu.HBM)
SEM_SPEC = pl.BlockSpec(memory_space=pltpu.SEMAPHORE)
ANY_SPEC = pl.BlockSpec(memory_space=pl.ANY)
SIDE_EFFECT = pltpu.SideEffectType.DATAFLOW_SIDE_EFFECTING


def _half(ref, slot, which):
    rh = ref.shape[1] // 2
    return ref.at[slot, pl.ds(pl.multiple_of(which * rh, 16), rh), :]


def _hbm(a):
    return pltpu.with_memory_space_constraint(a, pltpu.HBM)


def gather_start(name, groups, after):
    flat = [b for g in groups for b in g]
    n, ng = len(flat), len(groups)

    def body(*refs):
        send, recv = refs[n + 1:n + 1 + ng], refs[n + 1 + ng:n + 1 + 2 * ng]
        out, token = refs[n + 1 + 2 * ng:2 * n + 1 + 2 * ng], refs[2 * n + 1 + 2 * ng]
        token[...] = jnp.zeros_like(token)
        x, y, c = _place()
        me = 2 * x + y
        a = 0
        for g, grp in enumerate(groups):
            for k in range(len(grp)):
                piece = _half(out[a], me, c)
                for j, (px, py, _) in enumerate(_other_chips(x, y)):
                    pltpu.make_async_remote_copy(
                        src_ref=piece, dst_ref=piece, send_sem=send[g].at[3 * k + j], recv_sem=recv[g].at[3 * k + j],
                        device_id=(px, py, c), device_id_type=MESH).start()
                a += 1

    sems = [pltpu.SemaphoreType.DMA((3 * len(g),)) for g in groups]
    res = pl.pallas_call(
        body, name=name, in_specs=[HBM_SPEC] * n + [ANY_SPEC],
        out_specs=[SEM_SPEC] * (2 * ng) + [HBM_SPEC] * n + [pl.BlockSpec(memory_space=pltpu.VMEM)],
        out_shape=sems + sems + [pltpu.HBM(b.shape, b.dtype) for b in flat] + [jax.ShapeDtypeStruct((8, 128), F32)],
        input_output_aliases={a: 2 * ng + a for a in range(n)},
        compiler_params=pltpu.CompilerParams(has_side_effects=SIDE_EFFECT),
    )(*[_hbm(b) for b in flat], after)
    send, recv, bufs = res[:ng], res[ng:2 * ng], list(res[2 * ng:2 * ng + n])
    out, a = [], 0
    for g, grp in enumerate(groups):
        out.append((send[g], recv[g], bufs[a:a + len(grp)]))
        a += len(grp)
    return out, res[2 * ng + n]


def gather_relay(name, started, after):
    send1, recv1, bufs = started
    n = len(bufs)

    def body(*refs):
        s1, r1 = refs[n], refs[n + 1]
        s2, r2, out, token = refs[n + 3], refs[n + 4], refs[n + 5:2 * n + 5], refs[2 * n + 5]
        x, y, c = _place()
        me = 2 * x + y
        chips = _other_chips(x, y)
        for k in range(n):
            for j, (px, py, idx) in enumerate(chips):
                cp = pltpu.make_async_remote_copy(
                    src_ref=_half(out[k], me, c), dst_ref=_half(out[k], idx, c), send_sem=s1.at[3 * k + j],
                    recv_sem=r1.at[3 * k + j], device_id=(px, py, c), device_id_type=MESH)
                cp.wait_send()
                cp.wait_recv()
        for k in range(n):
            for j, (px, py, idx) in enumerate(chips):
                piece = _half(out[k], idx, c)
                pltpu.make_async_remote_copy(
                    src_ref=piece, dst_ref=piece, send_sem=s2.at[3 * k + j], recv_sem=r2.at[3 * k + j],
                    device_id=(x, y, 1 - c), device_id_type=MESH).start()
        token[...] = jnp.zeros_like(token)

    sem = pltpu.SemaphoreType.DMA((3 * n,))
    res = pl.pallas_call(
        body, name=name, in_specs=[HBM_SPEC] * n + [SEM_SPEC, SEM_SPEC, ANY_SPEC],
        out_specs=[SEM_SPEC, SEM_SPEC] + [HBM_SPEC] * n + [pl.BlockSpec(memory_space=pltpu.VMEM)],
        out_shape=[sem, sem] + [pltpu.HBM(b.shape, b.dtype) for b in bufs] + [jax.ShapeDtypeStruct((8, 128), F32)],
        input_output_aliases={a: 2 + a for a in range(n)},
        compiler_params=pltpu.CompilerParams(has_side_effects=SIDE_EFFECT),
    )(*bufs, send1, recv1, after)
    return res[0], res[1], list(res[2:2 + n]), res[2 + n]


def gather_wait(name, relayed, after):
    send2, recv2, bufs, _ = relayed
    n = len(bufs)

    def body(*refs):
        s2, r2, out = refs[n], refs[n + 1], refs[n + 3:]
        x, y, c = _place()
        for k in range(n):
            for j, (px, py, idx) in enumerate(_other_chips(x, y)):
                cp = pltpu.make_async_remote_copy(
                    src_ref=_half(out[k], idx, c), dst_ref=_half(out[k], idx, 1 - c), send_sem=s2.at[3 * k + j],
                    recv_sem=r2.at[3 * k + j], device_id=(x, y, 1 - c), device_id_type=MESH)
                cp.wait_send()
                cp.wait_recv()

    res = pl.pallas_call(
        body, name=name, in_specs=[HBM_SPEC] * n + [SEM_SPEC, SEM_SPEC, ANY_SPEC], out_specs=[HBM_SPEC] * n,
        out_shape=[pltpu.HBM(b.shape, b.dtype) for b in bufs], input_output_aliases={a: a for a in range(n)},
        compiler_params=pltpu.CompilerParams(has_side_effects=SIDE_EFFECT),
    )(*bufs, send2, recv2, after)
    return list(res)


def split_start(name, parts, after):
    bufs = [b for p in parts for b in p[0]]
    land_shapes = [s for p in parts for s in p[1]]
    nb, nl = len(bufs), len(land_shapes)

    def body(*refs):
        send, recv = refs[nb + 1], refs[nb + 2]
        out, lands, token = refs[nb + 3:2 * nb + 3], refs[2 * nb + 3:2 * nb + 3 + nl], refs[2 * nb + 3 + nl]
        x, y, c = _place()
        b0 = l0 = k0 = 0
        for p_bufs, p_lands, n_copies, plan in parts:
            copies = plan(out[b0:b0 + len(p_bufs)], lands[l0:l0 + len(p_lands)], x, y, c)
            for k, (src, dst, to, _) in enumerate(copies):
                pltpu.make_async_remote_copy(src_ref=src, dst_ref=dst, send_sem=send.at[k0 + k],
                                             recv_sem=recv.at[k0 + k], device_id=to, device_id_type=MESH).start()
            b0, l0, k0 = b0 + len(p_bufs), l0 + len(p_lands), k0 + n_copies
        token[...] = jnp.zeros_like(token)

    sem = pltpu.SemaphoreType.DMA((sum(p[2] for p in parts),))
    res = pl.pallas_call(
        body, name=name, in_specs=[HBM_SPEC] * nb + [ANY_SPEC],
        out_specs=[SEM_SPEC, SEM_SPEC] + [HBM_SPEC] * (nb + nl) + [pl.BlockSpec(memory_space=pltpu.VMEM)],
        out_shape=[sem, sem] + [pltpu.HBM(b.shape, b.dtype) for b in bufs]
        + [pltpu.HBM(s, d) for s, d in land_shapes] + [jax.ShapeDtypeStruct((8, 128), F32)],
        input_output_aliases={a: 2 + a for a in range(nb)},
        compiler_params=pltpu.CompilerParams(has_side_effects=SIDE_EFFECT),
    )(*[_hbm(b) for b in bufs], after)
    out_bufs, out_lands, token = list(res[2:2 + nb]), list(res[2 + nb:2 + nb + nl]), res[2 + nb + nl]
    states, b0, l0, k0 = [], 0, 0, 0
    for p_bufs, p_lands, n_copies, _ in parts:
        states.append((res[0], res[1], out_bufs[b0:b0 + len(p_bufs)], out_lands[l0:l0 + len(p_lands)], token, k0))
        b0, l0, k0 = b0 + len(p_bufs), l0 + len(p_lands), k0 + n_copies
    return states


def split_wait(name, started, plan, after):
    send, recv, bufs, lands, _, k0 = started
    nb, nl = len(bufs), len(lands)

    def body(*refs):
        s, r = refs[nb + nl], refs[nb + nl + 1]
        out, lo = refs[nb + nl + 3:2 * nb + nl + 3], refs[2 * nb + nl + 3:]
        x, y, c = _place()
        for k, (src, _, to, landed) in enumerate(plan(out, lo, x, y, c)):
            cp = pltpu.make_async_remote_copy(src_ref=src, dst_ref=landed, send_sem=s.at[k0 + k],
                                              recv_sem=r.at[k0 + k], device_id=to, device_id_type=MESH)
            cp.wait_send()
            cp.wait_recv()

    res = pl.pallas_call(
        body, name=name, in_specs=[HBM_SPEC] * (nb + nl) + [SEM_SPEC, SEM_SPEC, ANY_SPEC],
        out_specs=[HBM_SPEC] * (nb + nl), out_shape=[pltpu.HBM(b.shape, b.dtype) for b in bufs + lands],
        input_output_aliases={a: a for a in range(nb + nl)},
        compiler_params=pltpu.CompilerParams(has_side_effects=SIDE_EFFECT),
    )(*bufs, *lands, send, recv, after)
    return list(res[:nb]), list(res[nb:])


def swap_halves_plan(parts, lands, x, y, c):
    out = []
    for a in range(len(parts)):
        rh = parts[a].shape[1] // 2
        theirs = parts[a].at[:, pl.ds(pl.multiple_of((1 - c) * rh, 16), rh), :]
        out.append((theirs, lands[a], (x, y, 1 - c), lands[a]))
    return out


def chip_exchange_plan(sums, lands, x, y, c):
    me = 2 * x + y
    out = []
    for a in range(len(sums)):
        for (px, py, idx), flip in zip(_other_chips(x, y), (2, 1, 3)):
            out.append((sums[a].at[flip - 1], lands[a].at[me], (px, py, c), lands[a].at[idx]))
    return out


def share_grad_plan(halves, lands, x, y, c):
    return [(halves[a], lands[a], (x, y, 1 - c), lands[a]) for a in range(len(halves))]


def allreduce_pack(name, pack):
    R, W = pack.shape

    def body(p_ref, o_ref, sib, pair, got, send_sems, recv_sems):
        x, y, c = _place()

        def swap(k, src, dst, to):
            cp = pltpu.make_async_remote_copy(src_ref=src, dst_ref=dst, send_sem=send_sems.at[k],
                                              recv_sem=recv_sems.at[k], device_id=to, device_id_type=MESH)
            cp.start()
            return cp

        cp = swap(0, p_ref, sib, (x, y, 1 - c))
        cp.wait()
        pair[...] = p_ref[...] + sib[...]
        cps = [swap(1, pair, got.at[0], (1 - x, y, c)), swap(2, pair, got.at[1], (x, 1 - y, c)),
               swap(3, pair, got.at[2], (1 - x, 1 - y, c))]
        for cp in cps:
            cp.wait()
        o_ref[...] = (pair[...] + got[1]) + (got[0] + got[2])

    return pl.pallas_call(
        body, name=name, out_shape=jax.ShapeDtypeStruct((R, W), F32),
        in_specs=[pl.BlockSpec(memory_space=pltpu.VMEM)], out_specs=pl.BlockSpec(memory_space=pltpu.VMEM),
        scratch_shapes=[pltpu.VMEM((R, W), F32), pltpu.VMEM((R, W), F32), pltpu.VMEM((3, R, W), F32),
                        pltpu.SemaphoreType.DMA((4,)), pltpu.SemaphoreType.DMA((4,))],
        compiler_params=_cparams(),
    )(pack)


BIG_WEIGHTS = [
    ("ffn_a_w_in", "col"), ("ffn_a_w_out", "row"), ("ffn_b_w_in", "col"), ("ffn_b_w_out", "row"),
    ("mla_w_in", "row"), ("mla_w_uq", "col"), ("mla_w_ukv", "col"), ("mla_w_o", "row"),
    ("conv_w_pw1", "col"), ("conv_w_pw2", "row"), ("ple_w_proj", "col"), ("ple_w_gate", "row"),
]
WEIGHT_ORDER = ["ffn_a_norm", "ffn_a_w_in", "ffn_a_w_out", "ffn_b_norm", "ffn_b_w_in", "ffn_b_w_out", "mix_norm",
                "mla_w_in", "mla_q_lat_norm", "mla_kv_lat_norm", "mla_w_uq", "mla_w_ukv", "mla_q_gain", "mla_k_gain",
                "mla_w_o", "conv_w_pw1", "conv_b_pw1", "conv_w_dw", "conv_b_dw", "conv_ln_g", "conv_ln_b", "conv_w_pw2",
                "ple_w_proj", "ple_norm", "ple_gate_norm", "ple_w_gate"]
REPLICATED_SMALL = ["ffn_a_norm", "ffn_b_norm", "mix_norm", "ple_norm", "ple_gate_norm",
                    "mla_q_lat_norm", "mla_kv_lat_norm", "mla_q_gain", "mla_k_gain"]
SHARDED_SMALL = ["conv_b_pw1", "conv_w_dw", "conv_b_dw", "conv_ln_g", "conv_ln_b"]
PACK_ROWS = 8


def _pack_rows(arrs, width):
    out = []
    for a in arrs:
        r = -(-a.shape[0] // PACK_ROWS) * PACK_ROWS
        out.append(jnp.pad(a, ((0, r - a.shape[0]), (0, width - a.shape[1]))))
    return jnp.concatenate(out, axis=0)


def _unpack_rows(pack, shapes):
    out, r0 = [], 0
    for (r, w) in shapes:
        out.append(pack[r0:r0 + r, :w])
        r0 += -(-r // PACK_ROWS) * PACK_ROWS
    return out


def kernel(x, p, positions, ffn_a_norm, ffn_a_w_in, ffn_a_w_out, ffn_b_norm, ffn_b_w_in, ffn_b_w_out, mix_norm, mla_w_in, mla_q_lat_norm, mla_kv_lat_norm, mla_w_uq, mla_w_ukv, mla_q_gain, mla_k_gain, mla_w_o, conv_w_pw1, conv_b_pw1, conv_w_dw, conv_b_dw, conv_ln_g, conv_ln_b, conv_w_pw2, ple_w_proj, ple_norm, ple_gate_norm, ple_w_gate, loss_target, m_ffn_a_norm, m_ffn_a_w_in, m_ffn_a_w_out, m_ffn_b_norm, m_ffn_b_w_in, m_ffn_b_w_out, m_mix_norm, m_mla_w_in, m_mla_q_lat_norm, m_mla_kv_lat_norm, m_mla_w_uq, m_mla_w_ukv, m_mla_q_gain, m_mla_k_gain, m_mla_w_o, m_conv_w_pw1, m_conv_b_pw1, m_conv_w_dw, m_conv_b_dw, m_conv_ln_g, m_conv_ln_b, m_conv_w_pw2, m_ple_w_proj, m_ple_norm, m_ple_gate_norm, m_ple_w_gate, v_ffn_a_norm, v_ffn_a_w_in, v_ffn_a_w_out, v_ffn_b_norm, v_ffn_b_w_in, v_ffn_b_w_out, v_mix_norm, v_mla_w_in, v_mla_q_lat_norm, v_mla_kv_lat_norm, v_mla_w_uq, v_mla_w_ukv, v_mla_q_gain, v_mla_k_gain, v_mla_w_o, v_conv_w_pw1, v_conv_b_pw1, v_conv_w_dw, v_conv_b_dw, v_conv_ln_g, v_conv_ln_b, v_conv_w_pw2, v_ple_w_proj, v_ple_norm, v_ple_gate_norm, v_ple_w_gate):
    args = dict(locals())
    W = {n: args[n] for n in WEIGHT_ORDER}
    M1 = {n: args["m_" + n] for n in WEIGHT_ORDER}
    V2 = {n: args["v_" + n] for n in WEIGHT_ORDER}

    T, D = x.shape[1], x.shape[2]
    depth = ffn_a_norm.shape[0]
    H = mla_w_ukv.shape[2] * N_CHIPS // (D_NOPE + D_V)
    QL, KL = mla_q_lat_norm.shape[1], mla_kv_lat_norm.shape[1]
    C = conv_w_pw2.shape[1] * N_CHIPS
    lat_w = QL + KL + D_ROPE
    lat_pad = QL + KL + 128

    cx, cy, cc = lax.axis_index("x"), lax.axis_index("y"), lax.axis_index("c")
    chip = (2 * cx + cy).astype(jnp.int32)
    chip_arr = chip.reshape(1)
    core_arr = cc.astype(jnp.int32).reshape(1)

    def stage_groups(i):
        mix = ([("mla_w_in", i // 2), ("mla_w_uq", i // 2), ("mla_w_ukv", i // 2), ("mla_w_o", i // 2)] if i % 2 == 0
               else [("conv_w_pw1", i // 2), ("conv_w_pw2", i // 2)])
        return [[("ffn_a_w_in", i)], [("ffn_a_w_out", i)], mix, [("ffn_b_w_in", i)], [("ffn_b_w_out", i)],
                [("ple_w_proj", i), ("ple_w_gate", i)]]

    groups = [g for i in range(depth) for g in stage_groups(i)]
    A_IN, A_OUT, MIX, B_IN, B_OUT, PLE, PER_LAYER = 0, 1, 2, 3, 4, 5, 6

    def slot_of(key, after=None):
        n, l = key
        return cast_into_slot(f"cast_{n}_{l}", W[n].reshape(-1, W[n].shape[-1]), chip_arr, l, W[n].shape[0], after)

    small_mine = _pack_rows([conv_b_pw1.reshape(1, -1), conv_w_dw[0], conv_b_dw, conv_ln_g, conv_ln_b],
                            conv_b_pw1.shape[-1])
    small_slot = lax.dynamic_update_slice(jnp.zeros((N_CHIPS,) + small_mine.shape, F32), small_mine[None],
                                          (chip, 0, 0))
    EXTRA = {PER_LAYER + MIX: [("small", 0)]} if depth > 1 else {}

    def pieces(gi, after=None):
        return [slot_of(k, after) for k in groups[gi]] + [small_slot] * (gi in EXTRA)

    FIRST = 2
    started, tok0 = gather_start("gather_start_first", [pieces(gi) for gi in range(FIRST)], chip_arr)
    small = REPLICATED_SMALL + SHARDED_SMALL
    small_packs = [_pack_rows([src[n].reshape(-1, src[n].shape[-1]) for n in small], D) for src in (W, M1, V2)]
    tok0 = tok0 + 0.0 * (small_packs[0][:8, :128] + small_packs[1][:8, :128] + small_packs[2][:8, :128])
    rest, tok_rest = gather_start("gather_start_rest", [pieces(gi, tok0) for gi in range(FIRST, len(groups))], tok0)
    started = started + rest
    relayed, G = {}, {}

    def relay(g, after):
        if g >= len(groups):
            return None
        relayed[g] = gather_relay(f"gather_relay_{g}", started[g], after)
        return relayed[g][3]

    def ready(g, after):
        if g not in relayed:
            relay(g, after)
        for key, buf in zip(groups[g] + EXTRA.get(g, []), gather_wait(f"gather_wait_{g}", relayed[g], after)):
            G[key] = buf

    relay(0, tok_rest)

    pad_gain = lambda g: jnp.pad(g, ((0, 0), (0, HEAD_PAD - QK_DIM)))
    q_gain_p, k_gain_p = pad_gain(mla_q_gain), pad_gain(mla_k_gain)
    tabs = rope_tables("rope_tables", positions.reshape(T, 1).astype(F32))

    def ffn_fwd(tag, h, norm, w_in, w_out, layer, g_in):
        hn = rms_fwd(f"{tag}_rms", h, norm)
        ready(g_in, hn)
        tok = relay(g_in + 1, hn) if g_in > 0 else None
        gu, act = ffn_in_act(f"{tag}_in", hn, G[(w_in, layer)], after=tok)
        ready(g_in + 1, act)
        tok = relay(g_in + 2, act)
        out = mm_nn(f"{tag}_out", act, G[(w_out, layer)], "row", 0, out_dtype=F32, scale=FFN_RESIDUAL_WEIGHT, res=h,
                    tm=1024, tn=512, whole_k=True, after=tok)
        return out, (h, hn, gu, act)

    saved = []
    h = x[0]
    for i in range(depth):
        L = {}
        g0 = PER_LAYER * i
        h, L["ffn_a"] = ffn_fwd(f"l{i}_ffa", h, ffn_a_norm[i:i + 1], "ffn_a_w_in", "ffn_a_w_out", i, g0 + A_IN)
        L["h1"] = h
        hn = rms_fwd(f"l{i}_mix_rms", h, mix_norm[i:i + 1])
        L["hn_m"] = hn
        ready(g0 + MIX, hn)
        j = i // 2
        if i % 2 == 0:
            w_in_pad = jnp.pad(G[("mla_w_in", j)].reshape(D, lat_w), ((0, 0), (0, lat_pad - lat_w)))[None]
            uq = G[("mla_w_uq", j)].transpose(1, 0, 2).reshape(QL, H, QK_DIM)
            w_uq_pad = jnp.pad(uq, ((0, 0), (0, 0), (0, HEAD_PAD - QK_DIM))).reshape(1, QL, H * HEAD_PAD)
            lat = mm_nn(f"l{i}_lat", hn, w_in_pad, "row", 0, out_dtype=F32, tm=2048)
            cq, ckv = lat_norm_fwd(f"l{i}_latnorm", lat, mla_q_lat_norm[j:j + 1], mla_kv_lat_norm[j:j + 1])
            q_raw = mm_nn(f"l{i}_uq", cq, w_uq_pad, "row", 0, out_dtype=F32, tm=2048)
            kv_raw = mm_nn(f"l{i}_ukv", ckv, G[("mla_w_ukv", j)], "col", 0, out_dtype=F32, tm=2048)
            Qh, Kh, Vh = mla_prep_fwd(f"l{i}_prep", q_raw, kv_raw, lat, tabs, q_gain_p, k_gain_p, QL + KL)
            O = attn_fwd(f"l{i}_attn", Qh, Kh, Vh)
            tok = relay(g0 + B_IN, O)
            h = mm_nn(f"l{i}_wo", O, G[("mla_w_o", j)], "row", 0, out_dtype=F32, res=h, tm=1024, tn=1024, after=tok)
            L["mla"] = (lat, cq, ckv, q_raw, kv_raw, Qh, Kh, Vh, O, w_in_pad, w_uq_pad)
        else:
            sm, cw = G[("small", 0)], C // N_CHIPS
            b_pw1_full = sm[:, 0, :].reshape(1, 2 * C)
            w_dw_full = sm[:, 8:8 + CONV_TAPS_PAD, :cw].transpose(1, 0, 2).reshape(CONV_TAPS_PAD, C)
            b_dw_full, ln_g_full, ln_b_full = [sm[:, r, :cw].reshape(1, C) for r in (40, 48, 56)]
            ag = mm_nn(f"l{i}_pw1", hn, G[("conv_w_pw1", j)], "col", 0, out_dtype=F32, bias=b_pw1_full, tm=2048)
            u = glu_fwd(f"l{i}_glu", ag)
            yc = dwconv_fwd(f"l{i}_dw", u, w_dw_full, b_dw_full)
            cact = ln_silu_fwd(f"l{i}_ln", yc, ln_g_full, ln_b_full)
            tok = relay(g0 + B_IN, cact)
            h = mm_nn(f"l{i}_pw2", cact, G[("conv_w_pw2", j)], "row", 0, out_dtype=F32, res=h, tm=1024, tn=1024,
                      after=tok)
            L["conv"] = (ag, u, yc, cact)
        L["h2"] = h
        h, L["ffn_b"] = ffn_fwd(f"l{i}_ffb", h, ffn_b_norm[i:i + 1], "ffn_b_w_in", "ffn_b_w_out", i, g0 + B_IN)
        L["h3"] = h
        ready(g0 + PLE, h)
        pe = mm_nn(f"l{i}_ple_proj", p[i, 0], G[("ple_w_proj", i)], "col", 0, out_dtype=F32, tm=2048)
        hg = rms_fwd(f"l{i}_gate_rms", h, ple_gate_norm[i:i + 1])
        tok = relay(g0 + PER_LAYER, hg)
        z = mm_nn(f"l{i}_ple_gate", hg, G[("ple_w_gate", i)], "row", 0, out_dtype=F32, tm=2048, after=tok)
        h = ple_fwd(f"l{i}_ple", h, pe, z, ple_norm[i:i + 1])
        L["ple"] = (pe, hg, z)
        saved.append(L)

    d_h, loss_part = loss_head("loss_head", h, loss_target[0])
    loss = lax.psum(loss_part[0, 0], ("x", "y", "c"))

    GW = {}
    SG = {}
    ids_arr = jnp.stack([cc.astype(jnp.int32), chip])
    two = lambda a: a.reshape(-1, a.shape[-1])
    merged = {}
    pipe = {}
    swapped = {}
    order = list(reversed(range(len(groups))))
    ticks = [0]

    def put_small(name, i, val):
        SG.setdefault(name, {})[i] = val

    deferred = []

    def sibling_half(g, after):
        _, theirs = split_wait(f"share_wait_{g}", pipe[g], share_grad_plan, after)
        for (n, l), gr in zip(groups[g], theirs):
            merged[n] = adamw_half(f"adamw_sib_{n}_{l}", ids_arr, gr, two(W[n]), two(M1[n]), two(V2[n]), l,
                                   W[n].shape[0], merged[n])
        return merged[groups[g][-1][0]][0]

    def reduce_tick(after, defer=True):
        k, tok = ticks[0], after
        ticks[0] += 1
        grp = lambda j: order[j] if 0 <= j < len(order) else None
        g = grp(k - 4)
        if g is not None:
            if defer:
                deferred.append(g)
            else:
                sibling_half(g, tok)
        starts = []
        g = grp(k - 3)
        if g is not None:
            _, landed = split_wait(f"exchange_wait_{g}", pipe[g], chip_exchange_plan, tok)
            halves = []
            for (n, l), (q, r), ld in zip(groups[g], swapped[g], landed):
                *merged[n], mine = adamw_half(f"adamw_own_{n}_{l}", ids_arr, (q, r, ld), two(W[n]), two(M1[n]),
                                              two(V2[n]), l, W[n].shape[0], merged.get(n))
                halves.append(mine)
            starts.append((g, (halves, [(h.shape, h.dtype) for h in halves], len(halves), share_grad_plan)))
        g = grp(k - 1)
        if g is not None:
            parts, got = split_wait(f"swap_wait_{g}", pipe[g], swap_halves_plan, tok)
            swapped[g] = list(zip(parts, got))
            sums = [add_halves(f"pair_sum_{n}_{l}", ids_arr, q, r) for (n, l), q, r in zip(groups[g], parts, got)]
            lands = [((N_CHIPS,) + s.shape[1:], s.dtype) for s in sums]
            starts.append((g, (sums, lands, 3 * len(sums), chip_exchange_plan)))
        g = grp(k)
        if g is not None:
            parts = [GW[key] for key in groups[g]]
            lands = [((q.shape[0], q.shape[1] // 2, q.shape[2]), q.dtype) for q in parts]
            starts.append((g, (parts, lands, len(parts), swap_halves_plan)))
        if starts:
            behind = ids_arr if grp(k) is not None else tok
            for (g, _), state in zip(starts, split_start(f"reduce_start_{k}", [p for _, p in starts], behind)):
                pipe[g] = state
            tok = state[4]
        return tok

    def ffn_bwd(tag, d_h, d_hb, norm, w_in, w_out, layer, fw, tok):
        h_in, hn, gu, act = fw
        GW[(w_out, layer)] = mm_tn(f"{tag}_dwout", act, d_hb, "row", scale=FFN_RESIDUAL_WEIGHT, tk=1408, tn=1024,
                                   after=tok)
        tok = reduce_tick(GW[(w_out, layer)])
        dgu = ffn_dact_dgu(f"{tag}_dact", d_hb, G[(w_out, layer)], gu, FFN_RESIDUAL_WEIGHT, after=tok)
        GW[(w_in, layer)] = mm_tn(f"{tag}_dwin", hn, dgu, "col")
        tok = reduce_tick(GW[(w_in, layer)])
        d_hn = _mm_nt(f"{tag}_dhn", dgu, G[(w_in, layer)], "col", 0, D, out_dtype=F32, tc=2816, after=tok)
        return (*rms_bwd_res(f"{tag}_drms", h_in, d_hn, norm, d_h), tok)

    d_hb, tok = None, None
    for i in reversed(range(depth)):
        L = saved[i]
        j = i // 2
        pe, hg, z = L["ple"]
        d_z, d_pe, g = ple_bwd(f"l{i}_dple", d_h, pe, z, ple_norm[i:i + 1])
        put_small("ple_norm", i, g)
        d_hg = _mm_nt(f"l{i}_dhg", d_z, G[("ple_w_gate", i)], "row", 0, D, out_dtype=F32, to=512, tc=2048, after=tok)
        GW[("ple_w_gate", i)] = mm_tn(f"l{i}_dwgate", hg, d_z, "row", tk=512, tn=1024)
        GW[("ple_w_proj", i)] = mm_tn(f"l{i}_dwproj", p[i, 0], d_pe, "col")
        tok = reduce_tick(GW[("ple_w_proj", i)])
        d_h, d_hb, g = rms_bwd_res(f"l{i}_dgate_rms", L["h3"], d_hg, ple_gate_norm[i:i + 1], d_h)
        put_small("ple_gate_norm", i, g)

        d_h, d_hb, g, tok = ffn_bwd(f"l{i}_ffb", d_h, d_hb, ffn_b_norm[i:i + 1], "ffn_b_w_in", "ffn_b_w_out", i,
                                    L["ffn_b"], tok)
        put_small("ffn_b_norm", i, g)

        hn = L["hn_m"]
        if i % 2 == 0:
            lat, cq, ckv, q_raw, kv_raw, Qh, Kh, Vh, O, w_in_pad, w_uq_pad = L["mla"]
            d_O = _mm_nt(f"l{i}_dO", d_hb, G[("mla_w_o", j)], "row", 0, H * D_V, out_dtype=BF16, to=512, tc=2048,
                         after=tok)
            GW[("mla_w_o", j)] = mm_tn(f"l{i}_dwo", O, d_hb, "row", tk=512, tn=1024)
            dQ, dK, dV = attn_bwd(f"l{i}_dattn", Qh, Kh, Vh, d_O)
            d_q_raw, d_kv_raw, d_kr, gq, gk = mla_prep_bwd(f"l{i}_dprep", dQ, dK, dV, q_raw, kv_raw, lat, tabs,
                                                           q_gain_p, k_gain_p, QL + KL)
            put_small("mla_q_gain", j, gq[:, :QK_DIM])
            put_small("mla_k_gain", j, gk[:, :QK_DIM])
            d_cq = _mm_nt(f"l{i}_dcq", d_q_raw, w_uq_pad, "row", 0, QL, out_dtype=F32, to=512, tc=2048)
            g_uq = mm_tn(f"l{i}_dwuq", cq, d_q_raw, "row", shards=1, out_dtype=F32, tk=512, tn=1024)
            g_uq = g_uq.reshape(QL, H, HEAD_PAD)[:, :, :QK_DIM].reshape(QL, N_CHIPS, -1).transpose(1, 0, 2)
            GW[("mla_w_uq", j)] = g_uq.astype(BF16)
            d_ckv = _mm_nt(f"l{i}_dckv", d_kv_raw, G[("mla_w_ukv", j)], "col", 0, KL, out_dtype=F32, to=512, tc=1024)
            GW[("mla_w_ukv", j)] = mm_tn(f"l{i}_dwukv", ckv, d_kv_raw, "col", tk=512, tn=1024)
            d_lat, gq, gk = lat_norm_bwd(f"l{i}_dlatnorm", lat, d_cq, d_ckv, d_kr, mla_q_lat_norm[j:j + 1],
                                         mla_kv_lat_norm[j:j + 1])
            put_small("mla_q_lat_norm", j, gq)
            put_small("mla_kv_lat_norm", j, gk)
            d_hn = _mm_nt(f"l{i}_dhn_lat", d_lat, w_in_pad, "row", 0, D, out_dtype=F32, to=1024, tc=lat_pad)
            g_in = mm_tn(f"l{i}_dwin_lat", hn, d_lat, "row", shards=1, out_dtype=F32, tk=1024, tn=lat_pad)
            GW[("mla_w_in", j)] = g_in[0, :, :lat_w].reshape(N_CHIPS, D // N_CHIPS, lat_w).astype(BF16)
            tok = reduce_tick(GW[("mla_w_in", j)])
        else:
            ag, u, yc, cact = L["conv"]
            d_cact = _mm_nt(f"l{i}_dcact", d_hb, G[("conv_w_pw2", j)], "row", 0, C, out_dtype=F32, to=512, tc=2048,
                            after=tok)
            GW[("conv_w_pw2", j)] = mm_tn(f"l{i}_dwpw2", cact, d_hb, "row", tk=512, tn=1024)
            d_yc, g1, g2, g3 = ln_silu_bwd(f"l{i}_dln", yc, d_cact, ln_g_full, ln_b_full)
            put_small("conv_ln_g", j, g1)
            put_small("conv_ln_b", j, g2)
            put_small("conv_b_dw", j, g3)
            d_u = dwconv_bwd_u(f"l{i}_ddw_u", d_yc, w_dw_full)
            put_small("conv_w_dw", j, dwconv_bwd_w(f"l{i}_ddw_w", u, d_yc))
            d_ag, g = glu_bwd(f"l{i}_dglu", ag, d_u)
            put_small("conv_b_pw1", j, g)
            d_hn = _mm_nt(f"l{i}_dhn_pw1", d_ag, G[("conv_w_pw1", j)], "col", 0, D, out_dtype=F32, tc=1024)
            GW[("conv_w_pw1", j)] = mm_tn(f"l{i}_dwpw1", hn, d_ag, "col", tn=1024)
            tok = reduce_tick(GW[("conv_w_pw1", j)])
        d_h, d_hb, g = rms_bwd_res(f"l{i}_dmix_rms", L["h1"], d_hn, mix_norm[i:i + 1], d_h)
        put_small("mix_norm", i, g)

        d_h, d_hb, g, tok = ffn_bwd(f"l{i}_ffa", d_h, d_hb, ffn_a_norm[i:i + 1], "ffn_a_w_in", "ffn_a_w_out", i,
                                    L["ffn_a"], tok)
        put_small("ffn_a_norm", i, g)
    grad_x = d_h[None]

    tok = reduce_tick(reduce_tick(d_h))
    for g in deferred:
        tok = sibling_half(g, tok)
    tok = reduce_tick(reduce_tick(tok, defer=False), defer=False)
    names = [n for n, _ in BIG_WEIGHTS]
    grads, delta, new_m, new_v = {}, {}, {}, {}
    for n in names:
        grads[n], delta[n], new_m[n], new_v[n] = [a.reshape(W[n].shape) for a in merged[n]]

    rep = []
    for n in REPLICATED_SMALL:
        rep.append(jnp.concatenate([SG[n][i] for i in sorted(SG[n])], axis=0))
    shd = [SG["conv_b_pw1"][0].reshape(2, C), SG["conv_w_dw"][0][:CONV_WIDTH], SG["conv_b_dw"][0],
           SG["conv_ln_g"][0], SG["conv_ln_b"][0]]
    red = allreduce_pack("allreduce_small", _pack_rows(rep + shd, D))
    red = _unpack_rows(red, [a.shape for a in rep + shd])
    for n, g in zip(REPLICATED_SMALL, red):
        grads[n] = g
    own = lambda a, w: lax.dynamic_slice_in_dim(a, chip * w, w, axis=a.ndim - 1)
    sh = red[len(rep):]
    grads["conv_b_pw1"] = own(sh[0].reshape(1, 2 * C), 2 * C // N_CHIPS)
    grads["conv_w_dw"] = own(sh[1], C // N_CHIPS)[None]
    grads["conv_b_dw"] = own(sh[2], C // N_CHIPS)
    grads["conv_ln_g"] = own(sh[3], C // N_CHIPS)
    grads["conv_ln_b"] = own(sh[4], C // N_CHIPS)

    shapes = [two(W[n]).shape for n in small]
    grad_pack = _pack_rows([two(grads[n]) for n in small], D)
    outs = adamw("adamw_small", small_packs[0], grad_pack, small_packs[1], small_packs[2])
    for dst, pk in zip((delta, new_m, new_v), outs):
        for n, a in zip(small, _unpack_rows(pk, shapes)):
            dst[n] = a.reshape(W[n].shape)
    for n in small:
        grads[n] = grads[n].reshape(W[n].shape)

    return (loss, grad_x, *[grads[n] for n in WEIGHT_ORDER], *[delta[n] for n in WEIGHT_ORDER],
            *[new_m[n] for n in WEIGHT_ORDER], *[new_v[n] for n in WEIGHT_ORDER])
```

```python
import functools
import math

import jax
import jax.numpy as jnp
from jax import lax
from jax.experimental import pallas as pl
from jax.experimental.pallas import tpu as pltpu

F32, BF16 = jnp.float32, jnp.bfloat16
MESH = pl.DeviceIdType.MESH

N_CHIPS = 4
EPS = 1e-6
D_NOPE, D_ROPE, D_V = 128, 64, 128
QK_DIM = D_NOPE + D_ROPE
HEAD_PAD = 256
ROPE_THETA = 10000.0
CONV_WIDTH = 31
CONV_TAPS_PAD = 32
FFN_RESIDUAL_WEIGHT = 0.5
ADAM_LR, ADAM_B1, ADAM_B2, ADAM_EPS, ADAM_WD, ADAM_STEP = 0.001, 0.9, 0.999, 1e-08, 0.01, 10
VMEM_LIMIT_BYTES = 56 * 1024 * 1024
NEG_BIG = -1e30

NN_DIMS = (((1,), (0,)), ((), ()))
NT_DIMS = (((1,), (1,)), ((), ()))
TN_DIMS = (((0,), (0,)), ((), ()))


def _cparams(semantics=None):
    kw = dict(vmem_limit_bytes=VMEM_LIMIT_BYTES)
    if semantics is not None:
        kw["dimension_semantics"] = semantics
    return pltpu.CompilerParams(**kw)


def _tile(n, pref, mult=128):
    if n <= pref:
        return n
    t = (pref // mult) * mult
    while t >= mult:
        if n % t == 0:
            return t
        t -= mult
    return n


def _rowwise(name, fn, rows, vecs, outs, accs=(), tm=256, rc=32):
    T = rows[0].shape[0]
    tm = min(tm, T)
    rc = min(rc, tm)
    nr, nv, no, na = len(rows), len(vecs), len(outs), len(accs)
    steps = tm // rc

    def body(*refs):
        row_refs = refs[:nr]
        vec_refs = refs[nr:nr + nv]
        out_refs = refs[nr + nv:nr + nv + no]
        acc_refs = refs[nr + nv + no:]
        if na:
            @pl.when(pl.program_id(0) == 0)
            def _():
                for a in acc_refs:
                    a[...] = jnp.zeros_like(a)

        def step(r, carry):
            sl = pl.ds(pl.multiple_of(r * rc, rc), rc)
            res = fn(*[x[sl, :] for x in row_refs], *[v[...] for v in vec_refs])
            for o, val in zip(out_refs, res[:no]):
                o[sl, :] = val.astype(o.dtype)
            return tuple(c + val for c, val in zip(carry, res[no:]))

        init = tuple(jnp.zeros(s, F32) for s in accs)
        tot = lax.fori_loop(0, steps, step, init)
        for a, val in zip(acc_refs, tot):
            a[...] += val

    in_specs = [pl.BlockSpec((tm, x.shape[1]), lambda i: (i, 0)) for x in rows]
    in_specs += [pl.BlockSpec(v.shape, lambda i: (0, 0)) for v in vecs]
    out_specs = [pl.BlockSpec((tm, d), lambda i: (i, 0)) for d, _ in outs]
    out_specs += [pl.BlockSpec(s, lambda i: (0, 0)) for s in accs]
    out_shape = [jax.ShapeDtypeStruct((T, d), dt) for d, dt in outs]
    out_shape += [jax.ShapeDtypeStruct(s, F32) for s in accs]
    return pl.pallas_call(
        body, name=name, grid=(T // tm,), in_specs=in_specs, out_specs=out_specs, out_shape=out_shape,
        compiler_params=_cparams(("arbitrary",)),
    )(*rows, *vecs)


def _colsum(v):
    return jnp.sum(v, axis=0, keepdims=True)


def _rstd(x):
    return lax.rsqrt(jnp.mean(x * x, axis=-1, keepdims=True) + EPS)


def _rms_bwd(x, dy, g):
    r = _rstd(x)
    xh = x * r
    dyg = dy * g
    dx = r * (dyg - xh * jnp.mean(dyg * xh, axis=-1, keepdims=True))
    return dx, dy * xh


def _sigmoid(x):
    return 1.0 / (1.0 + jnp.exp(-x))


def rms_fwd(name, h, g):
    def fn(x, gv):
        return ((x * _rstd(x)) * gv,)
    return _rowwise(name, fn, [h], [g], [(h.shape[1], BF16)])[0]


def rms_bwd_res(name, h, d_y, g, d_res):
    D = h.shape[1]

    def fn(x, dy, dr, gv):
        dx, dgr = _rms_bwd(x, dy, gv)
        dh = dr + dx
        return dh, dh, _colsum(dgr)
    return _rowwise(name, fn, [h, d_y, d_res], [g], [(D, F32), (D, BF16)], [(1, D)], tm=128)


def loss_head(name, y, target):
    D = y.shape[1]

    def fn(yv, tv):
        e = yv - tv
        tot = jnp.sum(_colsum(e * e), axis=1, keepdims=True) * (0.5 / D)
        return e * (1.0 / D), jnp.broadcast_to(tot, (1, 128))
    return _rowwise(name, fn, [y, target], [], [(D, F32)], [(1, 128)])


def ple_fwd(name, h, pe, z, g_e):
    def fn(hv, pv, zv, gv):
        return (hv + (pv * _rstd(pv)) * gv * _sigmoid(zv),)
    return _rowwise(name, fn, [h, pe, z], [g_e], [(h.shape[1], F32)], tm=128)[0]


def ple_bwd(name, d_h, pe, z, g_e):
    D = d_h.shape[1]

    def fn(dh, pv, zv, gv):
        gate = _sigmoid(zv)
        e = (pv * _rstd(pv)) * gv
        d_z = dh * e * gate * (1.0 - gate)
        d_pe, dgr = _rms_bwd(pv, dh * gate, gv)
        return d_z, d_pe, _colsum(dgr)
    return _rowwise(name, fn, [d_h, pe, z], [g_e], [(D, BF16), (D, BF16)], [(1, D)], tm=128)


def lat_norm_fwd(name, lat, g_q, g_kv):
    QL, KL = g_q.shape[1], g_kv.shape[1]

    def fn(v, gq, gk):
        a = v[:, :QL]
        b = v[:, QL:QL + KL]
        return (a * _rstd(a)) * gq, (b * _rstd(b)) * gk
    return _rowwise(name, fn, [lat], [g_q, g_kv], [(QL, BF16), (KL, BF16)])


def lat_norm_bwd(name, lat, d_cq, d_ckv, d_krope, g_q, g_kv):
    QL, KL = g_q.shape[1], g_kv.shape[1]

    def fn(v, dq, dk, dr, gq, gk):
        da, ga = _rms_bwd(v[:, :QL], dq, gq)
        db, gb = _rms_bwd(v[:, QL:QL + KL], dk, gk)
        return jnp.concatenate([da, db, dr], axis=-1), _colsum(ga), _colsum(gb)
    return _rowwise(name, fn, [lat, d_cq, d_ckv, d_krope], [g_q, g_kv],
                    [(lat.shape[1], BF16)], [(1, QL), (1, KL)])


def glu_fwd(name, ag):
    C = ag.shape[1] // 2

    def fn(v):
        return (v[:, :C] * _sigmoid(v[:, C:]),)
    return _rowwise(name, fn, [ag], [], [(C, F32)], tm=128)[0]


def glu_bwd(name, ag, d_u):
    C = ag.shape[1] // 2

    def fn(v, du):
        a = v[:, :C]
        s = _sigmoid(v[:, C:])
        d = jnp.concatenate([du * s, du * a * s * (1.0 - s)], axis=-1)
        return d, _colsum(d)
    return _rowwise(name, fn, [ag, d_u], [], [(2 * C, BF16)], [(1, 2 * C)], tm=128)


def ln_silu_fwd(name, yc, g, b):
    def fn(v, gv, bv):
        xc = v - jnp.mean(v, axis=-1, keepdims=True)
        ln = xc * lax.rsqrt(jnp.mean(xc * xc, axis=-1, keepdims=True) + EPS) * gv + bv
        return (ln * _sigmoid(ln),)
    return _rowwise(name, fn, [yc], [g, b], [(yc.shape[1], BF16)], tm=128)[0]


def ln_silu_bwd(name, yc, d_out, g, b):
    C = yc.shape[1]

    def fn(v, do, gv, bv):
        xc = v - jnp.mean(v, axis=-1, keepdims=True)
        r = lax.rsqrt(jnp.mean(xc * xc, axis=-1, keepdims=True) + EPS)
        xh = xc * r
        ln = xh * gv + bv
        s = _sigmoid(ln)
        d_ln = do * s * (1.0 + ln * (1.0 - s))
        dxh = d_ln * gv
        dy = r * (dxh - jnp.mean(dxh, axis=-1, keepdims=True) - xh * jnp.mean(dxh * xh, axis=-1, keepdims=True))
        return dy, _colsum(d_ln * xh), _colsum(d_ln), _colsum(dy)
    return _rowwise(name, fn, [yc, d_out], [g, b], [(C, F32)], [(1, C), (1, C), (1, C)], tm=128)


def add_halves(name, ids, own, got):
    S, Rh, C = got.shape
    tr = _tile(Rh, 256, 16)
    nrb = Rh // tr
    n = (S - 1) * nrb
    N_BUF = 3

    def body(ids_ref, own_ref, got_ref, out_ref, a_buf, b_buf, o_buf, a_sem, b_sem, o_sem):
        core, chip = ids_ref[0], ids_ref[1]

        def copies(i, slot):
            s, r = i // nrb, i % nrb
            shard = chip ^ (s + 1)
            mine = pl.ds(pl.multiple_of((core * nrb + r) * tr, 16), tr)
            rows = pl.ds(pl.multiple_of(r * tr, 16), tr)
            return (pltpu.make_async_copy(own_ref.at[shard, mine, :], a_buf.at[slot], a_sem.at[slot]),
                    pltpu.make_async_copy(got_ref.at[shard, rows, :], b_buf.at[slot], b_sem.at[slot]),
                    pltpu.make_async_copy(o_buf.at[slot], out_ref.at[s, rows, :], o_sem.at[slot]))

        for i in range(min(N_BUF, n)):
            rd_a, rd_b, _ = copies(i, i)
            rd_a.start()
            rd_b.start()

        def step(i, carry):
            slot = i % N_BUF
            rd_a, rd_b, wr = copies(i, slot)
            rd_a.wait()
            rd_b.wait()

            @pl.when(i >= N_BUF)
            def _():
                copies(i - N_BUF, slot)[2].wait()

            o_buf[slot] = (a_buf[slot].astype(F32) + b_buf[slot].astype(F32)).astype(BF16)
            wr.start()

            @pl.when(i + N_BUF < n)
            def _():
                nx_a, nx_b, _ = copies(i + N_BUF, slot)
                nx_a.start()
                nx_b.start()
            return carry

        lax.fori_loop(0, n, step, 0)
        for i in range(max(n - N_BUF, 0), n):
            copies(i, i % N_BUF)[2].wait()

    buf = pltpu.VMEM((N_BUF, tr, C), BF16)
    sem = pltpu.SemaphoreType.DMA((N_BUF,))
    return pl.pallas_call(
        body, name=name, out_shape=jax.ShapeDtypeStruct((S - 1, Rh, C), BF16),
        in_specs=[pl.BlockSpec(memory_space=pltpu.SMEM), pl.BlockSpec(memory_space=pl.ANY),
                  pl.BlockSpec(memory_space=pl.ANY)],
        out_specs=pl.BlockSpec(memory_space=pl.ANY),
        scratch_shapes=[buf, buf, buf, sem, sem, sem], compiler_params=_cparams())(ids, own, got)


def adamw_half(name, ids, grad, w, m, v, layer, layers, into):
    own = isinstance(grad, tuple)
    Rh, C = grad[1].shape[1:] if own else grad.shape
    tr = _tile(Rh, 128, 16)
    nrb = Rh // tr
    n_grad = 5 if own else 1
    c1 = 1.0 / (1.0 - ADAM_B1 ** ADAM_STEP)
    c2 = 1.0 / (1.0 - ADAM_B2 ** ADAM_STEP)

    def body(ids_ref, *refs):
        g_refs, (w_ref, m_ref, v_ref) = refs[:n_grad], refs[n_grad:n_grad + 3]
        outs = refs[-5:] if own else refs[-4:]
        go_ref, d_ref, nm_ref, nv_ref = outs[:4]
        if own:
            f = [g[...].astype(F32) for g in g_refs]
            gv = ((f[0] + f[1]) + f[2]) + (f[3] + f[4])
            outs[4][...] = gv
        else:
            gv = g_refs[0][...]
        nm = ADAM_B1 * m_ref[...] + (1.0 - ADAM_B1) * gv
        nv = ADAM_B2 * v_ref[...] + (1.0 - ADAM_B2) * (gv * gv)
        go_ref[...] = gv
        d_ref[...] = -ADAM_LR * ((nm * c1) / (jnp.sqrt(nv * c2) + ADAM_EPS) + ADAM_WD * w_ref[...])
        nm_ref[...] = nm
        nv_ref[...] = nv

    half = (lambda ids: ids[0]) if own else (lambda ids: 1 - ids[0])
    rows = pl.BlockSpec((tr, C), lambda r, ids: ((2 * layer + half(ids)) * nrb + r, 0))
    plain = pl.BlockSpec((tr, C), lambda r, ids: (r, 0))
    if own:
        slot = lambda flip: pl.BlockSpec((None, tr, C), lambda r, ids: (ids[1] ^ flip, r, 0))
        mine = pl.BlockSpec((None, tr, C), lambda r, ids: (ids[1], ids[0] * nrb + r, 0))
        in_specs = [mine, slot(0), slot(1), slot(2), slot(3), rows, rows, rows]
        operands = [ids, grad[0], grad[1], grad[2], grad[2], grad[2], w, m, v]
    else:
        in_specs = [plain, rows, rows, rows]
        operands = [ids, grad, w, m, v]
    aliases = {}
    if into is not None:
        in_specs += [pl.BlockSpec(memory_space=pl.ANY)] * 4
        aliases = {len(operands) + k: k for k in range(4)}
        operands += list(into)
    full = jax.ShapeDtypeStruct((layers * 2 * Rh, C), F32)
    gs = pltpu.PrefetchScalarGridSpec(num_scalar_prefetch=1, grid=(nrb,), in_specs=in_specs,
                                      out_specs=[rows] * 4 + [plain] * own)
    return pl.pallas_call(
        body, name=name, grid_spec=gs, out_shape=[full] * 4 + [jax.ShapeDtypeStruct((Rh, C), F32)] * own,
        input_output_aliases=aliases, compiler_params=_cparams(("arbitrary",)))(*operands)


def cast_into_slot(name, w, chip, layer, layers, after=None):
    R, C = w.shape[0] // layers, w.shape[1]
    tr = _tile(R, 512, 16)
    nrb = R // tr

    def body(s_ref, w_ref, *rest):
        rest[-1][...] = w_ref[...].astype(BF16)

    in_specs = [pl.BlockSpec((tr, C), lambda r, s: (layer * nrb + r, 0))]
    operands = [chip, w]
    if after is not None:
        in_specs.append(pl.BlockSpec(memory_space=pl.ANY))
        operands.append(after)
    gs = pltpu.PrefetchScalarGridSpec(
        num_scalar_prefetch=1, grid=(nrb,), in_specs=in_specs,
        out_specs=pl.BlockSpec((None, tr, C), lambda r, s: (s[0], r, 0)))
    return pl.pallas_call(
        body, name=name, grid_spec=gs, out_shape=jax.ShapeDtypeStruct((N_CHIPS, R, C), BF16),
        compiler_params=_cparams(("arbitrary",)))(*operands)


def adamw(name, w, g, m, v):
    R, C = w.shape
    tr = _tile(R, 256, 8)
    c1 = 1.0 / (1.0 - ADAM_B1 ** ADAM_STEP)
    c2 = 1.0 / (1.0 - ADAM_B2 ** ADAM_STEP)

    def body(w_ref, g_ref, m_ref, v_ref, d_ref, nm_ref, nv_ref):
        gv = g_ref[...]
        nm = ADAM_B1 * m_ref[...] + (1.0 - ADAM_B1) * gv
        nv = ADAM_B2 * v_ref[...] + (1.0 - ADAM_B2) * (gv * gv)
        d_ref[...] = -ADAM_LR * ((nm * c1) / (jnp.sqrt(nv * c2) + ADAM_EPS) + ADAM_WD * w_ref[...])
        nm_ref[...] = nm
        nv_ref[...] = nv

    spec = pl.BlockSpec((tr, C), lambda r: (r, 0))
    return pl.pallas_call(
        body, name=name, grid=(R // tr,), in_specs=[spec] * 4, out_specs=[spec] * 3,
        out_shape=[jax.ShapeDtypeStruct((R, C), F32)] * 3, compiler_params=_cparams(("arbitrary",)))(w, g, m, v)


def _matmul(name, a, b, *, grid, a_blk, a_map, b_blk, b_map, o_shape, o_dtype, o_blk, o_map, dims,
            scale=None, res=None, bias=None, bias_blk=None, bias_map=None, alias_into=None, after=None):
    nk = grid[2]
    has_res, has_bias, has_into = res is not None, bias is not None, alias_into is not None
    acc_shape = tuple(d for d in o_blk if d is not None)

    def body(*refs):
        a_ref, b_ref = refs[0], refs[1]
        pos = 2
        res_ref = bias_ref = None
        if has_res:
            res_ref = refs[pos]
            pos += 1
        if has_bias:
            bias_ref = refs[pos]
            pos += 1
        if has_into:
            pos += 1
        if after is not None:
            pos += 1
        o_ref = refs[pos]
        av, bv = a_ref[...], b_ref[...]
        if bv.ndim == 3:
            bv = bv.reshape(-1, bv.shape[-1])
        if av.dtype != BF16:
            av = av.astype(BF16)
        if bv.dtype != BF16:
            bv = bv.astype(BF16)
        part = lax.dot_general(av, bv, dims, preferred_element_type=F32)

        def finish(acc):
            if scale is not None:
                acc = acc * scale
            if has_bias:
                acc = acc + bias_ref[...]
            if has_res:
                acc = acc + res_ref[...]
            o_ref[...] = acc.astype(o_ref.dtype)

        if nk == 1:
            finish(part)
        else:
            acc_ref = refs[pos + 1]
            k = pl.program_id(2)

            @pl.when(k == 0)
            def _():
                acc_ref[...] = part

            @pl.when(k > 0)
            def _():
                acc_ref[...] += part

            @pl.when(k == nk - 1)
            def _():
                finish(acc_ref[...])

    operands = [a, b]
    in_specs = [pl.BlockSpec(a_blk, a_map), pl.BlockSpec(b_blk, b_map)]
    if has_res:
        operands.append(res)
        in_specs.append(pl.BlockSpec(o_blk, o_map))
    if has_bias:
        operands.append(bias)
        in_specs.append(pl.BlockSpec(bias_blk, bias_map))
    aliases = {}
    if has_into:
        aliases = {len(operands): 0}
        operands.append(alias_into)
        in_specs.append(pl.BlockSpec(memory_space=pl.ANY))
    if after is not None:
        operands.append(after)
        in_specs.append(pl.BlockSpec(memory_space=pl.ANY))
    return pl.pallas_call(
        body, name=name, grid=grid, in_specs=in_specs, out_specs=pl.BlockSpec(o_blk, o_map),
        out_shape=jax.ShapeDtypeStruct(o_shape, o_dtype),
        scratch_shapes=[pltpu.VMEM(acc_shape, F32)] if nk > 1 else [],
        input_output_aliases=aliases,
        compiler_params=_cparams(("parallel", "parallel", "arbitrary")),
    )(*operands)


def mm_nn(name, a, w3, kind, layer, *, out_dtype, scale=None, res=None, bias=None, tm=1024, tn=512, tk=2048,
          after=None, whole_k=False):
    M, K = a.shape
    S, _, C = w3.shape
    tm = _tile(M, tm, 16)
    b_blk = None
    if kind == "col":
        N = S * C
        tk, tn = _tile(K, tk), _tile(C, tn)
        kb, nb = K // tk, C // tn
        b_map = lambda n, m, k: (n // nb, layer * kb + k, n % nb)
    elif whole_k:
        N, K4, tk, tn = C, K // S, K, _tile(C, tn)
        b_blk, b_map = (S, K4, tn), lambda n, m, k: (0, layer, n)
    else:
        N, K4 = C, K // S
        tk, tn = _tile(K4, tk), _tile(C, tn)
        kb4 = K4 // tk
        b_map = lambda n, m, k: (k // kb4, layer * kb4 + k % kb4, n)
    return _matmul(
        name, a, w3, grid=(N // tn, M // tm, K // tk),
        a_blk=(tm, tk), a_map=lambda n, m, k: (m, k), b_blk=b_blk or (None, tk, tn), b_map=b_map,
        o_shape=(M, N), o_dtype=out_dtype, o_blk=(tm, tn), o_map=lambda n, m, k: (m, n), dims=NN_DIMS,
        scale=scale, res=res, bias=bias, bias_blk=(1, tn), bias_map=lambda n, m, k: (0, n), after=after)


def _mm_nt(name, g, w3, kind, layer, K, *, out_dtype, scale=None, tm=1024, to=1024, tc=1408, after=None):
    S, _, C = w3.shape
    tc = _tile(C, tc)
    if g.ndim == 3:
        M, N = g.shape[1], 2 * g.shape[2]
        tm = _tile(M, tm, 16)
        cb = g.shape[2] // tc
        a_blk, a_map = (None, tm, tc), lambda o, m, c: (c // cb, m, c % cb)
    else:
        M, N = g.shape
        tm = _tile(M, tm, 16)
        a_blk, a_map = (tm, tc), lambda o, m, c: (m, c)
    if kind == "col":
        nb = C // tc
        to = _tile(K, to)
        ob = K // to
        b_map = lambda o, m, c: (c // nb, layer * ob + o, c % nb)
    else:
        K4 = K // S
        to = _tile(K4, to)
        ob4 = K4 // to
        b_map = lambda o, m, c: (o // ob4, layer * ob4 + o % ob4, c)
    return _matmul(
        name, g, w3, grid=(K // to, M // tm, N // tc),
        a_blk=a_blk, a_map=a_map, b_blk=(None, to, tc), b_map=b_map,
        o_shape=(M, K), o_dtype=out_dtype, o_blk=(tm, to), o_map=lambda o, m, c: (m, o), dims=NT_DIMS,
        scale=scale, after=after)


def mm_tn(name, a, g, kind, *, shards=N_CHIPS, layer=0, layers=1, into=None, out_dtype=BF16, scale=None,
          tk=1024, tn=1408, tm=2048, after=None):
    M, K = a.shape
    N = 2 * g.shape[2] if g.ndim == 3 else g.shape[1]
    tm = _tile(M, tm, 16)
    if kind == "col":
        C = N // shards
        tk, tn = _tile(K, tk), _tile(C, tn)
        kb, nb = K // tk, C // tn
        o_shape = (shards, layers * K, C)
        o_map = lambda k, n, m: (n // nb, layer * kb + k, n % nb)
    else:
        K4 = K // shards
        tk, tn = _tile(K4, tk), _tile(N, tn)
        kb4 = K4 // tk
        o_shape = (shards, layers * K4, N)
        o_map = lambda k, n, m: (k // kb4, layer * kb4 + k % kb4, n)
    if g.ndim == 3:
        nbh = g.shape[2] // tn
        g_blk, g_map = (None, tm, tn), lambda k, n, m: (n // nbh, m, n % nbh)
    else:
        g_blk, g_map = (tm, tn), lambda k, n, m: (m, n)
    return _matmul(
        name, a, g, grid=(K // tk, N // tn, M // tm),
        a_blk=(tm, tk), a_map=lambda k, n, m: (m, k), b_blk=g_blk, b_map=g_map,
        o_shape=o_shape, o_dtype=out_dtype, o_blk=(None, tk, tn), o_map=o_map, dims=TN_DIMS,
        scale=scale, alias_into=into, after=after)


def ffn_in_act(name, hn, w3, *, tn=256, after=None):
    M, K = hn.shape
    S, _, C = w3.shape
    tn = _tile(C, tn)
    nb = C // tn

    def body(a_ref, bg_ref, bu_ref, *rest):
        gu_ref, act_ref = rest[-2:]
        a = a_ref[...]
        g = jnp.dot(a, bg_ref[...], preferred_element_type=F32)
        u = jnp.dot(a, bu_ref[...], preferred_element_type=F32)
        gu_ref[0] = g.astype(BF16)
        gu_ref[1] = u.astype(BF16)
        act_ref[...] = (g * _sigmoid(g) * u).astype(BF16)

    in_specs = [pl.BlockSpec((M, K), lambda j: (0, 0)),
                pl.BlockSpec((None, K, tn), lambda j: (j // nb, 0, j % nb)),
                pl.BlockSpec((None, K, tn), lambda j: (S // 2 + j // nb, 0, j % nb))]
    operands = [hn, w3, w3]
    if after is not None:
        in_specs.append(pl.BlockSpec(memory_space=pl.ANY))
        operands.append(after)
    width = S // 2 * C
    return pl.pallas_call(
        body, name=name, grid=(width // tn,), in_specs=in_specs,
        out_specs=[pl.BlockSpec((2, M, tn), lambda j: (0, 0, j)), pl.BlockSpec((M, tn), lambda j: (0, j))],
        out_shape=[jax.ShapeDtypeStruct((2, M, width), BF16), jax.ShapeDtypeStruct((M, width), BF16)],
        compiler_params=_cparams(("arbitrary",)))(*operands)


def ffn_dact_dgu(name, d_out, w3, gu, scale, *, tm=512, after=None):
    M, N = d_out.shape
    S, K4, _ = w3.shape
    tm = _tile(M, tm, 16)
    to = K4
    chunk = 256 if to > 256 else to

    def body(a_ref, b_ref, gu_ref, *rest):
        o_ref = rest[-1]
        a = a_ref[...]
        for c0 in range(0, to, chunk):
            cols = slice(c0, min(c0 + chunk, to))
            da = lax.dot_general(a, b_ref[cols, :], NT_DIMS, preferred_element_type=F32) * scale
            g = gu_ref[0, :, cols].astype(F32)
            u = gu_ref[1, :, cols].astype(F32)
            s = _sigmoid(g)
            o_ref[0, :, cols] = (da * u * s * (1.0 + g * (1.0 - s))).astype(BF16)
            o_ref[1, :, cols] = (da * g * s).astype(BF16)

    halves = pl.BlockSpec((2, tm, to), lambda o, m: (0, m, o))
    in_specs = [pl.BlockSpec((tm, N), lambda o, m: (m, 0)), pl.BlockSpec((None, to, N), lambda o, m: (o, 0, 0)), halves]
    operands = [d_out, w3, gu]
    if after is not None:
        in_specs.append(pl.BlockSpec(memory_space=pl.ANY))
        operands.append(after)
    return pl.pallas_call(
        body, name=name, grid=(S, M // tm), in_specs=in_specs, out_specs=halves,
        out_shape=jax.ShapeDtypeStruct((2, M, S * K4), BF16),
        compiler_params=_cparams(("parallel", "arbitrary")))(*operands)


def rope_tables(name, pos):
    T = pos.shape[0]
    half = D_ROPE // 2

    def body(p_ref, c_ref, s1_ref, s2_ref):
        lane = lax.broadcasted_iota(jnp.int32, (T, 128), 1)
        idx = (lane & (half - 1)).astype(F32)
        ang = p_ref[...] * jnp.exp(idx * (-2.0 * math.log(ROPE_THETA) / D_ROPE))
        cs, sn = jnp.cos(ang), jnp.sin(ang)
        c_ref[...] = jnp.where(lane < D_ROPE, cs, 0.0)
        s1_ref[...] = jnp.where(lane < half, -sn, 0.0)
        s2_ref[...] = jnp.where((lane >= half) & (lane < D_ROPE), sn, 0.0)

    return pl.pallas_call(body, name=name, out_shape=[jax.ShapeDtypeStruct((T, 128), F32)] * 3,
                          compiler_params=_cparams())(pos)


def _rope(v, cs, s1, s2):
    return v * cs + pltpu.roll(v, 128 - D_ROPE // 2, 1) * s1 + pltpu.roll(v, D_ROPE // 2, 1) * s2


def _rope_bwd(d, cs, s1, s2):
    return d * cs + pltpu.roll(d * s1, D_ROPE // 2, 1) + pltpu.roll(d * s2, 128 - D_ROPE // 2, 1)


def _head_rstd(n, r):
    ms = jnp.sum(n * n + r * r, axis=-1, keepdims=True) * (1.0 / QK_DIM)
    return lax.rsqrt(ms + EPS)


def mla_prep_fwd(name, q_raw, kv_raw, lat, tabs, q_gain, k_gain, rope_col, tm=128):
    T = q_raw.shape[0]
    H = q_raw.shape[1] // HEAD_PAD
    tm = min(tm, T)
    rope_blk = rope_col // 128

    def body(q_ref, kv_ref, kr_ref, c_ref, s1_ref, s2_ref, qg_ref, kg_ref, Q_ref, K_ref, V_ref):
        cs, s1, s2 = c_ref[...], s1_ref[...], s2_ref[...]
        qg, kg = qg_ref[...], kg_ref[...]
        kr = kr_ref[...]
        for h in range(H):
            lo = HEAD_PAD * h
            n, r = q_ref[:, lo:lo + 128], q_ref[:, lo + 128:lo + 256]
            rs = _head_rstd(n, r)
            Q_ref[h, :, 0:128] = (n * rs * qg[:, :128]).astype(BF16)
            Q_ref[h, :, 128:256] = _rope(r * rs * qg[:, 128:], cs, s1, s2).astype(BF16)
            n = kv_ref[:, lo:lo + 128]
            rs = _head_rstd(n, kr)
            K_ref[h, :, 0:128] = (n * rs * kg[:, :128]).astype(BF16)
            K_ref[h, :, 128:256] = _rope(kr * rs * kg[:, 128:], cs, s1, s2).astype(BF16)
            V_ref[h] = kv_ref[:, lo + 128:lo + 256].astype(BF16)

    row = lambda w: pl.BlockSpec((tm, w), lambda i: (i, 0))
    vec = pl.BlockSpec((1, HEAD_PAD), lambda i: (0, 0))
    return pl.pallas_call(
        body, name=name, grid=(T // tm,),
        in_specs=[row(H * HEAD_PAD), row(H * HEAD_PAD), pl.BlockSpec((tm, 128), lambda i: (i, rope_blk)),
                  row(128), row(128), row(128), vec, vec],
        out_specs=[pl.BlockSpec((H, tm, HEAD_PAD), lambda i: (0, i, 0))] * 2 + [pl.BlockSpec((H, tm, D_V), lambda i: (0, i, 0))],
        out_shape=[jax.ShapeDtypeStruct((H, T, HEAD_PAD), BF16)] * 2 + [jax.ShapeDtypeStruct((H, T, D_V), BF16)],
        compiler_params=_cparams(("arbitrary",)),
    )(q_raw, kv_raw, lat, *tabs, q_gain, k_gain)


def mla_prep_bwd(name, dQ, dK, dV, q_raw, kv_raw, lat, tabs, q_gain, k_gain, rope_col, tm=128):
    T = q_raw.shape[0]
    H = q_raw.shape[1] // HEAD_PAD
    tm = min(tm, T)
    rope_blk = rope_col // 128

    def body(dQ_ref, dK_ref, dV_ref, q_ref, kv_ref, kr_ref, c_ref, s1_ref, s2_ref, qg_ref, kg_ref,
             dq_ref, dkv_ref, dkr_ref, dqg_ref, dkg_ref):
        @pl.when(pl.program_id(0) == 0)
        def _():
            dqg_ref[...] = jnp.zeros_like(dqg_ref)
            dkg_ref[...] = jnp.zeros_like(dkg_ref)

        cs, s1, s2 = c_ref[...], s1_ref[...], s2_ref[...]
        qg, kg = qg_ref[...], kg_ref[...]
        kr = kr_ref[...]
        dkr = jnp.zeros((tm, 128), F32)
        gq_n = jnp.zeros((1, 128), F32)
        gq_r = jnp.zeros((1, 128), F32)
        gk_n = jnp.zeros((1, 128), F32)
        gk_r = jnp.zeros((1, 128), F32)

        def norm_bwd(n, r, dn, dr, gain):
            rs = _head_rstd(n, r)
            nh, rh = n * rs, r * rs
            dng, drg = dn * gain[:, :128], dr * gain[:, 128:]
            mean = jnp.sum(dng * nh + drg * rh, axis=-1, keepdims=True) * (1.0 / QK_DIM)
            return rs * (dng - nh * mean), rs * (drg - rh * mean), _colsum(dn * nh), _colsum(dr * rh)

        for h in range(H):
            lo = HEAD_PAD * h
            n, r = q_ref[:, lo:lo + 128], q_ref[:, lo + 128:lo + 256]
            dn = dQ_ref[h, :, 0:128].astype(F32)
            dr = _rope_bwd(dQ_ref[h, :, 128:256].astype(F32), cs, s1, s2)
            a, b, g1, g2 = norm_bwd(n, r, dn, dr, qg)
            dq_ref[:, lo:lo + 128] = a.astype(BF16)
            dq_ref[:, lo + 128:lo + 256] = b.astype(BF16)
            gq_n, gq_r = gq_n + g1, gq_r + g2
            n = kv_ref[:, lo:lo + 128]
            dn = dK_ref[h, :, 0:128].astype(F32)
            dr = _rope_bwd(dK_ref[h, :, 128:256].astype(F32), cs, s1, s2)
            a, b, g1, g2 = norm_bwd(n, kr, dn, dr, kg)
            dkv_ref[:, lo:lo + 128] = a.astype(BF16)
            dkv_ref[:, lo + 128:lo + 256] = dV_ref[h].astype(BF16)
            dkr = dkr + b
            gk_n, gk_r = gk_n + g1, gk_r + g2
        dkr_ref[...] = dkr
        dqg_ref[:, 0:128] += gq_n
        dqg_ref[:, 128:256] += gq_r
        dkg_ref[:, 0:128] += gk_n
        dkg_ref[:, 128:256] += gk_r

    row = lambda w: pl.BlockSpec((tm, w), lambda i: (i, 0))
    vec = pl.BlockSpec((1, HEAD_PAD), lambda i: (0, 0))
    hd = lambda w: pl.BlockSpec((H, tm, w), lambda i: (0, i, 0))
    return pl.pallas_call(
        body, name=name, grid=(T // tm,),
        in_specs=[hd(HEAD_PAD), hd(HEAD_PAD), hd(D_V), row(H * HEAD_PAD), row(H * HEAD_PAD),
                  pl.BlockSpec((tm, 128), lambda i: (i, rope_blk)), row(128), row(128), row(128), vec, vec],
        out_specs=[row(H * HEAD_PAD), row(H * HEAD_PAD), row(128), vec, vec],
        out_shape=[jax.ShapeDtypeStruct((T, H * HEAD_PAD), BF16)] * 2 + [jax.ShapeDtypeStruct((T, 128), F32)]
        + [jax.ShapeDtypeStruct((1, HEAD_PAD), F32)] * 2,
        compiler_params=_cparams(("arbitrary",)),
    )(dQ, dK, dV, q_raw, kv_raw, lat, *tabs, q_gain, k_gain)


def _causal_probs(q, k, scale, row0):
    s = lax.dot_general(q, k, NT_DIMS, preferred_element_type=F32) * scale
    row = row0 + lax.broadcasted_iota(jnp.int32, s.shape, 0)
    col = lax.broadcasted_iota(jnp.int32, s.shape, 1)
    s = jnp.where(col <= row, s, NEG_BIG)
    p = jnp.exp(s - jnp.max(s, axis=-1, keepdims=True))
    return p, jnp.sum(p, axis=-1, keepdims=True)


def attn_fwd(name, Q, K, V, tq=512):
    H, T, E = Q.shape
    tq = min(tq, T)
    nq = T // tq
    scale = QK_DIM ** -0.5

    def body(q_ref, k_ref, v_ref, o_ref):
        i = pl.program_id(1)
        for ib in range(nq):
            @pl.when(i == ib)
            def _():
                n = (ib + 1) * tq
                p, l = _causal_probs(q_ref[...], k_ref[0:n, :], scale, ib * tq)
                o = jnp.dot(p.astype(BF16), v_ref[0:n, :], preferred_element_type=F32)
                o_ref[...] = (o / l).astype(o_ref.dtype)

    return pl.pallas_call(
        body, name=name, grid=(H, nq),
        in_specs=[pl.BlockSpec((None, tq, E), lambda h, i: (h, i, 0)),
                  pl.BlockSpec((None, T, E), lambda h, i: (h, 0, 0)),
                  pl.BlockSpec((None, T, D_V), lambda h, i: (h, 0, 0))],
        out_specs=pl.BlockSpec((tq, D_V), lambda h, i: (i, h)),
        out_shape=jax.ShapeDtypeStruct((T, H * D_V), BF16),
        compiler_params=_cparams(("parallel", "arbitrary")),
    )(Q, K, V)


def attn_bwd(name, Q, K, V, dO, tq=512):
    H, T, E = Q.shape
    tq = min(tq, T)
    nq = T // tq
    scale = QK_DIM ** -0.5

    def body(q_ref, k_ref, v_ref, do_ref, dq_ref, dk_ref, dv_ref):
        i = pl.program_id(1)

        @pl.when(i == 0)
        def _():
            dk_ref[...] = jnp.zeros_like(dk_ref)
            dv_ref[...] = jnp.zeros_like(dv_ref)

        for ib in range(nq):
            @pl.when(i == ib)
            def _():
                n = (ib + 1) * tq
                q, k, v, do = q_ref[...], k_ref[0:n, :], v_ref[0:n, :], do_ref[...]
                p, l = _causal_probs(q, k, scale, ib * tq)
                p = p / l
                dp = lax.dot_general(do, v, NT_DIMS, preferred_element_type=F32)
                ds = p * (dp - jnp.sum(p * dp, axis=-1, keepdims=True)) * scale
                dsb, pb = ds.astype(BF16), p.astype(BF16)
                dq_ref[...] = jnp.dot(dsb, k, preferred_element_type=F32)
                dk_ref[0:n, :] += lax.dot_general(dsb, q, TN_DIMS, preferred_element_type=F32)
                dv_ref[0:n, :] += lax.dot_general(pb, do, TN_DIMS, preferred_element_type=F32)

    return pl.pallas_call(
        body, name=name, grid=(H, nq),
        in_specs=[pl.BlockSpec((None, tq, E), lambda h, i: (h, i, 0)),
                  pl.BlockSpec((None, T, E), lambda h, i: (h, 0, 0)),
                  pl.BlockSpec((None, T, D_V), lambda h, i: (h, 0, 0)),
                  pl.BlockSpec((tq, D_V), lambda h, i: (i, h))],
        out_specs=[pl.BlockSpec((None, tq, E), lambda h, i: (h, i, 0)),
                   pl.BlockSpec((None, T, E), lambda h, i: (h, 0, 0)),
                   pl.BlockSpec((None, T, D_V), lambda h, i: (h, 0, 0))],
        out_shape=[jax.ShapeDtypeStruct((H, T, E), F32)] * 2 + [jax.ShapeDtypeStruct((H, T, D_V), F32)],
        compiler_params=_cparams(("parallel", "arbitrary")),
    )(Q, K, V, dO)


def _taps_by_residue(offsets):
    groups = {}
    for k, off in enumerate(offsets):
        groups.setdefault(off % 8, []).append((k, off - off % 8))
    return groups


def _dw_specs(T, C, tm, tc, halo):
    cur = pl.BlockSpec((tm, tc), lambda j, i: (i, j))
    last = T // tm - 1
    if halo == "prev":
        nbr = pl.BlockSpec((tm, tc), lambda j, i: (jnp.maximum(i - 1, 0), j))
    else:
        nbr = pl.BlockSpec((tm, tc), lambda j, i: (jnp.minimum(i + 1, last), j))
    return cur, nbr


def dwconv_fwd(name, u, w, b, tm=256, tc=512, rs=32):
    T, C = u.shape
    tm, tc = min(tm, T), min(tc, C)
    cur, prev = _dw_specs(T, C, tm, tc, "prev")

    def body(up_ref, uc_ref, w_ref, b_ref, o_ref, scr):
        i = pl.program_id(1)

        @pl.when(i == 0)
        def _():
            scr[pl.ds(0, tm), :] = jnp.zeros((tm, tc), F32)

        @pl.when(i > 0)
        def _():
            scr[pl.ds(0, tm), :] = up_ref[...]

        scr[pl.ds(tm, tm), :] = uc_ref[...]
        for s in range(tm // rs):
            acc = jnp.broadcast_to(b_ref[...], (rs, tc))
            for k in range(CONV_WIDTH):
                acc = acc + w_ref[pl.ds(k, 1), :] * scr[pl.ds(tm - (CONV_WIDTH - 1) + k + rs * s, rs), :]
            o_ref[pl.ds(rs * s, rs), :] = acc

    return pl.pallas_call(
        body, name=name, grid=(C // tc, T // tm),
        in_specs=[prev, cur, pl.BlockSpec((CONV_TAPS_PAD, tc), lambda j, i: (0, j)), pl.BlockSpec((1, tc), lambda j, i: (0, j))],
        out_specs=cur, out_shape=jax.ShapeDtypeStruct((T, C), F32),
        scratch_shapes=[pltpu.VMEM((2 * tm, tc), F32)], compiler_params=_cparams(("parallel", "arbitrary")),
    )(u, u, w, b)


def dwconv_bwd_u(name, dy, w, tm=256, tc=512, rs=32):
    T, C = dy.shape
    tm, tc = min(tm, T), min(tc, C)
    cur, nxt = _dw_specs(T, C, tm, tc, "next")
    last = T // tm - 1

    taps = _taps_by_residue([(CONV_WIDTH - 1) - k for k in range(CONV_WIDTH)])

    def body(dc_ref, dn_ref, w_ref, o_ref, scr, tmp):
        i = pl.program_id(1)
        scr[pl.ds(0, tm), :] = dc_ref[...]
        scr[pl.ds(2 * tm, 8), :] = jnp.zeros((8, tc), F32)

        @pl.when(i == last)
        def _():
            scr[pl.ds(tm, tm), :] = jnp.zeros((tm, tc), F32)

        @pl.when(i < last)
        def _():
            scr[pl.ds(tm, tm), :] = dn_ref[...]

        for s in range(tm // rs):
            acc = jnp.zeros((rs, tc), F32)
            for r, group in taps.items():
                a = jnp.zeros((rs + 8, tc), F32)
                for k, base in group:
                    a = a + w_ref[pl.ds(k, 1), :] * scr[pl.ds(base + rs * s, rs + 8), :]
                if r == 0:
                    acc = acc + a[:rs]
                else:
                    tmp[...] = a
                    acc = acc + tmp[pl.ds(r, rs), :]
            o_ref[pl.ds(rs * s, rs), :] = acc

    return pl.pallas_call(
        body, name=name, grid=(C // tc, T // tm),
        in_specs=[cur, nxt, pl.BlockSpec((CONV_TAPS_PAD, tc), lambda j, i: (0, j))],
        out_specs=cur, out_shape=jax.ShapeDtypeStruct((T, C), F32),
        scratch_shapes=[pltpu.VMEM((2 * tm + 8, tc), F32), pltpu.VMEM((rs + 8, tc), F32)],
        compiler_params=_cparams(("parallel", "arbitrary")),
    )(dy, dy, w)


def dwconv_bwd_w(name, u, dy, tm=256, tc=512, rs=32):
    T, C = u.shape
    tm, tc = min(tm, T), min(tc, C)
    cur, prev = _dw_specs(T, C, tm, tc, "prev")

    taps = _taps_by_residue([tm - (CONV_WIDTH - 1) + k for k in range(CONV_WIDTH)])
    span = tm + 8

    def body(up_ref, uc_ref, dy_ref, o_ref, scr, dyp, dys):
        i = pl.program_id(1)

        @pl.when(i == 0)
        def _():
            scr[pl.ds(0, tm), :] = jnp.zeros((tm, tc), F32)
            o_ref[...] = jnp.zeros_like(o_ref)

        @pl.when(i > 0)
        def _():
            scr[pl.ds(0, tm), :] = up_ref[...]

        scr[pl.ds(tm, tm), :] = uc_ref[...]
        scr[pl.ds(2 * tm, 8), :] = jnp.zeros((8, tc), F32)
        dyp[pl.ds(0, 8), :] = jnp.zeros((8, tc), F32)
        dyp[pl.ds(8, tm), :] = dy_ref[...]
        dyp[pl.ds(8 + tm, 8), :] = jnp.zeros((8, tc), F32)
        for r in taps:
            dys[r] = dyp[pl.ds(8 - r, span), :]
        for r, group in taps.items():
            for k, base in group:
                acc = jnp.zeros((8, tc), F32)
                for s in range(span // 8):
                    acc = acc + dys[r, pl.ds(8 * s, 8), :] * scr[pl.ds(base + 8 * s, 8), :]
                o_ref[pl.ds(k, 1), :] += _colsum(acc)

    return pl.pallas_call(
        body, name=name, grid=(C // tc, T // tm),
        in_specs=[prev, cur, cur],
        out_specs=pl.BlockSpec((CONV_TAPS_PAD, tc), lambda j, i: (0, j)),
        out_shape=jax.ShapeDtypeStruct((CONV_TAPS_PAD, C), F32),
        scratch_shapes=[pltpu.VMEM((2 * tm + 8, tc), F32), pltpu.VMEM((tm + 16, tc), F32),
                        pltpu.VMEM((8, span, tc), F32)],
        compiler_params=_cparams(("parallel", "arbitrary")),
    )(u, u, dy)


def _place():
    x, y, c = lax.axis_index("x"), lax.axis_index("y"), lax.axis_index("c")
    return x, y, c


def _other_chips(x, y):
    return [(1 - x, y, 2 * (1 - x) + y), (x, 1 - y, 2 * x + (1 - y)), (1 - x, 1 - y, 2 * (1 - x) + (1 - y))]


def _hbm_specs(n):
    return [pl.BlockSpec(memory_space=pl.ANY)] * n


HBM_SPEC = pl.BlockSpec(memory_space=pltpu.HBM)
SEM_SPEC = pl.BlockSpec(memory_space=pltpu.SEMAPHORE)
ANY_SPEC = pl.BlockSpec(memory_space=pl.ANY)
SIDE_EFFECT = pltpu.SideEffectType.DATAFLOW_SIDE_EFFECTING


def _half(ref, slot, which):
    rh = ref.shape[1] // 2
    return ref.at[slot, pl.ds(pl.multiple_of(which * rh, 16), rh), :]


def _hbm(a):
    return pltpu.with_memory_space_constraint(a, pltpu.HBM)


def gather_start(name, groups, after):
    flat = [b for g in groups for b in g]
    n, ng = len(flat), len(groups)

    def body(*refs):
        send, recv = refs[n + 1:n + 1 + ng], refs[n + 1 + ng:n + 1 + 2 * ng]
        out, token = refs[n + 1 + 2 * ng:2 * n + 1 + 2 * ng], refs[2 * n + 1 + 2 * ng]
        token[...] = jnp.zeros_like(token)
        x, y, c = _place()
        me = 2 * x + y
        a = 0
        for g, grp in enumerate(groups):
            for k in range(len(grp)):
                piece = _half(out[a], me, c)
                for j, (px, py, _) in enumerate(_other_chips(x, y)):
                    pltpu.make_async_remote_copy(
                        src_ref=piece, dst_ref=piece, send_sem=send[g].at[3 * k + j], recv_sem=recv[g].at[3 * k + j],
                        device_id=(px, py, c), device_id_type=MESH).start()
                a += 1

    sems = [pltpu.SemaphoreType.DMA((3 * len(g),)) for g in groups]
    res = pl.pallas_call(
        body, name=name, in_specs=[HBM_SPEC] * n + [ANY_SPEC],
        out_specs=[SEM_SPEC] * (2 * ng) + [HBM_SPEC] * n + [pl.BlockSpec(memory_space=pltpu.VMEM)],
        out_shape=sems + sems + [pltpu.HBM(b.shape, b.dtype) for b in flat] + [jax.ShapeDtypeStruct((8, 128), F32)],
        input_output_aliases={a: 2 * ng + a for a in range(n)},
        compiler_params=pltpu.CompilerParams(has_side_effects=SIDE_EFFECT),
    )(*[_hbm(b) for b in flat], after)
    send, recv, bufs = res[:ng], res[ng:2 * ng], list(res[2 * ng:2 * ng + n])
    out, a = [], 0
    for g, grp in enumerate(groups):
        out.append((send[g], recv[g], bufs[a:a + len(grp)]))
        a += len(grp)
    return out, res[2 * ng + n]


def gather_relay(name, started, after):
    send1, recv1, bufs = started
    n = len(bufs)

    def body(*refs):
        s1, r1 = refs[n], refs[n + 1]
        s2, r2, out, token = refs[n + 3], refs[n + 4], refs[n + 5:2 * n + 5], refs[2 * n + 5]
        x, y, c = _place()
        me = 2 * x + y
        chips = _other_chips(x, y)
        for k in range(n):
            for j, (px, py, idx) in enumerate(chips):
                cp = pltpu.make_async_remote_copy(
                    src_ref=_half(out[k], me, c), dst_ref=_half(out[k], idx, c), send_sem=s1.at[3 * k + j],
                    recv_sem=r1.at[3 * k + j], device_id=(px, py, c), device_id_type=MESH)
                cp.wait_send()
                cp.wait_recv()
        for k in range(n):
            for j, (px, py, idx) in enumerate(chips):
                piece = _half(out[k], idx, c)
                pltpu.make_async_remote_copy(
                    src_ref=piece, dst_ref=piece, send_sem=s2.at[3 * k + j], recv_sem=r2.at[3 * k + j],
                    device_id=(x, y, 1 - c), device_id_type=MESH).start()
        token[...] = jnp.zeros_like(token)

    sem = pltpu.SemaphoreType.DMA((3 * n,))
    res = pl.pallas_call(
        body, name=name, in_specs=[HBM_SPEC] * n + [SEM_SPEC, SEM_SPEC, ANY_SPEC],
        out_specs=[SEM_SPEC, SEM_SPEC] + [HBM_SPEC] * n + [pl.BlockSpec(memory_space=pltpu.VMEM)],
        out_shape=[sem, sem] + [pltpu.HBM(b.shape, b.dtype) for b in bufs] + [jax.ShapeDtypeStruct((8, 128), F32)],
        input_output_aliases={a: 2 + a for a in range(n)},
        compiler_params=pltpu.CompilerParams(has_side_effects=SIDE_EFFECT),
    )(*bufs, send1, recv1, after)
    return res[0], res[1], list(res[2:2 + n]), res[2 + n]


def gather_wait(name, relayed, after):
    send2, recv2, bufs, _ = relayed
    n = len(bufs)

    def body(*refs):
        s2, r2, out = refs[n], refs[n + 1], refs[n + 3:]
        x, y, c = _place()
        for k in range(n):
            for j, (px, py, idx) in enumerate(_other_chips(x, y)):
                cp = pltpu.make_async_remote_copy(
                    src_ref=_half(out[k], idx, c), dst_ref=_half(out[k], idx, 1 - c), send_sem=s2.at[3 * k + j],
                    recv_sem=r2.at[3 * k + j], device_id=(x, y, 1 - c), device_id_type=MESH)
                cp.wait_send()
                cp.wait_recv()

    res = pl.pallas_call(
        body, name=name, in_specs=[HBM_SPEC] * n + [SEM_SPEC, SEM_SPEC, ANY_SPEC], out_specs=[HBM_SPEC] * n,
        out_shape=[pltpu.HBM(b.shape, b.dtype) for b in bufs], input_output_aliases={a: a for a in range(n)},
        compiler_params=pltpu.CompilerParams(has_side_effects=SIDE_EFFECT),
    )(*bufs, send2, recv2, after)
    return list(res)


def split_start(name, parts, after):
    bufs = [b for p in parts for b in p[0]]
    land_shapes = [s for p in parts for s in p[1]]
    nb, nl = len(bufs), len(land_shapes)

    def body(*refs):
        send, recv = refs[nb + 1], refs[nb + 2]
        out, lands, token = refs[nb + 3:2 * nb + 3], refs[2 * nb + 3:2 * nb + 3 + nl], refs[2 * nb + 3 + nl]
        x, y, c = _place()
        b0 = l0 = k0 = 0
        for p_bufs, p_lands, n_copies, plan in parts:
            copies = plan(out[b0:b0 + len(p_bufs)], lands[l0:l0 + len(p_lands)], x, y, c)
            for k, (src, dst, to, _) in enumerate(copies):
                pltpu.make_async_remote_copy(src_ref=src, dst_ref=dst, send_sem=send.at[k0 + k],
                                             recv_sem=recv.at[k0 + k], device_id=to, device_id_type=MESH).start()
            b0, l0, k0 = b0 + len(p_bufs), l0 + len(p_lands), k0 + n_copies
        token[...] = jnp.zeros_like(token)

    sem = pltpu.SemaphoreType.DMA((sum(p[2] for p in parts),))
    res = pl.pallas_call(
        body, name=name, in_specs=[HBM_SPEC] * nb + [ANY_SPEC],
        out_specs=[SEM_SPEC, SEM_SPEC] + [HBM_SPEC] * (nb + nl) + [pl.BlockSpec(memory_space=pltpu.VMEM)],
        out_shape=[sem, sem] + [pltpu.HBM(b.shape, b.dtype) for b in bufs]
        + [pltpu.HBM(s, d) for s, d in land_shapes] + [jax.ShapeDtypeStruct((8, 128), F32)],
        input_output_aliases={a: 2 + a for a in range(nb)},
        compiler_params=pltpu.CompilerParams(has_side_effects=SIDE_EFFECT),
    )(*[_hbm(b) for b in bufs], after)
    out_bufs, out_lands, token = list(res[2:2 + nb]), list(res[2 + nb:2 + nb + nl]), res[2 + nb + nl]
    states, b0, l0, k0 = [], 0, 0, 0
    for p_bufs, p_lands, n_copies, _ in parts:
        states.append((res[0], res[1], out_bufs[b0:b0 + len(p_bufs)], out_lands[l0:l0 + len(p_lands)], token, k0))
        b0, l0, k0 = b0 + len(p_bufs), l0 + len(p_lands), k0 + n_copies
    return states


def split_wait(name, started, plan, after):
    send, recv, bufs, lands, _, k0 = started
    nb, nl = len(bufs), len(lands)

    def body(*refs):
        s, r = refs[nb + nl], refs[nb + nl + 1]
        out, lo = refs[nb + nl + 3:2 * nb + nl + 3], refs[2 * nb + nl + 3:]
        x, y, c = _place()
        for k, (src, _, to, landed) in enumerate(plan(out, lo, x, y, c)):
            cp = pltpu.make_async_remote_copy(src_ref=src, dst_ref=landed, send_sem=s.at[k0 + k],
                                              recv_sem=r.at[k0 + k], device_id=to, device_id_type=MESH)
            cp.wait_send()
            cp.wait_recv()

    res = pl.pallas_call(
        body, name=name, in_specs=[HBM_SPEC] * (nb + nl) + [SEM_SPEC, SEM_SPEC, ANY_SPEC],
        out_specs=[HBM_SPEC] * (nb + nl), out_shape=[pltpu.HBM(b.shape, b.dtype) for b in bufs + lands],
        input_output_aliases={a: a for a in range(nb + nl)},
        compiler_params=pltpu.CompilerParams(has_side_effects=SIDE_EFFECT),
    )(*bufs, *lands, send, recv, after)
    return list(res[:nb]), list(res[nb:])


def swap_halves_plan(parts, lands, x, y, c):
    out = []
    for a in range(len(parts)):
        rh = parts[a].shape[1] // 2
        theirs = parts[a].at[:, pl.ds(pl.multiple_of((1 - c) * rh, 16), rh), :]
        out.append((theirs, lands[a], (x, y, 1 - c), lands[a]))
    return out


def chip_exchange_plan(sums, lands, x, y, c):
    me = 2 * x + y
    out = []
    for a in range(len(sums)):
        for (px, py, idx), flip in zip(_other_chips(x, y), (2, 1, 3)):
            out.append((sums[a].at[flip - 1], lands[a].at[me], (px, py, c), lands[a].at[idx]))
    return out


def share_grad_plan(halves, lands, x, y, c):
    return [(halves[a], lands[a], (x, y, 1 - c), lands[a]) for a in range(len(halves))]


def allreduce_pack(name, pack):
    R, W = pack.shape

    def body(p_ref, o_ref, sib, pair, got, send_sems, recv_sems):
        x, y, c = _place()

        def swap(k, src, dst, to):
            cp = pltpu.make_async_remote_copy(src_ref=src, dst_ref=dst, send_sem=send_sems.at[k],
                                              recv_sem=recv_sems.at[k], device_id=to, device_id_type=MESH)
            cp.start()
            return cp

        cp = swap(0, p_ref, sib, (x, y, 1 - c))
        cp.wait()
        pair[...] = p_ref[...] + sib[...]
        cps = [swap(1, pair, got.at[0], (1 - x, y, c)), swap(2, pair, got.at[1], (x, 1 - y, c)),
               swap(3, pair, got.at[2], (1 - x, 1 - y, c))]
        for cp in cps:
            cp.wait()
        o_ref[...] = (pair[...] + got[1]) + (got[0] + got[2])

    return pl.pallas_call(
        body, name=name, out_shape=jax.ShapeDtypeStruct((R, W), F32),
        in_specs=[pl.BlockSpec(memory_space=pltpu.VMEM)], out_specs=pl.BlockSpec(memory_space=pltpu.VMEM),
        scratch_shapes=[pltpu.VMEM((R, W), F32), pltpu.VMEM((R, W), F32), pltpu.VMEM((3, R, W), F32),
                        pltpu.SemaphoreType.DMA((4,)), pltpu.SemaphoreType.DMA((4,))],
        compiler_params=_cparams(),
    )(pack)


BIG_WEIGHTS = [
    ("ffn_a_w_in", "col"), ("ffn_a_w_out", "row"), ("ffn_b_w_in", "col"), ("ffn_b_w_out", "row"),
    ("mla_w_in", "row"), ("mla_w_uq", "col"), ("mla_w_ukv", "col"), ("mla_w_o", "row"),
    ("conv_w_pw1", "col"), ("conv_w_pw2", "row"), ("ple_w_proj", "col"), ("ple_w_gate", "row"),
]
WEIGHT_ORDER = ["ffn_a_norm", "ffn_a_w_in", "ffn_a_w_out", "ffn_b_norm", "ffn_b_w_in", "ffn_b_w_out", "mix_norm",
                "mla_w_in", "mla_q_lat_norm", "mla_kv_lat_norm", "mla_w_uq", "mla_w_ukv", "mla_q_gain", "mla_k_gain",
                "mla_w_o", "conv_w_pw1", "conv_b_pw1", "conv_w_dw", "conv_b_dw", "conv_ln_g", "conv_ln_b", "conv_w_pw2",
                "ple_w_proj", "ple_norm", "ple_gate_norm", "ple_w_gate"]
REPLICATED_SMALL = ["ffn_a_norm", "ffn_b_norm", "mix_norm", "ple_norm", "ple_gate_norm",
                    "mla_q_lat_norm", "mla_kv_lat_norm", "mla_q_gain", "mla_k_gain"]
SHARDED_SMALL = ["conv_b_pw1", "conv_w_dw", "conv_b_dw", "conv_ln_g", "conv_ln_b"]
PACK_ROWS = 8


def _pack_rows(arrs, width):
    out = []
    for a in arrs:
        r = -(-a.shape[0] // PACK_ROWS) * PACK_ROWS
        out.append(jnp.pad(a, ((0, r - a.shape[0]), (0, width - a.shape[1]))))
    return jnp.concatenate(out, axis=0)


def _unpack_rows(pack, shapes):
    out, r0 = [], 0
    for (r, w) in shapes:
        out.append(pack[r0:r0 + r, :w])
        r0 += -(-r // PACK_ROWS) * PACK_ROWS
    return out


def kernel(x, p, positions, ffn_a_norm, ffn_a_w_in, ffn_a_w_out, ffn_b_norm, ffn_b_w_in, ffn_b_w_out, mix_norm, mla_w_in, mla_q_lat_norm, mla_kv_lat_norm, mla_w_uq, mla_w_ukv, mla_q_gain, mla_k_gain, mla_w_o, conv_w_pw1, conv_b_pw1, conv_w_dw, conv_b_dw, conv_ln_g, conv_ln_b, conv_w_pw2, ple_w_proj, ple_norm, ple_gate_norm, ple_w_gate, loss_target, m_ffn_a_norm, m_ffn_a_w_in, m_ffn_a_w_out, m_ffn_b_norm, m_ffn_b_w_in, m_ffn_b_w_out, m_mix_norm, m_mla_w_in, m_mla_q_lat_norm, m_mla_kv_lat_norm, m_mla_w_uq, m_mla_w_ukv, m_mla_q_gain, m_mla_k_gain, m_mla_w_o, m_conv_w_pw1, m_conv_b_pw1, m_conv_w_dw, m_conv_b_dw, m_conv_ln_g, m_conv_ln_b, m_conv_w_pw2, m_ple_w_proj, m_ple_norm, m_ple_gate_norm, m_ple_w_gate, v_ffn_a_norm, v_ffn_a_w_in, v_ffn_a_w_out, v_ffn_b_norm, v_ffn_b_w_in, v_ffn_b_w_out, v_mix_norm, v_mla_w_in, v_mla_q_lat_norm, v_mla_kv_lat_norm, v_mla_w_uq, v_mla_w_ukv, v_mla_q_gain, v_mla_k_gain, v_mla_w_o, v_conv_w_pw1, v_conv_b_pw1, v_conv_w_dw, v_conv_b_dw, v_conv_ln_g, v_conv_ln_b, v_conv_w_pw2, v_ple_w_proj, v_ple_norm, v_ple_gate_norm, v_ple_w_gate):
    args = dict(locals())
    W = {n: args[n] for n in WEIGHT_ORDER}
    M1 = {n: args["m_" + n] for n in WEIGHT_ORDER}
    V2 = {n: args["v_" + n] for n in WEIGHT_ORDER}

    T, D = x.shape[1], x.shape[2]
    depth = ffn_a_norm.shape[0]
    H = mla_w_ukv.shape[2] * N_CHIPS // (D_NOPE + D_V)
    QL, KL = mla_q_lat_norm.shape[1], mla_kv_lat_norm.shape[1]
    C = conv_w_pw2.shape[1] * N_CHIPS
    lat_w = QL + KL + D_ROPE
    lat_pad = QL + KL + 128

    cx, cy, cc = lax.axis_index("x"), lax.axis_index("y"), lax.axis_index("c")
    chip = (2 * cx + cy).astype(jnp.int32)
    chip_arr = chip.reshape(1)
    core_arr = cc.astype(jnp.int32).reshape(1)

    def stage_groups(i):
        mix = ([("mla_w_in", i // 2), ("mla_w_uq", i // 2), ("mla_w_ukv", i // 2), ("mla_w_o", i // 2)] if i % 2 == 0
               else [("conv_w_pw1", i // 2), ("conv_w_pw2", i // 2)])
        return [[("ffn_a_w_in", i)], [("ffn_a_w_out", i)], mix, [("ffn_b_w_in", i)], [("ffn_b_w_out", i)],
                [("ple_w_proj", i), ("ple_w_gate", i)]]

    groups = [g for i in range(depth) for g in stage_groups(i)]
    A_IN, A_OUT, MIX, B_IN, B_OUT, PLE, PER_LAYER = 0, 1, 2, 3, 4, 5, 6

    def slot_of(key, after=None):
        n, l = key
        return cast_into_slot(f"cast_{n}_{l}", W[n].reshape(-1, W[n].shape[-1]), chip_arr, l, W[n].shape[0], after)

    small_mine = _pack_rows([conv_b_pw1.reshape(1, -1), conv_w_dw[0], conv_b_dw, conv_ln_g, conv_ln_b],
                            conv_b_pw1.shape[-1])
    small_slot = lax.dynamic_update_slice(jnp.zeros((N_CHIPS,) + small_mine.shape, F32), small_mine[None],
                                          (chip, 0, 0))
    EXTRA = {PER_LAYER + MIX: [("small", 0)]} if depth > 1 else {}

    def pieces(gi, after=None):
        return [slot_of(k, after) for k in groups[gi]] + [small_slot] * (gi in EXTRA)

    FIRST = 2
    started, tok0 = gather_start("gather_start_first", [pieces(gi) for gi in range(FIRST)], chip_arr)
    small = REPLICATED_SMALL + SHARDED_SMALL
    small_packs = [_pack_rows([src[n].reshape(-1, src[n].shape[-1]) for n in small], D) for src in (W, M1, V2)]
    tok0 = tok0 + 0.0 * (small_packs[0][:8, :128] + small_packs[1][:8, :128] + small_packs[2][:8, :128])
    rest, tok_rest = gather_start("gather_start_rest", [pieces(gi, tok0) for gi in range(FIRST, len(groups))], tok0)
    started = started + rest
    relayed, G = {}, {}

    def relay(g, after):
        if g >= len(groups):
            return None
        relayed[g] = gather_relay(f"gather_relay_{g}", started[g], after)
        return relayed[g][3]

    def ready(g, after):
        if g not in relayed:
            relay(g, after)
        for key, buf in zip(groups[g] + EXTRA.get(g, []), gather_wait(f"gather_wait_{g}", relayed[g], after)):
            G[key] = buf

    relay(0, tok_rest)

    pad_gain = lambda g: jnp.pad(g, ((0, 0), (0, HEAD_PAD - QK_DIM)))
    q_gain_p, k_gain_p = pad_gain(mla_q_gain), pad_gain(mla_k_gain)
    tabs = rope_tables("rope_tables", positions.reshape(T, 1).astype(F32))

    def ffn_fwd(tag, h, norm, w_in, w_out, layer, g_in):
        hn = rms_fwd(f"{tag}_rms", h, norm)
        ready(g_in, hn)
        tok = relay(g_in + 1, hn) if g_in > 0 else None
        gu, act = ffn_in_act(f"{tag}_in", hn, G[(w_in, layer)], after=tok)
        ready(g_in + 1, act)
        tok = relay(g_in + 2, act)
        out = mm_nn(f"{tag}_out", act, G[(w_out, layer)], "row", 0, out_dtype=F32, scale=FFN_RESIDUAL_WEIGHT, res=h,
                    tm=1024, tn=512, whole_k=True, after=tok)
        return out, (h, hn, gu, act)

    saved = []
    h = x[0]
    for i in range(depth):
        L = {}
        g0 = PER_LAYER * i
        h, L["ffn_a"] = ffn_fwd(f"l{i}_ffa", h, ffn_a_norm[i:i + 1], "ffn_a_w_in", "ffn_a_w_out", i, g0 + A_IN)
        L["h1"] = h
        hn = rms_fwd(f"l{i}_mix_rms", h, mix_norm[i:i + 1])
        L["hn_m"] = hn
        ready(g0 + MIX, hn)
        j = i // 2
        if i % 2 == 0:
            w_in_pad = jnp.pad(G[("mla_w_in", j)].reshape(D, lat_w), ((0, 0), (0, lat_pad - lat_w)))[None]
            uq = G[("mla_w_uq", j)].transpose(1, 0, 2).reshape(QL, H, QK_DIM)
            w_uq_pad = jnp.pad(uq, ((0, 0), (0, 0), (0, HEAD_PAD - QK_DIM))).reshape(1, QL, H * HEAD_PAD)
            lat = mm_nn(f"l{i}_lat", hn, w_in_pad, "row", 0, out_dtype=F32, tm=2048)
            cq, ckv = lat_norm_fwd(f"l{i}_latnorm", lat, mla_q_lat_norm[j:j + 1], mla_kv_lat_norm[j:j + 1])
            q_raw = mm_nn(f"l{i}_uq", cq, w_uq_pad, "row", 0, out_dtype=F32, tm=2048)
            kv_raw = mm_nn(f"l{i}_ukv", ckv, G[("mla_w_ukv", j)], "col", 0, out_dtype=F32, tm=2048)
            Qh, Kh, Vh = mla_prep_fwd(f"l{i}_prep", q_raw, kv_raw, lat, tabs, q_gain_p, k_gain_p, QL + KL)
            O = attn_fwd(f"l{i}_attn", Qh, Kh, Vh)
            tok = relay(g0 + B_IN, O)
            h = mm_nn(f"l{i}_wo", O, G[("mla_w_o", j)], "row", 0, out_dtype=F32, res=h, tm=1024, tn=1024, after=tok)
            L["mla"] = (lat, cq, ckv, q_raw, kv_raw, Qh, Kh, Vh, O, w_in_pad, w_uq_pad)
        else:
            sm, cw = G[("small", 0)], C // N_CHIPS
            b_pw1_full = sm[:, 0, :].reshape(1, 2 * C)
            w_dw_full = sm[:, 8:8 + CONV_TAPS_PAD, :cw].transpose(1, 0, 2).reshape(CONV_TAPS_PAD, C)
            b_dw_full, ln_g_full, ln_b_full = [sm[:, r, :cw].reshape(1, C) for r in (40, 48, 56)]
            ag = mm_nn(f"l{i}_pw1", hn, G[("conv_w_pw1", j)], "col", 0, out_dtype=F32, bias=b_pw1_full, tm=2048)
            u = glu_fwd(f"l{i}_glu", ag)
            yc = dwconv_fwd(f"l{i}_dw", u, w_dw_full, b_dw_full)
            cact = ln_silu_fwd(f"l{i}_ln", yc, ln_g_full, ln_b_full)
            tok = relay(g0 + B_IN, cact)
            h = mm_nn(f"l{i}_pw2", cact, G[("conv_w_pw2", j)], "row", 0, out_dtype=F32, res=h, tm=1024, tn=1024,
                      after=tok)
            L["conv"] = (ag, u, yc, cact)
        L["h2"] = h
        h, L["ffn_b"] = ffn_fwd(f"l{i}_ffb", h, ffn_b_norm[i:i + 1], "ffn_b_w_in", "ffn_b_w_out", i, g0 + B_IN)
        L["h3"] = h
        ready(g0 + PLE, h)
        pe = mm_nn(f"l{i}_ple_proj", p[i, 0], G[("ple_w_proj", i)], "col", 0, out_dtype=F32, tm=2048)
        hg = rms_fwd(f"l{i}_gate_rms", h, ple_gate_norm[i:i + 1])
        tok = relay(g0 + PER_LAYER, hg)
        z = mm_nn(f"l{i}_ple_gate", hg, G[("ple_w_gate", i)], "row", 0, out_dtype=F32, tm=2048, after=tok)
        h = ple_fwd(f"l{i}_ple", h, pe, z, ple_norm[i:i + 1])
        L["ple"] = (pe, hg, z)
        saved.append(L)

    d_h, loss_part = loss_head("loss_head", h, loss_target[0])
    loss = lax.psum(loss_part[0, 0], ("x", "y", "c"))

    GW = {}
    SG = {}
    ids_arr = jnp.stack([cc.astype(jnp.int32), chip])
    two = lambda a: a.reshape(-1, a.shape[-1])
    merged = {}
    pipe = {}
    swapped = {}
    order = list(reversed(range(len(groups))))
    ticks = [0]

    def put_small(name, i, val):
        SG.setdefault(name, {})[i] = val

    deferred = []

    def sibling_half(g, after):
        _, theirs = split_wait(f"share_wait_{g}", pipe[g], share_grad_plan, after)
        for (n, l), gr in zip(groups[g], theirs):
            merged[n] = adamw_half(f"adamw_sib_{n}_{l}", ids_arr, gr, two(W[n]), two(M1[n]), two(V2[n]), l,
                                   W[n].shape[0], merged[n])
        return merged[groups[g][-1][0]][0]

    def reduce_tick(after, defer=True):
        k, tok = ticks[0], after
        ticks[0] += 1
        grp = lambda j: order[j] if 0 <= j < len(order) else None
        g = grp(k - 4)
        if g is not None:
            if defer:
                deferred.append(g)
            else:
                sibling_half(g, tok)
        starts = []
        g = grp(k - 3)
        if g is not None:
            _, landed = split_wait(f"exchange_wait_{g}", pipe[g], chip_exchange_plan, tok)
            halves = []
            for (n, l), (q, r), ld in zip(groups[g], swapped[g], landed):
                *merged[n], mine = adamw_half(f"adamw_own_{n}_{l}", ids_arr, (q, r, ld), two(W[n]), two(M1[n]),
                                              two(V2[n]), l, W[n].shape[0], merged.get(n))
                halves.append(mine)
            starts.append((g, (halves, [(h.shape, h.dtype) for h in halves], len(halves), share_grad_plan)))
        g = grp(k - 1)
        if g is not None:
            parts, got = split_wait(f"swap_wait_{g}", pipe[g], swap_halves_plan, tok)
            swapped[g] = list(zip(parts, got))
            sums = [add_halves(f"pair_sum_{n}_{l}", ids_arr, q, r) for (n, l), q, r in zip(groups[g], parts, got)]
            lands = [((N_CHIPS,) + s.shape[1:], s.dtype) for s in sums]
            starts.append((g, (sums, lands, 3 * len(sums), chip_exchange_plan)))
        g = grp(k)
        if g is not None:
            parts = [GW[key] for key in groups[g]]
            lands = [((q.shape[0], q.shape[1] // 2, q.shape[2]), q.dtype) for q in parts]
            starts.append((g, (parts, lands, len(parts), swap_halves_plan)))
        if starts:
            behind = ids_arr if grp(k) is not None else tok
            for (g, _), state in zip(starts, split_start(f"reduce_start_{k}", [p for _, p in starts], behind)):
                pipe[g] = state
            tok = state[4]
        return tok

    def ffn_bwd(tag, d_h, d_hb, norm, w_in, w_out, layer, fw, tok):
        h_in, hn, gu, act = fw
        GW[(w_out, layer)] = mm_tn(f"{tag}_dwout", act, d_hb, "row", scale=FFN_RESIDUAL_WEIGHT, tk=1408, tn=1024,
                                   after=tok)
        tok = reduce_tick(GW[(w_out, layer)])
        dgu = ffn_dact_dgu(f"{tag}_dact", d_hb, G[(w_out, layer)], gu, FFN_RESIDUAL_WEIGHT, after=tok)
        GW[(w_in, layer)] = mm_tn(f"{tag}_dwin", hn, dgu, "col")
        tok = reduce_tick(GW[(w_in, layer)])
        d_hn = _mm_nt(f"{tag}_dhn", dgu, G[(w_in, layer)], "col", 0, D, out_dtype=F32, tc=2816, after=tok)
        return (*rms_bwd_res(f"{tag}_drms", h_in, d_hn, norm, d_h), tok)

    d_hb, tok = None, None
    for i in reversed(range(depth)):
        L = saved[i]
        j = i // 2
        pe, hg, z = L["ple"]
        d_z, d_pe, g = ple_bwd(f"l{i}_dple", d_h, pe, z, ple_norm[i:i + 1])
        put_small("ple_norm", i, g)
        d_hg = _mm_nt(f"l{i}_dhg", d_z, G[("ple_w_gate", i)], "row", 0, D, out_dtype=F32, to=512, tc=2048, after=tok)
        GW[("ple_w_gate", i)] = mm_tn(f"l{i}_dwgate", hg, d_z, "row", tk=512, tn=1024)
        GW[("ple_w_proj", i)] = mm_tn(f"l{i}_dwproj", p[i, 0], d_pe, "col")
        tok = reduce_tick(GW[("ple_w_proj", i)])
        d_h, d_hb, g = rms_bwd_res(f"l{i}_dgate_rms", L["h3"], d_hg, ple_gate_norm[i:i + 1], d_h)
        put_small("ple_gate_norm", i, g)

        d_h, d_hb, g, tok = ffn_bwd(f"l{i}_ffb", d_h, d_hb, ffn_b_norm[i:i + 1], "ffn_b_w_in", "ffn_b_w_out", i,
                                    L["ffn_b"], tok)
        put_small("ffn_b_norm", i, g)

        hn = L["hn_m"]
        if i % 2 == 0:
            lat, cq, ckv, q_raw, kv_raw, Qh, Kh, Vh, O, w_in_pad, w_uq_pad = L["mla"]
            d_O = _mm_nt(f"l{i}_dO", d_hb, G[("mla_w_o", j)], "row", 0, H * D_V, out_dtype=BF16, to=512, tc=2048,
                         after=tok)
            GW[("mla_w_o", j)] = mm_tn(f"l{i}_dwo", O, d_hb, "row", tk=512, tn=1024)
            dQ, dK, dV = attn_bwd(f"l{i}_dattn", Qh, Kh, Vh, d_O)
            d_q_raw, d_kv_raw, d_kr, gq, gk = mla_prep_bwd(f"l{i}_dprep", dQ, dK, dV, q_raw, kv_raw, lat, tabs,
                                                           q_gain_p, k_gain_p, QL + KL)
            put_small("mla_q_gain", j, gq[:, :QK_DIM])
            put_small("mla_k_gain", j, gk[:, :QK_DIM])
            d_cq = _mm_nt(f"l{i}_dcq", d_q_raw, w_uq_pad, "row", 0, QL, out_dtype=F32, to=512, tc=2048)
            g_uq = mm_tn(f"l{i}_dwuq", cq, d_q_raw, "row", shards=1, out_dtype=F32, tk=512, tn=1024)
            g_uq = g_uq.reshape(QL, H, HEAD_PAD)[:, :, :QK_DIM].reshape(QL, N_CHIPS, -1).transpose(1, 0, 2)
            GW[("mla_w_uq", j)] = g_uq.astype(BF16)
            d_ckv = _mm_nt(f"l{i}_dckv", d_kv_raw, G[("mla_w_ukv", j)], "col", 0, KL, out_dtype=F32, to=512, tc=1024)
            GW[("mla_w_ukv", j)] = mm_tn(f"l{i}_dwukv", ckv, d_kv_raw, "col", tk=512, tn=1024)
            d_lat, gq, gk = lat_norm_bwd(f"l{i}_dlatnorm", lat, d_cq, d_ckv, d_kr, mla_q_lat_norm[j:j + 1],
                                         mla_kv_lat_norm[j:j + 1])
            put_small("mla_q_lat_norm", j, gq)
            put_small("mla_kv_lat_norm", j, gk)
            d_hn = _mm_nt(f"l{i}_dhn_lat", d_lat, w_in_pad, "row", 0, D, out_dtype=F32, to=1024, tc=lat_pad)
            g_in = mm_tn(f"l{i}_dwin_lat", hn, d_lat, "row", shards=1, out_dtype=F32, tk=1024, tn=lat_pad)
            GW[("mla_w_in", j)] = g_in[0, :, :lat_w].reshape(N_CHIPS, D // N_CHIPS, lat_w).astype(BF16)
            tok = reduce_tick(GW[("mla_w_in", j)])
        else:
            ag, u, yc, cact = L["conv"]
            d_cact = _mm_nt(f"l{i}_dcact", d_hb, G[("conv_w_pw2", j)], "row", 0, C, out_dtype=F32, to=512, tc=2048,
                            after=tok)
            GW[("conv_w_pw2", j)] = mm_tn(f"l{i}_dwpw2", cact, d_hb, "row", tk=512, tn=1024)
            d_yc, g1, g2, g3 = ln_silu_bwd(f"l{i}_dln", yc, d_cact, ln_g_full, ln_b_full)
            put_small("conv_ln_g", j, g1)
            put_small("conv_ln_b", j, g2)
            put_small("conv_b_dw", j, g3)
            d_u = dwconv_bwd_u(f"l{i}_ddw_u", d_yc, w_dw_full)
            put_small("conv_w_dw", j, dwconv_bwd_w(f"l{i}_ddw_w", u, d_yc))
            d_ag, g = glu_bwd(f"l{i}_dglu", ag, d_u)
            put_small("conv_b_pw1", j, g)
            d_hn = _mm_nt(f"l{i}_dhn_pw1", d_ag, G[("conv_w_pw1", j)], "col", 0, D, out_dtype=F32, tc=1024)
            GW[("conv_w_pw1", j)] = mm_tn(f"l{i}_dwpw1", hn, d_ag, "col", tn=1024)
            tok = reduce_tick(GW[("conv_w_pw1", j)])
        d_h, d_hb, g = rms_bwd_res(f"l{i}_dmix_rms", L["h1"], d_hn, mix_norm[i:i + 1], d_h)
        put_small("mix_norm", i, g)

        d_h, d_hb, g, tok = ffn_bwd(f"l{i}_ffa", d_h, d_hb, ffn_a_norm[i:i + 1], "ffn_a_w_in", "ffn_a_w_out", i,
                                    L["ffn_a"], tok)
        put_small("ffn_a_norm", i, g)
    grad_x = d_h[None]

    tok = reduce_tick(reduce_tick(d_h))
    for g in deferred:
        tok = sibling_half(g, tok)
    tok = reduce_tick(reduce_tick(tok, defer=False), defer=False)
    names = [n for n, _ in BIG_WEIGHTS]
    grads, delta, new_m, new_v = {}, {}, {}, {}
    for n in names:
        grads[n], delta[n], new_m[n], new_v[n] = [a.reshape(W[n].shape) for a in merged[n]]

    rep = []
    for n in REPLICATED_SMALL:
        rep.append(jnp.concatenate([SG[n][i] for i in sorted(SG[n])], axis=0))
    shd = [SG["conv_b_pw1"][0].reshape(2, C), SG["conv_w_dw"][0][:CONV_WIDTH], SG["conv_b_dw"][0],
           SG["conv_ln_g"][0], SG["conv_ln_b"][0]]
    red = allreduce_pack("allreduce_small", _pack_rows(rep + shd, D))
    red = _unpack_rows(red, [a.shape for a in rep + shd])
    for n, g in zip(REPLICATED_SMALL, red):
        grads[n] = g
    own = lambda a, w: lax.dynamic_slice_in_dim(a, chip * w, w, axis=a.ndim - 1)
    sh = red[len(rep):]
    grads["conv_b_pw1"] = own(sh[0].reshape(1, 2 * C), 2 * C // N_CHIPS)
    grads["conv_w_dw"] = own(sh[1], C // N_CHIPS)[None]
    grads["conv_b_dw"] = own(sh[2], C // N_CHIPS)
    grads["conv_ln_g"] = own(sh[3], C // N_CHIPS)
    grads["conv_ln_b"] = own(sh[4], C // N_CHIPS)

    shapes = [two(W[n]).shape for n in small]
    grad_pack = _pack_rows([two(grads[n]) for n in small], D)
    outs = adamw("adamw_small", small_packs[0], grad_pack, small_packs[1], small_packs[2])
    for dst, pk in zip((delta, new_m, new_v), outs):
        for n, a in zip(small, _unpack_rows(pk, shapes)):
            dst[n] = a.reshape(W[n].shape)
    for n in small:
        grads[n] = grads[n].reshape(W[n].shape)

    return (loss, grad_x, *[grads[n] for n in WEIGHT_ORDER], *[delta[n] for n in WEIGHT_ORDER],
            *[new_m[n] for n in WEIGHT_ORDER], *[new_v[n] for n in WEIGHT_ORDER])
```
